```python
import math
import jax, jax.numpy as jnp
from jax import lax
import numpy as np

D_MODEL = 1024
BATCH = 8
SEQ = 2048
DEPTH = 4

MEM_LEN = 256
RNN_WIDTH = D_MODEL
RNN_BLOCKS = 4
RNN_BLOCK = RNN_WIDTH // RNN_BLOCKS
CONV_WIDTH = 4
LRU_C = 8.0
HEAD_DIM = 64
N_Q_HEADS = D_MODEL // HEAD_DIM
N_KV_HEADS = 2
GROUP = N_Q_HEADS // N_KV_HEADS
ATTN_WIDTH = N_Q_HEADS * HEAD_DIM
KV_WIDTH = N_KV_HEADS * HEAD_DIM
WINDOW = 128
BLOCK = 128
ROPE_THETA = 500000.0
ROT_DIM = HEAD_DIM // 4
IN_COLS = 2 * RNN_WIDTH + ATTN_WIDTH + 2 * KV_WIDTH + 2 * D_MODEL
CROSS_HEADS = 4
CROSS_HEAD_DIM = D_MODEL // CROSS_HEADS
CROSS_WIDTH = CROSS_HEADS * CROSS_HEAD_DIM
D_FF = -(-8 * D_MODEL // (3 * 256)) * 256
LN_EPS = 1e-5
DEEPNORM_ALPHA = (2 * DEPTH) ** 0.25
DEEPNORM_BETA = (8 * DEPTH) ** -0.25
NEG_INF = -1e30

kernel_name = "hawk_swa_sink_hybrid_deepnorm_trunk"


def layer_norm(x, g, b):
    xf = x.astype(jnp.float32)
    mu = jnp.mean(xf, axis=-1, keepdims=True)
    var = jnp.mean(jnp.square(xf - mu), axis=-1, keepdims=True)
    y = (xf - mu) * lax.rsqrt(var + LN_EPS)
    return (y * g.astype(jnp.float32) + b.astype(jnp.float32)).astype(x.dtype)


def rope_tables(seq_len):
    pos = jnp.arange(seq_len, dtype=jnp.float32)
    inv_freq = ROPE_THETA ** (-jnp.arange(0, ROT_DIM, 2, dtype=jnp.float32) / ROT_DIM)
    ang = pos[:, None] * inv_freq[None, :]
    return jnp.cos(ang), jnp.sin(ang)


def apply_partial_rope(t, cos, sin):
    half = ROT_DIM // 2
    c = cos[None, :, None, :].astype(t.dtype)
    s = sin[None, :, None, :].astype(t.dtype)
    t1, t2, rest = t[..., :half], t[..., half:ROT_DIM], t[..., ROT_DIM:]
    return jnp.concatenate([t1 * c - t2 * s, t2 * c + t1 * s, rest], axis=-1)


def rglru_branch(xr, gr, conv_w, conv_b, w_rg, b_rg, w_ig, b_ig, lru_lambda):
    B, S, _ = xr.shape
    xp = jnp.pad(xr, ((0, 0), (CONV_WIDTH - 1, 0), (0, 0)))
    xc = conv_b
    for k in range(CONV_WIDTH):
        xc = xc + xp[:, k:k + S] * conv_w[k]
    xb = xc.reshape(B, S, RNN_BLOCKS, RNN_BLOCK)
    r = jax.nn.sigmoid(jnp.einsum('bsnc,ncd->bsnd', xb, w_rg).reshape(B, S, RNN_WIDTH) + b_rg)
    i = jax.nn.sigmoid(jnp.einsum('bsnc,ncd->bsnd', xb, w_ig).reshape(B, S, RNN_WIDTH) + b_ig)
    log_a = -LRU_C * r.astype(jnp.float32) * jax.nn.softplus(-lru_lambda.astype(jnp.float32))
    a = jnp.exp(log_a)
    mult = jnp.sqrt(-jnp.expm1(2.0 * log_a))
    b_in = mult * (i * xc).astype(jnp.float32)

    def combine(lhs, rhs):
        a1, b1 = lhs
        a2, b2 = rhs
        return a1 * a2, a2 * b1 + b2

    _, h = lax.associative_scan(combine, (a, b_in), axis=1)
    return h.astype(xr.dtype) * jax.nn.gelu(gr)


def swa_sink_branch(q, k, v, sinks, cos, sin):
    B, S, _ = q.shape
    NB = S // BLOCK
    q = apply_partial_rope(q.reshape(B, S, N_Q_HEADS, HEAD_DIM), cos, sin)
    k = apply_partial_rope(k.reshape(B, S, N_KV_HEADS, HEAD_DIM), cos, sin)
    v = v.reshape(B, S, N_KV_HEADS, HEAD_DIM)
    qb = q.reshape(B, NB, BLOCK, N_KV_HEADS, GROUP, HEAD_DIM)

    def band(t):
        tp = jnp.pad(t, ((0, 0), (BLOCK, 0), (0, 0), (0, 0))).reshape(B, NB + 1, BLOCK, N_KV_HEADS, HEAD_DIM)
        return jnp.concatenate([tp[:, :-1], tp[:, 1:]], axis=2)

    kb, vb = band(k), band(v)
    scores = jnp.einsum('bnqhgd,bnjhd->bnhgqj', qb, kb).astype(jnp.float32) * (HEAD_DIM ** -0.5)
    blk = jnp.arange(NB)[:, None, None]
    qpos = blk * BLOCK + jnp.arange(BLOCK)[None, :, None]
    kpos = (blk - 1) * BLOCK + jnp.arange(2 * BLOCK)[None, None, :]
    valid = (kpos <= qpos) & (kpos > qpos - WINDOW) & (kpos >= 0)
    scores = jnp.where(valid[None, :, None, None], scores, NEG_INF)
    sink = sinks.astype(jnp.float32).reshape(N_KV_HEADS, GROUP)[None, None, :, :, None, None]
    sink = jnp.broadcast_to(sink, scores.shape[:-1] + (1,))
    probs = jax.nn.softmax(jnp.concatenate([scores, sink], axis=-1), axis=-1)[..., :-1]
    out = jnp.einsum('bnhgqj,bnjhd->bnqhgd', probs.astype(vb.dtype), vb)
    return out.reshape(B, S, ATTN_WIDTH)


def hybrid_mixer(u, w_in, conv_w, conv_b, w_rg, b_rg, w_ig, b_ig, lru_lambda,
                 w_br_rnn, w_br_attn, sinks, w_out, cos, sin):
    widths = (RNN_WIDTH, RNN_WIDTH, ATTN_WIDTH, KV_WIDTH, KV_WIDTH, D_MODEL, D_MODEL)
    points = np.cumsum(widths)[:-1].tolist()
    proj = u @ w_in
    xr, gr, q, k, v, g_rnn, g_attn = jnp.split(proj, points, axis=-1)
    y_rnn = rglru_branch(xr, gr, conv_w, conv_b, w_rg, b_rg, w_ig, b_ig, lru_lambda)
    y_attn = swa_sink_branch(q, k, v, sinks, cos, sin)
    merged = jax.nn.sigmoid(g_rnn) * (y_rnn @ w_br_rnn) + jax.nn.sigmoid(g_attn) * (y_attn @ w_br_attn)
    return merged @ w_out


def cross_attention(u, mem, cq_w, ckv_w, co_w):
    B, S, _ = u.shape
    M = mem.shape[1]
    q = (u @ cq_w).reshape(B, S, CROSS_HEADS, CROSS_HEAD_DIM)
    k, v = jnp.split(mem @ ckv_w, 2, axis=-1)
    k = k.reshape(B, M, CROSS_HEADS, CROSS_HEAD_DIM)
    v = v.reshape(B, M, CROSS_HEADS, CROSS_HEAD_DIM)
    s = jnp.einsum('bshd,bmhd->bhsm', q, k).astype(jnp.float32) * (CROSS_HEAD_DIM ** -0.5)
    p = jax.nn.softmax(s, axis=-1)
    o = jnp.einsum('bhsm,bmhd->bshd', p.astype(v.dtype), v).reshape(B, S, CROSS_WIDTH)
    return o @ co_w


def swiglu(u, wi, wo):
    gate, up = jnp.split(u @ wi, 2, axis=-1)
    return (jax.nn.silu(gate) * up) @ wo


def _fwd_setup_inputs(seed: int = 0) -> dict:
    key = jax.random.key(seed)
    ks = jax.random.split(key, 26)
    L = DEPTH
    f32 = jnp.float32

    def nrm(k, shape, scale):
        return jax.random.normal(k, shape, f32) * scale

    u = jax.random.uniform(ks[9], (L, RNN_WIDTH), f32, 0.9, 0.999)
    p = u ** (1.0 / LRU_C)
    lru_lambda = jnp.log(p) - jnp.log1p(-p)
    return {
        "x": nrm(ks[0], (BATCH, SEQ, D_MODEL), 1.0),
        "mem": nrm(ks[1], (BATCH, MEM_LEN, D_MODEL), 1.0),
        "w_in": nrm(ks[2], (L, D_MODEL, IN_COLS), D_MODEL ** -0.5),
        "conv_w": nrm(ks[3], (L, CONV_WIDTH, RNN_WIDTH), CONV_WIDTH ** -0.5),
        "conv_b": nrm(ks[4], (L, RNN_WIDTH), 0.01),
        "w_rg": nrm(ks[5], (L, RNN_BLOCKS, RNN_BLOCK, RNN_BLOCK), RNN_BLOCK ** -0.5),
        "b_rg": nrm(ks[6], (L, RNN_WIDTH), 0.01),
        "w_ig": nrm(ks[7], (L, RNN_BLOCKS, RNN_BLOCK, RNN_BLOCK), RNN_BLOCK ** -0.5),
        "b_ig": nrm(ks[8], (L, RNN_WIDTH), 0.01),
        "lru_lambda": lru_lambda,
        "w_br_rnn": nrm(ks[10], (L, RNN_WIDTH, D_MODEL), RNN_WIDTH ** -0.5),
        "w_br_attn": nrm(ks[11], (L, ATTN_WIDTH, D_MODEL), ATTN_WIDTH ** -0.5),
        "sinks": nrm(ks[12], (L, N_Q_HEADS), 0.5),
        "w_out": nrm(ks[13], (L, D_MODEL, D_MODEL), DEEPNORM_BETA * D_MODEL ** -0.5),
        "ln1_g": 1.0 + nrm(ks[14], (L, D_MODEL), 0.02),
        "ln1_b": nrm(ks[15], (L, D_MODEL), 0.02),
        "cq_w": nrm(ks[16], (L, D_MODEL, CROSS_WIDTH), D_MODEL ** -0.5),
        "ckv_w": nrm(ks[17], (L, D_MODEL, 2 * CROSS_WIDTH), D_MODEL ** -0.5),
        "co_w": nrm(ks[18], (L, CROSS_WIDTH, D_MODEL), DEEPNORM_BETA * CROSS_WIDTH ** -0.5),
        "ln2_g": 1.0 + nrm(ks[19], (L, D_MODEL), 0.02),
        "ln2_b": nrm(ks[20], (L, D_MODEL), 0.02),
        "ffn_wi": nrm(ks[21], (L, D_MODEL, 2 * D_FF), D_MODEL ** -0.5),
        "ffn_wo": nrm(ks[22], (L, D_FF, D_MODEL), DEEPNORM_BETA * D_FF ** -0.5),
        "ln3_g": 1.0 + nrm(ks[23], (L, D_MODEL), 0.02),
        "ln3_b": nrm(ks[24], (L, D_MODEL), 0.02),
    }


def _fwd_reference(x, mem, w_in, conv_w, conv_b, w_rg, b_rg, w_ig, b_ig, lru_lambda,
              w_br_rnn, w_br_attn, sinks, w_out, ln1_g, ln1_b,
              cq_w, ckv_w, co_w, ln2_g, ln2_b,
              ffn_wi, ffn_wo, ln3_g, ln3_b):
    cos, sin = rope_tables(x.shape[1])
    h = x
    for l in range(DEPTH):
        mix = hybrid_mixer(h, w_in[l], conv_w[l], conv_b[l], w_rg[l], b_rg[l], w_ig[l], b_ig[l],
                           lru_lambda[l], w_br_rnn[l], w_br_attn[l], sinks[l], w_out[l], cos, sin)
        h = layer_norm(DEEPNORM_ALPHA * h + mix, ln1_g[l], ln1_b[l])
        h = layer_norm(DEEPNORM_ALPHA * h + cross_attention(h, mem, cq_w[l], ckv_w[l], co_w[l]),
                       ln2_g[l], ln2_b[l])
        h = layer_norm(DEEPNORM_ALPHA * h + swiglu(h, ffn_wi[l], ffn_wo[l]), ln3_g[l], ln3_b[l])
    return h


import jax as _jax
import jax.numpy as _jnp

TWIN_FORMAT = 'train_step'
FWD_PARAMS = ['x', 'mem', 'w_in', 'conv_w', 'conv_b', 'w_rg', 'b_rg', 'w_ig', 'b_ig', 'lru_lambda', 'w_br_rnn', 'w_br_attn', 'sinks', 'w_out', 'ln1_g', 'ln1_b', 'cq_w', 'ckv_w', 'co_w', 'ln2_g', 'ln2_b', 'ffn_wi', 'ffn_wo', 'ln3_g', 'ln3_b']
TWIN_WEIGHTS = ['w_in', 'conv_w', 'conv_b', 'w_rg', 'b_rg', 'w_ig', 'b_ig', 'lru_lambda', 'w_br_rnn', 'w_br_attn', 'sinks', 'w_out', 'ln1_g', 'ln1_b', 'cq_w', 'ckv_w', 'co_w', 'ln2_g', 'ln2_b', 'ffn_wi', 'ffn_wo', 'ln3_g', 'ln3_b']
TWIN_DIFF_INPUT = 'x'
TWIN_INPUTS = ['x', 'mem', 'w_in', 'conv_w', 'conv_b', 'w_rg', 'b_rg', 'w_ig', 'b_ig', 'lru_lambda', 'w_br_rnn', 'w_br_attn', 'sinks', 'w_out', 'ln1_g', 'ln1_b', 'cq_w', 'ckv_w', 'co_w', 'ln2_g', 'ln2_b', 'ffn_wi', 'ffn_wo', 'ln3_g', 'ln3_b', 'loss_target', 'm_w_in', 'm_conv_w', 'm_conv_b', 'm_w_rg', 'm_b_rg', 'm_w_ig', 'm_b_ig', 'm_lru_lambda', 'm_w_br_rnn', 'm_w_br_attn', 'm_sinks', 'm_w_out', 'm_ln1_g', 'm_ln1_b', 'm_cq_w', 'm_ckv_w', 'm_co_w', 'm_ln2_g', 'm_ln2_b', 'm_ffn_wi', 'm_ffn_wo', 'm_ln3_g', 'm_ln3_b', 'v_w_in', 'v_conv_w', 'v_conv_b', 'v_w_rg', 'v_b_rg', 'v_w_ig', 'v_b_ig', 'v_lru_lambda', 'v_w_br_rnn', 'v_w_br_attn', 'v_sinks', 'v_w_out', 'v_ln1_g', 'v_ln1_b', 'v_cq_w', 'v_ckv_w', 'v_co_w', 'v_ln2_g', 'v_ln2_b', 'v_ffn_wi', 'v_ffn_wo', 'v_ln3_g', 'v_ln3_b']
TWIN_OUTPUTS = ['loss', 'grad_x', 'grad_w_in', 'grad_conv_w', 'grad_conv_b', 'grad_w_rg', 'grad_b_rg', 'grad_w_ig', 'grad_b_ig', 'grad_lru_lambda', 'grad_w_br_rnn', 'grad_w_br_attn', 'grad_sinks', 'grad_w_out', 'grad_ln1_g', 'grad_ln1_b', 'grad_cq_w', 'grad_ckv_w', 'grad_co_w', 'grad_ln2_g', 'grad_ln2_b', 'grad_ffn_wi', 'grad_ffn_wo', 'grad_ln3_g', 'grad_ln3_b', 'delta_w_in', 'delta_conv_w', 'delta_conv_b', 'delta_w_rg', 'delta_b_rg', 'delta_w_ig', 'delta_b_ig', 'delta_lru_lambda', 'delta_w_br_rnn', 'delta_w_br_attn', 'delta_sinks', 'delta_w_out', 'delta_ln1_g', 'delta_ln1_b', 'delta_cq_w', 'delta_ckv_w', 'delta_co_w', 'delta_ln2_g', 'delta_ln2_b', 'delta_ffn_wi', 'delta_ffn_wo', 'delta_ln3_g', 'delta_ln3_b', 'new_m_w_in', 'new_m_conv_w', 'new_m_conv_b', 'new_m_w_rg', 'new_m_b_rg', 'new_m_w_ig', 'new_m_b_ig', 'new_m_lru_lambda', 'new_m_w_br_rnn', 'new_m_w_br_attn', 'new_m_sinks', 'new_m_w_out', 'new_m_ln1_g', 'new_m_ln1_b', 'new_m_cq_w', 'new_m_ckv_w', 'new_m_co_w', 'new_m_ln2_g', 'new_m_ln2_b', 'new_m_ffn_wi', 'new_m_ffn_wo', 'new_m_ln3_g', 'new_m_ln3_b', 'new_v_w_in', 'new_v_conv_w', 'new_v_conv_b', 'new_v_w_rg', 'new_v_b_rg', 'new_v_w_ig', 'new_v_b_ig', 'new_v_lru_lambda', 'new_v_w_br_rnn', 'new_v_w_br_attn', 'new_v_sinks', 'new_v_w_out', 'new_v_ln1_g', 'new_v_ln1_b', 'new_v_cq_w', 'new_v_ckv_w', 'new_v_co_w', 'new_v_ln2_g', 'new_v_ln2_b', 'new_v_ffn_wi', 'new_v_ffn_wo', 'new_v_ln3_g', 'new_v_ln3_b']
TWIN_LEAF_KINDS = {'loss': 'loss', 'grad_x': 'grad_x', 'grad_w_in': 'grad_w', 'grad_conv_w': 'grad_w', 'grad_conv_b': 'grad_w', 'grad_w_rg': 'grad_w', 'grad_b_rg': 'grad_w', 'grad_w_ig': 'grad_w', 'grad_b_ig': 'grad_w', 'grad_lru_lambda': 'grad_w', 'grad_w_br_rnn': 'grad_w', 'grad_w_br_attn': 'grad_w', 'grad_sinks': 'grad_w', 'grad_w_out': 'grad_w', 'grad_ln1_g': 'grad_w', 'grad_ln1_b': 'grad_w', 'grad_cq_w': 'grad_w', 'grad_ckv_w': 'grad_w', 'grad_co_w': 'grad_w', 'grad_ln2_g': 'grad_w', 'grad_ln2_b': 'grad_w', 'grad_ffn_wi': 'grad_w', 'grad_ffn_wo': 'grad_w', 'grad_ln3_g': 'grad_w', 'grad_ln3_b': 'grad_w', 'delta_w_in': 'delta_w', 'delta_conv_w': 'delta_w', 'delta_conv_b': 'delta_w', 'delta_w_rg': 'delta_w', 'delta_b_rg': 'delta_w', 'delta_w_ig': 'delta_w', 'delta_b_ig': 'delta_w', 'delta_lru_lambda': 'delta_w', 'delta_w_br_rnn': 'delta_w', 'delta_w_br_attn': 'delta_w', 'delta_sinks': 'delta_w', 'delta_w_out': 'delta_w', 'delta_ln1_g': 'delta_w', 'delta_ln1_b': 'delta_w', 'delta_cq_w': 'delta_w', 'delta_ckv_w': 'delta_w', 'delta_co_w': 'delta_w', 'delta_ln2_g': 'delta_w', 'delta_ln2_b': 'delta_w', 'delta_ffn_wi': 'delta_w', 'delta_ffn_wo': 'delta_w', 'delta_ln3_g': 'delta_w', 'delta_ln3_b': 'delta_w', 'new_m_w_in': 'new_m', 'new_m_conv_w': 'new_m', 'new_m_conv_b': 'new_m', 'new_m_w_rg': 'new_m', 'new_m_b_rg': 'new_m', 'new_m_w_ig': 'new_m', 'new_m_b_ig': 'new_m', 'new_m_lru_lambda': 'new_m', 'new_m_w_br_rnn': 'new_m', 'new_m_w_br_attn': 'new_m', 'new_m_sinks': 'new_m', 'new_m_w_out': 'new_m', 'new_m_ln1_g': 'new_m', 'new_m_ln1_b': 'new_m', 'new_m_cq_w': 'new_m', 'new_m_ckv_w': 'new_m', 'new_m_co_w': 'new_m', 'new_m_ln2_g': 'new_m', 'new_m_ln2_b': 'new_m', 'new_m_ffn_wi': 'new_m', 'new_m_ffn_wo': 'new_m', 'new_m_ln3_g': 'new_m', 'new_m_ln3_b': 'new_m', 'new_v_w_in': 'new_v', 'new_v_conv_w': 'new_v', 'new_v_conv_b': 'new_v', 'new_v_w_rg': 'new_v', 'new_v_b_rg': 'new_v', 'new_v_w_ig': 'new_v', 'new_v_b_ig': 'new_v', 'new_v_lru_lambda': 'new_v', 'new_v_w_br_rnn': 'new_v', 'new_v_w_br_attn': 'new_v', 'new_v_sinks': 'new_v', 'new_v_w_out': 'new_v', 'new_v_ln1_g': 'new_v', 'new_v_ln1_b': 'new_v', 'new_v_cq_w': 'new_v', 'new_v_ckv_w': 'new_v', 'new_v_co_w': 'new_v', 'new_v_ln2_g': 'new_v', 'new_v_ln2_b': 'new_v', 'new_v_ffn_wi': 'new_v', 'new_v_ffn_wo': 'new_v', 'new_v_ln3_g': 'new_v', 'new_v_ln3_b': 'new_v'}


def _forward(args):
    return _fwd_reference(*[args[k] for k in FWD_PARAMS])


def _output_shape():
    out = _jax.eval_shape(lambda: _forward(_fwd_setup_inputs(0)))
    return out.shape, out.dtype

N_MICROBATCH = 1
ADAM_LR = 0.001
ADAM_B1 = 0.9
ADAM_B2 = 0.999
ADAM_EPS = 1e-08
ADAM_WD = 0.01
ADAM_STEP = 10
PER_EXAMPLE_BATCH_AXIS = {'x': 0, 'mem': 0, 'loss_target': 0}
SHARED_INPUTS = []
_WEIGHT_DTYPES = {'w_in': _jnp.float32, 'conv_w': _jnp.float32, 'conv_b': _jnp.float32, 'w_rg': _jnp.float32, 'b_rg': _jnp.float32, 'w_ig': _jnp.float32, 'b_ig': _jnp.float32, 'lru_lambda': _jnp.float32, 'w_br_rnn': _jnp.float32, 'w_br_attn': _jnp.float32, 'sinks': _jnp.float32, 'w_out': _jnp.float32, 'ln1_g': _jnp.float32, 'ln1_b': _jnp.float32, 'cq_w': _jnp.float32, 'ckv_w': _jnp.float32, 'co_w': _jnp.float32, 'ln2_g': _jnp.float32, 'ln2_b': _jnp.float32, 'ffn_wi': _jnp.float32, 'ffn_wo': _jnp.float32, 'ln3_g': _jnp.float32, 'ln3_b': _jnp.float32}
MOMENT_SCALE = {'w_in': 5.461891e-03, 'conv_w': 8.620389e-03, 'conv_b': 9.606812e-02, 'w_rg': 2.158815e-03, 'b_rg': 1.931262e-03, 'w_ig': 3.847663e-03, 'b_ig': 3.422812e-03, 'lru_lambda': 4.120119e-03, 'w_br_rnn': 8.490272e-03, 'w_br_attn': 3.832981e-03, 'sinks': 2.768807e-03, 'w_out': 2.049093e-02, 'ln1_g': 5.664227e-01, 'ln1_b': 2.972499e-01, 'cq_w': 3.382050e-03, 'ckv_w': 3.740446e-03, 'co_w': 9.587400e-03, 'ln2_g': 5.665745e-01, 'ln2_b': 2.975397e-01, 'ffn_wi': 1.156446e-02, 'ffn_wo': 4.487069e-02, 'ln3_g': 8.067704e+00, 'ln3_b': 6.116907e-01}


def _to_microbatches(a, axis):
    t = _jnp.moveaxis(a, axis, 0)
    t = t.reshape((N_MICROBATCH, t.shape[0] // N_MICROBATCH) + t.shape[1:])
    return _jnp.moveaxis(t, 1, axis + 1)


def setup_inputs(seed: int = 0) -> dict:
    inp = _fwd_setup_inputs(seed)
    key = _jax.random.fold_in(_jax.random.key(seed), 7919)
    shape, _ = _output_shape()
    out = dict(inp)
    out["loss_target"] = _jax.random.normal(_jax.random.fold_in(key, 0), shape, _jnp.float32)
    for i, name in enumerate(TWIN_WEIGHTS):
        w = inp[name].astype(_jnp.float32)
        if MOMENT_SCALE is None:
            s = _jnp.sqrt(_jnp.mean(_jnp.square(w)) + 1e-30)
        else:
            s = MOMENT_SCALE[name]
        km, kv = _jax.random.split(_jax.random.fold_in(key, i + 1))
        out[name] = w
        out["m_" + name] = s * _jax.random.normal(km, w.shape, _jnp.float32)
        out["v_" + name] = (s * s) * _jax.random.uniform(kv, w.shape, _jnp.float32, 0.5, 1.5)
    if N_MICROBATCH > 1:
        for name, axis in PER_EXAMPLE_BATCH_AXIS.items():
            out[name] = _to_microbatches(out[name], axis)
    return {'x': out['x'], 'mem': out['mem'], 'w_in': out['w_in'], 'conv_w': out['conv_w'], 'conv_b': out['conv_b'], 'w_rg': out['w_rg'], 'b_rg': out['b_rg'], 'w_ig': out['w_ig'], 'b_ig': out['b_ig'], 'lru_lambda': out['lru_lambda'], 'w_br_rnn': out['w_br_rnn'], 'w_br_attn': out['w_br_attn'], 'sinks': out['sinks'], 'w_out': out['w_out'], 'ln1_g': out['ln1_g'], 'ln1_b': out['ln1_b'], 'cq_w': out['cq_w'], 'ckv_w': out['ckv_w'], 'co_w': out['co_w'], 'ln2_g': out['ln2_g'], 'ln2_b': out['ln2_b'], 'ffn_wi': out['ffn_wi'], 'ffn_wo': out['ffn_wo'], 'ln3_g': out['ln3_g'], 'ln3_b': out['ln3_b'], 'loss_target': out['loss_target'], 'm_w_in': out['m_w_in'], 'm_conv_w': out['m_conv_w'], 'm_conv_b': out['m_conv_b'], 'm_w_rg': out['m_w_rg'], 'm_b_rg': out['m_b_rg'], 'm_w_ig': out['m_w_ig'], 'm_b_ig': out['m_b_ig'], 'm_lru_lambda': out['m_lru_lambda'], 'm_w_br_rnn': out['m_w_br_rnn'], 'm_w_br_attn': out['m_w_br_attn'], 'm_sinks': out['m_sinks'], 'm_w_out': out['m_w_out'], 'm_ln1_g': out['m_ln1_g'], 'm_ln1_b': out['m_ln1_b'], 'm_cq_w': out['m_cq_w'], 'm_ckv_w': out['m_ckv_w'], 'm_co_w': out['m_co_w'], 'm_ln2_g': out['m_ln2_g'], 'm_ln2_b': out['m_ln2_b'], 'm_ffn_wi': out['m_ffn_wi'], 'm_ffn_wo': out['m_ffn_wo'], 'm_ln3_g': out['m_ln3_g'], 'm_ln3_b': out['m_ln3_b'], 'v_w_in': out['v_w_in'], 'v_conv_w': out['v_conv_w'], 'v_conv_b': out['v_conv_b'], 'v_w_rg': out['v_w_rg'], 'v_b_rg': out['v_b_rg'], 'v_w_ig': out['v_w_ig'], 'v_b_ig': out['v_b_ig'], 'v_lru_lambda': out['v_lru_lambda'], 'v_w_br_rnn': out['v_w_br_rnn'], 'v_w_br_attn': out['v_w_br_attn'], 'v_sinks': out['v_sinks'], 'v_w_out': out['v_w_out'], 'v_ln1_g': out['v_ln1_g'], 'v_ln1_b': out['v_ln1_b'], 'v_cq_w': out['v_cq_w'], 'v_ckv_w': out['v_ckv_w'], 'v_co_w': out['v_co_w'], 'v_ln2_g': out['v_ln2_g'], 'v_ln2_b': out['v_ln2_b'], 'v_ffn_wi': out['v_ffn_wi'], 'v_ffn_wo': out['v_ffn_wo'], 'v_ln3_g': out['v_ln3_g'], 'v_ln3_b': out['v_ln3_b']}


def _loss(weights, diff, rest, loss_target):
    with _jax.named_scope("forward"):
        args = {**rest, TWIN_DIFF_INPUT: diff, **{k: w.astype(_WEIGHT_DTYPES[k]) for k, w in weights.items()}}
        y = _forward(args)
    with _jax.named_scope("loss_head"):
        err = _jnp.square(y.astype(_jnp.float32) - loss_target)
        return 0.5 * _jnp.sum(_jnp.mean(err, axis=-1)) if err.ndim else 0.5 * err


def _adamw(w, g, m, v):
    m = ADAM_B1 * m + (1.0 - ADAM_B1) * g
    v = ADAM_B2 * v + (1.0 - ADAM_B2) * _jnp.square(g)
    m_hat = m / (1.0 - ADAM_B1 ** ADAM_STEP)
    v_hat = v / (1.0 - ADAM_B2 ** ADAM_STEP)
    delta = -ADAM_LR * (m_hat / (_jnp.sqrt(v_hat) + ADAM_EPS) + ADAM_WD * w)
    return delta, m, v


def reference(x, mem, w_in, conv_w, conv_b, w_rg, b_rg, w_ig, b_ig, lru_lambda, w_br_rnn, w_br_attn, sinks, w_out, ln1_g, ln1_b, cq_w, ckv_w, co_w, ln2_g, ln2_b, ffn_wi, ffn_wo, ln3_g, ln3_b, loss_target, m_w_in, m_conv_w, m_conv_b, m_w_rg, m_b_rg, m_w_ig, m_b_ig, m_lru_lambda, m_w_br_rnn, m_w_br_attn, m_sinks, m_w_out, m_ln1_g, m_ln1_b, m_cq_w, m_ckv_w, m_co_w, m_ln2_g, m_ln2_b, m_ffn_wi, m_ffn_wo, m_ln3_g, m_ln3_b, v_w_in, v_conv_w, v_conv_b, v_w_rg, v_b_rg, v_w_ig, v_b_ig, v_lru_lambda, v_w_br_rnn, v_w_br_attn, v_sinks, v_w_out, v_ln1_g, v_ln1_b, v_cq_w, v_ckv_w, v_co_w, v_ln2_g, v_ln2_b, v_ffn_wi, v_ffn_wo, v_ln3_g, v_ln3_b):
    given = dict(x=x, mem=mem, w_in=w_in, conv_w=conv_w, conv_b=conv_b, w_rg=w_rg, b_rg=b_rg, w_ig=w_ig, b_ig=b_ig, lru_lambda=lru_lambda, w_br_rnn=w_br_rnn, w_br_attn=w_br_attn, sinks=sinks, w_out=w_out, ln1_g=ln1_g, ln1_b=ln1_b, cq_w=cq_w, ckv_w=ckv_w, co_w=co_w, ln2_g=ln2_g, ln2_b=ln2_b, ffn_wi=ffn_wi, ffn_wo=ffn_wo, ln3_g=ln3_g, ln3_b=ln3_b, loss_target=loss_target, m_w_in=m_w_in, m_conv_w=m_conv_w, m_conv_b=m_conv_b, m_w_rg=m_w_rg, m_b_rg=m_b_rg, m_w_ig=m_w_ig, m_b_ig=m_b_ig, m_lru_lambda=m_lru_lambda, m_w_br_rnn=m_w_br_rnn, m_w_br_attn=m_w_br_attn, m_sinks=m_sinks, m_w_out=m_w_out, m_ln1_g=m_ln1_g, m_ln1_b=m_ln1_b, m_cq_w=m_cq_w, m_ckv_w=m_ckv_w, m_co_w=m_co_w, m_ln2_g=m_ln2_g, m_ln2_b=m_ln2_b, m_ffn_wi=m_ffn_wi, m_ffn_wo=m_ffn_wo, m_ln3_g=m_ln3_g, m_ln3_b=m_ln3_b, v_w_in=v_w_in, v_conv_w=v_conv_w, v_conv_b=v_conv_b, v_w_rg=v_w_rg, v_b_rg=v_b_rg, v_w_ig=v_w_ig, v_b_ig=v_b_ig, v_lru_lambda=v_lru_lambda, v_w_br_rnn=v_w_br_rnn, v_w_br_attn=v_w_br_attn, v_sinks=v_sinks, v_w_out=v_w_out, v_ln1_g=v_ln1_g, v_ln1_b=v_ln1_b, v_cq_w=v_cq_w, v_ckv_w=v_ckv_w, v_co_w=v_co_w, v_ln2_g=v_ln2_g, v_ln2_b=v_ln2_b, v_ffn_wi=v_ffn_wi, v_ffn_wo=v_ffn_wo, v_ln3_g=v_ln3_g, v_ln3_b=v_ln3_b)
    weights = {n: given[n] for n in TWIN_WEIGHTS}
    shared = {n: given[n] for n in SHARED_INPUTS}
    per_example = {n: given[n] for n in ['x', 'mem']}
    grad_fn = _jax.value_and_grad(_loss, argnums=(0, 1))

    def one_microbatch(ex, loss_target):
        ex = dict(ex)
        diff = ex.pop(TWIN_DIFF_INPUT)
        return grad_fn(weights, diff, {**shared, **ex}, loss_target)

    if N_MICROBATCH == 1:
        loss, (grad_w, grad_x) = one_microbatch(per_example, given["loss_target"])
    else:
        def body(carry, xs):
            loss_sum, grad_sum = carry
            l_k, (gw_k, gx_k) = one_microbatch(xs[0], xs[1])
            with _jax.named_scope("update"):
                return (loss_sum + l_k, _jax.tree.map(_jnp.add, grad_sum, gw_k)), gx_k

        init = (_jnp.zeros((), _jnp.float32), _jax.tree.map(_jnp.zeros_like, weights))
        (loss, grad_w), grad_x = _jax.lax.scan(body, init, (per_example, given["loss_target"]))
    with _jax.named_scope("update"):
        delta_w, new_m, new_v = {}, {}, {}
        for n in TWIN_WEIGHTS:
            delta_w[n], new_m[n], new_v[n] = _adamw(weights[n], grad_w[n], given["m_" + n], given["v_" + n])
    return (loss, grad_x, *[grad_w[n] for n in TWIN_WEIGHTS], *[delta_w[n] for n in TWIN_WEIGHTS],
            *[new_m[n] for n in TWIN_WEIGHTS], *[new_v[n] for n in TWIN_WEIGHTS])
```

```python
import functools
import math

import jax
import jax.numpy as jnp
import numpy as np
from jax import lax
from jax.experimental import pallas as pl
from jax.experimental.pallas import tpu as pltpu

F32 = jnp.float32
BF16 = jnp.bfloat16
MXU_DTYPE = BF16

HEAD_DIM = 64
N_KV_HEADS = 2
WINDOW = 128
ROT_DIM = HEAD_DIM // 4
ROPE_THETA = 500000.0
CROSS_HEADS = 4
RNN_BLOCKS = 4
CONV_WIDTH = 4
LRU_C = 8.0
LN_EPS = 1e-5
NEG_INF = -1e30
ADAM_LR = 0.001
ADAM_B1 = 0.9
ADAM_B2 = 0.999
ADAM_EPS = 1e-08
ADAM_WD = 0.01
ADAM_STEP = 10

VMEM_BYTES_V7X = 64 * 1024 * 1024
VMEM_BLOCK_BUDGET = 36 * 1024 * 1024
LANES = 128
SUBLANES = 8

MESH_ID = pl.DeviceIdType.MESH
N_CHIPS = 4
N_DEV = 8

BIG = ("w_in", "w_rg", "w_ig", "w_br_rnn", "w_br_attn", "w_out", "cq_w", "ckv_w", "co_w", "ffn_wi", "ffn_wo")
SHARD_AXIS = {"w_in": 1, "w_rg": 1, "w_ig": 1, "w_br_rnn": 0, "w_br_attn": 0, "w_out": 0, "cq_w": 0, "ckv_w": 1,
              "co_w": 0, "ffn_wi": 1, "ffn_wo": 0}
SMALL = ("conv_w", "conv_b", "b_rg", "b_ig", "lru_lambda", "sinks", "ln1_g", "ln1_b", "ln2_g", "ln2_b", "ln3_g", "ln3_b")
WEIGHTS = ("w_in", "conv_w", "conv_b", "w_rg", "b_rg", "w_ig", "b_ig", "lru_lambda", "w_br_rnn", "w_br_attn", "sinks",
           "w_out", "ln1_g", "ln1_b", "cq_w", "ckv_w", "co_w", "ln2_g", "ln2_b", "ffn_wi", "ffn_wo", "ln3_g", "ln3_b")
GATE_WEIGHTS = ("w_rg", "w_ig")
COL_BLOCKED = ("ckv_w", "ffn_wi")


def _params(dims=None, vmem=None):
    return pltpu.CompilerParams(dimension_semantics=dims, vmem_limit_bytes=vmem)


def _vmem_limit(block_bytes, temp_bytes=0):
    want = int(2 * block_bytes + temp_bytes) + (6 << 20)
    return max(32 << 20, min(want, VMEM_BYTES_V7X - (6 << 20)))


def _divisors(n, align, cap):
    out = [d for d in range(align, min(n, cap) + 1, align) if n % d == 0]
    if n <= cap and n not in out:
        out.append(n)
    return sorted(out, reverse=True) or [n]


def _sigmoid(x):
    return 1.0 / (1.0 + jnp.exp(-x))


def _gelu_parts(x):
    c = math.sqrt(2.0 / math.pi)
    u = c * (x + 0.044715 * x * x * x)
    t = jnp.tanh(u)
    return t, c * (1.0 + 3 * 0.044715 * x * x)


def _gelu(x):
    t, _ = _gelu_parts(x)
    return 0.5 * x * (1.0 + t)


def _gelu_grad(x):
    t, du = _gelu_parts(x)
    return 0.5 * (1.0 + t) + 0.5 * x * (1.0 - t * t) * du


def _neg_expm1(x):
    series = x * (1.0 + x * (0.5 + x * (1.0 / 6 + x * (1.0 / 24 + x * (1.0 / 120)))))
    return -jnp.where(x > -0.1, series, jnp.exp(x) - 1.0)


def _softplus_neg(lam):
    x = -lam
    return jnp.maximum(x, 0.0) + jnp.log1p(jnp.exp(-jnp.abs(x)))


STEP_US = 0.35
HBM_BYTES_PER_US = 2.5e6


def mm(a, b, mode, name, *, b_index=(), chips=False, out_chips=False, out_dtype=F32):
    nlead = len(b_index) + (1 if chips else 0)
    bk, bn = b.shape[nlead:]
    if mode == "nn":
        (M, K), N, cs = a.shape, (bn * N_CHIPS if chips else bn), bn
    elif mode == "nt":
        (M, K), N, cs = a.shape, bk, bn
    else:
        (K, M), N, cs = a.shape, bn, bn // N_CHIPS
    asz, bsz, osz = a.dtype.itemsize, b.dtype.itemsize, jnp.dtype(out_dtype).itemsize
    n_unit = cs if (chips and mode == "nn") or out_chips else N
    k_unit = cs if (chips and mode == "nt") else K
    tms = _divisors(M, LANES if mode == "tn" else SUBLANES, 2048)
    tns = _divisors(n_unit, LANES, 2048)
    tks = [K] if mode == "tn" else sorted(set(_divisors(k_unit, LANES, k_unit) + ([K] if not chips else [])), reverse=True)
    best = None
    for tm in tms:
        for tn in tns:
            for tk in tks:
                nk = K // tk
                if nk > 1 and osz != 4:
                    continue
                blocks = tm * tk * asz + tn * tk * bsz + tm * tn * osz
                temps = tm * tk * (2 + (4 if mode == "tn" else 0)) + tn * tk * 2 + tm * tn * 4
                if 2 * blocks + temps > VMEM_BLOCK_BUDGET + (8 << 20):
                    continue
                ni, nj = M // tm, N // tn
                traffic = M * K * asz * (nj if nk > 1 else 1) + N * K * bsz * (1 if nj * nk == 1 else ni) + M * N * osz
                cost = ni * nj * nk * STEP_US + traffic / HBM_BYTES_PER_US
                if best is None or cost < best[0]:
                    best = (cost, tm, tn, tk, blocks, temps)
    _, tm, tn, tk, blocks, temps = best
    nk, npc, kpc = K // tk, n_unit // tn, k_unit // tk

    def body(a_ref, b_ref, o_ref):
        av = a_ref[...]
        if mode == "tn":
            av = av.T
        av = av.astype(MXU_DTYPE)
        bv = b_ref[...].astype(MXU_DTYPE)
        dn = (((1,), (1,)), ((), ())) if mode == "nt" else (((1,), (0,)), ((), ()))
        r = lax.dot_general(av, bv, dn, preferred_element_type=F32)
        if nk == 1:
            o_ref[...] = r.astype(o_ref.dtype)
        else:
            @pl.when(pl.program_id(2) == 0)
            def _():
                o_ref[...] = r

            @pl.when(pl.program_id(2) > 0)
            def _():
                o_ref[...] += r

    if mode == "tn":
        a_spec = pl.BlockSpec((tk, tm), lambda i, j, k: (k, i))
    else:
        a_spec = pl.BlockSpec((tm, tk), lambda i, j, k: (i, k))
    lead = (None,) * nlead
    if mode == "nt":
        bmap = (lambda i, j, k: b_index + (k // kpc, j, k % kpc)) if chips else (lambda i, j, k: b_index + (j, k))
        b_spec = pl.BlockSpec(lead + (tn, tk), bmap)
    else:
        bmap = (lambda i, j, k: b_index + (j // npc, k, j % npc)) if chips else (lambda i, j, k: b_index + (k, j))
        b_spec = pl.BlockSpec(lead + (tk, tn), bmap)
    if out_chips:
        o_spec = pl.BlockSpec((None, tm, tn), lambda i, j, k: (j // npc, i, j % npc))
        o_shape = jax.ShapeDtypeStruct((N_CHIPS, M, cs), out_dtype)
    else:
        o_spec = pl.BlockSpec((tm, tn), lambda i, j, k: (i, j))
        o_shape = jax.ShapeDtypeStruct((M, N), out_dtype)
    return pl.pallas_call(
        body, name=name, grid=(M // tm, N // tn, nk), in_specs=[a_spec, b_spec], out_specs=o_spec, out_shape=o_shape,
        compiler_params=_params(("parallel", "parallel", "arbitrary"), _vmem_limit(blocks, temps)),
    )(a, b)


ROW_TILE = 512
GATE_ROWS = 1024


def ln_fwd(h, f, g, b, alpha, name):
    S, D = h.shape
    tr = min(ROW_TILE, S)

    def body(h_ref, f_ref, g_ref, b_ref, y_ref, xh_ref, rs_ref):
        z = alpha * h_ref[...] + f_ref[...]
        mu = jnp.mean(z, axis=-1, keepdims=True)
        zc = z - mu
        var = jnp.mean(zc * zc, axis=-1, keepdims=True)
        rs = lax.rsqrt(var + LN_EPS)
        xh = zc * rs
        y_ref[...] = xh * g_ref[...] + b_ref[...]
        xh_ref[...] = xh
        rs_ref[...] = rs

    row = pl.BlockSpec((tr, D), lambda i: (i, 0))
    vec = pl.BlockSpec((1, D), lambda i: (0, 0))
    return pl.pallas_call(
        body, name=name, grid=(S // tr,), in_specs=[row, row, vec, vec],
        out_specs=[row, row, pl.BlockSpec((tr, 1), lambda i: (i, 0))],
        out_shape=[jax.ShapeDtypeStruct((S, D), F32), jax.ShapeDtypeStruct((S, D), F32), jax.ShapeDtypeStruct((S, 1), F32)],
        compiler_params=_params(("parallel",), 48 << 20),
    )(h, f, g, b)


def ln_bwd(dy_a, dy_b, xh, rs, g, c1, name):
    S, D = xh.shape
    tr = min(ROW_TILE, S)
    two = dy_b is not None

    def body(*refs):
        if two:
            a_ref, b_ref, xh_ref, rs_ref, g_ref, dz_ref, dg_ref, db_ref = refs
            dy = c1 * a_ref[...] + b_ref[...]
        else:
            a_ref, xh_ref, rs_ref, g_ref, dz_ref, dg_ref, db_ref = refs
            dy = a_ref[...]
        x = xh_ref[...]
        dyg = dy * g_ref[...]
        m1 = jnp.mean(dyg, axis=-1, keepdims=True)
        m2 = jnp.mean(dyg * x, axis=-1, keepdims=True)
        dz_ref[...] = rs_ref[...] * (dyg - m1 - x * m2)

        @pl.when(pl.program_id(0) == 0)
        def _():
            dg_ref[...] = jnp.zeros_like(dg_ref)
            db_ref[...] = jnp.zeros_like(db_ref)

        dg_ref[...] += jnp.sum(dy * x, axis=0, keepdims=True)
        db_ref[...] += jnp.sum(dy, axis=0, keepdims=True)

    row = pl.BlockSpec((tr, D), lambda i: (i, 0))
    vec = pl.BlockSpec((1, D), lambda i: (0, 0))
    ins = [row, row] if two else [row]
    args = (dy_a, dy_b) if two else (dy_a,)
    return pl.pallas_call(
        body, name=name, grid=(S // tr,), in_specs=ins + [row, pl.BlockSpec((tr, 1), lambda i: (i, 0)), vec],
        out_specs=[row, vec, vec],
        out_shape=[jax.ShapeDtypeStruct((S, D), F32), jax.ShapeDtypeStruct((1, D), F32), jax.ShapeDtypeStruct((1, D), F32)],
        compiler_params=_params(("arbitrary",), 48 << 20),
    )(*args, xh, rs, g)


def axpby(a, b, c1, name):
    S, D = a.shape
    tr = min(ROW_TILE, S)

    def body(a_ref, b_ref, o_ref):
        o_ref[...] = c1 * a_ref[...] + b_ref[...]

    row = pl.BlockSpec((tr, D), lambda i: (i, 0))
    return pl.pallas_call(body, name=name, grid=(S // tr,), in_specs=[row, row], out_specs=row,
                          out_shape=jax.ShapeDtypeStruct((S, D), F32), compiler_params=_params(("parallel",)))(a, b)


def loss_head(y, t, name):
    S, D = y.shape
    tr = min(ROW_TILE, S)
    nsteps = S // tr

    def body(y_ref, t_ref, dy_ref, l_ref, acc_ref):
        i = pl.program_id(0)

        @pl.when(i == 0)
        def _():
            acc_ref[...] = jnp.zeros_like(acc_ref)

        e = y_ref[...] - t_ref[...]
        dy_ref[...] = e * (1.0 / D)
        acc_ref[...] += jnp.sum(e * e, axis=0, keepdims=True)

        @pl.when(i == nsteps - 1)
        def _():
            l_ref[...] = jnp.sum(acc_ref[...], axis=1, keepdims=True) * (0.5 / D)

    row = pl.BlockSpec((tr, D), lambda i: (i, 0))
    return pl.pallas_call(
        body, name=name, grid=(nsteps,), in_specs=[row, row],
        out_specs=[row, pl.BlockSpec((1, 1), lambda i: (0, 0))],
        out_shape=[jax.ShapeDtypeStruct((S, D), F32), jax.ShapeDtypeStruct((1, 1), F32)],
        scratch_shapes=[pltpu.VMEM((1, D), F32)], compiler_params=_params(("arbitrary",)),
    )(y, t)


def swiglu_fwd(gu, name):
    S, F2 = gu.shape
    Fh = F2 // 2
    tc = _divisors(Fh, LANES, 1536)[0]
    nb = Fh // tc
    tr = min(ROW_TILE, S)

    def body(g_ref, u_ref, o_ref):
        g = g_ref[...]
        o_ref[...] = g * _sigmoid(g) * u_ref[...]

    return pl.pallas_call(
        body, name=name, grid=(S // tr, nb),
        in_specs=[pl.BlockSpec((tr, tc), lambda i, j: (i, j)), pl.BlockSpec((tr, tc), lambda i, j: (i, nb + j))],
        out_specs=pl.BlockSpec((tr, tc), lambda i, j: (i, j)), out_shape=jax.ShapeDtypeStruct((S, Fh), F32),
        compiler_params=_params(("parallel", "parallel")),
    )(gu, gu)


def swiglu_bwd(gu, dact, name):
    S, F2 = gu.shape
    Fh = F2 // 2
    tc = _divisors(Fh, LANES, 1536)[0]
    nb = Fh // tc
    tr = min(ROW_TILE, S)

    def body(g_ref, u_ref, d_ref, o_ref):
        g, u, d = g_ref[...], u_ref[...], d_ref[...]
        s = _sigmoid(g)
        dgate = d * u * (s * (1.0 + g * (1.0 - s)))
        dup = d * (g * s)
        o_ref[...] = jnp.where(pl.program_id(1) < nb, dgate, dup)

    return pl.pallas_call(
        body, name=name, grid=(S // tr, 2 * nb),
        in_specs=[pl.BlockSpec((tr, tc), lambda i, j: (i, j % nb)), pl.BlockSpec((tr, tc), lambda i, j: (i, nb + j % nb)),
                  pl.BlockSpec((tr, tc), lambda i, j: (i, j % nb))],
        out_specs=pl.BlockSpec((tr, tc), lambda i, j: (i, j)), out_shape=jax.ShapeDtypeStruct((S, F2), F32),
        compiler_params=_params(("parallel", "parallel")),
    )(gu, gu, dact)


GATE_COLS = 256


def merge_fwd(proj, pr, pa, name):
    S, D = pr.shape
    tr = min(GATE_ROWS, S)
    c0 = (3 * D + 2 * N_KV_HEADS * HEAD_DIM) // GATE_COLS
    c1 = c0 + D // GATE_COLS

    def body(gr_ref, ga_ref, pr_ref, pa_ref, o_ref):
        o_ref[...] = _sigmoid(gr_ref[...]) * pr_ref[...] + _sigmoid(ga_ref[...]) * pa_ref[...]

    blk = pl.BlockSpec((tr, GATE_COLS), lambda i, j: (i, j))
    return pl.pallas_call(
        body, name=name, grid=(S // tr, D // GATE_COLS),
        in_specs=[pl.BlockSpec((tr, GATE_COLS), lambda i, j: (i, c0 + j)), pl.BlockSpec((tr, GATE_COLS), lambda i, j: (i, c1 + j)),
                  blk, blk],
        out_specs=blk, out_shape=jax.ShapeDtypeStruct((S, D), F32), compiler_params=_params(("parallel", "parallel")),
    )(proj, proj, pr, pa)


def merge_bwd(proj, pr, pa, dm, name):
    S, D = pr.shape
    tr = min(GATE_ROWS, S)
    c0 = (3 * D + 2 * N_KV_HEADS * HEAD_DIM) // GATE_COLS
    c1 = c0 + D // GATE_COLS

    def body(gr_ref, ga_ref, pr_ref, pa_ref, dm_ref, dpr_ref, dpa_ref, dgr_ref, dga_ref):
        sr, sa, d = _sigmoid(gr_ref[...]), _sigmoid(ga_ref[...]), dm_ref[...]
        dpr_ref[...] = d * sr
        dpa_ref[...] = d * sa
        dgr_ref[...] = d * pr_ref[...] * (sr * (1.0 - sr))
        dga_ref[...] = d * pa_ref[...] * (sa * (1.0 - sa))

    blk = pl.BlockSpec((tr, GATE_COLS), lambda i, j: (i, j))
    sds = jax.ShapeDtypeStruct((S, D), F32)
    return pl.pallas_call(
        body, name=name, grid=(S // tr, D // GATE_COLS),
        in_specs=[pl.BlockSpec((tr, GATE_COLS), lambda i, j: (i, c0 + j)), pl.BlockSpec((tr, GATE_COLS), lambda i, j: (i, c1 + j)),
                  blk, blk, blk],
        out_specs=[blk, blk, blk, blk], out_shape=[sds, sds, sds, sds], compiler_params=_params(("parallel", "parallel")),
    )(proj, proj, pr, pa, dm)


RG_ROWS = 512


def _shift_down(cur, prev, d, row, first):
    halo = jnp.where(first, 0.0, pltpu.roll(prev, d, 0))
    return jnp.where(row >= d, pltpu.roll(cur, d, 0), halo)


def _shift_up(cur, nxt, d, row, last, tr):
    halo = jnp.where(last, 0.0, pltpu.roll(nxt, tr - d, 0))
    return jnp.where(row < tr - d, pltpu.roll(cur, tr - d, 0), halo)


def _lru_coeffs(r, lam):
    sp = _softplus_neg(lam)
    la = -LRU_C * r * sp
    return sp, la, jnp.exp(la), _neg_expm1(2.0 * la)


def rg_gates_fwd(proj, conv_w, conv_b, w_rg, b_rg, w_ig, b_ig, lam, name):
    S = proj.shape[0]
    nblk, bw, _ = w_rg.shape
    D = nblk * bw
    tr = min(RG_ROWS, S)

    def body(xr_ref, xp_ref, cw_ref, cb_ref, wr_ref, br_ref, wi_ref, bi_ref, lam_ref, xc_ref, r_ref, i_ref, a_ref, b_ref):
        first = pl.program_id(1) == 0
        cur, prev = xr_ref[...], xp_ref[...]
        row = lax.broadcasted_iota(jnp.int32, cur.shape, 0)
        xc = cb_ref[...]
        for k in range(CONV_WIDTH - 1):
            xc = xc + _shift_down(cur, prev, CONV_WIDTH - 1 - k, row, first) * cw_ref[k:k + 1, :]
        xc = xc + cur * cw_ref[CONV_WIDTH - 1:CONV_WIDTH, :]
        xm = xc.astype(MXU_DTYPE)
        r = _sigmoid(jnp.dot(xm, wr_ref[...].astype(MXU_DTYPE), preferred_element_type=F32) + br_ref[...])
        ig = _sigmoid(jnp.dot(xm, wi_ref[...].astype(MXU_DTYPE), preferred_element_type=F32) + bi_ref[...])
        _, _, a, em = _lru_coeffs(r, lam_ref[...])
        xc_ref[...] = xc
        r_ref[...] = r
        i_ref[...] = ig
        a_ref[...] = a
        b_ref[...] = jnp.sqrt(em) * (ig * xc)

    tile = pl.BlockSpec((tr, bw), lambda n, i: (i, n))
    vec = pl.BlockSpec((1, bw), lambda n, i: (0, n))
    wblk = pl.BlockSpec((None, bw, bw), lambda n, i: (n, 0, 0))
    sds = jax.ShapeDtypeStruct((S, D), F32)
    return pl.pallas_call(
        body, name=name, grid=(nblk, S // tr),
        in_specs=[tile, pl.BlockSpec((tr, bw), lambda n, i: (jnp.maximum(i - 1, 0), n)),
                  pl.BlockSpec((CONV_WIDTH, bw), lambda n, i: (0, n)), vec, wblk, vec, wblk, vec, vec],
        out_specs=[tile] * 5, out_shape=[sds] * 5, compiler_params=_params(("parallel", "parallel")),
    )(proj, proj, conv_w, conv_b, w_rg, b_rg, w_ig, b_ig, lam)


SCAN_COLS = 256
CHUNK = SUBLANES


def rg_scan_fwd(proj, a, b, name):
    S, D = a.shape
    cb = min(SCAN_COLS, D)
    goff = D // cb

    def body(a_ref, b_ref, g_ref, hs_ref, y_ref):
        row = lax.broadcasted_iota(jnp.int32, (CHUNK, cb), 0)

        def step(c, carry):
            r0 = pl.multiple_of(c * CHUNK, CHUNK)
            A = a_ref[pl.ds(r0, CHUNK), :]
            B = b_ref[pl.ds(r0, CHUNK), :]
            for d in (1, 2, 4):
                As = jnp.where(row >= d, pltpu.roll(A, d, 0), 1.0)
                Bs = jnp.where(row >= d, pltpu.roll(B, d, 0), 0.0)
                B = A * Bs + B
                A = A * As
            H = B + A * carry
            hs_ref[pl.ds(r0, CHUNK), :] = H
            return jnp.sum(jnp.where(row == CHUNK - 1, H, 0.0), axis=0, keepdims=True)

        lax.fori_loop(0, S // CHUNK, step, jnp.zeros((1, cb), F32))
        y_ref[...] = hs_ref[...] * _gelu(g_ref[...])

    col = pl.BlockSpec((S, cb), lambda j: (0, j))
    sds = jax.ShapeDtypeStruct((S, D), F32)
    return pl.pallas_call(
        body, name=name, grid=(D // cb,), in_specs=[col, col, pl.BlockSpec((S, cb), lambda j: (0, goff + j))],
        out_specs=[col, col], out_shape=[sds, sds], compiler_params=_params(("parallel",), _vmem_limit(5 * S * cb * 4, 4 * S * cb * 4)),
    )(a, b, proj)


def rg_scan_bwd(proj, dy, hs, a, name):
    S, D = a.shape
    cb = min(SCAN_COLS, D)
    goff = D // cb
    nchunks = S // CHUNK

    def body(g_ref, dy_ref, hs_ref, a_ref, dg_ref, gt_ref):
        gate, dy = g_ref[...], dy_ref[...]
        dg_ref[...] = dy * hs_ref[...] * _gelu_grad(gate)
        gt_ref[...] = dy * _gelu(gate)
        row = lax.broadcasted_iota(jnp.int32, (CHUNK, cb), 0)

        def step(k, carry):
            c = nchunks - 1 - k
            r0 = pl.multiple_of(c * CHUNK, CHUNK)
            rn = pl.multiple_of(jnp.minimum(c + 1, nchunks - 1) * CHUNK, CHUNK)
            last = c == nchunks - 1
            nxt = jnp.where(last, 0.0, pltpu.roll(a_ref[pl.ds(rn, CHUNK), :], CHUNK - 1, 0))
            A = jnp.where(row < CHUNK - 1, pltpu.roll(a_ref[pl.ds(r0, CHUNK), :], CHUNK - 1, 0), nxt)
            B = gt_ref[pl.ds(r0, CHUNK), :]
            for d in (1, 2, 4):
                As = jnp.where(row < CHUNK - d, pltpu.roll(A, CHUNK - d, 0), 1.0)
                Bs = jnp.where(row < CHUNK - d, pltpu.roll(B, CHUNK - d, 0), 0.0)
                B = A * Bs + B
                A = A * As
            G = B + A * carry
            gt_ref[pl.ds(r0, CHUNK), :] = G
            return jnp.sum(jnp.where(row == 0, G, 0.0), axis=0, keepdims=True)

        lax.fori_loop(0, nchunks, step, jnp.zeros((1, cb), F32))

    col = pl.BlockSpec((S, cb), lambda j: (0, j))
    sds = jax.ShapeDtypeStruct((S, D), F32)
    return pl.pallas_call(
        body, name=name, grid=(D // cb,), in_specs=[pl.BlockSpec((S, cb), lambda j: (0, goff + j)), col, col, col],
        out_specs=[col, col], out_shape=[sds, sds], compiler_params=_params(("parallel",), _vmem_limit(6 * S * cb * 4, 6 * S * cb * 4)),
    )(proj, dy, hs, a)


def rg_gates_bwd(gt, hs, xc, r, ig, w_rg, w_ig, lam, name):
    S, D = xc.shape
    nblk, bw, _ = w_rg.shape
    tr = min(RG_ROWS, S)

    def body(gt_ref, hs_ref, hp_ref, xc_ref, r_ref, i_ref, wr_ref, wi_ref, lam_ref,
             dxc_ref, dwr_ref, dwi_ref, dbr_ref, dbi_ref, dl_ref):
        step = pl.program_id(1)
        g, hs, xc, r, ig, lam = gt_ref[...], hs_ref[...], xc_ref[...], r_ref[...], i_ref[...], lam_ref[...]
        row = lax.broadcasted_iota(jnp.int32, g.shape, 0)
        hprev = _shift_down(hs, hp_ref[...], 1, row, step == 0)
        sp, _, a, em = _lru_coeffs(r, lam)
        mult = jnp.sqrt(em)
        du = g * mult
        dla = g * hprev * a - (g * (ig * xc)) * (a * a) / mult
        dpr = (dla * (-LRU_C * sp)) * (r * (1.0 - r))
        dpi = (du * xc) * (ig * (1.0 - ig))
        dprm, dpim = dpr.astype(MXU_DTYPE), dpi.astype(MXU_DTYPE)
        nt = (((1,), (1,)), ((), ()))
        dxc_ref[...] = (du * ig + lax.dot_general(dprm, wr_ref[...].astype(MXU_DTYPE), nt, preferred_element_type=F32)
                        + lax.dot_general(dpim, wi_ref[...].astype(MXU_DTYPE), nt, preferred_element_type=F32))

        @pl.when(step == 0)
        def _():
            for ref in (dwr_ref, dwi_ref, dbr_ref, dbi_ref, dl_ref):
                ref[...] = jnp.zeros_like(ref)

        xct = xc.T.astype(MXU_DTYPE)
        dwr_ref[...] += jnp.dot(xct, dprm, preferred_element_type=F32)
        dwi_ref[...] += jnp.dot(xct, dpim, preferred_element_type=F32)
        dbr_ref[...] += jnp.sum(dpr, axis=0, keepdims=True)
        dbi_ref[...] += jnp.sum(dpi, axis=0, keepdims=True)
        dl_ref[...] += jnp.sum(dla * (-LRU_C * r), axis=0, keepdims=True) * (-_sigmoid(-lam))

    tile = pl.BlockSpec((tr, bw), lambda n, i: (i, n))
    vec = pl.BlockSpec((1, bw), lambda n, i: (0, n))
    wblk = pl.BlockSpec((None, bw, bw), lambda n, i: (n, 0, 0))
    return pl.pallas_call(
        body, name=name, grid=(nblk, S // tr),
        in_specs=[tile, tile, pl.BlockSpec((tr, bw), lambda n, i: (jnp.maximum(i - 1, 0), n)), tile, tile, tile, wblk, wblk, vec],
        out_specs=[tile, wblk, wblk, vec, vec, vec],
        out_shape=[jax.ShapeDtypeStruct((S, D), F32), jax.ShapeDtypeStruct((nblk, bw, bw), F32), jax.ShapeDtypeStruct((nblk, bw, bw), F32),
                   jax.ShapeDtypeStruct((1, D), F32), jax.ShapeDtypeStruct((1, D), F32), jax.ShapeDtypeStruct((1, D), F32)],
        compiler_params=_params(("parallel", "arbitrary")),
    )(gt, hs, hs, xc, r, ig, w_rg, w_ig, lam)


def rg_conv_bwd(proj, dxc, conv_w, name):
    S, D = dxc.shape
    bw = min(SCAN_COLS, D)
    tr = min(RG_ROWS, S)
    nsteps = S // tr

    def body(d_ref, dn_ref, xr_ref, xp_ref, cw_ref, dxr_ref, dcw_ref, dcb_ref):
        step = pl.program_id(1)
        d, xr = d_ref[...], xr_ref[...]
        row = lax.broadcasted_iota(jnp.int32, d.shape, 0)
        dxr = d * cw_ref[CONV_WIDTH - 1:CONV_WIDTH, :]
        for k in range(CONV_WIDTH - 1):
            dxr = dxr + _shift_up(d, dn_ref[...], CONV_WIDTH - 1 - k, row, step == nsteps - 1, tr) * cw_ref[k:k + 1, :]
        dxr_ref[...] = dxr

        @pl.when(step == 0)
        def _():
            dcw_ref[...] = jnp.zeros_like(dcw_ref)
            dcb_ref[...] = jnp.zeros_like(dcb_ref)

        for k in range(CONV_WIDTH - 1):
            xs = _shift_down(xr, xp_ref[...], CONV_WIDTH - 1 - k, row, step == 0)
            dcw_ref[k:k + 1, :] += jnp.sum(d * xs, axis=0, keepdims=True)
        dcw_ref[CONV_WIDTH - 1:CONV_WIDTH, :] += jnp.sum(d * xr, axis=0, keepdims=True)
        dcb_ref[...] += jnp.sum(d, axis=0, keepdims=True)

    tile = pl.BlockSpec((tr, bw), lambda n, i: (i, n))
    cwb = pl.BlockSpec((CONV_WIDTH, bw), lambda n, i: (0, n))
    return pl.pallas_call(
        body, name=name, grid=(D // bw, nsteps),
        in_specs=[tile, pl.BlockSpec((tr, bw), lambda n, i: (jnp.minimum(i + 1, nsteps - 1), n)), tile,
                  pl.BlockSpec((tr, bw), lambda n, i: (jnp.maximum(i - 1, 0), n)), cwb],
        out_specs=[tile, cwb, pl.BlockSpec((1, bw), lambda n, i: (0, n))],
        out_shape=[jax.ShapeDtypeStruct((S, D), F32), jax.ShapeDtypeStruct((CONV_WIDTH, D), F32), jax.ShapeDtypeStruct((1, D), F32)],
        compiler_params=_params(("parallel", "arbitrary")),
    )(dxc, dxc, proj, proj, conv_w)


def rope_table(S):
    half = ROT_DIM // 2
    pos = jnp.arange(S, dtype=F32)
    inv = ROPE_THETA ** (-jnp.arange(0, ROT_DIM, 2, dtype=F32) / ROT_DIM)
    ang = pos[:, None] * inv[None, :]
    cos, sin = jnp.cos(ang), jnp.sin(ang)
    zero = jnp.zeros((S, HEAD_DIM - ROT_DIM), F32)
    c = jnp.concatenate([cos, cos, zero + 1.0], axis=1)
    a = jnp.concatenate([-sin, jnp.zeros((S, half), F32), zero], axis=1)
    b = jnp.concatenate([jnp.zeros((S, half), F32), sin, zero], axis=1)
    return jnp.stack([jnp.tile(t, (1, LANES // HEAD_DIM)) for t in (c, a, b)])


def _rope(t, tab):
    half = ROT_DIM // 2
    return t * tab[0] + pltpu.roll(t, LANES - half, 1) * tab[1] + pltpu.roll(t, half, 1) * tab[2]


def _rope_t(d, tab):
    half = ROT_DIM // 2
    return d * tab[0] + pltpu.roll(d * tab[1], half, 1) + pltpu.roll(d * tab[2], LANES - half, 1)


def _dup_head(t, hk, lo):
    sw = pltpu.roll(t, HEAD_DIM, 1)
    return jnp.where(lo, t, sw) if hk == 0 else jnp.where(lo, sw, t)


def _attn_common(n, sink_ref, q_ref, kp_ref, kc_ref, vp_ref, vc_ref, tc_ref, tp_ref, hk, pairs):
    tq = (tc_ref[0], tc_ref[1], tc_ref[2])
    tp = (tp_ref[0], tp_ref[1], tp_ref[2])
    lo = lax.broadcasted_iota(jnp.int32, (WINDOW, LANES), 1) < HEAD_DIM
    lo2 = lax.broadcasted_iota(jnp.int32, (2 * WINDOW, LANES), 1) < HEAD_DIM
    kband = jnp.concatenate([_rope(kp_ref[...], tp), _rope(kc_ref[...], tq)], axis=0)
    vband = jnp.concatenate([vp_ref[...], vc_ref[...]], axis=0)
    kd = _dup_head(kband, hk, lo2).astype(MXU_DTYPE)
    vd = _dup_head(vband, hk, lo2).astype(MXU_DTYPE)
    rows, sks = [], []
    for j in range(pairs):
        col = hk * pairs + j
        qp = _rope(q_ref[:, col * LANES:(col + 1) * LANES], tq)
        rows += [jnp.where(lo, qp, 0.0), jnp.where(lo, 0.0, qp)]
        sks += [jnp.full((WINDOW, 1), sink_ref[2 * col], F32), jnp.full((WINDOW, 1), sink_ref[2 * col + 1], F32)]
    qg = jnp.concatenate(rows, axis=0)
    sk = jnp.concatenate(sks, axis=0)
    G = 2 * pairs * WINDOW
    ri = lax.broadcasted_iota(jnp.int32, (G, 2 * WINDOW), 0) & (WINDOW - 1)
    kj = lax.broadcasted_iota(jnp.int32, (G, 2 * WINDOW), 1) - WINDOW
    valid = (kj <= ri) & (kj > ri - WINDOW) & (kj + n * WINDOW >= 0)
    s = lax.dot_general(qg.astype(MXU_DTYPE), kd, (((1,), (1,)), ((), ())), preferred_element_type=F32) * (HEAD_DIM ** -0.5)
    s = jnp.where(valid, s, NEG_INF)
    m = jnp.maximum(jnp.max(s, axis=1, keepdims=True), sk)
    e = jnp.exp(s - m)
    es = jnp.exp(sk - m)
    inv = 1.0 / (jnp.sum(e, axis=1, keepdims=True) + es)
    return qg, kd, vd, e * inv, es * inv, lo, lo2, tq, tp


def _attn_specs(D, NB):
    kcol = 3 * D // LANES
    q = pl.BlockSpec((WINDOW, D), lambda n: (n, 2))
    kc = pl.BlockSpec((WINDOW, LANES), lambda n: (n, kcol))
    kp = pl.BlockSpec((WINDOW, LANES), lambda n: (jnp.maximum(n - 1, 0), kcol))
    vc = pl.BlockSpec((WINDOW, LANES), lambda n: (n, kcol + 1))
    vp = pl.BlockSpec((WINDOW, LANES), lambda n: (jnp.maximum(n - 1, 0), kcol + 1))
    tc = pl.BlockSpec((3, WINDOW, LANES), lambda n: (0, n, 0))
    tp = pl.BlockSpec((3, WINDOW, LANES), lambda n: (0, jnp.maximum(n - 1, 0), 0))
    sink = pl.BlockSpec(memory_space=pltpu.SMEM)
    return [sink, q, kp, kc, vp, vc, tc, tp]


def attn_fwd(proj, sinks, tab, D, name):
    S = proj.shape[0]
    NB = S // WINDOW
    pairs = D // HEAD_DIM // N_KV_HEADS // 2

    def body(sink_ref, q_ref, kp_ref, kc_ref, vp_ref, vc_ref, tc_ref, tp_ref, o_ref):
        n = pl.program_id(0)
        for hk in range(N_KV_HEADS):
            _, _, vd, p, _, lo, _, _, _ = _attn_common(n, sink_ref, q_ref, kp_ref, kc_ref, vp_ref, vc_ref, tc_ref, tp_ref, hk, pairs)
            o = jnp.dot(p.astype(MXU_DTYPE), vd, preferred_element_type=F32)
            for j in range(pairs):
                col = hk * pairs + j
                oa = o[(2 * j) * WINDOW:(2 * j + 1) * WINDOW]
                ob = o[(2 * j + 1) * WINDOW:(2 * j + 2) * WINDOW]
                o_ref[:, col * LANES:(col + 1) * LANES] = jnp.where(lo, oa, ob)

    return pl.pallas_call(
        body, name=name, grid=(NB,), in_specs=_attn_specs(D, NB),
        out_specs=pl.BlockSpec((WINDOW, D), lambda n: (n, 0)), out_shape=jax.ShapeDtypeStruct((S, D), F32),
        compiler_params=_params(("parallel",)),
    )(sinks, proj, proj, proj, proj, proj, tab, tab)


def attn_bwd(proj, sinks, tab, o, do, D, name):
    S = proj.shape[0]
    NB = S // WINDOW
    pairs = D // HEAD_DIM // N_KV_HEADS // 2

    def body(sink_ref, q_ref, kp_ref, kc_ref, vp_ref, vc_ref, tc_ref, tp_ref, o_ref, do_ref, dq_ref, dk_ref, dv_ref, ds_ref):
        n = pl.program_id(0)

        @pl.when(n == 0)
        def _():
            ds_ref[...] = jnp.zeros_like(ds_ref)

        lane1 = lax.broadcasted_iota(jnp.int32, (1, LANES), 1)
        dsink = jnp.zeros((1, LANES), F32)
        dkt = dvt = None
        for hk in range(N_KV_HEADS):
            qg, kd, vd, p, ps, lo, lo2, tq, tp = _attn_common(n, sink_ref, q_ref, kp_ref, kc_ref, vp_ref, vc_ref, tc_ref, tp_ref, hk, pairs)
            dos, os_ = [], []
            for j in range(pairs):
                col = hk * pairs + j
                dop = do_ref[:, col * LANES:(col + 1) * LANES]
                op = o_ref[:, col * LANES:(col + 1) * LANES]
                dos += [jnp.where(lo, dop, 0.0), jnp.where(lo, 0.0, dop)]
                os_ += [jnp.where(lo, op, 0.0), jnp.where(lo, 0.0, op)]
            dog = jnp.concatenate(dos, axis=0)
            og = jnp.concatenate(os_, axis=0)
            dogm = dog.astype(MXU_DTYPE)
            dp = lax.dot_general(dogm, vd, (((1,), (1,)), ((), ())), preferred_element_type=F32)
            dr = jnp.sum(dog * og, axis=1, keepdims=True)
            ds = p * (dp - dr) * (HEAD_DIM ** -0.5)
            dsm = ds.astype(MXU_DTYPE)
            dqg = jnp.dot(dsm, kd, preferred_element_type=F32)
            dkd = jnp.dot(ds.T.astype(MXU_DTYPE), qg.astype(MXU_DTYPE), preferred_element_type=F32)
            dvd = jnp.dot(p.T.astype(MXU_DTYPE), dogm, preferred_element_type=F32)
            dkf = dkd + pltpu.roll(dkd, HEAD_DIM, 1)
            dvf = dvd + pltpu.roll(dvd, HEAD_DIM, 1)
            if hk == 0:
                dkt, dvt = dkf, dvf
            else:
                dkt, dvt = jnp.where(lo2, dkt, dkf), jnp.where(lo2, dvt, dvf)
            sd = ps * dr
            for j in range(pairs):
                col = hk * pairs + j
                dqa = dqg[(2 * j) * WINDOW:(2 * j + 1) * WINDOW]
                dqb = dqg[(2 * j + 1) * WINDOW:(2 * j + 2) * WINDOW]
                dq_ref[:, col * LANES:(col + 1) * LANES] = _rope_t(jnp.where(lo, dqa, dqb), tq)
                for t in range(2):
                    part = sd[(2 * j + t) * WINDOW:(2 * j + t + 1) * WINDOW]
                    val = jnp.sum(part, axis=0, keepdims=True)
                    dsink = dsink - jnp.where(lane1 == 2 * col + t, val, 0.0)
        dk_ref[...] = jnp.concatenate([_rope_t(dkt[:WINDOW], tp), _rope_t(dkt[WINDOW:], tq)], axis=0)
        dv_ref[...] = dvt
        ds_ref[...] += dsink

    blk = pl.BlockSpec((WINDOW, D), lambda n: (n, 0))
    band = pl.BlockSpec((None, 2 * WINDOW, LANES), lambda n: (n, 0, 0))
    return pl.pallas_call(
        body, name=name, grid=(NB,), in_specs=_attn_specs(D, NB) + [blk, blk],
        out_specs=[blk, band, band, pl.BlockSpec((1, LANES), lambda n: (0, 0))],
        out_shape=[jax.ShapeDtypeStruct((S, D), F32), jax.ShapeDtypeStruct((NB, 2 * WINDOW, LANES), F32),
                   jax.ShapeDtypeStruct((NB, 2 * WINDOW, LANES), F32), jax.ShapeDtypeStruct((1, LANES), F32)],
        compiler_params=_params(("arbitrary",)),
    )(sinks, proj, proj, proj, proj, proj, tab, tab, o, do)


def band_fold(dkb, dvb, name):
    NB = dkb.shape[0]
    k4 = dkb.reshape(NB, 2, WINDOW, LANES)
    v4 = dvb.reshape(NB, 2, WINDOW, LANES)

    def body(kc_ref, kn_ref, vc_ref, vn_ref, dk_ref, dv_ref):
        more = pl.program_id(0) < NB - 1
        dk_ref[...] = kc_ref[...] + jnp.where(more, kn_ref[...], 0.0)
        dv_ref[...] = vc_ref[...] + jnp.where(more, vn_ref[...], 0.0)

    cur = pl.BlockSpec((None, None, WINDOW, LANES), lambda n: (n, 1, 0, 0))
    nxt = pl.BlockSpec((None, None, WINDOW, LANES), lambda n: (jnp.minimum(n + 1, NB - 1), 0, 0, 0))
    out = pl.BlockSpec((WINDOW, LANES), lambda n: (n, 0))
    sds = jax.ShapeDtypeStruct((NB * WINDOW, LANES), F32)
    return pl.pallas_call(body, name=name, grid=(NB,), in_specs=[cur, nxt, cur, nxt], out_specs=[out, out], out_shape=[sds, sds],
                          compiler_params=_params(("parallel",)))(k4, k4, v4, v4)


CROSS_ROWS = 512


def _cross_probs(q, k, scale):
    s = lax.dot_general(q.astype(MXU_DTYPE), k.astype(MXU_DTYPE), (((1,), (1,)), ((), ())), preferred_element_type=F32) * scale
    e = jnp.exp(s - jnp.max(s, axis=1, keepdims=True))
    return e / jnp.sum(e, axis=1, keepdims=True)


def cross_fwd(qc, kv, name):
    S, D = qc.shape
    M = kv.shape[0]
    hd = D // CROSS_HEADS
    tq = min(CROSS_ROWS, S)

    def body(q_ref, kv_ref, o_ref):
        for h in range(CROSS_HEADS):
            p = _cross_probs(q_ref[:, h * hd:(h + 1) * hd], kv_ref[:, h * hd:(h + 1) * hd], hd ** -0.5)
            v = kv_ref[:, D + h * hd:D + (h + 1) * hd].astype(MXU_DTYPE)
            o_ref[:, h * hd:(h + 1) * hd] = jnp.dot(p.astype(MXU_DTYPE), v, preferred_element_type=F32)

    return pl.pallas_call(
        body, name=name, grid=(S // tq,), in_specs=[pl.BlockSpec((tq, D), lambda i: (i, 0)), pl.BlockSpec((M, 2 * D), lambda i: (0, 0))],
        out_specs=pl.BlockSpec((tq, D), lambda i: (i, 0)), out_shape=jax.ShapeDtypeStruct((S, D), F32),
        compiler_params=_params(("parallel",)),
    )(qc, kv)


def cross_bwd(qc, kv, do, name):
    S, D = qc.shape
    M = kv.shape[0]
    hd = D // CROSS_HEADS
    tq = min(CROSS_ROWS, S)

    def body(q_ref, kv_ref, do_ref, dq_ref, dkv_ref):
        @pl.when(pl.program_id(0) == 0)
        def _():
            dkv_ref[...] = jnp.zeros_like(dkv_ref)

        for h in range(CROSS_HEADS):
            q = q_ref[:, h * hd:(h + 1) * hd]
            k = kv_ref[:, h * hd:(h + 1) * hd]
            v = kv_ref[:, D + h * hd:D + (h + 1) * hd].astype(MXU_DTYPE)
            dom = do_ref[:, h * hd:(h + 1) * hd].astype(MXU_DTYPE)
            p = _cross_probs(q, k, hd ** -0.5)
            dp = lax.dot_general(dom, v, (((1,), (1,)), ((), ())), preferred_element_type=F32)
            ds = p * (dp - jnp.sum(p * dp, axis=1, keepdims=True)) * (hd ** -0.5)
            dq_ref[:, h * hd:(h + 1) * hd] = jnp.dot(ds.astype(MXU_DTYPE), k.astype(MXU_DTYPE), preferred_element_type=F32)
            dkv_ref[:, h * hd:(h + 1) * hd] += jnp.dot(ds.T.astype(MXU_DTYPE), q.astype(MXU_DTYPE), preferred_element_type=F32)
            dkv_ref[:, D + h * hd:D + (h + 1) * hd] += jnp.dot(p.T.astype(MXU_DTYPE), dom, preferred_element_type=F32)

    row = pl.BlockSpec((tq, D), lambda i: (i, 0))
    full = pl.BlockSpec((M, 2 * D), lambda i: (0, 0))
    return pl.pallas_call(
        body, name=name, grid=(S // tq,), in_specs=[row, full, row], out_specs=[row, full],
        out_shape=[jax.ShapeDtypeStruct((S, D), F32), jax.ShapeDtypeStruct((M, 2 * D), F32)],
        compiler_params=_params(("arbitrary",)),
    )(qc, kv, do)


def adamw(w, g, m, v, name):
    shape = w.shape
    cols = shape[-1]
    rows = int(np.prod(shape[:-1]))
    w2, g2, m2, v2 = (t.reshape(rows, cols) for t in (w, g, m, v))
    tr = _divisors(rows, SUBLANES, max(SUBLANES, (1 << 20) // (cols * 4) // SUBLANES * SUBLANES))[0]

    def body(w_ref, g_ref, m_ref, v_ref, d_ref, mo_ref, vo_ref):
        gg = g_ref[...]
        mn = ADAM_B1 * m_ref[...] + (1.0 - ADAM_B1) * gg
        vn = ADAM_B2 * v_ref[...] + (1.0 - ADAM_B2) * (gg * gg)
        m_hat = mn / (1.0 - ADAM_B1 ** ADAM_STEP)
        v_hat = vn / (1.0 - ADAM_B2 ** ADAM_STEP)
        d_ref[...] = -ADAM_LR * (m_hat / (jnp.sqrt(v_hat) + ADAM_EPS) + ADAM_WD * w_ref[...])
        mo_ref[...] = mn
        vo_ref[...] = vn

    blk = pl.BlockSpec((tr, cols), lambda i: (i, 0))
    sds = jax.ShapeDtypeStruct((rows, cols), F32)
    d, mn, vn = pl.pallas_call(body, name=name, grid=(rows // tr,), in_specs=[blk] * 4, out_specs=[blk] * 3, out_shape=[sds] * 3,
                               compiler_params=_params(("parallel",)))(w2, g2, m2, v2)
    return d.reshape(shape), mn.reshape(shape), vn.reshape(shape)


def sum_devices(parts, name):
    n, rows, cols = parts.shape

    def body(p_ref, o_ref):
        acc = p_ref[0]
        for k in range(1, n):
            acc = acc + p_ref[k]
        o_ref[...] = acc

    return pl.pallas_call(body, name=name, in_specs=[pl.BlockSpec(memory_space=pltpu.VMEM)],
                          out_specs=pl.BlockSpec(memory_space=pltpu.VMEM), out_shape=jax.ShapeDtypeStruct((rows, cols), F32))(parts)


HBM_SPEC = pl.BlockSpec(memory_space=pltpu.HBM)


def _place():
    return lax.axis_index("x"), lax.axis_index("y"), lax.axis_index("c")


def _remote(src, dst, send_sems, recv_sems, k, to):
    return pltpu.make_async_remote_copy(src_ref=src, dst_ref=dst, send_sem=send_sems.at[k], recv_sem=recv_sems.at[k],
                                        device_id=to, device_id_type=MESH_ID)


def gather_weights(shards, name):
    n = len(shards)
    lh = shards[0].shape[0] // 2

    def gathered_shape(s):
        return s.shape[:-2] + (N_CHIPS,) + s.shape[-2:]

    def body(*refs):
        w_refs, out_refs = refs[:n], refs[n:2 * n]
        send_sems, recv_sems, pass_send, pass_recv = refs[2 * n:]
        x, y, c = _place()
        sibling = (x, y, 1 - c)
        chips = [(1 - x, y), (x, 1 - y), (1 - x, 1 - y)]
        mine = pl.ds(c * lh, lh)
        theirs = pl.ds((1 - c) * lh, lh)
        every = pl.ds(0, 2 * lh)

        def slab(a, px, py, layers):
            ref, slot = out_refs[a], 2 * px + py
            return ref.at[layers, slot] if len(ref.shape) == 4 else ref.at[layers, :, slot]

        own = [_remote(w_refs[a], slab(a, x, y, every), pass_send, pass_recv, 3 * n + a, sibling) for a in range(n)]
        first = [_remote(w_refs[a].at[mine], slab(a, x, y, mine), send_sems, recv_sems, 3 * a + k, (*chip, c))
                 for a in range(n) for k, chip in enumerate(chips)]
        for cp in own + first:
            cp.start()
        passed = []
        for k, chip in enumerate(chips):
            for a in range(n):
                landed = slab(a, *chip, mine)
                _remote(landed, landed, send_sems, recv_sems, 3 * a + k, (*chip, c)).wait_recv()
                cp = _remote(landed, landed, pass_send, pass_recv, 3 * a + k, sibling)
                cp.start()
                passed.append(cp)
        for k, chip in enumerate(chips):
            for a in range(n):
                landed = slab(a, *chip, theirs)
                _remote(landed, landed, pass_send, pass_recv, 3 * a + k, sibling).wait_recv()
        for cp in own:
            cp.wait()
        for cp in first + passed:
            cp.wait_send()

    return pl.pallas_call(
        body, name=name, in_specs=[HBM_SPEC] * n, out_specs=[HBM_SPEC] * n,
        out_shape=[jax.ShapeDtypeStruct(gathered_shape(s), s.dtype) for s in shards],
        scratch_shapes=[pltpu.SemaphoreType.DMA((3 * n,))] * 2 + [pltpu.SemaphoreType.DMA((4 * n,))] * 2,
    )(*shards)


def swap_sibling(parts, name):
    n = len(parts)

    def body(*refs):
        v_refs, out_refs, send_sems, recv_sems = refs[:n], refs[n:2 * n], refs[2 * n], refs[2 * n + 1]
        x, y, c = _place()
        cps = []
        for a in range(n):
            hr = v_refs[a].shape[2] // 2
            cps.append(_remote(v_refs[a].at[:, :, pl.ds((1 - c) * hr, hr)], out_refs[a], send_sems, recv_sems, a, (x, y, 1 - c)))
        for cp in cps:
            cp.start()
        for cp in cps:
            cp.wait()

    return pl.pallas_call(
        body, name=name, in_specs=[HBM_SPEC] * n, out_specs=[HBM_SPEC] * n,
        out_shape=[jax.ShapeDtypeStruct(v.shape[:2] + (v.shape[2] // 2, v.shape[3]), v.dtype) for v in parts],
        scratch_shapes=[pltpu.SemaphoreType.DMA((n,))] * 2,
    )(*parts)


def scatter_chips(parts, name):
    n = len(parts)

    def body(*refs):
        t_refs, out_refs, send_sems, recv_sems = refs[:n], refs[n:2 * n], refs[2 * n], refs[2 * n + 1]
        x, y, c = _place()
        chips = [(1 - x, y), (x, 1 - y), (1 - x, 1 - y)]
        sent = [_remote(t_refs[a].at[:, 2 * px + py], out_refs[a].at[:, k], send_sems, recv_sems, 3 * a + k, (px, py, c))
                for a in range(n) for k, (px, py) in enumerate(chips)]
        for cp in sent:
            cp.start()
        for a in range(n):
            for k, (px, py) in enumerate(chips):
                landed = out_refs[a].at[:, k]
                _remote(landed, landed, send_sems, recv_sems, 3 * a + k, (px, py, c)).wait_recv()
        for cp in sent:
            cp.wait_send()

    return pl.pallas_call(
        body, name=name, in_specs=[HBM_SPEC] * n, out_specs=[HBM_SPEC] * n,
        out_shape=[jax.ShapeDtypeStruct((t.shape[0], N_CHIPS - 1) + t.shape[2:], t.dtype) for t in parts],
        scratch_shapes=[pltpu.SemaphoreType.DMA((3 * n,))] * 2,
    )(*parts)


def join_halves(halves, name):
    n = len(halves)

    def body(*refs):
        out_refs, send_sems, recv_sems = refs[n:2 * n], refs[2 * n], refs[2 * n + 1]
        x, y, c = _place()
        cps = []
        for a in range(n):
            hr = out_refs[a].shape[1] // 2
            mine = out_refs[a].at[:, pl.ds(c * hr, hr)]
            cps.append(_remote(mine, mine, send_sems, recv_sems, a, (x, y, 1 - c)))
        for cp in cps:
            cp.start()
        for a in range(n):
            hr = out_refs[a].shape[1] // 2
            theirs = out_refs[a].at[:, pl.ds((1 - c) * hr, hr)]
            _remote(theirs, theirs, send_sems, recv_sems, a, (x, y, 1 - c)).wait_recv()
        for cp in cps:
            cp.wait_send()

    return pl.pallas_call(
        body, name=name, in_specs=[HBM_SPEC] * n, out_specs=[HBM_SPEC] * n,
        out_shape=[jax.ShapeDtypeStruct(f.shape, f.dtype) for f in halves], input_output_aliases={a: a for a in range(n)},
        scratch_shapes=[pltpu.SemaphoreType.DMA((n,))] * 2,
    )(*halves)


def gather_devices(v, name):
    def body(v_ref, out_ref, send_sems, recv_sems, local_sem):
        x, y, c = _place()
        me = 4 * x + 2 * y + c
        own = pltpu.make_async_copy(v_ref, out_ref.at[me], local_sem)
        own.start()
        peers = [((x + dx) % 2, (y + dy) % 2, (c + dc) % 2) for dx in (0, 1) for dy in (0, 1) for dc in (0, 1)][1:]
        sent = []
        for k, peer in enumerate(peers):
            cp = pltpu.make_async_remote_copy(src_ref=v_ref, dst_ref=out_ref.at[me], send_sem=send_sems.at[k], recv_sem=recv_sems.at[k],
                                              device_id=peer, device_id_type=MESH_ID)
            cp.start()
            sent.append(cp)
        for k, (px, py, pc) in enumerate(peers):
            slot = out_ref.at[4 * px + 2 * py + pc]
            pltpu.make_async_remote_copy(src_ref=slot, dst_ref=slot, send_sem=send_sems.at[k], recv_sem=recv_sems.at[k],
                                         device_id=(px, py, pc), device_id_type=MESH_ID).wait_recv()
        for cp in sent:
            cp.wait_send()
        own.wait()

    vm = pl.BlockSpec(memory_space=pltpu.VMEM)
    return pl.pallas_call(body, name=name, in_specs=[vm], out_specs=vm, out_shape=jax.ShapeDtypeStruct((N_DEV,) + v.shape, v.dtype),
                          scratch_shapes=[pltpu.SemaphoreType.DMA((N_DEV - 1,)), pltpu.SemaphoreType.DMA((N_DEV - 1,)),
                                          pltpu.SemaphoreType.DMA])(v)


ADD_ROWS = 512


def add_pair(place, a, b, name):
    L, n, hr, cols = b.shape
    tr = _divisors(hr, 2 * SUBLANES, ADD_ROWS)[0]
    nb = hr // tr

    def body(p_ref, a_ref, b_ref, o_ref):
        del p_ref
        o_ref[...] = (a_ref[...].astype(F32) + b_ref[...].astype(F32)).astype(o_ref.dtype)

    blk = pl.BlockSpec((None, None, tr, cols), lambda l, d, i, p: (l, d, i, 0))
    grid_spec = pltpu.PrefetchScalarGridSpec(
        num_scalar_prefetch=1, grid=(L, n, nb),
        in_specs=[pl.BlockSpec((None, None, tr, cols), lambda l, d, i, p: (l, d, p[0] * nb + i, 0)), blk], out_specs=blk)
    return pl.pallas_call(body, name=name, grid_spec=grid_spec, out_shape=jax.ShapeDtypeStruct(b.shape, b.dtype),
                          compiler_params=_params(("parallel", "parallel", "parallel")))(place, a, b)


def add_chips(place, own, others, name):
    L, n, hr, cols = others.shape
    tr = _divisors(hr, 2 * SUBLANES, ADD_ROWS)[0]
    nb = hr // tr

    def body(p_ref, own_ref, *refs):
        del p_ref
        acc = own_ref[...].astype(F32)
        for k in range(n):
            acc = acc + refs[k][...].astype(F32)
        refs[n][...] = acc

    ins = [pl.BlockSpec((None, None, tr, cols), lambda l, i, p: (l, p[1], i, 0))]
    ins += [pl.BlockSpec((None, None, tr, cols), functools.partial(lambda k, l, i, p: (l, k, i, 0), k)) for k in range(n)]
    grid_spec = pltpu.PrefetchScalarGridSpec(num_scalar_prefetch=1, grid=(L, nb), in_specs=ins,
                                             out_specs=pl.BlockSpec((None, tr, cols), lambda l, i, p: (l, p[0] * nb + i, 0)))
    return pl.pallas_call(body, name=name, grid_spec=grid_spec, out_shape=jax.ShapeDtypeStruct((L, 2 * hr, cols), F32),
                          compiler_params=_params(("parallel", "parallel")))(place, own, *([others] * n))


def _alpha(depth):
    return (2 * depth) ** 0.25


def _wmm(a, weight, mode, name):
    arr, how = weight
    return mm(a, arr, mode, name, **how)


def layer_fwd(h, mem, w, tab, alpha):
    D = h.shape[1]
    proj = _wmm(h, w["w_in"], "nn", "mm_proj")
    xc, r, ig, a, b = rg_gates_fwd(proj, w["conv_w"], w["conv_b"], w["w_rg"], w["b_rg"], w["w_ig"], w["b_ig"], w["lru_lambda"], "rg_gates_fwd")
    hs, y_rnn = rg_scan_fwd(proj, a, b, "rg_scan_fwd")
    y_attn = attn_fwd(proj, w["sinks"], tab, D, "attn_fwd")
    pr = _wmm(y_rnn, w["w_br_rnn"], "nn", "mm_br_rnn")
    pa = _wmm(y_attn, w["w_br_attn"], "nn", "mm_br_attn")
    merged = merge_fwd(proj, pr, pa, "merge_fwd")
    mix = _wmm(merged, w["w_out"], "nn", "mm_out")
    h1, xh1, rs1 = ln_fwd(h, mix, w["ln1_g"], w["ln1_b"], alpha, "ln1_fwd")
    qc = _wmm(h1, w["cq_w"], "nn", "mm_cq")
    kv = _wmm(mem, w["ckv_w"], "nn", "mm_ckv")
    o = cross_fwd(qc, kv, "cross_fwd")
    co = _wmm(o, w["co_w"], "nn", "mm_co")
    h2, xh2, rs2 = ln_fwd(h1, co, w["ln2_g"], w["ln2_b"], alpha, "ln2_fwd")
    gu = _wmm(h2, w["ffn_wi"], "nn", "mm_ffn_wi")
    act = swiglu_fwd(gu, "swiglu_fwd")
    f = _wmm(act, w["ffn_wo"], "nn", "mm_ffn_wo")
    h3, xh3, rs3 = ln_fwd(h2, f, w["ln3_g"], w["ln3_b"], alpha, "ln3_fwd")
    saved = dict(h=h, proj=proj, xc=xc, r=r, ig=ig, a=a, hs=hs, y_rnn=y_rnn, y_attn=y_attn, pr=pr, pa=pa, xh1=xh1, rs1=rs1, h1=h1,
                 qc=qc, kv=kv, o=o, xh2=xh2, rs2=rs2, h2=h2, gu=gu, xh3=xh3, rs3=rs3)
    return h3, saved


def layer_bwd(dh, mem, w, s, tab, alpha):
    D = dh.shape[1]
    g = {}
    wg = dict(out_dtype=MXU_DTYPE)
    dz3, g["ln3_g"], g["ln3_b"] = ln_bwd(dh, None, s["xh3"], s["rs3"], w["ln3_g"], 1.0, "ln3_bwd")
    act = swiglu_fwd(s["gu"], "swiglu_refwd")
    g["ffn_wo"] = mm(act, dz3, "tn", "mm_d_ffn_wo", **wg)
    dact = _wmm(dz3, w["ffn_wo"], "nt", "mm_dact")
    dgu = swiglu_bwd(s["gu"], dact, "swiglu_bwd")
    g["ffn_wi"] = mm(s["h2"], dgu, "tn", "mm_d_ffn_wi", out_chips=True, **wg)
    dh2 = _wmm(dgu, w["ffn_wi"], "nt", "mm_dh2")
    dz2, g["ln2_g"], g["ln2_b"] = ln_bwd(dz3, dh2, s["xh2"], s["rs2"], w["ln2_g"], alpha, "ln2_bwd")
    g["co_w"] = mm(s["o"], dz2, "tn", "mm_d_co", **wg)
    do = _wmm(dz2, w["co_w"], "nt", "mm_do")
    dqc, dkv = cross_bwd(s["qc"], s["kv"], do, "cross_bwd")
    g["cq_w"] = mm(s["h1"], dqc, "tn", "mm_d_cq", **wg)
    g["ckv_w"] = mm(mem, dkv, "tn", "mm_d_ckv", out_chips=True, **wg)
    dh1 = _wmm(dqc, w["cq_w"], "nt", "mm_dh1")
    dz1, g["ln1_g"], g["ln1_b"] = ln_bwd(dz2, dh1, s["xh1"], s["rs1"], w["ln1_g"], alpha, "ln1_bwd")
    merged = merge_fwd(s["proj"], s["pr"], s["pa"], "merge_refwd")
    g["w_out"] = mm(merged, dz1, "tn", "mm_d_out", **wg)
    dm = _wmm(dz1, w["w_out"], "nt", "mm_dmerged")
    dpr, dpa, dg_rnn, dg_attn = merge_bwd(s["proj"], s["pr"], s["pa"], dm, "merge_bwd")
    g["w_br_rnn"] = mm(s["y_rnn"], dpr, "tn", "mm_d_br_rnn", **wg)
    g["w_br_attn"] = mm(s["y_attn"], dpa, "tn", "mm_d_br_attn", **wg)
    dy_rnn = _wmm(dpr, w["w_br_rnn"], "nt", "mm_dy_rnn")
    dy_attn = _wmm(dpa, w["w_br_attn"], "nt", "mm_dy_attn")
    dq, dkb, dvb, dsink = attn_bwd(s["proj"], w["sinks"], tab, s["y_attn"], dy_attn, D, "attn_bwd")
    dk, dv = band_fold(dkb, dvb, "band_fold")
    g["sinks"] = dsink[:, :w["sinks"].shape[0]]
    dgr, gt = rg_scan_bwd(s["proj"], dy_rnn, s["hs"], s["a"], "rg_scan_bwd")
    dxc, g["w_rg"], g["w_ig"], g["b_rg"], g["b_ig"], g["lru_lambda"] = rg_gates_bwd(
        gt, s["hs"], s["xc"], s["r"], s["ig"], w["w_rg"], w["w_ig"], w["lru_lambda"], "rg_gates_bwd")
    dxr, g["conv_w"], g["conv_b"] = rg_conv_bwd(s["proj"], dxc, w["conv_w"], "rg_conv_bwd")
    dproj = jnp.concatenate([dxr, dgr, dq, dk, dv, dg_rnn, dg_attn], axis=1)
    g["w_in"] = mm(s["h"], dproj, "tn", "mm_d_in")
    dhm = _wmm(dproj, w["w_in"], "nt", "mm_dh")
    return axpby(dz1, dhm, alpha, "layer_dx"), g


def local_step(x, mem, target, layers):
    L = len(layers)
    alpha = _alpha(L)
    tab = rope_table(x.shape[0])
    h, saved = x, []
    for wl in layers:
        h, s = layer_fwd(h, mem, wl, tab, alpha)
        saved.append(s)
    dh, loss = loss_head(h, target, "loss_head")
    grads = [None] * L
    for l in reversed(range(L)):
        dh, grads[l] = layer_bwd(dh, mem, layers[l], saved[l], tab, alpha)
    return loss, dh, grads


def _pad_rows(flat):
    n = flat.shape[0]
    rows = -(-n // (LANES * SUBLANES)) * SUBLANES
    return jnp.pad(flat, (0, rows * LANES - n)).reshape(rows, LANES)


def kernel(x, mem, w_in, conv_w, conv_b, w_rg, b_rg, w_ig, b_ig, lru_lambda, w_br_rnn, w_br_attn, sinks, w_out, ln1_g, ln1_b, cq_w, ckv_w, co_w, ln2_g, ln2_b, ffn_wi, ffn_wo, ln3_g, ln3_b, loss_target, m_w_in, m_conv_w, m_conv_b, m_w_rg, m_b_rg, m_w_ig, m_b_ig, m_lru_lambda, m_w_br_rnn, m_w_br_attn, m_sinks, m_w_out, m_ln1_g, m_ln1_b, m_cq_w, m_ckv_w, m_co_w, m_ln2_g, m_ln2_b, m_ffn_wi, m_ffn_wo, m_ln3_g, m_ln3_b, v_w_in, v_conv_w, v_conv_b, v_w_rg, v_b_rg, v_w_ig, v_b_ig, v_lru_lambda, v_w_br_rnn, v_w_br_attn, v_sinks, v_w_out, v_ln1_g, v_ln1_b, v_cq_w, v_ckv_w, v_co_w, v_ln2_g, v_ln2_b, v_ffn_wi, v_ffn_wo, v_ln3_g, v_ln3_b):
    args = dict(locals())
    w = {n: args[n] for n in WEIGHTS}
    m = {n: args["m_" + n] for n in WEIGHTS}
    v = {n: args["v_" + n] for n in WEIGHTS}
    cx, cy, cc = _place()
    chip = 2 * cx + cy
    L = w_in.shape[0]

    gathered = dict(zip(BIG, gather_weights([w[n].astype(MXU_DTYPE) for n in BIG], "gather_weights")))
    cw_rows = _pad_rows(conv_w.reshape(-1))
    cw_all = gather_devices(cw_rows, "gather_conv_w")[0::2]
    cw_parts = cw_all.reshape(N_CHIPS, -1)[:, :conv_w.size].reshape((N_CHIPS,) + conv_w.shape)
    conv_full = jnp.concatenate([cw_parts[k] for k in range(N_CHIPS)], axis=2)
    layers = []
    for l in range(L):
        wl = {}
        for n in BIG:
            gw = gathered[n]
            rows_joined = gw.reshape(gw.shape[:-3] + (-1, gw.shape[-1]))
            if n == "w_in":
                wl[n] = (jnp.concatenate([gw[l, k] for k in range(N_CHIPS)], axis=1), {})
            elif n in COL_BLOCKED:
                wl[n] = (gw, dict(b_index=(l,), chips=True))
            elif n in GATE_WEIGHTS:
                wl[n] = rows_joined[l]
            else:
                wl[n] = (rows_joined, dict(b_index=(l,)))
        for n in SMALL:
            wl[n] = conv_full[l] if n == "conv_w" else w[n][l] if n == "sinks" else w[n][l][None, :]
        layers.append(wl)

    loss11, dx, grads = local_step(x[0], mem[0], loss_target[0], layers)
    loss = lax.psum(loss11[0, 0], ("x", "y", "c"))

    def for_chips(n, g):
        if n in COL_BLOCKED:
            return g
        if n in GATE_WEIGHTS:
            nb, bw, _ = g.shape
            g = g.reshape(nb, N_CHIPS, bw // N_CHIPS, bw).transpose(1, 0, 2, 3).reshape(N_CHIPS, nb * bw // N_CHIPS, bw)
        elif SHARD_AXIS[n] == 0:
            g = g.reshape(N_CHIPS, g.shape[0] // N_CHIPS, g.shape[1])
        else:
            g = jnp.stack(jnp.split(g, N_CHIPS, axis=1))
        return g.astype(MXU_DTYPE)

    partial_sums = [jnp.stack([for_chips(n, gl[n]) for gl in grads]) for n in BIG]
    place = jnp.stack([cc, chip]).astype(jnp.int32)
    from_sibling = swap_sibling(partial_sums, "grad_to_sibling")
    chip_sums = [add_pair(place, a, b, "grad_add_pair_" + n) for n, a, b in zip(BIG, partial_sums, from_sibling)]
    from_chips = scatter_chips(chip_sums, "grad_scatter")
    reduced = [add_chips(place, a, b, "grad_add_chips_" + n) for n, a, b in zip(BIG, chip_sums, from_chips)]
    reduced = join_halves(reduced, "grad_join")
    gshard = {n: r.reshape(w[n].shape) for n, r in zip(BIG, reduced)}

    small_full = {n: jnp.stack([gl[n] for gl in grads]).reshape(w[n].shape[:1] + ((CONV_WIDTH, -1) if n == "conv_w" else (-1,)))
                  for n in SMALL}
    small_flat = jnp.concatenate([small_full[n].reshape(-1) for n in SMALL])
    small_sum = sum_devices(gather_devices(_pad_rows(small_flat), "gather_small_grads"), "sum_small_grads").reshape(-1)
    off = 0
    for n in SMALL:
        gfull = small_sum[off:off + small_full[n].size].reshape(small_full[n].shape)
        off += small_full[n].size
        if n == "conv_w":
            width = conv_w.shape[2]
            gfull = lax.dynamic_slice_in_dim(gfull, chip * width, width, axis=2)
        gshard[n] = gfull

    delta, new_m, new_v = {}, {}, {}
    for n in WEIGHTS:
        delta[n], new_m[n], new_v[n] = adamw(w[n], gshard[n], m[n], v[n], "adamw_" + n)
    return (loss, dx[None], *[gshard[n] for n in WEIGHTS], *[delta[n] for n in WEIGHTS], *[new_m[n] for n in WEIGHTS],
            *[new_v[n] for n in WEIGHTS])
```

```python
import functools
import math

import jax
import jax.numpy as jnp
import numpy as np
from jax import lax
from jax.experimental import pallas as pl
from jax.experimental.pallas import tpu as pltpu

F32 = jnp.float32
BF16 = jnp.bfloat16
MXU_DTYPE = BF16

HEAD_DIM = 64
N_KV_HEADS = 2
WINDOW = 128
ROT_DIM = HEAD_DIM // 4
ROPE_THETA = 500000.0
CROSS_HEADS = 4
RNN_BLOCKS = 4
CONV_WIDTH = 4
LRU_C = 8.0
LN_EPS = 1e-5
NEG_INF = -1e30
ADAM_LR = 0.001
ADAM_B1 = 0.9
ADAM_B2 = 0.999
ADAM_EPS = 1e-08
ADAM_WD = 0.01
ADAM_STEP = 10

VMEM_BYTES_V7X = 64 * 1024 * 1024
VMEM_BLOCK_BUDGET = 36 * 1024 * 1024
LANES = 128
SUBLANES = 8

MESH_ID = pl.DeviceIdType.MESH
N_CHIPS = 4
N_DEV = 8

BIG = ("w_in", "w_rg", "w_ig", "w_br_rnn", "w_br_attn", "w_out", "cq_w", "ckv_w", "co_w", "ffn_wi", "ffn_wo")
SHARD_AXIS = {"w_in": 1, "w_rg": 1, "w_ig": 1, "w_br_rnn": 0, "w_br_attn": 0, "w_out": 0, "cq_w": 0, "ckv_w": 1,
              "co_w": 0, "ffn_wi": 1, "ffn_wo": 0}
SMALL = ("conv_w", "conv_b", "b_rg", "b_ig", "lru_lambda", "sinks", "ln1_g", "ln1_b", "ln2_g", "ln2_b", "ln3_g", "ln3_b")
WEIGHTS = ("w_in", "conv_w", "conv_b", "w_rg", "b_rg", "w_ig", "b_ig", "lru_lambda", "w_br_rnn", "w_br_attn", "sinks",
           "w_out", "ln1_g", "ln1_b", "cq_w", "ckv_w", "co_w", "ln2_g", "ln2_b", "ffn_wi", "ffn_wo", "ln3_g", "ln3_b")
GATE_WEIGHTS = ("w_rg", "w_ig")
COL_BLOCKED = ("ckv_w", "ffn_wi")


def _params(dims=None, vmem=None):
    return pltpu.CompilerParams(dimension_semantics=dims, vmem_limit_bytes=vmem)


def _vmem_limit(block_bytes, temp_bytes=0):
    want = int(2 * block_bytes + temp_bytes) + (6 << 20)
    return max(32 << 20, min(want, VMEM_BYTES_V7X - (6 << 20)))


def _divisors(n, align, cap):
    out = [d for d in range(align, min(n, cap) + 1, align) if n % d == 0]
    if n <= cap and n not in out:
        out.append(n)
    return sorted(out, reverse=True) or [n]


def _sigmoid(x):
    return 1.0 / (1.0 + jnp.exp(-x))


def _gelu_parts(x):
    c = math.sqrt(2.0 / math.pi)
    u = c * (x + 0.044715 * x * x * x)
    t = jnp.tanh(u)
    return t, c * (1.0 + 3 * 0.044715 * x * x)


def _gelu(x):
    t, _ = _gelu_parts(x)
    return 0.5 * x * (1.0 + t)


def _gelu_grad(x):
    t, du = _gelu_parts(x)
    return 0.5 * (1.0 + t) + 0.5 * x * (1.0 - t * t) * du


def _neg_expm1(x):
    series = x * (1.0 + x * (0.5 + x * (1.0 / 6 + x * (1.0 / 24 + x * (1.0 / 120)))))
    return -jnp.where(x > -0.1, series, jnp.exp(x) - 1.0)


def _softplus_neg(lam):
    x = -lam
    return jnp.maximum(x, 0.0) + jnp.log1p(jnp.exp(-jnp.abs(x)))


STEP_US = 0.35
HBM_BYTES_PER_US = 2.5e6
MXU_FLOPS_PER_US = 7e8


def mm(a, b, mode, name, *, b_index=(), chips=False, out_chips=False, out_dtype=F32):
    nlead = len(b_index) + (1 if chips else 0)
    bk, bn = b.shape[nlead:]
    if mode == "nn":
        (M, K), N, cs = a.shape, (bn * N_CHIPS if chips else bn), bn
    elif mode == "nt":
        (M, K), N, cs = a.shape, bk, bn
    else:
        (K, M), N, cs = a.shape, bn, bn // N_CHIPS
    asz, bsz, osz = a.dtype.itemsize, b.dtype.itemsize, jnp.dtype(out_dtype).itemsize
    n_unit = cs if (chips and mode == "nn") or out_chips else N
    k_unit = cs if (chips and mode == "nt") else K
    tms = _divisors(M, LANES if mode == "tn" else SUBLANES, 2048)
    tns = _divisors(n_unit, LANES, 2048)
    tks = sorted(set(_divisors(k_unit, LANES, k_unit) + ([K] if not chips else [])), reverse=True)
    best = None
    for tm in tms:
        for tn in tns:
            for tk in tks:
                nk = K // tk
                scratch = tm * tn * 4 if (nk > 1 and osz != 4) else 0
                blocks = tm * tk * asz + tn * tk * bsz + tm * tn * osz
                temps = tm * tk * (2 + (4 if mode == "tn" else 0)) + tn * tk * 2 + tm * tn * 4 + scratch
                if 2 * blocks + temps > VMEM_BLOCK_BUDGET + (8 << 20):
                    continue
                ni, nj = M // tm, N // tn
                traffic = M * K * asz * (nj if nk > 1 else 1) + N * K * bsz * (1 if nj * nk == 1 else ni) + M * N * osz
                busy = max(traffic / HBM_BYTES_PER_US, 2.0 * M * N * K / MXU_FLOPS_PER_US)
                cost = ni * nj * nk * STEP_US + busy + blocks / HBM_BYTES_PER_US
                if best is None or cost < best[0]:
                    best = (cost, tm, tn, tk, blocks, temps)
    _, tm, tn, tk, blocks, temps = best
    nk, npc, kpc = K // tk, n_unit // tn, k_unit // tk
    use_scratch = nk > 1 and osz != 4

    def body(a_ref, b_ref, o_ref, *acc):
        av = a_ref[...].astype(MXU_DTYPE)
        bv = b_ref[...].astype(MXU_DTYPE)
        dn = {"nn": (((1,), (0,)), ((), ())), "nt": (((1,), (1,)), ((), ())), "tn": (((0,), (0,)), ((), ()))}[mode]
        r = lax.dot_general(av, bv, dn, preferred_element_type=F32)
        if nk == 1:
            o_ref[...] = r.astype(o_ref.dtype)
        else:
            acc_ref = acc[0] if use_scratch else o_ref

            @pl.when(pl.program_id(2) == 0)
            def _():
                acc_ref[...] = r

            @pl.when(pl.program_id(2) > 0)
            def _():
                acc_ref[...] += r

            if use_scratch:
                @pl.when(pl.program_id(2) == nk - 1)
                def _():
                    o_ref[...] = acc_ref[...].astype(o_ref.dtype)

    if mode == "tn":
        a_spec = pl.BlockSpec((tk, tm), lambda i, j, k: (k, i))
    else:
        a_spec = pl.BlockSpec((tm, tk), lambda i, j, k: (i, k))
    lead = (None,) * nlead
    if mode == "nt":
        bmap = (lambda i, j, k: b_index + (k // kpc, j, k % kpc)) if chips else (lambda i, j, k: b_index + (j, k))
        b_spec = pl.BlockSpec(lead + (tn, tk), bmap)
    else:
        bmap = (lambda i, j, k: b_index + (j // npc, k, j % npc)) if chips else (lambda i, j, k: b_index + (k, j))
        b_spec = pl.BlockSpec(lead + (tk, tn), bmap)
    if out_chips:
        o_spec = pl.BlockSpec((None, tm, tn), lambda i, j, k: (j // npc, i, j % npc))
        o_shape = jax.ShapeDtypeStruct((N_CHIPS, M, cs), out_dtype)
    else:
        o_spec = pl.BlockSpec((tm, tn), lambda i, j, k: (i, j))
        o_shape = jax.ShapeDtypeStruct((M, N), out_dtype)
    return pl.pallas_call(
        body, name=name, grid=(M // tm, N // tn, nk), in_specs=[a_spec, b_spec], out_specs=o_spec, out_shape=o_shape,
        scratch_shapes=[pltpu.VMEM((tm, tn), F32)] if use_scratch else [],
        compiler_params=_params(("parallel", "parallel", "arbitrary"), _vmem_limit(blocks, temps)),
    )(a, b)


ROW_TILE = 512
GATE_ROWS = 1024


def ln_fwd(h, f, g, b, alpha, name):
    S, D = h.shape
    tr = min(ROW_TILE, S)

    def body(h_ref, f_ref, g_ref, b_ref, y_ref, xh_ref, rs_ref):
        z = alpha * h_ref[...] + f_ref[...]
        mu = jnp.mean(z, axis=-1, keepdims=True)
        zc = z - mu
        var = jnp.mean(zc * zc, axis=-1, keepdims=True)
        rs = lax.rsqrt(var + LN_EPS)
        xh = zc * rs
        y_ref[...] = xh * g_ref[...] + b_ref[...]
        xh_ref[...] = xh
        rs_ref[...] = rs

    row = pl.BlockSpec((tr, D), lambda i: (i, 0))
    vec = pl.BlockSpec((1, D), lambda i: (0, 0))
    return pl.pallas_call(
        body, name=name, grid=(S // tr,), in_specs=[row, row, vec, vec],
        out_specs=[row, row, pl.BlockSpec((tr, 1), lambda i: (i, 0))],
        out_shape=[jax.ShapeDtypeStruct((S, D), F32), jax.ShapeDtypeStruct((S, D), F32), jax.ShapeDtypeStruct((S, 1), F32)],
        compiler_params=_params(("parallel",), 48 << 20),
    )(h, f, g, b)


def ln_bwd(dy_a, dy_b, xh, rs, g, c1, name):
    S, D = xh.shape
    tr = min(ROW_TILE, S)
    two = dy_b is not None

    def body(*refs):
        if two:
            a_ref, b_ref, xh_ref, rs_ref, g_ref, dz_ref, dg_ref, db_ref = refs
            dy = c1 * a_ref[...] + b_ref[...]
        else:
            a_ref, xh_ref, rs_ref, g_ref, dz_ref, dg_ref, db_ref = refs
            dy = a_ref[...]
        x = xh_ref[...]
        dyg = dy * g_ref[...]
        m1 = jnp.mean(dyg, axis=-1, keepdims=True)
        m2 = jnp.mean(dyg * x, axis=-1, keepdims=True)
        dz_ref[...] = rs_ref[...] * (dyg - m1 - x * m2)

        @pl.when(pl.program_id(0) == 0)
        def _():
            dg_ref[...] = jnp.zeros_like(dg_ref)
            db_ref[...] = jnp.zeros_like(db_ref)

        dg_ref[...] += jnp.sum(dy * x, axis=0, keepdims=True)
        db_ref[...] += jnp.sum(dy, axis=0, keepdims=True)

    row = pl.BlockSpec((tr, D), lambda i: (i, 0))
    vec = pl.BlockSpec((1, D), lambda i: (0, 0))
    ins = [row, row] if two else [row]
    args = (dy_a, dy_b) if two else (dy_a,)
    return pl.pallas_call(
        body, name=name, grid=(S // tr,), in_specs=ins + [row, pl.BlockSpec((tr, 1), lambda i: (i, 0)), vec],
        out_specs=[row, vec, vec],
        out_shape=[jax.ShapeDtypeStruct((S, D), F32), jax.ShapeDtypeStruct((1, D), F32), jax.ShapeDtypeStruct((1, D), F32)],
        compiler_params=_params(("arbitrary",), 48 << 20),
    )(*args, xh, rs, g)


def axpby(a, b, c1, name):
    S, D = a.shape
    tr = min(ROW_TILE, S)

    def body(a_ref, b_ref, o_ref):
        o_ref[...] = c1 * a_ref[...] + b_ref[...]

    row = pl.BlockSpec((tr, D), lambda i: (i, 0))
    return pl.pallas_call(body, name=name, grid=(S // tr,), in_specs=[row, row], out_specs=row,
                          out_shape=jax.ShapeDtypeStruct((S, D), F32), compiler_params=_params(("parallel",)))(a, b)


def loss_head(y, t, name):
    S, D = y.shape
    tr = min(ROW_TILE, S)
    nsteps = S // tr

    def body(y_ref, t_ref, dy_ref, l_ref, acc_ref):
        i = pl.program_id(0)

        @pl.when(i == 0)
        def _():
            acc_ref[...] = jnp.zeros_like(acc_ref)

        e = y_ref[...] - t_ref[...]
        dy_ref[...] = e * (1.0 / D)
        acc_ref[...] += jnp.sum(e * e, axis=0, keepdims=True)

        @pl.when(i == nsteps - 1)
        def _():
            l_ref[...] = jnp.sum(acc_ref[...], axis=1, keepdims=True) * (0.5 / D)

    row = pl.BlockSpec((tr, D), lambda i: (i, 0))
    return pl.pallas_call(
        body, name=name, grid=(nsteps,), in_specs=[row, row],
        out_specs=[row, pl.BlockSpec((1, 1), lambda i: (0, 0))],
        out_shape=[jax.ShapeDtypeStruct((S, D), F32), jax.ShapeDtypeStruct((1, 1), F32)],
        scratch_shapes=[pltpu.VMEM((1, D), F32)], compiler_params=_params(("arbitrary",)),
    )(y, t)


def swiglu_fwd(gu, name):
    S, F2 = gu.shape
    Fh = F2 // 2
    tc = _divisors(Fh, LANES, 1536)[0]
    nb = Fh // tc
    tr = min(ROW_TILE, S)

    def body(g_ref, u_ref, o_ref):
        g = g_ref[...]
        o_ref[...] = g * _sigmoid(g) * u_ref[...]

    return pl.pallas_call(
        body, name=name, grid=(S // tr, nb),
        in_specs=[pl.BlockSpec((tr, tc), lambda i, j: (i, j)), pl.BlockSpec((tr, tc), lambda i, j: (i, nb + j))],
        out_specs=pl.BlockSpec((tr, tc), lambda i, j: (i, j)), out_shape=jax.ShapeDtypeStruct((S, Fh), F32),
        compiler_params=_params(("parallel", "parallel")),
    )(gu, gu)


def swiglu_bwd(gu, dact, name):
    S, F2 = gu.shape
    Fh = F2 // 2
    tc = _divisors(Fh, LANES, 1536)[0]
    nb = Fh // tc
    tr = min(ROW_TILE, S)

    def body(g_ref, u_ref, d_ref, o_ref):
        g, u, d = g_ref[...], u_ref[...], d_ref[...]
        s = _sigmoid(g)
        dgate = d * u * (s * (1.0 + g * (1.0 - s)))
        dup = d * (g * s)
        o_ref[...] = jnp.where(pl.program_id(1) < nb, dgate, dup)

    return pl.pallas_call(
        body, name=name, grid=(S // tr, 2 * nb),
        in_specs=[pl.BlockSpec((tr, tc), lambda i, j: (i, j % nb)), pl.BlockSpec((tr, tc), lambda i, j: (i, nb + j % nb)),
                  pl.BlockSpec((tr, tc), lambda i, j: (i, j % nb))],
        out_specs=pl.BlockSpec((tr, tc), lambda i, j: (i, j)), out_shape=jax.ShapeDtypeStruct((S, F2), F32),
        compiler_params=_params(("parallel", "parallel")),
    )(gu, gu, dact)


GATE_COLS = 256


def merge_fwd(proj, pr, pa, name):
    S, D = pr.shape
    tr = min(GATE_ROWS, S)
    c0 = (3 * D + 2 * N_KV_HEADS * HEAD_DIM) // GATE_COLS
    c1 = c0 + D // GATE_COLS

    def body(gr_ref, ga_ref, pr_ref, pa_ref, o_ref):
        o_ref[...] = _sigmoid(gr_ref[...]) * pr_ref[...] + _sigmoid(ga_ref[...]) * pa_ref[...]

    blk = pl.BlockSpec((tr, GATE_COLS), lambda i, j: (i, j))
    return pl.pallas_call(
        body, name=name, grid=(S // tr, D // GATE_COLS),
        in_specs=[pl.BlockSpec((tr, GATE_COLS), lambda i, j: (i, c0 + j)), pl.BlockSpec((tr, GATE_COLS), lambda i, j: (i, c1 + j)),
                  blk, blk],
        out_specs=blk, out_shape=jax.ShapeDtypeStruct((S, D), F32), compiler_params=_params(("parallel", "parallel")),
    )(proj, proj, pr, pa)


def merge_bwd(proj, pr, pa, dm, name):
    S, D = pr.shape
    tr = min(GATE_ROWS, S)
    c0 = (3 * D + 2 * N_KV_HEADS * HEAD_DIM) // GATE_COLS
    c1 = c0 + D // GATE_COLS

    def body(gr_ref, ga_ref, pr_ref, pa_ref, dm_ref, dpr_ref, dpa_ref, dgr_ref, dga_ref):
        sr, sa, d = _sigmoid(gr_ref[...]), _sigmoid(ga_ref[...]), dm_ref[...]
        dpr_ref[...] = d * sr
        dpa_ref[...] = d * sa
        dgr_ref[...] = d * pr_ref[...] * (sr * (1.0 - sr))
        dga_ref[...] = d * pa_ref[...] * (sa * (1.0 - sa))

    blk = pl.BlockSpec((tr, GATE_COLS), lambda i, j: (i, j))
    sds = jax.ShapeDtypeStruct((S, D), F32)
    return pl.pallas_call(
        body, name=name, grid=(S // tr, D // GATE_COLS),
        in_specs=[pl.BlockSpec((tr, GATE_COLS), lambda i, j: (i, c0 + j)), pl.BlockSpec((tr, GATE_COLS), lambda i, j: (i, c1 + j)),
                  blk, blk, blk],
        out_specs=[blk, blk, blk, blk], out_shape=[sds, sds, sds, sds], compiler_params=_params(("parallel", "parallel")),
    )(proj, proj, pr, pa, dm)


RG_ROWS = 512


def _shift_down(cur, prev, d, row, first):
    halo = jnp.where(first, 0.0, pltpu.roll(prev, d, 0))
    return jnp.where(row >= d, pltpu.roll(cur, d, 0), halo)


def _shift_up(cur, nxt, d, row, last, tr):
    halo = jnp.where(last, 0.0, pltpu.roll(nxt, tr - d, 0))
    return jnp.where(row < tr - d, pltpu.roll(cur, tr - d, 0), halo)


def _lru_coeffs(r, lam):
    sp = _softplus_neg(lam)
    la = -LRU_C * r * sp
    return sp, la, jnp.exp(la), _neg_expm1(2.0 * la)


def rg_gates_fwd(proj, conv_w, conv_b, w_rg, b_rg, w_ig, b_ig, lam, name):
    S = proj.shape[0]
    nblk, bw, _ = w_rg.shape
    D = nblk * bw
    tr = min(RG_ROWS, S)

    def body(xr_ref, xp_ref, cw_ref, cb_ref, wr_ref, br_ref, wi_ref, bi_ref, lam_ref, xc_ref, r_ref, i_ref, a_ref, b_ref):
        first = pl.program_id(1) == 0
        cur, prev = xr_ref[...], xp_ref[...]
        row = lax.broadcasted_iota(jnp.int32, cur.shape, 0)
        xc = cb_ref[...]
        for k in range(CONV_WIDTH - 1):
            xc = xc + _shift_down(cur, prev, CONV_WIDTH - 1 - k, row, first) * cw_ref[k:k + 1, :]
        xc = xc + cur * cw_ref[CONV_WIDTH - 1:CONV_WIDTH, :]
        xm = xc.astype(MXU_DTYPE)
        r = _sigmoid(jnp.dot(xm, wr_ref[...].astype(MXU_DTYPE), preferred_element_type=F32) + br_ref[...])
        ig = _sigmoid(jnp.dot(xm, wi_ref[...].astype(MXU_DTYPE), preferred_element_type=F32) + bi_ref[...])
        _, _, a, em = _lru_coeffs(r, lam_ref[...])
        xc_ref[...] = xc
        r_ref[...] = r
        i_ref[...] = ig
        a_ref[...] = a
        b_ref[...] = jnp.sqrt(em) * (ig * xc)

    tile = pl.BlockSpec((tr, bw), lambda n, i: (i, n))
    vec = pl.BlockSpec((1, bw), lambda n, i: (0, n))
    wblk = pl.BlockSpec((None, bw, bw), lambda n, i: (n, 0, 0))
    sds = jax.ShapeDtypeStruct((S, D), F32)
    return pl.pallas_call(
        body, name=name, grid=(nblk, S // tr),
        in_specs=[tile, pl.BlockSpec((tr, bw), lambda n, i: (jnp.maximum(i - 1, 0), n)),
                  pl.BlockSpec((CONV_WIDTH, bw), lambda n, i: (0, n)), vec, wblk, vec, wblk, vec, vec],
        out_specs=[tile] * 5, out_shape=[sds] * 5, compiler_params=_params(("parallel", "parallel")),
    )(proj, proj, conv_w, conv_b, w_rg, b_rg, w_ig, b_ig, lam)


SCAN_COLS = 256
CHUNK = SUBLANES
SCAN_UNROLL = 4


def rg_scan_fwd(proj, a, b, name):
    S, D = a.shape
    cb = min(SCAN_COLS, D)
    goff = D // cb

    def body(a_ref, b_ref, g_ref, hs_ref, y_ref):
        row = lax.broadcasted_iota(jnp.int32, (CHUNK, cb), 0)

        def step(c, carry):
            r0 = pl.multiple_of(c * CHUNK, CHUNK)
            A = a_ref[pl.ds(r0, CHUNK), :]
            B = b_ref[pl.ds(r0, CHUNK), :]
            for d in (1, 2, 4):
                As = jnp.where(row >= d, pltpu.roll(A, d, 0), 1.0)
                Bs = jnp.where(row >= d, pltpu.roll(B, d, 0), 0.0)
                B = A * Bs + B
                A = A * As
            H = B + A * carry
            hs_ref[pl.ds(r0, CHUNK), :] = H
            return jnp.sum(jnp.where(row == CHUNK - 1, H, 0.0), axis=0, keepdims=True)

        lax.fori_loop(0, S // CHUNK, step, jnp.zeros((1, cb), F32), unroll=SCAN_UNROLL)
        y_ref[...] = hs_ref[...] * _gelu(g_ref[...])

    col = pl.BlockSpec((S, cb), lambda j: (0, j))
    sds = jax.ShapeDtypeStruct((S, D), F32)
    return pl.pallas_call(
        body, name=name, grid=(D // cb,), in_specs=[col, col, pl.BlockSpec((S, cb), lambda j: (0, goff + j))],
        out_specs=[col, col], out_shape=[sds, sds], compiler_params=_params(("parallel",), _vmem_limit(5 * S * cb * 4, 4 * S * cb * 4)),
    )(a, b, proj)


def rg_scan_bwd(proj, dy, hs, a, name):
    S, D = a.shape
    cb = min(SCAN_COLS, D)
    goff = D // cb
    nchunks = S // CHUNK

    def body(g_ref, dy_ref, hs_ref, a_ref, dg_ref, gt_ref):
        gate, dy = g_ref[...], dy_ref[...]
        dg_ref[...] = dy * hs_ref[...] * _gelu_grad(gate)
        gt_ref[...] = dy * _gelu(gate)
        row = lax.broadcasted_iota(jnp.int32, (CHUNK, cb), 0)

        def step(k, carry):
            c = nchunks - 1 - k
            r0 = pl.multiple_of(c * CHUNK, CHUNK)
            rn = pl.multiple_of(jnp.minimum(c + 1, nchunks - 1) * CHUNK, CHUNK)
            last = c == nchunks - 1
            nxt = jnp.where(last, 0.0, pltpu.roll(a_ref[pl.ds(rn, CHUNK), :], CHUNK - 1, 0))
            A = jnp.where(row < CHUNK - 1, pltpu.roll(a_ref[pl.ds(r0, CHUNK), :], CHUNK - 1, 0), nxt)
            B = gt_ref[pl.ds(r0, CHUNK), :]
            for d in (1, 2, 4):
                As = jnp.where(row < CHUNK - d, pltpu.roll(A, CHUNK - d, 0), 1.0)
                Bs = jnp.where(row < CHUNK - d, pltpu.roll(B, CHUNK - d, 0), 0.0)
                B = A * Bs + B
                A = A * As
            G = B + A * carry
            gt_ref[pl.ds(r0, CHUNK), :] = G
            return jnp.sum(jnp.where(row == 0, G, 0.0), axis=0, keepdims=True)

        lax.fori_loop(0, nchunks, step, jnp.zeros((1, cb), F32), unroll=SCAN_UNROLL)

    col = pl.BlockSpec((S, cb), lambda j: (0, j))
    sds = jax.ShapeDtypeStruct((S, D), F32)
    return pl.pallas_call(
        body, name=name, grid=(D // cb,), in_specs=[pl.BlockSpec((S, cb), lambda j: (0, goff + j)), col, col, col],
        out_specs=[col, col], out_shape=[sds, sds], compiler_params=_params(("parallel",), _vmem_limit(6 * S * cb * 4, 6 * S * cb * 4)),
    )(proj, dy, hs, a)


def rg_gates_bwd(gt, hs, xc, r, ig, w_rg, w_ig, lam, name):
    S, D = xc.shape
    nblk, bw, _ = w_rg.shape
    tr = min(RG_ROWS, S)

    def body(gt_ref, hs_ref, hp_ref, xc_ref, r_ref, i_ref, wr_ref, wi_ref, lam_ref,
             dxc_ref, dwr_ref, dwi_ref, dbr_ref, dbi_ref, dl_ref):
        step = pl.program_id(1)
        g, hs, xc, r, ig, lam = gt_ref[...], hs_ref[...], xc_ref[...], r_ref[...], i_ref[...], lam_ref[...]
        row = lax.broadcasted_iota(jnp.int32, g.shape, 0)
        hprev = _shift_down(hs, hp_ref[...], 1, row, step == 0)
        sp, _, a, em = _lru_coeffs(r, lam)
        mult = jnp.sqrt(em)
        du = g * mult
        dla = g * hprev * a - (g * (ig * xc)) * (a * a) / mult
        dpr = (dla * (-LRU_C * sp)) * (r * (1.0 - r))
        dpi = (du * xc) * (ig * (1.0 - ig))
        dprm, dpim = dpr.astype(MXU_DTYPE), dpi.astype(MXU_DTYPE)
        nt = (((1,), (1,)), ((), ()))
        dxc_ref[...] = (du * ig + lax.dot_general(dprm, wr_ref[...].astype(MXU_DTYPE), nt, preferred_element_type=F32)
                        + lax.dot_general(dpim, wi_ref[...].astype(MXU_DTYPE), nt, preferred_element_type=F32))

        @pl.when(step == 0)
        def _():
            for ref in (dwr_ref, dwi_ref, dbr_ref, dbi_ref, dl_ref):
                ref[...] = jnp.zeros_like(ref)

        xct = xc.T.astype(MXU_DTYPE)
        dwr_ref[...] += jnp.dot(xct, dprm, preferred_element_type=F32)
        dwi_ref[...] += jnp.dot(xct, dpim, preferred_element_type=F32)
        dbr_ref[...] += jnp.sum(dpr, axis=0, keepdims=True)
        dbi_ref[...] += jnp.sum(dpi, axis=0, keepdims=True)
        dl_ref[...] += jnp.sum(dla * (-LRU_C * r), axis=0, keepdims=True) * (-_sigmoid(-lam))

    tile = pl.BlockSpec((tr, bw), lambda n, i: (i, n))
    vec = pl.BlockSpec((1, bw), lambda n, i: (0, n))
    wblk = pl.BlockSpec((None, bw, bw), lambda n, i: (n, 0, 0))
    return pl.pallas_call(
        body, name=name, grid=(nblk, S // tr),
        in_specs=[tile, tile, pl.BlockSpec((tr, bw), lambda n, i: (jnp.maximum(i - 1, 0), n)), tile, tile, tile, wblk, wblk, vec],
        out_specs=[tile, wblk, wblk, vec, vec, vec],
        out_shape=[jax.ShapeDtypeStruct((S, D), F32), jax.ShapeDtypeStruct((nblk, bw, bw), F32), jax.ShapeDtypeStruct((nblk, bw, bw), F32),
                   jax.ShapeDtypeStruct((1, D), F32), jax.ShapeDtypeStruct((1, D), F32), jax.ShapeDtypeStruct((1, D), F32)],
        compiler_params=_params(("parallel", "arbitrary")),
    )(gt, hs, hs, xc, r, ig, w_rg, w_ig, lam)


def rg_conv_bwd(proj, dxc, conv_w, name):
    S, D = dxc.shape
    bw = min(SCAN_COLS, D)
    tr = min(RG_ROWS, S)
    nsteps = S // tr

    def body(d_ref, dn_ref, xr_ref, xp_ref, cw_ref, dxr_ref, dcw_ref, dcb_ref):
        step = pl.program_id(1)
        d, xr = d_ref[...], xr_ref[...]
        row = lax.broadcasted_iota(jnp.int32, d.shape, 0)
        dxr = d * cw_ref[CONV_WIDTH - 1:CONV_WIDTH, :]
        for k in range(CONV_WIDTH - 1):
            dxr = dxr + _shift_up(d, dn_ref[...], CONV_WIDTH - 1 - k, row, step == nsteps - 1, tr) * cw_ref[k:k + 1, :]
        dxr_ref[...] = dxr

        @pl.when(step == 0)
        def _():
            dcw_ref[...] = jnp.zeros_like(dcw_ref)
            dcb_ref[...] = jnp.zeros_like(dcb_ref)

        for k in range(CONV_WIDTH - 1):
            xs = _shift_down(xr, xp_ref[...], CONV_WIDTH - 1 - k, row, step == 0)
            dcw_ref[k:k + 1, :] += jnp.sum(d * xs, axis=0, keepdims=True)
        dcw_ref[CONV_WIDTH - 1:CONV_WIDTH, :] += jnp.sum(d * xr, axis=0, keepdims=True)
        dcb_ref[...] += jnp.sum(d, axis=0, keepdims=True)

    tile = pl.BlockSpec((tr, bw), lambda n, i: (i, n))
    cwb = pl.BlockSpec((CONV_WIDTH, bw), lambda n, i: (0, n))
    return pl.pallas_call(
        body, name=name, grid=(D // bw, nsteps),
        in_specs=[tile, pl.BlockSpec((tr, bw), lambda n, i: (jnp.minimum(i + 1, nsteps - 1), n)), tile,
                  pl.BlockSpec((tr, bw), lambda n, i: (jnp.maximum(i - 1, 0), n)), cwb],
        out_specs=[tile, cwb, pl.BlockSpec((1, bw), lambda n, i: (0, n))],
        out_shape=[jax.ShapeDtypeStruct((S, D), F32), jax.ShapeDtypeStruct((CONV_WIDTH, D), F32), jax.ShapeDtypeStruct((1, D), F32)],
        compiler_params=_params(("parallel", "arbitrary")),
    )(dxc, dxc, proj, proj, conv_w)


def rope_table(S):
    half = ROT_DIM // 2
    pos = jnp.arange(S, dtype=F32)
    inv = ROPE_THETA ** (-jnp.arange(0, ROT_DIM, 2, dtype=F32) / ROT_DIM)
    ang = pos[:, None] * inv[None, :]
    cos, sin = jnp.cos(ang), jnp.sin(ang)
    zero = jnp.zeros((S, HEAD_DIM - ROT_DIM), F32)
    c = jnp.concatenate([cos, cos, zero + 1.0], axis=1)
    a = jnp.concatenate([-sin, jnp.zeros((S, half), F32), zero], axis=1)
    b = jnp.concatenate([jnp.zeros((S, half), F32), sin, zero], axis=1)
    return jnp.stack([jnp.tile(t, (1, LANES // HEAD_DIM)) for t in (c, a, b)])


def _rope(t, tab):
    half = ROT_DIM // 2
    return t * tab[0] + pltpu.roll(t, LANES - half, 1) * tab[1] + pltpu.roll(t, half, 1) * tab[2]


def _rope_t(d, tab):
    half = ROT_DIM // 2
    return d * tab[0] + pltpu.roll(d * tab[1], half, 1) + pltpu.roll(d * tab[2], LANES - half, 1)


def _dup_head(t, hk, lo):
    sw = pltpu.roll(t, HEAD_DIM, 1)
    return jnp.where(lo, t, sw) if hk == 0 else jnp.where(lo, sw, t)


def _attn_common(n, sink_ref, q_ref, kp_ref, kc_ref, vp_ref, vc_ref, tc_ref, tp_ref, hk, pairs):
    tq = (tc_ref[0], tc_ref[1], tc_ref[2])
    tp = (tp_ref[0], tp_ref[1], tp_ref[2])
    lo = lax.broadcasted_iota(jnp.int32, (WINDOW, LANES), 1) < HEAD_DIM
    lo2 = lax.broadcasted_iota(jnp.int32, (2 * WINDOW, LANES), 1) < HEAD_DIM
    kband = jnp.concatenate([_rope(kp_ref[...], tp), _rope(kc_ref[...], tq)], axis=0)
    vband = jnp.concatenate([vp_ref[...], vc_ref[...]], axis=0)
    kd = _dup_head(kband, hk, lo2).astype(MXU_DTYPE)
    vd = _dup_head(vband, hk, lo2).astype(MXU_DTYPE)
    rows, sks = [], []
    for j in range(pairs):
        col = hk * pairs + j
        qp = _rope(q_ref[:, col * LANES:(col + 1) * LANES], tq)
        rows += [jnp.where(lo, qp, 0.0), jnp.where(lo, 0.0, qp)]
        sks += [jnp.full((WINDOW, 1), sink_ref[2 * col], F32), jnp.full((WINDOW, 1), sink_ref[2 * col + 1], F32)]
    qg = jnp.concatenate(rows, axis=0)
    sk = jnp.concatenate(sks, axis=0)
    G = 2 * pairs * WINDOW
    ri = lax.broadcasted_iota(jnp.int32, (G, 2 * WINDOW), 0) & (WINDOW - 1)
    kj = lax.broadcasted_iota(jnp.int32, (G, 2 * WINDOW), 1) - WINDOW
    valid = (kj <= ri) & (kj > ri - WINDOW) & (kj + n * WINDOW >= 0)
    s = lax.dot_general(qg.astype(MXU_DTYPE), kd, (((1,), (1,)), ((), ())), preferred_element_type=F32) * (HEAD_DIM ** -0.5)
    s = jnp.where(valid, s, NEG_INF)
    m = jnp.maximum(jnp.max(s, axis=1, keepdims=True), sk)
    e = jnp.exp(s - m)
    es = jnp.exp(sk - m)
    inv = 1.0 / (jnp.sum(e, axis=1, keepdims=True) + es)
    return qg, kd, vd, e * inv, es * inv, lo, lo2, tq, tp


def _attn_specs(D, NB):
    kcol = 3 * D // LANES
    q = pl.BlockSpec((WINDOW, D), lambda n: (n, 2))
    kc = pl.BlockSpec((WINDOW, LANES), lambda n: (n, kcol))
    kp = pl.BlockSpec((WINDOW, LANES), lambda n: (jnp.maximum(n - 1, 0), kcol))
    vc = pl.BlockSpec((WINDOW, LANES), lambda n: (n, kcol + 1))
    vp = pl.BlockSpec((WINDOW, LANES), lambda n: (jnp.maximum(n - 1, 0), kcol + 1))
    tc = pl.BlockSpec((3, WINDOW, LANES), lambda n: (0, n, 0))
    tp = pl.BlockSpec((3, WINDOW, LANES), lambda n: (0, jnp.maximum(n - 1, 0), 0))
    sink = pl.BlockSpec(memory_space=pltpu.SMEM)
    return [sink, q, kp, kc, vp, vc, tc, tp]


def attn_fwd(proj, sinks, tab, D, name):
    S = proj.shape[0]
    NB = S // WINDOW
    pairs = D // HEAD_DIM // N_KV_HEADS // 2

    def body(sink_ref, q_ref, kp_ref, kc_ref, vp_ref, vc_ref, tc_ref, tp_ref, o_ref):
        n = pl.program_id(0)
        for hk in range(N_KV_HEADS):
            _, _, vd, p, _, lo, _, _, _ = _attn_common(n, sink_ref, q_ref, kp_ref, kc_ref, vp_ref, vc_ref, tc_ref, tp_ref, hk, pairs)
            o = jnp.dot(p.astype(MXU_DTYPE), vd, preferred_element_type=F32)
            for j in range(pairs):
                col = hk * pairs + j
                oa = o[(2 * j) * WINDOW:(2 * j + 1) * WINDOW]
                ob = o[(2 * j + 1) * WINDOW:(2 * j + 2) * WINDOW]
                o_ref[:, col * LANES:(col + 1) * LANES] = jnp.where(lo, oa, ob)

    return pl.pallas_call(
        body, name=name, grid=(NB,), in_specs=_attn_specs(D, NB),
        out_specs=pl.BlockSpec((WINDOW, D), lambda n: (n, 0)), out_shape=jax.ShapeDtypeStruct((S, D), F32),
        compiler_params=_params(("parallel",)),
    )(sinks, proj, proj, proj, proj, proj, tab, tab)


def attn_bwd(proj, sinks, tab, o, do, D, name):
    S = proj.shape[0]
    NB = S // WINDOW
    pairs = D // HEAD_DIM // N_KV_HEADS // 2

    def body(sink_ref, q_ref, kp_ref, kc_ref, vp_ref, vc_ref, tc_ref, tp_ref, o_ref, do_ref, dq_ref, dk_ref, dv_ref, ds_ref):
        n = pl.program_id(0)

        @pl.when(n == 0)
        def _():
            ds_ref[...] = jnp.zeros_like(ds_ref)

        lane1 = lax.broadcasted_iota(jnp.int32, (1, LANES), 1)
        dsink = jnp.zeros((1, LANES), F32)
        dkt = dvt = None
        for hk in range(N_KV_HEADS):
            qg, kd, vd, p, ps, lo, lo2, tq, tp = _attn_common(n, sink_ref, q_ref, kp_ref, kc_ref, vp_ref, vc_ref, tc_ref, tp_ref, hk, pairs)
            dos, os_ = [], []
            for j in range(pairs):
                col = hk * pairs + j
                dop = do_ref[:, col * LANES:(col + 1) * LANES]
                op = o_ref[:, col * LANES:(col + 1) * LANES]
                dos += [jnp.where(lo, dop, 0.0), jnp.where(lo, 0.0, dop)]
                os_ += [jnp.where(lo, op, 0.0), jnp.where(lo, 0.0, op)]
            dog = jnp.concatenate(dos, axis=0)
            og = jnp.concatenate(os_, axis=0)
            dogm = dog.astype(MXU_DTYPE)
            dp = lax.dot_general(dogm, vd, (((1,), (1,)), ((), ())), preferred_element_type=F32)
            dr = jnp.sum(dog * og, axis=1, keepdims=True)
            ds = p * (dp - dr) * (HEAD_DIM ** -0.5)
            dsm = ds.astype(MXU_DTYPE)
            dqg = jnp.dot(dsm, kd, preferred_element_type=F32)
            dkd = jnp.dot(ds.T.astype(MXU_DTYPE), qg.astype(MXU_DTYPE), preferred_element_type=F32)
            dvd = jnp.dot(p.T.astype(MXU_DTYPE), dogm, preferred_element_type=F32)
            dkf = dkd + pltpu.roll(dkd, HEAD_DIM, 1)
            dvf = dvd + pltpu.roll(dvd, HEAD_DIM, 1)
            if hk == 0:
                dkt, dvt = dkf, dvf
            else:
                dkt, dvt = jnp.where(lo2, dkt, dkf), jnp.where(lo2, dvt, dvf)
            sd = ps * dr
            for j in range(pairs):
                col = hk * pairs + j
                dqa = dqg[(2 * j) * WINDOW:(2 * j + 1) * WINDOW]
                dqb = dqg[(2 * j + 1) * WINDOW:(2 * j + 2) * WINDOW]
                dq_ref[:, col * LANES:(col + 1) * LANES] = _rope_t(jnp.where(lo, dqa, dqb), tq)
                for t in range(2):
                    part = sd[(2 * j + t) * WINDOW:(2 * j + t + 1) * WINDOW]
                    val = jnp.sum(part, axis=0, keepdims=True)
                    dsink = dsink - jnp.where(lane1 == 2 * col + t, val, 0.0)
        dk_ref[...] = jnp.concatenate([_rope_t(dkt[:WINDOW], tp), _rope_t(dkt[WINDOW:], tq)], axis=0)
        dv_ref[...] = dvt
        ds_ref[...] += dsink

    blk = pl.BlockSpec((WINDOW, D), lambda n: (n, 0))
    band = pl.BlockSpec((None, 2 * WINDOW, LANES), lambda n: (n, 0, 0))
    return pl.pallas_call(
        body, name=name, grid=(NB,), in_specs=_attn_specs(D, NB) + [blk, blk],
        out_specs=[blk, band, band, pl.BlockSpec((1, LANES), lambda n: (0, 0))],
        out_shape=[jax.ShapeDtypeStruct((S, D), F32), jax.ShapeDtypeStruct((NB, 2 * WINDOW, LANES), F32),
                   jax.ShapeDtypeStruct((NB, 2 * WINDOW, LANES), F32), jax.ShapeDtypeStruct((1, LANES), F32)],
        compiler_params=_params(("arbitrary",)),
    )(sinks, proj, proj, proj, proj, proj, tab, tab, o, do)


def band_fold(dkb, dvb, name):
    NB = dkb.shape[0]
    k4 = dkb.reshape(NB, 2, WINDOW, LANES)
    v4 = dvb.reshape(NB, 2, WINDOW, LANES)

    def body(kc_ref, kn_ref, vc_ref, vn_ref, dk_ref, dv_ref):
        more = pl.program_id(0) < NB - 1
        dk_ref[...] = kc_ref[...] + jnp.where(more, kn_ref[...], 0.0)
        dv_ref[...] = vc_ref[...] + jnp.where(more, vn_ref[...], 0.0)

    cur = pl.BlockSpec((None, None, WINDOW, LANES), lambda n: (n, 1, 0, 0))
    nxt = pl.BlockSpec((None, None, WINDOW, LANES), lambda n: (jnp.minimum(n + 1, NB - 1), 0, 0, 0))
    out = pl.BlockSpec((WINDOW, LANES), lambda n: (n, 0))
    sds = jax.ShapeDtypeStruct((NB * WINDOW, LANES), F32)
    return pl.pallas_call(body, name=name, grid=(NB,), in_specs=[cur, nxt, cur, nxt], out_specs=[out, out], out_shape=[sds, sds],
                          compiler_params=_params(("parallel",)))(k4, k4, v4, v4)


CROSS_ROWS = 512


def _cross_probs(q, k, scale):
    s = lax.dot_general(q.astype(MXU_DTYPE), k.astype(MXU_DTYPE), (((1,), (1,)), ((), ())), preferred_element_type=F32) * scale
    e = jnp.exp(s - jnp.max(s, axis=1, keepdims=True))
    return e / jnp.sum(e, axis=1, keepdims=True)


def cross_fwd(qc, kv, name):
    S, D = qc.shape
    M = kv.shape[0]
    hd = D // CROSS_HEADS
    tq = min(CROSS_ROWS, S)

    def body(q_ref, kv_ref, o_ref):
        for h in range(CROSS_HEADS):
            p = _cross_probs(q_ref[:, h * hd:(h + 1) * hd], kv_ref[:, h * hd:(h + 1) * hd], hd ** -0.5)
            v = kv_ref[:, D + h * hd:D + (h + 1) * hd].astype(MXU_DTYPE)
            o_ref[:, h * hd:(h + 1) * hd] = jnp.dot(p.astype(MXU_DTYPE), v, preferred_element_type=F32)

    return pl.pallas_call(
        body, name=name, grid=(S // tq,), in_specs=[pl.BlockSpec((tq, D), lambda i: (i, 0)), pl.BlockSpec((M, 2 * D), lambda i: (0, 0))],
        out_specs=pl.BlockSpec((tq, D), lambda i: (i, 0)), out_shape=jax.ShapeDtypeStruct((S, D), F32),
        compiler_params=_params(("parallel",)),
    )(qc, kv)


def cross_bwd(qc, kv, do, name):
    S, D = qc.shape
    M = kv.shape[0]
    hd = D // CROSS_HEADS
    tq = min(CROSS_ROWS, S)

    def body(q_ref, kv_ref, do_ref, dq_ref, dkv_ref):
        @pl.when(pl.program_id(0) == 0)
        def _():
            dkv_ref[...] = jnp.zeros_like(dkv_ref)

        for h in range(CROSS_HEADS):
            q = q_ref[:, h * hd:(h + 1) * hd]
            k = kv_ref[:, h * hd:(h + 1) * hd]
            v = kv_ref[:, D + h * hd:D + (h + 1) * hd].astype(MXU_DTYPE)
            dom = do_ref[:, h * hd:(h + 1) * hd].astype(MXU_DTYPE)
            p = _cross_probs(q, k, hd ** -0.5)
            dp = lax.dot_general(dom, v, (((1,), (1,)), ((), ())), preferred_element_type=F32)
            ds = p * (dp - jnp.sum(p * dp, axis=1, keepdims=True)) * (hd ** -0.5)
            dq_ref[:, h * hd:(h + 1) * hd] = jnp.dot(ds.astype(MXU_DTYPE), k.astype(MXU_DTYPE), preferred_element_type=F32)
            dkv_ref[:, h * hd:(h + 1) * hd] += jnp.dot(ds.T.astype(MXU_DTYPE), q.astype(MXU_DTYPE), preferred_element_type=F32)
            dkv_ref[:, D + h * hd:D + (h + 1) * hd] += jnp.dot(p.T.astype(MXU_DTYPE), dom, preferred_element_type=F32)

    row = pl.BlockSpec((tq, D), lambda i: (i, 0))
    full = pl.BlockSpec((M, 2 * D), lambda i: (0, 0))
    return pl.pallas_call(
        body, name=name, grid=(S // tq,), in_specs=[row, full, row], out_specs=[row, full],
        out_shape=[jax.ShapeDtypeStruct((S, D), F32), jax.ShapeDtypeStruct((M, 2 * D), F32)],
        compiler_params=_params(("arbitrary",)),
    )(qc, kv, do)


def adamw(w, g, m, v, name):
    shape = w.shape
    cols = shape[-1]
    rows = int(np.prod(shape[:-1]))
    w2, g2, m2, v2 = (t.reshape(rows, cols) for t in (w, g, m, v))
    tr = _divisors(rows, SUBLANES, max(SUBLANES, (1 << 20) // (cols * 4) // SUBLANES * SUBLANES))[0]

    def body(w_ref, g_ref, m_ref, v_ref, d_ref, mo_ref, vo_ref, go_ref):
        gg = g_ref[...]
        mn = ADAM_B1 * m_ref[...] + (1.0 - ADAM_B1) * gg
        vn = ADAM_B2 * v_ref[...] + (1.0 - ADAM_B2) * (gg * gg)
        m_hat = mn / (1.0 - ADAM_B1 ** ADAM_STEP)
        v_hat = vn / (1.0 - ADAM_B2 ** ADAM_STEP)
        d_ref[...] = -ADAM_LR * (m_hat / (jnp.sqrt(v_hat) + ADAM_EPS) + ADAM_WD * w_ref[...])
        mo_ref[...] = mn
        vo_ref[...] = vn
        go_ref[...] = gg

    blk = pl.BlockSpec((tr, cols), lambda i: (i, 0))
    sds = jax.ShapeDtypeStruct((rows, cols), F32)
    d, mn, vn, go = pl.pallas_call(body, name=name, grid=(rows // tr,), in_specs=[blk] * 4, out_specs=[blk] * 4, out_shape=[sds] * 4,
                                   compiler_params=_params(("parallel",)))(w2, g2, m2, v2)
    return d.reshape(shape), mn.reshape(shape), vn.reshape(shape), go.reshape(shape)


def sum_devices(parts, name):
    n, rows, cols = parts.shape

    def body(p_ref, o_ref):
        acc = p_ref[0]
        for k in range(1, n):
            acc = acc + p_ref[k]
        o_ref[...] = acc

    return pl.pallas_call(body, name=name, in_specs=[pl.BlockSpec(memory_space=pltpu.VMEM)],
                          out_specs=pl.BlockSpec(memory_space=pltpu.VMEM), out_shape=jax.ShapeDtypeStruct((rows, cols), F32))(parts)


HBM_SPEC = pl.BlockSpec(memory_space=pltpu.HBM)


def _place():
    return lax.axis_index("x"), lax.axis_index("y"), lax.axis_index("c")


def _remote(src, dst, send_sems, recv_sems, k, to):
    return pltpu.make_async_remote_copy(src_ref=src, dst_ref=dst, send_sem=send_sems.at[k], recv_sem=recv_sems.at[k],
                                        device_id=to, device_id_type=MESH_ID)


def gather_weights(shards, name):
    n = len(shards)
    lh = shards[0].shape[0] // 2

    def gathered_shape(s):
        return s.shape[:-2] + (N_CHIPS,) + s.shape[-2:]

    def body(*refs):
        w_refs, out_refs = refs[:n], refs[n:2 * n]
        send_sems, recv_sems, pass_send, pass_recv = refs[2 * n:]
        x, y, c = _place()
        sibling = (x, y, 1 - c)
        chips = [(1 - x, y), (x, 1 - y), (1 - x, 1 - y)]
        mine = pl.ds(c * lh, lh)
        theirs = pl.ds((1 - c) * lh, lh)
        every = pl.ds(0, 2 * lh)

        def slab(a, px, py, layers):
            ref, slot = out_refs[a], 2 * px + py
            return ref.at[layers, slot] if len(ref.shape) == 4 else ref.at[layers, :, slot]

        own = [_remote(w_refs[a], slab(a, x, y, every), pass_send, pass_recv, 3 * n + a, sibling) for a in range(n)]
        first = [_remote(w_refs[a].at[mine], slab(a, x, y, mine), send_sems, recv_sems, 3 * a + k, (*chip, c))
                 for a in range(n) for k, chip in enumerate(chips)]
        for cp in own + first:
            cp.start()
        passed = []
        for k, chip in enumerate(chips):
            for a in range(n):
                landed = slab(a, *chip, mine)
                _remote(landed, landed, send_sems, recv_sems, 3 * a + k, (*chip, c)).wait_recv()
                cp = _remote(landed, landed, pass_send, pass_recv, 3 * a + k, sibling)
                cp.start()
                passed.append(cp)
        for k, chip in enumerate(chips):
            for a in range(n):
                landed = slab(a, *chip, theirs)
                _remote(landed, landed, pass_send, pass_recv, 3 * a + k, sibling).wait_recv()
        for cp in own:
            cp.wait()
        for cp in first + passed:
            cp.wait_send()

    return pl.pallas_call(
        body, name=name, in_specs=[HBM_SPEC] * n, out_specs=[HBM_SPEC] * n,
        out_shape=[jax.ShapeDtypeStruct(gathered_shape(s), s.dtype) for s in shards],
        scratch_shapes=[pltpu.SemaphoreType.DMA((3 * n,))] * 2 + [pltpu.SemaphoreType.DMA((4 * n,))] * 2,
    )(*shards)


def swap_sibling(parts, name):
    n = len(parts)

    def body(*refs):
        v_refs, out_refs, send_sems, recv_sems = refs[:n], refs[n:2 * n], refs[2 * n], refs[2 * n + 1]
        x, y, c = _place()
        cps = []
        for a in range(n):
            hr = v_refs[a].shape[2] // 2
            cps.append(_remote(v_refs[a].at[:, :, pl.ds((1 - c) * hr, hr)], out_refs[a], send_sems, recv_sems, a, (x, y, 1 - c)))
        for cp in cps:
            cp.start()
        for cp in cps:
            cp.wait()

    return pl.pallas_call(
        body, name=name, in_specs=[HBM_SPEC] * n, out_specs=[HBM_SPEC] * n,
        out_shape=[jax.ShapeDtypeStruct(v.shape[:2] + (v.shape[2] // 2, v.shape[3]), v.dtype) for v in parts],
        scratch_shapes=[pltpu.SemaphoreType.DMA((n,))] * 2,
    )(*parts)


def scatter_chips(parts, name):
    n = len(parts)

    def body(*refs):
        t_refs, out_refs, send_sems, recv_sems = refs[:n], refs[n:2 * n], refs[2 * n], refs[2 * n + 1]
        x, y, c = _place()
        chips = [(1 - x, y), (x, 1 - y), (1 - x, 1 - y)]
        sent = [_remote(t_refs[a].at[:, 2 * px + py], out_refs[a].at[:, k], send_sems, recv_sems, 3 * a + k, (px, py, c))
                for a in range(n) for k, (px, py) in enumerate(chips)]
        for cp in sent:
            cp.start()
        for a in range(n):
            for k, (px, py) in enumerate(chips):
                landed = out_refs[a].at[:, k]
                _remote(landed, landed, send_sems, recv_sems, 3 * a + k, (px, py, c)).wait_recv()
        for cp in sent:
            cp.wait_send()

    return pl.pallas_call(
        body, name=name, in_specs=[HBM_SPEC] * n, out_specs=[HBM_SPEC] * n,
        out_shape=[jax.ShapeDtypeStruct((t.shape[0], N_CHIPS - 1) + t.shape[2:], t.dtype) for t in parts],
        scratch_shapes=[pltpu.SemaphoreType.DMA((3 * n,))] * 2,
    )(*parts)


def join_halves(halves, name):
    n = len(halves)

    def body(*refs):
        out_refs, send_sems, recv_sems = refs[n:2 * n], refs[2 * n], refs[2 * n + 1]
        x, y, c = _place()
        cps = []
        for a in range(n):
            hr = out_refs[a].shape[1] // 2
            mine = out_refs[a].at[:, pl.ds(c * hr, hr)]
            cps.append(_remote(mine, mine, send_sems, recv_sems, a, (x, y, 1 - c)))
        for cp in cps:
            cp.start()
        for a in range(n):
            hr = out_refs[a].shape[1] // 2
            theirs = out_refs[a].at[:, pl.ds((1 - c) * hr, hr)]
            _remote(theirs, theirs, send_sems, recv_sems, a, (x, y, 1 - c)).wait_recv()
        for cp in cps:
            cp.wait_send()

    return pl.pallas_call(
        body, name=name, in_specs=[HBM_SPEC] * n, out_specs=[HBM_SPEC] * n,
        out_shape=[jax.ShapeDtypeStruct(f.shape, f.dtype) for f in halves], input_output_aliases={a: a for a in range(n)},
        scratch_shapes=[pltpu.SemaphoreType.DMA((n,))] * 2,
    )(*halves)


def gather_devices(v, name):
    def body(v_ref, out_ref, send_sems, recv_sems, local_sem):
        x, y, c = _place()
        me = 4 * x + 2 * y + c
        own = pltpu.make_async_copy(v_ref, out_ref.at[me], local_sem)
        own.start()
        peers = [((x + dx) % 2, (y + dy) % 2, (c + dc) % 2) for dx in (0, 1) for dy in (0, 1) for dc in (0, 1)][1:]
        sent = []
        for k, peer in enumerate(peers):
            cp = pltpu.make_async_remote_copy(src_ref=v_ref, dst_ref=out_ref.at[me], send_sem=send_sems.at[k], recv_sem=recv_sems.at[k],
                                              device_id=peer, device_id_type=MESH_ID)
            cp.start()
            sent.append(cp)
        for k, (px, py, pc) in enumerate(peers):
            slot = out_ref.at[4 * px + 2 * py + pc]
            pltpu.make_async_remote_copy(src_ref=slot, dst_ref=slot, send_sem=send_sems.at[k], recv_sem=recv_sems.at[k],
                                         device_id=(px, py, pc), device_id_type=MESH_ID).wait_recv()
        for cp in sent:
            cp.wait_send()
        own.wait()

    vm = pl.BlockSpec(memory_space=pltpu.VMEM)
    return pl.pallas_call(body, name=name, in_specs=[vm], out_specs=vm, out_shape=jax.ShapeDtypeStruct((N_DEV,) + v.shape, v.dtype),
                          scratch_shapes=[pltpu.SemaphoreType.DMA((N_DEV - 1,)), pltpu.SemaphoreType.DMA((N_DEV - 1,)),
                                          pltpu.SemaphoreType.DMA])(v)


ADD_ROWS = 512


def add_pair(place, a, b, name):
    L, n, hr, cols = b.shape
    tr = _divisors(hr, 2 * SUBLANES, ADD_ROWS)[0]
    nb = hr // tr

    def body(p_ref, a_ref, b_ref, o_ref):
        del p_ref
        o_ref[...] = (a_ref[...].astype(F32) + b_ref[...].astype(F32)).astype(o_ref.dtype)

    blk = pl.BlockSpec((None, None, tr, cols), lambda l, d, i, p: (l, d, i, 0))
    grid_spec = pltpu.PrefetchScalarGridSpec(
        num_scalar_prefetch=1, grid=(L, n, nb),
        in_specs=[pl.BlockSpec((None, None, tr, cols), lambda l, d, i, p: (l, d, p[0] * nb + i, 0)), blk], out_specs=blk)
    return pl.pallas_call(body, name=name, grid_spec=grid_spec, out_shape=jax.ShapeDtypeStruct(b.shape, b.dtype),
                          compiler_params=_params(("parallel", "parallel", "parallel")))(place, a, b)


def add_chips(place, own, others, name):
    L, n, hr, cols = others.shape
    tr = _divisors(hr, 2 * SUBLANES, ADD_ROWS)[0]
    nb = hr // tr

    def body(p_ref, own_ref, *refs):
        del p_ref
        acc = own_ref[...].astype(F32)
        for k in range(n):
            acc = acc + refs[k][...].astype(F32)
        refs[n][...] = acc

    ins = [pl.BlockSpec((None, None, tr, cols), lambda l, i, p: (l, p[1], i, 0))]
    ins += [pl.BlockSpec((None, None, tr, cols), functools.partial(lambda k, l, i, p: (l, k, i, 0), k)) for k in range(n)]
    grid_spec = pltpu.PrefetchScalarGridSpec(num_scalar_prefetch=1, grid=(L, nb), in_specs=ins,
                                             out_specs=pl.BlockSpec((None, tr, cols), lambda l, i, p: (l, p[0] * nb + i, 0)))
    return pl.pallas_call(body, name=name, grid_spec=grid_spec, out_shape=jax.ShapeDtypeStruct((L, 2 * hr, cols), F32),
                          compiler_params=_params(("parallel", "parallel")))(place, own, *([others] * n))


def _alpha(depth):
    return (2 * depth) ** 0.25


def _wmm(a, weight, mode, name):
    arr, how = weight
    return mm(a, arr, mode, name, **how)


def layer_fwd(h, mem, w, tab, alpha):
    D = h.shape[1]
    proj = _wmm(h, w["w_in"], "nn", "mm_proj")
    xc, r, ig, a, b = rg_gates_fwd(proj, w["conv_w"], w["conv_b"], w["w_rg"], w["b_rg"], w["w_ig"], w["b_ig"], w["lru_lambda"], "rg_gates_fwd")
    hs, y_rnn = rg_scan_fwd(proj, a, b, "rg_scan_fwd")
    y_attn = attn_fwd(proj, w["sinks"], tab, D, "attn_fwd")
    pr = _wmm(y_rnn, w["w_br_rnn"], "nn", "mm_br_rnn")
    pa = _wmm(y_attn, w["w_br_attn"], "nn", "mm_br_attn")
    merged = merge_fwd(proj, pr, pa, "merge_fwd")
    mix = _wmm(merged, w["w_out"], "nn", "mm_out")
    h1, xh1, rs1 = ln_fwd(h, mix, w["ln1_g"], w["ln1_b"], alpha, "ln1_fwd")
    qc = _wmm(h1, w["cq_w"], "nn", "mm_cq")
    kv = _wmm(mem, w["ckv_w"], "nn", "mm_ckv")
    o = cross_fwd(qc, kv, "cross_fwd")
    co = _wmm(o, w["co_w"], "nn", "mm_co")
    h2, xh2, rs2 = ln_fwd(h1, co, w["ln2_g"], w["ln2_b"], alpha, "ln2_fwd")
    gu = _wmm(h2, w["ffn_wi"], "nn", "mm_ffn_wi")
    act = swiglu_fwd(gu, "swiglu_fwd")
    f = _wmm(act, w["ffn_wo"], "nn", "mm_ffn_wo")
    h3, xh3, rs3 = ln_fwd(h2, f, w["ln3_g"], w["ln3_b"], alpha, "ln3_fwd")
    saved = dict(h=h, proj=proj, xc=xc, r=r, ig=ig, a=a, hs=hs, y_rnn=y_rnn, y_attn=y_attn, pr=pr, pa=pa, xh1=xh1, rs1=rs1, h1=h1,
                 qc=qc, kv=kv, o=o, xh2=xh2, rs2=rs2, h2=h2, gu=gu, xh3=xh3, rs3=rs3)
    return h3, saved


def layer_bwd(dh, mem, w, s, tab, alpha):
    D = dh.shape[1]
    g = {}
    wg = dict(out_dtype=MXU_DTYPE)
    dz3, g["ln3_g"], g["ln3_b"] = ln_bwd(dh, None, s["xh3"], s["rs3"], w["ln3_g"], 1.0, "ln3_bwd")
    act = swiglu_fwd(s["gu"], "swiglu_refwd")
    g["ffn_wo"] = mm(act, dz3, "tn", "mm_d_ffn_wo", **wg)
    dact = _wmm(dz3, w["ffn_wo"], "nt", "mm_dact")
    dgu = swiglu_bwd(s["gu"], dact, "swiglu_bwd")
    g["ffn_wi"] = mm(s["h2"], dgu, "tn", "mm_d_ffn_wi", out_chips=True, **wg)
    dh2 = _wmm(dgu, w["ffn_wi"], "nt", "mm_dh2")
    dz2, g["ln2_g"], g["ln2_b"] = ln_bwd(dz3, dh2, s["xh2"], s["rs2"], w["ln2_g"], alpha, "ln2_bwd")
    g["co_w"] = mm(s["o"], dz2, "tn", "mm_d_co", **wg)
    do = _wmm(dz2, w["co_w"], "nt", "mm_do")
    dqc, dkv = cross_bwd(s["qc"], s["kv"], do, "cross_bwd")
    g["cq_w"] = mm(s["h1"], dqc, "tn", "mm_d_cq", **wg)
    g["ckv_w"] = mm(mem, dkv, "tn", "mm_d_ckv", out_chips=True, **wg)
    dh1 = _wmm(dqc, w["cq_w"], "nt", "mm_dh1")
    dz1, g["ln1_g"], g["ln1_b"] = ln_bwd(dz2, dh1, s["xh1"], s["rs1"], w["ln1_g"], alpha, "ln1_bwd")
    merged = merge_fwd(s["proj"], s["pr"], s["pa"], "merge_refwd")
    g["w_out"] = mm(merged, dz1, "tn", "mm_d_out", **wg)
    dm = _wmm(dz1, w["w_out"], "nt", "mm_dmerged")
    dpr, dpa, dg_rnn, dg_attn = merge_bwd(s["proj"], s["pr"], s["pa"], dm, "merge_bwd")
    g["w_br_rnn"] = mm(s["y_rnn"], dpr, "tn", "mm_d_br_rnn", **wg)
    g["w_br_attn"] = mm(s["y_attn"], dpa, "tn", "mm_d_br_attn", **wg)
    dy_rnn = _wmm(dpr, w["w_br_rnn"], "nt", "mm_dy_rnn")
    dy_attn = _wmm(dpa, w["w_br_attn"], "nt", "mm_dy_attn")
    dq, dkb, dvb, dsink = attn_bwd(s["proj"], w["sinks"], tab, s["y_attn"], dy_attn, D, "attn_bwd")
    dk, dv = band_fold(dkb, dvb, "band_fold")
    g["sinks"] = dsink[:, :w["sinks"].shape[0]]
    dgr, gt = rg_scan_bwd(s["proj"], dy_rnn, s["hs"], s["a"], "rg_scan_bwd")
    dxc, g["w_rg"], g["w_ig"], g["b_rg"], g["b_ig"], g["lru_lambda"] = rg_gates_bwd(
        gt, s["hs"], s["xc"], s["r"], s["ig"], w["w_rg"], w["w_ig"], w["lru_lambda"], "rg_gates_bwd")
    dxr, g["conv_w"], g["conv_b"] = rg_conv_bwd(s["proj"], dxc, w["conv_w"], "rg_conv_bwd")
    dproj = jnp.concatenate([dxr, dgr, dq, dk, dv, dg_rnn, dg_attn], axis=1)
    g["w_in"] = mm(s["h"], dproj, "tn", "mm_d_in")
    dhm = _wmm(dproj, w["w_in"], "nt", "mm_dh")
    return axpby(dz1, dhm, alpha, "layer_dx"), g


def local_step(x, mem, target, layers):
    L = len(layers)
    alpha = _alpha(L)
    tab = rope_table(x.shape[0])
    h, saved = x, []
    for wl in layers:
        h, s = layer_fwd(h, mem, wl, tab, alpha)
        saved.append(s)
    dh, loss = loss_head(h, target, "loss_head")
    grads = [None] * L
    for l in reversed(range(L)):
        dh, grads[l] = layer_bwd(dh, mem, layers[l], saved[l], tab, alpha)
    return loss, dh, grads


def _pad_rows(flat):
    n = flat.shape[0]
    rows = -(-n // (LANES * SUBLANES)) * SUBLANES
    return jnp.pad(flat, (0, rows * LANES - n)).reshape(rows, LANES)


def kernel(x, mem, w_in, conv_w, conv_b, w_rg, b_rg, w_ig, b_ig, lru_lambda, w_br_rnn, w_br_attn, sinks, w_out, ln1_g, ln1_b, cq_w, ckv_w, co_w, ln2_g, ln2_b, ffn_wi, ffn_wo, ln3_g, ln3_b, loss_target, m_w_in, m_conv_w, m_conv_b, m_w_rg, m_b_rg, m_w_ig, m_b_ig, m_lru_lambda, m_w_br_rnn, m_w_br_attn, m_sinks, m_w_out, m_ln1_g, m_ln1_b, m_cq_w, m_ckv_w, m_co_w, m_ln2_g, m_ln2_b, m_ffn_wi, m_ffn_wo, m_ln3_g, m_ln3_b, v_w_in, v_conv_w, v_conv_b, v_w_rg, v_b_rg, v_w_ig, v_b_ig, v_lru_lambda, v_w_br_rnn, v_w_br_attn, v_sinks, v_w_out, v_ln1_g, v_ln1_b, v_cq_w, v_ckv_w, v_co_w, v_ln2_g, v_ln2_b, v_ffn_wi, v_ffn_wo, v_ln3_g, v_ln3_b):
    args = dict(locals())
    w = {n: args[n] for n in WEIGHTS}
    m = {n: args["m_" + n] for n in WEIGHTS}
    v = {n: args["v_" + n] for n in WEIGHTS}
    cx, cy, cc = _place()
    chip = 2 * cx + cy
    L = w_in.shape[0]

    gathered = dict(zip(BIG, gather_weights([w[n].astype(MXU_DTYPE) for n in BIG], "gather_weights")))
    cw_rows = _pad_rows(conv_w.reshape(-1))
    cw_all = gather_devices(cw_rows, "gather_conv_w")[0::2]
    cw_parts = cw_all.reshape(N_CHIPS, -1)[:, :conv_w.size].reshape((N_CHIPS,) + conv_w.shape)
    conv_full = jnp.concatenate([cw_parts[k] for k in range(N_CHIPS)], axis=2)
    layers = []
    for l in range(L):
        wl = {}
        for n in BIG:
            gw = gathered[n]
            rows_joined = gw.reshape(gw.shape[:-3] + (-1, gw.shape[-1]))
            if n == "w_in":
                wl[n] = (jnp.concatenate([gw[l, k] for k in range(N_CHIPS)], axis=1), {})
            elif n in COL_BLOCKED:
                wl[n] = (gw, dict(b_index=(l,), chips=True))
            elif n in GATE_WEIGHTS:
                wl[n] = rows_joined[l]
            else:
                wl[n] = (rows_joined, dict(b_index=(l,)))
        for n in SMALL:
            wl[n] = conv_full[l] if n == "conv_w" else w[n][l] if n == "sinks" else w[n][l][None, :]
        layers.append(wl)

    loss11, dx, grads = local_step(x[0], mem[0], loss_target[0], layers)
    loss = lax.psum(loss11[0, 0], ("x", "y", "c"))

    def for_chips(n, g):
        if n in COL_BLOCKED:
            return g
        if n in GATE_WEIGHTS:
            nb, bw, _ = g.shape
            g = g.reshape(nb, N_CHIPS, bw // N_CHIPS, bw).transpose(1, 0, 2, 3).reshape(N_CHIPS, nb * bw // N_CHIPS, bw)
        elif SHARD_AXIS[n] == 0:
            g = g.reshape(N_CHIPS, g.shape[0] // N_CHIPS, g.shape[1])
        else:
            g = jnp.stack(jnp.split(g, N_CHIPS, axis=1))
        return g.astype(MXU_DTYPE)

    partial_sums = [jnp.stack([for_chips(n, gl[n]) for gl in grads]) for n in BIG]
    place = jnp.stack([cc, chip]).astype(jnp.int32)
    from_sibling = swap_sibling(partial_sums, "grad_to_sibling")
    chip_sums = [add_pair(place, a, b, "grad_add_pair_" + n) for n, a, b in zip(BIG, partial_sums, from_sibling)]
    from_chips = scatter_chips(chip_sums, "grad_scatter")
    reduced = [add_chips(place, a, b, "grad_add_chips_" + n) for n, a, b in zip(BIG, chip_sums, from_chips)]
    reduced = join_halves(reduced, "grad_join")
    gshard = {n: r.reshape(w[n].shape) for n, r in zip(BIG, reduced)}

    small_full = {n: jnp.stack([gl[n] for gl in grads]).reshape(w[n].shape[:1] + ((CONV_WIDTH, -1) if n == "conv_w" else (-1,)))
                  for n in SMALL}
    small_flat = jnp.concatenate([small_full[n].reshape(-1) for n in SMALL])
    small_sum = sum_devices(gather_devices(_pad_rows(small_flat), "gather_small_grads"), "sum_small_grads").reshape(-1)
    off = 0
    for n in SMALL:
        gfull = small_sum[off:off + small_full[n].size].reshape(small_full[n].shape)
        off += small_full[n].size
        if n == "conv_w":
            width = conv_w.shape[2]
            gfull = lax.dynamic_slice_in_dim(gfull, chip * width, width, axis=2)
        gshard[n] = gfull

    delta, new_m, new_v, grad = {}, {}, {}, {}
    for n in WEIGHTS:
        delta[n], new_m[n], new_v[n], grad[n] = adamw(w[n], gshard[n], m[n], v[n], "adamw_" + n)
    return (loss, dx[None], *[grad[n] for n in WEIGHTS], *[delta[n] for n in WEIGHTS], *[new_m[n] for n in WEIGHTS],
            *[new_v[n] for n in WEIGHTS])
```

```python
import functools
import math

import jax
import jax.numpy as jnp
import numpy as np
from jax import lax
from jax.experimental import pallas as pl
from jax.experimental.pallas import tpu as pltpu

F32 = jnp.float32
BF16 = jnp.bfloat16
MXU_DTYPE = BF16

HEAD_DIM = 64
N_KV_HEADS = 2
WINDOW = 128
ROT_DIM = HEAD_DIM // 4
ROPE_THETA = 500000.0
CROSS_HEADS = 4
RNN_BLOCKS = 4
CONV_WIDTH = 4
LRU_C = 8.0
LN_EPS = 1e-5
NEG_INF = -1e30
ADAM_LR = 0.001
ADAM_B1 = 0.9
ADAM_B2 = 0.999
ADAM_EPS = 1e-08
ADAM_WD = 0.01
ADAM_STEP = 10

VMEM_BYTES_V7X = 64 * 1024 * 1024
VMEM_BLOCK_BUDGET = 36 * 1024 * 1024
LANES = 128
SUBLANES = 8

MESH_ID = pl.DeviceIdType.MESH
N_CHIPS = 4
N_DEV = 8

BIG = ("w_in", "w_rg", "w_ig", "w_br_rnn", "w_br_attn", "w_out", "cq_w", "ckv_w", "co_w", "ffn_wi", "ffn_wo")
SHARD_AXIS = {"w_in": 1, "w_rg": 1, "w_ig": 1, "w_br_rnn": 0, "w_br_attn": 0, "w_out": 0, "cq_w": 0, "ckv_w": 1,
              "co_w": 0, "ffn_wi": 1, "ffn_wo": 0}
SMALL = ("conv_w", "conv_b", "b_rg", "b_ig", "lru_lambda", "sinks", "ln1_g", "ln1_b", "ln2_g", "ln2_b", "ln3_g", "ln3_b")
WEIGHTS = ("w_in", "conv_w", "conv_b", "w_rg", "b_rg", "w_ig", "b_ig", "lru_lambda", "w_br_rnn", "w_br_attn", "sinks",
           "w_out", "ln1_g", "ln1_b", "cq_w", "ckv_w", "co_w", "ln2_g", "ln2_b", "ffn_wi", "ffn_wo", "ln3_g", "ln3_b")
GATE_WEIGHTS = ("w_rg", "w_ig")
COL_BLOCKED = ("ckv_w", "ffn_wi")


def _params(dims=None, vmem=None):
    return pltpu.CompilerParams(dimension_semantics=dims, vmem_limit_bytes=vmem)


def _vmem_limit(block_bytes, temp_bytes=0):
    want = int(2 * block_bytes + temp_bytes) + (6 << 20)
    return max(32 << 20, min(want, VMEM_BYTES_V7X - (6 << 20)))


def _divisors(n, align, cap):
    out = [d for d in range(align, min(n, cap) + 1, align) if n % d == 0]
    if n <= cap and n not in out:
        out.append(n)
    return sorted(out, reverse=True) or [n]


def _sigmoid(x):
    return 1.0 / (1.0 + jnp.exp(-x))


def _gelu_parts(x):
    c = math.sqrt(2.0 / math.pi)
    u = c * (x + 0.044715 * x * x * x)
    t = jnp.tanh(u)
    return t, c * (1.0 + 3 * 0.044715 * x * x)


def _gelu(x):
    t, _ = _gelu_parts(x)
    return 0.5 * x * (1.0 + t)


def _gelu_grad(x):
    t, du = _gelu_parts(x)
    return 0.5 * (1.0 + t) + 0.5 * x * (1.0 - t * t) * du


def _neg_expm1(x):
    series = x * (1.0 + x * (0.5 + x * (1.0 / 6 + x * (1.0 / 24 + x * (1.0 / 120)))))
    return -jnp.where(x > -0.1, series, jnp.exp(x) - 1.0)


def _softplus_neg(lam):
    x = -lam
    return jnp.maximum(x, 0.0) + jnp.log1p(jnp.exp(-jnp.abs(x)))


STEP_US = 0.35
HBM_BYTES_PER_US = 2.5e6
MXU_FLOPS_PER_US = 7e8


def mm(a, b, mode, name, *, b_index=(), chips=False, out_chips=False, out_dtype=F32, deps=()):
    nlead = len(b_index) + (1 if chips else 0)
    bk, bn = b.shape[nlead:]
    if mode == "nn":
        (M, K), N, cs = a.shape, (bn * N_CHIPS if chips else bn), bn
    elif mode == "nt":
        (M, K), N, cs = a.shape, bk, bn
    else:
        (K, M), N, cs = a.shape, bn, bn // N_CHIPS
    asz, bsz, osz = a.dtype.itemsize, b.dtype.itemsize, jnp.dtype(out_dtype).itemsize
    n_unit = cs if (chips and mode == "nn") or out_chips else N
    k_unit = cs if (chips and mode == "nt") else K
    tms = _divisors(M, LANES if mode == "tn" else SUBLANES, 2048)
    tns = _divisors(n_unit, LANES, 2048)
    tks = sorted(set(_divisors(k_unit, LANES, k_unit) + ([K] if not chips else [])), reverse=True)
    best = None
    for tm in tms:
        for tn in tns:
            for tk in tks:
                nk = K // tk
                scratch = tm * tn * 4 if (nk > 1 and osz != 4) else 0
                blocks = tm * tk * asz + tn * tk * bsz + tm * tn * osz
                temps = tm * tk * (2 + (4 if mode == "tn" else 0)) + tn * tk * 2 + tm * tn * 4 + scratch
                if 2 * blocks + temps > VMEM_BLOCK_BUDGET + (8 << 20):
                    continue
                ni, nj = M // tm, N // tn
                traffic = M * K * asz * (nj if nk > 1 else 1) + N * K * bsz * (1 if nj * nk == 1 else ni) + M * N * osz
                busy = max(traffic / HBM_BYTES_PER_US, 2.0 * M * N * K / MXU_FLOPS_PER_US)
                cost = ni * nj * nk * STEP_US + busy + blocks / HBM_BYTES_PER_US
                if best is None or cost < best[0]:
                    best = (cost, tm, tn, tk, blocks, temps)
    _, tm, tn, tk, blocks, temps = best
    nk, npc, kpc = K // tk, n_unit // tn, k_unit // tk
    use_scratch = nk > 1 and osz != 4

    def body(a_ref, b_ref, *rest):
        o_ref, acc = rest[len(deps)], rest[len(deps) + 1:]
        av = a_ref[...].astype(MXU_DTYPE)
        bv = b_ref[...].astype(MXU_DTYPE)
        dn = {"nn": (((1,), (0,)), ((), ())), "nt": (((1,), (1,)), ((), ())), "tn": (((0,), (0,)), ((), ()))}[mode]
        r = lax.dot_general(av, bv, dn, preferred_element_type=F32)
        if nk == 1:
            o_ref[...] = r.astype(o_ref.dtype)
        else:
            acc_ref = acc[0] if use_scratch else o_ref

            @pl.when(pl.program_id(2) == 0)
            def _():
                acc_ref[...] = r

            @pl.when(pl.program_id(2) > 0)
            def _():
                acc_ref[...] += r

            if use_scratch:
                @pl.when(pl.program_id(2) == nk - 1)
                def _():
                    o_ref[...] = acc_ref[...].astype(o_ref.dtype)

    if mode == "tn":
        a_spec = pl.BlockSpec((tk, tm), lambda i, j, k: (k, i))
    else:
        a_spec = pl.BlockSpec((tm, tk), lambda i, j, k: (i, k))
    lead = (None,) * nlead
    if mode == "nt":
        bmap = (lambda i, j, k: b_index + (k // kpc, j, k % kpc)) if chips else (lambda i, j, k: b_index + (j, k))
        b_spec = pl.BlockSpec(lead + (tn, tk), bmap)
    else:
        bmap = (lambda i, j, k: b_index + (j // npc, k, j % npc)) if chips else (lambda i, j, k: b_index + (k, j))
        b_spec = pl.BlockSpec(lead + (tk, tn), bmap)
    if out_chips:
        o_spec = pl.BlockSpec((None, tm, tn), lambda i, j, k: (j // npc, i, j % npc))
        o_shape = jax.ShapeDtypeStruct((N_CHIPS, M, cs), out_dtype)
    else:
        o_spec = pl.BlockSpec((tm, tn), lambda i, j, k: (i, j))
        o_shape = jax.ShapeDtypeStruct((M, N), out_dtype)
    return pl.pallas_call(
        body, name=name, grid=(M // tm, N // tn, nk), in_specs=[a_spec, b_spec] + [pl.BlockSpec(memory_space=pl.ANY)] * len(deps),
        out_specs=o_spec, out_shape=o_shape, scratch_shapes=[pltpu.VMEM((tm, tn), F32)] if use_scratch else [],
        compiler_params=_params(("parallel", "parallel", "arbitrary"), _vmem_limit(blocks, temps)),
    )(a, b, *deps)


ROW_TILE = 512
GATE_ROWS = 1024


def ln_fwd(h, f, g, b, alpha, name):
    S, D = h.shape
    tr = min(ROW_TILE, S)

    def body(h_ref, f_ref, g_ref, b_ref, y_ref, xh_ref, rs_ref):
        z = alpha * h_ref[...] + f_ref[...]
        mu = jnp.mean(z, axis=-1, keepdims=True)
        zc = z - mu
        var = jnp.mean(zc * zc, axis=-1, keepdims=True)
        rs = lax.rsqrt(var + LN_EPS)
        xh = zc * rs
        y_ref[...] = xh * g_ref[...] + b_ref[...]
        xh_ref[...] = xh
        rs_ref[...] = rs

    row = pl.BlockSpec((tr, D), lambda i: (i, 0))
    vec = pl.BlockSpec((1, D), lambda i: (0, 0))
    return pl.pallas_call(
        body, name=name, grid=(S // tr,), in_specs=[row, row, vec, vec],
        out_specs=[row, row, pl.BlockSpec((tr, 1), lambda i: (i, 0))],
        out_shape=[jax.ShapeDtypeStruct((S, D), F32), jax.ShapeDtypeStruct((S, D), F32), jax.ShapeDtypeStruct((S, 1), F32)],
        compiler_params=_params(("parallel",), 48 << 20),
    )(h, f, g, b)


def ln_bwd(dy_a, dy_b, xh, rs, g, c1, name):
    S, D = xh.shape
    tr = min(ROW_TILE, S)
    two = dy_b is not None

    def body(*refs):
        if two:
            a_ref, b_ref, xh_ref, rs_ref, g_ref, dz_ref, dg_ref, db_ref = refs
            dy = c1 * a_ref[...] + b_ref[...]
        else:
            a_ref, xh_ref, rs_ref, g_ref, dz_ref, dg_ref, db_ref = refs
            dy = a_ref[...]
        x = xh_ref[...]
        dyg = dy * g_ref[...]
        m1 = jnp.mean(dyg, axis=-1, keepdims=True)
        m2 = jnp.mean(dyg * x, axis=-1, keepdims=True)
        dz_ref[...] = rs_ref[...] * (dyg - m1 - x * m2)

        @pl.when(pl.program_id(0) == 0)
        def _():
            dg_ref[...] = jnp.zeros_like(dg_ref)
            db_ref[...] = jnp.zeros_like(db_ref)

        dg_ref[...] += jnp.sum(dy * x, axis=0, keepdims=True)
        db_ref[...] += jnp.sum(dy, axis=0, keepdims=True)

    row = pl.BlockSpec((tr, D), lambda i: (i, 0))
    vec = pl.BlockSpec((1, D), lambda i: (0, 0))
    ins = [row, row] if two else [row]
    args = (dy_a, dy_b) if two else (dy_a,)
    return pl.pallas_call(
        body, name=name, grid=(S // tr,), in_specs=ins + [row, pl.BlockSpec((tr, 1), lambda i: (i, 0)), vec],
        out_specs=[row, vec, vec],
        out_shape=[jax.ShapeDtypeStruct((S, D), F32), jax.ShapeDtypeStruct((1, D), F32), jax.ShapeDtypeStruct((1, D), F32)],
        compiler_params=_params(("arbitrary",), 48 << 20),
    )(*args, xh, rs, g)


def axpby(a, b, c1, name):
    S, D = a.shape
    tr = min(ROW_TILE, S)

    def body(a_ref, b_ref, o_ref):
        o_ref[...] = c1 * a_ref[...] + b_ref[...]

    row = pl.BlockSpec((tr, D), lambda i: (i, 0))
    return pl.pallas_call(body, name=name, grid=(S // tr,), in_specs=[row, row], out_specs=row,
                          out_shape=jax.ShapeDtypeStruct((S, D), F32), compiler_params=_params(("parallel",)))(a, b)


def loss_head(y, t, name):
    S, D = y.shape
    tr = min(ROW_TILE, S)
    nsteps = S // tr

    def body(y_ref, t_ref, dy_ref, l_ref, acc_ref):
        i = pl.program_id(0)

        @pl.when(i == 0)
        def _():
            acc_ref[...] = jnp.zeros_like(acc_ref)

        e = y_ref[...] - t_ref[...]
        dy_ref[...] = e * (1.0 / D)
        acc_ref[...] += jnp.sum(e * e, axis=0, keepdims=True)

        @pl.when(i == nsteps - 1)
        def _():
            l_ref[...] = jnp.sum(acc_ref[...], axis=1, keepdims=True) * (0.5 / D)

    row = pl.BlockSpec((tr, D), lambda i: (i, 0))
    return pl.pallas_call(
        body, name=name, grid=(nsteps,), in_specs=[row, row],
        out_specs=[row, pl.BlockSpec((1, 1), lambda i: (0, 0))],
        out_shape=[jax.ShapeDtypeStruct((S, D), F32), jax.ShapeDtypeStruct((1, 1), F32)],
        scratch_shapes=[pltpu.VMEM((1, D), F32)], compiler_params=_params(("arbitrary",)),
    )(y, t)


def swiglu_fwd(gu, name):
    S, F2 = gu.shape
    Fh = F2 // 2
    tc = _divisors(Fh, LANES, 1536)[0]
    nb = Fh // tc
    tr = min(ROW_TILE, S)

    def body(g_ref, u_ref, o_ref):
        g = g_ref[...]
        o_ref[...] = g * _sigmoid(g) * u_ref[...]

    return pl.pallas_call(
        body, name=name, grid=(S // tr, nb),
        in_specs=[pl.BlockSpec((tr, tc), lambda i, j: (i, j)), pl.BlockSpec((tr, tc), lambda i, j: (i, nb + j))],
        out_specs=pl.BlockSpec((tr, tc), lambda i, j: (i, j)), out_shape=jax.ShapeDtypeStruct((S, Fh), F32),
        compiler_params=_params(("parallel", "parallel")),
    )(gu, gu)


def swiglu_bwd(gu, dact, name):
    S, F2 = gu.shape
    Fh = F2 // 2
    tc = _divisors(Fh, LANES, 1536)[0]
    nb = Fh // tc
    tr = min(ROW_TILE, S)

    def body(g_ref, u_ref, d_ref, o_ref):
        g, u, d = g_ref[...], u_ref[...], d_ref[...]
        s = _sigmoid(g)
        dgate = d * u * (s * (1.0 + g * (1.0 - s)))
        dup = d * (g * s)
        o_ref[...] = jnp.where(pl.program_id(1) < nb, dgate, dup)

    return pl.pallas_call(
        body, name=name, grid=(S // tr, 2 * nb),
        in_specs=[pl.BlockSpec((tr, tc), lambda i, j: (i, j % nb)), pl.BlockSpec((tr, tc), lambda i, j: (i, nb + j % nb)),
                  pl.BlockSpec((tr, tc), lambda i, j: (i, j % nb))],
        out_specs=pl.BlockSpec((tr, tc), lambda i, j: (i, j)), out_shape=jax.ShapeDtypeStruct((S, F2), F32),
        compiler_params=_params(("parallel", "parallel")),
    )(gu, gu, dact)


GATE_COLS = 256


def merge_fwd(proj, pr, pa, name):
    S, D = pr.shape
    tr = min(GATE_ROWS, S)
    c0 = (3 * D + 2 * N_KV_HEADS * HEAD_DIM) // GATE_COLS
    c1 = c0 + D // GATE_COLS

    def body(gr_ref, ga_ref, pr_ref, pa_ref, o_ref):
        o_ref[...] = _sigmoid(gr_ref[...]) * pr_ref[...] + _sigmoid(ga_ref[...]) * pa_ref[...]

    blk = pl.BlockSpec((tr, GATE_COLS), lambda i, j: (i, j))
    return pl.pallas_call(
        body, name=name, grid=(S // tr, D // GATE_COLS),
        in_specs=[pl.BlockSpec((tr, GATE_COLS), lambda i, j: (i, c0 + j)), pl.BlockSpec((tr, GATE_COLS), lambda i, j: (i, c1 + j)),
                  blk, blk],
        out_specs=blk, out_shape=jax.ShapeDtypeStruct((S, D), F32), compiler_params=_params(("parallel", "parallel")),
    )(proj, proj, pr, pa)


def merge_bwd(proj, pr, pa, dm, name):
    S, D = pr.shape
    tr = min(GATE_ROWS, S)
    c0 = (3 * D + 2 * N_KV_HEADS * HEAD_DIM) // GATE_COLS
    c1 = c0 + D // GATE_COLS

    def body(gr_ref, ga_ref, pr_ref, pa_ref, dm_ref, dpr_ref, dpa_ref, dgr_ref, dga_ref):
        sr, sa, d = _sigmoid(gr_ref[...]), _sigmoid(ga_ref[...]), dm_ref[...]
        dpr_ref[...] = d * sr
        dpa_ref[...] = d * sa
        dgr_ref[...] = d * pr_ref[...] * (sr * (1.0 - sr))
        dga_ref[...] = d * pa_ref[...] * (sa * (1.0 - sa))

    blk = pl.BlockSpec((tr, GATE_COLS), lambda i, j: (i, j))
    sds = jax.ShapeDtypeStruct((S, D), F32)
    return pl.pallas_call(
        body, name=name, grid=(S // tr, D // GATE_COLS),
        in_specs=[pl.BlockSpec((tr, GATE_COLS), lambda i, j: (i, c0 + j)), pl.BlockSpec((tr, GATE_COLS), lambda i, j: (i, c1 + j)),
                  blk, blk, blk],
        out_specs=[blk, blk, blk, blk], out_shape=[sds, sds, sds, sds], compiler_params=_params(("parallel", "parallel")),
    )(proj, proj, pr, pa, dm)


RG_ROWS = 512


def _shift_down(cur, prev, d, row, first):
    halo = jnp.where(first, 0.0, pltpu.roll(prev, d, 0))
    return jnp.where(row >= d, pltpu.roll(cur, d, 0), halo)


def _shift_up(cur, nxt, d, row, last, tr):
    halo = jnp.where(last, 0.0, pltpu.roll(nxt, tr - d, 0))
    return jnp.where(row < tr - d, pltpu.roll(cur, tr - d, 0), halo)


def _lru_coeffs(r, lam):
    sp = _softplus_neg(lam)
    la = -LRU_C * r * sp
    return sp, la, jnp.exp(la), _neg_expm1(2.0 * la)


def rg_gates_fwd(proj, conv_w, conv_b, w_rg, b_rg, w_ig, b_ig, lam, name):
    S = proj.shape[0]
    nblk, bw, _ = w_rg.shape
    D = nblk * bw
    tr = min(RG_ROWS, S)

    def body(xr_ref, xp_ref, cw_ref, cb_ref, wr_ref, br_ref, wi_ref, bi_ref, lam_ref, xc_ref, r_ref, i_ref, a_ref, b_ref):
        first = pl.program_id(1) == 0
        cur, prev = xr_ref[...], xp_ref[...]
        row = lax.broadcasted_iota(jnp.int32, cur.shape, 0)
        xc = cb_ref[...]
        for k in range(CONV_WIDTH - 1):
            xc = xc + _shift_down(cur, prev, CONV_WIDTH - 1 - k, row, first) * cw_ref[k:k + 1, :]
        xc = xc + cur * cw_ref[CONV_WIDTH - 1:CONV_WIDTH, :]
        xm = xc.astype(MXU_DTYPE)
        r = _sigmoid(jnp.dot(xm, wr_ref[...].astype(MXU_DTYPE), preferred_element_type=F32) + br_ref[...])
        ig = _sigmoid(jnp.dot(xm, wi_ref[...].astype(MXU_DTYPE), preferred_element_type=F32) + bi_ref[...])
        _, _, a, em = _lru_coeffs(r, lam_ref[...])
        xc_ref[...] = xc
        r_ref[...] = r
        i_ref[...] = ig
        a_ref[...] = a
        b_ref[...] = jnp.sqrt(em) * (ig * xc)

    tile = pl.BlockSpec((tr, bw), lambda n, i: (i, n))
    vec = pl.BlockSpec((1, bw), lambda n, i: (0, n))
    wblk = pl.BlockSpec((None, bw, bw), lambda n, i: (n, 0, 0))
    sds = jax.ShapeDtypeStruct((S, D), F32)
    return pl.pallas_call(
        body, name=name, grid=(nblk, S // tr),
        in_specs=[tile, pl.BlockSpec((tr, bw), lambda n, i: (jnp.maximum(i - 1, 0), n)),
                  pl.BlockSpec((CONV_WIDTH, bw), lambda n, i: (0, n)), vec, wblk, vec, wblk, vec, vec],
        out_specs=[tile] * 5, out_shape=[sds] * 5, compiler_params=_params(("parallel", "parallel")),
    )(proj, proj, conv_w, conv_b, w_rg, b_rg, w_ig, b_ig, lam)


SCAN_COLS = 256
CHUNK = SUBLANES
SCAN_UNROLL = 4


def rg_scan_fwd(proj, a, b, name):
    S, D = a.shape
    cb = min(SCAN_COLS, D)
    goff = D // cb

    def body(a_ref, b_ref, g_ref, hs_ref, y_ref):
        row = lax.broadcasted_iota(jnp.int32, (CHUNK, cb), 0)

        def step(c, carry):
            r0 = pl.multiple_of(c * CHUNK, CHUNK)
            A = a_ref[pl.ds(r0, CHUNK), :]
            B = b_ref[pl.ds(r0, CHUNK), :]
            for d in (1, 2, 4):
                As = jnp.where(row >= d, pltpu.roll(A, d, 0), 1.0)
                Bs = jnp.where(row >= d, pltpu.roll(B, d, 0), 0.0)
                B = A * Bs + B
                A = A * As
            H = B + A * carry
            hs_ref[pl.ds(r0, CHUNK), :] = H
            return jnp.sum(jnp.where(row == CHUNK - 1, H, 0.0), axis=0, keepdims=True)

        lax.fori_loop(0, S // CHUNK, step, jnp.zeros((1, cb), F32), unroll=SCAN_UNROLL)
        y_ref[...] = hs_ref[...] * _gelu(g_ref[...])

    col = pl.BlockSpec((S, cb), lambda j: (0, j))
    sds = jax.ShapeDtypeStruct((S, D), F32)
    return pl.pallas_call(
        body, name=name, grid=(D // cb,), in_specs=[col, col, pl.BlockSpec((S, cb), lambda j: (0, goff + j))],
        out_specs=[col, col], out_shape=[sds, sds], compiler_params=_params(("parallel",), _vmem_limit(5 * S * cb * 4, 4 * S * cb * 4)),
    )(a, b, proj)


def rg_scan_bwd(proj, dy, hs, a, name):
    S, D = a.shape
    cb = min(SCAN_COLS, D)
    goff = D // cb
    nchunks = S // CHUNK

    def body(g_ref, dy_ref, hs_ref, a_ref, dg_ref, gt_ref):
        gate, dy = g_ref[...], dy_ref[...]
        dg_ref[...] = dy * hs_ref[...] * _gelu_grad(gate)
        gt_ref[...] = dy * _gelu(gate)
        row = lax.broadcasted_iota(jnp.int32, (CHUNK, cb), 0)

        def step(k, carry):
            c = nchunks - 1 - k
            r0 = pl.multiple_of(c * CHUNK, CHUNK)
            rn = pl.multiple_of(jnp.minimum(c + 1, nchunks - 1) * CHUNK, CHUNK)
            last = c == nchunks - 1
            nxt = jnp.where(last, 0.0, pltpu.roll(a_ref[pl.ds(rn, CHUNK), :], CHUNK - 1, 0))
            A = jnp.where(row < CHUNK - 1, pltpu.roll(a_ref[pl.ds(r0, CHUNK), :], CHUNK - 1, 0), nxt)
            B = gt_ref[pl.ds(r0, CHUNK), :]
            for d in (1, 2, 4):
                As = jnp.where(row < CHUNK - d, pltpu.roll(A, CHUNK - d, 0), 1.0)
                Bs = jnp.where(row < CHUNK - d, pltpu.roll(B, CHUNK - d, 0), 0.0)
                B = A * Bs + B
                A = A * As
            G = B + A * carry
            gt_ref[pl.ds(r0, CHUNK), :] = G
            return jnp.sum(jnp.where(row == 0, G, 0.0), axis=0, keepdims=True)

        lax.fori_loop(0, nchunks, step, jnp.zeros((1, cb), F32), unroll=SCAN_UNROLL)

    col = pl.BlockSpec((S, cb), lambda j: (0, j))
    sds = jax.ShapeDtypeStruct((S, D), F32)
    return pl.pallas_call(
        body, name=name, grid=(D // cb,), in_specs=[pl.BlockSpec((S, cb), lambda j: (0, goff + j)), col, col, col],
        out_specs=[col, col], out_shape=[sds, sds], compiler_params=_params(("parallel",), _vmem_limit(6 * S * cb * 4, 6 * S * cb * 4)),
    )(proj, dy, hs, a)


def rg_gates_bwd(gt, hs, xc, r, ig, w_rg, w_ig, lam, name):
    S, D = xc.shape
    nblk, bw, _ = w_rg.shape
    tr = min(RG_ROWS, S)

    def body(gt_ref, hs_ref, hp_ref, xc_ref, r_ref, i_ref, wr_ref, wi_ref, lam_ref,
             dxc_ref, dwr_ref, dwi_ref, dbr_ref, dbi_ref, dl_ref):
        step = pl.program_id(1)
        g, hs, xc, r, ig, lam = gt_ref[...], hs_ref[...], xc_ref[...], r_ref[...], i_ref[...], lam_ref[...]
        row = lax.broadcasted_iota(jnp.int32, g.shape, 0)
        hprev = _shift_down(hs, hp_ref[...], 1, row, step == 0)
        sp, _, a, em = _lru_coeffs(r, lam)
        mult = jnp.sqrt(em)
        du = g * mult
        dla = g * hprev * a - (g * (ig * xc)) * (a * a) / mult
        dpr = (dla * (-LRU_C * sp)) * (r * (1.0 - r))
        dpi = (du * xc) * (ig * (1.0 - ig))
        dprm, dpim = dpr.astype(MXU_DTYPE), dpi.astype(MXU_DTYPE)
        nt = (((1,), (1,)), ((), ()))
        dxc_ref[...] = (du * ig + lax.dot_general(dprm, wr_ref[...].astype(MXU_DTYPE), nt, preferred_element_type=F32)
                        + lax.dot_general(dpim, wi_ref[...].astype(MXU_DTYPE), nt, preferred_element_type=F32))

        @pl.when(step == 0)
        def _():
            for ref in (dwr_ref, dwi_ref, dbr_ref, dbi_ref, dl_ref):
                ref[...] = jnp.zeros_like(ref)

        xct = xc.T.astype(MXU_DTYPE)
        dwr_ref[...] += jnp.dot(xct, dprm, preferred_element_type=F32)
        dwi_ref[...] += jnp.dot(xct, dpim, preferred_element_type=F32)
        dbr_ref[...] += jnp.sum(dpr, axis=0, keepdims=True)
        dbi_ref[...] += jnp.sum(dpi, axis=0, keepdims=True)
        dl_ref[...] += jnp.sum(dla * (-LRU_C * r), axis=0, keepdims=True) * (-_sigmoid(-lam))

    tile = pl.BlockSpec((tr, bw), lambda n, i: (i, n))
    vec = pl.BlockSpec((1, bw), lambda n, i: (0, n))
    wblk = pl.BlockSpec((None, bw, bw), lambda n, i: (n, 0, 0))
    return pl.pallas_call(
        body, name=name, grid=(nblk, S // tr),
        in_specs=[tile, tile, pl.BlockSpec((tr, bw), lambda n, i: (jnp.maximum(i - 1, 0), n)), tile, tile, tile, wblk, wblk, vec],
        out_specs=[tile, wblk, wblk, vec, vec, vec],
        out_shape=[jax.ShapeDtypeStruct((S, D), F32), jax.ShapeDtypeStruct((nblk, bw, bw), F32), jax.ShapeDtypeStruct((nblk, bw, bw), F32),
                   jax.ShapeDtypeStruct((1, D), F32), jax.ShapeDtypeStruct((1, D), F32), jax.ShapeDtypeStruct((1, D), F32)],
        compiler_params=_params(("parallel", "arbitrary")),
    )(gt, hs, hs, xc, r, ig, w_rg, w_ig, lam)


def rg_conv_bwd(proj, dxc, conv_w, name):
    S, D = dxc.shape
    bw = min(SCAN_COLS, D)
    tr = min(RG_ROWS, S)
    nsteps = S // tr

    def body(d_ref, dn_ref, xr_ref, xp_ref, cw_ref, dxr_ref, dcw_ref, dcb_ref):
        step = pl.program_id(1)
        d, xr = d_ref[...], xr_ref[...]
        row = lax.broadcasted_iota(jnp.int32, d.shape, 0)
        dxr = d * cw_ref[CONV_WIDTH - 1:CONV_WIDTH, :]
        for k in range(CONV_WIDTH - 1):
            dxr = dxr + _shift_up(d, dn_ref[...], CONV_WIDTH - 1 - k, row, step == nsteps - 1, tr) * cw_ref[k:k + 1, :]
        dxr_ref[...] = dxr

        @pl.when(step == 0)
        def _():
            dcw_ref[...] = jnp.zeros_like(dcw_ref)
            dcb_ref[...] = jnp.zeros_like(dcb_ref)

        for k in range(CONV_WIDTH - 1):
            xs = _shift_down(xr, xp_ref[...], CONV_WIDTH - 1 - k, row, step == 0)
            dcw_ref[k:k + 1, :] += jnp.sum(d * xs, axis=0, keepdims=True)
        dcw_ref[CONV_WIDTH - 1:CONV_WIDTH, :] += jnp.sum(d * xr, axis=0, keepdims=True)
        dcb_ref[...] += jnp.sum(d, axis=0, keepdims=True)

    tile = pl.BlockSpec((tr, bw), lambda n, i: (i, n))
    cwb = pl.BlockSpec((CONV_WIDTH, bw), lambda n, i: (0, n))
    return pl.pallas_call(
        body, name=name, grid=(D // bw, nsteps),
        in_specs=[tile, pl.BlockSpec((tr, bw), lambda n, i: (jnp.minimum(i + 1, nsteps - 1), n)), tile,
                  pl.BlockSpec((tr, bw), lambda n, i: (jnp.maximum(i - 1, 0), n)), cwb],
        out_specs=[tile, cwb, pl.BlockSpec((1, bw), lambda n, i: (0, n))],
        out_shape=[jax.ShapeDtypeStruct((S, D), F32), jax.ShapeDtypeStruct((CONV_WIDTH, D), F32), jax.ShapeDtypeStruct((1, D), F32)],
        compiler_params=_params(("parallel", "arbitrary")),
    )(dxc, dxc, proj, proj, conv_w)


def rope_table(S):
    half = ROT_DIM // 2
    pos = jnp.arange(S, dtype=F32)
    inv = ROPE_THETA ** (-jnp.arange(0, ROT_DIM, 2, dtype=F32) / ROT_DIM)
    ang = pos[:, None] * inv[None, :]
    cos, sin = jnp.cos(ang), jnp.sin(ang)
    zero = jnp.zeros((S, HEAD_DIM - ROT_DIM), F32)
    c = jnp.concatenate([cos, cos, zero + 1.0], axis=1)
    a = jnp.concatenate([-sin, jnp.zeros((S, half), F32), zero], axis=1)
    b = jnp.concatenate([jnp.zeros((S, half), F32), sin, zero], axis=1)
    return jnp.stack([jnp.tile(t, (1, LANES // HEAD_DIM)) for t in (c, a, b)])


def _rope(t, tab):
    half = ROT_DIM // 2
    return t * tab[0] + pltpu.roll(t, LANES - half, 1) * tab[1] + pltpu.roll(t, half, 1) * tab[2]


def _rope_t(d, tab):
    half = ROT_DIM // 2
    return d * tab[0] + pltpu.roll(d * tab[1], half, 1) + pltpu.roll(d * tab[2], LANES - half, 1)


def _dup_head(t, hk, lo):
    sw = pltpu.roll(t, HEAD_DIM, 1)
    return jnp.where(lo, t, sw) if hk == 0 else jnp.where(lo, sw, t)


def _attn_common(n, sink_ref, q_ref, kp_ref, kc_ref, vp_ref, vc_ref, tc_ref, tp_ref, hk, pairs):
    tq = (tc_ref[0], tc_ref[1], tc_ref[2])
    tp = (tp_ref[0], tp_ref[1], tp_ref[2])
    lo = lax.broadcasted_iota(jnp.int32, (WINDOW, LANES), 1) < HEAD_DIM
    lo2 = lax.broadcasted_iota(jnp.int32, (2 * WINDOW, LANES), 1) < HEAD_DIM
    kband = jnp.concatenate([_rope(kp_ref[...], tp), _rope(kc_ref[...], tq)], axis=0)
    vband = jnp.concatenate([vp_ref[...], vc_ref[...]], axis=0)
    kd = _dup_head(kband, hk, lo2).astype(MXU_DTYPE)
    vd = _dup_head(vband, hk, lo2).astype(MXU_DTYPE)
    rows, sks = [], []
    for j in range(pairs):
        col = hk * pairs + j
        qp = _rope(q_ref[:, col * LANES:(col + 1) * LANES], tq)
        rows += [jnp.where(lo, qp, 0.0), jnp.where(lo, 0.0, qp)]
        sks += [jnp.full((WINDOW, 1), sink_ref[2 * col], F32), jnp.full((WINDOW, 1), sink_ref[2 * col + 1], F32)]
    qg = jnp.concatenate(rows, axis=0)
    sk = jnp.concatenate(sks, axis=0)
    G = 2 * pairs * WINDOW
    ri = lax.broadcasted_iota(jnp.int32, (G, 2 * WINDOW), 0) & (WINDOW - 1)
    kj = lax.broadcasted_iota(jnp.int32, (G, 2 * WINDOW), 1) - WINDOW
    valid = (kj <= ri) & (kj > ri - WINDOW) & (kj + n * WINDOW >= 0)
    s = lax.dot_general(qg.astype(MXU_DTYPE), kd, (((1,), (1,)), ((), ())), preferred_element_type=F32) * (HEAD_DIM ** -0.5)
    s = jnp.where(valid, s, NEG_INF)
    m = jnp.maximum(jnp.max(s, axis=1, keepdims=True), sk)
    e = jnp.exp(s - m)
    es = jnp.exp(sk - m)
    inv = 1.0 / (jnp.sum(e, axis=1, keepdims=True) + es)
    return qg, kd, vd, e * inv, es * inv, lo, lo2, tq, tp


def _attn_specs(D, NB):
    kcol = 3 * D // LANES
    q = pl.BlockSpec((WINDOW, D), lambda n: (n, 2))
    kc = pl.BlockSpec((WINDOW, LANES), lambda n: (n, kcol))
    kp = pl.BlockSpec((WINDOW, LANES), lambda n: (jnp.maximum(n - 1, 0), kcol))
    vc = pl.BlockSpec((WINDOW, LANES), lambda n: (n, kcol + 1))
    vp = pl.BlockSpec((WINDOW, LANES), lambda n: (jnp.maximum(n - 1, 0), kcol + 1))
    tc = pl.BlockSpec((3, WINDOW, LANES), lambda n: (0, n, 0))
    tp = pl.BlockSpec((3, WINDOW, LANES), lambda n: (0, jnp.maximum(n - 1, 0), 0))
    sink = pl.BlockSpec(memory_space=pltpu.SMEM)
    return [sink, q, kp, kc, vp, vc, tc, tp]


def attn_fwd(proj, sinks, tab, D, name):
    S = proj.shape[0]
    NB = S // WINDOW
    pairs = D // HEAD_DIM // N_KV_HEADS // 2

    def body(sink_ref, q_ref, kp_ref, kc_ref, vp_ref, vc_ref, tc_ref, tp_ref, o_ref):
        n = pl.program_id(0)
        for hk in range(N_KV_HEADS):
            _, _, vd, p, _, lo, _, _, _ = _attn_common(n, sink_ref, q_ref, kp_ref, kc_ref, vp_ref, vc_ref, tc_ref, tp_ref, hk, pairs)
            o = jnp.dot(p.astype(MXU_DTYPE), vd, preferred_element_type=F32)
            for j in range(pairs):
                col = hk * pairs + j
                oa = o[(2 * j) * WINDOW:(2 * j + 1) * WINDOW]
                ob = o[(2 * j + 1) * WINDOW:(2 * j + 2) * WINDOW]
                o_ref[:, col * LANES:(col + 1) * LANES] = jnp.where(lo, oa, ob)

    return pl.pallas_call(
        body, name=name, grid=(NB,), in_specs=_attn_specs(D, NB),
        out_specs=pl.BlockSpec((WINDOW, D), lambda n: (n, 0)), out_shape=jax.ShapeDtypeStruct((S, D), F32),
        compiler_params=_params(("parallel",)),
    )(sinks, proj, proj, proj, proj, proj, tab, tab)


def attn_bwd(proj, sinks, tab, o, do, D, name):
    S = proj.shape[0]
    NB = S // WINDOW
    pairs = D // HEAD_DIM // N_KV_HEADS // 2

    def body(sink_ref, q_ref, kp_ref, kc_ref, vp_ref, vc_ref, tc_ref, tp_ref, o_ref, do_ref, dq_ref, dk_ref, dv_ref, ds_ref):
        n = pl.program_id(0)

        @pl.when(n == 0)
        def _():
            ds_ref[...] = jnp.zeros_like(ds_ref)

        lane1 = lax.broadcasted_iota(jnp.int32, (1, LANES), 1)
        dsink = jnp.zeros((1, LANES), F32)
        dkt = dvt = None
        for hk in range(N_KV_HEADS):
            qg, kd, vd, p, ps, lo, lo2, tq, tp = _attn_common(n, sink_ref, q_ref, kp_ref, kc_ref, vp_ref, vc_ref, tc_ref, tp_ref, hk, pairs)
            dos, os_ = [], []
            for j in range(pairs):
                col = hk * pairs + j
                dop = do_ref[:, col * LANES:(col + 1) * LANES]
                op = o_ref[:, col * LANES:(col + 1) * LANES]
                dos += [jnp.where(lo, dop, 0.0), jnp.where(lo, 0.0, dop)]
                os_ += [jnp.where(lo, op, 0.0), jnp.where(lo, 0.0, op)]
            dog = jnp.concatenate(dos, axis=0)
            og = jnp.concatenate(os_, axis=0)
            dogm = dog.astype(MXU_DTYPE)
            dp = lax.dot_general(dogm, vd, (((1,), (1,)), ((), ())), preferred_element_type=F32)
            dr = jnp.sum(dog * og, axis=1, keepdims=True)
            ds = p * (dp - dr) * (HEAD_DIM ** -0.5)
            dsm = ds.astype(MXU_DTYPE)
            dqg = jnp.dot(dsm, kd, preferred_element_type=F32)
            dkd = jnp.dot(ds.T.astype(MXU_DTYPE), qg.astype(MXU_DTYPE), preferred_element_type=F32)
            dvd = jnp.dot(p.T.astype(MXU_DTYPE), dogm, preferred_element_type=F32)
            dkf = dkd + pltpu.roll(dkd, HEAD_DIM, 1)
            dvf = dvd + pltpu.roll(dvd, HEAD_DIM, 1)
            if hk == 0:
                dkt, dvt = dkf, dvf
            else:
                dkt, dvt = jnp.where(lo2, dkt, dkf), jnp.where(lo2, dvt, dvf)
            sd = ps * dr
            for j in range(pairs):
                col = hk * pairs + j
                dqa = dqg[(2 * j) * WINDOW:(2 * j + 1) * WINDOW]
                dqb = dqg[(2 * j + 1) * WINDOW:(2 * j + 2) * WINDOW]
                dq_ref[:, col * LANES:(col + 1) * LANES] = _rope_t(jnp.where(lo, dqa, dqb), tq)
                for t in range(2):
                    part = sd[(2 * j + t) * WINDOW:(2 * j + t + 1) * WINDOW]
                    val = jnp.sum(part, axis=0, keepdims=True)
                    dsink = dsink - jnp.where(lane1 == 2 * col + t, val, 0.0)
        dk_ref[...] = jnp.concatenate([_rope_t(dkt[:WINDOW], tp), _rope_t(dkt[WINDOW:], tq)], axis=0)
        dv_ref[...] = dvt
        ds_ref[...] += dsink

    blk = pl.BlockSpec((WINDOW, D), lambda n: (n, 0))
    band = pl.BlockSpec((None, 2 * WINDOW, LANES), lambda n: (n, 0, 0))
    return pl.pallas_call(
        body, name=name, grid=(NB,), in_specs=_attn_specs(D, NB) + [blk, blk],
        out_specs=[blk, band, band, pl.BlockSpec((1, LANES), lambda n: (0, 0))],
        out_shape=[jax.ShapeDtypeStruct((S, D), F32), jax.ShapeDtypeStruct((NB, 2 * WINDOW, LANES), F32),
                   jax.ShapeDtypeStruct((NB, 2 * WINDOW, LANES), F32), jax.ShapeDtypeStruct((1, LANES), F32)],
        compiler_params=_params(("arbitrary",)),
    )(sinks, proj, proj, proj, proj, proj, tab, tab, o, do)


def band_fold(dkb, dvb, name):
    NB = dkb.shape[0]
    k4 = dkb.reshape(NB, 2, WINDOW, LANES)
    v4 = dvb.reshape(NB, 2, WINDOW, LANES)

    def body(kc_ref, kn_ref, vc_ref, vn_ref, dk_ref, dv_ref):
        more = pl.program_id(0) < NB - 1
        dk_ref[...] = kc_ref[...] + jnp.where(more, kn_ref[...], 0.0)
        dv_ref[...] = vc_ref[...] + jnp.where(more, vn_ref[...], 0.0)

    cur = pl.BlockSpec((None, None, WINDOW, LANES), lambda n: (n, 1, 0, 0))
    nxt = pl.BlockSpec((None, None, WINDOW, LANES), lambda n: (jnp.minimum(n + 1, NB - 1), 0, 0, 0))
    out = pl.BlockSpec((WINDOW, LANES), lambda n: (n, 0))
    sds = jax.ShapeDtypeStruct((NB * WINDOW, LANES), F32)
    return pl.pallas_call(body, name=name, grid=(NB,), in_specs=[cur, nxt, cur, nxt], out_specs=[out, out], out_shape=[sds, sds],
                          compiler_params=_params(("parallel",)))(k4, k4, v4, v4)


CROSS_ROWS = 512


def _cross_probs(q, k, scale):
    s = lax.dot_general(q.astype(MXU_DTYPE), k.astype(MXU_DTYPE), (((1,), (1,)), ((), ())), preferred_element_type=F32) * scale
    e = jnp.exp(s - jnp.max(s, axis=1, keepdims=True))
    return e / jnp.sum(e, axis=1, keepdims=True)


def cross_fwd(qc, kv, name):
    S, D = qc.shape
    M = kv.shape[0]
    hd = D // CROSS_HEADS
    tq = min(CROSS_ROWS, S)

    def body(q_ref, kv_ref, o_ref):
        for h in range(CROSS_HEADS):
            p = _cross_probs(q_ref[:, h * hd:(h + 1) * hd], kv_ref[:, h * hd:(h + 1) * hd], hd ** -0.5)
            v = kv_ref[:, D + h * hd:D + (h + 1) * hd].astype(MXU_DTYPE)
            o_ref[:, h * hd:(h + 1) * hd] = jnp.dot(p.astype(MXU_DTYPE), v, preferred_element_type=F32)

    return pl.pallas_call(
        body, name=name, grid=(S // tq,), in_specs=[pl.BlockSpec((tq, D), lambda i: (i, 0)), pl.BlockSpec((M, 2 * D), lambda i: (0, 0))],
        out_specs=pl.BlockSpec((tq, D), lambda i: (i, 0)), out_shape=jax.ShapeDtypeStruct((S, D), F32),
        compiler_params=_params(("parallel",)),
    )(qc, kv)


def cross_bwd(qc, kv, do, name):
    S, D = qc.shape
    M = kv.shape[0]
    hd = D // CROSS_HEADS
    tq = min(CROSS_ROWS, S)

    def body(q_ref, kv_ref, do_ref, dq_ref, dkv_ref):
        @pl.when(pl.program_id(0) == 0)
        def _():
            dkv_ref[...] = jnp.zeros_like(dkv_ref)

        for h in range(CROSS_HEADS):
            q = q_ref[:, h * hd:(h + 1) * hd]
            k = kv_ref[:, h * hd:(h + 1) * hd]
            v = kv_ref[:, D + h * hd:D + (h + 1) * hd].astype(MXU_DTYPE)
            dom = do_ref[:, h * hd:(h + 1) * hd].astype(MXU_DTYPE)
            p = _cross_probs(q, k, hd ** -0.5)
            dp = lax.dot_general(dom, v, (((1,), (1,)), ((), ())), preferred_element_type=F32)
            ds = p * (dp - jnp.sum(p * dp, axis=1, keepdims=True)) * (hd ** -0.5)
            dq_ref[:, h * hd:(h + 1) * hd] = jnp.dot(ds.astype(MXU_DTYPE), k.astype(MXU_DTYPE), preferred_element_type=F32)
            dkv_ref[:, h * hd:(h + 1) * hd] += jnp.dot(ds.T.astype(MXU_DTYPE), q.astype(MXU_DTYPE), preferred_element_type=F32)
            dkv_ref[:, D + h * hd:D + (h + 1) * hd] += jnp.dot(p.T.astype(MXU_DTYPE), dom, preferred_element_type=F32)

    row = pl.BlockSpec((tq, D), lambda i: (i, 0))
    full = pl.BlockSpec((M, 2 * D), lambda i: (0, 0))
    return pl.pallas_call(
        body, name=name, grid=(S // tq,), in_specs=[row, full, row], out_specs=[row, full],
        out_shape=[jax.ShapeDtypeStruct((S, D), F32), jax.ShapeDtypeStruct((M, 2 * D), F32)],
        compiler_params=_params(("arbitrary",)),
    )(qc, kv, do)


def adamw(w, g, m, v, name):
    shape = w.shape
    cols = shape[-1]
    rows = int(np.prod(shape[:-1]))
    w2, g2, m2, v2 = (t.reshape(rows, cols) for t in (w, g, m, v))
    tr = _divisors(rows, SUBLANES, max(SUBLANES, (1 << 20) // (cols * 4) // SUBLANES * SUBLANES))[0]

    def body(w_ref, g_ref, m_ref, v_ref, d_ref, mo_ref, vo_ref, go_ref):
        gg = g_ref[...]
        mn = ADAM_B1 * m_ref[...] + (1.0 - ADAM_B1) * gg
        vn = ADAM_B2 * v_ref[...] + (1.0 - ADAM_B2) * (gg * gg)
        m_hat = mn / (1.0 - ADAM_B1 ** ADAM_STEP)
        v_hat = vn / (1.0 - ADAM_B2 ** ADAM_STEP)
        d_ref[...] = -ADAM_LR * (m_hat / (jnp.sqrt(v_hat) + ADAM_EPS) + ADAM_WD * w_ref[...])
        mo_ref[...] = mn
        vo_ref[...] = vn
        go_ref[...] = gg

    blk = pl.BlockSpec((tr, cols), lambda i: (i, 0))
    sds = jax.ShapeDtypeStruct((rows, cols), F32)
    d, mn, vn, go = pl.pallas_call(body, name=name, grid=(rows // tr,), in_specs=[blk] * 4, out_specs=[blk] * 4, out_shape=[sds] * 4,
                                   compiler_params=_params(("parallel",)))(w2, g2, m2, v2)
    return d.reshape(shape), mn.reshape(shape), vn.reshape(shape), go.reshape(shape)


def sum_devices(parts, name):
    n, rows, cols = parts.shape

    def body(p_ref, o_ref):
        acc = p_ref[0]
        for k in range(1, n):
            acc = acc + p_ref[k]
        o_ref[...] = acc

    return pl.pallas_call(body, name=name, in_specs=[pl.BlockSpec(memory_space=pltpu.VMEM)],
                          out_specs=pl.BlockSpec(memory_space=pltpu.VMEM), out_shape=jax.ShapeDtypeStruct((rows, cols), F32))(parts)


HBM_SPEC = pl.BlockSpec(memory_space=pltpu.HBM)


def _place():
    return lax.axis_index("x"), lax.axis_index("y"), lax.axis_index("c")


def _remote(src, dst, send_sems, recv_sems, k, to):
    return pltpu.make_async_remote_copy(src_ref=src, dst_ref=dst, send_sem=send_sems.at[k], recv_sem=recv_sems.at[k],
                                        device_id=to, device_id_type=MESH_ID)


SEM_SPEC = pl.BlockSpec(memory_space=pltpu.SEMAPHORE)
ANY_SPEC = pl.BlockSpec(memory_space=pl.ANY)
SPLIT_COPY = pltpu.CompilerParams(has_side_effects=pltpu.SideEffectType.DATAFLOW_SIDE_EFFECTING)


def _in_hbm(arrays):
    return [pltpu.with_memory_space_constraint(a, pltpu.HBM) for a in arrays]


def _split_start(copies, sources, lands, after, n_sems, name):
    n = len(sources)

    def body(*refs):
        for cp in copies(refs[:n], refs[n:2 * n], refs[2 * n + 1], refs[2 * n + 2]):
            cp.start()
        refs[-1][...] = jnp.zeros_like(refs[-1])

    through = [pltpu.HBM(a.shape, a.dtype) for a in list(sources) + list(lands)]
    outs = pl.pallas_call(
        body, name=name, in_specs=[HBM_SPEC] * (2 * n) + [ANY_SPEC],
        out_specs=[SEM_SPEC, SEM_SPEC] + [HBM_SPEC] * (2 * n) + [pl.BlockSpec(memory_space=pltpu.VMEM)],
        out_shape=[pltpu.SemaphoreType.DMA((n_sems,)), pltpu.SemaphoreType.DMA((n_sems,))] + through
        + [jax.ShapeDtypeStruct((SUBLANES, LANES), F32)],
        input_output_aliases={i: 2 + i for i in range(2 * n)}, compiler_params=SPLIT_COPY,
    )(*_in_hbm(sources), *_in_hbm(lands), after)
    return outs[0], outs[1], outs[2:2 + n], outs[2 + n:2 + 2 * n], outs[-1]


def _split_wait(copies, send_sems, recv_sems, sources, lands, after, name):
    n = len(sources)

    def body(*refs):
        for cp in copies(refs[:n], refs[n:2 * n], refs[2 * n], refs[2 * n + 1]):
            cp.wait_send()
            cp.wait_recv()

    through = [pltpu.HBM(a.shape, a.dtype) for a in list(sources) + list(lands)]
    outs = pl.pallas_call(
        body, name=name, in_specs=[HBM_SPEC] * (2 * n) + [SEM_SPEC, SEM_SPEC, ANY_SPEC], out_specs=[HBM_SPEC] * (2 * n),
        out_shape=through, input_output_aliases={i: i for i in range(2 * n)}, compiler_params=SPLIT_COPY,
    )(*sources, *lands, send_sems, recv_sems, after)
    return outs[:n], outs[n:]


def _chip_slab(land, slot, rows):
    return land.at[slot, rows] if len(land.shape) == 3 else land.at[rows, slot]


def _gather_copies(w_refs, land_refs, send_sems, recv_sems):
    n = len(w_refs)
    x, y, c = _place()
    chips = [(1 - x, y), (x, 1 - y), (1 - x, 1 - y)]
    cps = []
    for a in range(n):
        hr = w_refs[a].shape[0] // 2
        mine, every = pl.ds(c * hr, hr), pl.ds(0, 2 * hr)
        cps.append(_remote(w_refs[a], _chip_slab(land_refs[a], 2 * x + y, every), send_sems, recv_sems, 3 * n + a, (x, y, 1 - c)))
        for k, chip in enumerate(chips):
            cps.append(_remote(w_refs[a].at[mine], _chip_slab(land_refs[a], 2 * x + y, mine), send_sems, recv_sems, 3 * a + k, (*chip, c)))
    return cps


def gather_start(shards, after, name):
    lands = [lax.empty(s.shape[:-2] + (N_CHIPS,) + s.shape[-2:], s.dtype) for s in shards]
    return _split_start(_gather_copies, shards, lands, after, 4 * len(shards), name)


def gather_wait(state, after, name):
    send_sems, recv_sems, sources, lands, _ = state
    return _split_wait(_gather_copies, send_sems, recv_sems, sources, lands, after, name)[1]


def gather_pass(lands, name):
    n = len(lands)

    def body(*refs):
        out_refs, send_sems, recv_sems = refs[n:2 * n], refs[2 * n], refs[2 * n + 1]
        x, y, c = _place()
        chips = [(1 - x, y), (x, 1 - y), (1 - x, 1 - y)]
        sent = []
        for a in range(n):
            hr = out_refs[a].shape[0 if len(out_refs[a].shape) == 4 else 1] // 2
            for k, (px, py) in enumerate(chips):
                landed = _chip_slab(out_refs[a], 2 * px + py, pl.ds(c * hr, hr))
                sent.append(_remote(landed, landed, send_sems, recv_sems, 3 * a + k, (x, y, 1 - c)))
        for cp in sent:
            cp.start()
        for a in range(n):
            hr = out_refs[a].shape[0 if len(out_refs[a].shape) == 4 else 1] // 2
            for k, (px, py) in enumerate(chips):
                theirs = _chip_slab(out_refs[a], 2 * px + py, pl.ds((1 - c) * hr, hr))
                _remote(theirs, theirs, send_sems, recv_sems, 3 * a + k, (x, y, 1 - c)).wait_recv()
        for cp in sent:
            cp.wait_send()

    return pl.pallas_call(
        body, name=name, in_specs=[HBM_SPEC] * n, out_specs=[HBM_SPEC] * n,
        out_shape=[jax.ShapeDtypeStruct(a.shape, a.dtype) for a in lands], input_output_aliases={a: a for a in range(n)},
        scratch_shapes=[pltpu.SemaphoreType.DMA((3 * n,))] * 2,
    )(*lands)


def _scatter_copies(t_refs, land_refs, send_sems, recv_sems):
    x, y, c = _place()
    chips = [(1 - x, y), (x, 1 - y), (1 - x, 1 - y)]
    return [_remote(t_refs[a].at[:, 2 * px + py], land_refs[a].at[:, k], send_sems, recv_sems, 3 * a + k, (px, py, c))
            for a in range(len(t_refs)) for k, (px, py) in enumerate(chips)]


def scatter_start(parts, after, name):
    lands = [lax.empty((t.shape[0], N_CHIPS - 1) + t.shape[2:], t.dtype) for t in parts]
    return _split_start(_scatter_copies, parts, lands, after, 3 * len(parts), name)


def scatter_wait(state, after, name):
    send_sems, recv_sems, sources, lands, _ = state
    return _split_wait(_scatter_copies, send_sems, recv_sems, sources, lands, after, name)


def swap_sibling(parts, name):
    n = len(parts)

    def body(*refs):
        v_refs, out_refs, send_sems, recv_sems = refs[:n], refs[n:2 * n], refs[2 * n], refs[2 * n + 1]
        x, y, c = _place()
        cps = []
        for a in range(n):
            hr = v_refs[a].shape[2] // 2
            cps.append(_remote(v_refs[a].at[:, :, pl.ds((1 - c) * hr, hr)], out_refs[a], send_sems, recv_sems, a, (x, y, 1 - c)))
        for cp in cps:
            cp.start()
        for cp in cps:
            cp.wait()

    return pl.pallas_call(
        body, name=name, in_specs=[HBM_SPEC] * n, out_specs=[HBM_SPEC] * n,
        out_shape=[jax.ShapeDtypeStruct(v.shape[:2] + (v.shape[2] // 2, v.shape[3]), v.dtype) for v in parts],
        scratch_shapes=[pltpu.SemaphoreType.DMA((n,))] * 2,
    )(*parts)


def join_halves(halves, layer, name):
    n = len(halves)

    def body(*refs):
        out_refs, send_sems, recv_sems = refs[n:2 * n], refs[2 * n], refs[2 * n + 1]
        x, y, c = _place()
        cps = []
        for a in range(n):
            hr = out_refs[a].shape[1] // 2
            mine = out_refs[a].at[layer, pl.ds(c * hr, hr)]
            cps.append(_remote(mine, mine, send_sems, recv_sems, a, (x, y, 1 - c)))
        for cp in cps:
            cp.start()
        for a in range(n):
            hr = out_refs[a].shape[1] // 2
            theirs = out_refs[a].at[layer, pl.ds((1 - c) * hr, hr)]
            _remote(theirs, theirs, send_sems, recv_sems, a, (x, y, 1 - c)).wait_recv()
        for cp in cps:
            cp.wait_send()

    return pl.pallas_call(
        body, name=name, in_specs=[HBM_SPEC] * n, out_specs=[HBM_SPEC] * n,
        out_shape=[jax.ShapeDtypeStruct(f.shape, f.dtype) for f in halves], input_output_aliases={a: a for a in range(n)},
        scratch_shapes=[pltpu.SemaphoreType.DMA((n,))] * 2,
    )(*halves)


def gather_devices(v, name):
    def body(v_ref, out_ref, send_sems, recv_sems, local_sem):
        x, y, c = _place()
        me = 4 * x + 2 * y + c
        own = pltpu.make_async_copy(v_ref, out_ref.at[me], local_sem)
        own.start()
        peers = [((x + dx) % 2, (y + dy) % 2, (c + dc) % 2) for dx in (0, 1) for dy in (0, 1) for dc in (0, 1)][1:]
        sent = []
        for k, peer in enumerate(peers):
            cp = pltpu.make_async_remote_copy(src_ref=v_ref, dst_ref=out_ref.at[me], send_sem=send_sems.at[k], recv_sem=recv_sems.at[k],
                                              device_id=peer, device_id_type=MESH_ID)
            cp.start()
            sent.append(cp)
        for k, (px, py, pc) in enumerate(peers):
            slot = out_ref.at[4 * px + 2 * py + pc]
            pltpu.make_async_remote_copy(src_ref=slot, dst_ref=slot, send_sem=send_sems.at[k], recv_sem=recv_sems.at[k],
                                         device_id=(px, py, pc), device_id_type=MESH_ID).wait_recv()
        for cp in sent:
            cp.wait_send()
        own.wait()

    vm = pl.BlockSpec(memory_space=pltpu.VMEM)
    return pl.pallas_call(body, name=name, in_specs=[vm], out_specs=vm, out_shape=jax.ShapeDtypeStruct((N_DEV,) + v.shape, v.dtype),
                          scratch_shapes=[pltpu.SemaphoreType.DMA((N_DEV - 1,)), pltpu.SemaphoreType.DMA((N_DEV - 1,)),
                                          pltpu.SemaphoreType.DMA])(v)


ADD_ROWS = 512


def add_pair(place, a, b, name):
    L, n, hr, cols = b.shape
    tr = _divisors(hr, 2 * SUBLANES, ADD_ROWS)[0]
    nb = hr // tr

    def body(p_ref, a_ref, b_ref, o_ref):
        del p_ref
        o_ref[...] = (a_ref[...].astype(F32) + b_ref[...].astype(F32)).astype(o_ref.dtype)

    blk = pl.BlockSpec((None, None, tr, cols), lambda l, d, i, p: (l, d, i, 0))
    grid_spec = pltpu.PrefetchScalarGridSpec(
        num_scalar_prefetch=1, grid=(L, n, nb),
        in_specs=[pl.BlockSpec((None, None, tr, cols), lambda l, d, i, p: (l, d, p[0] * nb + i, 0)), blk], out_specs=blk)
    return pl.pallas_call(body, name=name, grid_spec=grid_spec, out_shape=jax.ShapeDtypeStruct(b.shape, b.dtype),
                          compiler_params=_params(("parallel", "parallel", "parallel")))(place, a, b)


def add_chips(place, own, others, layer, stacked, name):
    _, n, hr, cols = others.shape
    tr = _divisors(hr, 2 * SUBLANES, ADD_ROWS)[0]
    nb = hr // tr
    create = isinstance(stacked, tuple)

    def body(p_ref, own_ref, *refs):
        del p_ref
        acc = own_ref[...].astype(F32)
        for k in range(n):
            acc = acc + refs[k][...].astype(F32)
        refs[-1][...] = acc

    ins = [pl.BlockSpec((None, None, tr, cols), lambda i, p: (0, p[1], i, 0))]
    ins += [pl.BlockSpec((None, None, tr, cols), functools.partial(lambda k, i, p: (0, k, i, 0), k)) for k in range(n)]
    grid_spec = pltpu.PrefetchScalarGridSpec(num_scalar_prefetch=1, grid=(nb,), in_specs=ins + ([] if create else [ANY_SPEC]),
                                             out_specs=pl.BlockSpec((None, tr, cols), lambda i, p: (layer, p[0] * nb + i, 0)))
    shape = stacked if create else stacked.shape
    return pl.pallas_call(body, name=name, grid_spec=grid_spec, out_shape=jax.ShapeDtypeStruct(shape, F32),
                          input_output_aliases={} if create else {n + 2: 0},
                          compiler_params=_params(("parallel",)))(place, own, *([others] * n), *([] if create else [stacked]))


def _alpha(depth):
    return (2 * depth) ** 0.25


def _wmm(a, weight, mode, name, deps=()):
    arr, how = weight
    return mm(a, arr, mode, name, deps=deps, **how)


def layer_fwd(h, mem, w, tab, alpha, deps=()):
    D = h.shape[1]
    proj = _wmm(h, w["w_in"], "nn", "mm_proj", deps)
    xc, r, ig, a, b = rg_gates_fwd(proj, w["conv_w"], w["conv_b"], w["w_rg"], w["b_rg"], w["w_ig"], w["b_ig"], w["lru_lambda"], "rg_gates_fwd")
    hs, y_rnn = rg_scan_fwd(proj, a, b, "rg_scan_fwd")
    y_attn = attn_fwd(proj, w["sinks"], tab, D, "attn_fwd")
    pr = _wmm(y_rnn, w["w_br_rnn"], "nn", "mm_br_rnn")
    pa = _wmm(y_attn, w["w_br_attn"], "nn", "mm_br_attn")
    merged = merge_fwd(proj, pr, pa, "merge_fwd")
    mix = _wmm(merged, w["w_out"], "nn", "mm_out")
    h1, xh1, rs1 = ln_fwd(h, mix, w["ln1_g"], w["ln1_b"], alpha, "ln1_fwd")
    qc = _wmm(h1, w["cq_w"], "nn", "mm_cq")
    kv = _wmm(mem, w["ckv_w"], "nn", "mm_ckv")
    o = cross_fwd(qc, kv, "cross_fwd")
    co = _wmm(o, w["co_w"], "nn", "mm_co")
    h2, xh2, rs2 = ln_fwd(h1, co, w["ln2_g"], w["ln2_b"], alpha, "ln2_fwd")
    gu = _wmm(h2, w["ffn_wi"], "nn", "mm_ffn_wi")
    act = swiglu_fwd(gu, "swiglu_fwd")
    f = _wmm(act, w["ffn_wo"], "nn", "mm_ffn_wo")
    h3, xh3, rs3 = ln_fwd(h2, f, w["ln3_g"], w["ln3_b"], alpha, "ln3_fwd")
    saved = dict(h=h, proj=proj, xc=xc, r=r, ig=ig, a=a, hs=hs, y_rnn=y_rnn, y_attn=y_attn, pr=pr, pa=pa, xh1=xh1, rs1=rs1, h1=h1,
                 qc=qc, kv=kv, o=o, xh2=xh2, rs2=rs2, h2=h2, gu=gu, xh3=xh3, rs3=rs3)
    return h3, saved


def layer_bwd(dh, mem, w, s, tab, alpha, deps=()):
    D = dh.shape[1]
    g = {}
    wg = dict(out_dtype=MXU_DTYPE)
    dz3, g["ln3_g"], g["ln3_b"] = ln_bwd(dh, None, s["xh3"], s["rs3"], w["ln3_g"], 1.0, "ln3_bwd")
    act = swiglu_fwd(s["gu"], "swiglu_refwd")
    g["ffn_wo"] = mm(act, dz3, "tn", "mm_d_ffn_wo", deps=deps, **wg)
    dact = _wmm(dz3, w["ffn_wo"], "nt", "mm_dact")
    dgu = swiglu_bwd(s["gu"], dact, "swiglu_bwd")
    g["ffn_wi"] = mm(s["h2"], dgu, "tn", "mm_d_ffn_wi", out_chips=True, **wg)
    dh2 = _wmm(dgu, w["ffn_wi"], "nt", "mm_dh2")
    dz2, g["ln2_g"], g["ln2_b"] = ln_bwd(dz3, dh2, s["xh2"], s["rs2"], w["ln2_g"], alpha, "ln2_bwd")
    g["co_w"] = mm(s["o"], dz2, "tn", "mm_d_co", **wg)
    do = _wmm(dz2, w["co_w"], "nt", "mm_do")
    dqc, dkv = cross_bwd(s["qc"], s["kv"], do, "cross_bwd")
    g["cq_w"] = mm(s["h1"], dqc, "tn", "mm_d_cq", **wg)
    g["ckv_w"] = mm(mem, dkv, "tn", "mm_d_ckv", out_chips=True, **wg)
    dh1 = _wmm(dqc, w["cq_w"], "nt", "mm_dh1")
    dz1, g["ln1_g"], g["ln1_b"] = ln_bwd(dz2, dh1, s["xh1"], s["rs1"], w["ln1_g"], alpha, "ln1_bwd")
    merged = merge_fwd(s["proj"], s["pr"], s["pa"], "merge_refwd")
    g["w_out"] = mm(merged, dz1, "tn", "mm_d_out", **wg)
    dm = _wmm(dz1, w["w_out"], "nt", "mm_dmerged")
    dpr, dpa, dg_rnn, dg_attn = merge_bwd(s["proj"], s["pr"], s["pa"], dm, "merge_bwd")
    g["w_br_rnn"] = mm(s["y_rnn"], dpr, "tn", "mm_d_br_rnn", **wg)
    g["w_br_attn"] = mm(s["y_attn"], dpa, "tn", "mm_d_br_attn", **wg)
    dy_rnn = _wmm(dpr, w["w_br_rnn"], "nt", "mm_dy_rnn")
    dy_attn = _wmm(dpa, w["w_br_attn"], "nt", "mm_dy_attn")
    dq, dkb, dvb, dsink = attn_bwd(s["proj"], w["sinks"], tab, s["y_attn"], dy_attn, D, "attn_bwd")
    dk, dv = band_fold(dkb, dvb, "band_fold")
    g["sinks"] = dsink[:, :w["sinks"].shape[0]]
    dgr, gt = rg_scan_bwd(s["proj"], dy_rnn, s["hs"], s["a"], "rg_scan_bwd")
    dxc, g["w_rg"], g["w_ig"], g["b_rg"], g["b_ig"], g["lru_lambda"] = rg_gates_bwd(
        gt, s["hs"], s["xc"], s["r"], s["ig"], w["w_rg"], w["w_ig"], w["lru_lambda"], "rg_gates_bwd")
    dxr, g["conv_w"], g["conv_b"] = rg_conv_bwd(s["proj"], dxc, w["conv_w"], "rg_conv_bwd")
    dproj = jnp.concatenate([dxr, dgr, dq, dk, dv, dg_rnn, dg_attn], axis=1)
    g["w_in"] = mm(s["h"], dproj, "tn", "mm_d_in")
    dhm = _wmm(dproj, w["w_in"], "nt", "mm_dh")
    return axpby(dz1, dhm, alpha, "layer_dx"), g


def local_step(x, mem, target, depth, weights_of, grads_done):
    alpha = _alpha(depth)
    tab = rope_table(x.shape[0])
    h, saved, layers = x, [], []
    for l in range(depth):
        wl, deps = weights_of(l, h)
        h, s = layer_fwd(h, mem, wl, tab, alpha, deps)
        layers.append(wl)
        saved.append(s)
    dh, loss = loss_head(h, target, "loss_head")
    deps = ()
    for l in reversed(range(depth)):
        dh, g = layer_bwd(dh, mem, layers[l], saved[l], tab, alpha, deps)
        deps = grads_done(l, g, dh)
    return loss, dh


def _pad_rows(flat):
    n = flat.shape[0]
    rows = -(-n // (LANES * SUBLANES)) * SUBLANES
    return jnp.pad(flat, (0, rows * LANES - n)).reshape(rows, LANES)


def kernel(x, mem, w_in, conv_w, conv_b, w_rg, b_rg, w_ig, b_ig, lru_lambda, w_br_rnn, w_br_attn, sinks, w_out, ln1_g, ln1_b, cq_w, ckv_w, co_w, ln2_g, ln2_b, ffn_wi, ffn_wo, ln3_g, ln3_b, loss_target, m_w_in, m_conv_w, m_conv_b, m_w_rg, m_b_rg, m_w_ig, m_b_ig, m_lru_lambda, m_w_br_rnn, m_w_br_attn, m_sinks, m_w_out, m_ln1_g, m_ln1_b, m_cq_w, m_ckv_w, m_co_w, m_ln2_g, m_ln2_b, m_ffn_wi, m_ffn_wo, m_ln3_g, m_ln3_b, v_w_in, v_conv_w, v_conv_b, v_w_rg, v_b_rg, v_w_ig, v_b_ig, v_lru_lambda, v_w_br_rnn, v_w_br_attn, v_sinks, v_w_out, v_ln1_g, v_ln1_b, v_cq_w, v_ckv_w, v_co_w, v_ln2_g, v_ln2_b, v_ffn_wi, v_ffn_wo, v_ln3_g, v_ln3_b):
    args = dict(locals())
    w = {n: args[n] for n in WEIGHTS}
    m = {n: args["m_" + n] for n in WEIGHTS}
    v = {n: args["v_" + n] for n in WEIGHTS}
    cx, cy, cc = _place()
    chip = 2 * cx + cy
    L = w_in.shape[0]

    place = jnp.stack([cc, chip]).astype(jnp.int32)
    cw_rows = _pad_rows(conv_w.reshape(-1))
    cw_all = gather_devices(cw_rows, "gather_conv_w")[0::2]
    cw_parts = cw_all.reshape(N_CHIPS, -1)[:, :conv_w.size].reshape((N_CHIPS,) + conv_w.shape)
    conv_full = jnp.concatenate([cw_parts[k] for k in range(N_CHIPS)], axis=2)

    shards = [[w[n][l].astype(MXU_DTYPE) for n in BIG] for l in range(L)]
    gathering = {0: gather_start(shards[0], cw_rows, "gather_start_0")}

    def weights_of(l, h):
        lands = gather_pass(gather_wait(gathering.pop(l), h, f"gather_wait_{l}"), f"gather_pass_{l}")
        deps = ()
        if l + 1 < L:
            gathering[l + 1] = gather_start(shards[l + 1], lands[0], f"gather_start_{l + 1}")
            deps = (gathering[l + 1][4],)
        wl = {}
        for n, gw in zip(BIG, lands):
            rows_joined = gw.reshape(gw.shape[:-3] + (-1, gw.shape[-1]))
            if n == "w_in":
                wl[n] = (jnp.concatenate([gw[k] for k in range(N_CHIPS)], axis=1), {})
            elif n in COL_BLOCKED:
                wl[n] = (gw, dict(chips=True))
            elif n in GATE_WEIGHTS:
                wl[n] = rows_joined
            else:
                wl[n] = (rows_joined, {})
        for n in SMALL:
            wl[n] = conv_full[l] if n == "conv_w" else w[n][l] if n == "sinks" else w[n][l][None, :]
        return wl, deps

    def for_chips(n, g):
        if n in COL_BLOCKED:
            return g
        if n in GATE_WEIGHTS:
            nb, bw, _ = g.shape
            g = g.reshape(nb, N_CHIPS, bw // N_CHIPS, bw).transpose(1, 0, 2, 3).reshape(N_CHIPS, nb * bw // N_CHIPS, bw)
        elif SHARD_AXIS[n] == 0:
            g = g.reshape(N_CHIPS, g.shape[0] // N_CHIPS, g.shape[1])
        else:
            g = jnp.stack(jnp.split(g, N_CHIPS, axis=1))
        return g.astype(MXU_DTYPE)

    reduced, scattering, small_grads = {}, {}, [None] * L

    def finish_layer(l, after):
        chip_sums, from_chips = scatter_wait(scattering.pop(l), after, f"grad_scatter_wait_{l}")
        for n, own, others in zip(BIG, chip_sums, from_chips):
            target = reduced.get(n, (L, 2 * own.shape[2], own.shape[3]))
            reduced[n] = add_chips(place, own, others, l, target, f"grad_add_chips_{n}_{l}")
        reduced.update(zip(BIG, join_halves([reduced[n] for n in BIG], l, f"grad_join_{l}")))

    def grads_done(l, g, dh):
        if l + 1 in scattering:
            finish_layer(l + 1, dh)
        small_grads[l] = {n: g[n] for n in SMALL}
        partial_sums = [for_chips(n, g[n])[None] for n in BIG]
        from_sibling = swap_sibling(partial_sums, f"grad_to_sibling_{l}")
        chip_sums = [add_pair(place, a, b, f"grad_add_pair_{n}_{l}") for n, a, b in zip(BIG, partial_sums, from_sibling)]
        scattering[l] = scatter_start(chip_sums, dh, f"grad_scatter_start_{l}")
        return (scattering[l][4],)

    loss11, dx = local_step(x[0], mem[0], loss_target[0], L, weights_of, grads_done)
    finish_layer(0, dx)
    loss = lax.psum(loss11[0, 0], ("x", "y", "c"))
    gshard = {n: reduced[n].reshape(w[n].shape) for n in BIG}

    small_full = {n: jnp.stack([gl[n] for gl in small_grads]).reshape(w[n].shape[:1] + ((CONV_WIDTH, -1) if n == "conv_w" else (-1,)))
                  for n in SMALL}
    small_flat = jnp.concatenate([small_full[n].reshape(-1) for n in SMALL])
    small_sum = sum_devices(gather_devices(_pad_rows(small_flat), "gather_small_grads"), "sum_small_grads").reshape(-1)
    off = 0
    for n in SMALL:
        gfull = small_sum[off:off + small_full[n].size].reshape(small_full[n].shape)
        off += small_full[n].size
        if n == "conv_w":
            width = conv_w.shape[2]
            gfull = lax.dynamic_slice_in_dim(gfull, chip * width, width, axis=2)
        gshard[n] = gfull

    delta, new_m, new_v, grad = {}, {}, {}, {}
    for n in WEIGHTS:
        delta[n], new_m[n], new_v[n], grad[n] = adamw(w[n], gshard[n], m[n], v[n], "adamw_" + n)
    return (loss, dx[None], *[grad[n] for n in WEIGHTS], *[delta[n] for n in WEIGHTS], *[new_m[n] for n in WEIGHTS],
            *[new_v[n] for n in WEIGHTS])
```

```python
import functools
import math

import jax
import jax.numpy as jnp
import numpy as np
from jax import lax
from jax.experimental import pallas as pl
from jax.experimental.pallas import tpu as pltpu

F32 = jnp.float32
BF16 = jnp.bfloat16
MXU_DTYPE = BF16

HEAD_DIM = 64
N_KV_HEADS = 2
WINDOW = 128
ROT_DIM = HEAD_DIM // 4
ROPE_THETA = 500000.0
CROSS_HEADS = 4
RNN_BLOCKS = 4
CONV_WIDTH = 4
LRU_C = 8.0
LN_EPS = 1e-5
NEG_INF = -1e30
ADAM_LR = 0.001
ADAM_B1 = 0.9
ADAM_B2 = 0.999
ADAM_EPS = 1e-08
ADAM_WD = 0.01
ADAM_STEP = 10

VMEM_BYTES_V7X = 64 * 1024 * 1024
VMEM_BLOCK_BUDGET = 36 * 1024 * 1024
LANES = 128
SUBLANES = 8

MESH_ID = pl.DeviceIdType.MESH
N_CHIPS = 4
N_DEV = 8

BIG = ("w_in", "w_rg", "w_ig", "w_br_rnn", "w_br_attn", "w_out", "cq_w", "ckv_w", "co_w", "ffn_wi", "ffn_wo")
SHARD_AXIS = {"w_in": 1, "w_rg": 1, "w_ig": 1, "w_br_rnn": 0, "w_br_attn": 0, "w_out": 0, "cq_w": 0, "ckv_w": 1,
              "co_w": 0, "ffn_wi": 1, "ffn_wo": 0}
SMALL = ("conv_w", "conv_b", "b_rg", "b_ig", "lru_lambda", "sinks", "ln1_g", "ln1_b", "ln2_g", "ln2_b", "ln3_g", "ln3_b")
WEIGHTS = ("w_in", "conv_w", "conv_b", "w_rg", "b_rg", "w_ig", "b_ig", "lru_lambda", "w_br_rnn", "w_br_attn", "sinks",
           "w_out", "ln1_g", "ln1_b", "cq_w", "ckv_w", "co_w", "ln2_g", "ln2_b", "ffn_wi", "ffn_wo", "ln3_g", "ln3_b")
GATE_WEIGHTS = ("w_rg", "w_ig")
COL_BLOCKED = ("ckv_w", "ffn_wi")
GATHER_FIRST = ("w_in", "w_rg", "w_ig")
SCATTER_FIRST = ("ffn_wo", "ffn_wi", "co_w", "cq_w", "ckv_w")


def _params(dims=None, vmem=None):
    return pltpu.CompilerParams(dimension_semantics=dims, vmem_limit_bytes=vmem)


def _vmem_limit(block_bytes, temp_bytes=0):
    want = int(2 * block_bytes + temp_bytes) + (6 << 20)
    return max(32 << 20, min(want, VMEM_BYTES_V7X - (6 << 20)))


def _divisors(n, align, cap):
    out = [d for d in range(align, min(n, cap) + 1, align) if n % d == 0]
    if n <= cap and n not in out:
        out.append(n)
    return sorted(out, reverse=True) or [n]


def _sigmoid(x):
    return 1.0 / (1.0 + jnp.exp(-x))


def _gelu_parts(x):
    c = math.sqrt(2.0 / math.pi)
    u = c * (x + 0.044715 * x * x * x)
    t = jnp.tanh(u)
    return t, c * (1.0 + 3 * 0.044715 * x * x)


def _gelu(x):
    t, _ = _gelu_parts(x)
    return 0.5 * x * (1.0 + t)


def _gelu_grad(x):
    t, du = _gelu_parts(x)
    return 0.5 * (1.0 + t) + 0.5 * x * (1.0 - t * t) * du


def _neg_expm1(x):
    series = x * (1.0 + x * (0.5 + x * (1.0 / 6 + x * (1.0 / 24 + x * (1.0 / 120)))))
    return -jnp.where(x > -0.1, series, jnp.exp(x) - 1.0)


def _softplus_neg(lam):
    x = -lam
    return jnp.maximum(x, 0.0) + jnp.log1p(jnp.exp(-jnp.abs(x)))


STEP_US = 0.35
HBM_BYTES_PER_US = 2.5e6
MXU_FLOPS_PER_US = 7e8


def mm(a, b, mode, name, *, b_index=(), a_blocks=0, b_blocks=0, out_blocks=0, out_dtype=F32, deps=()):
    nlead = len(b_index) + (1 if b_blocks else 0)
    bk, bn = b.shape[nlead:]
    M, K = (a.shape[-1], a.shape[-2]) if mode == "tn" else (a.shape[-2], a.shape[-1] * max(a_blocks, 1))
    N = bk if mode == "nt" else bn * max(b_blocks, 1) if mode == "nn" or mode == "tn" else bn
    asz, bsz, osz = a.dtype.itemsize, b.dtype.itemsize, jnp.dtype(out_dtype).itemsize
    n_unit = math.gcd(N // max(out_blocks, 1), N // max(b_blocks, 1) if mode != "nt" else N)
    k_unit = math.gcd(K // max(a_blocks, 1), K // max(b_blocks, 1) if mode == "nt" else K)
    tms = _divisors(M, LANES if mode == "tn" else SUBLANES, 2048)
    tns = _divisors(n_unit, LANES, 2048)
    tks = _divisors(k_unit, LANES, k_unit)
    best = None
    for tm in tms:
        for tn in tns:
            for tk in tks:
                nk = K // tk
                scratch = tm * tn * 4 if (nk > 1 and osz != 4) else 0
                blocks = tm * tk * asz + tn * tk * bsz + tm * tn * osz
                temps = tm * tk * (2 + (4 if mode == "tn" else 0)) + tn * tk * 2 + tm * tn * 4 + scratch
                if 2 * blocks + temps > VMEM_BLOCK_BUDGET + (8 << 20):
                    continue
                ni, nj = M // tm, N // tn
                traffic = M * K * asz * (nj if nk > 1 else 1) + N * K * bsz * (1 if nj * nk == 1 else ni) + M * N * osz
                busy = max(traffic / HBM_BYTES_PER_US, 2.0 * M * N * K / MXU_FLOPS_PER_US)
                cost = ni * nj * nk * STEP_US + busy + blocks / HBM_BYTES_PER_US
                if best is None or cost < best[0]:
                    best = (cost, tm, tn, tk, blocks, temps)
    _, tm, tn, tk, blocks, temps = best
    nk = K // tk
    use_scratch = nk > 1 and osz != 4

    def split(index, total, blocks, tile):
        per = total // blocks // tile
        return index // per, index % per

    def body(a_ref, b_ref, *rest):
        o_ref, acc = rest[len(deps)], rest[len(deps) + 1:]
        av = a_ref[...].astype(MXU_DTYPE)
        bv = b_ref[...].astype(MXU_DTYPE)
        dn = {"nn": (((1,), (0,)), ((), ())), "nt": (((1,), (1,)), ((), ())), "tn": (((0,), (0,)), ((), ()))}[mode]
        r = lax.dot_general(av, bv, dn, preferred_element_type=F32)
        if nk == 1:
            o_ref[...] = r.astype(o_ref.dtype)
        else:
            acc_ref = acc[0] if use_scratch else o_ref

            @pl.when(pl.program_id(2) == 0)
            def _():
                acc_ref[...] = r

            @pl.when(pl.program_id(2) > 0)
            def _():
                acc_ref[...] += r

            if use_scratch:
                @pl.when(pl.program_id(2) == nk - 1)
                def _():
                    o_ref[...] = acc_ref[...].astype(o_ref.dtype)

    if mode == "tn":
        a_spec = pl.BlockSpec((tk, tm), lambda i, j, k: (k, i))
    elif a_blocks:
        a_spec = pl.BlockSpec((None, tm, tk), lambda i, j, k: (split(k, K, a_blocks, tk)[0], i, split(k, K, a_blocks, tk)[1]))
    else:
        a_spec = pl.BlockSpec((tm, tk), lambda i, j, k: (i, k))
    lead = (None,) * nlead
    if mode == "nt":
        bmap = ((lambda i, j, k: b_index + (split(k, K, b_blocks, tk)[0], j, split(k, K, b_blocks, tk)[1])) if b_blocks
                else (lambda i, j, k: b_index + (j, k)))
        b_spec = pl.BlockSpec(lead + (tn, tk), bmap)
    else:
        bmap = ((lambda i, j, k: b_index + (split(j, N, b_blocks, tn)[0], k, split(j, N, b_blocks, tn)[1])) if b_blocks
                else (lambda i, j, k: b_index + (k, j)))
        b_spec = pl.BlockSpec(lead + (tk, tn), bmap)
    if out_blocks:
        o_spec = pl.BlockSpec((None, tm, tn), lambda i, j, k: (split(j, N, out_blocks, tn)[0], i, split(j, N, out_blocks, tn)[1]))
        o_shape = jax.ShapeDtypeStruct((out_blocks, M, N // out_blocks), out_dtype)
    else:
        o_spec = pl.BlockSpec((tm, tn), lambda i, j, k: (i, j))
        o_shape = jax.ShapeDtypeStruct((M, N), out_dtype)
    return pl.pallas_call(
        body, name=name, grid=(M // tm, N // tn, nk), in_specs=[a_spec, b_spec] + [pl.BlockSpec(memory_space=pl.ANY)] * len(deps),
        out_specs=o_spec, out_shape=o_shape, scratch_shapes=[pltpu.VMEM((tm, tn), F32)] if use_scratch else [],
        compiler_params=_params(("parallel", "parallel", "arbitrary"), _vmem_limit(blocks, temps)),
    )(a, b, *deps)


ROW_TILE = 512
GATE_ROWS = 1024


def ln_fwd(h, f, g, b, alpha, name):
    S, D = h.shape
    tr = min(ROW_TILE, S)

    def body(h_ref, f_ref, g_ref, b_ref, y_ref, xh_ref, rs_ref):
        z = alpha * h_ref[...] + f_ref[...]
        mu = jnp.mean(z, axis=-1, keepdims=True)
        zc = z - mu
        var = jnp.mean(zc * zc, axis=-1, keepdims=True)
        rs = lax.rsqrt(var + LN_EPS)
        xh = zc * rs
        y_ref[...] = xh * g_ref[...] + b_ref[...]
        xh_ref[...] = xh
        rs_ref[...] = rs

    row = pl.BlockSpec((tr, D), lambda i: (i, 0))
    vec = pl.BlockSpec((1, D), lambda i: (0, 0))
    return pl.pallas_call(
        body, name=name, grid=(S // tr,), in_specs=[row, row, vec, vec],
        out_specs=[row, row, pl.BlockSpec((tr, 1), lambda i: (i, 0))],
        out_shape=[jax.ShapeDtypeStruct((S, D), F32), jax.ShapeDtypeStruct((S, D), F32), jax.ShapeDtypeStruct((S, 1), F32)],
        compiler_params=_params(("parallel",), 48 << 20),
    )(h, f, g, b)


def ln_bwd(dy_a, dy_b, xh, rs, g, c1, name):
    S, D = xh.shape
    tr = min(ROW_TILE, S)
    two = dy_b is not None

    def body(*refs):
        if two:
            a_ref, b_ref, xh_ref, rs_ref, g_ref, dz_ref, dg_ref, db_ref = refs
            dy = c1 * a_ref[...] + b_ref[...]
        else:
            a_ref, xh_ref, rs_ref, g_ref, dz_ref, dg_ref, db_ref = refs
            dy = a_ref[...]
        x = xh_ref[...]
        dyg = dy * g_ref[...]
        m1 = jnp.mean(dyg, axis=-1, keepdims=True)
        m2 = jnp.mean(dyg * x, axis=-1, keepdims=True)
        dz_ref[...] = rs_ref[...] * (dyg - m1 - x * m2)

        @pl.when(pl.program_id(0) == 0)
        def _():
            dg_ref[...] = jnp.zeros_like(dg_ref)
            db_ref[...] = jnp.zeros_like(db_ref)

        dg_ref[...] += jnp.sum(dy * x, axis=0, keepdims=True)
        db_ref[...] += jnp.sum(dy, axis=0, keepdims=True)

    row = pl.BlockSpec((tr, D), lambda i: (i, 0))
    vec = pl.BlockSpec((1, D), lambda i: (0, 0))
    ins = [row, row] if two else [row]
    args = (dy_a, dy_b) if two else (dy_a,)
    return pl.pallas_call(
        body, name=name, grid=(S // tr,), in_specs=ins + [row, pl.BlockSpec((tr, 1), lambda i: (i, 0)), vec],
        out_specs=[row, vec, vec],
        out_shape=[jax.ShapeDtypeStruct((S, D), F32), jax.ShapeDtypeStruct((1, D), F32), jax.ShapeDtypeStruct((1, D), F32)],
        compiler_params=_params(("arbitrary",), 48 << 20),
    )(*args, xh, rs, g)


def axpby(a, b, c1, name):
    S, D = a.shape
    tr = min(ROW_TILE, S)

    def body(a_ref, b_ref, o_ref):
        o_ref[...] = c1 * a_ref[...] + b_ref[...]

    row = pl.BlockSpec((tr, D), lambda i: (i, 0))
    return pl.pallas_call(body, name=name, grid=(S // tr,), in_specs=[row, row], out_specs=row,
                          out_shape=jax.ShapeDtypeStruct((S, D), F32), compiler_params=_params(("parallel",)))(a, b)


def loss_head(y, t, name):
    S, D = y.shape
    tr = min(ROW_TILE, S)
    nsteps = S // tr

    def body(y_ref, t_ref, dy_ref, l_ref, acc_ref):
        i = pl.program_id(0)

        @pl.when(i == 0)
        def _():
            acc_ref[...] = jnp.zeros_like(acc_ref)

        e = y_ref[...] - t_ref[...]
        dy_ref[...] = e * (1.0 / D)
        acc_ref[...] += jnp.sum(e * e, axis=0, keepdims=True)

        @pl.when(i == nsteps - 1)
        def _():
            l_ref[...] = jnp.sum(acc_ref[...], axis=1, keepdims=True) * (0.5 / D)

    row = pl.BlockSpec((tr, D), lambda i: (i, 0))
    return pl.pallas_call(
        body, name=name, grid=(nsteps,), in_specs=[row, row],
        out_specs=[row, pl.BlockSpec((1, 1), lambda i: (0, 0))],
        out_shape=[jax.ShapeDtypeStruct((S, D), F32), jax.ShapeDtypeStruct((1, 1), F32)],
        scratch_shapes=[pltpu.VMEM((1, D), F32)], compiler_params=_params(("arbitrary",)),
    )(y, t)


SWIGLU_ROWS = 256


def swiglu_fwd(gu, name):
    _, S, Fh = gu.shape
    tc = _divisors(Fh, LANES, 1536)[0]
    tr = min(SWIGLU_ROWS, S)

    def body(gu_ref, o_ref):
        g = gu_ref[0]
        o_ref[...] = g * _sigmoid(g) * gu_ref[1]

    return pl.pallas_call(
        body, name=name, grid=(S // tr, Fh // tc), in_specs=[pl.BlockSpec((2, tr, tc), lambda i, j: (0, i, j))],
        out_specs=pl.BlockSpec((tr, tc), lambda i, j: (i, j)), out_shape=jax.ShapeDtypeStruct((S, Fh), F32),
        compiler_params=_params(("parallel", "parallel")),
    )(gu)


def swiglu_bwd(gu, dact, name):
    _, S, Fh = gu.shape
    tc = _divisors(Fh, LANES, 1536)[0]
    tr = min(SWIGLU_ROWS, S)

    def body(gu_ref, d_ref, o_ref):
        g, u, d = gu_ref[0], gu_ref[1], d_ref[...]
        s = _sigmoid(g)
        o_ref[0] = d * u * (s * (1.0 + g * (1.0 - s)))
        o_ref[1] = d * (g * s)

    both = pl.BlockSpec((2, tr, tc), lambda i, j: (0, i, j))
    return pl.pallas_call(
        body, name=name, grid=(S // tr, Fh // tc), in_specs=[both, pl.BlockSpec((tr, tc), lambda i, j: (i, j))],
        out_specs=both, out_shape=jax.ShapeDtypeStruct((2, S, Fh), F32), compiler_params=_params(("parallel", "parallel")),
    )(gu, dact)


GATE_COLS = 256


def merge_fwd(proj, pr, pa, name):
    S, D = pr.shape
    tr = min(GATE_ROWS, S)
    c0 = (3 * D + 2 * N_KV_HEADS * HEAD_DIM) // GATE_COLS
    c1 = c0 + D // GATE_COLS

    def body(gr_ref, ga_ref, pr_ref, pa_ref, o_ref):
        o_ref[...] = _sigmoid(gr_ref[...]) * pr_ref[...] + _sigmoid(ga_ref[...]) * pa_ref[...]

    blk = pl.BlockSpec((tr, GATE_COLS), lambda i, j: (i, j))
    return pl.pallas_call(
        body, name=name, grid=(S // tr, D // GATE_COLS),
        in_specs=[pl.BlockSpec((tr, GATE_COLS), lambda i, j: (i, c0 + j)), pl.BlockSpec((tr, GATE_COLS), lambda i, j: (i, c1 + j)),
                  blk, blk],
        out_specs=blk, out_shape=jax.ShapeDtypeStruct((S, D), F32), compiler_params=_params(("parallel", "parallel")),
    )(proj, proj, pr, pa)


def merge_bwd(proj, pr, pa, dm, name):
    S, D = pr.shape
    tr = min(GATE_ROWS, S)
    c0 = (3 * D + 2 * N_KV_HEADS * HEAD_DIM) // GATE_COLS
    c1 = c0 + D // GATE_COLS

    def body(gr_ref, ga_ref, pr_ref, pa_ref, dm_ref, dpr_ref, dpa_ref, dgr_ref, dga_ref):
        sr, sa, d = _sigmoid(gr_ref[...]), _sigmoid(ga_ref[...]), dm_ref[...]
        dpr_ref[...] = d * sr
        dpa_ref[...] = d * sa
        dgr_ref[...] = d * pr_ref[...] * (sr * (1.0 - sr))
        dga_ref[...] = d * pa_ref[...] * (sa * (1.0 - sa))

    blk = pl.BlockSpec((tr, GATE_COLS), lambda i, j: (i, j))
    sds = jax.ShapeDtypeStruct((S, D), F32)
    return pl.pallas_call(
        body, name=name, grid=(S // tr, D // GATE_COLS),
        in_specs=[pl.BlockSpec((tr, GATE_COLS), lambda i, j: (i, c0 + j)), pl.BlockSpec((tr, GATE_COLS), lambda i, j: (i, c1 + j)),
                  blk, blk, blk],
        out_specs=[blk, blk, blk, blk], out_shape=[sds, sds, sds, sds], compiler_params=_params(("parallel", "parallel")),
    )(proj, proj, pr, pa, dm)


RG_ROWS = 512


def _shift_down(cur, prev, d, row, first):
    halo = jnp.where(first, 0.0, pltpu.roll(prev, d, 0))
    return jnp.where(row >= d, pltpu.roll(cur, d, 0), halo)


def _shift_up(cur, nxt, d, row, last, tr):
    halo = jnp.where(last, 0.0, pltpu.roll(nxt, tr - d, 0))
    return jnp.where(row < tr - d, pltpu.roll(cur, tr - d, 0), halo)


def _lru_coeffs(r, lam):
    sp = _softplus_neg(lam)
    la = -LRU_C * r * sp
    return sp, la, jnp.exp(la), _neg_expm1(2.0 * la)


def rg_gates_fwd(proj, conv_w, conv_b, w_rg, b_rg, w_ig, b_ig, lam, name):
    S = proj.shape[0]
    nblk, bw, _ = w_rg.shape
    D = nblk * bw
    tr = min(RG_ROWS, S)

    def body(xr_ref, xp_ref, cw_ref, cb_ref, wr_ref, br_ref, wi_ref, bi_ref, lam_ref, xc_ref, r_ref, i_ref, a_ref, b_ref):
        first = pl.program_id(1) == 0
        cur, prev = xr_ref[...], xp_ref[...]
        row = lax.broadcasted_iota(jnp.int32, cur.shape, 0)
        xc = cb_ref[...]
        for k in range(CONV_WIDTH - 1):
            xc = xc + _shift_down(cur, prev, CONV_WIDTH - 1 - k, row, first) * cw_ref[k:k + 1, :]
        xc = xc + cur * cw_ref[CONV_WIDTH - 1:CONV_WIDTH, :]
        xm = xc.astype(MXU_DTYPE)
        r = _sigmoid(jnp.dot(xm, wr_ref[...].astype(MXU_DTYPE), preferred_element_type=F32) + br_ref[...])
        ig = _sigmoid(jnp.dot(xm, wi_ref[...].astype(MXU_DTYPE), preferred_element_type=F32) + bi_ref[...])
        _, _, a, em = _lru_coeffs(r, lam_ref[...])
        xc_ref[...] = xc
        r_ref[...] = r
        i_ref[...] = ig
        a_ref[...] = a
        b_ref[...] = jnp.sqrt(em) * (ig * xc)

    tile = pl.BlockSpec((tr, bw), lambda n, i: (i, n))
    vec = pl.BlockSpec((1, bw), lambda n, i: (0, n))
    wblk = pl.BlockSpec((None, bw, bw), lambda n, i: (n, 0, 0))
    sds = jax.ShapeDtypeStruct((S, D), F32)
    return pl.pallas_call(
        body, name=name, grid=(nblk, S // tr),
        in_specs=[tile, pl.BlockSpec((tr, bw), lambda n, i: (jnp.maximum(i - 1, 0), n)),
                  pl.BlockSpec((CONV_WIDTH, bw), lambda n, i: (0, n)), vec, wblk, vec, wblk, vec, vec],
        out_specs=[tile] * 5, out_shape=[sds] * 5, compiler_params=_params(("parallel", "parallel")),
    )(proj, proj, conv_w, conv_b, w_rg, b_rg, w_ig, b_ig, lam)


SCAN_COLS = 256
CHUNK = SUBLANES
SCAN_UNROLL = 4


def rg_scan_fwd(proj, a, b, name):
    S, D = a.shape
    cb = min(SCAN_COLS, D)
    goff = D // cb

    def body(a_ref, b_ref, g_ref, hs_ref, y_ref):
        row = lax.broadcasted_iota(jnp.int32, (CHUNK, cb), 0)

        def step(c, carry):
            r0 = pl.multiple_of(c * CHUNK, CHUNK)
            A = a_ref[pl.ds(r0, CHUNK), :]
            B = b_ref[pl.ds(r0, CHUNK), :]
            for d in (1, 2, 4):
                As = jnp.where(row >= d, pltpu.roll(A, d, 0), 1.0)
                Bs = jnp.where(row >= d, pltpu.roll(B, d, 0), 0.0)
                B = A * Bs + B
                A = A * As
            H = B + A * carry
            hs_ref[pl.ds(r0, CHUNK), :] = H
            return jnp.sum(jnp.where(row == CHUNK - 1, H, 0.0), axis=0, keepdims=True)

        lax.fori_loop(0, S // CHUNK, step, jnp.zeros((1, cb), F32), unroll=SCAN_UNROLL)
        y_ref[...] = hs_ref[...] * _gelu(g_ref[...])

    col = pl.BlockSpec((S, cb), lambda j: (0, j))
    sds = jax.ShapeDtypeStruct((S, D), F32)
    return pl.pallas_call(
        body, name=name, grid=(D // cb,), in_specs=[col, col, pl.BlockSpec((S, cb), lambda j: (0, goff + j))],
        out_specs=[col, col], out_shape=[sds, sds], compiler_params=_params(("parallel",), _vmem_limit(5 * S * cb * 4, 4 * S * cb * 4)),
    )(a, b, proj)


def rg_scan_bwd(proj, dy, hs, a, name):
    S, D = a.shape
    cb = min(SCAN_COLS, D)
    goff = D // cb
    nchunks = S // CHUNK

    def body(g_ref, dy_ref, hs_ref, a_ref, dg_ref, gt_ref):
        gate, dy = g_ref[...], dy_ref[...]
        dg_ref[...] = dy * hs_ref[...] * _gelu_grad(gate)
        gt_ref[...] = dy * _gelu(gate)
        row = lax.broadcasted_iota(jnp.int32, (CHUNK, cb), 0)

        def step(k, carry):
            c = nchunks - 1 - k
            r0 = pl.multiple_of(c * CHUNK, CHUNK)
            rn = pl.multiple_of(jnp.minimum(c + 1, nchunks - 1) * CHUNK, CHUNK)
            last = c == nchunks - 1
            nxt = jnp.where(last, 0.0, pltpu.roll(a_ref[pl.ds(rn, CHUNK), :], CHUNK - 1, 0))
            A = jnp.where(row < CHUNK - 1, pltpu.roll(a_ref[pl.ds(r0, CHUNK), :], CHUNK - 1, 0), nxt)
            B = gt_ref[pl.ds(r0, CHUNK), :]
            for d in (1, 2, 4):
                As = jnp.where(row < CHUNK - d, pltpu.roll(A, CHUNK - d, 0), 1.0)
                Bs = jnp.where(row < CHUNK - d, pltpu.roll(B, CHUNK - d, 0), 0.0)
                B = A * Bs + B
                A = A * As
            G = B + A * carry
            gt_ref[pl.ds(r0, CHUNK), :] = G
            return jnp.sum(jnp.where(row == 0, G, 0.0), axis=0, keepdims=True)

        lax.fori_loop(0, nchunks, step, jnp.zeros((1, cb), F32), unroll=SCAN_UNROLL)

    col = pl.BlockSpec((S, cb), lambda j: (0, j))
    sds = jax.ShapeDtypeStruct((S, D), F32)
    return pl.pallas_call(
        body, name=name, grid=(D // cb,), in_specs=[pl.BlockSpec((S, cb), lambda j: (0, goff + j)), col, col, col],
        out_specs=[col, col], out_shape=[sds, sds], compiler_params=_params(("parallel",), _vmem_limit(6 * S * cb * 4, 6 * S * cb * 4)),
    )(proj, dy, hs, a)


def rg_gates_bwd(gt, hs, xc, r, ig, w_rg, w_ig, lam, name):
    S, D = xc.shape
    nblk, bw, _ = w_rg.shape
    tr = min(RG_ROWS, S)

    def body(gt_ref, hs_ref, hp_ref, xc_ref, r_ref, i_ref, wr_ref, wi_ref, lam_ref,
             dxc_ref, dwr_ref, dwi_ref, dbr_ref, dbi_ref, dl_ref):
        step = pl.program_id(1)
        g, hs, xc, r, ig, lam = gt_ref[...], hs_ref[...], xc_ref[...], r_ref[...], i_ref[...], lam_ref[...]
        row = lax.broadcasted_iota(jnp.int32, g.shape, 0)
        hprev = _shift_down(hs, hp_ref[...], 1, row, step == 0)
        sp, _, a, em = _lru_coeffs(r, lam)
        mult = jnp.sqrt(em)
        du = g * mult
        dla = g * hprev * a - (g * (ig * xc)) * (a * a) / mult
        dpr = (dla * (-LRU_C * sp)) * (r * (1.0 - r))
        dpi = (du * xc) * (ig * (1.0 - ig))
        dprm, dpim = dpr.astype(MXU_DTYPE), dpi.astype(MXU_DTYPE)
        nt = (((1,), (1,)), ((), ()))
        dxc_ref[...] = (du * ig + lax.dot_general(dprm, wr_ref[...].astype(MXU_DTYPE), nt, preferred_element_type=F32)
                        + lax.dot_general(dpim, wi_ref[...].astype(MXU_DTYPE), nt, preferred_element_type=F32))

        @pl.when(step == 0)
        def _():
            for ref in (dwr_ref, dwi_ref, dbr_ref, dbi_ref, dl_ref):
                ref[...] = jnp.zeros_like(ref)

        xct = xc.T.astype(MXU_DTYPE)
        dwr_ref[...] += jnp.dot(xct, dprm, preferred_element_type=F32)
        dwi_ref[...] += jnp.dot(xct, dpim, preferred_element_type=F32)
        dbr_ref[...] += jnp.sum(dpr, axis=0, keepdims=True)
        dbi_ref[...] += jnp.sum(dpi, axis=0, keepdims=True)
        dl_ref[...] += jnp.sum(dla * (-LRU_C * r), axis=0, keepdims=True) * (-_sigmoid(-lam))

    tile = pl.BlockSpec((tr, bw), lambda n, i: (i, n))
    vec = pl.BlockSpec((1, bw), lambda n, i: (0, n))
    wblk = pl.BlockSpec((None, bw, bw), lambda n, i: (n, 0, 0))
    return pl.pallas_call(
        body, name=name, grid=(nblk, S // tr),
        in_specs=[tile, tile, pl.BlockSpec((tr, bw), lambda n, i: (jnp.maximum(i - 1, 0), n)), tile, tile, tile, wblk, wblk, vec],
        out_specs=[tile, wblk, wblk, vec, vec, vec],
        out_shape=[jax.ShapeDtypeStruct((S, D), F32), jax.ShapeDtypeStruct((nblk, bw, bw), F32), jax.ShapeDtypeStruct((nblk, bw, bw), F32),
                   jax.ShapeDtypeStruct((1, D), F32), jax.ShapeDtypeStruct((1, D), F32), jax.ShapeDtypeStruct((1, D), F32)],
        compiler_params=_params(("parallel", "arbitrary")),
    )(gt, hs, hs, xc, r, ig, w_rg, w_ig, lam)


def rg_conv_bwd(proj, dxc, conv_w, name):
    S, D = dxc.shape
    bw = min(SCAN_COLS, D)
    tr = min(RG_ROWS, S)
    nsteps = S // tr

    def body(d_ref, dn_ref, xr_ref, xp_ref, cw_ref, dxr_ref, dcw_ref, dcb_ref):
        step = pl.program_id(1)
        d, xr = d_ref[...], xr_ref[...]
        row = lax.broadcasted_iota(jnp.int32, d.shape, 0)
        dxr = d * cw_ref[CONV_WIDTH - 1:CONV_WIDTH, :]
        for k in range(CONV_WIDTH - 1):
            dxr = dxr + _shift_up(d, dn_ref[...], CONV_WIDTH - 1 - k, row, step == nsteps - 1, tr) * cw_ref[k:k + 1, :]
        dxr_ref[...] = dxr

        @pl.when(step == 0)
        def _():
            dcw_ref[...] = jnp.zeros_like(dcw_ref)
            dcb_ref[...] = jnp.zeros_like(dcb_ref)

        for k in range(CONV_WIDTH - 1):
            xs = _shift_down(xr, xp_ref[...], CONV_WIDTH - 1 - k, row, step == 0)
            dcw_ref[k:k + 1, :] += jnp.sum(d * xs, axis=0, keepdims=True)
        dcw_ref[CONV_WIDTH - 1:CONV_WIDTH, :] += jnp.sum(d * xr, axis=0, keepdims=True)
        dcb_ref[...] += jnp.sum(d, axis=0, keepdims=True)

    tile = pl.BlockSpec((tr, bw), lambda n, i: (i, n))
    cwb = pl.BlockSpec((CONV_WIDTH, bw), lambda n, i: (0, n))
    return pl.pallas_call(
        body, name=name, grid=(D // bw, nsteps),
        in_specs=[tile, pl.BlockSpec((tr, bw), lambda n, i: (jnp.minimum(i + 1, nsteps - 1), n)), tile,
                  pl.BlockSpec((tr, bw), lambda n, i: (jnp.maximum(i - 1, 0), n)), cwb],
        out_specs=[tile, cwb, pl.BlockSpec((1, bw), lambda n, i: (0, n))],
        out_shape=[jax.ShapeDtypeStruct((S, D), F32), jax.ShapeDtypeStruct((CONV_WIDTH, D), F32), jax.ShapeDtypeStruct((1, D), F32)],
        compiler_params=_params(("parallel", "arbitrary")),
    )(dxc, dxc, proj, proj, conv_w)


def rope_table(S):
    half = ROT_DIM // 2
    pos = jnp.arange(S, dtype=F32)
    inv = ROPE_THETA ** (-jnp.arange(0, ROT_DIM, 2, dtype=F32) / ROT_DIM)
    ang = pos[:, None] * inv[None, :]
    cos, sin = jnp.cos(ang), jnp.sin(ang)
    zero = jnp.zeros((S, HEAD_DIM - ROT_DIM), F32)
    c = jnp.concatenate([cos, cos, zero + 1.0], axis=1)
    a = jnp.concatenate([-sin, jnp.zeros((S, half), F32), zero], axis=1)
    b = jnp.concatenate([jnp.zeros((S, half), F32), sin, zero], axis=1)
    return jnp.stack([jnp.tile(t, (1, LANES // HEAD_DIM)) for t in (c, a, b)])


def _rope(t, tab):
    half = ROT_DIM // 2
    return t * tab[0] + pltpu.roll(t, LANES - half, 1) * tab[1] + pltpu.roll(t, half, 1) * tab[2]


def _rope_t(d, tab):
    half = ROT_DIM // 2
    return d * tab[0] + pltpu.roll(d * tab[1], half, 1) + pltpu.roll(d * tab[2], LANES - half, 1)


def _dup_head(t, hk, lo):
    sw = pltpu.roll(t, HEAD_DIM, 1)
    return jnp.where(lo, t, sw) if hk == 0 else jnp.where(lo, sw, t)


def _attn_common(n, sink_ref, q_ref, kp_ref, kc_ref, vp_ref, vc_ref, tc_ref, tp_ref, hk, pairs):
    tq = (tc_ref[0], tc_ref[1], tc_ref[2])
    tp = (tp_ref[0], tp_ref[1], tp_ref[2])
    lo = lax.broadcasted_iota(jnp.int32, (WINDOW, LANES), 1) < HEAD_DIM
    lo2 = lax.broadcasted_iota(jnp.int32, (2 * WINDOW, LANES), 1) < HEAD_DIM
    kband = jnp.concatenate([_rope(kp_ref[...], tp), _rope(kc_ref[...], tq)], axis=0)
    vband = jnp.concatenate([vp_ref[...], vc_ref[...]], axis=0)
    kd = _dup_head(kband, hk, lo2).astype(MXU_DTYPE)
    vd = _dup_head(vband, hk, lo2).astype(MXU_DTYPE)
    rows, sks = [], []
    for j in range(pairs):
        col = hk * pairs + j
        qp = _rope(q_ref[:, col * LANES:(col + 1) * LANES], tq)
        rows += [jnp.where(lo, qp, 0.0), jnp.where(lo, 0.0, qp)]
        sks += [jnp.full((WINDOW, 1), sink_ref[2 * col], F32), jnp.full((WINDOW, 1), sink_ref[2 * col + 1], F32)]
    qg = jnp.concatenate(rows, axis=0)
    sk = jnp.concatenate(sks, axis=0)
    G = 2 * pairs * WINDOW
    ri = lax.broadcasted_iota(jnp.int32, (G, 2 * WINDOW), 0) & (WINDOW - 1)
    kj = lax.broadcasted_iota(jnp.int32, (G, 2 * WINDOW), 1) - WINDOW
    valid = (kj <= ri) & (kj > ri - WINDOW) & (kj + n * WINDOW >= 0)
    s = lax.dot_general(qg.astype(MXU_DTYPE), kd, (((1,), (1,)), ((), ())), preferred_element_type=F32) * (HEAD_DIM ** -0.5)
    s = jnp.where(valid, s, NEG_INF)
    m = jnp.maximum(jnp.max(s, axis=1, keepdims=True), sk)
    e = jnp.exp(s - m)
    es = jnp.exp(sk - m)
    inv = 1.0 / (jnp.sum(e, axis=1, keepdims=True) + es)
    return qg, kd, vd, e * inv, es * inv, lo, lo2, tq, tp


def _attn_specs(D, NB):
    kcol = 3 * D // LANES
    q = pl.BlockSpec((WINDOW, D), lambda n: (n, 2))
    kc = pl.BlockSpec((WINDOW, LANES), lambda n: (n, kcol))
    kp = pl.BlockSpec((WINDOW, LANES), lambda n: (jnp.maximum(n - 1, 0), kcol))
    vc = pl.BlockSpec((WINDOW, LANES), lambda n: (n, kcol + 1))
    vp = pl.BlockSpec((WINDOW, LANES), lambda n: (jnp.maximum(n - 1, 0), kcol + 1))
    tc = pl.BlockSpec((3, WINDOW, LANES), lambda n: (0, n, 0))
    tp = pl.BlockSpec((3, WINDOW, LANES), lambda n: (0, jnp.maximum(n - 1, 0), 0))
    sink = pl.BlockSpec(memory_space=pltpu.SMEM)
    return [sink, q, kp, kc, vp, vc, tc, tp]


def attn_fwd(proj, sinks, tab, D, name):
    S = proj.shape[0]
    NB = S // WINDOW
    pairs = D // HEAD_DIM // N_KV_HEADS // 2

    def body(sink_ref, q_ref, kp_ref, kc_ref, vp_ref, vc_ref, tc_ref, tp_ref, o_ref):
        n = pl.program_id(0)
        for hk in range(N_KV_HEADS):
            _, _, vd, p, _, lo, _, _, _ = _attn_common(n, sink_ref, q_ref, kp_ref, kc_ref, vp_ref, vc_ref, tc_ref, tp_ref, hk, pairs)
            o = jnp.dot(p.astype(MXU_DTYPE), vd, preferred_element_type=F32)
            for j in range(pairs):
                col = hk * pairs + j
                oa = o[(2 * j) * WINDOW:(2 * j + 1) * WINDOW]
                ob = o[(2 * j + 1) * WINDOW:(2 * j + 2) * WINDOW]
                o_ref[:, col * LANES:(col + 1) * LANES] = jnp.where(lo, oa, ob)

    return pl.pallas_call(
        body, name=name, grid=(NB,), in_specs=_attn_specs(D, NB),
        out_specs=pl.BlockSpec((WINDOW, D), lambda n: (n, 0)), out_shape=jax.ShapeDtypeStruct((S, D), F32),
        compiler_params=_params(("parallel",)),
    )(sinks, proj, proj, proj, proj, proj, tab, tab)


def attn_bwd(proj, sinks, tab, o, do, D, name):
    S = proj.shape[0]
    NB = S // WINDOW
    pairs = D // HEAD_DIM // N_KV_HEADS // 2

    def body(sink_ref, q_ref, kp_ref, kc_ref, vp_ref, vc_ref, tc_ref, tp_ref, o_ref, do_ref, dq_ref, dk_ref, dv_ref, ds_ref):
        n = pl.program_id(0)

        @pl.when(n == 0)
        def _():
            ds_ref[...] = jnp.zeros_like(ds_ref)

        lane1 = lax.broadcasted_iota(jnp.int32, (1, LANES), 1)
        dsink = jnp.zeros((1, LANES), F32)
        dkt = dvt = None
        for hk in range(N_KV_HEADS):
            qg, kd, vd, p, ps, lo, lo2, tq, tp = _attn_common(n, sink_ref, q_ref, kp_ref, kc_ref, vp_ref, vc_ref, tc_ref, tp_ref, hk, pairs)
            dos, os_ = [], []
            for j in range(pairs):
                col = hk * pairs + j
                dop = do_ref[:, col * LANES:(col + 1) * LANES]
                op = o_ref[:, col * LANES:(col + 1) * LANES]
                dos += [jnp.where(lo, dop, 0.0), jnp.where(lo, 0.0, dop)]
                os_ += [jnp.where(lo, op, 0.0), jnp.where(lo, 0.0, op)]
            dog = jnp.concatenate(dos, axis=0)
            og = jnp.concatenate(os_, axis=0)
            dogm = dog.astype(MXU_DTYPE)
            dp = lax.dot_general(dogm, vd, (((1,), (1,)), ((), ())), preferred_element_type=F32)
            dr = jnp.sum(dog * og, axis=1, keepdims=True)
            ds = p * (dp - dr) * (HEAD_DIM ** -0.5)
            dsm = ds.astype(MXU_DTYPE)
            dqg = jnp.dot(dsm, kd, preferred_element_type=F32)
            dkd = jnp.dot(ds.T.astype(MXU_DTYPE), qg.astype(MXU_DTYPE), preferred_element_type=F32)
            dvd = jnp.dot(p.T.astype(MXU_DTYPE), dogm, preferred_element_type=F32)
            dkf = dkd + pltpu.roll(dkd, HEAD_DIM, 1)
            dvf = dvd + pltpu.roll(dvd, HEAD_DIM, 1)
            if hk == 0:
                dkt, dvt = dkf, dvf
            else:
                dkt, dvt = jnp.where(lo2, dkt, dkf), jnp.where(lo2, dvt, dvf)
            sd = ps * dr
            for j in range(pairs):
                col = hk * pairs + j
                dqa = dqg[(2 * j) * WINDOW:(2 * j + 1) * WINDOW]
                dqb = dqg[(2 * j + 1) * WINDOW:(2 * j + 2) * WINDOW]
                dq_ref[:, col * LANES:(col + 1) * LANES] = _rope_t(jnp.where(lo, dqa, dqb), tq)
                for t in range(2):
                    part = sd[(2 * j + t) * WINDOW:(2 * j + t + 1) * WINDOW]
                    val = jnp.sum(part, axis=0, keepdims=True)
                    dsink = dsink - jnp.where(lane1 == 2 * col + t, val, 0.0)
        dk_ref[...] = jnp.concatenate([_rope_t(dkt[:WINDOW], tp), _rope_t(dkt[WINDOW:], tq)], axis=0)
        dv_ref[...] = dvt
        ds_ref[...] += dsink

    blk = pl.BlockSpec((WINDOW, D), lambda n: (n, 0))
    band = pl.BlockSpec((None, 2 * WINDOW, LANES), lambda n: (n, 0, 0))
    return pl.pallas_call(
        body, name=name, grid=(NB,), in_specs=_attn_specs(D, NB) + [blk, blk],
        out_specs=[blk, band, band, pl.BlockSpec((1, LANES), lambda n: (0, 0))],
        out_shape=[jax.ShapeDtypeStruct((S, D), F32), jax.ShapeDtypeStruct((NB, 2 * WINDOW, LANES), F32),
                   jax.ShapeDtypeStruct((NB, 2 * WINDOW, LANES), F32), jax.ShapeDtypeStruct((1, LANES), F32)],
        compiler_params=_params(("arbitrary",)),
    )(sinks, proj, proj, proj, proj, proj, tab, tab, o, do)


def band_fold(dkb, dvb, name):
    NB = dkb.shape[0]
    k4 = dkb.reshape(NB, 2, WINDOW, LANES)
    v4 = dvb.reshape(NB, 2, WINDOW, LANES)

    def body(kc_ref, kn_ref, vc_ref, vn_ref, dk_ref, dv_ref):
        more = pl.program_id(0) < NB - 1
        dk_ref[...] = kc_ref[...] + jnp.where(more, kn_ref[...], 0.0)
        dv_ref[...] = vc_ref[...] + jnp.where(more, vn_ref[...], 0.0)

    cur = pl.BlockSpec((None, None, WINDOW, LANES), lambda n: (n, 1, 0, 0))
    nxt = pl.BlockSpec((None, None, WINDOW, LANES), lambda n: (jnp.minimum(n + 1, NB - 1), 0, 0, 0))
    out = pl.BlockSpec((WINDOW, LANES), lambda n: (n, 0))
    sds = jax.ShapeDtypeStruct((NB * WINDOW, LANES), F32)
    return pl.pallas_call(body, name=name, grid=(NB,), in_specs=[cur, nxt, cur, nxt], out_specs=[out, out], out_shape=[sds, sds],
                          compiler_params=_params(("parallel",)))(k4, k4, v4, v4)


CROSS_ROWS = 512


def _cross_probs(q, k, scale):
    s = lax.dot_general(q.astype(MXU_DTYPE), k.astype(MXU_DTYPE), (((1,), (1,)), ((), ())), preferred_element_type=F32) * scale
    e = jnp.exp(s - jnp.max(s, axis=1, keepdims=True))
    return e / jnp.sum(e, axis=1, keepdims=True)


def cross_fwd(qc, kv, name):
    S, D = qc.shape
    M = kv.shape[0]
    hd = D // CROSS_HEADS
    tq = min(CROSS_ROWS, S)

    def body(q_ref, kv_ref, o_ref):
        for h in range(CROSS_HEADS):
            p = _cross_probs(q_ref[:, h * hd:(h + 1) * hd], kv_ref[:, h * hd:(h + 1) * hd], hd ** -0.5)
            v = kv_ref[:, D + h * hd:D + (h + 1) * hd].astype(MXU_DTYPE)
            o_ref[:, h * hd:(h + 1) * hd] = jnp.dot(p.astype(MXU_DTYPE), v, preferred_element_type=F32)

    return pl.pallas_call(
        body, name=name, grid=(S // tq,), in_specs=[pl.BlockSpec((tq, D), lambda i: (i, 0)), pl.BlockSpec((M, 2 * D), lambda i: (0, 0))],
        out_specs=pl.BlockSpec((tq, D), lambda i: (i, 0)), out_shape=jax.ShapeDtypeStruct((S, D), F32),
        compiler_params=_params(("parallel",)),
    )(qc, kv)


def cross_bwd(qc, kv, do, name):
    S, D = qc.shape
    M = kv.shape[0]
    hd = D // CROSS_HEADS
    tq = min(CROSS_ROWS, S)

    def body(q_ref, kv_ref, do_ref, dq_ref, dkv_ref):
        @pl.when(pl.program_id(0) == 0)
        def _():
            dkv_ref[...] = jnp.zeros_like(dkv_ref)

        for h in range(CROSS_HEADS):
            q = q_ref[:, h * hd:(h + 1) * hd]
            k = kv_ref[:, h * hd:(h + 1) * hd]
            v = kv_ref[:, D + h * hd:D + (h + 1) * hd].astype(MXU_DTYPE)
            dom = do_ref[:, h * hd:(h + 1) * hd].astype(MXU_DTYPE)
            p = _cross_probs(q, k, hd ** -0.5)
            dp = lax.dot_general(dom, v, (((1,), (1,)), ((), ())), preferred_element_type=F32)
            ds = p * (dp - jnp.sum(p * dp, axis=1, keepdims=True)) * (hd ** -0.5)
            dq_ref[:, h * hd:(h + 1) * hd] = jnp.dot(ds.astype(MXU_DTYPE), k.astype(MXU_DTYPE), preferred_element_type=F32)
            dkv_ref[:, h * hd:(h + 1) * hd] += jnp.dot(ds.T.astype(MXU_DTYPE), q.astype(MXU_DTYPE), preferred_element_type=F32)
            dkv_ref[:, D + h * hd:D + (h + 1) * hd] += jnp.dot(p.T.astype(MXU_DTYPE), dom, preferred_element_type=F32)

    row = pl.BlockSpec((tq, D), lambda i: (i, 0))
    full = pl.BlockSpec((M, 2 * D), lambda i: (0, 0))
    return pl.pallas_call(
        body, name=name, grid=(S // tq,), in_specs=[row, full, row], out_specs=[row, full],
        out_shape=[jax.ShapeDtypeStruct((S, D), F32), jax.ShapeDtypeStruct((M, 2 * D), F32)],
        compiler_params=_params(("arbitrary",)),
    )(qc, kv, do)


def adamw(w, g, m, v, name):
    shape = w.shape
    cols = shape[-1]
    rows = int(np.prod(shape[:-1]))
    w2, g2, m2, v2 = (t.reshape(rows, cols) for t in (w, g, m, v))
    tr = _divisors(rows, SUBLANES, max(SUBLANES, (1 << 20) // (cols * 4) // SUBLANES * SUBLANES))[0]

    def body(w_ref, g_ref, m_ref, v_ref, d_ref, mo_ref, vo_ref, go_ref):
        gg = g_ref[...]
        mn = ADAM_B1 * m_ref[...] + (1.0 - ADAM_B1) * gg
        vn = ADAM_B2 * v_ref[...] + (1.0 - ADAM_B2) * (gg * gg)
        m_hat = mn / (1.0 - ADAM_B1 ** ADAM_STEP)
        v_hat = vn / (1.0 - ADAM_B2 ** ADAM_STEP)
        d_ref[...] = -ADAM_LR * (m_hat / (jnp.sqrt(v_hat) + ADAM_EPS) + ADAM_WD * w_ref[...])
        mo_ref[...] = mn
        vo_ref[...] = vn
        go_ref[...] = gg

    blk = pl.BlockSpec((tr, cols), lambda i: (i, 0))
    sds = jax.ShapeDtypeStruct((rows, cols), F32)
    d, mn, vn, go = pl.pallas_call(body, name=name, grid=(rows // tr,), in_specs=[blk] * 4, out_specs=[blk] * 4, out_shape=[sds] * 4,
                                   compiler_params=_params(("parallel",)))(w2, g2, m2, v2)
    return d.reshape(shape), mn.reshape(shape), vn.reshape(shape), go.reshape(shape)


def sum_devices(parts, name):
    n, rows, cols = parts.shape

    def body(p_ref, o_ref):
        acc = p_ref[0]
        for k in range(1, n):
            acc = acc + p_ref[k]
        o_ref[...] = acc

    return pl.pallas_call(body, name=name, in_specs=[pl.BlockSpec(memory_space=pltpu.VMEM)],
                          out_specs=pl.BlockSpec(memory_space=pltpu.VMEM), out_shape=jax.ShapeDtypeStruct((rows, cols), F32))(parts)


HBM_SPEC = pl.BlockSpec(memory_space=pltpu.HBM)


def _place():
    return lax.axis_index("x"), lax.axis_index("y"), lax.axis_index("c")


def _remote(src, dst, send_sems, recv_sems, k, to):
    return pltpu.make_async_remote_copy(src_ref=src, dst_ref=dst, send_sem=send_sems.at[k], recv_sem=recv_sems.at[k],
                                        device_id=to, device_id_type=MESH_ID)


SEM_SPEC = pl.BlockSpec(memory_space=pltpu.SEMAPHORE)
ANY_SPEC = pl.BlockSpec(memory_space=pl.ANY)
SPLIT_COPY = pltpu.CompilerParams(has_side_effects=pltpu.SideEffectType.DATAFLOW_SIDE_EFFECTING)


def _in_hbm(arrays):
    return [pltpu.with_memory_space_constraint(a, pltpu.HBM) for a in arrays]


def _split_start(copies, sources, lands, after, n_sems, name):
    n = len(sources)

    def body(*refs):
        for cp in copies(refs[:n], refs[n:2 * n], refs[2 * n + 1], refs[2 * n + 2]):
            cp.start()
        refs[-1][...] = jnp.zeros_like(refs[-1])

    through = [pltpu.HBM(a.shape, a.dtype) for a in list(sources) + list(lands)]
    outs = pl.pallas_call(
        body, name=name, in_specs=[HBM_SPEC] * (2 * n) + [ANY_SPEC],
        out_specs=[SEM_SPEC, SEM_SPEC] + [HBM_SPEC] * (2 * n) + [pl.BlockSpec(memory_space=pltpu.VMEM)],
        out_shape=[pltpu.SemaphoreType.DMA((n_sems,)), pltpu.SemaphoreType.DMA((n_sems,))] + through
        + [jax.ShapeDtypeStruct((SUBLANES, LANES), F32)],
        input_output_aliases={i: 2 + i for i in range(2 * n)}, compiler_params=SPLIT_COPY,
    )(*_in_hbm(sources), *_in_hbm(lands), after)
    return outs[0], outs[1], outs[2:2 + n], outs[2 + n:2 + 2 * n], outs[-1]


def _split_wait(copies, send_sems, recv_sems, sources, lands, after, name):
    n = len(sources)

    def body(*refs):
        for cp in copies(refs[:n], refs[n:2 * n], refs[2 * n], refs[2 * n + 1]):
            cp.wait_send()
            cp.wait_recv()

    through = [pltpu.HBM(a.shape, a.dtype) for a in list(sources) + list(lands)]
    outs = pl.pallas_call(
        body, name=name, in_specs=[HBM_SPEC] * (2 * n) + [SEM_SPEC, SEM_SPEC, ANY_SPEC], out_specs=[HBM_SPEC] * (2 * n),
        out_shape=through, input_output_aliases={i: i for i in range(2 * n)}, compiler_params=SPLIT_COPY,
    )(*sources, *lands, send_sems, recv_sems, after)
    return outs[:n], outs[n:]


def _chip_slab(land, slot, rows):
    return land.at[slot, rows] if len(land.shape) == 3 else land.at[rows, slot]


def _gather_copies(w_refs, land_refs, send_sems, recv_sems):
    n = len(w_refs)
    x, y, c = _place()
    chips = [(1 - x, y), (x, 1 - y), (1 - x, 1 - y)]
    cps = []
    for a in range(n):
        hr = w_refs[a].shape[0] // 2
        mine, every = pl.ds(c * hr, hr), pl.ds(0, 2 * hr)
        cps.append(_remote(w_refs[a], _chip_slab(land_refs[a], 2 * x + y, every), send_sems, recv_sems, 3 * n + a, (x, y, 1 - c)))
        for k, chip in enumerate(chips):
            cps.append(_remote(w_refs[a].at[mine], _chip_slab(land_refs[a], 2 * x + y, mine), send_sems, recv_sems, 3 * a + k, (*chip, c)))
    return cps


def gather_start(shards, after, name):
    lands = [lax.empty(s.shape[:-2] + (N_CHIPS,) + s.shape[-2:], s.dtype) for s in shards]
    return _split_start(_gather_copies, shards, lands, after, 4 * len(shards), name)


def gather_wait(state, after, name):
    send_sems, recv_sems, sources, lands, _ = state
    return _split_wait(_gather_copies, send_sems, recv_sems, sources, lands, after, name)[1]


def gather_pass(lands, name):
    n = len(lands)

    def body(*refs):
        out_refs, send_sems, recv_sems = refs[n:2 * n], refs[2 * n], refs[2 * n + 1]
        x, y, c = _place()
        chips = [(1 - x, y), (x, 1 - y), (1 - x, 1 - y)]
        sent = []
        for a in range(n):
            hr = out_refs[a].shape[0 if len(out_refs[a].shape) == 4 else 1] // 2
            for k, (px, py) in enumerate(chips):
                landed = _chip_slab(out_refs[a], 2 * px + py, pl.ds(c * hr, hr))
                sent.append(_remote(landed, landed, send_sems, recv_sems, 3 * a + k, (x, y, 1 - c)))
        for cp in sent:
            cp.start()
        for a in range(n):
            hr = out_refs[a].shape[0 if len(out_refs[a].shape) == 4 else 1] // 2
            for k, (px, py) in enumerate(chips):
                theirs = _chip_slab(out_refs[a], 2 * px + py, pl.ds((1 - c) * hr, hr))
                _remote(theirs, theirs, send_sems, recv_sems, 3 * a + k, (x, y, 1 - c)).wait_recv()
        for cp in sent:
            cp.wait_send()

    return pl.pallas_call(
        body, name=name, in_specs=[HBM_SPEC] * n, out_specs=[HBM_SPEC] * n,
        out_shape=[jax.ShapeDtypeStruct(a.shape, a.dtype) for a in lands], input_output_aliases={a: a for a in range(n)},
        scratch_shapes=[pltpu.SemaphoreType.DMA((3 * n,))] * 2,
    )(*lands)


def _scatter_copies(t_refs, land_refs, send_sems, recv_sems):
    x, y, c = _place()
    chips = [(1 - x, y), (x, 1 - y), (1 - x, 1 - y)]
    return [_remote(t_refs[a].at[:, 2 * px + py], land_refs[a].at[:, k], send_sems, recv_sems, 3 * a + k, (px, py, c))
            for a in range(len(t_refs)) for k, (px, py) in enumerate(chips)]


def scatter_start(parts, after, name):
    lands = [lax.empty((t.shape[0], N_CHIPS - 1) + t.shape[2:], t.dtype) for t in parts]
    return _split_start(_scatter_copies, parts, lands, after, 3 * len(parts), name)


def scatter_wait(state, after, name):
    send_sems, recv_sems, sources, lands, _ = state
    return _split_wait(_scatter_copies, send_sems, recv_sems, sources, lands, after, name)


def swap_sibling(parts, name):
    n = len(parts)

    def body(*refs):
        v_refs, out_refs, send_sems, recv_sems = refs[:n], refs[n:2 * n], refs[2 * n], refs[2 * n + 1]
        x, y, c = _place()
        cps = []
        for a in range(n):
            hr = v_refs[a].shape[2] // 2
            cps.append(_remote(v_refs[a].at[:, :, pl.ds((1 - c) * hr, hr)], out_refs[a], send_sems, recv_sems, a, (x, y, 1 - c)))
        for cp in cps:
            cp.start()
        for cp in cps:
            cp.wait()

    return pl.pallas_call(
        body, name=name, in_specs=[HBM_SPEC] * n, out_specs=[HBM_SPEC] * n,
        out_shape=[jax.ShapeDtypeStruct(v.shape[:2] + (v.shape[2] // 2, v.shape[3]), v.dtype) for v in parts],
        scratch_shapes=[pltpu.SemaphoreType.DMA((n,))] * 2,
    )(*parts)


def join_halves(halves, layer, name):
    n = len(halves)

    def body(*refs):
        out_refs, send_sems, recv_sems = refs[n:2 * n], refs[2 * n], refs[2 * n + 1]
        x, y, c = _place()
        cps = []
        for a in range(n):
            hr = out_refs[a].shape[1] // 2
            mine = out_refs[a].at[layer, pl.ds(c * hr, hr)]
            cps.append(_remote(mine, mine, send_sems, recv_sems, a, (x, y, 1 - c)))
        for cp in cps:
            cp.start()
        for a in range(n):
            hr = out_refs[a].shape[1] // 2
            theirs = out_refs[a].at[layer, pl.ds((1 - c) * hr, hr)]
            _remote(theirs, theirs, send_sems, recv_sems, a, (x, y, 1 - c)).wait_recv()
        for cp in cps:
            cp.wait_send()

    return pl.pallas_call(
        body, name=name, in_specs=[HBM_SPEC] * n, out_specs=[HBM_SPEC] * n,
        out_shape=[jax.ShapeDtypeStruct(f.shape, f.dtype) for f in halves], input_output_aliases={a: a for a in range(n)},
        scratch_shapes=[pltpu.SemaphoreType.DMA((n,))] * 2,
    )(*halves)


def gather_devices(v, name):
    def body(v_ref, out_ref, send_sems, recv_sems, local_sem):
        x, y, c = _place()
        me = 4 * x + 2 * y + c
        own = pltpu.make_async_copy(v_ref, out_ref.at[me], local_sem)
        own.start()
        peers = [((x + dx) % 2, (y + dy) % 2, (c + dc) % 2) for dx in (0, 1) for dy in (0, 1) for dc in (0, 1)][1:]
        sent = []
        for k, peer in enumerate(peers):
            cp = pltpu.make_async_remote_copy(src_ref=v_ref, dst_ref=out_ref.at[me], send_sem=send_sems.at[k], recv_sem=recv_sems.at[k],
                                              device_id=peer, device_id_type=MESH_ID)
            cp.start()
            sent.append(cp)
        for k, (px, py, pc) in enumerate(peers):
            slot = out_ref.at[4 * px + 2 * py + pc]
            pltpu.make_async_remote_copy(src_ref=slot, dst_ref=slot, send_sem=send_sems.at[k], recv_sem=recv_sems.at[k],
                                         device_id=(px, py, pc), device_id_type=MESH_ID).wait_recv()
        for cp in sent:
            cp.wait_send()
        own.wait()

    vm = pl.BlockSpec(memory_space=pltpu.VMEM)
    return pl.pallas_call(body, name=name, in_specs=[vm], out_specs=vm, out_shape=jax.ShapeDtypeStruct((N_DEV,) + v.shape, v.dtype),
                          scratch_shapes=[pltpu.SemaphoreType.DMA((N_DEV - 1,)), pltpu.SemaphoreType.DMA((N_DEV - 1,)),
                                          pltpu.SemaphoreType.DMA])(v)


ADD_ROWS = 512


def add_pair(place, a, b, name):
    L, n, hr, cols = b.shape
    tr = _divisors(hr, 2 * SUBLANES, ADD_ROWS)[0]
    nb = hr // tr

    def body(p_ref, a_ref, b_ref, o_ref):
        del p_ref
        o_ref[...] = (a_ref[...].astype(F32) + b_ref[...].astype(F32)).astype(o_ref.dtype)

    blk = pl.BlockSpec((None, None, tr, cols), lambda l, d, i, p: (l, d, i, 0))
    grid_spec = pltpu.PrefetchScalarGridSpec(
        num_scalar_prefetch=1, grid=(L, n, nb),
        in_specs=[pl.BlockSpec((None, None, tr, cols), lambda l, d, i, p: (l, d, p[0] * nb + i, 0)), blk], out_specs=blk)
    return pl.pallas_call(body, name=name, grid_spec=grid_spec, out_shape=jax.ShapeDtypeStruct(b.shape, b.dtype),
                          compiler_params=_params(("parallel", "parallel", "parallel")))(place, a, b)


def add_chips(place, own, others, layer, stacked, name):
    _, n, hr, cols = others.shape
    tr = _divisors(hr, 2 * SUBLANES, ADD_ROWS)[0]
    nb = hr // tr
    create = isinstance(stacked, tuple)

    def body(p_ref, own_ref, *refs):
        del p_ref
        acc = own_ref[...].astype(F32)
        for k in range(n):
            acc = acc + refs[k][...].astype(F32)
        refs[-1][...] = acc

    ins = [pl.BlockSpec((None, None, tr, cols), lambda i, p: (0, p[1], i, 0))]
    ins += [pl.BlockSpec((None, None, tr, cols), functools.partial(lambda k, i, p: (0, k, i, 0), k)) for k in range(n)]
    grid_spec = pltpu.PrefetchScalarGridSpec(num_scalar_prefetch=1, grid=(nb,), in_specs=ins + ([] if create else [ANY_SPEC]),
                                             out_specs=pl.BlockSpec((None, tr, cols), lambda i, p: (layer, p[0] * nb + i, 0)))
    shape = stacked if create else stacked.shape
    return pl.pallas_call(body, name=name, grid_spec=grid_spec, out_shape=jax.ShapeDtypeStruct(shape, F32),
                          input_output_aliases={} if create else {n + 2: 0},
                          compiler_params=_params(("parallel",)))(place, own, *([others] * n), *([] if create else [stacked]))


def _alpha(depth):
    return (2 * depth) ** 0.25


def _wmm(a, weight, mode, name, deps=(), **more):
    arr, how = weight
    return mm(a, arr, mode, name, deps=deps, **how, **more)


def layer_fwd(h, mem, w, tab, alpha, deps=(), late=None):
    D = h.shape[1]
    proj = _wmm(h, w["w_in"], "nn", "mm_proj", deps)
    xc, r, ig, a, b = rg_gates_fwd(proj, w["conv_w"], w["conv_b"], w["w_rg"], w["b_rg"], w["w_ig"], w["b_ig"], w["lru_lambda"], "rg_gates_fwd")
    hs, y_rnn = rg_scan_fwd(proj, a, b, "rg_scan_fwd")
    y_attn = attn_fwd(proj, w["sinks"], tab, D, "attn_fwd")
    deps = ()
    if late is not None:
        rest, deps = late(y_attn)
        w = {**w, **rest}
    pr = _wmm(y_rnn, w["w_br_rnn"], "nn", "mm_br_rnn", deps)
    pa = _wmm(y_attn, w["w_br_attn"], "nn", "mm_br_attn")
    merged = merge_fwd(proj, pr, pa, "merge_fwd")
    mix = _wmm(merged, w["w_out"], "nn", "mm_out")
    h1, xh1, rs1 = ln_fwd(h, mix, w["ln1_g"], w["ln1_b"], alpha, "ln1_fwd")
    qc = _wmm(h1, w["cq_w"], "nn", "mm_cq")
    kv = _wmm(mem, w["ckv_w"], "nn", "mm_ckv")
    o = cross_fwd(qc, kv, "cross_fwd")
    co = _wmm(o, w["co_w"], "nn", "mm_co")
    h2, xh2, rs2 = ln_fwd(h1, co, w["ln2_g"], w["ln2_b"], alpha, "ln2_fwd")
    gu = _wmm(h2, w["ffn_wi"], "nn", "mm_ffn_wi", out_blocks=2)
    act = swiglu_fwd(gu, "swiglu_fwd")
    f = _wmm(act, w["ffn_wo"], "nn", "mm_ffn_wo")
    h3, xh3, rs3 = ln_fwd(h2, f, w["ln3_g"], w["ln3_b"], alpha, "ln3_fwd")
    saved = dict(h=h, proj=proj, xc=xc, r=r, ig=ig, a=a, hs=hs, y_rnn=y_rnn, y_attn=y_attn, pr=pr, pa=pa, xh1=xh1, rs1=rs1, h1=h1,
                 qc=qc, kv=kv, o=o, xh2=xh2, rs2=rs2, h2=h2, gu=gu, xh3=xh3, rs3=rs3)
    return h3, saved, w


def layer_bwd(dh, mem, w, s, tab, alpha, deps=(), halfway=None):
    D = dh.shape[1]
    g = {}
    wg = dict(out_dtype=MXU_DTYPE)
    dz3, g["ln3_g"], g["ln3_b"] = ln_bwd(dh, None, s["xh3"], s["rs3"], w["ln3_g"], 1.0, "ln3_bwd")
    act = swiglu_fwd(s["gu"], "swiglu_refwd")
    g["ffn_wo"] = mm(act, dz3, "tn", "mm_d_ffn_wo", deps=deps, **wg)
    dact = _wmm(dz3, w["ffn_wo"], "nt", "mm_dact")
    dgu = swiglu_bwd(s["gu"], dact, "swiglu_bwd")
    g["ffn_wi"] = mm(s["h2"], dgu, "tn", "mm_d_ffn_wi", b_blocks=2, out_blocks=N_CHIPS, **wg)
    dh2 = _wmm(dgu, w["ffn_wi"], "nt", "mm_dh2", a_blocks=2)
    dz2, g["ln2_g"], g["ln2_b"] = ln_bwd(dz3, dh2, s["xh2"], s["rs2"], w["ln2_g"], alpha, "ln2_bwd")
    g["co_w"] = mm(s["o"], dz2, "tn", "mm_d_co", **wg)
    do = _wmm(dz2, w["co_w"], "nt", "mm_do")
    dqc, dkv = cross_bwd(s["qc"], s["kv"], do, "cross_bwd")
    g["cq_w"] = mm(s["h1"], dqc, "tn", "mm_d_cq", **wg)
    g["ckv_w"] = mm(mem, dkv, "tn", "mm_d_ckv", out_blocks=N_CHIPS, **wg)
    dh1 = _wmm(dqc, w["cq_w"], "nt", "mm_dh1")
    deps = halfway(g, dh1) if halfway is not None else ()
    dz1, g["ln1_g"], g["ln1_b"] = ln_bwd(dz2, dh1, s["xh1"], s["rs1"], w["ln1_g"], alpha, "ln1_bwd")
    merged = merge_fwd(s["proj"], s["pr"], s["pa"], "merge_refwd")
    g["w_out"] = mm(merged, dz1, "tn", "mm_d_out", deps=deps, **wg)
    dm = _wmm(dz1, w["w_out"], "nt", "mm_dmerged")
    dpr, dpa, dg_rnn, dg_attn = merge_bwd(s["proj"], s["pr"], s["pa"], dm, "merge_bwd")
    g["w_br_rnn"] = mm(s["y_rnn"], dpr, "tn", "mm_d_br_rnn", **wg)
    g["w_br_attn"] = mm(s["y_attn"], dpa, "tn", "mm_d_br_attn", **wg)
    dy_rnn = _wmm(dpr, w["w_br_rnn"], "nt", "mm_dy_rnn")
    dy_attn = _wmm(dpa, w["w_br_attn"], "nt", "mm_dy_attn")
    dq, dkb, dvb, dsink = attn_bwd(s["proj"], w["sinks"], tab, s["y_attn"], dy_attn, D, "attn_bwd")
    dk, dv = band_fold(dkb, dvb, "band_fold")
    g["sinks"] = dsink[:, :w["sinks"].shape[0]]
    dgr, gt = rg_scan_bwd(s["proj"], dy_rnn, s["hs"], s["a"], "rg_scan_bwd")
    dxc, g["w_rg"], g["w_ig"], g["b_rg"], g["b_ig"], g["lru_lambda"] = rg_gates_bwd(
        gt, s["hs"], s["xc"], s["r"], s["ig"], w["w_rg"], w["w_ig"], w["lru_lambda"], "rg_gates_bwd")
    dxr, g["conv_w"], g["conv_b"] = rg_conv_bwd(s["proj"], dxc, w["conv_w"], "rg_conv_bwd")
    dproj = jnp.concatenate([dxr, dgr, dq, dk, dv, dg_rnn, dg_attn], axis=1)
    g["w_in"] = mm(s["h"], dproj, "tn", "mm_d_in")
    dhm = _wmm(dproj, w["w_in"], "nt", "mm_dh")
    return axpby(dz1, dhm, alpha, "layer_dx"), g


def local_step(x, mem, target, depth, weights_of, grads_halfway, grads_done):
    alpha = _alpha(depth)
    tab = rope_table(x.shape[0])
    h, saved, layers = x, [], []
    for l in range(depth):
        wl, deps, late = weights_of(l, h)
        h, s, wl = layer_fwd(h, mem, wl, tab, alpha, deps, late)
        layers.append(wl)
        saved.append(s)
    dh, loss = loss_head(h, target, "loss_head")
    deps = ()
    for l in reversed(range(depth)):
        dh, g = layer_bwd(dh, mem, layers[l], saved[l], tab, alpha, deps, grads_halfway(l))
        deps = grads_done(l, g, dh)
    return loss, dh


def _pad_rows(flat):
    n = flat.shape[0]
    rows = -(-n // (LANES * SUBLANES)) * SUBLANES
    return jnp.pad(flat, (0, rows * LANES - n)).reshape(rows, LANES)


def kernel(x, mem, w_in, conv_w, conv_b, w_rg, b_rg, w_ig, b_ig, lru_lambda, w_br_rnn, w_br_attn, sinks, w_out, ln1_g, ln1_b, cq_w, ckv_w, co_w, ln2_g, ln2_b, ffn_wi, ffn_wo, ln3_g, ln3_b, loss_target, m_w_in, m_conv_w, m_conv_b, m_w_rg, m_b_rg, m_w_ig, m_b_ig, m_lru_lambda, m_w_br_rnn, m_w_br_attn, m_sinks, m_w_out, m_ln1_g, m_ln1_b, m_cq_w, m_ckv_w, m_co_w, m_ln2_g, m_ln2_b, m_ffn_wi, m_ffn_wo, m_ln3_g, m_ln3_b, v_w_in, v_conv_w, v_conv_b, v_w_rg, v_b_rg, v_w_ig, v_b_ig, v_lru_lambda, v_w_br_rnn, v_w_br_attn, v_sinks, v_w_out, v_ln1_g, v_ln1_b, v_cq_w, v_ckv_w, v_co_w, v_ln2_g, v_ln2_b, v_ffn_wi, v_ffn_wo, v_ln3_g, v_ln3_b):
    args = dict(locals())
    w = {n: args[n] for n in WEIGHTS}
    m = {n: args["m_" + n] for n in WEIGHTS}
    v = {n: args["v_" + n] for n in WEIGHTS}
    cx, cy, cc = _place()
    chip = 2 * cx + cy
    L = w_in.shape[0]

    place = jnp.stack([cc, chip]).astype(jnp.int32)
    cw_rows = _pad_rows(conv_w.reshape(-1))
    cw_all = gather_devices(cw_rows, "gather_conv_w")[0::2]
    cw_parts = cw_all.reshape(N_CHIPS, -1)[:, :conv_w.size].reshape((N_CHIPS,) + conv_w.shape)
    conv_full = jnp.concatenate([cw_parts[k] for k in range(N_CHIPS)], axis=2)

    shards = [{n: w[n][l].astype(MXU_DTYPE) for n in BIG} for l in range(L)]
    late_names = tuple(n for n in BIG if n not in GATHER_FIRST)
    gathering = {(0, GATHER_FIRST): gather_start([shards[0][n] for n in GATHER_FIRST], cw_rows, "gather_start_0a")}
    gathering[0, late_names] = gather_start([shards[0][n] for n in late_names], gathering[0, GATHER_FIRST][4], "gather_start_0b")

    def gathered(l, names, after, tag):
        lands = gather_pass(gather_wait(gathering.pop((l, names)), after, f"gather_wait_{tag}"), f"gather_pass_{tag}")
        wl = {}
        for n, gw in zip(names, lands):
            rows_joined = gw.reshape(gw.shape[:-3] + (-1, gw.shape[-1]))
            if n == "w_in":
                wl[n] = (jnp.concatenate([gw[k] for k in range(N_CHIPS)], axis=1), {})
            elif n in COL_BLOCKED:
                wl[n] = (gw, dict(b_blocks=N_CHIPS))
            elif n in GATE_WEIGHTS:
                wl[n] = rows_joined
            else:
                wl[n] = (rows_joined, {})
        return wl, lands

    def start_next(l, after):
        if l + 1 == L:
            return ()
        gathering[l + 1, BIG] = gather_start([shards[l + 1][n] for n in BIG], after, f"gather_start_{l + 1}")
        return (gathering[l + 1, BIG][4],)

    def weights_of(l, h):
        deps, late = (), None
        if l == 0:
            wl, _ = gathered(0, GATHER_FIRST, h, "0a")

            def late(after):
                rest, lands = gathered(0, late_names, after, "0b")
                return rest, start_next(0, lands[0])
        else:
            wl, lands = gathered(l, BIG, h, str(l))
            deps = start_next(l, lands[0])
        for n in SMALL:
            wl[n] = conv_full[l] if n == "conv_w" else w[n][l] if n == "sinks" else w[n][l][None, :]
        return wl, deps, late

    def for_chips(n, g):
        if n in COL_BLOCKED:
            return g
        if n in GATE_WEIGHTS:
            nb, bw, _ = g.shape
            g = g.reshape(nb, N_CHIPS, bw // N_CHIPS, bw).transpose(1, 0, 2, 3).reshape(N_CHIPS, nb * bw // N_CHIPS, bw)
        elif SHARD_AXIS[n] == 0:
            g = g.reshape(N_CHIPS, g.shape[0] // N_CHIPS, g.shape[1])
        else:
            g = jnp.stack(jnp.split(g, N_CHIPS, axis=1))
        return g.astype(MXU_DTYPE)

    reduced, scattering, small_grads = {}, {}, [None] * L
    late_grads = tuple(n for n in BIG if n not in SCATTER_FIRST)

    def start_scatter(l, names, g, after, tag):
        partial_sums = [for_chips(n, g[n])[None] for n in names]
        from_sibling = swap_sibling(partial_sums, f"grad_to_sibling_{tag}")
        chip_sums = [add_pair(place, a, b, f"grad_add_pair_{n}_{l}") for n, a, b in zip(names, partial_sums, from_sibling)]
        scattering[l, names] = scatter_start(chip_sums, after, f"grad_scatter_start_{tag}")
        return (scattering[l, names][4],)

    def finish_layer(l, after):
        for names in [k[1] for k in list(scattering) if k[0] == l]:
            tag = str(l) if names == BIG else f"{l}{'a' if names == SCATTER_FIRST else 'b'}"
            chip_sums, from_chips = scatter_wait(scattering.pop((l, names)), after, f"grad_scatter_wait_{tag}")
            for n, own, others in zip(names, chip_sums, from_chips):
                target = reduced.get(n, (L, 2 * own.shape[2], own.shape[3]))
                reduced[n] = add_chips(place, own, others, l, target, f"grad_add_chips_{n}_{l}")
        reduced.update(zip(BIG, join_halves([reduced[n] for n in BIG], l, f"grad_join_{l}")))

    def grads_halfway(l):
        if l > 0:
            return None

        def halfway(g, after):
            if L > 1:
                finish_layer(1, after)
            return start_scatter(0, SCATTER_FIRST, g, after, "0a")

        return halfway

    def grads_done(l, g, dh):
        small_grads[l] = {n: g[n] for n in SMALL}
        if l == 0:
            return start_scatter(0, late_grads, g, dh, "0b")
        if l + 1 < L:
            finish_layer(l + 1, dh)
        return start_scatter(l, BIG, g, dh, str(l))

    loss11, dx = local_step(x[0], mem[0], loss_target[0], L, weights_of, grads_halfway, grads_done)
    finish_layer(0, dx)
    loss = lax.psum(loss11[0, 0], ("x", "y", "c"))
    gshard = {n: reduced[n].reshape(w[n].shape) for n in BIG}

    small_full = {n: jnp.stack([gl[n] for gl in small_grads]).reshape(w[n].shape[:1] + ((CONV_WIDTH, -1) if n == "conv_w" else (-1,)))
                  for n in SMALL}
    small_flat = jnp.concatenate([small_full[n].reshape(-1) for n in SMALL])
    small_sum = sum_devices(gather_devices(_pad_rows(small_flat), "gather_small_grads"), "sum_small_grads").reshape(-1)
    off = 0
    for n in SMALL:
        gfull = small_sum[off:off + small_full[n].size].reshape(small_full[n].shape)
        off += small_full[n].size
        if n == "conv_w":
            width = conv_w.shape[2]
            gfull = lax.dynamic_slice_in_dim(gfull, chip * width, width, axis=2)
        gshard[n] = gfull

    delta, new_m, new_v, grad = {}, {}, {}, {}
    for n in WEIGHTS:
        delta[n], new_m[n], new_v[n], grad[n] = adamw(w[n], gshard[n], m[n], v[n], "adamw_" + n)
    return (loss, dx[None], *[grad[n] for n in WEIGHTS], *[delta[n] for n in WEIGHTS], *[new_m[n] for n in WEIGHTS],
            *[new_v[n] for n in WEIGHTS])
```

```python
import functools
import math

import jax
import jax.numpy as jnp
import numpy as np
from jax import lax
from jax.experimental import pallas as pl
from jax.experimental.pallas import tpu as pltpu

F32 = jnp.float32
BF16 = jnp.bfloat16
MXU_DTYPE = BF16

HEAD_DIM = 64
N_KV_HEADS = 2
WINDOW = 128
ROT_DIM = HEAD_DIM // 4
ROPE_THETA = 500000.0
CROSS_HEADS = 4
RNN_BLOCKS = 4
CONV_WIDTH = 4
LRU_C = 8.0
LN_EPS = 1e-5
NEG_INF = -1e30
ADAM_LR = 0.001
ADAM_B1 = 0.9
ADAM_B2 = 0.999
ADAM_EPS = 1e-08
ADAM_WD = 0.01
ADAM_STEP = 10

VMEM_BYTES_V7X = 64 * 1024 * 1024
VMEM_BLOCK_BUDGET = 36 * 1024 * 1024
LANES = 128
SUBLANES = 8

MESH_ID = pl.DeviceIdType.MESH
N_CHIPS = 4
N_DEV = 8

BIG = ("w_in", "w_rg", "w_ig", "w_br_rnn", "w_br_attn", "w_out", "cq_w", "ckv_w", "co_w", "ffn_wi", "ffn_wo")
SHARD_AXIS = {"w_in": 1, "w_rg": 1, "w_ig": 1, "w_br_rnn": 0, "w_br_attn": 0, "w_out": 0, "cq_w": 0, "ckv_w": 1,
              "co_w": 0, "ffn_wi": 1, "ffn_wo": 0}
SMALL = ("conv_w", "conv_b", "b_rg", "b_ig", "lru_lambda", "sinks", "ln1_g", "ln1_b", "ln2_g", "ln2_b", "ln3_g", "ln3_b")
WEIGHTS = ("w_in", "conv_w", "conv_b", "w_rg", "b_rg", "w_ig", "b_ig", "lru_lambda", "w_br_rnn", "w_br_attn", "sinks",
           "w_out", "ln1_g", "ln1_b", "cq_w", "ckv_w", "co_w", "ln2_g", "ln2_b", "ffn_wi", "ffn_wo", "ln3_g", "ln3_b")
GATE_WEIGHTS = ("w_rg", "w_ig")
COL_BLOCKED = ("ckv_w", "ffn_wi")
GATHER_FIRST = ("w_in", "w_rg", "w_ig")
SCATTER_FIRST = ("ffn_wo", "ffn_wi", "co_w", "cq_w", "ckv_w")


def _params(dims=None, vmem=None):
    return pltpu.CompilerParams(dimension_semantics=dims, vmem_limit_bytes=vmem)


def _vmem_limit(block_bytes, temp_bytes=0):
    want = int(2 * block_bytes + temp_bytes) + (6 << 20)
    return max(32 << 20, min(want, VMEM_BYTES_V7X - (6 << 20)))


def _divisors(n, align, cap):
    out = [d for d in range(align, min(n, cap) + 1, align) if n % d == 0]
    if n <= cap and n not in out:
        out.append(n)
    return sorted(out, reverse=True) or [n]


PIN_MIN_ELEMENTS = 1 << 18


def hbm_call(body, **kw):
    def in_hbm(s):
        return pltpu.HBM(s.shape, s.dtype) if math.prod(s.shape) >= PIN_MIN_ELEMENTS else s

    shapes = kw.pop("out_shape")
    shapes = [in_hbm(s) for s in shapes] if isinstance(shapes, (list, tuple)) else in_hbm(shapes)
    call = pl.pallas_call(body, out_shape=shapes, **kw)

    def run(*args):
        return call(*[pltpu.with_memory_space_constraint(a, pltpu.HBM) if a.size >= PIN_MIN_ELEMENTS else a for a in args])

    return run


def _sigmoid(x):
    return 1.0 / (1.0 + jnp.exp(-x))


def _gelu_parts(x):
    c = math.sqrt(2.0 / math.pi)
    u = c * (x + 0.044715 * x * x * x)
    t = jnp.tanh(u)
    return t, c * (1.0 + 3 * 0.044715 * x * x)


def _gelu(x):
    t, _ = _gelu_parts(x)
    return 0.5 * x * (1.0 + t)


def _gelu_grad(x):
    t, du = _gelu_parts(x)
    return 0.5 * (1.0 + t) + 0.5 * x * (1.0 - t * t) * du


def _neg_expm1(x):
    series = x * (1.0 + x * (0.5 + x * (1.0 / 6 + x * (1.0 / 24 + x * (1.0 / 120)))))
    return -jnp.where(x > -0.1, series, jnp.exp(x) - 1.0)


def _softplus_neg(lam):
    x = -lam
    return jnp.maximum(x, 0.0) + jnp.log1p(jnp.exp(-jnp.abs(x)))


STEP_US = 0.35
HBM_BYTES_PER_US = 2.5e6
MXU_FLOPS_PER_US = 7e8


def mm(a, b, mode, name, *, b_index=(), a_blocks=0, b_blocks=0, out_blocks=0, out_dtype=F32, deps=()):
    nlead = len(b_index) + (1 if b_blocks else 0)
    bk, bn = b.shape[nlead:]
    M, K = (a.shape[-1], a.shape[-2]) if mode == "tn" else (a.shape[-2], a.shape[-1] * max(a_blocks, 1))
    N = bk if mode == "nt" else bn * max(b_blocks, 1) if mode == "nn" or mode == "tn" else bn
    asz, bsz, osz = a.dtype.itemsize, b.dtype.itemsize, jnp.dtype(out_dtype).itemsize
    n_unit = math.gcd(N // max(out_blocks, 1), N // max(b_blocks, 1) if mode != "nt" else N)
    k_unit = math.gcd(K // max(a_blocks, 1), K // max(b_blocks, 1) if mode == "nt" else K)
    tms = _divisors(M, LANES if mode == "tn" else SUBLANES, 2048)
    tns = _divisors(n_unit, LANES, 2048)
    tks = _divisors(k_unit, LANES, k_unit)
    best = None
    for tm in tms:
        for tn in tns:
            for tk in tks:
                nk = K // tk
                scratch = tm * tn * 4 if (nk > 1 and osz != 4) else 0
                blocks = tm * tk * asz + tn * tk * bsz + tm * tn * osz
                temps = tm * tk * (2 + (4 if mode == "tn" else 0)) + tn * tk * 2 + tm * tn * 4 + scratch
                if 2 * blocks + temps > VMEM_BLOCK_BUDGET + (8 << 20):
                    continue
                ni, nj = M // tm, N // tn
                traffic = M * K * asz * (nj if nk > 1 else 1) + N * K * bsz * (1 if nj * nk == 1 else ni) + M * N * osz
                busy = max(traffic / HBM_BYTES_PER_US, 2.0 * M * N * K / MXU_FLOPS_PER_US)
                cost = ni * nj * nk * STEP_US + busy + blocks / HBM_BYTES_PER_US
                if best is None or cost < best[0]:
                    best = (cost, tm, tn, tk, blocks, temps)
    _, tm, tn, tk, blocks, temps = best
    nk = K // tk
    use_scratch = nk > 1 and osz != 4

    def split(index, total, blocks, tile):
        per = total // blocks // tile
        return index // per, index % per

    def body(a_ref, b_ref, *rest):
        o_ref, acc = rest[len(deps)], rest[len(deps) + 1:]
        av = a_ref[...].astype(MXU_DTYPE)
        bv = b_ref[...].astype(MXU_DTYPE)
        dn = {"nn": (((1,), (0,)), ((), ())), "nt": (((1,), (1,)), ((), ())), "tn": (((0,), (0,)), ((), ()))}[mode]
        r = lax.dot_general(av, bv, dn, preferred_element_type=F32)
        if nk == 1:
            o_ref[...] = r.astype(o_ref.dtype)
        else:
            acc_ref = acc[0] if use_scratch else o_ref

            @pl.when(pl.program_id(2) == 0)
            def _():
                acc_ref[...] = r

            @pl.when(pl.program_id(2) > 0)
            def _():
                acc_ref[...] += r

            if use_scratch:
                @pl.when(pl.program_id(2) == nk - 1)
                def _():
                    o_ref[...] = acc_ref[...].astype(o_ref.dtype)

    if mode == "tn":
        a_spec = pl.BlockSpec((tk, tm), lambda i, j, k: (k, i))
    elif a_blocks:
        a_spec = pl.BlockSpec((None, tm, tk), lambda i, j, k: (split(k, K, a_blocks, tk)[0], i, split(k, K, a_blocks, tk)[1]))
    else:
        a_spec = pl.BlockSpec((tm, tk), lambda i, j, k: (i, k))
    lead = (None,) * nlead
    if mode == "nt":
        bmap = ((lambda i, j, k: b_index + (split(k, K, b_blocks, tk)[0], j, split(k, K, b_blocks, tk)[1])) if b_blocks
                else (lambda i, j, k: b_index + (j, k)))
        b_spec = pl.BlockSpec(lead + (tn, tk), bmap)
    else:
        bmap = ((lambda i, j, k: b_index + (split(j, N, b_blocks, tn)[0], k, split(j, N, b_blocks, tn)[1])) if b_blocks
                else (lambda i, j, k: b_index + (k, j)))
        b_spec = pl.BlockSpec(lead + (tk, tn), bmap)
    if out_blocks:
        o_spec = pl.BlockSpec((None, tm, tn), lambda i, j, k: (split(j, N, out_blocks, tn)[0], i, split(j, N, out_blocks, tn)[1]))
        o_shape = jax.ShapeDtypeStruct((out_blocks, M, N // out_blocks), out_dtype)
    else:
        o_spec = pl.BlockSpec((tm, tn), lambda i, j, k: (i, j))
        o_shape = jax.ShapeDtypeStruct((M, N), out_dtype)
    return hbm_call(
        body, name=name, grid=(M // tm, N // tn, nk), in_specs=[a_spec, b_spec] + [pl.BlockSpec(memory_space=pl.ANY)] * len(deps),
        out_specs=o_spec, out_shape=o_shape, scratch_shapes=[pltpu.VMEM((tm, tn), F32)] if use_scratch else [],
        compiler_params=_params(("parallel", "parallel", "arbitrary"), _vmem_limit(blocks, temps)),
    )(a, b, *deps)


ROW_TILE = 512
GATE_ROWS = 1024


def ln_fwd(h, f, g, b, alpha, name):
    S, D = h.shape
    tr = min(ROW_TILE, S)

    def body(h_ref, f_ref, g_ref, b_ref, y_ref, xh_ref, rs_ref):
        z = alpha * h_ref[...] + f_ref[...]
        mu = jnp.mean(z, axis=-1, keepdims=True)
        zc = z - mu
        var = jnp.mean(zc * zc, axis=-1, keepdims=True)
        rs = lax.rsqrt(var + LN_EPS)
        xh = zc * rs
        y_ref[...] = xh * g_ref[...] + b_ref[...]
        xh_ref[...] = xh
        rs_ref[...] = rs

    row = pl.BlockSpec((tr, D), lambda i: (i, 0))
    vec = pl.BlockSpec((1, D), lambda i: (0, 0))
    return hbm_call(
        body, name=name, grid=(S // tr,), in_specs=[row, row, vec, vec],
        out_specs=[row, row, pl.BlockSpec((tr, 1), lambda i: (i, 0))],
        out_shape=[jax.ShapeDtypeStruct((S, D), F32), jax.ShapeDtypeStruct((S, D), F32), jax.ShapeDtypeStruct((S, 1), F32)],
        compiler_params=_params(("parallel",), 48 << 20),
    )(h, f, g, b)


def ln_bwd(dy_a, dy_b, xh, rs, g, c1, name):
    S, D = xh.shape
    tr = min(ROW_TILE, S)
    two = dy_b is not None

    def body(*refs):
        if two:
            a_ref, b_ref, xh_ref, rs_ref, g_ref, dz_ref, dg_ref, db_ref = refs
            dy = c1 * a_ref[...] + b_ref[...]
        else:
            a_ref, xh_ref, rs_ref, g_ref, dz_ref, dg_ref, db_ref = refs
            dy = a_ref[...]
        x = xh_ref[...]
        dyg = dy * g_ref[...]
        m1 = jnp.mean(dyg, axis=-1, keepdims=True)
        m2 = jnp.mean(dyg * x, axis=-1, keepdims=True)
        dz_ref[...] = rs_ref[...] * (dyg - m1 - x * m2)

        @pl.when(pl.program_id(0) == 0)
        def _():
            dg_ref[...] = jnp.zeros_like(dg_ref)
            db_ref[...] = jnp.zeros_like(db_ref)

        dg_ref[...] += jnp.sum(dy * x, axis=0, keepdims=True)
        db_ref[...] += jnp.sum(dy, axis=0, keepdims=True)

    row = pl.BlockSpec((tr, D), lambda i: (i, 0))
    vec = pl.BlockSpec((1, D), lambda i: (0, 0))
    ins = [row, row] if two else [row]
    args = (dy_a, dy_b) if two else (dy_a,)
    return hbm_call(
        body, name=name, grid=(S // tr,), in_specs=ins + [row, pl.BlockSpec((tr, 1), lambda i: (i, 0)), vec],
        out_specs=[row, vec, vec],
        out_shape=[jax.ShapeDtypeStruct((S, D), F32), jax.ShapeDtypeStruct((1, D), F32), jax.ShapeDtypeStruct((1, D), F32)],
        compiler_params=_params(("arbitrary",), 48 << 20),
    )(*args, xh, rs, g)


def axpby(a, b, c1, name):
    S, D = a.shape
    tr = min(ROW_TILE, S)

    def body(a_ref, b_ref, o_ref):
        o_ref[...] = c1 * a_ref[...] + b_ref[...]

    row = pl.BlockSpec((tr, D), lambda i: (i, 0))
    return hbm_call(body, name=name, grid=(S // tr,), in_specs=[row, row], out_specs=row,
                          out_shape=jax.ShapeDtypeStruct((S, D), F32), compiler_params=_params(("parallel",)))(a, b)


def loss_head(y, t, name):
    S, D = y.shape
    tr = min(ROW_TILE, S)
    nsteps = S // tr

    def body(y_ref, t_ref, dy_ref, l_ref, acc_ref):
        i = pl.program_id(0)

        @pl.when(i == 0)
        def _():
            acc_ref[...] = jnp.zeros_like(acc_ref)

        e = y_ref[...] - t_ref[...]
        dy_ref[...] = e * (1.0 / D)
        acc_ref[...] += jnp.sum(e * e, axis=0, keepdims=True)

        @pl.when(i == nsteps - 1)
        def _():
            l_ref[...] = jnp.sum(acc_ref[...], axis=1, keepdims=True) * (0.5 / D)

    row = pl.BlockSpec((tr, D), lambda i: (i, 0))
    return hbm_call(
        body, name=name, grid=(nsteps,), in_specs=[row, row],
        out_specs=[row, pl.BlockSpec((1, 1), lambda i: (0, 0))],
        out_shape=[jax.ShapeDtypeStruct((S, D), F32), jax.ShapeDtypeStruct((1, 1), F32)],
        scratch_shapes=[pltpu.VMEM((1, D), F32)], compiler_params=_params(("arbitrary",)),
    )(y, t)


SWIGLU_ROWS = 256


def swiglu_fwd(gu, name):
    _, S, Fh = gu.shape
    tc = _divisors(Fh, LANES, 1536)[0]
    tr = min(SWIGLU_ROWS, S)

    def body(gu_ref, o_ref):
        g = gu_ref[0]
        o_ref[...] = g * _sigmoid(g) * gu_ref[1]

    return hbm_call(
        body, name=name, grid=(S // tr, Fh // tc), in_specs=[pl.BlockSpec((2, tr, tc), lambda i, j: (0, i, j))],
        out_specs=pl.BlockSpec((tr, tc), lambda i, j: (i, j)), out_shape=jax.ShapeDtypeStruct((S, Fh), F32),
        compiler_params=_params(("parallel", "parallel")),
    )(gu)


def swiglu_bwd(gu, dact, name):
    _, S, Fh = gu.shape
    tc = _divisors(Fh, LANES, 1536)[0]
    tr = min(SWIGLU_ROWS, S)

    def body(gu_ref, d_ref, o_ref):
        g, u, d = gu_ref[0], gu_ref[1], d_ref[...]
        s = _sigmoid(g)
        o_ref[0] = d * u * (s * (1.0 + g * (1.0 - s)))
        o_ref[1] = d * (g * s)

    both = pl.BlockSpec((2, tr, tc), lambda i, j: (0, i, j))
    return hbm_call(
        body, name=name, grid=(S // tr, Fh // tc), in_specs=[both, pl.BlockSpec((tr, tc), lambda i, j: (i, j))],
        out_specs=both, out_shape=jax.ShapeDtypeStruct((2, S, Fh), F32), compiler_params=_params(("parallel", "parallel")),
    )(gu, dact)


GATE_COLS = 256


def merge_fwd(proj, pr, pa, name):
    S, D = pr.shape
    tr = min(GATE_ROWS, S)
    c0 = (3 * D + 2 * N_KV_HEADS * HEAD_DIM) // GATE_COLS
    c1 = c0 + D // GATE_COLS

    def body(gr_ref, ga_ref, pr_ref, pa_ref, o_ref):
        o_ref[...] = _sigmoid(gr_ref[...]) * pr_ref[...] + _sigmoid(ga_ref[...]) * pa_ref[...]

    blk = pl.BlockSpec((tr, GATE_COLS), lambda i, j: (i, j))
    return hbm_call(
        body, name=name, grid=(S // tr, D // GATE_COLS),
        in_specs=[pl.BlockSpec((tr, GATE_COLS), lambda i, j: (i, c0 + j)), pl.BlockSpec((tr, GATE_COLS), lambda i, j: (i, c1 + j)),
                  blk, blk],
        out_specs=blk, out_shape=jax.ShapeDtypeStruct((S, D), F32), compiler_params=_params(("parallel", "parallel")),
    )(proj, proj, pr, pa)


def merge_bwd(proj, pr, pa, dm, name):
    S, D = pr.shape
    tr = min(GATE_ROWS, S)
    c0 = (3 * D + 2 * N_KV_HEADS * HEAD_DIM) // GATE_COLS
    c1 = c0 + D // GATE_COLS

    def body(gr_ref, ga_ref, pr_ref, pa_ref, dm_ref, dpr_ref, dpa_ref, dgr_ref, dga_ref):
        sr, sa, d = _sigmoid(gr_ref[...]), _sigmoid(ga_ref[...]), dm_ref[...]
        dpr_ref[...] = d * sr
        dpa_ref[...] = d * sa
        dgr_ref[...] = d * pr_ref[...] * (sr * (1.0 - sr))
        dga_ref[...] = d * pa_ref[...] * (sa * (1.0 - sa))

    blk = pl.BlockSpec((tr, GATE_COLS), lambda i, j: (i, j))
    sds = jax.ShapeDtypeStruct((S, D), F32)
    return hbm_call(
        body, name=name, grid=(S // tr, D // GATE_COLS),
        in_specs=[pl.BlockSpec((tr, GATE_COLS), lambda i, j: (i, c0 + j)), pl.BlockSpec((tr, GATE_COLS), lambda i, j: (i, c1 + j)),
                  blk, blk, blk],
        out_specs=[blk, blk, blk, blk], out_shape=[sds, sds, sds, sds], compiler_params=_params(("parallel", "parallel")),
    )(proj, proj, pr, pa, dm)


RG_ROWS = 512


def _shift_down(cur, prev, d, row, first):
    halo = jnp.where(first, 0.0, pltpu.roll(prev, d, 0))
    return jnp.where(row >= d, pltpu.roll(cur, d, 0), halo)


def _shift_up(cur, nxt, d, row, last, tr):
    halo = jnp.where(last, 0.0, pltpu.roll(nxt, tr - d, 0))
    return jnp.where(row < tr - d, pltpu.roll(cur, tr - d, 0), halo)


def _lru_coeffs(r, lam):
    sp = _softplus_neg(lam)
    la = -LRU_C * r * sp
    return sp, la, jnp.exp(la), _neg_expm1(2.0 * la)


def rg_gates_fwd(proj, conv_w, conv_b, w_rg, b_rg, w_ig, b_ig, lam, name):
    S = proj.shape[0]
    nblk, bw, _ = w_rg.shape
    D = nblk * bw
    tr = min(RG_ROWS, S)

    def body(xr_ref, xp_ref, cw_ref, cb_ref, wr_ref, br_ref, wi_ref, bi_ref, lam_ref, xc_ref, r_ref, i_ref, a_ref, b_ref):
        first = pl.program_id(1) == 0
        cur, prev = xr_ref[...], xp_ref[...]
        row = lax.broadcasted_iota(jnp.int32, cur.shape, 0)
        xc = cb_ref[...]
        for k in range(CONV_WIDTH - 1):
            xc = xc + _shift_down(cur, prev, CONV_WIDTH - 1 - k, row, first) * cw_ref[k:k + 1, :]
        xc = xc + cur * cw_ref[CONV_WIDTH - 1:CONV_WIDTH, :]
        xm = xc.astype(MXU_DTYPE)
        r = _sigmoid(jnp.dot(xm, wr_ref[...].astype(MXU_DTYPE), preferred_element_type=F32) + br_ref[...])
        ig = _sigmoid(jnp.dot(xm, wi_ref[...].astype(MXU_DTYPE), preferred_element_type=F32) + bi_ref[...])
        _, _, a, em = _lru_coeffs(r, lam_ref[...])
        xc_ref[...] = xc
        r_ref[...] = r
        i_ref[...] = ig
        a_ref[...] = a
        b_ref[...] = jnp.sqrt(em) * (ig * xc)

    tile = pl.BlockSpec((tr, bw), lambda n, i: (i, n))
    vec = pl.BlockSpec((1, bw), lambda n, i: (0, n))
    wblk = pl.BlockSpec((None, bw, bw), lambda n, i: (n, 0, 0))
    sds = jax.ShapeDtypeStruct((S, D), F32)
    return hbm_call(
        body, name=name, grid=(nblk, S // tr),
        in_specs=[tile, pl.BlockSpec((tr, bw), lambda n, i: (jnp.maximum(i - 1, 0), n)),
                  pl.BlockSpec((CONV_WIDTH, bw), lambda n, i: (0, n)), vec, wblk, vec, wblk, vec, vec],
        out_specs=[tile] * 5, out_shape=[sds] * 5, compiler_params=_params(("parallel", "parallel")),
    )(proj, proj, conv_w, conv_b, w_rg, b_rg, w_ig, b_ig, lam)


SCAN_COLS = 256
CHUNK = SUBLANES
SCAN_UNROLL = 4


def rg_scan_fwd(proj, a, b, name):
    S, D = a.shape
    cb = min(SCAN_COLS, D)
    goff = D // cb

    def body(a_ref, b_ref, g_ref, hs_ref, y_ref):
        row = lax.broadcasted_iota(jnp.int32, (CHUNK, cb), 0)

        def step(c, carry):
            r0 = pl.multiple_of(c * CHUNK, CHUNK)
            A = a_ref[pl.ds(r0, CHUNK), :]
            B = b_ref[pl.ds(r0, CHUNK), :]
            for d in (1, 2, 4):
                As = jnp.where(row >= d, pltpu.roll(A, d, 0), 1.0)
                Bs = jnp.where(row >= d, pltpu.roll(B, d, 0), 0.0)
                B = A * Bs + B
                A = A * As
            H = B + A * carry
            hs_ref[pl.ds(r0, CHUNK), :] = H
            return jnp.sum(jnp.where(row == CHUNK - 1, H, 0.0), axis=0, keepdims=True)

        lax.fori_loop(0, S // CHUNK, step, jnp.zeros((1, cb), F32), unroll=SCAN_UNROLL)
        y_ref[...] = hs_ref[...] * _gelu(g_ref[...])

    col = pl.BlockSpec((S, cb), lambda j: (0, j))
    sds = jax.ShapeDtypeStruct((S, D), F32)
    return hbm_call(
        body, name=name, grid=(D // cb,), in_specs=[col, col, pl.BlockSpec((S, cb), lambda j: (0, goff + j))],
        out_specs=[col, col], out_shape=[sds, sds], compiler_params=_params(("parallel",), _vmem_limit(5 * S * cb * 4, 4 * S * cb * 4)),
    )(a, b, proj)


def rg_scan_bwd(proj, dy, hs, a, name):
    S, D = a.shape
    cb = min(SCAN_COLS, D)
    goff = D // cb
    nchunks = S // CHUNK

    def body(g_ref, dy_ref, hs_ref, a_ref, dg_ref, gt_ref):
        gate, dy = g_ref[...], dy_ref[...]
        dg_ref[...] = dy * hs_ref[...] * _gelu_grad(gate)
        gt_ref[...] = dy * _gelu(gate)
        row = lax.broadcasted_iota(jnp.int32, (CHUNK, cb), 0)

        def step(k, carry):
            c = nchunks - 1 - k
            r0 = pl.multiple_of(c * CHUNK, CHUNK)
            rn = pl.multiple_of(jnp.minimum(c + 1, nchunks - 1) * CHUNK, CHUNK)
            last = c == nchunks - 1
            nxt = jnp.where(last, 0.0, pltpu.roll(a_ref[pl.ds(rn, CHUNK), :], CHUNK - 1, 0))
            A = jnp.where(row < CHUNK - 1, pltpu.roll(a_ref[pl.ds(r0, CHUNK), :], CHUNK - 1, 0), nxt)
            B = gt_ref[pl.ds(r0, CHUNK), :]
            for d in (1, 2, 4):
                As = jnp.where(row < CHUNK - d, pltpu.roll(A, CHUNK - d, 0), 1.0)
                Bs = jnp.where(row < CHUNK - d, pltpu.roll(B, CHUNK - d, 0), 0.0)
                B = A * Bs + B
                A = A * As
            G = B + A * carry
            gt_ref[pl.ds(r0, CHUNK), :] = G
            return jnp.sum(jnp.where(row == 0, G, 0.0), axis=0, keepdims=True)

        lax.fori_loop(0, nchunks, step, jnp.zeros((1, cb), F32), unroll=SCAN_UNROLL)

    col = pl.BlockSpec((S, cb), lambda j: (0, j))
    sds = jax.ShapeDtypeStruct((S, D), F32)
    return hbm_call(
        body, name=name, grid=(D // cb,), in_specs=[pl.BlockSpec((S, cb), lambda j: (0, goff + j)), col, col, col],
        out_specs=[col, col], out_shape=[sds, sds], compiler_params=_params(("parallel",), _vmem_limit(6 * S * cb * 4, 6 * S * cb * 4)),
    )(proj, dy, hs, a)


def rg_gates_bwd(gt, hs, xc, r, ig, w_rg, w_ig, lam, name):
    S, D = xc.shape
    nblk, bw, _ = w_rg.shape
    tr = min(RG_ROWS, S)

    def body(gt_ref, hs_ref, hp_ref, xc_ref, r_ref, i_ref, wr_ref, wi_ref, lam_ref,
             dxc_ref, dwr_ref, dwi_ref, dbr_ref, dbi_ref, dl_ref):
        step = pl.program_id(1)
        g, hs, xc, r, ig, lam = gt_ref[...], hs_ref[...], xc_ref[...], r_ref[...], i_ref[...], lam_ref[...]
        row = lax.broadcasted_iota(jnp.int32, g.shape, 0)
        hprev = _shift_down(hs, hp_ref[...], 1, row, step == 0)
        sp, _, a, em = _lru_coeffs(r, lam)
        mult = jnp.sqrt(em)
        du = g * mult
        dla = g * hprev * a - (g * (ig * xc)) * (a * a) / mult
        dpr = (dla * (-LRU_C * sp)) * (r * (1.0 - r))
        dpi = (du * xc) * (ig * (1.0 - ig))
        dprm, dpim = dpr.astype(MXU_DTYPE), dpi.astype(MXU_DTYPE)
        nt = (((1,), (1,)), ((), ()))
        dxc_ref[...] = (du * ig + lax.dot_general(dprm, wr_ref[...].astype(MXU_DTYPE), nt, preferred_element_type=F32)
                        + lax.dot_general(dpim, wi_ref[...].astype(MXU_DTYPE), nt, preferred_element_type=F32))

        @pl.when(step == 0)
        def _():
            for ref in (dwr_ref, dwi_ref, dbr_ref, dbi_ref, dl_ref):
                ref[...] = jnp.zeros_like(ref)

        xct = xc.T.astype(MXU_DTYPE)
        dwr_ref[...] += jnp.dot(xct, dprm, preferred_element_type=F32)
        dwi_ref[...] += jnp.dot(xct, dpim, preferred_element_type=F32)
        dbr_ref[...] += jnp.sum(dpr, axis=0, keepdims=True)
        dbi_ref[...] += jnp.sum(dpi, axis=0, keepdims=True)
        dl_ref[...] += jnp.sum(dla * (-LRU_C * r), axis=0, keepdims=True) * (-_sigmoid(-lam))

    tile = pl.BlockSpec((tr, bw), lambda n, i: (i, n))
    vec = pl.BlockSpec((1, bw), lambda n, i: (0, n))
    wblk = pl.BlockSpec((None, bw, bw), lambda n, i: (n, 0, 0))
    return hbm_call(
        body, name=name, grid=(nblk, S // tr),
        in_specs=[tile, tile, pl.BlockSpec((tr, bw), lambda n, i: (jnp.maximum(i - 1, 0), n)), tile, tile, tile, wblk, wblk, vec],
        out_specs=[tile, wblk, wblk, vec, vec, vec],
        out_shape=[jax.ShapeDtypeStruct((S, D), F32), jax.ShapeDtypeStruct((nblk, bw, bw), F32), jax.ShapeDtypeStruct((nblk, bw, bw), F32),
                   jax.ShapeDtypeStruct((1, D), F32), jax.ShapeDtypeStruct((1, D), F32), jax.ShapeDtypeStruct((1, D), F32)],
        compiler_params=_params(("parallel", "arbitrary")),
    )(gt, hs, hs, xc, r, ig, w_rg, w_ig, lam)


def rg_conv_bwd(proj, dxc, conv_w, name):
    S, D = dxc.shape
    bw = min(SCAN_COLS, D)
    tr = min(RG_ROWS, S)
    nsteps = S // tr

    def body(d_ref, dn_ref, xr_ref, xp_ref, cw_ref, dxr_ref, dcw_ref, dcb_ref):
        step = pl.program_id(1)
        d, xr = d_ref[...], xr_ref[...]
        row = lax.broadcasted_iota(jnp.int32, d.shape, 0)
        dxr = d * cw_ref[CONV_WIDTH - 1:CONV_WIDTH, :]
        for k in range(CONV_WIDTH - 1):
            dxr = dxr + _shift_up(d, dn_ref[...], CONV_WIDTH - 1 - k, row, step == nsteps - 1, tr) * cw_ref[k:k + 1, :]
        dxr_ref[...] = dxr

        @pl.when(step == 0)
        def _():
            dcw_ref[...] = jnp.zeros_like(dcw_ref)
            dcb_ref[...] = jnp.zeros_like(dcb_ref)

        for k in range(CONV_WIDTH - 1):
            xs = _shift_down(xr, xp_ref[...], CONV_WIDTH - 1 - k, row, step == 0)
            dcw_ref[k:k + 1, :] += jnp.sum(d * xs, axis=0, keepdims=True)
        dcw_ref[CONV_WIDTH - 1:CONV_WIDTH, :] += jnp.sum(d * xr, axis=0, keepdims=True)
        dcb_ref[...] += jnp.sum(d, axis=0, keepdims=True)

    tile = pl.BlockSpec((tr, bw), lambda n, i: (i, n))
    cwb = pl.BlockSpec((CONV_WIDTH, bw), lambda n, i: (0, n))
    return hbm_call(
        body, name=name, grid=(D // bw, nsteps),
        in_specs=[tile, pl.BlockSpec((tr, bw), lambda n, i: (jnp.minimum(i + 1, nsteps - 1), n)), tile,
                  pl.BlockSpec((tr, bw), lambda n, i: (jnp.maximum(i - 1, 0), n)), cwb],
        out_specs=[tile, cwb, pl.BlockSpec((1, bw), lambda n, i: (0, n))],
        out_shape=[jax.ShapeDtypeStruct((S, D), F32), jax.ShapeDtypeStruct((CONV_WIDTH, D), F32), jax.ShapeDtypeStruct((1, D), F32)],
        compiler_params=_params(("parallel", "arbitrary")),
    )(dxc, dxc, proj, proj, conv_w)


def rope_table(S):
    half = ROT_DIM // 2
    pos = jnp.arange(S, dtype=F32)
    inv = ROPE_THETA ** (-jnp.arange(0, ROT_DIM, 2, dtype=F32) / ROT_DIM)
    ang = pos[:, None] * inv[None, :]
    cos, sin = jnp.cos(ang), jnp.sin(ang)
    zero = jnp.zeros((S, HEAD_DIM - ROT_DIM), F32)
    c = jnp.concatenate([cos, cos, zero + 1.0], axis=1)
    a = jnp.concatenate([-sin, jnp.zeros((S, half), F32), zero], axis=1)
    b = jnp.concatenate([jnp.zeros((S, half), F32), sin, zero], axis=1)
    return jnp.stack([jnp.tile(t, (1, LANES // HEAD_DIM)) for t in (c, a, b)])


def _rope(t, tab):
    half = ROT_DIM // 2
    return t * tab[0] + pltpu.roll(t, LANES - half, 1) * tab[1] + pltpu.roll(t, half, 1) * tab[2]


def _rope_t(d, tab):
    half = ROT_DIM // 2
    return d * tab[0] + pltpu.roll(d * tab[1], half, 1) + pltpu.roll(d * tab[2], LANES - half, 1)


def _dup_head(t, hk, lo):
    sw = pltpu.roll(t, HEAD_DIM, 1)
    return jnp.where(lo, t, sw) if hk == 0 else jnp.where(lo, sw, t)


def _attn_common(n, sink_ref, q_ref, kp_ref, kc_ref, vp_ref, vc_ref, tc_ref, tp_ref, hk, pairs):
    tq = (tc_ref[0], tc_ref[1], tc_ref[2])
    tp = (tp_ref[0], tp_ref[1], tp_ref[2])
    lo = lax.broadcasted_iota(jnp.int32, (WINDOW, LANES), 1) < HEAD_DIM
    lo2 = lax.broadcasted_iota(jnp.int32, (2 * WINDOW, LANES), 1) < HEAD_DIM
    kband = jnp.concatenate([_rope(kp_ref[...], tp), _rope(kc_ref[...], tq)], axis=0)
    vband = jnp.concatenate([vp_ref[...], vc_ref[...]], axis=0)
    kd = _dup_head(kband, hk, lo2).astype(MXU_DTYPE)
    vd = _dup_head(vband, hk, lo2).astype(MXU_DTYPE)
    rows, sks = [], []
    for j in range(pairs):
        col = hk * pairs + j
        qp = _rope(q_ref[:, col * LANES:(col + 1) * LANES], tq)
        rows += [jnp.where(lo, qp, 0.0), jnp.where(lo, 0.0, qp)]
        sks += [jnp.full((WINDOW, 1), sink_ref[2 * col], F32), jnp.full((WINDOW, 1), sink_ref[2 * col + 1], F32)]
    qg = jnp.concatenate(rows, axis=0)
    sk = jnp.concatenate(sks, axis=0)
    G = 2 * pairs * WINDOW
    ri = lax.broadcasted_iota(jnp.int32, (G, 2 * WINDOW), 0) & (WINDOW - 1)
    kj = lax.broadcasted_iota(jnp.int32, (G, 2 * WINDOW), 1) - WINDOW
    valid = (kj <= ri) & (kj > ri - WINDOW) & (kj + n * WINDOW >= 0)
    s = lax.dot_general(qg.astype(MXU_DTYPE), kd, (((1,), (1,)), ((), ())), preferred_element_type=F32) * (HEAD_DIM ** -0.5)
    s = jnp.where(valid, s, NEG_INF)
    m = jnp.maximum(jnp.max(s, axis=1, keepdims=True), sk)
    e = jnp.exp(s - m)
    es = jnp.exp(sk - m)
    inv = 1.0 / (jnp.sum(e, axis=1, keepdims=True) + es)
    return qg, kd, vd, e * inv, es * inv, lo, lo2, tq, tp


def _attn_specs(D, NB):
    kcol = 3 * D // LANES
    q = pl.BlockSpec((WINDOW, D), lambda n: (n, 2))
    kc = pl.BlockSpec((WINDOW, LANES), lambda n: (n, kcol))
    kp = pl.BlockSpec((WINDOW, LANES), lambda n: (jnp.maximum(n - 1, 0), kcol))
    vc = pl.BlockSpec((WINDOW, LANES), lambda n: (n, kcol + 1))
    vp = pl.BlockSpec((WINDOW, LANES), lambda n: (jnp.maximum(n - 1, 0), kcol + 1))
    tc = pl.BlockSpec((3, WINDOW, LANES), lambda n: (0, n, 0))
    tp = pl.BlockSpec((3, WINDOW, LANES), lambda n: (0, jnp.maximum(n - 1, 0), 0))
    sink = pl.BlockSpec(memory_space=pltpu.SMEM)
    return [sink, q, kp, kc, vp, vc, tc, tp]


def attn_fwd(proj, sinks, tab, D, name):
    S = proj.shape[0]
    NB = S // WINDOW
    pairs = D // HEAD_DIM // N_KV_HEADS // 2

    def body(sink_ref, q_ref, kp_ref, kc_ref, vp_ref, vc_ref, tc_ref, tp_ref, o_ref):
        n = pl.program_id(0)
        for hk in range(N_KV_HEADS):
            _, _, vd, p, _, lo, _, _, _ = _attn_common(n, sink_ref, q_ref, kp_ref, kc_ref, vp_ref, vc_ref, tc_ref, tp_ref, hk, pairs)
            o = jnp.dot(p.astype(MXU_DTYPE), vd, preferred_element_type=F32)
            for j in range(pairs):
                col = hk * pairs + j
                oa = o[(2 * j) * WINDOW:(2 * j + 1) * WINDOW]
                ob = o[(2 * j + 1) * WINDOW:(2 * j + 2) * WINDOW]
                o_ref[:, col * LANES:(col + 1) * LANES] = jnp.where(lo, oa, ob)

    return hbm_call(
        body, name=name, grid=(NB,), in_specs=_attn_specs(D, NB),
        out_specs=pl.BlockSpec((WINDOW, D), lambda n: (n, 0)), out_shape=jax.ShapeDtypeStruct((S, D), F32),
        compiler_params=_params(("parallel",)),
    )(sinks, proj, proj, proj, proj, proj, tab, tab)


def attn_bwd(proj, sinks, tab, o, do, D, name):
    S = proj.shape[0]
    NB = S // WINDOW
    pairs = D // HEAD_DIM // N_KV_HEADS // 2

    def body(sink_ref, q_ref, kp_ref, kc_ref, vp_ref, vc_ref, tc_ref, tp_ref, o_ref, do_ref, dq_ref, dk_ref, dv_ref, ds_ref):
        n = pl.program_id(0)

        @pl.when(n == 0)
        def _():
            ds_ref[...] = jnp.zeros_like(ds_ref)

        lane1 = lax.broadcasted_iota(jnp.int32, (1, LANES), 1)
        dsink = jnp.zeros((1, LANES), F32)
        dkt = dvt = None
        for hk in range(N_KV_HEADS):
            qg, kd, vd, p, ps, lo, lo2, tq, tp = _attn_common(n, sink_ref, q_ref, kp_ref, kc_ref, vp_ref, vc_ref, tc_ref, tp_ref, hk, pairs)
            dos, os_ = [], []
            for j in range(pairs):
                col = hk * pairs + j
                dop = do_ref[:, col * LANES:(col + 1) * LANES]
                op = o_ref[:, col * LANES:(col + 1) * LANES]
                dos += [jnp.where(lo, dop, 0.0), jnp.where(lo, 0.0, dop)]
                os_ += [jnp.where(lo, op, 0.0), jnp.where(lo, 0.0, op)]
            dog = jnp.concatenate(dos, axis=0)
            og = jnp.concatenate(os_, axis=0)
            dogm = dog.astype(MXU_DTYPE)
            dp = lax.dot_general(dogm, vd, (((1,), (1,)), ((), ())), preferred_element_type=F32)
            dr = jnp.sum(dog * og, axis=1, keepdims=True)
            ds = p * (dp - dr) * (HEAD_DIM ** -0.5)
            dsm = ds.astype(MXU_DTYPE)
            dqg = jnp.dot(dsm, kd, preferred_element_type=F32)
            dkd = jnp.dot(ds.T.astype(MXU_DTYPE), qg.astype(MXU_DTYPE), preferred_element_type=F32)
            dvd = jnp.dot(p.T.astype(MXU_DTYPE), dogm, preferred_element_type=F32)
            dkf = dkd + pltpu.roll(dkd, HEAD_DIM, 1)
            dvf = dvd + pltpu.roll(dvd, HEAD_DIM, 1)
            if hk == 0:
                dkt, dvt = dkf, dvf
            else:
                dkt, dvt = jnp.where(lo2, dkt, dkf), jnp.where(lo2, dvt, dvf)
            sd = ps * dr
            for j in range(pairs):
                col = hk * pairs + j
                dqa = dqg[(2 * j) * WINDOW:(2 * j + 1) * WINDOW]
                dqb = dqg[(2 * j + 1) * WINDOW:(2 * j + 2) * WINDOW]
                dq_ref[:, col * LANES:(col + 1) * LANES] = _rope_t(jnp.where(lo, dqa, dqb), tq)
                for t in range(2):
                    part = sd[(2 * j + t) * WINDOW:(2 * j + t + 1) * WINDOW]
                    val = jnp.sum(part, axis=0, keepdims=True)
                    dsink = dsink - jnp.where(lane1 == 2 * col + t, val, 0.0)
        dk_ref[...] = jnp.concatenate([_rope_t(dkt[:WINDOW], tp), _rope_t(dkt[WINDOW:], tq)], axis=0)
        dv_ref[...] = dvt
        ds_ref[...] += dsink

    blk = pl.BlockSpec((WINDOW, D), lambda n: (n, 0))
    band = pl.BlockSpec((None, 2 * WINDOW, LANES), lambda n: (n, 0, 0))
    return hbm_call(
        body, name=name, grid=(NB,), in_specs=_attn_specs(D, NB) + [blk, blk],
        out_specs=[blk, band, band, pl.BlockSpec((1, LANES), lambda n: (0, 0))],
        out_shape=[jax.ShapeDtypeStruct((S, D), F32), jax.ShapeDtypeStruct((NB, 2 * WINDOW, LANES), F32),
                   jax.ShapeDtypeStruct((NB, 2 * WINDOW, LANES), F32), jax.ShapeDtypeStruct((1, LANES), F32)],
        compiler_params=_params(("arbitrary",)),
    )(sinks, proj, proj, proj, proj, proj, tab, tab, o, do)


def band_fold(dkb, dvb, name):
    NB = dkb.shape[0]
    k4 = dkb.reshape(NB, 2, WINDOW, LANES)
    v4 = dvb.reshape(NB, 2, WINDOW, LANES)

    def body(kc_ref, kn_ref, vc_ref, vn_ref, dk_ref, dv_ref):
        more = pl.program_id(0) < NB - 1
        dk_ref[...] = kc_ref[...] + jnp.where(more, kn_ref[...], 0.0)
        dv_ref[...] = vc_ref[...] + jnp.where(more, vn_ref[...], 0.0)

    cur = pl.BlockSpec((None, None, WINDOW, LANES), lambda n: (n, 1, 0, 0))
    nxt = pl.BlockSpec((None, None, WINDOW, LANES), lambda n: (jnp.minimum(n + 1, NB - 1), 0, 0, 0))
    out = pl.BlockSpec((WINDOW, LANES), lambda n: (n, 0))
    sds = jax.ShapeDtypeStruct((NB * WINDOW, LANES), F32)
    return hbm_call(body, name=name, grid=(NB,), in_specs=[cur, nxt, cur, nxt], out_specs=[out, out], out_shape=[sds, sds],
                          compiler_params=_params(("parallel",)))(k4, k4, v4, v4)


CROSS_ROWS = 512


def _cross_probs(q, k, scale):
    s = lax.dot_general(q.astype(MXU_DTYPE), k.astype(MXU_DTYPE), (((1,), (1,)), ((), ())), preferred_element_type=F32) * scale
    e = jnp.exp(s - jnp.max(s, axis=1, keepdims=True))
    return e / jnp.sum(e, axis=1, keepdims=True)


def cross_fwd(qc, kv, name):
    S, D = qc.shape
    M = kv.shape[0]
    hd = D // CROSS_HEADS
    tq = min(CROSS_ROWS, S)

    def body(q_ref, kv_ref, o_ref):
        for h in range(CROSS_HEADS):
            p = _cross_probs(q_ref[:, h * hd:(h + 1) * hd], kv_ref[:, h * hd:(h + 1) * hd], hd ** -0.5)
            v = kv_ref[:, D + h * hd:D + (h + 1) * hd].astype(MXU_DTYPE)
            o_ref[:, h * hd:(h + 1) * hd] = jnp.dot(p.astype(MXU_DTYPE), v, preferred_element_type=F32)

    return hbm_call(
        body, name=name, grid=(S // tq,), in_specs=[pl.BlockSpec((tq, D), lambda i: (i, 0)), pl.BlockSpec((M, 2 * D), lambda i: (0, 0))],
        out_specs=pl.BlockSpec((tq, D), lambda i: (i, 0)), out_shape=jax.ShapeDtypeStruct((S, D), F32),
        compiler_params=_params(("parallel",)),
    )(qc, kv)


def cross_bwd(qc, kv, do, name):
    S, D = qc.shape
    M = kv.shape[0]
    hd = D // CROSS_HEADS
    tq = min(CROSS_ROWS, S)

    def body(q_ref, kv_ref, do_ref, dq_ref, dkv_ref):
        @pl.when(pl.program_id(0) == 0)
        def _():
            dkv_ref[...] = jnp.zeros_like(dkv_ref)

        for h in range(CROSS_HEADS):
            q = q_ref[:, h * hd:(h + 1) * hd]
            k = kv_ref[:, h * hd:(h + 1) * hd]
            v = kv_ref[:, D + h * hd:D + (h + 1) * hd].astype(MXU_DTYPE)
            dom = do_ref[:, h * hd:(h + 1) * hd].astype(MXU_DTYPE)
            p = _cross_probs(q, k, hd ** -0.5)
            dp = lax.dot_general(dom, v, (((1,), (1,)), ((), ())), preferred_element_type=F32)
            ds = p * (dp - jnp.sum(p * dp, axis=1, keepdims=True)) * (hd ** -0.5)
            dq_ref[:, h * hd:(h + 1) * hd] = jnp.dot(ds.astype(MXU_DTYPE), k.astype(MXU_DTYPE), preferred_element_type=F32)
            dkv_ref[:, h * hd:(h + 1) * hd] += jnp.dot(ds.T.astype(MXU_DTYPE), q.astype(MXU_DTYPE), preferred_element_type=F32)
            dkv_ref[:, D + h * hd:D + (h + 1) * hd] += jnp.dot(p.T.astype(MXU_DTYPE), dom, preferred_element_type=F32)

    row = pl.BlockSpec((tq, D), lambda i: (i, 0))
    full = pl.BlockSpec((M, 2 * D), lambda i: (0, 0))
    return hbm_call(
        body, name=name, grid=(S // tq,), in_specs=[row, full, row], out_specs=[row, full],
        out_shape=[jax.ShapeDtypeStruct((S, D), F32), jax.ShapeDtypeStruct((M, 2 * D), F32)],
        compiler_params=_params(("arbitrary",)),
    )(qc, kv, do)


def adamw(w, g, m, v, name):
    shape = w.shape
    cols = shape[-1]
    rows = int(np.prod(shape[:-1]))
    w2, g2, m2, v2 = (t.reshape(rows, cols) for t in (w, g, m, v))
    tr = _divisors(rows, SUBLANES, max(SUBLANES, (1 << 20) // (cols * 4) // SUBLANES * SUBLANES))[0]

    def body(w_ref, g_ref, m_ref, v_ref, d_ref, mo_ref, vo_ref, go_ref):
        gg = g_ref[...]
        mn = ADAM_B1 * m_ref[...] + (1.0 - ADAM_B1) * gg
        vn = ADAM_B2 * v_ref[...] + (1.0 - ADAM_B2) * (gg * gg)
        m_hat = mn / (1.0 - ADAM_B1 ** ADAM_STEP)
        v_hat = vn / (1.0 - ADAM_B2 ** ADAM_STEP)
        d_ref[...] = -ADAM_LR * (m_hat / (jnp.sqrt(v_hat) + ADAM_EPS) + ADAM_WD * w_ref[...])
        mo_ref[...] = mn
        vo_ref[...] = vn
        go_ref[...] = gg

    blk = pl.BlockSpec((tr, cols), lambda i: (i, 0))
    sds = jax.ShapeDtypeStruct((rows, cols), F32)
    d, mn, vn, go = hbm_call(body, name=name, grid=(rows // tr,), in_specs=[blk] * 4, out_specs=[blk] * 4, out_shape=[sds] * 4,
                                   compiler_params=_params(("parallel",)))(w2, g2, m2, v2)
    return d.reshape(shape), mn.reshape(shape), vn.reshape(shape), go.reshape(shape)


def sum_devices(parts, name):
    n, rows, cols = parts.shape

    def body(p_ref, o_ref):
        acc = p_ref[0]
        for k in range(1, n):
            acc = acc + p_ref[k]
        o_ref[...] = acc

    return pl.pallas_call(body, name=name, in_specs=[pl.BlockSpec(memory_space=pltpu.VMEM)],
                          out_specs=pl.BlockSpec(memory_space=pltpu.VMEM), out_shape=jax.ShapeDtypeStruct((rows, cols), F32))(parts)


HBM_SPEC = pl.BlockSpec(memory_space=pltpu.HBM)


def _place():
    return lax.axis_index("x"), lax.axis_index("y"), lax.axis_index("c")


def _remote(src, dst, send_sems, recv_sems, k, to):
    return pltpu.make_async_remote_copy(src_ref=src, dst_ref=dst, send_sem=send_sems.at[k], recv_sem=recv_sems.at[k],
                                        device_id=to, device_id_type=MESH_ID)


SEM_SPEC = pl.BlockSpec(memory_space=pltpu.SEMAPHORE)
ANY_SPEC = pl.BlockSpec(memory_space=pl.ANY)
SPLIT_COPY = pltpu.CompilerParams(has_side_effects=pltpu.SideEffectType.DATAFLOW_SIDE_EFFECTING)


def _in_hbm(arrays):
    return [pltpu.with_memory_space_constraint(a, pltpu.HBM) for a in arrays]


def _split_start(copies, sources, lands, after, n_sems, name):
    n = len(sources)

    def body(*refs):
        for cp in copies(refs[:n], refs[n:2 * n], refs[2 * n + 1], refs[2 * n + 2]):
            cp.start()
        refs[-1][...] = jnp.zeros_like(refs[-1])

    through = [pltpu.HBM(a.shape, a.dtype) for a in list(sources) + list(lands)]
    outs = pl.pallas_call(
        body, name=name, in_specs=[HBM_SPEC] * (2 * n) + [ANY_SPEC],
        out_specs=[SEM_SPEC, SEM_SPEC] + [HBM_SPEC] * (2 * n) + [pl.BlockSpec(memory_space=pltpu.VMEM)],
        out_shape=[pltpu.SemaphoreType.DMA((n_sems,)), pltpu.SemaphoreType.DMA((n_sems,))] + through
        + [jax.ShapeDtypeStruct((SUBLANES, LANES), F32)],
        input_output_aliases={i: 2 + i for i in range(2 * n)}, compiler_params=SPLIT_COPY,
    )(*_in_hbm(sources), *_in_hbm(lands), after)
    return outs[0], outs[1], outs[2:2 + n], outs[2 + n:2 + 2 * n], outs[-1]


def _split_wait(copies, send_sems, recv_sems, sources, lands, after, name):
    n = len(sources)

    def body(*refs):
        for cp in copies(refs[:n], refs[n:2 * n], refs[2 * n], refs[2 * n + 1]):
            cp.wait_send()
            cp.wait_recv()

    through = [pltpu.HBM(a.shape, a.dtype) for a in list(sources) + list(lands)]
    outs = pl.pallas_call(
        body, name=name, in_specs=[HBM_SPEC] * (2 * n) + [SEM_SPEC, SEM_SPEC, ANY_SPEC], out_specs=[HBM_SPEC] * (2 * n),
        out_shape=through, input_output_aliases={i: i for i in range(2 * n)}, compiler_params=SPLIT_COPY,
    )(*sources, *lands, send_sems, recv_sems, after)
    return outs[:n], outs[n:]


def _chip_slab(land, slot, rows):
    return land.at[slot, rows] if len(land.shape) == 3 else land.at[rows, slot]


def _gather_copies(w_refs, land_refs, send_sems, recv_sems):
    n = len(w_refs)
    x, y, c = _place()
    chips = [(1 - x, y), (x, 1 - y), (1 - x, 1 - y)]
    cps = []
    for a in range(n):
        hr = w_refs[a].shape[0] // 2
        mine, every = pl.ds(c * hr, hr), pl.ds(0, 2 * hr)
        cps.append(_remote(w_refs[a], _chip_slab(land_refs[a], 2 * x + y, every), send_sems, recv_sems, 3 * n + a, (x, y, 1 - c)))
        for k, chip in enumerate(chips):
            cps.append(_remote(w_refs[a].at[mine], _chip_slab(land_refs[a], 2 * x + y, mine), send_sems, recv_sems, 3 * a + k, (*chip, c)))
    return cps


def gather_start(shards, after, name):
    lands = [lax.empty(s.shape[:-2] + (N_CHIPS,) + s.shape[-2:], s.dtype) for s in shards]
    return _split_start(_gather_copies, shards, lands, after, 4 * len(shards), name)


def gather_wait(state, after, name):
    send_sems, recv_sems, sources, lands, _ = state
    return _split_wait(_gather_copies, send_sems, recv_sems, sources, lands, after, name)[1]


def gather_pass(lands, name):
    n = len(lands)

    def body(*refs):
        out_refs, send_sems, recv_sems = refs[n:2 * n], refs[2 * n], refs[2 * n + 1]
        x, y, c = _place()
        chips = [(1 - x, y), (x, 1 - y), (1 - x, 1 - y)]
        sent = []
        for a in range(n):
            hr = out_refs[a].shape[0 if len(out_refs[a].shape) == 4 else 1] // 2
            for k, (px, py) in enumerate(chips):
                landed = _chip_slab(out_refs[a], 2 * px + py, pl.ds(c * hr, hr))
                sent.append(_remote(landed, landed, send_sems, recv_sems, 3 * a + k, (x, y, 1 - c)))
        for cp in sent:
            cp.start()
        for a in range(n):
            hr = out_refs[a].shape[0 if len(out_refs[a].shape) == 4 else 1] // 2
            for k, (px, py) in enumerate(chips):
                theirs = _chip_slab(out_refs[a], 2 * px + py, pl.ds((1 - c) * hr, hr))
                _remote(theirs, theirs, send_sems, recv_sems, 3 * a + k, (x, y, 1 - c)).wait_recv()
        for cp in sent:
            cp.wait_send()

    return hbm_call(
        body, name=name, in_specs=[HBM_SPEC] * n, out_specs=[HBM_SPEC] * n,
        out_shape=[jax.ShapeDtypeStruct(a.shape, a.dtype) for a in lands], input_output_aliases={a: a for a in range(n)},
        scratch_shapes=[pltpu.SemaphoreType.DMA((3 * n,))] * 2,
    )(*lands)


def _scatter_copies(t_refs, land_refs, send_sems, recv_sems):
    x, y, c = _place()
    chips = [(1 - x, y), (x, 1 - y), (1 - x, 1 - y)]
    return [_remote(t_refs[a].at[:, 2 * px + py], land_refs[a].at[:, k], send_sems, recv_sems, 3 * a + k, (px, py, c))
            for a in range(len(t_refs)) for k, (px, py) in enumerate(chips)]


def scatter_start(parts, after, name):
    lands = [lax.empty((t.shape[0], N_CHIPS - 1) + t.shape[2:], t.dtype) for t in parts]
    return _split_start(_scatter_copies, parts, lands, after, 3 * len(parts), name)


def scatter_wait(state, after, name):
    send_sems, recv_sems, sources, lands, _ = state
    return _split_wait(_scatter_copies, send_sems, recv_sems, sources, lands, after, name)


def swap_sibling(parts, name):
    n = len(parts)

    def body(*refs):
        v_refs, out_refs, send_sems, recv_sems = refs[:n], refs[n:2 * n], refs[2 * n], refs[2 * n + 1]
        x, y, c = _place()
        cps = []
        for a in range(n):
            hr = v_refs[a].shape[2] // 2
            cps.append(_remote(v_refs[a].at[:, :, pl.ds((1 - c) * hr, hr)], out_refs[a], send_sems, recv_sems, a, (x, y, 1 - c)))
        for cp in cps:
            cp.start()
        for cp in cps:
            cp.wait()

    return hbm_call(
        body, name=name, in_specs=[HBM_SPEC] * n, out_specs=[HBM_SPEC] * n,
        out_shape=[jax.ShapeDtypeStruct(v.shape[:2] + (v.shape[2] // 2, v.shape[3]), v.dtype) for v in parts],
        scratch_shapes=[pltpu.SemaphoreType.DMA((n,))] * 2,
    )(*parts)


def join_halves(halves, layer, name):
    n = len(halves)

    def body(*refs):
        out_refs, send_sems, recv_sems = refs[n:2 * n], refs[2 * n], refs[2 * n + 1]
        x, y, c = _place()
        cps = []
        for a in range(n):
            hr = out_refs[a].shape[1] // 2
            mine = out_refs[a].at[layer, pl.ds(c * hr, hr)]
            cps.append(_remote(mine, mine, send_sems, recv_sems, a, (x, y, 1 - c)))
        for cp in cps:
            cp.start()
        for a in range(n):
            hr = out_refs[a].shape[1] // 2
            theirs = out_refs[a].at[layer, pl.ds((1 - c) * hr, hr)]
            _remote(theirs, theirs, send_sems, recv_sems, a, (x, y, 1 - c)).wait_recv()
        for cp in cps:
            cp.wait_send()

    return hbm_call(
        body, name=name, in_specs=[HBM_SPEC] * n, out_specs=[HBM_SPEC] * n,
        out_shape=[jax.ShapeDtypeStruct(f.shape, f.dtype) for f in halves], input_output_aliases={a: a for a in range(n)},
        scratch_shapes=[pltpu.SemaphoreType.DMA((n,))] * 2,
    )(*halves)


def gather_devices(v, name):
    def body(v_ref, out_ref, send_sems, recv_sems, local_sem):
        x, y, c = _place()
        me = 4 * x + 2 * y + c
        own = pltpu.make_async_copy(v_ref, out_ref.at[me], local_sem)
        own.start()
        peers = [((x + dx) % 2, (y + dy) % 2, (c + dc) % 2) for dx in (0, 1) for dy in (0, 1) for dc in (0, 1)][1:]
        sent = []
        for k, peer in enumerate(peers):
            cp = pltpu.make_async_remote_copy(src_ref=v_ref, dst_ref=out_ref.at[me], send_sem=send_sems.at[k], recv_sem=recv_sems.at[k],
                                              device_id=peer, device_id_type=MESH_ID)
            cp.start()
            sent.append(cp)
        for k, (px, py, pc) in enumerate(peers):
            slot = out_ref.at[4 * px + 2 * py + pc]
            pltpu.make_async_remote_copy(src_ref=slot, dst_ref=slot, send_sem=send_sems.at[k], recv_sem=recv_sems.at[k],
                                         device_id=(px, py, pc), device_id_type=MESH_ID).wait_recv()
        for cp in sent:
            cp.wait_send()
        own.wait()

    vm = pl.BlockSpec(memory_space=pltpu.VMEM)
    return pl.pallas_call(body, name=name, in_specs=[vm], out_specs=vm, out_shape=jax.ShapeDtypeStruct((N_DEV,) + v.shape, v.dtype),
                          scratch_shapes=[pltpu.SemaphoreType.DMA((N_DEV - 1,)), pltpu.SemaphoreType.DMA((N_DEV - 1,)),
                                          pltpu.SemaphoreType.DMA])(v)


ADD_ROWS = 512


def add_pair(place, a, b, name):
    L, n, hr, cols = b.shape
    tr = _divisors(hr, 2 * SUBLANES, ADD_ROWS)[0]
    nb = hr // tr

    def body(p_ref, a_ref, b_ref, o_ref):
        del p_ref
        o_ref[...] = (a_ref[...].astype(F32) + b_ref[...].astype(F32)).astype(o_ref.dtype)

    blk = pl.BlockSpec((None, None, tr, cols), lambda l, d, i, p: (l, d, i, 0))
    grid_spec = pltpu.PrefetchScalarGridSpec(
        num_scalar_prefetch=1, grid=(L, n, nb),
        in_specs=[pl.BlockSpec((None, None, tr, cols), lambda l, d, i, p: (l, d, p[0] * nb + i, 0)), blk], out_specs=blk)
    return hbm_call(body, name=name, grid_spec=grid_spec, out_shape=jax.ShapeDtypeStruct(b.shape, b.dtype),
                          compiler_params=_params(("parallel", "parallel", "parallel")))(place, a, b)


def add_chips(place, own, others, layer, stacked, name):
    _, n, hr, cols = others.shape
    tr = _divisors(hr, 2 * SUBLANES, ADD_ROWS)[0]
    nb = hr // tr
    create = isinstance(stacked, tuple)

    def body(p_ref, own_ref, *refs):
        del p_ref
        acc = own_ref[...].astype(F32)
        for k in range(n):
            acc = acc + refs[k][...].astype(F32)
        refs[-1][...] = acc

    ins = [pl.BlockSpec((None, None, tr, cols), lambda i, p: (0, p[1], i, 0))]
    ins += [pl.BlockSpec((None, None, tr, cols), functools.partial(lambda k, i, p: (0, k, i, 0), k)) for k in range(n)]
    grid_spec = pltpu.PrefetchScalarGridSpec(num_scalar_prefetch=1, grid=(nb,), in_specs=ins + ([] if create else [ANY_SPEC]),
                                             out_specs=pl.BlockSpec((None, tr, cols), lambda i, p: (layer, p[0] * nb + i, 0)))
    shape = stacked if create else stacked.shape
    return hbm_call(body, name=name, grid_spec=grid_spec, out_shape=jax.ShapeDtypeStruct(shape, F32),
                          input_output_aliases={} if create else {n + 2: 0},
                          compiler_params=_params(("parallel",)))(place, own, *([others] * n), *([] if create else [stacked]))


def _alpha(depth):
    return (2 * depth) ** 0.25


def _wmm(a, weight, mode, name, deps=(), **more):
    arr, how = weight
    return mm(a, arr, mode, name, deps=deps, **how, **more)


def layer_fwd(h, mem, w, tab, alpha, deps=(), late=None):
    D = h.shape[1]
    proj = _wmm(h, w["w_in"], "nn", "mm_proj", deps)
    xc, r, ig, a, b = rg_gates_fwd(proj, w["conv_w"], w["conv_b"], w["w_rg"], w["b_rg"], w["w_ig"], w["b_ig"], w["lru_lambda"], "rg_gates_fwd")
    hs, y_rnn = rg_scan_fwd(proj, a, b, "rg_scan_fwd")
    y_attn = attn_fwd(proj, w["sinks"], tab, D, "attn_fwd")
    deps = ()
    if late is not None:
        rest, deps = late(y_attn)
        w = {**w, **rest}
    pr = _wmm(y_rnn, w["w_br_rnn"], "nn", "mm_br_rnn", deps)
    pa = _wmm(y_attn, w["w_br_attn"], "nn", "mm_br_attn")
    merged = merge_fwd(proj, pr, pa, "merge_fwd")
    mix = _wmm(merged, w["w_out"], "nn", "mm_out")
    h1, xh1, rs1 = ln_fwd(h, mix, w["ln1_g"], w["ln1_b"], alpha, "ln1_fwd")
    qc = _wmm(h1, w["cq_w"], "nn", "mm_cq")
    kv = _wmm(mem, w["ckv_w"], "nn", "mm_ckv")
    o = cross_fwd(qc, kv, "cross_fwd")
    co = _wmm(o, w["co_w"], "nn", "mm_co")
    h2, xh2, rs2 = ln_fwd(h1, co, w["ln2_g"], w["ln2_b"], alpha, "ln2_fwd")
    gu = _wmm(h2, w["ffn_wi"], "nn", "mm_ffn_wi", out_blocks=2)
    act = swiglu_fwd(gu, "swiglu_fwd")
    f = _wmm(act, w["ffn_wo"], "nn", "mm_ffn_wo")
    h3, xh3, rs3 = ln_fwd(h2, f, w["ln3_g"], w["ln3_b"], alpha, "ln3_fwd")
    saved = dict(h=h, proj=proj, xc=xc, r=r, ig=ig, a=a, hs=hs, y_rnn=y_rnn, y_attn=y_attn, pr=pr, pa=pa, xh1=xh1, rs1=rs1, h1=h1,
                 qc=qc, kv=kv, o=o, xh2=xh2, rs2=rs2, h2=h2, gu=gu, xh3=xh3, rs3=rs3)
    return h3, saved, w


def layer_bwd(dh, mem, w, s, tab, alpha, deps=(), halfway=None):
    D = dh.shape[1]
    g = {}
    wg = dict(out_dtype=MXU_DTYPE)
    dz3, g["ln3_g"], g["ln3_b"] = ln_bwd(dh, None, s["xh3"], s["rs3"], w["ln3_g"], 1.0, "ln3_bwd")
    act = swiglu_fwd(s["gu"], "swiglu_refwd")
    g["ffn_wo"] = mm(act, dz3, "tn", "mm_d_ffn_wo", deps=deps, **wg)
    dact = _wmm(dz3, w["ffn_wo"], "nt", "mm_dact")
    dgu = swiglu_bwd(s["gu"], dact, "swiglu_bwd")
    g["ffn_wi"] = mm(s["h2"], dgu, "tn", "mm_d_ffn_wi", b_blocks=2, out_blocks=N_CHIPS, **wg)
    dh2 = _wmm(dgu, w["ffn_wi"], "nt", "mm_dh2", a_blocks=2)
    dz2, g["ln2_g"], g["ln2_b"] = ln_bwd(dz3, dh2, s["xh2"], s["rs2"], w["ln2_g"], alpha, "ln2_bwd")
    g["co_w"] = mm(s["o"], dz2, "tn", "mm_d_co", **wg)
    do = _wmm(dz2, w["co_w"], "nt", "mm_do")
    dqc, dkv = cross_bwd(s["qc"], s["kv"], do, "cross_bwd")
    g["cq_w"] = mm(s["h1"], dqc, "tn", "mm_d_cq", **wg)
    g["ckv_w"] = mm(mem, dkv, "tn", "mm_d_ckv", out_blocks=N_CHIPS, **wg)
    dh1 = _wmm(dqc, w["cq_w"], "nt", "mm_dh1")
    deps = halfway(g, dh1) if halfway is not None else ()
    dz1, g["ln1_g"], g["ln1_b"] = ln_bwd(dz2, dh1, s["xh1"], s["rs1"], w["ln1_g"], alpha, "ln1_bwd")
    merged = merge_fwd(s["proj"], s["pr"], s["pa"], "merge_refwd")
    g["w_out"] = mm(merged, dz1, "tn", "mm_d_out", deps=deps, **wg)
    dm = _wmm(dz1, w["w_out"], "nt", "mm_dmerged")
    dpr, dpa, dg_rnn, dg_attn = merge_bwd(s["proj"], s["pr"], s["pa"], dm, "merge_bwd")
    g["w_br_rnn"] = mm(s["y_rnn"], dpr, "tn", "mm_d_br_rnn", **wg)
    g["w_br_attn"] = mm(s["y_attn"], dpa, "tn", "mm_d_br_attn", **wg)
    dy_rnn = _wmm(dpr, w["w_br_rnn"], "nt", "mm_dy_rnn")
    dy_attn = _wmm(dpa, w["w_br_attn"], "nt", "mm_dy_attn")
    dq, dkb, dvb, dsink = attn_bwd(s["proj"], w["sinks"], tab, s["y_attn"], dy_attn, D, "attn_bwd")
    dk, dv = band_fold(dkb, dvb, "band_fold")
    g["sinks"] = dsink[:, :w["sinks"].shape[0]]
    dgr, gt = rg_scan_bwd(s["proj"], dy_rnn, s["hs"], s["a"], "rg_scan_bwd")
    dxc, g["w_rg"], g["w_ig"], g["b_rg"], g["b_ig"], g["lru_lambda"] = rg_gates_bwd(
        gt, s["hs"], s["xc"], s["r"], s["ig"], w["w_rg"], w["w_ig"], w["lru_lambda"], "rg_gates_bwd")
    dxr, g["conv_w"], g["conv_b"] = rg_conv_bwd(s["proj"], dxc, w["conv_w"], "rg_conv_bwd")
    dproj = jnp.concatenate([dxr, dgr, dq, dk, dv, dg_rnn, dg_attn], axis=1)
    g["w_in"] = mm(s["h"], dproj, "tn", "mm_d_in")
    dhm = _wmm(dproj, w["w_in"], "nt", "mm_dh")
    return axpby(dz1, dhm, alpha, "layer_dx"), g


def local_step(x, mem, target, depth, weights_of, grads_halfway, grads_done):
    alpha = _alpha(depth)
    tab = rope_table(x.shape[0])
    h, saved, layers = x, [], []
    for l in range(depth):
        wl, deps, late = weights_of(l, h)
        h, s, wl = layer_fwd(h, mem, wl, tab, alpha, deps, late)
        layers.append(wl)
        saved.append(s)
    dh, loss = loss_head(h, target, "loss_head")
    deps = ()
    for l in reversed(range(depth)):
        dh, g = layer_bwd(dh, mem, layers[l], saved[l], tab, alpha, deps, grads_halfway(l))
        deps = grads_done(l, g, dh)
    return loss, dh


def _pad_rows(flat):
    n = flat.shape[0]
    rows = -(-n // (LANES * SUBLANES)) * SUBLANES
    return jnp.pad(flat, (0, rows * LANES - n)).reshape(rows, LANES)


def kernel(x, mem, w_in, conv_w, conv_b, w_rg, b_rg, w_ig, b_ig, lru_lambda, w_br_rnn, w_br_attn, sinks, w_out, ln1_g, ln1_b, cq_w, ckv_w, co_w, ln2_g, ln2_b, ffn_wi, ffn_wo, ln3_g, ln3_b, loss_target, m_w_in, m_conv_w, m_conv_b, m_w_rg, m_b_rg, m_w_ig, m_b_ig, m_lru_lambda, m_w_br_rnn, m_w_br_attn, m_sinks, m_w_out, m_ln1_g, m_ln1_b, m_cq_w, m_ckv_w, m_co_w, m_ln2_g, m_ln2_b, m_ffn_wi, m_ffn_wo, m_ln3_g, m_ln3_b, v_w_in, v_conv_w, v_conv_b, v_w_rg, v_b_rg, v_w_ig, v_b_ig, v_lru_lambda, v_w_br_rnn, v_w_br_attn, v_sinks, v_w_out, v_ln1_g, v_ln1_b, v_cq_w, v_ckv_w, v_co_w, v_ln2_g, v_ln2_b, v_ffn_wi, v_ffn_wo, v_ln3_g, v_ln3_b):
    args = dict(locals())
    w = {n: args[n] for n in WEIGHTS}
    m = {n: args["m_" + n] for n in WEIGHTS}
    v = {n: args["v_" + n] for n in WEIGHTS}
    cx, cy, cc = _place()
    chip = 2 * cx + cy
    L = w_in.shape[0]

    place = jnp.stack([cc, chip]).astype(jnp.int32)
    cw_rows = _pad_rows(conv_w.reshape(-1))
    cw_all = gather_devices(cw_rows, "gather_conv_w")[0::2]
    cw_parts = cw_all.reshape(N_CHIPS, -1)[:, :conv_w.size].reshape((N_CHIPS,) + conv_w.shape)
    conv_full = jnp.concatenate([cw_parts[k] for k in range(N_CHIPS)], axis=2)

    shards = [{n: w[n][l].astype(MXU_DTYPE) for n in BIG} for l in range(L)]
    late_names = tuple(n for n in BIG if n not in GATHER_FIRST)
    gathering = {(0, GATHER_FIRST): gather_start([shards[0][n] for n in GATHER_FIRST], cw_rows, "gather_start_0a")}
    gathering[0, late_names] = gather_start([shards[0][n] for n in late_names], gathering[0, GATHER_FIRST][4], "gather_start_0b")

    def gathered(l, names, after, tag):
        lands = gather_pass(gather_wait(gathering.pop((l, names)), after, f"gather_wait_{tag}"), f"gather_pass_{tag}")
        wl = {}
        for n, gw in zip(names, lands):
            rows_joined = gw.reshape(gw.shape[:-3] + (-1, gw.shape[-1]))
            if n == "w_in":
                wl[n] = (jnp.concatenate([gw[k] for k in range(N_CHIPS)], axis=1), {})
            elif n in COL_BLOCKED:
                wl[n] = (gw, dict(b_blocks=N_CHIPS))
            elif n in GATE_WEIGHTS:
                wl[n] = rows_joined
            else:
                wl[n] = (rows_joined, {})
        return wl, lands

    def start_next(l, after):
        if l + 1 == L:
            return ()
        gathering[l + 1, BIG] = gather_start([shards[l + 1][n] for n in BIG], after, f"gather_start_{l + 1}")
        return (gathering[l + 1, BIG][4],)

    def weights_of(l, h):
        deps, late = (), None
        if l == 0:
            wl, _ = gathered(0, GATHER_FIRST, h, "0a")

            def late(after):
                rest, lands = gathered(0, late_names, after, "0b")
                return rest, start_next(0, lands[0])
        else:
            wl, lands = gathered(l, BIG, h, str(l))
            deps = start_next(l, lands[0])
        for n in SMALL:
            wl[n] = conv_full[l] if n == "conv_w" else w[n][l] if n == "sinks" else w[n][l][None, :]
        return wl, deps, late

    def for_chips(n, g):
        if n in COL_BLOCKED:
            return g
        if n in GATE_WEIGHTS:
            nb, bw, _ = g.shape
            g = g.reshape(nb, N_CHIPS, bw // N_CHIPS, bw).transpose(1, 0, 2, 3).reshape(N_CHIPS, nb * bw // N_CHIPS, bw)
        elif SHARD_AXIS[n] == 0:
            g = g.reshape(N_CHIPS, g.shape[0] // N_CHIPS, g.shape[1])
        else:
            g = jnp.stack(jnp.split(g, N_CHIPS, axis=1))
        return g.astype(MXU_DTYPE)

    reduced, scattering, small_grads = {}, {}, [None] * L
    late_grads = tuple(n for n in BIG if n not in SCATTER_FIRST)

    def start_scatter(l, names, g, after, tag):
        partial_sums = [for_chips(n, g[n])[None] for n in names]
        from_sibling = swap_sibling(partial_sums, f"grad_to_sibling_{tag}")
        chip_sums = [add_pair(place, a, b, f"grad_add_pair_{n}_{l}") for n, a, b in zip(names, partial_sums, from_sibling)]
        scattering[l, names] = scatter_start(chip_sums, after, f"grad_scatter_start_{tag}")
        return (scattering[l, names][4],)

    def finish_layer(l, after):
        for names in [k[1] for k in list(scattering) if k[0] == l]:
            tag = str(l) if names == BIG else f"{l}{'a' if names == SCATTER_FIRST else 'b'}"
            chip_sums, from_chips = scatter_wait(scattering.pop((l, names)), after, f"grad_scatter_wait_{tag}")
            for n, own, others in zip(names, chip_sums, from_chips):
                target = reduced.get(n, (L, 2 * own.shape[2], own.shape[3]))
                reduced[n] = add_chips(place, own, others, l, target, f"grad_add_chips_{n}_{l}")
        reduced.update(zip(BIG, join_halves([reduced[n] for n in BIG], l, f"grad_join_{l}")))

    def grads_halfway(l):
        if l > 0:
            return None

        def halfway(g, after):
            if L > 1:
                finish_layer(1, after)
            return start_scatter(0, SCATTER_FIRST, g, after, "0a")

        return halfway

    def grads_done(l, g, dh):
        small_grads[l] = {n: g[n] for n in SMALL}
        if l == 0:
            return start_scatter(0, late_grads, g, dh, "0b")
        if l + 1 < L:
            finish_layer(l + 1, dh)
        return start_scatter(l, BIG, g, dh, str(l))

    loss11, dx = local_step(x[0], mem[0], loss_target[0], L, weights_of, grads_halfway, grads_done)
    finish_layer(0, dx)
    loss = lax.psum(loss11[0, 0], ("x", "y", "c"))
    gshard = {n: reduced[n].reshape(w[n].shape) for n in BIG}

    small_full = {n: jnp.stack([gl[n] for gl in small_grads]).reshape(w[n].shape[:1] + ((CONV_WIDTH, -1) if n == "conv_w" else (-1,)))
                  for n in SMALL}
    small_flat = jnp.concatenate([small_full[n].reshape(-1) for n in SMALL])
    small_sum = sum_devices(gather_devices(_pad_rows(small_flat), "gather_small_grads"), "sum_small_grads").reshape(-1)
    off = 0
    for n in SMALL:
        gfull = small_sum[off:off + small_full[n].size].reshape(small_full[n].shape)
        off += small_full[n].size
        if n == "conv_w":
            width = conv_w.shape[2]
            gfull = lax.dynamic_slice_in_dim(gfull, chip * width, width, axis=2)
        gshard[n] = gfull

    delta, new_m, new_v, grad = {}, {}, {}, {}
    for n in WEIGHTS:
        delta[n], new_m[n], new_v[n], grad[n] = adamw(w[n], gshard[n], m[n], v[n], "adamw_" + n)
    return (loss, dx[None], *[grad[n] for n in WEIGHTS], *[delta[n] for n in WEIGHTS], *[new_m[n] for n in WEIGHTS],
            *[new_v[n] for n in WEIGHTS])
```

```python
import functools
import math

import jax
import jax.numpy as jnp
import numpy as np
from jax import lax
from jax.experimental import pallas as pl
from jax.experimental.pallas import tpu as pltpu

F32 = jnp.float32
BF16 = jnp.bfloat16
MXU_DTYPE = BF16

HEAD_DIM = 64
N_KV_HEADS = 2
WINDOW = 128
ROT_DIM = HEAD_DIM // 4
ROPE_THETA = 500000.0
CROSS_HEADS = 4
RNN_BLOCKS = 4
CONV_WIDTH = 4
LRU_C = 8.0
LN_EPS = 1e-5
NEG_INF = -1e30
ADAM_LR = 0.001
ADAM_B1 = 0.9
ADAM_B2 = 0.999
ADAM_EPS = 1e-08
ADAM_WD = 0.01
ADAM_STEP = 10

VMEM_BYTES_V7X = 64 * 1024 * 1024
VMEM_BLOCK_BUDGET = 36 * 1024 * 1024
LANES = 128
SUBLANES = 8

MESH_ID = pl.DeviceIdType.MESH
N_CHIPS = 4
N_DEV = 8

BIG = ("w_in", "w_rg", "w_ig", "w_br_rnn", "w_br_attn", "w_out", "cq_w", "ckv_w", "co_w", "ffn_wi", "ffn_wo")
SHARD_AXIS = {"w_in": 1, "w_rg": 1, "w_ig": 1, "w_br_rnn": 0, "w_br_attn": 0, "w_out": 0, "cq_w": 0, "ckv_w": 1,
              "co_w": 0, "ffn_wi": 1, "ffn_wo": 0}
SMALL = ("conv_w", "conv_b", "b_rg", "b_ig", "lru_lambda", "sinks", "ln1_g", "ln1_b", "ln2_g", "ln2_b", "ln3_g", "ln3_b")
WEIGHTS = ("w_in", "conv_w", "conv_b", "w_rg", "b_rg", "w_ig", "b_ig", "lru_lambda", "w_br_rnn", "w_br_attn", "sinks",
           "w_out", "ln1_g", "ln1_b", "cq_w", "ckv_w", "co_w", "ln2_g", "ln2_b", "ffn_wi", "ffn_wo", "ln3_g", "ln3_b")
GATE_WEIGHTS = ("w_rg", "w_ig")
COL_BLOCKED = ("ckv_w", "ffn_wi")
GATHER_FIRST = ("w_in", "w_rg", "w_ig")
SCATTER_FIRST = ("ffn_wo", "ffn_wi", "co_w", "cq_w", "ckv_w")


def _params(dims=None, vmem=None):
    return pltpu.CompilerParams(dimension_semantics=dims, vmem_limit_bytes=vmem)


def _vmem_limit(block_bytes, temp_bytes=0):
    want = int(2 * block_bytes + temp_bytes) + (6 << 20)
    return max(32 << 20, min(want, VMEM_BYTES_V7X - (6 << 20)))


def _divisors(n, align, cap):
    out = [d for d in range(align, min(n, cap) + 1, align) if n % d == 0]
    if n <= cap and n not in out:
        out.append(n)
    return sorted(out, reverse=True) or [n]


PIN_MIN_ELEMENTS = 1 << 18


def hbm_call(body, **kw):
    def in_hbm(s):
        return pltpu.HBM(s.shape, s.dtype) if math.prod(s.shape) >= PIN_MIN_ELEMENTS else s

    shapes = kw.pop("out_shape")
    shapes = [in_hbm(s) for s in shapes] if isinstance(shapes, (list, tuple)) else in_hbm(shapes)
    call = pl.pallas_call(body, out_shape=shapes, **kw)

    def run(*args):
        return call(*[pltpu.with_memory_space_constraint(a, pltpu.HBM) if a.size >= PIN_MIN_ELEMENTS else a for a in args])

    return run


def _sigmoid(x):
    return 1.0 / (1.0 + jnp.exp(-x))


def _gelu_parts(x):
    c = math.sqrt(2.0 / math.pi)
    u = c * (x + 0.044715 * x * x * x)
    t = jnp.tanh(u)
    return t, c * (1.0 + 3 * 0.044715 * x * x)


def _gelu(x):
    t, _ = _gelu_parts(x)
    return 0.5 * x * (1.0 + t)


def _gelu_grad(x):
    t, du = _gelu_parts(x)
    return 0.5 * (1.0 + t) + 0.5 * x * (1.0 - t * t) * du


def _neg_expm1(x):
    series = x * (1.0 + x * (0.5 + x * (1.0 / 6 + x * (1.0 / 24 + x * (1.0 / 120)))))
    return -jnp.where(x > -0.1, series, jnp.exp(x) - 1.0)


def _softplus_neg(lam):
    x = -lam
    return jnp.maximum(x, 0.0) + jnp.log1p(jnp.exp(-jnp.abs(x)))


STEP_US = 0.35
HBM_BYTES_PER_US = 2.5e6
MXU_FLOPS_PER_US = 7e8


def mm(a, b, mode, name, *, b_index=(), a_blocks=0, b_blocks=0, out_blocks=0, out_dtype=F32, deps=()):
    nlead = len(b_index) + (1 if b_blocks else 0)
    bk, bn = b.shape[nlead:]
    M, K = (a.shape[-1], a.shape[-2]) if mode == "tn" else (a.shape[-2], a.shape[-1] * max(a_blocks, 1))
    N = bk if mode == "nt" else bn * max(b_blocks, 1) if mode == "nn" or mode == "tn" else bn
    asz, bsz, osz = a.dtype.itemsize, b.dtype.itemsize, jnp.dtype(out_dtype).itemsize
    n_unit = math.gcd(N // max(out_blocks, 1), N // max(b_blocks, 1) if mode != "nt" else N)
    k_unit = math.gcd(K // max(a_blocks, 1), K // max(b_blocks, 1) if mode == "nt" else K)
    tms = _divisors(M, LANES if mode == "tn" else SUBLANES, 2048)
    tns = _divisors(n_unit, LANES, 2048)
    tks = _divisors(k_unit, LANES, k_unit)
    best = None
    for tm in tms:
        for tn in tns:
            for tk in tks:
                nk = K // tk
                scratch = tm * tn * 4 if (nk > 1 and osz != 4) else 0
                blocks = tm * tk * asz + tn * tk * bsz + tm * tn * osz
                temps = tm * tk * (2 + (4 if mode == "tn" else 0)) + tn * tk * 2 + tm * tn * 4 + scratch
                if 2 * blocks + temps > VMEM_BLOCK_BUDGET + (8 << 20):
                    continue
                ni, nj = M // tm, N // tn
                traffic = M * K * asz * (nj if nk > 1 else 1) + N * K * bsz * (1 if nj * nk == 1 else ni) + M * N * osz
                busy = max(traffic / HBM_BYTES_PER_US, 2.0 * M * N * K / MXU_FLOPS_PER_US)
                cost = ni * nj * nk * STEP_US + busy + blocks / HBM_BYTES_PER_US
                if best is None or cost < best[0]:
                    best = (cost, tm, tn, tk, blocks, temps)
    _, tm, tn, tk, blocks, temps = best
    nk = K // tk
    use_scratch = nk > 1 and osz != 4

    def split(index, total, blocks, tile):
        per = total // blocks // tile
        return index // per, index % per

    def body(a_ref, b_ref, *rest):
        o_ref, acc = rest[len(deps)], rest[len(deps) + 1:]
        av = a_ref[...].astype(MXU_DTYPE)
        bv = b_ref[...].astype(MXU_DTYPE)
        dn = {"nn": (((1,), (0,)), ((), ())), "nt": (((1,), (1,)), ((), ())), "tn": (((0,), (0,)), ((), ()))}[mode]
        r = lax.dot_general(av, bv, dn, preferred_element_type=F32)
        if nk == 1:
            o_ref[...] = r.astype(o_ref.dtype)
        else:
            acc_ref = acc[0] if use_scratch else o_ref

            @pl.when(pl.program_id(2) == 0)
            def _():
                acc_ref[...] = r

            @pl.when(pl.program_id(2) > 0)
            def _():
                acc_ref[...] += r

            if use_scratch:
                @pl.when(pl.program_id(2) == nk - 1)
                def _():
                    o_ref[...] = acc_ref[...].astype(o_ref.dtype)

    if mode == "tn":
        a_spec = pl.BlockSpec((tk, tm), lambda i, j, k: (k, i))
    elif a_blocks:
        a_spec = pl.BlockSpec((None, tm, tk), lambda i, j, k: (split(k, K, a_blocks, tk)[0], i, split(k, K, a_blocks, tk)[1]))
    else:
        a_spec = pl.BlockSpec((tm, tk), lambda i, j, k: (i, k))
    lead = (None,) * nlead
    if mode == "nt":
        bmap = ((lambda i, j, k: b_index + (split(k, K, b_blocks, tk)[0], j, split(k, K, b_blocks, tk)[1])) if b_blocks
                else (lambda i, j, k: b_index + (j, k)))
        b_spec = pl.BlockSpec(lead + (tn, tk), bmap)
    else:
        bmap = ((lambda i, j, k: b_index + (split(j, N, b_blocks, tn)[0], k, split(j, N, b_blocks, tn)[1])) if b_blocks
                else (lambda i, j, k: b_index + (k, j)))
        b_spec = pl.BlockSpec(lead + (tk, tn), bmap)
    if out_blocks:
        o_spec = pl.BlockSpec((None, tm, tn), lambda i, j, k: (split(j, N, out_blocks, tn)[0], i, split(j, N, out_blocks, tn)[1]))
        o_shape = jax.ShapeDtypeStruct((out_blocks, M, N // out_blocks), out_dtype)
    else:
        o_spec = pl.BlockSpec((tm, tn), lambda i, j, k: (i, j))
        o_shape = jax.ShapeDtypeStruct((M, N), out_dtype)
    return hbm_call(
        body, name=name, grid=(M // tm, N // tn, nk), in_specs=[a_spec, b_spec] + [pl.BlockSpec(memory_space=pl.ANY)] * len(deps),
        out_specs=o_spec, out_shape=o_shape, scratch_shapes=[pltpu.VMEM((tm, tn), F32)] if use_scratch else [],
        compiler_params=_params(("parallel", "parallel", "arbitrary"), _vmem_limit(blocks, temps)),
    )(a, b, *deps)


ROW_TILE = 512
GATE_ROWS = 1024


def ln_fwd(h, f, g, b, alpha, name):
    S, D = h.shape
    tr = min(ROW_TILE, S)

    def body(h_ref, f_ref, g_ref, b_ref, y_ref, xh_ref, rs_ref):
        z = alpha * h_ref[...] + f_ref[...]
        mu = jnp.mean(z, axis=-1, keepdims=True)
        zc = z - mu
        var = jnp.mean(zc * zc, axis=-1, keepdims=True)
        rs = lax.rsqrt(var + LN_EPS)
        xh = zc * rs
        y_ref[...] = xh * g_ref[...] + b_ref[...]
        xh_ref[...] = xh
        rs_ref[...] = rs

    row = pl.BlockSpec((tr, D), lambda i: (i, 0))
    vec = pl.BlockSpec((1, D), lambda i: (0, 0))
    return hbm_call(
        body, name=name, grid=(S // tr,), in_specs=[row, row, vec, vec],
        out_specs=[row, row, pl.BlockSpec((tr, 1), lambda i: (i, 0))],
        out_shape=[jax.ShapeDtypeStruct((S, D), F32), jax.ShapeDtypeStruct((S, D), F32), jax.ShapeDtypeStruct((S, 1), F32)],
        compiler_params=_params(("parallel",), 48 << 20),
    )(h, f, g, b)


def ln_bwd(dy_a, dy_b, xh, rs, g, c1, name):
    S, D = xh.shape
    tr = min(ROW_TILE, S)
    two = dy_b is not None

    def body(*refs):
        if two:
            a_ref, b_ref, xh_ref, rs_ref, g_ref, dz_ref, dg_ref, db_ref = refs
            dy = c1 * a_ref[...] + b_ref[...]
        else:
            a_ref, xh_ref, rs_ref, g_ref, dz_ref, dg_ref, db_ref = refs
            dy = a_ref[...]
        x = xh_ref[...]
        dyg = dy * g_ref[...]
        m1 = jnp.mean(dyg, axis=-1, keepdims=True)
        m2 = jnp.mean(dyg * x, axis=-1, keepdims=True)
        dz_ref[...] = rs_ref[...] * (dyg - m1 - x * m2)

        @pl.when(pl.program_id(0) == 0)
        def _():
            dg_ref[...] = jnp.zeros_like(dg_ref)
            db_ref[...] = jnp.zeros_like(db_ref)

        dg_ref[...] += jnp.sum(dy * x, axis=0, keepdims=True)
        db_ref[...] += jnp.sum(dy, axis=0, keepdims=True)

    row = pl.BlockSpec((tr, D), lambda i: (i, 0))
    vec = pl.BlockSpec((1, D), lambda i: (0, 0))
    ins = [row, row] if two else [row]
    args = (dy_a, dy_b) if two else (dy_a,)
    return hbm_call(
        body, name=name, grid=(S // tr,), in_specs=ins + [row, pl.BlockSpec((tr, 1), lambda i: (i, 0)), vec],
        out_specs=[row, vec, vec],
        out_shape=[jax.ShapeDtypeStruct((S, D), F32), jax.ShapeDtypeStruct((1, D), F32), jax.ShapeDtypeStruct((1, D), F32)],
        compiler_params=_params(("arbitrary",), 48 << 20),
    )(*args, xh, rs, g)


def axpby(a, b, c1, name):
    S, D = a.shape
    tr = min(ROW_TILE, S)

    def body(a_ref, b_ref, o_ref):
        o_ref[...] = c1 * a_ref[...] + b_ref[...]

    row = pl.BlockSpec((tr, D), lambda i: (i, 0))
    return hbm_call(body, name=name, grid=(S // tr,), in_specs=[row, row], out_specs=row,
                          out_shape=jax.ShapeDtypeStruct((S, D), F32), compiler_params=_params(("parallel",)))(a, b)


def loss_head(y, t, name):
    S, D = y.shape
    tr = min(ROW_TILE, S)
    nsteps = S // tr

    def body(y_ref, t_ref, dy_ref, l_ref, acc_ref):
        i = pl.program_id(0)

        @pl.when(i == 0)
        def _():
            acc_ref[...] = jnp.zeros_like(acc_ref)

        e = y_ref[...] - t_ref[...]
        dy_ref[...] = e * (1.0 / D)
        acc_ref[...] += jnp.sum(e * e, axis=0, keepdims=True)

        @pl.when(i == nsteps - 1)
        def _():
            l_ref[...] = jnp.sum(acc_ref[...], axis=1, keepdims=True) * (0.5 / D)

    row = pl.BlockSpec((tr, D), lambda i: (i, 0))
    return hbm_call(
        body, name=name, grid=(nsteps,), in_specs=[row, row],
        out_specs=[row, pl.BlockSpec((1, 1), lambda i: (0, 0))],
        out_shape=[jax.ShapeDtypeStruct((S, D), F32), jax.ShapeDtypeStruct((1, 1), F32)],
        scratch_shapes=[pltpu.VMEM((1, D), F32)], compiler_params=_params(("arbitrary",)),
    )(y, t)


SWIGLU_ROWS = 256


def swiglu_fwd(gu, name):
    _, S, Fh = gu.shape
    tc = _divisors(Fh, LANES, 1536)[0]
    tr = min(SWIGLU_ROWS, S)

    def body(gu_ref, o_ref):
        g = gu_ref[0]
        o_ref[...] = (g * _sigmoid(g) * gu_ref[1]).astype(o_ref.dtype)

    return hbm_call(
        body, name=name, grid=(S // tr, Fh // tc), in_specs=[pl.BlockSpec((2, tr, tc), lambda i, j: (0, i, j))],
        out_specs=pl.BlockSpec((tr, tc), lambda i, j: (i, j)), out_shape=jax.ShapeDtypeStruct((S, Fh), MXU_DTYPE),
        compiler_params=_params(("parallel", "parallel")),
    )(gu)


def swiglu_bwd(gu, dact, name):
    _, S, Fh = gu.shape
    tc = _divisors(Fh, LANES, 1536)[0]
    tr = min(SWIGLU_ROWS, S)

    def body(gu_ref, d_ref, o_ref):
        g, u, d = gu_ref[0], gu_ref[1], d_ref[...]
        s = _sigmoid(g)
        o_ref[0] = (d * u * (s * (1.0 + g * (1.0 - s)))).astype(o_ref.dtype)
        o_ref[1] = (d * (g * s)).astype(o_ref.dtype)

    both = pl.BlockSpec((2, tr, tc), lambda i, j: (0, i, j))
    return hbm_call(
        body, name=name, grid=(S // tr, Fh // tc), in_specs=[both, pl.BlockSpec((tr, tc), lambda i, j: (i, j))],
        out_specs=both, out_shape=jax.ShapeDtypeStruct((2, S, Fh), MXU_DTYPE), compiler_params=_params(("parallel", "parallel")),
    )(gu, dact)


GATE_COLS = 256


def merge_fwd(proj, pr, pa, name):
    S, D = pr.shape
    tr = min(GATE_ROWS, S)
    c0 = (3 * D + 2 * N_KV_HEADS * HEAD_DIM) // GATE_COLS
    c1 = c0 + D // GATE_COLS

    def body(gr_ref, ga_ref, pr_ref, pa_ref, o_ref):
        o_ref[...] = (_sigmoid(gr_ref[...]) * pr_ref[...] + _sigmoid(ga_ref[...]) * pa_ref[...]).astype(o_ref.dtype)

    blk = pl.BlockSpec((tr, GATE_COLS), lambda i, j: (i, j))
    return hbm_call(
        body, name=name, grid=(S // tr, D // GATE_COLS),
        in_specs=[pl.BlockSpec((tr, GATE_COLS), lambda i, j: (i, c0 + j)), pl.BlockSpec((tr, GATE_COLS), lambda i, j: (i, c1 + j)),
                  blk, blk],
        out_specs=blk, out_shape=jax.ShapeDtypeStruct((S, D), MXU_DTYPE), compiler_params=_params(("parallel", "parallel")),
    )(proj, proj, pr, pa)


def merge_bwd(proj, pr, pa, dm, name):
    S, D = pr.shape
    tr = min(GATE_ROWS, S)
    c0 = (3 * D + 2 * N_KV_HEADS * HEAD_DIM) // GATE_COLS
    c1 = c0 + D // GATE_COLS

    def body(gr_ref, ga_ref, pr_ref, pa_ref, dm_ref, dpr_ref, dpa_ref, dgr_ref, dga_ref):
        sr, sa, d = _sigmoid(gr_ref[...]), _sigmoid(ga_ref[...]), dm_ref[...]
        dpr_ref[...] = (d * sr).astype(dpr_ref.dtype)
        dpa_ref[...] = (d * sa).astype(dpa_ref.dtype)
        dgr_ref[...] = (d * pr_ref[...] * (sr * (1.0 - sr))).astype(dgr_ref.dtype)
        dga_ref[...] = (d * pa_ref[...] * (sa * (1.0 - sa))).astype(dga_ref.dtype)

    blk = pl.BlockSpec((tr, GATE_COLS), lambda i, j: (i, j))
    sds = jax.ShapeDtypeStruct((S, D), MXU_DTYPE)
    return hbm_call(
        body, name=name, grid=(S // tr, D // GATE_COLS),
        in_specs=[pl.BlockSpec((tr, GATE_COLS), lambda i, j: (i, c0 + j)), pl.BlockSpec((tr, GATE_COLS), lambda i, j: (i, c1 + j)),
                  blk, blk, blk],
        out_specs=[blk, blk, blk, blk], out_shape=[sds, sds, sds, sds], compiler_params=_params(("parallel", "parallel")),
    )(proj, proj, pr, pa, dm)


RG_ROWS = 512


def _shift_down(cur, prev, d, row, first):
    halo = jnp.where(first, 0.0, pltpu.roll(prev, d, 0))
    return jnp.where(row >= d, pltpu.roll(cur, d, 0), halo)


def _shift_up(cur, nxt, d, row, last, tr):
    halo = jnp.where(last, 0.0, pltpu.roll(nxt, tr - d, 0))
    return jnp.where(row < tr - d, pltpu.roll(cur, tr - d, 0), halo)


def _lru_coeffs(r, lam):
    sp = _softplus_neg(lam)
    la = -LRU_C * r * sp
    return sp, la, jnp.exp(la), _neg_expm1(2.0 * la)


def rg_gates_fwd(proj, conv_w, conv_b, w_rg, b_rg, w_ig, b_ig, lam, name):
    S = proj.shape[0]
    nblk, bw, _ = w_rg.shape
    D = nblk * bw
    tr = min(RG_ROWS, S)

    def body(xr_ref, xp_ref, cw_ref, cb_ref, wr_ref, br_ref, wi_ref, bi_ref, lam_ref, xc_ref, r_ref, i_ref, a_ref, b_ref):
        first = pl.program_id(1) == 0
        cur, prev = xr_ref[...], xp_ref[...]
        row = lax.broadcasted_iota(jnp.int32, cur.shape, 0)
        xc = cb_ref[...]
        for k in range(CONV_WIDTH - 1):
            xc = xc + _shift_down(cur, prev, CONV_WIDTH - 1 - k, row, first) * cw_ref[k:k + 1, :]
        xc = xc + cur * cw_ref[CONV_WIDTH - 1:CONV_WIDTH, :]
        xm = xc.astype(MXU_DTYPE)
        r = _sigmoid(jnp.dot(xm, wr_ref[...].astype(MXU_DTYPE), preferred_element_type=F32) + br_ref[...])
        ig = _sigmoid(jnp.dot(xm, wi_ref[...].astype(MXU_DTYPE), preferred_element_type=F32) + bi_ref[...])
        _, _, a, em = _lru_coeffs(r, lam_ref[...])
        xc_ref[...] = xc
        r_ref[...] = r
        i_ref[...] = ig
        a_ref[...] = a
        b_ref[...] = jnp.sqrt(em) * (ig * xc)

    tile = pl.BlockSpec((tr, bw), lambda n, i: (i, n))
    vec = pl.BlockSpec((1, bw), lambda n, i: (0, n))
    wblk = pl.BlockSpec((None, bw, bw), lambda n, i: (n, 0, 0))
    sds = jax.ShapeDtypeStruct((S, D), F32)
    return hbm_call(
        body, name=name, grid=(nblk, S // tr),
        in_specs=[tile, pl.BlockSpec((tr, bw), lambda n, i: (jnp.maximum(i - 1, 0), n)),
                  pl.BlockSpec((CONV_WIDTH, bw), lambda n, i: (0, n)), vec, wblk, vec, wblk, vec, vec],
        out_specs=[tile] * 5, out_shape=[sds] * 5, compiler_params=_params(("parallel", "parallel")),
    )(proj, proj, conv_w, conv_b, w_rg, b_rg, w_ig, b_ig, lam)


SCAN_COLS = 256
CHUNK = SUBLANES
SCAN_UNROLL = 4


def rg_scan_fwd(proj, a, b, name):
    S, D = a.shape
    cb = min(SCAN_COLS, D)
    goff = D // cb

    def body(a_ref, b_ref, g_ref, hs_ref, y_ref):
        row = lax.broadcasted_iota(jnp.int32, (CHUNK, cb), 0)

        def step(c, carry):
            r0 = pl.multiple_of(c * CHUNK, CHUNK)
            A = a_ref[pl.ds(r0, CHUNK), :]
            B = b_ref[pl.ds(r0, CHUNK), :]
            for d in (1, 2, 4):
                As = jnp.where(row >= d, pltpu.roll(A, d, 0), 1.0)
                Bs = jnp.where(row >= d, pltpu.roll(B, d, 0), 0.0)
                B = A * Bs + B
                A = A * As
            H = B + A * carry
            hs_ref[pl.ds(r0, CHUNK), :] = H
            return jnp.sum(jnp.where(row == CHUNK - 1, H, 0.0), axis=0, keepdims=True)

        lax.fori_loop(0, S // CHUNK, step, jnp.zeros((1, cb), F32), unroll=SCAN_UNROLL)
        y_ref[...] = (hs_ref[...] * _gelu(g_ref[...])).astype(y_ref.dtype)

    col = pl.BlockSpec((S, cb), lambda j: (0, j))
    return hbm_call(
        body, name=name, grid=(D // cb,), in_specs=[col, col, pl.BlockSpec((S, cb), lambda j: (0, goff + j))],
        out_specs=[col, col], out_shape=[jax.ShapeDtypeStruct((S, D), F32), jax.ShapeDtypeStruct((S, D), MXU_DTYPE)],
        compiler_params=_params(("parallel",), _vmem_limit(5 * S * cb * 4, 4 * S * cb * 4)),
    )(a, b, proj)


def rg_scan_bwd(proj, dy, hs, a, name):
    S, D = a.shape
    cb = min(SCAN_COLS, D)
    goff = D // cb
    nchunks = S // CHUNK

    def body(g_ref, dy_ref, hs_ref, a_ref, dg_ref, gt_ref):
        gate, dy = g_ref[...], dy_ref[...]
        dg_ref[...] = (dy * hs_ref[...] * _gelu_grad(gate)).astype(dg_ref.dtype)
        gt_ref[...] = dy * _gelu(gate)
        row = lax.broadcasted_iota(jnp.int32, (CHUNK, cb), 0)

        def step(k, carry):
            c = nchunks - 1 - k
            r0 = pl.multiple_of(c * CHUNK, CHUNK)
            rn = pl.multiple_of(jnp.minimum(c + 1, nchunks - 1) * CHUNK, CHUNK)
            last = c == nchunks - 1
            nxt = jnp.where(last, 0.0, pltpu.roll(a_ref[pl.ds(rn, CHUNK), :], CHUNK - 1, 0))
            A = jnp.where(row < CHUNK - 1, pltpu.roll(a_ref[pl.ds(r0, CHUNK), :], CHUNK - 1, 0), nxt)
            B = gt_ref[pl.ds(r0, CHUNK), :]
            for d in (1, 2, 4):
                As = jnp.where(row < CHUNK - d, pltpu.roll(A, CHUNK - d, 0), 1.0)
                Bs = jnp.where(row < CHUNK - d, pltpu.roll(B, CHUNK - d, 0), 0.0)
                B = A * Bs + B
                A = A * As
            G = B + A * carry
            gt_ref[pl.ds(r0, CHUNK), :] = G
            return jnp.sum(jnp.where(row == 0, G, 0.0), axis=0, keepdims=True)

        lax.fori_loop(0, nchunks, step, jnp.zeros((1, cb), F32), unroll=SCAN_UNROLL)

    col = pl.BlockSpec((S, cb), lambda j: (0, j))
    return hbm_call(
        body, name=name, grid=(D // cb,), in_specs=[pl.BlockSpec((S, cb), lambda j: (0, goff + j)), col, col, col],
        out_specs=[col, col], out_shape=[jax.ShapeDtypeStruct((S, D), MXU_DTYPE), jax.ShapeDtypeStruct((S, D), F32)],
        compiler_params=_params(("parallel",), _vmem_limit(6 * S * cb * 4, 6 * S * cb * 4)),
    )(proj, dy, hs, a)


def rg_gates_bwd(gt, hs, xc, r, ig, w_rg, w_ig, lam, name):
    S, D = xc.shape
    nblk, bw, _ = w_rg.shape
    tr = min(RG_ROWS, S)

    def body(gt_ref, hs_ref, hp_ref, xc_ref, r_ref, i_ref, wr_ref, wi_ref, lam_ref,
             dxc_ref, dwr_ref, dwi_ref, dbr_ref, dbi_ref, dl_ref):
        step = pl.program_id(1)
        g, hs, xc, r, ig, lam = gt_ref[...], hs_ref[...], xc_ref[...], r_ref[...], i_ref[...], lam_ref[...]
        row = lax.broadcasted_iota(jnp.int32, g.shape, 0)
        hprev = _shift_down(hs, hp_ref[...], 1, row, step == 0)
        sp, _, a, em = _lru_coeffs(r, lam)
        mult = jnp.sqrt(em)
        du = g * mult
        dla = g * hprev * a - (g * (ig * xc)) * (a * a) / mult
        dpr = (dla * (-LRU_C * sp)) * (r * (1.0 - r))
        dpi = (du * xc) * (ig * (1.0 - ig))
        dprm, dpim = dpr.astype(MXU_DTYPE), dpi.astype(MXU_DTYPE)
        nt = (((1,), (1,)), ((), ()))
        dxc_ref[...] = (du * ig + lax.dot_general(dprm, wr_ref[...].astype(MXU_DTYPE), nt, preferred_element_type=F32)
                        + lax.dot_general(dpim, wi_ref[...].astype(MXU_DTYPE), nt, preferred_element_type=F32))

        @pl.when(step == 0)
        def _():
            for ref in (dwr_ref, dwi_ref, dbr_ref, dbi_ref, dl_ref):
                ref[...] = jnp.zeros_like(ref)

        xct = xc.T.astype(MXU_DTYPE)
        dwr_ref[...] += jnp.dot(xct, dprm, preferred_element_type=F32)
        dwi_ref[...] += jnp.dot(xct, dpim, preferred_element_type=F32)
        dbr_ref[...] += jnp.sum(dpr, axis=0, keepdims=True)
        dbi_ref[...] += jnp.sum(dpi, axis=0, keepdims=True)
        dl_ref[...] += jnp.sum(dla * (-LRU_C * r), axis=0, keepdims=True) * (-_sigmoid(-lam))

    tile = pl.BlockSpec((tr, bw), lambda n, i: (i, n))
    vec = pl.BlockSpec((1, bw), lambda n, i: (0, n))
    wblk = pl.BlockSpec((None, bw, bw), lambda n, i: (n, 0, 0))
    return hbm_call(
        body, name=name, grid=(nblk, S // tr),
        in_specs=[tile, tile, pl.BlockSpec((tr, bw), lambda n, i: (jnp.maximum(i - 1, 0), n)), tile, tile, tile, wblk, wblk, vec],
        out_specs=[tile, wblk, wblk, vec, vec, vec],
        out_shape=[jax.ShapeDtypeStruct((S, D), F32), jax.ShapeDtypeStruct((nblk, bw, bw), F32), jax.ShapeDtypeStruct((nblk, bw, bw), F32),
                   jax.ShapeDtypeStruct((1, D), F32), jax.ShapeDtypeStruct((1, D), F32), jax.ShapeDtypeStruct((1, D), F32)],
        compiler_params=_params(("parallel", "arbitrary")),
    )(gt, hs, hs, xc, r, ig, w_rg, w_ig, lam)


def rg_conv_bwd(proj, dxc, conv_w, name):
    S, D = dxc.shape
    bw = min(SCAN_COLS, D)
    tr = min(RG_ROWS, S)
    nsteps = S // tr

    def body(d_ref, dn_ref, xr_ref, xp_ref, cw_ref, dxr_ref, dcw_ref, dcb_ref):
        step = pl.program_id(1)
        d, xr = d_ref[...], xr_ref[...]
        row = lax.broadcasted_iota(jnp.int32, d.shape, 0)
        dxr = d * cw_ref[CONV_WIDTH - 1:CONV_WIDTH, :]
        for k in range(CONV_WIDTH - 1):
            dxr = dxr + _shift_up(d, dn_ref[...], CONV_WIDTH - 1 - k, row, step == nsteps - 1, tr) * cw_ref[k:k + 1, :]
        dxr_ref[...] = dxr.astype(dxr_ref.dtype)

        @pl.when(step == 0)
        def _():
            dcw_ref[...] = jnp.zeros_like(dcw_ref)
            dcb_ref[...] = jnp.zeros_like(dcb_ref)

        for k in range(CONV_WIDTH - 1):
            xs = _shift_down(xr, xp_ref[...], CONV_WIDTH - 1 - k, row, step == 0)
            dcw_ref[k:k + 1, :] += jnp.sum(d * xs, axis=0, keepdims=True)
        dcw_ref[CONV_WIDTH - 1:CONV_WIDTH, :] += jnp.sum(d * xr, axis=0, keepdims=True)
        dcb_ref[...] += jnp.sum(d, axis=0, keepdims=True)

    tile = pl.BlockSpec((tr, bw), lambda n, i: (i, n))
    cwb = pl.BlockSpec((CONV_WIDTH, bw), lambda n, i: (0, n))
    return hbm_call(
        body, name=name, grid=(D // bw, nsteps),
        in_specs=[tile, pl.BlockSpec((tr, bw), lambda n, i: (jnp.minimum(i + 1, nsteps - 1), n)), tile,
                  pl.BlockSpec((tr, bw), lambda n, i: (jnp.maximum(i - 1, 0), n)), cwb],
        out_specs=[tile, cwb, pl.BlockSpec((1, bw), lambda n, i: (0, n))],
        out_shape=[jax.ShapeDtypeStruct((S, D), MXU_DTYPE), jax.ShapeDtypeStruct((CONV_WIDTH, D), F32), jax.ShapeDtypeStruct((1, D), F32)],
        compiler_params=_params(("parallel", "arbitrary")),
    )(dxc, dxc, proj, proj, conv_w)


def rope_table(S):
    half = ROT_DIM // 2
    pos = jnp.arange(S, dtype=F32)
    inv = ROPE_THETA ** (-jnp.arange(0, ROT_DIM, 2, dtype=F32) / ROT_DIM)
    ang = pos[:, None] * inv[None, :]
    cos, sin = jnp.cos(ang), jnp.sin(ang)
    zero = jnp.zeros((S, HEAD_DIM - ROT_DIM), F32)
    c = jnp.concatenate([cos, cos, zero + 1.0], axis=1)
    a = jnp.concatenate([-sin, jnp.zeros((S, half), F32), zero], axis=1)
    b = jnp.concatenate([jnp.zeros((S, half), F32), sin, zero], axis=1)
    return jnp.stack([jnp.tile(t, (1, LANES // HEAD_DIM)) for t in (c, a, b)])


def _rope(t, tab):
    half = ROT_DIM // 2
    return t * tab[0] + pltpu.roll(t, LANES - half, 1) * tab[1] + pltpu.roll(t, half, 1) * tab[2]


def _rope_t(d, tab):
    half = ROT_DIM // 2
    return d * tab[0] + pltpu.roll(d * tab[1], half, 1) + pltpu.roll(d * tab[2], LANES - half, 1)


def _dup_head(t, hk, lo):
    sw = pltpu.roll(t, HEAD_DIM, 1)
    return jnp.where(lo, t, sw) if hk == 0 else jnp.where(lo, sw, t)


def _attn_common(n, sink_ref, q_ref, kp_ref, kc_ref, vp_ref, vc_ref, tc_ref, tp_ref, hk, pairs):
    tq = (tc_ref[0], tc_ref[1], tc_ref[2])
    tp = (tp_ref[0], tp_ref[1], tp_ref[2])
    lo = lax.broadcasted_iota(jnp.int32, (WINDOW, LANES), 1) < HEAD_DIM
    lo2 = lax.broadcasted_iota(jnp.int32, (2 * WINDOW, LANES), 1) < HEAD_DIM
    kband = jnp.concatenate([_rope(kp_ref[...], tp), _rope(kc_ref[...], tq)], axis=0)
    vband = jnp.concatenate([vp_ref[...], vc_ref[...]], axis=0)
    kd = _dup_head(kband, hk, lo2).astype(MXU_DTYPE)
    vd = _dup_head(vband, hk, lo2).astype(MXU_DTYPE)
    rows, sks = [], []
    for j in range(pairs):
        col = hk * pairs + j
        qp = _rope(q_ref[:, col * LANES:(col + 1) * LANES], tq)
        rows += [jnp.where(lo, qp, 0.0), jnp.where(lo, 0.0, qp)]
        sks += [jnp.full((WINDOW, 1), sink_ref[2 * col], F32), jnp.full((WINDOW, 1), sink_ref[2 * col + 1], F32)]
    qg = jnp.concatenate(rows, axis=0)
    sk = jnp.concatenate(sks, axis=0)
    G = 2 * pairs * WINDOW
    ri = lax.broadcasted_iota(jnp.int32, (G, 2 * WINDOW), 0) & (WINDOW - 1)
    kj = lax.broadcasted_iota(jnp.int32, (G, 2 * WINDOW), 1) - WINDOW
    valid = (kj <= ri) & (kj > ri - WINDOW) & (kj + n * WINDOW >= 0)
    s = lax.dot_general(qg.astype(MXU_DTYPE), kd, (((1,), (1,)), ((), ())), preferred_element_type=F32) * (HEAD_DIM ** -0.5)
    s = jnp.where(valid, s, NEG_INF)
    m = jnp.maximum(jnp.max(s, axis=1, keepdims=True), sk)
    e = jnp.exp(s - m)
    es = jnp.exp(sk - m)
    inv = 1.0 / (jnp.sum(e, axis=1, keepdims=True) + es)
    return qg, kd, vd, e * inv, es * inv, lo, lo2, tq, tp


def _attn_specs(D, NB):
    kcol = 3 * D // LANES
    q = pl.BlockSpec((WINDOW, D), lambda n: (n, 2))
    kc = pl.BlockSpec((WINDOW, LANES), lambda n: (n, kcol))
    kp = pl.BlockSpec((WINDOW, LANES), lambda n: (jnp.maximum(n - 1, 0), kcol))
    vc = pl.BlockSpec((WINDOW, LANES), lambda n: (n, kcol + 1))
    vp = pl.BlockSpec((WINDOW, LANES), lambda n: (jnp.maximum(n - 1, 0), kcol + 1))
    tc = pl.BlockSpec((3, WINDOW, LANES), lambda n: (0, n, 0))
    tp = pl.BlockSpec((3, WINDOW, LANES), lambda n: (0, jnp.maximum(n - 1, 0), 0))
    sink = pl.BlockSpec(memory_space=pltpu.SMEM)
    return [sink, q, kp, kc, vp, vc, tc, tp]


def attn_fwd(proj, sinks, tab, D, name):
    S = proj.shape[0]
    NB = S // WINDOW
    pairs = D // HEAD_DIM // N_KV_HEADS // 2

    def body(sink_ref, q_ref, kp_ref, kc_ref, vp_ref, vc_ref, tc_ref, tp_ref, o_ref):
        n = pl.program_id(0)
        for hk in range(N_KV_HEADS):
            _, _, vd, p, _, lo, _, _, _ = _attn_common(n, sink_ref, q_ref, kp_ref, kc_ref, vp_ref, vc_ref, tc_ref, tp_ref, hk, pairs)
            o = jnp.dot(p.astype(MXU_DTYPE), vd, preferred_element_type=F32)
            for j in range(pairs):
                col = hk * pairs + j
                oa = o[(2 * j) * WINDOW:(2 * j + 1) * WINDOW]
                ob = o[(2 * j + 1) * WINDOW:(2 * j + 2) * WINDOW]
                o_ref[:, col * LANES:(col + 1) * LANES] = jnp.where(lo, oa, ob)

    return hbm_call(
        body, name=name, grid=(NB,), in_specs=_attn_specs(D, NB),
        out_specs=pl.BlockSpec((WINDOW, D), lambda n: (n, 0)), out_shape=jax.ShapeDtypeStruct((S, D), F32),
        compiler_params=_params(("parallel",)),
    )(sinks, proj, proj, proj, proj, proj, tab, tab)


def attn_bwd(proj, sinks, tab, o, do, D, name):
    S = proj.shape[0]
    NB = S // WINDOW
    pairs = D // HEAD_DIM // N_KV_HEADS // 2

    def body(sink_ref, q_ref, kp_ref, kc_ref, vp_ref, vc_ref, tc_ref, tp_ref, o_ref, do_ref, dq_ref, dk_ref, dv_ref, ds_ref):
        n = pl.program_id(0)

        @pl.when(n == 0)
        def _():
            ds_ref[...] = jnp.zeros_like(ds_ref)

        lane1 = lax.broadcasted_iota(jnp.int32, (1, LANES), 1)
        dsink = jnp.zeros((1, LANES), F32)
        dkt = dvt = None
        for hk in range(N_KV_HEADS):
            qg, kd, vd, p, ps, lo, lo2, tq, tp = _attn_common(n, sink_ref, q_ref, kp_ref, kc_ref, vp_ref, vc_ref, tc_ref, tp_ref, hk, pairs)
            dos, os_ = [], []
            for j in range(pairs):
                col = hk * pairs + j
                dop = do_ref[:, col * LANES:(col + 1) * LANES]
                op = o_ref[:, col * LANES:(col + 1) * LANES]
                dos += [jnp.where(lo, dop, 0.0), jnp.where(lo, 0.0, dop)]
                os_ += [jnp.where(lo, op, 0.0), jnp.where(lo, 0.0, op)]
            dog = jnp.concatenate(dos, axis=0)
            og = jnp.concatenate(os_, axis=0)
            dogm = dog.astype(MXU_DTYPE)
            dp = lax.dot_general(dogm, vd, (((1,), (1,)), ((), ())), preferred_element_type=F32)
            dr = jnp.sum(dog * og, axis=1, keepdims=True)
            ds = p * (dp - dr) * (HEAD_DIM ** -0.5)
            dsm = ds.astype(MXU_DTYPE)
            dqg = jnp.dot(dsm, kd, preferred_element_type=F32)
            dkd = jnp.dot(ds.T.astype(MXU_DTYPE), qg.astype(MXU_DTYPE), preferred_element_type=F32)
            dvd = jnp.dot(p.T.astype(MXU_DTYPE), dogm, preferred_element_type=F32)
            dkf = dkd + pltpu.roll(dkd, HEAD_DIM, 1)
            dvf = dvd + pltpu.roll(dvd, HEAD_DIM, 1)
            if hk == 0:
                dkt, dvt = dkf, dvf
            else:
                dkt, dvt = jnp.where(lo2, dkt, dkf), jnp.where(lo2, dvt, dvf)
            sd = ps * dr
            for j in range(pairs):
                col = hk * pairs + j
                dqa = dqg[(2 * j) * WINDOW:(2 * j + 1) * WINDOW]
                dqb = dqg[(2 * j + 1) * WINDOW:(2 * j + 2) * WINDOW]
                dq_ref[:, col * LANES:(col + 1) * LANES] = _rope_t(jnp.where(lo, dqa, dqb), tq).astype(dq_ref.dtype)
                for t in range(2):
                    part = sd[(2 * j + t) * WINDOW:(2 * j + t + 1) * WINDOW]
                    val = jnp.sum(part, axis=0, keepdims=True)
                    dsink = dsink - jnp.where(lane1 == 2 * col + t, val, 0.0)
        dk_ref[...] = jnp.concatenate([_rope_t(dkt[:WINDOW], tp), _rope_t(dkt[WINDOW:], tq)], axis=0)
        dv_ref[...] = dvt
        ds_ref[...] += dsink

    blk = pl.BlockSpec((WINDOW, D), lambda n: (n, 0))
    band = pl.BlockSpec((None, 2 * WINDOW, LANES), lambda n: (n, 0, 0))
    return hbm_call(
        body, name=name, grid=(NB,), in_specs=_attn_specs(D, NB) + [blk, blk],
        out_specs=[blk, band, band, pl.BlockSpec((1, LANES), lambda n: (0, 0))],
        out_shape=[jax.ShapeDtypeStruct((S, D), MXU_DTYPE), jax.ShapeDtypeStruct((NB, 2 * WINDOW, LANES), F32),
                   jax.ShapeDtypeStruct((NB, 2 * WINDOW, LANES), F32), jax.ShapeDtypeStruct((1, LANES), F32)],
        compiler_params=_params(("arbitrary",)),
    )(sinks, proj, proj, proj, proj, proj, tab, tab, o, do)


def band_fold(dkb, dvb, name):
    NB = dkb.shape[0]
    k4 = dkb.reshape(NB, 2, WINDOW, LANES)
    v4 = dvb.reshape(NB, 2, WINDOW, LANES)

    def body(kc_ref, kn_ref, vc_ref, vn_ref, dk_ref, dv_ref):
        more = pl.program_id(0) < NB - 1
        dk_ref[...] = (kc_ref[...] + jnp.where(more, kn_ref[...], 0.0)).astype(dk_ref.dtype)
        dv_ref[...] = (vc_ref[...] + jnp.where(more, vn_ref[...], 0.0)).astype(dv_ref.dtype)

    cur = pl.BlockSpec((None, None, WINDOW, LANES), lambda n: (n, 1, 0, 0))
    nxt = pl.BlockSpec((None, None, WINDOW, LANES), lambda n: (jnp.minimum(n + 1, NB - 1), 0, 0, 0))
    out = pl.BlockSpec((WINDOW, LANES), lambda n: (n, 0))
    sds = jax.ShapeDtypeStruct((NB * WINDOW, LANES), MXU_DTYPE)
    return hbm_call(body, name=name, grid=(NB,), in_specs=[cur, nxt, cur, nxt], out_specs=[out, out], out_shape=[sds, sds],
                          compiler_params=_params(("parallel",)))(k4, k4, v4, v4)


CROSS_ROWS = 512


def _cross_probs(q, k, scale):
    s = lax.dot_general(q.astype(MXU_DTYPE), k.astype(MXU_DTYPE), (((1,), (1,)), ((), ())), preferred_element_type=F32) * scale
    e = jnp.exp(s - jnp.max(s, axis=1, keepdims=True))
    return e / jnp.sum(e, axis=1, keepdims=True)


def cross_fwd(qc, kv, name):
    S, D = qc.shape
    M = kv.shape[0]
    hd = D // CROSS_HEADS
    tq = min(CROSS_ROWS, S)

    def body(q_ref, kv_ref, o_ref):
        for h in range(CROSS_HEADS):
            p = _cross_probs(q_ref[:, h * hd:(h + 1) * hd], kv_ref[:, h * hd:(h + 1) * hd], hd ** -0.5)
            v = kv_ref[:, D + h * hd:D + (h + 1) * hd].astype(MXU_DTYPE)
            o_ref[:, h * hd:(h + 1) * hd] = jnp.dot(p.astype(MXU_DTYPE), v, preferred_element_type=F32).astype(o_ref.dtype)

    return hbm_call(
        body, name=name, grid=(S // tq,), in_specs=[pl.BlockSpec((tq, D), lambda i: (i, 0)), pl.BlockSpec((M, 2 * D), lambda i: (0, 0))],
        out_specs=pl.BlockSpec((tq, D), lambda i: (i, 0)), out_shape=jax.ShapeDtypeStruct((S, D), MXU_DTYPE),
        compiler_params=_params(("parallel",)),
    )(qc, kv)


def cross_bwd(qc, kv, do, name):
    S, D = qc.shape
    M = kv.shape[0]
    hd = D // CROSS_HEADS
    tq = min(CROSS_ROWS, S)

    def body(q_ref, kv_ref, do_ref, dq_ref, dkv_ref):
        @pl.when(pl.program_id(0) == 0)
        def _():
            dkv_ref[...] = jnp.zeros_like(dkv_ref)

        for h in range(CROSS_HEADS):
            q = q_ref[:, h * hd:(h + 1) * hd]
            k = kv_ref[:, h * hd:(h + 1) * hd]
            v = kv_ref[:, D + h * hd:D + (h + 1) * hd].astype(MXU_DTYPE)
            dom = do_ref[:, h * hd:(h + 1) * hd].astype(MXU_DTYPE)
            p = _cross_probs(q, k, hd ** -0.5)
            dp = lax.dot_general(dom, v, (((1,), (1,)), ((), ())), preferred_element_type=F32)
            ds = p * (dp - jnp.sum(p * dp, axis=1, keepdims=True)) * (hd ** -0.5)
            dq_ref[:, h * hd:(h + 1) * hd] = jnp.dot(ds.astype(MXU_DTYPE), k.astype(MXU_DTYPE),
                                                     preferred_element_type=F32).astype(dq_ref.dtype)
            dkv_ref[:, h * hd:(h + 1) * hd] += jnp.dot(ds.T.astype(MXU_DTYPE), q.astype(MXU_DTYPE), preferred_element_type=F32)
            dkv_ref[:, D + h * hd:D + (h + 1) * hd] += jnp.dot(p.T.astype(MXU_DTYPE), dom, preferred_element_type=F32)

    row = pl.BlockSpec((tq, D), lambda i: (i, 0))
    full = pl.BlockSpec((M, 2 * D), lambda i: (0, 0))
    return hbm_call(
        body, name=name, grid=(S // tq,), in_specs=[row, full, row], out_specs=[row, full],
        out_shape=[jax.ShapeDtypeStruct((S, D), MXU_DTYPE), jax.ShapeDtypeStruct((M, 2 * D), F32)],
        compiler_params=_params(("arbitrary",)),
    )(qc, kv, do)


def adamw(w, g, m, v, name):
    shape = w.shape
    cols = shape[-1]
    rows = int(np.prod(shape[:-1]))
    w2, g2, m2, v2 = (t.reshape(rows, cols) for t in (w, g, m, v))
    tr = _divisors(rows, SUBLANES, max(SUBLANES, (1 << 20) // (cols * 4) // SUBLANES * SUBLANES))[0]

    def body(w_ref, g_ref, m_ref, v_ref, d_ref, mo_ref, vo_ref, go_ref):
        gg = g_ref[...]
        mn = ADAM_B1 * m_ref[...] + (1.0 - ADAM_B1) * gg
        vn = ADAM_B2 * v_ref[...] + (1.0 - ADAM_B2) * (gg * gg)
        m_hat = mn / (1.0 - ADAM_B1 ** ADAM_STEP)
        v_hat = vn / (1.0 - ADAM_B2 ** ADAM_STEP)
        d_ref[...] = -ADAM_LR * (m_hat / (jnp.sqrt(v_hat) + ADAM_EPS) + ADAM_WD * w_ref[...])
        mo_ref[...] = mn
        vo_ref[...] = vn
        go_ref[...] = gg

    blk = pl.BlockSpec((tr, cols), lambda i: (i, 0))
    sds = jax.ShapeDtypeStruct((rows, cols), F32)
    d, mn, vn, go = hbm_call(body, name=name, grid=(rows // tr,), in_specs=[blk] * 4, out_specs=[blk] * 4, out_shape=[sds] * 4,
                                   compiler_params=_params(("parallel",)))(w2, g2, m2, v2)
    return d.reshape(shape), mn.reshape(shape), vn.reshape(shape), go.reshape(shape)


def sum_devices(parts, name):
    n, rows, cols = parts.shape

    def body(p_ref, o_ref):
        acc = p_ref[0]
        for k in range(1, n):
            acc = acc + p_ref[k]
        o_ref[...] = acc

    return pl.pallas_call(body, name=name, in_specs=[pl.BlockSpec(memory_space=pltpu.VMEM)],
                          out_specs=pl.BlockSpec(memory_space=pltpu.VMEM), out_shape=jax.ShapeDtypeStruct((rows, cols), F32))(parts)


HBM_SPEC = pl.BlockSpec(memory_space=pltpu.HBM)


def _place():
    return lax.axis_index("x"), lax.axis_index("y"), lax.axis_index("c")


def _remote(src, dst, send_sems, recv_sems, k, to):
    return pltpu.make_async_remote_copy(src_ref=src, dst_ref=dst, send_sem=send_sems.at[k], recv_sem=recv_sems.at[k],
                                        device_id=to, device_id_type=MESH_ID)


SEM_SPEC = pl.BlockSpec(memory_space=pltpu.SEMAPHORE)
ANY_SPEC = pl.BlockSpec(memory_space=pl.ANY)
SPLIT_COPY = pltpu.CompilerParams(has_side_effects=pltpu.SideEffectType.DATAFLOW_SIDE_EFFECTING)


def _in_hbm(arrays):
    return [pltpu.with_memory_space_constraint(a, pltpu.HBM) for a in arrays]


def _split_start(copies, sources, lands, after, n_sems, name):
    n = len(sources)

    def body(*refs):
        for cp in copies(refs[:n], refs[n:2 * n], refs[2 * n + 1], refs[2 * n + 2]):
            cp.start()
        refs[-1][...] = jnp.zeros_like(refs[-1])

    through = [pltpu.HBM(a.shape, a.dtype) for a in list(sources) + list(lands)]
    outs = pl.pallas_call(
        body, name=name, in_specs=[HBM_SPEC] * (2 * n) + [ANY_SPEC],
        out_specs=[SEM_SPEC, SEM_SPEC] + [HBM_SPEC] * (2 * n) + [pl.BlockSpec(memory_space=pltpu.VMEM)],
        out_shape=[pltpu.SemaphoreType.DMA((n_sems,)), pltpu.SemaphoreType.DMA((n_sems,))] + through
        + [jax.ShapeDtypeStruct((SUBLANES, LANES), F32)],
        input_output_aliases={i: 2 + i for i in range(2 * n)}, compiler_params=SPLIT_COPY,
    )(*_in_hbm(sources), *_in_hbm(lands), after)
    return outs[0], outs[1], outs[2:2 + n], outs[2 + n:2 + 2 * n], outs[-1]


def _split_wait(copies, send_sems, recv_sems, sources, lands, after, name):
    n = len(sources)

    def body(*refs):
        for cp in copies(refs[:n], refs[n:2 * n], refs[2 * n], refs[2 * n + 1]):
            cp.wait_send()
            cp.wait_recv()

    through = [pltpu.HBM(a.shape, a.dtype) for a in list(sources) + list(lands)]
    outs = pl.pallas_call(
        body, name=name, in_specs=[HBM_SPEC] * (2 * n) + [SEM_SPEC, SEM_SPEC, ANY_SPEC], out_specs=[HBM_SPEC] * (2 * n),
        out_shape=through, input_output_aliases={i: i for i in range(2 * n)}, compiler_params=SPLIT_COPY,
    )(*sources, *lands, send_sems, recv_sems, after)
    return outs[:n], outs[n:]


def _chip_slab(land, slot, rows):
    return land.at[slot, rows] if len(land.shape) == 3 else land.at[rows, slot]


def _gather_copies(w_refs, land_refs, send_sems, recv_sems):
    n = len(w_refs)
    x, y, c = _place()
    chips = [(1 - x, y), (x, 1 - y), (1 - x, 1 - y)]
    cps = []
    for a in range(n):
        hr = w_refs[a].shape[0] // 2
        mine, every = pl.ds(c * hr, hr), pl.ds(0, 2 * hr)
        cps.append(_remote(w_refs[a], _chip_slab(land_refs[a], 2 * x + y, every), send_sems, recv_sems, 3 * n + a, (x, y, 1 - c)))
        for k, chip in enumerate(chips):
            cps.append(_remote(w_refs[a].at[mine], _chip_slab(land_refs[a], 2 * x + y, mine), send_sems, recv_sems, 3 * a + k, (*chip, c)))
    return cps


def gather_start(shards, after, name):
    lands = [lax.empty(s.shape[:-2] + (N_CHIPS,) + s.shape[-2:], s.dtype) for s in shards]
    return _split_start(_gather_copies, shards, lands, after, 4 * len(shards), name)


def gather_wait(state, after, name):
    send_sems, recv_sems, sources, lands, _ = state
    return _split_wait(_gather_copies, send_sems, recv_sems, sources, lands, after, name)[1]


def gather_pass(lands, name):
    n = len(lands)

    def body(*refs):
        out_refs, send_sems, recv_sems = refs[n:2 * n], refs[2 * n], refs[2 * n + 1]
        x, y, c = _place()
        chips = [(1 - x, y), (x, 1 - y), (1 - x, 1 - y)]
        sent = []
        for a in range(n):
            hr = out_refs[a].shape[0 if len(out_refs[a].shape) == 4 else 1] // 2
            for k, (px, py) in enumerate(chips):
                landed = _chip_slab(out_refs[a], 2 * px + py, pl.ds(c * hr, hr))
                sent.append(_remote(landed, landed, send_sems, recv_sems, 3 * a + k, (x, y, 1 - c)))
        for cp in sent:
            cp.start()
        for a in range(n):
            hr = out_refs[a].shape[0 if len(out_refs[a].shape) == 4 else 1] // 2
            for k, (px, py) in enumerate(chips):
                theirs = _chip_slab(out_refs[a], 2 * px + py, pl.ds((1 - c) * hr, hr))
                _remote(theirs, theirs, send_sems, recv_sems, 3 * a + k, (x, y, 1 - c)).wait_recv()
        for cp in sent:
            cp.wait_send()

    return hbm_call(
        body, name=name, in_specs=[HBM_SPEC] * n, out_specs=[HBM_SPEC] * n,
        out_shape=[jax.ShapeDtypeStruct(a.shape, a.dtype) for a in lands], input_output_aliases={a: a for a in range(n)},
        scratch_shapes=[pltpu.SemaphoreType.DMA((3 * n,))] * 2,
    )(*lands)


def _scatter_copies(t_refs, land_refs, send_sems, recv_sems):
    x, y, c = _place()
    chips = [(1 - x, y), (x, 1 - y), (1 - x, 1 - y)]
    return [_remote(t_refs[a].at[:, 2 * px + py], land_refs[a].at[:, k], send_sems, recv_sems, 3 * a + k, (px, py, c))
            for a in range(len(t_refs)) for k, (px, py) in enumerate(chips)]


def scatter_start(parts, after, name):
    lands = [lax.empty((t.shape[0], N_CHIPS - 1) + t.shape[2:], t.dtype) for t in parts]
    return _split_start(_scatter_copies, parts, lands, after, 3 * len(parts), name)


def scatter_wait(state, after, name):
    send_sems, recv_sems, sources, lands, _ = state
    return _split_wait(_scatter_copies, send_sems, recv_sems, sources, lands, after, name)


def swap_sibling(parts, name):
    n = len(parts)

    def body(*refs):
        v_refs, out_refs, send_sems, recv_sems = refs[:n], refs[n:2 * n], refs[2 * n], refs[2 * n + 1]
        x, y, c = _place()
        cps = []
        for a in range(n):
            hr = v_refs[a].shape[2] // 2
            cps.append(_remote(v_refs[a].at[:, :, pl.ds((1 - c) * hr, hr)], out_refs[a], send_sems, recv_sems, a, (x, y, 1 - c)))
        for cp in cps:
            cp.start()
        for cp in cps:
            cp.wait()

    return hbm_call(
        body, name=name, in_specs=[HBM_SPEC] * n, out_specs=[HBM_SPEC] * n,
        out_shape=[jax.ShapeDtypeStruct(v.shape[:2] + (v.shape[2] // 2, v.shape[3]), v.dtype) for v in parts],
        scratch_shapes=[pltpu.SemaphoreType.DMA((n,))] * 2,
    )(*parts)


def join_halves(halves, layer, name):
    n = len(halves)

    def body(*refs):
        out_refs, send_sems, recv_sems = refs[n:2 * n], refs[2 * n], refs[2 * n + 1]
        x, y, c = _place()
        cps = []
        for a in range(n):
            hr = out_refs[a].shape[1] // 2
            mine = out_refs[a].at[layer, pl.ds(c * hr, hr)]
            cps.append(_remote(mine, mine, send_sems, recv_sems, a, (x, y, 1 - c)))
        for cp in cps:
            cp.start()
        for a in range(n):
            hr = out_refs[a].shape[1] // 2
            theirs = out_refs[a].at[layer, pl.ds((1 - c) * hr, hr)]
            _remote(theirs, theirs, send_sems, recv_sems, a, (x, y, 1 - c)).wait_recv()
        for cp in cps:
            cp.wait_send()

    return hbm_call(
        body, name=name, in_specs=[HBM_SPEC] * n, out_specs=[HBM_SPEC] * n,
        out_shape=[jax.ShapeDtypeStruct(f.shape, f.dtype) for f in halves], input_output_aliases={a: a for a in range(n)},
        scratch_shapes=[pltpu.SemaphoreType.DMA((n,))] * 2,
    )(*halves)


def gather_devices(v, name):
    def body(v_ref, out_ref, send_sems, recv_sems, local_sem):
        x, y, c = _place()
        me = 4 * x + 2 * y + c
        own = pltpu.make_async_copy(v_ref, out_ref.at[me], local_sem)
        own.start()
        peers = [((x + dx) % 2, (y + dy) % 2, (c + dc) % 2) for dx in (0, 1) for dy in (0, 1) for dc in (0, 1)][1:]
        sent = []
        for k, peer in enumerate(peers):
            cp = pltpu.make_async_remote_copy(src_ref=v_ref, dst_ref=out_ref.at[me], send_sem=send_sems.at[k], recv_sem=recv_sems.at[k],
                                              device_id=peer, device_id_type=MESH_ID)
            cp.start()
            sent.append(cp)
        for k, (px, py, pc) in enumerate(peers):
            slot = out_ref.at[4 * px + 2 * py + pc]
            pltpu.make_async_remote_copy(src_ref=slot, dst_ref=slot, send_sem=send_sems.at[k], recv_sem=recv_sems.at[k],
                                         device_id=(px, py, pc), device_id_type=MESH_ID).wait_recv()
        for cp in sent:
            cp.wait_send()
        own.wait()

    vm = pl.BlockSpec(memory_space=pltpu.VMEM)
    return pl.pallas_call(body, name=name, in_specs=[vm], out_specs=vm, out_shape=jax.ShapeDtypeStruct((N_DEV,) + v.shape, v.dtype),
                          scratch_shapes=[pltpu.SemaphoreType.DMA((N_DEV - 1,)), pltpu.SemaphoreType.DMA((N_DEV - 1,)),
                                          pltpu.SemaphoreType.DMA])(v)


ADD_ROWS = 512


def add_pair(place, a, b, name):
    L, n, hr, cols = b.shape
    tr = _divisors(hr, 2 * SUBLANES, ADD_ROWS)[0]
    nb = hr // tr

    def body(p_ref, a_ref, b_ref, o_ref):
        del p_ref
        o_ref[...] = (a_ref[...].astype(F32) + b_ref[...].astype(F32)).astype(o_ref.dtype)

    blk = pl.BlockSpec((None, None, tr, cols), lambda l, d, i, p: (l, d, i, 0))
    grid_spec = pltpu.PrefetchScalarGridSpec(
        num_scalar_prefetch=1, grid=(L, n, nb),
        in_specs=[pl.BlockSpec((None, None, tr, cols), lambda l, d, i, p: (l, d, p[0] * nb + i, 0)), blk], out_specs=blk)
    return hbm_call(body, name=name, grid_spec=grid_spec, out_shape=jax.ShapeDtypeStruct(b.shape, b.dtype),
                          compiler_params=_params(("parallel", "parallel", "parallel")))(place, a, b)


def add_chips(place, own, others, layer, stacked, name):
    _, n, hr, cols = others.shape
    tr = _divisors(hr, 2 * SUBLANES, ADD_ROWS)[0]
    nb = hr // tr
    create = isinstance(stacked, tuple)

    def body(p_ref, own_ref, *refs):
        del p_ref
        acc = own_ref[...].astype(F32)
        for k in range(n):
            acc = acc + refs[k][...].astype(F32)
        refs[-1][...] = acc

    ins = [pl.BlockSpec((None, None, tr, cols), lambda i, p: (0, p[1], i, 0))]
    ins += [pl.BlockSpec((None, None, tr, cols), functools.partial(lambda k, i, p: (0, k, i, 0), k)) for k in range(n)]
    grid_spec = pltpu.PrefetchScalarGridSpec(num_scalar_prefetch=1, grid=(nb,), in_specs=ins + ([] if create else [ANY_SPEC]),
                                             out_specs=pl.BlockSpec((None, tr, cols), lambda i, p: (layer, p[0] * nb + i, 0)))
    shape = stacked if create else stacked.shape
    return hbm_call(body, name=name, grid_spec=grid_spec, out_shape=jax.ShapeDtypeStruct(shape, F32),
                          input_output_aliases={} if create else {n + 2: 0},
                          compiler_params=_params(("parallel",)))(place, own, *([others] * n), *([] if create else [stacked]))


def _alpha(depth):
    return (2 * depth) ** 0.25


def _wmm(a, weight, mode, name, deps=(), **more):
    arr, how = weight
    return mm(a, arr, mode, name, deps=deps, **how, **more)


def layer_fwd(h, mem, w, tab, alpha, deps=(), late=None):
    D = h.shape[1]
    proj = _wmm(h, w["w_in"], "nn", "mm_proj", deps)
    xc, r, ig, a, b = rg_gates_fwd(proj, w["conv_w"], w["conv_b"], w["w_rg"], w["b_rg"], w["w_ig"], w["b_ig"], w["lru_lambda"], "rg_gates_fwd")
    hs, y_rnn = rg_scan_fwd(proj, a, b, "rg_scan_fwd")
    y_attn = attn_fwd(proj, w["sinks"], tab, D, "attn_fwd")
    deps = ()
    if late is not None:
        rest, deps = late(y_attn)
        w = {**w, **rest}
    pr = _wmm(y_rnn, w["w_br_rnn"], "nn", "mm_br_rnn", deps)
    pa = _wmm(y_attn, w["w_br_attn"], "nn", "mm_br_attn")
    merged = merge_fwd(proj, pr, pa, "merge_fwd")
    mix = _wmm(merged, w["w_out"], "nn", "mm_out")
    h1, xh1, rs1 = ln_fwd(h, mix, w["ln1_g"], w["ln1_b"], alpha, "ln1_fwd")
    qc = _wmm(h1, w["cq_w"], "nn", "mm_cq", out_dtype=MXU_DTYPE)
    kv = _wmm(mem, w["ckv_w"], "nn", "mm_ckv", out_dtype=MXU_DTYPE)
    o = cross_fwd(qc, kv, "cross_fwd")
    co = _wmm(o, w["co_w"], "nn", "mm_co")
    h2, xh2, rs2 = ln_fwd(h1, co, w["ln2_g"], w["ln2_b"], alpha, "ln2_fwd")
    gu = _wmm(h2, w["ffn_wi"], "nn", "mm_ffn_wi", out_blocks=2)
    act = swiglu_fwd(gu, "swiglu_fwd")
    f = _wmm(act, w["ffn_wo"], "nn", "mm_ffn_wo")
    h3, xh3, rs3 = ln_fwd(h2, f, w["ln3_g"], w["ln3_b"], alpha, "ln3_fwd")
    saved = dict(h=h, proj=proj, xc=xc, r=r, ig=ig, a=a, hs=hs, y_rnn=y_rnn, y_attn=y_attn, pr=pr, pa=pa, xh1=xh1, rs1=rs1, h1=h1,
                 qc=qc, kv=kv, o=o, xh2=xh2, rs2=rs2, h2=h2, gu=gu, xh3=xh3, rs3=rs3)
    return h3, saved, w


def layer_bwd(dh, mem, w, s, tab, alpha, deps=(), halfway=None):
    D = dh.shape[1]
    g = {}
    wg = dict(out_dtype=MXU_DTYPE)
    dz3, g["ln3_g"], g["ln3_b"] = ln_bwd(dh, None, s["xh3"], s["rs3"], w["ln3_g"], 1.0, "ln3_bwd")
    act = swiglu_fwd(s["gu"], "swiglu_refwd")
    g["ffn_wo"] = mm(act, dz3, "tn", "mm_d_ffn_wo", deps=deps, **wg)
    dact = _wmm(dz3, w["ffn_wo"], "nt", "mm_dact")
    dgu = swiglu_bwd(s["gu"], dact, "swiglu_bwd")
    g["ffn_wi"] = mm(s["h2"], dgu, "tn", "mm_d_ffn_wi", b_blocks=2, out_blocks=N_CHIPS, **wg)
    dh2 = _wmm(dgu, w["ffn_wi"], "nt", "mm_dh2", a_blocks=2)
    dz2, g["ln2_g"], g["ln2_b"] = ln_bwd(dz3, dh2, s["xh2"], s["rs2"], w["ln2_g"], alpha, "ln2_bwd")
    g["co_w"] = mm(s["o"], dz2, "tn", "mm_d_co", **wg)
    do = _wmm(dz2, w["co_w"], "nt", "mm_do", out_dtype=MXU_DTYPE)
    dqc, dkv = cross_bwd(s["qc"], s["kv"], do, "cross_bwd")
    g["cq_w"] = mm(s["h1"], dqc, "tn", "mm_d_cq", **wg)
    g["ckv_w"] = mm(mem, dkv, "tn", "mm_d_ckv", out_blocks=N_CHIPS, **wg)
    dh1 = _wmm(dqc, w["cq_w"], "nt", "mm_dh1")
    deps = halfway(g, dh1) if halfway is not None else ()
    dz1, g["ln1_g"], g["ln1_b"] = ln_bwd(dz2, dh1, s["xh1"], s["rs1"], w["ln1_g"], alpha, "ln1_bwd")
    merged = merge_fwd(s["proj"], s["pr"], s["pa"], "merge_refwd")
    g["w_out"] = mm(merged, dz1, "tn", "mm_d_out", deps=deps, **wg)
    dm = _wmm(dz1, w["w_out"], "nt", "mm_dmerged")
    dpr, dpa, dg_rnn, dg_attn = merge_bwd(s["proj"], s["pr"], s["pa"], dm, "merge_bwd")
    g["w_br_rnn"] = mm(s["y_rnn"], dpr, "tn", "mm_d_br_rnn", **wg)
    g["w_br_attn"] = mm(s["y_attn"], dpa, "tn", "mm_d_br_attn", **wg)
    dy_rnn = _wmm(dpr, w["w_br_rnn"], "nt", "mm_dy_rnn")
    dy_attn = _wmm(dpa, w["w_br_attn"], "nt", "mm_dy_attn")
    dq, dkb, dvb, dsink = attn_bwd(s["proj"], w["sinks"], tab, s["y_attn"], dy_attn, D, "attn_bwd")
    dk, dv = band_fold(dkb, dvb, "band_fold")
    g["sinks"] = dsink[:, :w["sinks"].shape[0]]
    dgr, gt = rg_scan_bwd(s["proj"], dy_rnn, s["hs"], s["a"], "rg_scan_bwd")
    dxc, g["w_rg"], g["w_ig"], g["b_rg"], g["b_ig"], g["lru_lambda"] = rg_gates_bwd(
        gt, s["hs"], s["xc"], s["r"], s["ig"], w["w_rg"], w["w_ig"], w["lru_lambda"], "rg_gates_bwd")
    dxr, g["conv_w"], g["conv_b"] = rg_conv_bwd(s["proj"], dxc, w["conv_w"], "rg_conv_bwd")
    dproj = jnp.concatenate([dxr, dgr, dq, dk, dv, dg_rnn, dg_attn], axis=1)
    g["w_in"] = mm(s["h"], dproj, "tn", "mm_d_in")
    dhm = _wmm(dproj, w["w_in"], "nt", "mm_dh")
    return axpby(dz1, dhm, alpha, "layer_dx"), g


def local_step(x, mem, target, depth, weights_of, grads_halfway, grads_done):
    alpha = _alpha(depth)
    tab = rope_table(x.shape[0])
    h, saved, layers = x, [], []
    for l in range(depth):
        wl, deps, late = weights_of(l, h)
        h, s, wl = layer_fwd(h, mem, wl, tab, alpha, deps, late)
        layers.append(wl)
        saved.append(s)
    dh, loss = loss_head(h, target, "loss_head")
    deps = ()
    for l in reversed(range(depth)):
        dh, g = layer_bwd(dh, mem, layers[l], saved[l], tab, alpha, deps, grads_halfway(l))
        deps = grads_done(l, g, dh)
    return loss, dh


def _pad_rows(flat):
    n = flat.shape[0]
    rows = -(-n // (LANES * SUBLANES)) * SUBLANES
    return jnp.pad(flat, (0, rows * LANES - n)).reshape(rows, LANES)


def kernel(x, mem, w_in, conv_w, conv_b, w_rg, b_rg, w_ig, b_ig, lru_lambda, w_br_rnn, w_br_attn, sinks, w_out, ln1_g, ln1_b, cq_w, ckv_w, co_w, ln2_g, ln2_b, ffn_wi, ffn_wo, ln3_g, ln3_b, loss_target, m_w_in, m_conv_w, m_conv_b, m_w_rg, m_b_rg, m_w_ig, m_b_ig, m_lru_lambda, m_w_br_rnn, m_w_br_attn, m_sinks, m_w_out, m_ln1_g, m_ln1_b, m_cq_w, m_ckv_w, m_co_w, m_ln2_g, m_ln2_b, m_ffn_wi, m_ffn_wo, m_ln3_g, m_ln3_b, v_w_in, v_conv_w, v_conv_b, v_w_rg, v_b_rg, v_w_ig, v_b_ig, v_lru_lambda, v_w_br_rnn, v_w_br_attn, v_sinks, v_w_out, v_ln1_g, v_ln1_b, v_cq_w, v_ckv_w, v_co_w, v_ln2_g, v_ln2_b, v_ffn_wi, v_ffn_wo, v_ln3_g, v_ln3_b):
    args = dict(locals())
    w = {n: args[n] for n in WEIGHTS}
    m = {n: args["m_" + n] for n in WEIGHTS}
    v = {n: args["v_" + n] for n in WEIGHTS}
    cx, cy, cc = _place()
    chip = 2 * cx + cy
    L = w_in.shape[0]

    place = jnp.stack([cc, chip]).astype(jnp.int32)
    cw_rows = _pad_rows(conv_w.reshape(-1))
    cw_all = gather_devices(cw_rows, "gather_conv_w")[0::2]
    cw_parts = cw_all.reshape(N_CHIPS, -1)[:, :conv_w.size].reshape((N_CHIPS,) + conv_w.shape)
    conv_full = jnp.concatenate([cw_parts[k] for k in range(N_CHIPS)], axis=2)

    shards = [{n: w[n][l].astype(MXU_DTYPE) for n in BIG} for l in range(L)]
    late_names = tuple(n for n in BIG if n not in GATHER_FIRST)
    gathering = {(0, GATHER_FIRST): gather_start([shards[0][n] for n in GATHER_FIRST], cw_rows, "gather_start_0a")}
    gathering[0, late_names] = gather_start([shards[0][n] for n in late_names], gathering[0, GATHER_FIRST][4], "gather_start_0b")

    def gathered(l, names, after, tag):
        lands = gather_pass(gather_wait(gathering.pop((l, names)), after, f"gather_wait_{tag}"), f"gather_pass_{tag}")
        wl = {}
        for n, gw in zip(names, lands):
            rows_joined = gw.reshape(gw.shape[:-3] + (-1, gw.shape[-1]))
            if n == "w_in":
                wl[n] = (jnp.concatenate([gw[k] for k in range(N_CHIPS)], axis=1), {})
            elif n in COL_BLOCKED:
                wl[n] = (gw, dict(b_blocks=N_CHIPS))
            elif n in GATE_WEIGHTS:
                wl[n] = rows_joined
            else:
                wl[n] = (rows_joined, {})
        return wl, lands

    def start_next(l, after):
        if l + 1 == L:
            return ()
        gathering[l + 1, BIG] = gather_start([shards[l + 1][n] for n in BIG], after, f"gather_start_{l + 1}")
        return (gathering[l + 1, BIG][4],)

    def weights_of(l, h):
        deps, late = (), None
        if l == 0:
            wl, _ = gathered(0, GATHER_FIRST, h, "0a")

            def late(after):
                rest, lands = gathered(0, late_names, after, "0b")
                return rest, start_next(0, lands[0])
        else:
            wl, lands = gathered(l, BIG, h, str(l))
            deps = start_next(l, lands[0])
        for n in SMALL:
            wl[n] = conv_full[l] if n == "conv_w" else w[n][l] if n == "sinks" else w[n][l][None, :]
        return wl, deps, late

    def for_chips(n, g):
        if n in COL_BLOCKED:
            return g
        if n in GATE_WEIGHTS:
            nb, bw, _ = g.shape
            g = g.reshape(nb, N_CHIPS, bw // N_CHIPS, bw).transpose(1, 0, 2, 3).reshape(N_CHIPS, nb * bw // N_CHIPS, bw)
        elif SHARD_AXIS[n] == 0:
            g = g.reshape(N_CHIPS, g.shape[0] // N_CHIPS, g.shape[1])
        else:
            g = jnp.stack(jnp.split(g, N_CHIPS, axis=1))
        return g.astype(MXU_DTYPE)

    reduced, scattering, small_grads = {}, {}, [None] * L
    late_grads = tuple(n for n in BIG if n not in SCATTER_FIRST)

    def start_scatter(l, names, g, after, tag):
        partial_sums = [for_chips(n, g[n])[None] for n in names]
        from_sibling = swap_sibling(partial_sums, f"grad_to_sibling_{tag}")
        chip_sums = [add_pair(place, a, b, f"grad_add_pair_{n}_{l}") for n, a, b in zip(names, partial_sums, from_sibling)]
        scattering[l, names] = scatter_start(chip_sums, after, f"grad_scatter_start_{tag}")
        return (scattering[l, names][4],)

    def finish_layer(l, after):
        for names in [k[1] for k in list(scattering) if k[0] == l]:
            tag = str(l) if names == BIG else f"{l}{'a' if names == SCATTER_FIRST else 'b'}"
            chip_sums, from_chips = scatter_wait(scattering.pop((l, names)), after, f"grad_scatter_wait_{tag}")
            for n, own, others in zip(names, chip_sums, from_chips):
                target = reduced.get(n, (L, 2 * own.shape[2], own.shape[3]))
                reduced[n] = add_chips(place, own, others, l, target, f"grad_add_chips_{n}_{l}")
        reduced.update(zip(BIG, join_halves([reduced[n] for n in BIG], l, f"grad_join_{l}")))

    def grads_halfway(l):
        if l > 0:
            return None

        def halfway(g, after):
            if L > 1:
                finish_layer(1, after)
            return start_scatter(0, SCATTER_FIRST, g, after, "0a")

        return halfway

    def grads_done(l, g, dh):
        small_grads[l] = {n: g[n] for n in SMALL}
        if l == 0:
            return start_scatter(0, late_grads, g, dh, "0b")
        if l + 1 < L:
            finish_layer(l + 1, dh)
        return start_scatter(l, BIG, g, dh, str(l))

    loss11, dx = local_step(x[0], mem[0], loss_target[0], L, weights_of, grads_halfway, grads_done)
    finish_layer(0, dx)
    loss = lax.psum(loss11[0, 0], ("x", "y", "c"))
    gshard = {n: reduced[n].reshape(w[n].shape) for n in BIG}

    small_full = {n: jnp.stack([gl[n] for gl in small_grads]).reshape(w[n].shape[:1] + ((CONV_WIDTH, -1) if n == "conv_w" else (-1,)))
                  for n in SMALL}
    small_flat = jnp.concatenate([small_full[n].reshape(-1) for n in SMALL])
    small_sum = sum_devices(gather_devices(_pad_rows(small_flat), "gather_small_grads"), "sum_small_grads").reshape(-1)
    off = 0
    for n in SMALL:
        gfull = small_sum[off:off + small_full[n].size].reshape(small_full[n].shape)
        off += small_full[n].size
        if n == "conv_w":
            width = conv_w.shape[2]
            gfull = lax.dynamic_slice_in_dim(gfull, chip * width, width, axis=2)
        gshard[n] = gfull

    delta, new_m, new_v, grad = {}, {}, {}, {}
    for n in WEIGHTS:
        delta[n], new_m[n], new_v[n], grad[n] = adamw(w[n], gshard[n], m[n], v[n], "adamw_" + n)
    return (loss, dx[None], *[grad[n] for n in WEIGHTS], *[delta[n] for n in WEIGHTS], *[new_m[n] for n in WEIGHTS],
            *[new_v[n] for n in WEIGHTS])
```

```python
import functools
import math

import jax
import jax.numpy as jnp
import numpy as np
from jax import lax
from jax.experimental import pallas as pl
from jax.experimental.pallas import tpu as pltpu

F32 = jnp.float32
BF16 = jnp.bfloat16
MXU_DTYPE = BF16

HEAD_DIM = 64
N_KV_HEADS = 2
WINDOW = 128
ROT_DIM = HEAD_DIM // 4
ROPE_THETA = 500000.0
CROSS_HEADS = 4
RNN_BLOCKS = 4
CONV_WIDTH = 4
LRU_C = 8.0
LN_EPS = 1e-5
NEG_INF = -1e30
ADAM_LR = 0.001
ADAM_B1 = 0.9
ADAM_B2 = 0.999
ADAM_EPS = 1e-08
ADAM_WD = 0.01
ADAM_STEP = 10

VMEM_BYTES_V7X = 64 * 1024 * 1024
VMEM_BLOCK_BUDGET = 36 * 1024 * 1024
LANES = 128
SUBLANES = 8

MESH_ID = pl.DeviceIdType.MESH
N_CHIPS = 4
N_DEV = 8

BIG = ("w_in", "w_rg", "w_ig", "w_br_rnn", "w_br_attn", "w_out", "cq_w", "ckv_w", "co_w", "ffn_wi", "ffn_wo")
SHARD_AXIS = {"w_in": 1, "w_rg": 1, "w_ig": 1, "w_br_rnn": 0, "w_br_attn": 0, "w_out": 0, "cq_w": 0, "ckv_w": 1,
              "co_w": 0, "ffn_wi": 1, "ffn_wo": 0}
SMALL = ("conv_w", "conv_b", "b_rg", "b_ig", "lru_lambda", "sinks", "ln1_g", "ln1_b", "ln2_g", "ln2_b", "ln3_g", "ln3_b")
WEIGHTS = ("w_in", "conv_w", "conv_b", "w_rg", "b_rg", "w_ig", "b_ig", "lru_lambda", "w_br_rnn", "w_br_attn", "sinks",
           "w_out", "ln1_g", "ln1_b", "cq_w", "ckv_w", "co_w", "ln2_g", "ln2_b", "ffn_wi", "ffn_wo", "ln3_g", "ln3_b")
GATE_WEIGHTS = ("w_rg", "w_ig")
COL_BLOCKED = ("ckv_w", "ffn_wi")
GATHER_FIRST = ("w_in", "w_rg", "w_ig")
SCATTER_FIRST = ("ffn_wo", "ffn_wi", "co_w", "cq_w", "ckv_w")


def _params(dims=None, vmem=None):
    return pltpu.CompilerParams(dimension_semantics=dims, vmem_limit_bytes=vmem)


def _vmem_limit(block_bytes, temp_bytes=0):
    want = int(2 * block_bytes + temp_bytes) + (6 << 20)
    return max(32 << 20, min(want, VMEM_BYTES_V7X - (6 << 20)))


def _divisors(n, align, cap):
    out = [d for d in range(align, min(n, cap) + 1, align) if n % d == 0]
    if n <= cap and n not in out:
        out.append(n)
    return sorted(out, reverse=True) or [n]


PIN_MIN_ELEMENTS = 1 << 18


def hbm_call(body, **kw):
    def in_hbm(s):
        return pltpu.HBM(s.shape, s.dtype) if math.prod(s.shape) >= PIN_MIN_ELEMENTS else s

    shapes = kw.pop("out_shape")
    shapes = [in_hbm(s) for s in shapes] if isinstance(shapes, (list, tuple)) else in_hbm(shapes)
    call = pl.pallas_call(body, out_shape=shapes, **kw)

    def run(*args):
        return call(*[pltpu.with_memory_space_constraint(a, pltpu.HBM) if a.size >= PIN_MIN_ELEMENTS else a for a in args])

    return run


def _sigmoid(x):
    return 1.0 / (1.0 + jnp.exp(-x))


def _gelu_parts(x):
    c = math.sqrt(2.0 / math.pi)
    u = c * (x + 0.044715 * x * x * x)
    t = jnp.tanh(u)
    return t, c * (1.0 + 3 * 0.044715 * x * x)


def _gelu(x):
    t, _ = _gelu_parts(x)
    return 0.5 * x * (1.0 + t)


def _gelu_grad(x):
    t, du = _gelu_parts(x)
    return 0.5 * (1.0 + t) + 0.5 * x * (1.0 - t * t) * du


def _neg_expm1(x):
    series = x * (1.0 + x * (0.5 + x * (1.0 / 6 + x * (1.0 / 24 + x * (1.0 / 120)))))
    return -jnp.where(x > -0.1, series, jnp.exp(x) - 1.0)


def _softplus_neg(lam):
    x = -lam
    return jnp.maximum(x, 0.0) + jnp.log1p(jnp.exp(-jnp.abs(x)))


STEP_US = 0.35
HBM_BYTES_PER_US = 2.5e6
MXU_FLOPS_PER_US = 7e8


def _layer_norm(z, g, b):
    mu = jnp.mean(z, axis=-1, keepdims=True)
    zc = z - mu
    rs = lax.rsqrt(jnp.mean(zc * zc, axis=-1, keepdims=True) + LN_EPS)
    xh = zc * rs
    return xh * g + b, xh, rs


def mm(a, b, mode, name, *, b_index=(), a_blocks=0, b_blocks=0, out_blocks=0, out_dtype=F32, deps=(), post_norm=None):
    nlead = len(b_index) + (1 if b_blocks else 0)
    bk, bn = b.shape[nlead:]
    M, K = (a.shape[-1], a.shape[-2]) if mode == "tn" else (a.shape[-2], a.shape[-1] * max(a_blocks, 1))
    N = bk if mode == "nt" else bn * max(b_blocks, 1) if mode == "nn" or mode == "tn" else bn
    asz, bsz, osz = a.dtype.itemsize, b.dtype.itemsize, jnp.dtype(out_dtype).itemsize
    n_unit = math.gcd(N // max(out_blocks, 1), N // max(b_blocks, 1) if mode != "nt" else N)
    k_unit = math.gcd(K // max(a_blocks, 1), K // max(b_blocks, 1) if mode == "nt" else K)
    tms = _divisors(M, LANES if mode == "tn" else SUBLANES, 2048)
    tns = [N] if post_norm else _divisors(n_unit, LANES, 2048)
    tks = _divisors(k_unit, LANES, k_unit)
    best = None
    for tm in tms:
        for tn in tns:
            for tk in tks:
                nk = K // tk
                scratch = tm * tn * 4 if (nk > 1 and osz != 4) else 0
                blocks = tm * tk * asz + tn * tk * bsz + tm * tn * osz * (3 if post_norm else 1)
                temps = tm * tk * (2 + (4 if mode == "tn" else 0)) + tn * tk * 2 + tm * tn * 4 + scratch
                if 2 * blocks + temps > VMEM_BLOCK_BUDGET + (8 << 20):
                    continue
                ni, nj = M // tm, N // tn
                traffic = M * K * asz * (nj if nk > 1 else 1) + N * K * bsz * (1 if nj * nk == 1 else ni) + M * N * osz
                busy = max(traffic / HBM_BYTES_PER_US, 2.0 * M * N * K / MXU_FLOPS_PER_US)
                cost = ni * nj * nk * STEP_US + busy + blocks / HBM_BYTES_PER_US
                if best is None or cost < best[0]:
                    best = (cost, tm, tn, tk, blocks, temps)
    _, tm, tn, tk, blocks, temps = best
    nk = K // tk
    use_scratch = nk > 1 and osz != 4

    def split(index, total, blocks, tile):
        per = total // blocks // tile
        return index // per, index % per

    def body(a_ref, b_ref, *rest):
        rest = rest[len(deps):]
        if post_norm:
            h_ref, g_ref, beta_ref, o_ref, xh_ref, rs_ref = rest[:6]
            acc = rest[6:]
        else:
            o_ref, acc = rest[0], rest[1:]
        av = a_ref[...].astype(MXU_DTYPE)
        bv = b_ref[...].astype(MXU_DTYPE)
        dn = {"nn": (((1,), (0,)), ((), ())), "nt": (((1,), (1,)), ((), ())), "tn": (((0,), (0,)), ((), ()))}[mode]
        r = lax.dot_general(av, bv, dn, preferred_element_type=F32)

        def normalise(f):
            o_ref[...], xh_ref[...], rs_ref[...] = _layer_norm(post_norm[3] * h_ref[...] + f, g_ref[...], beta_ref[...])

        if nk == 1 and post_norm:
            normalise(r)
        elif nk == 1:
            o_ref[...] = r.astype(o_ref.dtype)
        else:
            acc_ref = acc[0] if use_scratch else o_ref

            @pl.when(pl.program_id(2) == 0)
            def _():
                acc_ref[...] = r

            @pl.when(pl.program_id(2) > 0)
            def _():
                acc_ref[...] += r

            if use_scratch:
                @pl.when(pl.program_id(2) == nk - 1)
                def _():
                    o_ref[...] = acc_ref[...].astype(o_ref.dtype)
            elif post_norm:
                @pl.when(pl.program_id(2) == nk - 1)
                def _():
                    normalise(o_ref[...])

    if mode == "tn":
        a_spec = pl.BlockSpec((tk, tm), lambda i, j, k: (k, i))
    elif a_blocks:
        a_spec = pl.BlockSpec((None, tm, tk), lambda i, j, k: (split(k, K, a_blocks, tk)[0], i, split(k, K, a_blocks, tk)[1]))
    else:
        a_spec = pl.BlockSpec((tm, tk), lambda i, j, k: (i, k))
    lead = (None,) * nlead
    if mode == "nt":
        bmap = ((lambda i, j, k: b_index + (split(k, K, b_blocks, tk)[0], j, split(k, K, b_blocks, tk)[1])) if b_blocks
                else (lambda i, j, k: b_index + (j, k)))
        b_spec = pl.BlockSpec(lead + (tn, tk), bmap)
    else:
        bmap = ((lambda i, j, k: b_index + (split(j, N, b_blocks, tn)[0], k, split(j, N, b_blocks, tn)[1])) if b_blocks
                else (lambda i, j, k: b_index + (k, j)))
        b_spec = pl.BlockSpec(lead + (tk, tn), bmap)
    if out_blocks:
        o_spec = pl.BlockSpec((None, tm, tn), lambda i, j, k: (split(j, N, out_blocks, tn)[0], i, split(j, N, out_blocks, tn)[1]))
        o_shape = jax.ShapeDtypeStruct((out_blocks, M, N // out_blocks), out_dtype)
    else:
        o_spec = pl.BlockSpec((tm, tn), lambda i, j, k: (i, j))
        o_shape = jax.ShapeDtypeStruct((M, N), out_dtype)
    in_specs, extra = [a_spec, b_spec] + [pl.BlockSpec(memory_space=pl.ANY)] * len(deps), ()
    if post_norm:
        vec = pl.BlockSpec((1, N), lambda i, j, k: (0, 0))
        in_specs += [pl.BlockSpec((tm, N), lambda i, j, k: (i, 0)), vec, vec]
        o_spec = [o_spec, pl.BlockSpec((tm, N), lambda i, j, k: (i, 0)), pl.BlockSpec((tm, 1), lambda i, j, k: (i, 0))]
        o_shape = [o_shape, jax.ShapeDtypeStruct((M, N), F32), jax.ShapeDtypeStruct((M, 1), F32)]
        extra = post_norm[:3]
    return hbm_call(
        body, name=name, grid=(M // tm, N // tn, nk), in_specs=in_specs, out_specs=o_spec, out_shape=o_shape,
        scratch_shapes=[pltpu.VMEM((tm, tn), F32)] if use_scratch else [],
        compiler_params=_params(("parallel", "parallel", "arbitrary"), _vmem_limit(blocks, temps)),
    )(a, b, *deps, *extra)


ROW_TILE = 512
GATE_ROWS = 1024


def ln_bwd(dy_a, dy_b, xh, rs, g, c1, name):
    S, D = xh.shape
    tr = min(ROW_TILE, S)
    two = dy_b is not None

    def body(*refs):
        if two:
            a_ref, b_ref, xh_ref, rs_ref, g_ref, dz_ref, dg_ref, db_ref = refs
            dy = c1 * a_ref[...] + b_ref[...]
        else:
            a_ref, xh_ref, rs_ref, g_ref, dz_ref, dg_ref, db_ref = refs
            dy = a_ref[...]
        x = xh_ref[...]
        dyg = dy * g_ref[...]
        m1 = jnp.mean(dyg, axis=-1, keepdims=True)
        m2 = jnp.mean(dyg * x, axis=-1, keepdims=True)
        dz_ref[...] = rs_ref[...] * (dyg - m1 - x * m2)

        @pl.when(pl.program_id(0) == 0)
        def _():
            dg_ref[...] = jnp.zeros_like(dg_ref)
            db_ref[...] = jnp.zeros_like(db_ref)

        dg_ref[...] += jnp.sum(dy * x, axis=0, keepdims=True)
        db_ref[...] += jnp.sum(dy, axis=0, keepdims=True)

    row = pl.BlockSpec((tr, D), lambda i: (i, 0))
    vec = pl.BlockSpec((1, D), lambda i: (0, 0))
    ins = [row, row] if two else [row]
    args = (dy_a, dy_b) if two else (dy_a,)
    return hbm_call(
        body, name=name, grid=(S // tr,), in_specs=ins + [row, pl.BlockSpec((tr, 1), lambda i: (i, 0)), vec],
        out_specs=[row, vec, vec],
        out_shape=[jax.ShapeDtypeStruct((S, D), F32), jax.ShapeDtypeStruct((1, D), F32), jax.ShapeDtypeStruct((1, D), F32)],
        compiler_params=_params(("arbitrary",), 48 << 20),
    )(*args, xh, rs, g)


def axpby(a, b, c1, name):
    S, D = a.shape
    tr = min(ROW_TILE, S)

    def body(a_ref, b_ref, o_ref):
        o_ref[...] = c1 * a_ref[...] + b_ref[...]

    row = pl.BlockSpec((tr, D), lambda i: (i, 0))
    return hbm_call(body, name=name, grid=(S // tr,), in_specs=[row, row], out_specs=row,
                          out_shape=jax.ShapeDtypeStruct((S, D), F32), compiler_params=_params(("parallel",)))(a, b)


def loss_head(y, t, name):
    S, D = y.shape
    tr = min(ROW_TILE, S)
    nsteps = S // tr

    def body(y_ref, t_ref, dy_ref, l_ref, acc_ref):
        i = pl.program_id(0)

        @pl.when(i == 0)
        def _():
            acc_ref[...] = jnp.zeros_like(acc_ref)

        e = y_ref[...] - t_ref[...]
        dy_ref[...] = e * (1.0 / D)
        acc_ref[...] += jnp.sum(e * e, axis=0, keepdims=True)

        @pl.when(i == nsteps - 1)
        def _():
            l_ref[...] = jnp.sum(acc_ref[...], axis=1, keepdims=True) * (0.5 / D)

    row = pl.BlockSpec((tr, D), lambda i: (i, 0))
    return hbm_call(
        body, name=name, grid=(nsteps,), in_specs=[row, row],
        out_specs=[row, pl.BlockSpec((1, 1), lambda i: (0, 0))],
        out_shape=[jax.ShapeDtypeStruct((S, D), F32), jax.ShapeDtypeStruct((1, 1), F32)],
        scratch_shapes=[pltpu.VMEM((1, D), F32)], compiler_params=_params(("arbitrary",)),
    )(y, t)


SWIGLU_ROWS = 256


def swiglu_fwd(gu, name):
    _, S, Fh = gu.shape
    tc = _divisors(Fh, LANES, 1536)[0]
    tr = min(SWIGLU_ROWS, S)

    def body(gu_ref, o_ref):
        g = gu_ref[0]
        o_ref[...] = (g * _sigmoid(g) * gu_ref[1]).astype(o_ref.dtype)

    return hbm_call(
        body, name=name, grid=(S // tr, Fh // tc), in_specs=[pl.BlockSpec((2, tr, tc), lambda i, j: (0, i, j))],
        out_specs=pl.BlockSpec((tr, tc), lambda i, j: (i, j)), out_shape=jax.ShapeDtypeStruct((S, Fh), MXU_DTYPE),
        compiler_params=_params(("parallel", "parallel")),
    )(gu)


def swiglu_bwd(gu, dact, name):
    _, S, Fh = gu.shape
    tc = _divisors(Fh, LANES, 1536)[0]
    tr = min(SWIGLU_ROWS, S)

    def body(gu_ref, d_ref, o_ref):
        g, u, d = gu_ref[0], gu_ref[1], d_ref[...]
        s = _sigmoid(g)
        o_ref[0] = (d * u * (s * (1.0 + g * (1.0 - s)))).astype(o_ref.dtype)
        o_ref[1] = (d * (g * s)).astype(o_ref.dtype)

    both = pl.BlockSpec((2, tr, tc), lambda i, j: (0, i, j))
    return hbm_call(
        body, name=name, grid=(S // tr, Fh // tc), in_specs=[both, pl.BlockSpec((tr, tc), lambda i, j: (i, j))],
        out_specs=both, out_shape=jax.ShapeDtypeStruct((2, S, Fh), MXU_DTYPE), compiler_params=_params(("parallel", "parallel")),
    )(gu, dact)


GATE_COLS = 256


def merge_fwd(proj, pr, pa, name):
    S, D = pr.shape
    tr = min(GATE_ROWS, S)
    c0 = (3 * D + 2 * N_KV_HEADS * HEAD_DIM) // GATE_COLS
    c1 = c0 + D // GATE_COLS

    def body(gr_ref, ga_ref, pr_ref, pa_ref, o_ref):
        o_ref[...] = (_sigmoid(gr_ref[...]) * pr_ref[...] + _sigmoid(ga_ref[...]) * pa_ref[...]).astype(o_ref.dtype)

    blk = pl.BlockSpec((tr, GATE_COLS), lambda i, j: (i, j))
    return hbm_call(
        body, name=name, grid=(S // tr, D // GATE_COLS),
        in_specs=[pl.BlockSpec((tr, GATE_COLS), lambda i, j: (i, c0 + j)), pl.BlockSpec((tr, GATE_COLS), lambda i, j: (i, c1 + j)),
                  blk, blk],
        out_specs=blk, out_shape=jax.ShapeDtypeStruct((S, D), MXU_DTYPE), compiler_params=_params(("parallel", "parallel")),
    )(proj, proj, pr, pa)


def merge_bwd(proj, pr, pa, dm, name):
    S, D = pr.shape
    tr = min(GATE_ROWS, S)
    c0 = (3 * D + 2 * N_KV_HEADS * HEAD_DIM) // GATE_COLS
    c1 = c0 + D // GATE_COLS

    def body(gr_ref, ga_ref, pr_ref, pa_ref, dm_ref, dpr_ref, dpa_ref, dgr_ref, dga_ref):
        sr, sa, d = _sigmoid(gr_ref[...]), _sigmoid(ga_ref[...]), dm_ref[...]
        dpr_ref[...] = (d * sr).astype(dpr_ref.dtype)
        dpa_ref[...] = (d * sa).astype(dpa_ref.dtype)
        dgr_ref[...] = (d * pr_ref[...] * (sr * (1.0 - sr))).astype(dgr_ref.dtype)
        dga_ref[...] = (d * pa_ref[...] * (sa * (1.0 - sa))).astype(dga_ref.dtype)

    blk = pl.BlockSpec((tr, GATE_COLS), lambda i, j: (i, j))
    sds = jax.ShapeDtypeStruct((S, D), MXU_DTYPE)
    return hbm_call(
        body, name=name, grid=(S // tr, D // GATE_COLS),
        in_specs=[pl.BlockSpec((tr, GATE_COLS), lambda i, j: (i, c0 + j)), pl.BlockSpec((tr, GATE_COLS), lambda i, j: (i, c1 + j)),
                  blk, blk, blk],
        out_specs=[blk, blk, blk, blk], out_shape=[sds, sds, sds, sds], compiler_params=_params(("parallel", "parallel")),
    )(proj, proj, pr, pa, dm)


RG_ROWS = 512


def _shift_down(cur, prev, d, row, first):
    halo = jnp.where(first, 0.0, pltpu.roll(prev, d, 0))
    return jnp.where(row >= d, pltpu.roll(cur, d, 0), halo)


def _shift_up(cur, nxt, d, row, last, tr):
    halo = jnp.where(last, 0.0, pltpu.roll(nxt, tr - d, 0))
    return jnp.where(row < tr - d, pltpu.roll(cur, tr - d, 0), halo)


def _lru_coeffs(r, lam):
    sp = _softplus_neg(lam)
    la = -LRU_C * r * sp
    return sp, la, jnp.exp(la), _neg_expm1(2.0 * la)


def rg_gates_fwd(proj, conv_w, conv_b, w_rg, b_rg, w_ig, b_ig, lam, name):
    S = proj.shape[0]
    nblk, bw, _ = w_rg.shape
    D = nblk * bw
    tr = min(RG_ROWS, S)

    def body(xr_ref, xp_ref, cw_ref, cb_ref, wr_ref, br_ref, wi_ref, bi_ref, lam_ref, xc_ref, r_ref, i_ref, a_ref, b_ref):
        first = pl.program_id(1) == 0
        cur, prev = xr_ref[...], xp_ref[...]
        row = lax.broadcasted_iota(jnp.int32, cur.shape, 0)
        xc = cb_ref[...]
        for k in range(CONV_WIDTH - 1):
            xc = xc + _shift_down(cur, prev, CONV_WIDTH - 1 - k, row, first) * cw_ref[k:k + 1, :]
        xc = xc + cur * cw_ref[CONV_WIDTH - 1:CONV_WIDTH, :]
        xm = xc.astype(MXU_DTYPE)
        r = _sigmoid(jnp.dot(xm, wr_ref[...].astype(MXU_DTYPE), preferred_element_type=F32) + br_ref[...])
        ig = _sigmoid(jnp.dot(xm, wi_ref[...].astype(MXU_DTYPE), preferred_element_type=F32) + bi_ref[...])
        _, _, a, em = _lru_coeffs(r, lam_ref[...])
        xc_ref[...] = xc
        r_ref[...] = r
        i_ref[...] = ig
        a_ref[...] = a
        b_ref[...] = jnp.sqrt(em) * (ig * xc)

    tile = pl.BlockSpec((tr, bw), lambda n, i: (i, n))
    vec = pl.BlockSpec((1, bw), lambda n, i: (0, n))
    wblk = pl.BlockSpec((None, bw, bw), lambda n, i: (n, 0, 0))
    sds = jax.ShapeDtypeStruct((S, D), F32)
    return hbm_call(
        body, name=name, grid=(nblk, S // tr),
        in_specs=[tile, pl.BlockSpec((tr, bw), lambda n, i: (jnp.maximum(i - 1, 0), n)),
                  pl.BlockSpec((CONV_WIDTH, bw), lambda n, i: (0, n)), vec, wblk, vec, wblk, vec, vec],
        out_specs=[tile] * 5, out_shape=[sds] * 5, compiler_params=_params(("parallel", "parallel")),
    )(proj, proj, conv_w, conv_b, w_rg, b_rg, w_ig, b_ig, lam)


SCAN_COLS = 256
CHUNK = SUBLANES
SCAN_UNROLL = 4


def rg_scan_fwd(proj, a, b, name):
    S, D = a.shape
    cb = min(SCAN_COLS, D)
    goff = D // cb

    def body(a_ref, b_ref, g_ref, hs_ref, y_ref):
        row = lax.broadcasted_iota(jnp.int32, (CHUNK, cb), 0)

        def step(c, carry):
            r0 = pl.multiple_of(c * CHUNK, CHUNK)
            A = a_ref[pl.ds(r0, CHUNK), :]
            B = b_ref[pl.ds(r0, CHUNK), :]
            for d in (1, 2, 4):
                As = jnp.where(row >= d, pltpu.roll(A, d, 0), 1.0)
                Bs = jnp.where(row >= d, pltpu.roll(B, d, 0), 0.0)
                B = A * Bs + B
                A = A * As
            hs_ref[pl.ds(r0, CHUNK), :] = B + A * carry
            a_end = jnp.sum(jnp.where(row == CHUNK - 1, A, 0.0), axis=0, keepdims=True)
            b_end = jnp.sum(jnp.where(row == CHUNK - 1, B, 0.0), axis=0, keepdims=True)
            return b_end + a_end * carry

        lax.fori_loop(0, S // CHUNK, step, jnp.zeros((1, cb), F32), unroll=SCAN_UNROLL)
        y_ref[...] = (hs_ref[...] * _gelu(g_ref[...])).astype(y_ref.dtype)

    col = pl.BlockSpec((S, cb), lambda j: (0, j))
    return hbm_call(
        body, name=name, grid=(D // cb,), in_specs=[col, col, pl.BlockSpec((S, cb), lambda j: (0, goff + j))],
        out_specs=[col, col], out_shape=[jax.ShapeDtypeStruct((S, D), F32), jax.ShapeDtypeStruct((S, D), MXU_DTYPE)],
        compiler_params=_params(("parallel",), _vmem_limit(5 * S * cb * 4, 4 * S * cb * 4)),
    )(a, b, proj)


def rg_scan_bwd(proj, dy, hs, a, name):
    S, D = a.shape
    cb = min(SCAN_COLS, D)
    goff = D // cb
    nchunks = S // CHUNK

    def body(g_ref, dy_ref, hs_ref, a_ref, dg_ref, gt_ref):
        gate, dy = g_ref[...], dy_ref[...]
        dg_ref[...] = (dy * hs_ref[...] * _gelu_grad(gate)).astype(dg_ref.dtype)
        gt_ref[...] = dy * _gelu(gate)
        row = lax.broadcasted_iota(jnp.int32, (CHUNK, cb), 0)

        def step(k, carry):
            c = nchunks - 1 - k
            r0 = pl.multiple_of(c * CHUNK, CHUNK)
            rn = pl.multiple_of(jnp.minimum(c + 1, nchunks - 1) * CHUNK, CHUNK)
            last = c == nchunks - 1
            nxt = jnp.where(last, 0.0, pltpu.roll(a_ref[pl.ds(rn, CHUNK), :], CHUNK - 1, 0))
            A = jnp.where(row < CHUNK - 1, pltpu.roll(a_ref[pl.ds(r0, CHUNK), :], CHUNK - 1, 0), nxt)
            B = gt_ref[pl.ds(r0, CHUNK), :]
            for d in (1, 2, 4):
                As = jnp.where(row < CHUNK - d, pltpu.roll(A, CHUNK - d, 0), 1.0)
                Bs = jnp.where(row < CHUNK - d, pltpu.roll(B, CHUNK - d, 0), 0.0)
                B = A * Bs + B
                A = A * As
            gt_ref[pl.ds(r0, CHUNK), :] = B + A * carry
            a_end = jnp.sum(jnp.where(row == 0, A, 0.0), axis=0, keepdims=True)
            b_end = jnp.sum(jnp.where(row == 0, B, 0.0), axis=0, keepdims=True)
            return b_end + a_end * carry

        lax.fori_loop(0, nchunks, step, jnp.zeros((1, cb), F32), unroll=SCAN_UNROLL)

    col = pl.BlockSpec((S, cb), lambda j: (0, j))
    return hbm_call(
        body, name=name, grid=(D // cb,), in_specs=[pl.BlockSpec((S, cb), lambda j: (0, goff + j)), col, col, col],
        out_specs=[col, col], out_shape=[jax.ShapeDtypeStruct((S, D), MXU_DTYPE), jax.ShapeDtypeStruct((S, D), F32)],
        compiler_params=_params(("parallel",), _vmem_limit(6 * S * cb * 4, 6 * S * cb * 4)),
    )(proj, dy, hs, a)


def rg_gates_bwd(gt, hs, xc, r, ig, w_rg, w_ig, lam, name):
    S, D = xc.shape
    nblk, bw, _ = w_rg.shape
    tr = min(RG_ROWS, S)

    def body(gt_ref, hs_ref, hp_ref, xc_ref, r_ref, i_ref, wr_ref, wi_ref, lam_ref,
             dxc_ref, dwr_ref, dwi_ref, dbr_ref, dbi_ref, dl_ref):
        step = pl.program_id(1)
        g, hs, xc, r, ig, lam = gt_ref[...], hs_ref[...], xc_ref[...], r_ref[...], i_ref[...], lam_ref[...]
        row = lax.broadcasted_iota(jnp.int32, g.shape, 0)
        hprev = _shift_down(hs, hp_ref[...], 1, row, step == 0)
        sp, _, a, em = _lru_coeffs(r, lam)
        mult = jnp.sqrt(em)
        du = g * mult
        dla = g * hprev * a - (g * (ig * xc)) * (a * a) / mult
        dpr = (dla * (-LRU_C * sp)) * (r * (1.0 - r))
        dpi = (du * xc) * (ig * (1.0 - ig))
        dprm, dpim = dpr.astype(MXU_DTYPE), dpi.astype(MXU_DTYPE)
        nt = (((1,), (1,)), ((), ()))
        dxc_ref[...] = (du * ig + lax.dot_general(dprm, wr_ref[...].astype(MXU_DTYPE), nt, preferred_element_type=F32)
                        + lax.dot_general(dpim, wi_ref[...].astype(MXU_DTYPE), nt, preferred_element_type=F32))

        @pl.when(step == 0)
        def _():
            for ref in (dwr_ref, dwi_ref, dbr_ref, dbi_ref, dl_ref):
                ref[...] = jnp.zeros_like(ref)

        xct = xc.T.astype(MXU_DTYPE)
        dwr_ref[...] += jnp.dot(xct, dprm, preferred_element_type=F32)
        dwi_ref[...] += jnp.dot(xct, dpim, preferred_element_type=F32)
        dbr_ref[...] += jnp.sum(dpr, axis=0, keepdims=True)
        dbi_ref[...] += jnp.sum(dpi, axis=0, keepdims=True)
        dl_ref[...] += jnp.sum(dla * (-LRU_C * r), axis=0, keepdims=True) * (-_sigmoid(-lam))

    tile = pl.BlockSpec((tr, bw), lambda n, i: (i, n))
    vec = pl.BlockSpec((1, bw), lambda n, i: (0, n))
    wblk = pl.BlockSpec((None, bw, bw), lambda n, i: (n, 0, 0))
    return hbm_call(
        body, name=name, grid=(nblk, S // tr),
        in_specs=[tile, tile, pl.BlockSpec((tr, bw), lambda n, i: (jnp.maximum(i - 1, 0), n)), tile, tile, tile, wblk, wblk, vec],
        out_specs=[tile, wblk, wblk, vec, vec, vec],
        out_shape=[jax.ShapeDtypeStruct((S, D), F32), jax.ShapeDtypeStruct((nblk, bw, bw), F32), jax.ShapeDtypeStruct((nblk, bw, bw), F32),
                   jax.ShapeDtypeStruct((1, D), F32), jax.ShapeDtypeStruct((1, D), F32), jax.ShapeDtypeStruct((1, D), F32)],
        compiler_params=_params(("parallel", "arbitrary")),
    )(gt, hs, hs, xc, r, ig, w_rg, w_ig, lam)


def rg_conv_bwd(proj, dxc, conv_w, name):
    S, D = dxc.shape
    bw = min(SCAN_COLS, D)
    tr = min(RG_ROWS, S)
    nsteps = S // tr

    def body(d_ref, dn_ref, xr_ref, xp_ref, cw_ref, dxr_ref, dcw_ref, dcb_ref):
        step = pl.program_id(1)
        d, xr = d_ref[...], xr_ref[...]
        row = lax.broadcasted_iota(jnp.int32, d.shape, 0)
        dxr = d * cw_ref[CONV_WIDTH - 1:CONV_WIDTH, :]
        for k in range(CONV_WIDTH - 1):
            dxr = dxr + _shift_up(d, dn_ref[...], CONV_WIDTH - 1 - k, row, step == nsteps - 1, tr) * cw_ref[k:k + 1, :]
        dxr_ref[...] = dxr.astype(dxr_ref.dtype)

        @pl.when(step == 0)
        def _():
            dcw_ref[...] = jnp.zeros_like(dcw_ref)
            dcb_ref[...] = jnp.zeros_like(dcb_ref)

        for k in range(CONV_WIDTH - 1):
            xs = _shift_down(xr, xp_ref[...], CONV_WIDTH - 1 - k, row, step == 0)
            dcw_ref[k:k + 1, :] += jnp.sum(d * xs, axis=0, keepdims=True)
        dcw_ref[CONV_WIDTH - 1:CONV_WIDTH, :] += jnp.sum(d * xr, axis=0, keepdims=True)
        dcb_ref[...] += jnp.sum(d, axis=0, keepdims=True)

    tile = pl.BlockSpec((tr, bw), lambda n, i: (i, n))
    cwb = pl.BlockSpec((CONV_WIDTH, bw), lambda n, i: (0, n))
    return hbm_call(
        body, name=name, grid=(D // bw, nsteps),
        in_specs=[tile, pl.BlockSpec((tr, bw), lambda n, i: (jnp.minimum(i + 1, nsteps - 1), n)), tile,
                  pl.BlockSpec((tr, bw), lambda n, i: (jnp.maximum(i - 1, 0), n)), cwb],
        out_specs=[tile, cwb, pl.BlockSpec((1, bw), lambda n, i: (0, n))],
        out_shape=[jax.ShapeDtypeStruct((S, D), MXU_DTYPE), jax.ShapeDtypeStruct((CONV_WIDTH, D), F32), jax.ShapeDtypeStruct((1, D), F32)],
        compiler_params=_params(("parallel", "arbitrary")),
    )(dxc, dxc, proj, proj, conv_w)


def rope_table(S):
    half = ROT_DIM // 2
    pos = jnp.arange(S, dtype=F32)
    inv = ROPE_THETA ** (-jnp.arange(0, ROT_DIM, 2, dtype=F32) / ROT_DIM)
    ang = pos[:, None] * inv[None, :]
    cos, sin = jnp.cos(ang), jnp.sin(ang)
    zero = jnp.zeros((S, HEAD_DIM - ROT_DIM), F32)
    c = jnp.concatenate([cos, cos, zero + 1.0], axis=1)
    a = jnp.concatenate([-sin, jnp.zeros((S, half), F32), zero], axis=1)
    b = jnp.concatenate([jnp.zeros((S, half), F32), sin, zero], axis=1)
    return jnp.stack([jnp.tile(t, (1, LANES // HEAD_DIM)) for t in (c, a, b)])


def _rope(t, tab):
    half = ROT_DIM // 2
    return t * tab[0] + pltpu.roll(t, LANES - half, 1) * tab[1] + pltpu.roll(t, half, 1) * tab[2]


def _rope_t(d, tab):
    half = ROT_DIM // 2
    return d * tab[0] + pltpu.roll(d * tab[1], half, 1) + pltpu.roll(d * tab[2], LANES - half, 1)


def _dup_head(t, hk, lo):
    sw = pltpu.roll(t, HEAD_DIM, 1)
    return jnp.where(lo, t, sw) if hk == 0 else jnp.where(lo, sw, t)


def _attn_common(n, sink_ref, q_ref, kp_ref, kc_ref, vp_ref, vc_ref, tc_ref, tp_ref, hk, pairs):
    tq = (tc_ref[0], tc_ref[1], tc_ref[2])
    tp = (tp_ref[0], tp_ref[1], tp_ref[2])
    lo = lax.broadcasted_iota(jnp.int32, (WINDOW, LANES), 1) < HEAD_DIM
    lo2 = lax.broadcasted_iota(jnp.int32, (2 * WINDOW, LANES), 1) < HEAD_DIM
    kband = jnp.concatenate([_rope(kp_ref[...], tp), _rope(kc_ref[...], tq)], axis=0)
    vband = jnp.concatenate([vp_ref[...], vc_ref[...]], axis=0)
    kd = _dup_head(kband, hk, lo2).astype(MXU_DTYPE)
    vd = _dup_head(vband, hk, lo2).astype(MXU_DTYPE)
    rows, sks = [], []
    for j in range(pairs):
        col = hk * pairs + j
        qp = _rope(q_ref[:, col * LANES:(col + 1) * LANES], tq)
        rows += [jnp.where(lo, qp, 0.0), jnp.where(lo, 0.0, qp)]
        sks += [jnp.full((WINDOW, 1), sink_ref[2 * col], F32), jnp.full((WINDOW, 1), sink_ref[2 * col + 1], F32)]
    qg = jnp.concatenate(rows, axis=0)
    sk = jnp.concatenate(sks, axis=0)
    G = 2 * pairs * WINDOW
    ri = lax.broadcasted_iota(jnp.int32, (G, 2 * WINDOW), 0) & (WINDOW - 1)
    kj = lax.broadcasted_iota(jnp.int32, (G, 2 * WINDOW), 1) - WINDOW
    valid = (kj <= ri) & (kj > ri - WINDOW) & (kj + n * WINDOW >= 0)
    s = lax.dot_general(qg.astype(MXU_DTYPE), kd, (((1,), (1,)), ((), ())), preferred_element_type=F32) * (HEAD_DIM ** -0.5)
    s = jnp.where(valid, s, NEG_INF)
    m = jnp.maximum(jnp.max(s, axis=1, keepdims=True), sk)
    e = jnp.exp(s - m)
    es = jnp.exp(sk - m)
    inv = 1.0 / (jnp.sum(e, axis=1, keepdims=True) + es)
    return qg, kd, vd, e * inv, es * inv, lo, lo2, tq, tp


def _attn_specs(D, NB):
    kcol = 3 * D // LANES
    q = pl.BlockSpec((WINDOW, D), lambda n: (n, 2))
    kc = pl.BlockSpec((WINDOW, LANES), lambda n: (n, kcol))
    kp = pl.BlockSpec((WINDOW, LANES), lambda n: (jnp.maximum(n - 1, 0), kcol))
    vc = pl.BlockSpec((WINDOW, LANES), lambda n: (n, kcol + 1))
    vp = pl.BlockSpec((WINDOW, LANES), lambda n: (jnp.maximum(n - 1, 0), kcol + 1))
    tc = pl.BlockSpec((3, WINDOW, LANES), lambda n: (0, n, 0))
    tp = pl.BlockSpec((3, WINDOW, LANES), lambda n: (0, jnp.maximum(n - 1, 0), 0))
    sink = pl.BlockSpec(memory_space=pltpu.SMEM)
    return [sink, q, kp, kc, vp, vc, tc, tp]


def attn_fwd(proj, sinks, tab, D, name):
    S = proj.shape[0]
    NB = S // WINDOW
    pairs = D // HEAD_DIM // N_KV_HEADS // 2

    def body(sink_ref, q_ref, kp_ref, kc_ref, vp_ref, vc_ref, tc_ref, tp_ref, o_ref):
        n = pl.program_id(0)
        for hk in range(N_KV_HEADS):
            _, _, vd, p, _, lo, _, _, _ = _attn_common(n, sink_ref, q_ref, kp_ref, kc_ref, vp_ref, vc_ref, tc_ref, tp_ref, hk, pairs)
            o = jnp.dot(p.astype(MXU_DTYPE), vd, preferred_element_type=F32)
            for j in range(pairs):
                col = hk * pairs + j
                oa = o[(2 * j) * WINDOW:(2 * j + 1) * WINDOW]
                ob = o[(2 * j + 1) * WINDOW:(2 * j + 2) * WINDOW]
                o_ref[:, col * LANES:(col + 1) * LANES] = jnp.where(lo, oa, ob)

    return hbm_call(
        body, name=name, grid=(NB,), in_specs=_attn_specs(D, NB),
        out_specs=pl.BlockSpec((WINDOW, D), lambda n: (n, 0)), out_shape=jax.ShapeDtypeStruct((S, D), F32),
        compiler_params=_params(("parallel",)),
    )(sinks, proj, proj, proj, proj, proj, tab, tab)


def attn_bwd(proj, sinks, tab, o, do, D, name):
    S = proj.shape[0]
    NB = S // WINDOW
    pairs = D // HEAD_DIM // N_KV_HEADS // 2

    def body(sink_ref, q_ref, kp_ref, kc_ref, vp_ref, vc_ref, tc_ref, tp_ref, o_ref, do_ref, dq_ref, dk_ref, dv_ref, ds_ref):
        n = pl.program_id(0)

        @pl.when(n == 0)
        def _():
            ds_ref[...] = jnp.zeros_like(ds_ref)

        lane1 = lax.broadcasted_iota(jnp.int32, (1, LANES), 1)
        dsink = jnp.zeros((1, LANES), F32)
        dkt = dvt = None
        for hk in range(N_KV_HEADS):
            qg, kd, vd, p, ps, lo, lo2, tq, tp = _attn_common(n, sink_ref, q_ref, kp_ref, kc_ref, vp_ref, vc_ref, tc_ref, tp_ref, hk, pairs)
            dos, os_ = [], []
            for j in range(pairs):
                col = hk * pairs + j
                dop = do_ref[:, col * LANES:(col + 1) * LANES]
                op = o_ref[:, col * LANES:(col + 1) * LANES]
                dos += [jnp.where(lo, dop, 0.0), jnp.where(lo, 0.0, dop)]
                os_ += [jnp.where(lo, op, 0.0), jnp.where(lo, 0.0, op)]
            dog = jnp.concatenate(dos, axis=0)
            og = jnp.concatenate(os_, axis=0)
            dogm = dog.astype(MXU_DTYPE)
            dp = lax.dot_general(dogm, vd, (((1,), (1,)), ((), ())), preferred_element_type=F32)
            dr = jnp.sum(dog * og, axis=1, keepdims=True)
            ds = p * (dp - dr) * (HEAD_DIM ** -0.5)
            dsm = ds.astype(MXU_DTYPE)
            dqg = jnp.dot(dsm, kd, preferred_element_type=F32)
            dkd = jnp.dot(ds.T.astype(MXU_DTYPE), qg.astype(MXU_DTYPE), preferred_element_type=F32)
            dvd = jnp.dot(p.T.astype(MXU_DTYPE), dogm, preferred_element_type=F32)
            dkf = dkd + pltpu.roll(dkd, HEAD_DIM, 1)
            dvf = dvd + pltpu.roll(dvd, HEAD_DIM, 1)
            if hk == 0:
                dkt, dvt = dkf, dvf
            else:
                dkt, dvt = jnp.where(lo2, dkt, dkf), jnp.where(lo2, dvt, dvf)
            sd = ps * dr
            for j in range(pairs):
                col = hk * pairs + j
                dqa = dqg[(2 * j) * WINDOW:(2 * j + 1) * WINDOW]
                dqb = dqg[(2 * j + 1) * WINDOW:(2 * j + 2) * WINDOW]
                dq_ref[:, col * LANES:(col + 1) * LANES] = _rope_t(jnp.where(lo, dqa, dqb), tq).astype(dq_ref.dtype)
                for t in range(2):
                    part = sd[(2 * j + t) * WINDOW:(2 * j + t + 1) * WINDOW]
                    val = jnp.sum(part, axis=0, keepdims=True)
                    dsink = dsink - jnp.where(lane1 == 2 * col + t, val, 0.0)
        dk_ref[...] = jnp.concatenate([_rope_t(dkt[:WINDOW], tp), _rope_t(dkt[WINDOW:], tq)], axis=0)
        dv_ref[...] = dvt
        ds_ref[...] += dsink

    blk = pl.BlockSpec((WINDOW, D), lambda n: (n, 0))
    band = pl.BlockSpec((None, 2 * WINDOW, LANES), lambda n: (n, 0, 0))
    return hbm_call(
        body, name=name, grid=(NB,), in_specs=_attn_specs(D, NB) + [blk, blk],
        out_specs=[blk, band, band, pl.BlockSpec((1, LANES), lambda n: (0, 0))],
        out_shape=[jax.ShapeDtypeStruct((S, D), MXU_DTYPE), jax.ShapeDtypeStruct((NB, 2 * WINDOW, LANES), F32),
                   jax.ShapeDtypeStruct((NB, 2 * WINDOW, LANES), F32), jax.ShapeDtypeStruct((1, LANES), F32)],
        compiler_params=_params(("arbitrary",)),
    )(sinks, proj, proj, proj, proj, proj, tab, tab, o, do)


def band_fold(dkb, dvb, name):
    NB = dkb.shape[0]
    k4 = dkb.reshape(NB, 2, WINDOW, LANES)
    v4 = dvb.reshape(NB, 2, WINDOW, LANES)

    def body(kc_ref, kn_ref, vc_ref, vn_ref, dk_ref, dv_ref):
        more = pl.program_id(0) < NB - 1
        dk_ref[...] = (kc_ref[...] + jnp.where(more, kn_ref[...], 0.0)).astype(dk_ref.dtype)
        dv_ref[...] = (vc_ref[...] + jnp.where(more, vn_ref[...], 0.0)).astype(dv_ref.dtype)

    cur = pl.BlockSpec((None, None, WINDOW, LANES), lambda n: (n, 1, 0, 0))
    nxt = pl.BlockSpec((None, None, WINDOW, LANES), lambda n: (jnp.minimum(n + 1, NB - 1), 0, 0, 0))
    out = pl.BlockSpec((WINDOW, LANES), lambda n: (n, 0))
    sds = jax.ShapeDtypeStruct((NB * WINDOW, LANES), MXU_DTYPE)
    return hbm_call(body, name=name, grid=(NB,), in_specs=[cur, nxt, cur, nxt], out_specs=[out, out], out_shape=[sds, sds],
                          compiler_params=_params(("parallel",)))(k4, k4, v4, v4)


CROSS_ROWS = 512


def _cross_probs(q, k, scale):
    s = lax.dot_general(q.astype(MXU_DTYPE), k.astype(MXU_DTYPE), (((1,), (1,)), ((), ())), preferred_element_type=F32) * scale
    e = jnp.exp(s - jnp.max(s, axis=1, keepdims=True))
    return e / jnp.sum(e, axis=1, keepdims=True)


def cross_fwd(qc, kv, name):
    S, D = qc.shape
    M = kv.shape[0]
    hd = D // CROSS_HEADS
    tq = min(CROSS_ROWS, S)

    def body(q_ref, kv_ref, o_ref):
        for h in range(CROSS_HEADS):
            p = _cross_probs(q_ref[:, h * hd:(h + 1) * hd], kv_ref[:, h * hd:(h + 1) * hd], hd ** -0.5)
            v = kv_ref[:, D + h * hd:D + (h + 1) * hd].astype(MXU_DTYPE)
            o_ref[:, h * hd:(h + 1) * hd] = jnp.dot(p.astype(MXU_DTYPE), v, preferred_element_type=F32).astype(o_ref.dtype)

    return hbm_call(
        body, name=name, grid=(S // tq,), in_specs=[pl.BlockSpec((tq, D), lambda i: (i, 0)), pl.BlockSpec((M, 2 * D), lambda i: (0, 0))],
        out_specs=pl.BlockSpec((tq, D), lambda i: (i, 0)), out_shape=jax.ShapeDtypeStruct((S, D), MXU_DTYPE),
        compiler_params=_params(("parallel",)),
    )(qc, kv)


def cross_bwd(qc, kv, do, name):
    S, D = qc.shape
    M = kv.shape[0]
    hd = D // CROSS_HEADS
    tq = min(CROSS_ROWS, S)

    def body(q_ref, kv_ref, do_ref, dq_ref, dkv_ref):
        @pl.when(pl.program_id(0) == 0)
        def _():
            dkv_ref[...] = jnp.zeros_like(dkv_ref)

        for h in range(CROSS_HEADS):
            q = q_ref[:, h * hd:(h + 1) * hd]
            k = kv_ref[:, h * hd:(h + 1) * hd]
            v = kv_ref[:, D + h * hd:D + (h + 1) * hd].astype(MXU_DTYPE)
            dom = do_ref[:, h * hd:(h + 1) * hd].astype(MXU_DTYPE)
            p = _cross_probs(q, k, hd ** -0.5)
            dp = lax.dot_general(dom, v, (((1,), (1,)), ((), ())), preferred_element_type=F32)
            ds = p * (dp - jnp.sum(p * dp, axis=1, keepdims=True)) * (hd ** -0.5)
            dq_ref[:, h * hd:(h + 1) * hd] = jnp.dot(ds.astype(MXU_DTYPE), k.astype(MXU_DTYPE),
                                                     preferred_element_type=F32).astype(dq_ref.dtype)
            dkv_ref[:, h * hd:(h + 1) * hd] += jnp.dot(ds.T.astype(MXU_DTYPE), q.astype(MXU_DTYPE), preferred_element_type=F32)
            dkv_ref[:, D + h * hd:D + (h + 1) * hd] += jnp.dot(p.T.astype(MXU_DTYPE), dom, preferred_element_type=F32)

    row = pl.BlockSpec((tq, D), lambda i: (i, 0))
    full = pl.BlockSpec((M, 2 * D), lambda i: (0, 0))
    return hbm_call(
        body, name=name, grid=(S // tq,), in_specs=[row, full, row], out_specs=[row, full],
        out_shape=[jax.ShapeDtypeStruct((S, D), MXU_DTYPE), jax.ShapeDtypeStruct((M, 2 * D), F32)],
        compiler_params=_params(("arbitrary",)),
    )(qc, kv, do)


def adamw(w, g, m, v, name):
    shape = w.shape
    cols = shape[-1]
    rows = int(np.prod(shape[:-1]))
    w2, g2, m2, v2 = (t.reshape(rows, cols) for t in (w, g, m, v))
    tr = _divisors(rows, SUBLANES, max(SUBLANES, (1 << 20) // (cols * 4) // SUBLANES * SUBLANES))[0]

    def body(w_ref, g_ref, m_ref, v_ref, d_ref, mo_ref, vo_ref, go_ref):
        gg = g_ref[...]
        mn = ADAM_B1 * m_ref[...] + (1.0 - ADAM_B1) * gg
        vn = ADAM_B2 * v_ref[...] + (1.0 - ADAM_B2) * (gg * gg)
        m_hat = mn / (1.0 - ADAM_B1 ** ADAM_STEP)
        v_hat = vn / (1.0 - ADAM_B2 ** ADAM_STEP)
        d_ref[...] = -ADAM_LR * (m_hat / (jnp.sqrt(v_hat) + ADAM_EPS) + ADAM_WD * w_ref[...])
        mo_ref[...] = mn
        vo_ref[...] = vn
        go_ref[...] = gg

    blk = pl.BlockSpec((tr, cols), lambda i: (i, 0))
    sds = jax.ShapeDtypeStruct((rows, cols), F32)
    d, mn, vn, go = hbm_call(body, name=name, grid=(rows // tr,), in_specs=[blk] * 4, out_specs=[blk] * 4, out_shape=[sds] * 4,
                                   compiler_params=_params(("parallel",)))(w2, g2, m2, v2)
    return d.reshape(shape), mn.reshape(shape), vn.reshape(shape), go.reshape(shape)


def sum_devices(parts, name):
    n, rows, cols = parts.shape

    def body(p_ref, o_ref):
        acc = p_ref[0]
        for k in range(1, n):
            acc = acc + p_ref[k]
        o_ref[...] = acc

    return pl.pallas_call(body, name=name, in_specs=[pl.BlockSpec(memory_space=pltpu.VMEM)],
                          out_specs=pl.BlockSpec(memory_space=pltpu.VMEM), out_shape=jax.ShapeDtypeStruct((rows, cols), F32))(parts)


HBM_SPEC = pl.BlockSpec(memory_space=pltpu.HBM)


def _place():
    return lax.axis_index("x"), lax.axis_index("y"), lax.axis_index("c")


def _remote(src, dst, send_sems, recv_sems, k, to):
    return pltpu.make_async_remote_copy(src_ref=src, dst_ref=dst, send_sem=send_sems.at[k], recv_sem=recv_sems.at[k],
                                        device_id=to, device_id_type=MESH_ID)


SEM_SPEC = pl.BlockSpec(memory_space=pltpu.SEMAPHORE)
ANY_SPEC = pl.BlockSpec(memory_space=pl.ANY)
SPLIT_COPY = pltpu.CompilerParams(has_side_effects=pltpu.SideEffectType.DATAFLOW_SIDE_EFFECTING)


def _in_hbm(arrays):
    return [pltpu.with_memory_space_constraint(a, pltpu.HBM) for a in arrays]


def _split_start(copies, sources, lands, after, n_sems, name):
    n = len(sources)

    def body(*refs):
        for cp in copies(refs[:n], refs[n:2 * n], refs[2 * n + 1], refs[2 * n + 2]):
            cp.start()
        refs[-1][...] = jnp.zeros_like(refs[-1])

    through = [pltpu.HBM(a.shape, a.dtype) for a in list(sources) + list(lands)]
    outs = pl.pallas_call(
        body, name=name, in_specs=[HBM_SPEC] * (2 * n) + [ANY_SPEC],
        out_specs=[SEM_SPEC, SEM_SPEC] + [HBM_SPEC] * (2 * n) + [pl.BlockSpec(memory_space=pltpu.VMEM)],
        out_shape=[pltpu.SemaphoreType.DMA((n_sems,)), pltpu.SemaphoreType.DMA((n_sems,))] + through
        + [jax.ShapeDtypeStruct((SUBLANES, LANES), F32)],
        input_output_aliases={i: 2 + i for i in range(2 * n)}, compiler_params=SPLIT_COPY,
    )(*_in_hbm(sources), *_in_hbm(lands), after)
    return outs[0], outs[1], outs[2:2 + n], outs[2 + n:2 + 2 * n], outs[-1]


def _split_wait(copies, send_sems, recv_sems, sources, lands, after, name):
    n = len(sources)

    def body(*refs):
        for cp in copies(refs[:n], refs[n:2 * n], refs[2 * n], refs[2 * n + 1]):
            cp.wait_send()
            cp.wait_recv()

    through = [pltpu.HBM(a.shape, a.dtype) for a in list(sources) + list(lands)]
    outs = pl.pallas_call(
        body, name=name, in_specs=[HBM_SPEC] * (2 * n) + [SEM_SPEC, SEM_SPEC, ANY_SPEC], out_specs=[HBM_SPEC] * (2 * n),
        out_shape=through, input_output_aliases={i: i for i in range(2 * n)}, compiler_params=SPLIT_COPY,
    )(*sources, *lands, send_sems, recv_sems, after)
    return outs[:n], outs[n:]


def _chip_slab(land, slot, rows):
    return land.at[slot, rows] if len(land.shape) == 3 else land.at[rows, slot]


def _gather_copies(w_refs, land_refs, send_sems, recv_sems):
    n = len(w_refs)
    x, y, c = _place()
    chips = [(1 - x, y), (x, 1 - y), (1 - x, 1 - y)]
    cps = []
    for a in range(n):
        hr = w_refs[a].shape[0] // 2
        mine, every = pl.ds(c * hr, hr), pl.ds(0, 2 * hr)
        cps.append(_remote(w_refs[a], _chip_slab(land_refs[a], 2 * x + y, every), send_sems, recv_sems, 3 * n + a, (x, y, 1 - c)))
        for k, chip in enumerate(chips):
            cps.append(_remote(w_refs[a].at[mine], _chip_slab(land_refs[a], 2 * x + y, mine), send_sems, recv_sems, 3 * a + k, (*chip, c)))
    return cps


def gather_start(shards, after, name):
    lands = [lax.empty(s.shape[:-2] + (N_CHIPS,) + s.shape[-2:], s.dtype) for s in shards]
    return _split_start(_gather_copies, shards, lands, after, 4 * len(shards), name)


def gather_wait(state, after, name):
    send_sems, recv_sems, sources, lands, _ = state
    return _split_wait(_gather_copies, send_sems, recv_sems, sources, lands, after, name)[1]


def gather_pass(lands, name):
    n = len(lands)

    def body(*refs):
        out_refs, send_sems, recv_sems = refs[n:2 * n], refs[2 * n], refs[2 * n + 1]
        x, y, c = _place()
        chips = [(1 - x, y), (x, 1 - y), (1 - x, 1 - y)]
        sent = []
        for a in range(n):
            hr = out_refs[a].shape[0 if len(out_refs[a].shape) == 4 else 1] // 2
            for k, (px, py) in enumerate(chips):
                landed = _chip_slab(out_refs[a], 2 * px + py, pl.ds(c * hr, hr))
                sent.append(_remote(landed, landed, send_sems, recv_sems, 3 * a + k, (x, y, 1 - c)))
        for cp in sent:
            cp.start()
        for a in range(n):
            hr = out_refs[a].shape[0 if len(out_refs[a].shape) == 4 else 1] // 2
            for k, (px, py) in enumerate(chips):
                theirs = _chip_slab(out_refs[a], 2 * px + py, pl.ds((1 - c) * hr, hr))
                _remote(theirs, theirs, send_sems, recv_sems, 3 * a + k, (x, y, 1 - c)).wait_recv()
        for cp in sent:
            cp.wait_send()

    return hbm_call(
        body, name=name, in_specs=[HBM_SPEC] * n, out_specs=[HBM_SPEC] * n,
        out_shape=[jax.ShapeDtypeStruct(a.shape, a.dtype) for a in lands], input_output_aliases={a: a for a in range(n)},
        scratch_shapes=[pltpu.SemaphoreType.DMA((3 * n,))] * 2,
    )(*lands)


def _scatter_copies(t_refs, land_refs, send_sems, recv_sems):
    x, y, c = _place()
    chips = [(1 - x, y), (x, 1 - y), (1 - x, 1 - y)]
    return [_remote(t_refs[a].at[:, 2 * px + py], land_refs[a].at[:, k], send_sems, recv_sems, 3 * a + k, (px, py, c))
            for a in range(len(t_refs)) for k, (px, py) in enumerate(chips)]


def scatter_start(parts, after, name):
    lands = [lax.empty((t.shape[0], N_CHIPS - 1) + t.shape[2:], t.dtype) for t in parts]
    return _split_start(_scatter_copies, parts, lands, after, 3 * len(parts), name)


def scatter_wait(state, after, name):
    send_sems, recv_sems, sources, lands, _ = state
    return _split_wait(_scatter_copies, send_sems, recv_sems, sources, lands, after, name)


def swap_sibling(parts, name):
    n = len(parts)

    def body(*refs):
        v_refs, out_refs, send_sems, recv_sems = refs[:n], refs[n:2 * n], refs[2 * n], refs[2 * n + 1]
        x, y, c = _place()
        cps = []
        for a in range(n):
            hr = v_refs[a].shape[2] // 2
            cps.append(_remote(v_refs[a].at[:, :, pl.ds((1 - c) * hr, hr)], out_refs[a], send_sems, recv_sems, a, (x, y, 1 - c)))
        for cp in cps:
            cp.start()
        for cp in cps:
            cp.wait()

    return hbm_call(
        body, name=name, in_specs=[HBM_SPEC] * n, out_specs=[HBM_SPEC] * n,
        out_shape=[jax.ShapeDtypeStruct(v.shape[:2] + (v.shape[2] // 2, v.shape[3]), v.dtype) for v in parts],
        scratch_shapes=[pltpu.SemaphoreType.DMA((n,))] * 2,
    )(*parts)


def join_halves(halves, layer, name):
    n = len(halves)

    def body(*refs):
        out_refs, send_sems, recv_sems = refs[n:2 * n], refs[2 * n], refs[2 * n + 1]
        x, y, c = _place()
        cps = []
        for a in range(n):
            hr = out_refs[a].shape[1] // 2
            mine = out_refs[a].at[layer, pl.ds(c * hr, hr)]
            cps.append(_remote(mine, mine, send_sems, recv_sems, a, (x, y, 1 - c)))
        for cp in cps:
            cp.start()
        for a in range(n):
            hr = out_refs[a].shape[1] // 2
            theirs = out_refs[a].at[layer, pl.ds((1 - c) * hr, hr)]
            _remote(theirs, theirs, send_sems, recv_sems, a, (x, y, 1 - c)).wait_recv()
        for cp in cps:
            cp.wait_send()

    return hbm_call(
        body, name=name, in_specs=[HBM_SPEC] * n, out_specs=[HBM_SPEC] * n,
        out_shape=[jax.ShapeDtypeStruct(f.shape, f.dtype) for f in halves], input_output_aliases={a: a for a in range(n)},
        scratch_shapes=[pltpu.SemaphoreType.DMA((n,))] * 2,
    )(*halves)


def gather_devices(v, name):
    def body(v_ref, out_ref, send_sems, recv_sems, local_sem):
        x, y, c = _place()
        me = 4 * x + 2 * y + c
        own = pltpu.make_async_copy(v_ref, out_ref.at[me], local_sem)
        own.start()
        peers = [((x + dx) % 2, (y + dy) % 2, (c + dc) % 2) for dx in (0, 1) for dy in (0, 1) for dc in (0, 1)][1:]
        sent = []
        for k, peer in enumerate(peers):
            cp = pltpu.make_async_remote_copy(src_ref=v_ref, dst_ref=out_ref.at[me], send_sem=send_sems.at[k], recv_sem=recv_sems.at[k],
                                              device_id=peer, device_id_type=MESH_ID)
            cp.start()
            sent.append(cp)
        for k, (px, py, pc) in enumerate(peers):
            slot = out_ref.at[4 * px + 2 * py + pc]
            pltpu.make_async_remote_copy(src_ref=slot, dst_ref=slot, send_sem=send_sems.at[k], recv_sem=recv_sems.at[k],
                                         device_id=(px, py, pc), device_id_type=MESH_ID).wait_recv()
        for cp in sent:
            cp.wait_send()
        own.wait()

    vm = pl.BlockSpec(memory_space=pltpu.VMEM)
    return pl.pallas_call(body, name=name, in_specs=[vm], out_specs=vm, out_shape=jax.ShapeDtypeStruct((N_DEV,) + v.shape, v.dtype),
                          scratch_shapes=[pltpu.SemaphoreType.DMA((N_DEV - 1,)), pltpu.SemaphoreType.DMA((N_DEV - 1,)),
                                          pltpu.SemaphoreType.DMA])(v)


ADD_ROWS = 512


def add_pair(place, a, b, name):
    L, n, hr, cols = b.shape
    tr = _divisors(hr, 2 * SUBLANES, ADD_ROWS)[0]
    nb = hr // tr

    def body(p_ref, a_ref, b_ref, o_ref):
        del p_ref
        o_ref[...] = (a_ref[...].astype(F32) + b_ref[...].astype(F32)).astype(o_ref.dtype)

    blk = pl.BlockSpec((None, None, tr, cols), lambda l, d, i, p: (l, d, i, 0))
    grid_spec = pltpu.PrefetchScalarGridSpec(
        num_scalar_prefetch=1, grid=(L, n, nb),
        in_specs=[pl.BlockSpec((None, None, tr, cols), lambda l, d, i, p: (l, d, p[0] * nb + i, 0)), blk], out_specs=blk)
    return hbm_call(body, name=name, grid_spec=grid_spec, out_shape=jax.ShapeDtypeStruct(b.shape, b.dtype),
                          compiler_params=_params(("parallel", "parallel", "parallel")))(place, a, b)


def add_chips(place, own, others, layer, stacked, name):
    _, n, hr, cols = others.shape
    tr = _divisors(hr, 2 * SUBLANES, ADD_ROWS)[0]
    nb = hr // tr
    create = isinstance(stacked, tuple)

    def body(p_ref, own_ref, *refs):
        del p_ref
        acc = own_ref[...].astype(F32)
        for k in range(n):
            acc = acc + refs[k][...].astype(F32)
        refs[-1][...] = acc

    ins = [pl.BlockSpec((None, None, tr, cols), lambda i, p: (0, p[1], i, 0))]
    ins += [pl.BlockSpec((None, None, tr, cols), functools.partial(lambda k, i, p: (0, k, i, 0), k)) for k in range(n)]
    grid_spec = pltpu.PrefetchScalarGridSpec(num_scalar_prefetch=1, grid=(nb,), in_specs=ins + ([] if create else [ANY_SPEC]),
                                             out_specs=pl.BlockSpec((None, tr, cols), lambda i, p: (layer, p[0] * nb + i, 0)))
    shape = stacked if create else stacked.shape
    return hbm_call(body, name=name, grid_spec=grid_spec, out_shape=jax.ShapeDtypeStruct(shape, F32),
                          input_output_aliases={} if create else {n + 2: 0},
                          compiler_params=_params(("parallel",)))(place, own, *([others] * n), *([] if create else [stacked]))


def _alpha(depth):
    return (2 * depth) ** 0.25


def _wmm(a, weight, mode, name, deps=(), **more):
    arr, how = weight
    return mm(a, arr, mode, name, deps=deps, **how, **more)


def layer_fwd(h, mem, w, tab, alpha, deps=(), late=None):
    D = h.shape[1]
    proj = _wmm(h, w["w_in"], "nn", "mm_proj", deps)
    xc, r, ig, a, b = rg_gates_fwd(proj, w["conv_w"], w["conv_b"], w["w_rg"], w["b_rg"], w["w_ig"], w["b_ig"], w["lru_lambda"], "rg_gates_fwd")
    hs, y_rnn = rg_scan_fwd(proj, a, b, "rg_scan_fwd")
    y_attn = attn_fwd(proj, w["sinks"], tab, D, "attn_fwd")
    deps = ()
    if late is not None:
        rest, deps = late(y_attn)
        w = {**w, **rest}
    pr = _wmm(y_rnn, w["w_br_rnn"], "nn", "mm_br_rnn", deps)
    pa = _wmm(y_attn, w["w_br_attn"], "nn", "mm_br_attn")
    merged = merge_fwd(proj, pr, pa, "merge_fwd")
    h1, xh1, rs1 = _wmm(merged, w["w_out"], "nn", "mm_out_ln1", post_norm=(h, w["ln1_g"], w["ln1_b"], alpha))
    qc = _wmm(h1, w["cq_w"], "nn", "mm_cq", out_dtype=MXU_DTYPE)
    kv = _wmm(mem, w["ckv_w"], "nn", "mm_ckv", out_dtype=MXU_DTYPE)
    o = cross_fwd(qc, kv, "cross_fwd")
    h2, xh2, rs2 = _wmm(o, w["co_w"], "nn", "mm_co_ln2", post_norm=(h1, w["ln2_g"], w["ln2_b"], alpha))
    gu = _wmm(h2, w["ffn_wi"], "nn", "mm_ffn_wi", out_blocks=2)
    act = swiglu_fwd(gu, "swiglu_fwd")
    h3, xh3, rs3 = _wmm(act, w["ffn_wo"], "nn", "mm_ffn_wo_ln3", post_norm=(h2, w["ln3_g"], w["ln3_b"], alpha))
    saved = dict(h=h, proj=proj, xc=xc, r=r, ig=ig, a=a, hs=hs, y_rnn=y_rnn, y_attn=y_attn, pr=pr, pa=pa, xh1=xh1, rs1=rs1, h1=h1,
                 qc=qc, kv=kv, o=o, xh2=xh2, rs2=rs2, h2=h2, gu=gu, xh3=xh3, rs3=rs3)
    return h3, saved, w


def layer_bwd(dh, mem, w, s, tab, alpha, deps=(), halfway=None):
    D = dh.shape[1]
    g = {}
    wg = dict(out_dtype=MXU_DTYPE)
    dz3, g["ln3_g"], g["ln3_b"] = ln_bwd(dh, None, s["xh3"], s["rs3"], w["ln3_g"], 1.0, "ln3_bwd")
    act = swiglu_fwd(s["gu"], "swiglu_refwd")
    g["ffn_wo"] = mm(act, dz3, "tn", "mm_d_ffn_wo", deps=deps, **wg)
    dact = _wmm(dz3, w["ffn_wo"], "nt", "mm_dact")
    dgu = swiglu_bwd(s["gu"], dact, "swiglu_bwd")
    g["ffn_wi"] = mm(s["h2"], dgu, "tn", "mm_d_ffn_wi", b_blocks=2, out_blocks=N_CHIPS, **wg)
    dh2 = _wmm(dgu, w["ffn_wi"], "nt", "mm_dh2", a_blocks=2)
    dz2, g["ln2_g"], g["ln2_b"] = ln_bwd(dz3, dh2, s["xh2"], s["rs2"], w["ln2_g"], alpha, "ln2_bwd")
    g["co_w"] = mm(s["o"], dz2, "tn", "mm_d_co", **wg)
    do = _wmm(dz2, w["co_w"], "nt", "mm_do", out_dtype=MXU_DTYPE)
    dqc, dkv = cross_bwd(s["qc"], s["kv"], do, "cross_bwd")
    g["cq_w"] = mm(s["h1"], dqc, "tn", "mm_d_cq", **wg)
    g["ckv_w"] = mm(mem, dkv, "tn", "mm_d_ckv", out_blocks=N_CHIPS, **wg)
    dh1 = _wmm(dqc, w["cq_w"], "nt", "mm_dh1")
    deps = halfway(g, dh1) if halfway is not None else ()
    dz1, g["ln1_g"], g["ln1_b"] = ln_bwd(dz2, dh1, s["xh1"], s["rs1"], w["ln1_g"], alpha, "ln1_bwd")
    merged = merge_fwd(s["proj"], s["pr"], s["pa"], "merge_refwd")
    g["w_out"] = mm(merged, dz1, "tn", "mm_d_out", deps=deps, **wg)
    dm = _wmm(dz1, w["w_out"], "nt", "mm_dmerged")
    dpr, dpa, dg_rnn, dg_attn = merge_bwd(s["proj"], s["pr"], s["pa"], dm, "merge_bwd")
    g["w_br_rnn"] = mm(s["y_rnn"], dpr, "tn", "mm_d_br_rnn", **wg)
    g["w_br_attn"] = mm(s["y_attn"], dpa, "tn", "mm_d_br_attn", **wg)
    dy_rnn = _wmm(dpr, w["w_br_rnn"], "nt", "mm_dy_rnn")
    dy_attn = _wmm(dpa, w["w_br_attn"], "nt", "mm_dy_attn")
    dq, dkb, dvb, dsink = attn_bwd(s["proj"], w["sinks"], tab, s["y_attn"], dy_attn, D, "attn_bwd")
    dk, dv = band_fold(dkb, dvb, "band_fold")
    g["sinks"] = dsink[:, :w["sinks"].shape[0]]
    dgr, gt = rg_scan_bwd(s["proj"], dy_rnn, s["hs"], s["a"], "rg_scan_bwd")
    dxc, g["w_rg"], g["w_ig"], g["b_rg"], g["b_ig"], g["lru_lambda"] = rg_gates_bwd(
        gt, s["hs"], s["xc"], s["r"], s["ig"], w["w_rg"], w["w_ig"], w["lru_lambda"], "rg_gates_bwd")
    dxr, g["conv_w"], g["conv_b"] = rg_conv_bwd(s["proj"], dxc, w["conv_w"], "rg_conv_bwd")
    dproj = jnp.concatenate([dxr, dgr, dq, dk, dv, dg_rnn, dg_attn], axis=1)
    g["w_in"] = mm(s["h"], dproj, "tn", "mm_d_in")
    dhm = _wmm(dproj, w["w_in"], "nt", "mm_dh")
    return axpby(dz1, dhm, alpha, "layer_dx"), g


def local_step(x, mem, target, depth, weights_of, grads_halfway, grads_done):
    alpha = _alpha(depth)
    tab = rope_table(x.shape[0])
    h, saved, layers = x, [], []
    for l in range(depth):
        wl, deps, late = weights_of(l, h)
        h, s, wl = layer_fwd(h, mem, wl, tab, alpha, deps, late)
        layers.append(wl)
        saved.append(s)
    dh, loss = loss_head(h, target, "loss_head")
    deps = ()
    for l in reversed(range(depth)):
        dh, g = layer_bwd(dh, mem, layers[l], saved[l], tab, alpha, deps, grads_halfway(l))
        deps = grads_done(l, g, dh)
    return loss, dh


def _pad_rows(flat):
    n = flat.shape[0]
    rows = -(-n // (LANES * SUBLANES)) * SUBLANES
    return jnp.pad(flat, (0, rows * LANES - n)).reshape(rows, LANES)


def kernel(x, mem, w_in, conv_w, conv_b, w_rg, b_rg, w_ig, b_ig, lru_lambda, w_br_rnn, w_br_attn, sinks, w_out, ln1_g, ln1_b, cq_w, ckv_w, co_w, ln2_g, ln2_b, ffn_wi, ffn_wo, ln3_g, ln3_b, loss_target, m_w_in, m_conv_w, m_conv_b, m_w_rg, m_b_rg, m_w_ig, m_b_ig, m_lru_lambda, m_w_br_rnn, m_w_br_attn, m_sinks, m_w_out, m_ln1_g, m_ln1_b, m_cq_w, m_ckv_w, m_co_w, m_ln2_g, m_ln2_b, m_ffn_wi, m_ffn_wo, m_ln3_g, m_ln3_b, v_w_in, v_conv_w, v_conv_b, v_w_rg, v_b_rg, v_w_ig, v_b_ig, v_lru_lambda, v_w_br_rnn, v_w_br_attn, v_sinks, v_w_out, v_ln1_g, v_ln1_b, v_cq_w, v_ckv_w, v_co_w, v_ln2_g, v_ln2_b, v_ffn_wi, v_ffn_wo, v_ln3_g, v_ln3_b):
    args = dict(locals())
    w = {n: args[n] for n in WEIGHTS}
    m = {n: args["m_" + n] for n in WEIGHTS}
    v = {n: args["v_" + n] for n in WEIGHTS}
    cx, cy, cc = _place()
    chip = 2 * cx + cy
    L = w_in.shape[0]

    place = jnp.stack([cc, chip]).astype(jnp.int32)
    cw_rows = _pad_rows(conv_w.reshape(-1))
    cw_all = gather_devices(cw_rows, "gather_conv_w")[0::2]
    cw_parts = cw_all.reshape(N_CHIPS, -1)[:, :conv_w.size].reshape((N_CHIPS,) + conv_w.shape)
    conv_full = jnp.concatenate([cw_parts[k] for k in range(N_CHIPS)], axis=2)

    shards = [{n: w[n][l].astype(MXU_DTYPE) for n in BIG} for l in range(L)]
    late_names = tuple(n for n in BIG if n not in GATHER_FIRST)
    gathering = {(0, GATHER_FIRST): gather_start([shards[0][n] for n in GATHER_FIRST], cw_rows, "gather_start_0a")}
    gathering[0, late_names] = gather_start([shards[0][n] for n in late_names], gathering[0, GATHER_FIRST][4], "gather_start_0b")

    def gathered(l, names, after, tag):
        lands = gather_pass(gather_wait(gathering.pop((l, names)), after, f"gather_wait_{tag}"), f"gather_pass_{tag}")
        wl = {}
        for n, gw in zip(names, lands):
            rows_joined = gw.reshape(gw.shape[:-3] + (-1, gw.shape[-1]))
            if n == "w_in":
                wl[n] = (jnp.concatenate([gw[k] for k in range(N_CHIPS)], axis=1), {})
            elif n in COL_BLOCKED:
                wl[n] = (gw, dict(b_blocks=N_CHIPS))
            elif n in GATE_WEIGHTS:
                wl[n] = rows_joined
            else:
                wl[n] = (rows_joined, {})
        return wl, lands

    def start_next(l, after):
        if l + 1 == L:
            return ()
        gathering[l + 1, BIG] = gather_start([shards[l + 1][n] for n in BIG], after, f"gather_start_{l + 1}")
        return (gathering[l + 1, BIG][4],)

    def weights_of(l, h):
        deps, late = (), None
        if l == 0:
            wl, _ = gathered(0, GATHER_FIRST, h, "0a")

            def late(after):
                rest, lands = gathered(0, late_names, after, "0b")
                return rest, start_next(0, lands[0])
        else:
            wl, lands = gathered(l, BIG, h, str(l))
            deps = start_next(l, lands[0])
        for n in SMALL:
            wl[n] = conv_full[l] if n == "conv_w" else w[n][l] if n == "sinks" else w[n][l][None, :]
        return wl, deps, late

    def for_chips(n, g):
        if n in COL_BLOCKED:
            return g
        if n in GATE_WEIGHTS:
            nb, bw, _ = g.shape
            g = g.reshape(nb, N_CHIPS, bw // N_CHIPS, bw).transpose(1, 0, 2, 3).reshape(N_CHIPS, nb * bw // N_CHIPS, bw)
        elif SHARD_AXIS[n] == 0:
            g = g.reshape(N_CHIPS, g.shape[0] // N_CHIPS, g.shape[1])
        else:
            g = jnp.stack(jnp.split(g, N_CHIPS, axis=1))
        return g.astype(MXU_DTYPE)

    reduced, scattering, small_grads = {}, {}, [None] * L
    late_grads = tuple(n for n in BIG if n not in SCATTER_FIRST)

    def start_scatter(l, names, g, after, tag):
        partial_sums = [for_chips(n, g[n])[None] for n in names]
        from_sibling = swap_sibling(partial_sums, f"grad_to_sibling_{tag}")
        chip_sums = [add_pair(place, a, b, f"grad_add_pair_{n}_{l}") for n, a, b in zip(names, partial_sums, from_sibling)]
        scattering[l, names] = scatter_start(chip_sums, after, f"grad_scatter_start_{tag}")
        return (scattering[l, names][4],)

    def finish_layer(l, after):
        for names in [k[1] for k in list(scattering) if k[0] == l]:
            tag = str(l) if names == BIG else f"{l}{'a' if names == SCATTER_FIRST else 'b'}"
            chip_sums, from_chips = scatter_wait(scattering.pop((l, names)), after, f"grad_scatter_wait_{tag}")
            for n, own, others in zip(names, chip_sums, from_chips):
                target = reduced.get(n, (L, 2 * own.shape[2], own.shape[3]))
                reduced[n] = add_chips(place, own, others, l, target, f"grad_add_chips_{n}_{l}")
        reduced.update(zip(BIG, join_halves([reduced[n] for n in BIG], l, f"grad_join_{l}")))

    def grads_halfway(l):
        if l > 0:
            return None

        def halfway(g, after):
            if L > 1:
                finish_layer(1, after)
            return start_scatter(0, SCATTER_FIRST, g, after, "0a")

        return halfway

    def grads_done(l, g, dh):
        small_grads[l] = {n: g[n] for n in SMALL}
        if l == 0:
            return start_scatter(0, late_grads, g, dh, "0b")
        if l + 1 < L:
            finish_layer(l + 1, dh)
        return start_scatter(l, BIG, g, dh, str(l))

    loss11, dx = local_step(x[0], mem[0], loss_target[0], L, weights_of, grads_halfway, grads_done)
    finish_layer(0, dx)
    loss = lax.psum(loss11[0, 0], ("x", "y", "c"))
    gshard = {n: reduced[n].reshape(w[n].shape) for n in BIG}

    small_full = {n: jnp.stack([gl[n] for gl in small_grads]).reshape(w[n].shape[:1] + ((CONV_WIDTH, -1) if n == "conv_w" else (-1,)))
                  for n in SMALL}
    small_flat = jnp.concatenate([small_full[n].reshape(-1) for n in SMALL])
    small_sum = sum_devices(gather_devices(_pad_rows(small_flat), "gather_small_grads"), "sum_small_grads").reshape(-1)
    off = 0
    for n in SMALL:
        gfull = small_sum[off:off + small_full[n].size].reshape(small_full[n].shape)
        off += small_full[n].size
        if n == "conv_w":
            width = conv_w.shape[2]
            gfull = lax.dynamic_slice_in_dim(gfull, chip * width, width, axis=2)
        gshard[n] = gfull

    delta, new_m, new_v, grad = {}, {}, {}, {}
    for n in WEIGHTS:
        delta[n], new_m[n], new_v[n], grad[n] = adamw(w[n], gshard[n], m[n], v[n], "adamw_" + n)
    return (loss, dx[None], *[grad[n] for n in WEIGHTS], *[delta[n] for n in WEIGHTS], *[new_m[n] for n in WEIGHTS],
            *[new_v[n] for n in WEIGHTS])
```

```python
import functools
import math

import jax
import jax.numpy as jnp
import numpy as np
from jax import lax
from jax.experimental import pallas as pl
from jax.experimental.pallas import tpu as pltpu

F32 = jnp.float32
BF16 = jnp.bfloat16
MXU_DTYPE = BF16

HEAD_DIM = 64
N_KV_HEADS = 2
WINDOW = 128
ROT_DIM = HEAD_DIM // 4
ROPE_THETA = 500000.0
CROSS_HEADS = 4
RNN_BLOCKS = 4
CONV_WIDTH = 4
LRU_C = 8.0
LN_EPS = 1e-5
NEG_INF = -1e30
ADAM_LR = 0.001
ADAM_B1 = 0.9
ADAM_B2 = 0.999
ADAM_EPS = 1e-08
ADAM_WD = 0.01
ADAM_STEP = 10

VMEM_BYTES_V7X = 64 * 1024 * 1024
VMEM_BLOCK_BUDGET = 36 * 1024 * 1024
LANES = 128
SUBLANES = 8

MESH_ID = pl.DeviceIdType.MESH
N_CHIPS = 4
N_DEV = 8

BIG = ("w_in", "w_rg", "w_ig", "w_br_rnn", "w_br_attn", "w_out", "cq_w", "ckv_w", "co_w", "ffn_wi", "ffn_wo")
SHARD_AXIS = {"w_in": 1, "w_rg": 1, "w_ig": 1, "w_br_rnn": 0, "w_br_attn": 0, "w_out": 0, "cq_w": 0, "ckv_w": 1,
              "co_w": 0, "ffn_wi": 1, "ffn_wo": 0}
SMALL = ("conv_w", "conv_b", "b_rg", "b_ig", "lru_lambda", "sinks", "ln1_g", "ln1_b", "ln2_g", "ln2_b", "ln3_g", "ln3_b")
WEIGHTS = ("w_in", "conv_w", "conv_b", "w_rg", "b_rg", "w_ig", "b_ig", "lru_lambda", "w_br_rnn", "w_br_attn", "sinks",
           "w_out", "ln1_g", "ln1_b", "cq_w", "ckv_w", "co_w", "ln2_g", "ln2_b", "ffn_wi", "ffn_wo", "ln3_g", "ln3_b")
GATE_WEIGHTS = ("w_rg", "w_ig")
COL_BLOCKED = ("ckv_w", "ffn_wi")
GATHER_FIRST = ("w_in", "w_rg", "w_ig")
SCATTER_FIRST = ("ffn_wo", "ffn_wi", "co_w", "cq_w", "ckv_w")


def _params(dims=None, vmem=None):
    return pltpu.CompilerParams(dimension_semantics=dims, vmem_limit_bytes=vmem)


def _vmem_limit(block_bytes, temp_bytes=0):
    want = int(2 * block_bytes + temp_bytes) + (6 << 20)
    return max(32 << 20, min(want, VMEM_BYTES_V7X - (6 << 20)))


def _divisors(n, align, cap):
    out = [d for d in range(align, min(n, cap) + 1, align) if n % d == 0]
    if n <= cap and n not in out:
        out.append(n)
    return sorted(out, reverse=True) or [n]


PIN_MIN_ELEMENTS = 1 << 18


def hbm_call(body, **kw):
    def in_hbm(s):
        return pltpu.HBM(s.shape, s.dtype) if math.prod(s.shape) >= PIN_MIN_ELEMENTS else s

    shapes = kw.pop("out_shape")
    shapes = [in_hbm(s) for s in shapes] if isinstance(shapes, (list, tuple)) else in_hbm(shapes)
    call = pl.pallas_call(body, out_shape=shapes, **kw)

    def run(*args):
        return call(*[pltpu.with_memory_space_constraint(a, pltpu.HBM) if a.size >= PIN_MIN_ELEMENTS else a for a in args])

    return run


def _sigmoid(x):
    return 1.0 / (1.0 + jnp.exp(-x))


def _gelu_parts(x):
    c = math.sqrt(2.0 / math.pi)
    u = c * (x + 0.044715 * x * x * x)
    t = jnp.tanh(u)
    return t, c * (1.0 + 3 * 0.044715 * x * x)


def _gelu(x):
    t, _ = _gelu_parts(x)
    return 0.5 * x * (1.0 + t)


def _gelu_grad(x):
    t, du = _gelu_parts(x)
    return 0.5 * (1.0 + t) + 0.5 * x * (1.0 - t * t) * du


def _neg_expm1(x):
    series = x * (1.0 + x * (0.5 + x * (1.0 / 6 + x * (1.0 / 24 + x * (1.0 / 120)))))
    return -jnp.where(x > -0.1, series, jnp.exp(x) - 1.0)


def _softplus_neg(lam):
    x = -lam
    return jnp.maximum(x, 0.0) + jnp.log1p(jnp.exp(-jnp.abs(x)))


STEP_US = 0.35
HBM_BYTES_PER_US = 2.5e6
MXU_FLOPS_PER_US = 7e8


def _layer_norm(z, g, b):
    mu = jnp.mean(z, axis=-1, keepdims=True)
    zc = z - mu
    rs = lax.rsqrt(jnp.mean(zc * zc, axis=-1, keepdims=True) + LN_EPS)
    xh = zc * rs
    return xh * g + b, xh, rs


def mm(a, b, mode, name, *, b_index=(), a_blocks=0, b_blocks=0, out_blocks=0, out_dtype=F32, deps=(), post_norm=None):
    nlead = len(b_index) + (1 if b_blocks else 0)
    bk, bn = b.shape[nlead:]
    M, K = (a.shape[-1], a.shape[-2]) if mode == "tn" else (a.shape[-2], a.shape[-1] * max(a_blocks, 1))
    N = bk if mode == "nt" else bn * max(b_blocks, 1) if mode == "nn" or mode == "tn" else bn
    asz, bsz, osz = a.dtype.itemsize, b.dtype.itemsize, jnp.dtype(out_dtype).itemsize
    n_unit = math.gcd(N // max(out_blocks, 1), N // max(b_blocks, 1) if mode != "nt" else N)
    k_unit = math.gcd(K // max(a_blocks, 1), K // max(b_blocks, 1) if mode == "nt" else K)
    tms = _divisors(M, LANES if mode == "tn" else SUBLANES, 2048)
    tns = [N] if post_norm else _divisors(n_unit, LANES, 2048)
    tks = _divisors(k_unit, LANES, k_unit)
    best = None
    for tm in tms:
        for tn in tns:
            for tk in tks:
                nk = K // tk
                scratch = tm * tn * 4 if (nk > 1 and osz != 4) else 0
                blocks = tm * tk * asz + tn * tk * bsz + tm * tn * osz * (3 if post_norm else 1)
                temps = tm * tk * (2 + (4 if mode == "tn" else 0)) + tn * tk * 2 + tm * tn * 4 + scratch
                if 2 * blocks + temps > VMEM_BLOCK_BUDGET + (8 << 20):
                    continue
                ni, nj = M // tm, N // tn
                traffic = M * K * asz * (nj if nk > 1 else 1) + N * K * bsz * (1 if nj * nk == 1 else ni) + M * N * osz
                busy = max(traffic / HBM_BYTES_PER_US, 2.0 * M * N * K / MXU_FLOPS_PER_US)
                cost = ni * nj * nk * STEP_US + busy + blocks / HBM_BYTES_PER_US
                if best is None or cost < best[0]:
                    best = (cost, tm, tn, tk, blocks, temps)
    _, tm, tn, tk, blocks, temps = best
    nk = K // tk
    use_scratch = nk > 1 and osz != 4

    def split(index, total, blocks, tile):
        per = total // blocks // tile
        return index // per, index % per

    def body(a_ref, b_ref, *rest):
        rest = rest[len(deps):]
        if post_norm:
            h_ref, g_ref, beta_ref, o_ref, xh_ref, rs_ref = rest[:6]
            acc = rest[6:]
        else:
            o_ref, acc = rest[0], rest[1:]
        av = a_ref[...].astype(MXU_DTYPE)
        bv = b_ref[...].astype(MXU_DTYPE)
        dn = {"nn": (((1,), (0,)), ((), ())), "nt": (((1,), (1,)), ((), ())), "tn": (((0,), (0,)), ((), ()))}[mode]
        r = lax.dot_general(av, bv, dn, preferred_element_type=F32)

        def normalise(f):
            o_ref[...], xh_ref[...], rs_ref[...] = _layer_norm(post_norm[3] * h_ref[...] + f, g_ref[...], beta_ref[...])

        if nk == 1 and post_norm:
            normalise(r)
        elif nk == 1:
            o_ref[...] = r.astype(o_ref.dtype)
        else:
            acc_ref = acc[0] if use_scratch else o_ref

            @pl.when(pl.program_id(2) == 0)
            def _():
                acc_ref[...] = r

            @pl.when(pl.program_id(2) > 0)
            def _():
                acc_ref[...] += r

            if use_scratch:
                @pl.when(pl.program_id(2) == nk - 1)
                def _():
                    o_ref[...] = acc_ref[...].astype(o_ref.dtype)
            elif post_norm:
                @pl.when(pl.program_id(2) == nk - 1)
                def _():
                    normalise(o_ref[...])

    if mode == "tn":
        a_spec = pl.BlockSpec((tk, tm), lambda i, j, k: (k, i))
    elif a_blocks:
        a_spec = pl.BlockSpec((None, tm, tk), lambda i, j, k: (split(k, K, a_blocks, tk)[0], i, split(k, K, a_blocks, tk)[1]))
    else:
        a_spec = pl.BlockSpec((tm, tk), lambda i, j, k: (i, k))
    lead = (None,) * nlead
    if mode == "nt":
        bmap = ((lambda i, j, k: b_index + (split(k, K, b_blocks, tk)[0], j, split(k, K, b_blocks, tk)[1])) if b_blocks
                else (lambda i, j, k: b_index + (j, k)))
        b_spec = pl.BlockSpec(lead + (tn, tk), bmap)
    else:
        bmap = ((lambda i, j, k: b_index + (split(j, N, b_blocks, tn)[0], k, split(j, N, b_blocks, tn)[1])) if b_blocks
                else (lambda i, j, k: b_index + (k, j)))
        b_spec = pl.BlockSpec(lead + (tk, tn), bmap)
    if out_blocks:
        o_spec = pl.BlockSpec((None, tm, tn), lambda i, j, k: (split(j, N, out_blocks, tn)[0], i, split(j, N, out_blocks, tn)[1]))
        o_shape = jax.ShapeDtypeStruct((out_blocks, M, N // out_blocks), out_dtype)
    else:
        o_spec = pl.BlockSpec((tm, tn), lambda i, j, k: (i, j))
        o_shape = jax.ShapeDtypeStruct((M, N), out_dtype)
    in_specs, extra = [a_spec, b_spec] + [pl.BlockSpec(memory_space=pl.ANY)] * len(deps), ()
    if post_norm:
        vec = pl.BlockSpec((1, N), lambda i, j, k: (0, 0))
        in_specs += [pl.BlockSpec((tm, N), lambda i, j, k: (i, 0)), vec, vec]
        o_spec = [o_spec, pl.BlockSpec((tm, N), lambda i, j, k: (i, 0)), pl.BlockSpec((tm, 1), lambda i, j, k: (i, 0))]
        o_shape = [o_shape, jax.ShapeDtypeStruct((M, N), F32), jax.ShapeDtypeStruct((M, 1), F32)]
        extra = post_norm[:3]
    return hbm_call(
        body, name=name, grid=(M // tm, N // tn, nk), in_specs=in_specs, out_specs=o_spec, out_shape=o_shape,
        scratch_shapes=[pltpu.VMEM((tm, tn), F32)] if use_scratch else [],
        compiler_params=_params(("parallel", "parallel", "arbitrary"), _vmem_limit(blocks, temps)),
    )(a, b, *deps, *extra)


ROW_TILE = 512
GATE_ROWS = 1024


def ln_bwd(dy_a, dy_b, xh, rs, g, c1, name):
    S, D = xh.shape
    tr = min(ROW_TILE, S)
    two = dy_b is not None

    def body(*refs):
        if two:
            a_ref, b_ref, xh_ref, rs_ref, g_ref, dz_ref, dg_ref, db_ref = refs
            dy = c1 * a_ref[...] + b_ref[...]
        else:
            a_ref, xh_ref, rs_ref, g_ref, dz_ref, dg_ref, db_ref = refs
            dy = a_ref[...]
        x = xh_ref[...]
        dyg = dy * g_ref[...]
        m1 = jnp.mean(dyg, axis=-1, keepdims=True)
        m2 = jnp.mean(dyg * x, axis=-1, keepdims=True)
        dz_ref[...] = rs_ref[...] * (dyg - m1 - x * m2)

        @pl.when(pl.program_id(0) == 0)
        def _():
            dg_ref[...] = jnp.zeros_like(dg_ref)
            db_ref[...] = jnp.zeros_like(db_ref)

        dg_ref[...] += jnp.sum(dy * x, axis=0, keepdims=True)
        db_ref[...] += jnp.sum(dy, axis=0, keepdims=True)

    row = pl.BlockSpec((tr, D), lambda i: (i, 0))
    vec = pl.BlockSpec((1, D), lambda i: (0, 0))
    ins = [row, row] if two else [row]
    args = (dy_a, dy_b) if two else (dy_a,)
    return hbm_call(
        body, name=name, grid=(S // tr,), in_specs=ins + [row, pl.BlockSpec((tr, 1), lambda i: (i, 0)), vec],
        out_specs=[row, vec, vec],
        out_shape=[jax.ShapeDtypeStruct((S, D), F32), jax.ShapeDtypeStruct((1, D), F32), jax.ShapeDtypeStruct((1, D), F32)],
        compiler_params=_params(("arbitrary",), 48 << 20),
    )(*args, xh, rs, g)


def axpby(a, b, c1, name):
    S, D = a.shape
    tr = min(ROW_TILE, S)

    def body(a_ref, b_ref, o_ref):
        o_ref[...] = c1 * a_ref[...] + b_ref[...]

    row = pl.BlockSpec((tr, D), lambda i: (i, 0))
    return hbm_call(body, name=name, grid=(S // tr,), in_specs=[row, row], out_specs=row,
                          out_shape=jax.ShapeDtypeStruct((S, D), F32), compiler_params=_params(("parallel",)))(a, b)


def loss_head(y, t, name):
    S, D = y.shape
    tr = min(ROW_TILE, S)
    nsteps = S // tr

    def body(y_ref, t_ref, dy_ref, l_ref, acc_ref):
        i = pl.program_id(0)

        @pl.when(i == 0)
        def _():
            acc_ref[...] = jnp.zeros_like(acc_ref)

        e = y_ref[...] - t_ref[...]
        dy_ref[...] = e * (1.0 / D)
        acc_ref[...] += jnp.sum(e * e, axis=0, keepdims=True)

        @pl.when(i == nsteps - 1)
        def _():
            l_ref[...] = jnp.sum(acc_ref[...], axis=1, keepdims=True) * (0.5 / D)

    row = pl.BlockSpec((tr, D), lambda i: (i, 0))
    return hbm_call(
        body, name=name, grid=(nsteps,), in_specs=[row, row],
        out_specs=[row, pl.BlockSpec((1, 1), lambda i: (0, 0))],
        out_shape=[jax.ShapeDtypeStruct((S, D), F32), jax.ShapeDtypeStruct((1, 1), F32)],
        scratch_shapes=[pltpu.VMEM((1, D), F32)], compiler_params=_params(("arbitrary",)),
    )(y, t)


SWIGLU_ROWS = 256


def swiglu_fwd(gu, name):
    _, S, Fh = gu.shape
    tc = _divisors(Fh, LANES, 1536)[0]
    tr = min(SWIGLU_ROWS, S)

    def body(gu_ref, o_ref):
        g = gu_ref[0]
        o_ref[...] = (g * _sigmoid(g) * gu_ref[1]).astype(o_ref.dtype)

    return hbm_call(
        body, name=name, grid=(S // tr, Fh // tc), in_specs=[pl.BlockSpec((2, tr, tc), lambda i, j: (0, i, j))],
        out_specs=pl.BlockSpec((tr, tc), lambda i, j: (i, j)), out_shape=jax.ShapeDtypeStruct((S, Fh), MXU_DTYPE),
        compiler_params=_params(("parallel", "parallel")),
    )(gu)


def swiglu_bwd(gu, dact, name):
    _, S, Fh = gu.shape
    tc = _divisors(Fh, LANES, 1536)[0]
    tr = min(SWIGLU_ROWS, S)

    def body(gu_ref, d_ref, o_ref):
        g, u, d = gu_ref[0], gu_ref[1], d_ref[...]
        s = _sigmoid(g)
        o_ref[0] = (d * u * (s * (1.0 + g * (1.0 - s)))).astype(o_ref.dtype)
        o_ref[1] = (d * (g * s)).astype(o_ref.dtype)

    both = pl.BlockSpec((2, tr, tc), lambda i, j: (0, i, j))
    return hbm_call(
        body, name=name, grid=(S // tr, Fh // tc), in_specs=[both, pl.BlockSpec((tr, tc), lambda i, j: (i, j))],
        out_specs=both, out_shape=jax.ShapeDtypeStruct((2, S, Fh), MXU_DTYPE), compiler_params=_params(("parallel", "parallel")),
    )(gu, dact)


GATE_COLS = 256


def merge_fwd(proj, pr, pa, name):
    S, D = pr.shape
    tr = min(GATE_ROWS, S)
    c0 = (3 * D + 2 * N_KV_HEADS * HEAD_DIM) // GATE_COLS
    c1 = c0 + D // GATE_COLS

    def body(gr_ref, ga_ref, pr_ref, pa_ref, o_ref):
        o_ref[...] = (_sigmoid(gr_ref[...]) * pr_ref[...] + _sigmoid(ga_ref[...]) * pa_ref[...]).astype(o_ref.dtype)

    blk = pl.BlockSpec((tr, GATE_COLS), lambda i, j: (i, j))
    return hbm_call(
        body, name=name, grid=(S // tr, D // GATE_COLS),
        in_specs=[pl.BlockSpec((tr, GATE_COLS), lambda i, j: (i, c0 + j)), pl.BlockSpec((tr, GATE_COLS), lambda i, j: (i, c1 + j)),
                  blk, blk],
        out_specs=blk, out_shape=jax.ShapeDtypeStruct((S, D), MXU_DTYPE), compiler_params=_params(("parallel", "parallel")),
    )(proj, proj, pr, pa)


def merge_bwd(proj, pr, pa, dm, name):
    S, D = pr.shape
    tr = min(GATE_ROWS, S)
    c0 = (3 * D + 2 * N_KV_HEADS * HEAD_DIM) // GATE_COLS
    c1 = c0 + D // GATE_COLS

    def body(gr_ref, ga_ref, pr_ref, pa_ref, dm_ref, dpr_ref, dpa_ref, dgr_ref, dga_ref):
        sr, sa, d = _sigmoid(gr_ref[...]), _sigmoid(ga_ref[...]), dm_ref[...]
        dpr_ref[...] = (d * sr).astype(dpr_ref.dtype)
        dpa_ref[...] = (d * sa).astype(dpa_ref.dtype)
        dgr_ref[...] = (d * pr_ref[...] * (sr * (1.0 - sr))).astype(dgr_ref.dtype)
        dga_ref[...] = (d * pa_ref[...] * (sa * (1.0 - sa))).astype(dga_ref.dtype)

    blk = pl.BlockSpec((tr, GATE_COLS), lambda i, j: (i, j))
    sds = jax.ShapeDtypeStruct((S, D), MXU_DTYPE)
    return hbm_call(
        body, name=name, grid=(S // tr, D // GATE_COLS),
        in_specs=[pl.BlockSpec((tr, GATE_COLS), lambda i, j: (i, c0 + j)), pl.BlockSpec((tr, GATE_COLS), lambda i, j: (i, c1 + j)),
                  blk, blk, blk],
        out_specs=[blk, blk, blk, blk], out_shape=[sds, sds, sds, sds], compiler_params=_params(("parallel", "parallel")),
    )(proj, proj, pr, pa, dm)


RG_ROWS = 512


def _shift_down(cur, prev, d, row, first):
    halo = jnp.where(first, 0.0, pltpu.roll(prev, d, 0))
    return jnp.where(row >= d, pltpu.roll(cur, d, 0), halo)


def _shift_up(cur, nxt, d, row, last, tr):
    halo = jnp.where(last, 0.0, pltpu.roll(nxt, tr - d, 0))
    return jnp.where(row < tr - d, pltpu.roll(cur, tr - d, 0), halo)


def _lru_coeffs(r, lam):
    sp = _softplus_neg(lam)
    la = -LRU_C * r * sp
    return sp, la, jnp.exp(la), _neg_expm1(2.0 * la)


def rg_gates_fwd(proj, conv_w, conv_b, w_rg, b_rg, w_ig, b_ig, lam, name):
    S = proj.shape[0]
    nblk, bw, _ = w_rg.shape
    D = nblk * bw
    tr = min(RG_ROWS, S)

    def body(xr_ref, xp_ref, cw_ref, cb_ref, wr_ref, br_ref, wi_ref, bi_ref, lam_ref, xc_ref, r_ref, i_ref, a_ref, b_ref):
        first = pl.program_id(1) == 0
        cur, prev = xr_ref[...], xp_ref[...]
        row = lax.broadcasted_iota(jnp.int32, cur.shape, 0)
        xc = cb_ref[...]
        for k in range(CONV_WIDTH - 1):
            xc = xc + _shift_down(cur, prev, CONV_WIDTH - 1 - k, row, first) * cw_ref[k:k + 1, :]
        xc = xc + cur * cw_ref[CONV_WIDTH - 1:CONV_WIDTH, :]
        xm = xc.astype(MXU_DTYPE)
        r = _sigmoid(jnp.dot(xm, wr_ref[...].astype(MXU_DTYPE), preferred_element_type=F32) + br_ref[...])
        ig = _sigmoid(jnp.dot(xm, wi_ref[...].astype(MXU_DTYPE), preferred_element_type=F32) + bi_ref[...])
        _, _, a, em = _lru_coeffs(r, lam_ref[...])
        xc_ref[...] = xc
        r_ref[...] = r
        i_ref[...] = ig
        a_ref[...] = a
        b_ref[...] = jnp.sqrt(em) * (ig * xc)

    tile = pl.BlockSpec((tr, bw), lambda n, i: (i, n))
    vec = pl.BlockSpec((1, bw), lambda n, i: (0, n))
    wblk = pl.BlockSpec((None, bw, bw), lambda n, i: (n, 0, 0))
    sds = jax.ShapeDtypeStruct((S, D), F32)
    return hbm_call(
        body, name=name, grid=(nblk, S // tr),
        in_specs=[tile, pl.BlockSpec((tr, bw), lambda n, i: (jnp.maximum(i - 1, 0), n)),
                  pl.BlockSpec((CONV_WIDTH, bw), lambda n, i: (0, n)), vec, wblk, vec, wblk, vec, vec],
        out_specs=[tile] * 5, out_shape=[sds] * 5, compiler_params=_params(("parallel", "parallel")),
    )(proj, proj, conv_w, conv_b, w_rg, b_rg, w_ig, b_ig, lam)


SCAN_COLS = 256
CHUNK = SUBLANES
SCAN_UNROLL = 4


def rg_scan_fwd(proj, a, b, name):
    S, D = a.shape
    cb = min(SCAN_COLS, D)
    goff = D // cb

    def body(a_ref, b_ref, g_ref, hs_ref, y_ref):
        row = lax.broadcasted_iota(jnp.int32, (CHUNK, cb), 0)

        def step(c, carry):
            r0 = pl.multiple_of(c * CHUNK, CHUNK)
            A = a_ref[pl.ds(r0, CHUNK), :]
            B = b_ref[pl.ds(r0, CHUNK), :]
            for d in (1, 2, 4):
                As = jnp.where(row >= d, pltpu.roll(A, d, 0), 1.0)
                Bs = jnp.where(row >= d, pltpu.roll(B, d, 0), 0.0)
                B = A * Bs + B
                A = A * As
            hs_ref[pl.ds(r0, CHUNK), :] = B + A * carry
            a_end = jnp.sum(jnp.where(row == CHUNK - 1, A, 0.0), axis=0, keepdims=True)
            b_end = jnp.sum(jnp.where(row == CHUNK - 1, B, 0.0), axis=0, keepdims=True)
            return b_end + a_end * carry

        lax.fori_loop(0, S // CHUNK, step, jnp.zeros((1, cb), F32), unroll=SCAN_UNROLL)
        y_ref[...] = (hs_ref[...] * _gelu(g_ref[...])).astype(y_ref.dtype)

    col = pl.BlockSpec((S, cb), lambda j: (0, j))
    return hbm_call(
        body, name=name, grid=(D // cb,), in_specs=[col, col, pl.BlockSpec((S, cb), lambda j: (0, goff + j))],
        out_specs=[col, col], out_shape=[jax.ShapeDtypeStruct((S, D), F32), jax.ShapeDtypeStruct((S, D), MXU_DTYPE)],
        compiler_params=_params(("parallel",), _vmem_limit(5 * S * cb * 4, 4 * S * cb * 4)),
    )(a, b, proj)


def rg_scan_bwd(proj, dy, hs, a, name):
    S, D = a.shape
    cb = min(SCAN_COLS, D)
    goff = D // cb
    nchunks = S // CHUNK

    def body(g_ref, dy_ref, hs_ref, a_ref, dg_ref, gt_ref):
        gate, dy = g_ref[...], dy_ref[...]
        dg_ref[...] = (dy * hs_ref[...] * _gelu_grad(gate)).astype(dg_ref.dtype)
        gt_ref[...] = dy * _gelu(gate)
        row = lax.broadcasted_iota(jnp.int32, (CHUNK, cb), 0)

        def step(k, carry):
            c = nchunks - 1 - k
            r0 = pl.multiple_of(c * CHUNK, CHUNK)
            rn = pl.multiple_of(jnp.minimum(c + 1, nchunks - 1) * CHUNK, CHUNK)
            last = c == nchunks - 1
            nxt = jnp.where(last, 0.0, pltpu.roll(a_ref[pl.ds(rn, CHUNK), :], CHUNK - 1, 0))
            A = jnp.where(row < CHUNK - 1, pltpu.roll(a_ref[pl.ds(r0, CHUNK), :], CHUNK - 1, 0), nxt)
            B = gt_ref[pl.ds(r0, CHUNK), :]
            for d in (1, 2, 4):
                As = jnp.where(row < CHUNK - d, pltpu.roll(A, CHUNK - d, 0), 1.0)
                Bs = jnp.where(row < CHUNK - d, pltpu.roll(B, CHUNK - d, 0), 0.0)
                B = A * Bs + B
                A = A * As
            gt_ref[pl.ds(r0, CHUNK), :] = B + A * carry
            a_end = jnp.sum(jnp.where(row == 0, A, 0.0), axis=0, keepdims=True)
            b_end = jnp.sum(jnp.where(row == 0, B, 0.0), axis=0, keepdims=True)
            return b_end + a_end * carry

        lax.fori_loop(0, nchunks, step, jnp.zeros((1, cb), F32), unroll=SCAN_UNROLL)

    col = pl.BlockSpec((S, cb), lambda j: (0, j))
    return hbm_call(
        body, name=name, grid=(D // cb,), in_specs=[pl.BlockSpec((S, cb), lambda j: (0, goff + j)), col, col, col],
        out_specs=[col, col], out_shape=[jax.ShapeDtypeStruct((S, D), MXU_DTYPE), jax.ShapeDtypeStruct((S, D), F32)],
        compiler_params=_params(("parallel",), _vmem_limit(6 * S * cb * 4, 6 * S * cb * 4)),
    )(proj, dy, hs, a)


def rg_gates_bwd(gt, hs, xc, r, ig, w_rg, w_ig, lam, name):
    S, D = xc.shape
    nblk, bw, _ = w_rg.shape
    tr = min(RG_ROWS, S)

    def body(gt_ref, hs_ref, hp_ref, xc_ref, r_ref, i_ref, wr_ref, wi_ref, lam_ref,
             dxc_ref, dwr_ref, dwi_ref, dbr_ref, dbi_ref, dl_ref):
        step = pl.program_id(1)
        g, hs, xc, r, ig, lam = gt_ref[...], hs_ref[...], xc_ref[...], r_ref[...], i_ref[...], lam_ref[...]
        row = lax.broadcasted_iota(jnp.int32, g.shape, 0)
        hprev = _shift_down(hs, hp_ref[...], 1, row, step == 0)
        sp, _, a, em = _lru_coeffs(r, lam)
        mult = jnp.sqrt(em)
        du = g * mult
        dla = g * hprev * a - (g * (ig * xc)) * (a * a) / mult
        dpr = (dla * (-LRU_C * sp)) * (r * (1.0 - r))
        dpi = (du * xc) * (ig * (1.0 - ig))
        dprm, dpim = dpr.astype(MXU_DTYPE), dpi.astype(MXU_DTYPE)
        nt = (((1,), (1,)), ((), ()))
        dxc_ref[...] = (du * ig + lax.dot_general(dprm, wr_ref[...].astype(MXU_DTYPE), nt, preferred_element_type=F32)
                        + lax.dot_general(dpim, wi_ref[...].astype(MXU_DTYPE), nt, preferred_element_type=F32))

        @pl.when(step == 0)
        def _():
            for ref in (dwr_ref, dwi_ref, dbr_ref, dbi_ref, dl_ref):
                ref[...] = jnp.zeros_like(ref)

        xct = xc.T.astype(MXU_DTYPE)
        dwr_ref[...] += jnp.dot(xct, dprm, preferred_element_type=F32)
        dwi_ref[...] += jnp.dot(xct, dpim, preferred_element_type=F32)
        dbr_ref[...] += jnp.sum(dpr, axis=0, keepdims=True)
        dbi_ref[...] += jnp.sum(dpi, axis=0, keepdims=True)
        dl_ref[...] += jnp.sum(dla * (-LRU_C * r), axis=0, keepdims=True) * (-_sigmoid(-lam))

    tile = pl.BlockSpec((tr, bw), lambda n, i: (i, n))
    vec = pl.BlockSpec((1, bw), lambda n, i: (0, n))
    wblk = pl.BlockSpec((None, bw, bw), lambda n, i: (n, 0, 0))
    return hbm_call(
        body, name=name, grid=(nblk, S // tr),
        in_specs=[tile, tile, pl.BlockSpec((tr, bw), lambda n, i: (jnp.maximum(i - 1, 0), n)), tile, tile, tile, wblk, wblk, vec],
        out_specs=[tile, wblk, wblk, vec, vec, vec],
        out_shape=[jax.ShapeDtypeStruct((S, D), F32), jax.ShapeDtypeStruct((nblk, bw, bw), F32), jax.ShapeDtypeStruct((nblk, bw, bw), F32),
                   jax.ShapeDtypeStruct((1, D), F32), jax.ShapeDtypeStruct((1, D), F32), jax.ShapeDtypeStruct((1, D), F32)],
        compiler_params=_params(("parallel", "arbitrary")),
    )(gt, hs, hs, xc, r, ig, w_rg, w_ig, lam)


def rg_conv_bwd(proj, dxc, conv_w, name):
    S, D = dxc.shape
    bw = min(SCAN_COLS, D)
    tr = min(RG_ROWS, S)
    nsteps = S // tr

    def body(d_ref, dn_ref, xr_ref, xp_ref, cw_ref, dxr_ref, dcw_ref, dcb_ref):
        step = pl.program_id(1)
        d, xr = d_ref[...], xr_ref[...]
        row = lax.broadcasted_iota(jnp.int32, d.shape, 0)
        dxr = d * cw_ref[CONV_WIDTH - 1:CONV_WIDTH, :]
        for k in range(CONV_WIDTH - 1):
            dxr = dxr + _shift_up(d, dn_ref[...], CONV_WIDTH - 1 - k, row, step == nsteps - 1, tr) * cw_ref[k:k + 1, :]
        dxr_ref[...] = dxr.astype(dxr_ref.dtype)

        @pl.when(step == 0)
        def _():
            dcw_ref[...] = jnp.zeros_like(dcw_ref)
            dcb_ref[...] = jnp.zeros_like(dcb_ref)

        for k in range(CONV_WIDTH - 1):
            xs = _shift_down(xr, xp_ref[...], CONV_WIDTH - 1 - k, row, step == 0)
            dcw_ref[k:k + 1, :] += jnp.sum(d * xs, axis=0, keepdims=True)
        dcw_ref[CONV_WIDTH - 1:CONV_WIDTH, :] += jnp.sum(d * xr, axis=0, keepdims=True)
        dcb_ref[...] += jnp.sum(d, axis=0, keepdims=True)

    tile = pl.BlockSpec((tr, bw), lambda n, i: (i, n))
    cwb = pl.BlockSpec((CONV_WIDTH, bw), lambda n, i: (0, n))
    return hbm_call(
        body, name=name, grid=(D // bw, nsteps),
        in_specs=[tile, pl.BlockSpec((tr, bw), lambda n, i: (jnp.minimum(i + 1, nsteps - 1), n)), tile,
                  pl.BlockSpec((tr, bw), lambda n, i: (jnp.maximum(i - 1, 0), n)), cwb],
        out_specs=[tile, cwb, pl.BlockSpec((1, bw), lambda n, i: (0, n))],
        out_shape=[jax.ShapeDtypeStruct((S, D), MXU_DTYPE), jax.ShapeDtypeStruct((CONV_WIDTH, D), F32), jax.ShapeDtypeStruct((1, D), F32)],
        compiler_params=_params(("parallel", "arbitrary")),
    )(dxc, dxc, proj, proj, conv_w)


def rope_table(S):
    half = ROT_DIM // 2
    pos = jnp.arange(S, dtype=F32)
    inv = ROPE_THETA ** (-jnp.arange(0, ROT_DIM, 2, dtype=F32) / ROT_DIM)
    ang = pos[:, None] * inv[None, :]
    cos, sin = jnp.cos(ang), jnp.sin(ang)
    zero = jnp.zeros((S, HEAD_DIM - ROT_DIM), F32)
    c = jnp.concatenate([cos, cos, zero + 1.0], axis=1)
    a = jnp.concatenate([-sin, jnp.zeros((S, half), F32), zero], axis=1)
    b = jnp.concatenate([jnp.zeros((S, half), F32), sin, zero], axis=1)
    return jnp.stack([jnp.tile(t, (1, LANES // HEAD_DIM)) for t in (c, a, b)])


def _rope(t, tab):
    half = ROT_DIM // 2
    return t * tab[0] + pltpu.roll(t, LANES - half, 1) * tab[1] + pltpu.roll(t, half, 1) * tab[2]


def _rope_t(d, tab):
    half = ROT_DIM // 2
    return d * tab[0] + pltpu.roll(d * tab[1], half, 1) + pltpu.roll(d * tab[2], LANES - half, 1)


def _dup_head(t, hk, lo):
    sw = pltpu.roll(t, HEAD_DIM, 1)
    return jnp.where(lo, t, sw) if hk == 0 else jnp.where(lo, sw, t)


def _attn_common(n, sink_ref, q_ref, kp_ref, kc_ref, vp_ref, vc_ref, tc_ref, tp_ref, hk, pairs):
    tq = (tc_ref[0], tc_ref[1], tc_ref[2])
    tp = (tp_ref[0], tp_ref[1], tp_ref[2])
    lo = lax.broadcasted_iota(jnp.int32, (WINDOW, LANES), 1) < HEAD_DIM
    lo2 = lax.broadcasted_iota(jnp.int32, (2 * WINDOW, LANES), 1) < HEAD_DIM
    kband = jnp.concatenate([_rope(kp_ref[...], tp), _rope(kc_ref[...], tq)], axis=0)
    vband = jnp.concatenate([vp_ref[...], vc_ref[...]], axis=0)
    kd = _dup_head(kband, hk, lo2).astype(MXU_DTYPE)
    vd = _dup_head(vband, hk, lo2).astype(MXU_DTYPE)
    rows, sks = [], []
    for j in range(pairs):
        col = hk * pairs + j
        qp = _rope(q_ref[:, col * LANES:(col + 1) * LANES], tq)
        rows += [jnp.where(lo, qp, 0.0), jnp.where(lo, 0.0, qp)]
        sks += [jnp.full((WINDOW, 1), sink_ref[2 * col], F32), jnp.full((WINDOW, 1), sink_ref[2 * col + 1], F32)]
    qg = jnp.concatenate(rows, axis=0)
    sk = jnp.concatenate(sks, axis=0)
    G = 2 * pairs * WINDOW
    own = lax.broadcasted_iota(jnp.int32, (G, WINDOW), 1) <= (lax.broadcasted_iota(jnp.int32, (G, WINDOW), 0) & (WINDOW - 1))
    s = lax.dot_general(qg.astype(MXU_DTYPE), kd, (((1,), (1,)), ((), ())), preferred_element_type=F32) * (HEAD_DIM ** -0.5)
    s = jnp.where(own, s[:, WINDOW:], s[:, :WINDOW] + jnp.where(n > 0, 0.0, NEG_INF))
    m = jnp.maximum(jnp.max(s, axis=1, keepdims=True), sk)
    e = jnp.exp(s - m)
    es = jnp.exp(sk - m)
    inv = 1.0 / (jnp.sum(e, axis=1, keepdims=True) + es)
    return qg, kd, vd, e * inv, es * inv, own, lo, lo2, tq, tp


def _unfold_band(t, own):
    return jnp.concatenate([jnp.where(own, 0.0, t), jnp.where(own, t, 0.0)], axis=1)


def _attn_specs(D, NB):
    kcol = 3 * D // LANES
    q = pl.BlockSpec((WINDOW, D), lambda n: (n, 2))
    kc = pl.BlockSpec((WINDOW, LANES), lambda n: (n, kcol))
    kp = pl.BlockSpec((WINDOW, LANES), lambda n: (jnp.maximum(n - 1, 0), kcol))
    vc = pl.BlockSpec((WINDOW, LANES), lambda n: (n, kcol + 1))
    vp = pl.BlockSpec((WINDOW, LANES), lambda n: (jnp.maximum(n - 1, 0), kcol + 1))
    tc = pl.BlockSpec((3, WINDOW, LANES), lambda n: (0, n, 0))
    tp = pl.BlockSpec((3, WINDOW, LANES), lambda n: (0, jnp.maximum(n - 1, 0), 0))
    sink = pl.BlockSpec(memory_space=pltpu.SMEM)
    return [sink, q, kp, kc, vp, vc, tc, tp]


def attn_fwd(proj, sinks, tab, D, name):
    S = proj.shape[0]
    NB = S // WINDOW
    pairs = D // HEAD_DIM // N_KV_HEADS // 2

    def body(sink_ref, q_ref, kp_ref, kc_ref, vp_ref, vc_ref, tc_ref, tp_ref, o_ref):
        n = pl.program_id(0)
        for hk in range(N_KV_HEADS):
            _, _, vd, p, _, own, lo, _, _, _ = _attn_common(n, sink_ref, q_ref, kp_ref, kc_ref, vp_ref, vc_ref, tc_ref, tp_ref, hk, pairs)
            o = jnp.dot(_unfold_band(p, own).astype(MXU_DTYPE), vd, preferred_element_type=F32)
            for j in range(pairs):
                col = hk * pairs + j
                oa = o[(2 * j) * WINDOW:(2 * j + 1) * WINDOW]
                ob = o[(2 * j + 1) * WINDOW:(2 * j + 2) * WINDOW]
                o_ref[:, col * LANES:(col + 1) * LANES] = jnp.where(lo, oa, ob)

    return hbm_call(
        body, name=name, grid=(NB,), in_specs=_attn_specs(D, NB),
        out_specs=pl.BlockSpec((WINDOW, D), lambda n: (n, 0)), out_shape=jax.ShapeDtypeStruct((S, D), F32),
        compiler_params=_params(("parallel",)),
    )(sinks, proj, proj, proj, proj, proj, tab, tab)


def attn_bwd(proj, sinks, tab, o, do, D, name):
    S = proj.shape[0]
    NB = S // WINDOW
    pairs = D // HEAD_DIM // N_KV_HEADS // 2

    def body(sink_ref, q_ref, kp_ref, kc_ref, vp_ref, vc_ref, tc_ref, tp_ref, o_ref, do_ref, dq_ref, dk_ref, dv_ref, ds_ref):
        n = pl.program_id(0)

        @pl.when(n == 0)
        def _():
            ds_ref[...] = jnp.zeros_like(ds_ref)

        lane1 = lax.broadcasted_iota(jnp.int32, (1, LANES), 1)
        dsink = jnp.zeros((1, LANES), F32)
        dkt = dvt = None
        for hk in range(N_KV_HEADS):
            qg, kd, vd, p, ps, own, lo, lo2, tq, tp = _attn_common(n, sink_ref, q_ref, kp_ref, kc_ref, vp_ref, vc_ref, tc_ref, tp_ref, hk, pairs)
            dos, os_ = [], []
            for j in range(pairs):
                col = hk * pairs + j
                dop = do_ref[:, col * LANES:(col + 1) * LANES]
                op = o_ref[:, col * LANES:(col + 1) * LANES]
                dos += [jnp.where(lo, dop, 0.0), jnp.where(lo, 0.0, dop)]
                os_ += [jnp.where(lo, op, 0.0), jnp.where(lo, 0.0, op)]
            dog = jnp.concatenate(dos, axis=0)
            og = jnp.concatenate(os_, axis=0)
            dogm = dog.astype(MXU_DTYPE)
            dp = lax.dot_general(dogm, vd, (((1,), (1,)), ((), ())), preferred_element_type=F32)
            dp = jnp.where(own, dp[:, WINDOW:], dp[:, :WINDOW])
            dr = jnp.sum(dog * og, axis=1, keepdims=True)
            ds = _unfold_band(p * (dp - dr) * (HEAD_DIM ** -0.5), own)
            dsm = ds.astype(MXU_DTYPE)
            dqg = jnp.dot(dsm, kd, preferred_element_type=F32)
            dkd = jnp.dot(ds.T.astype(MXU_DTYPE), qg.astype(MXU_DTYPE), preferred_element_type=F32)
            dvd = jnp.dot(_unfold_band(p, own).T.astype(MXU_DTYPE), dogm, preferred_element_type=F32)
            dkf = dkd + pltpu.roll(dkd, HEAD_DIM, 1)
            dvf = dvd + pltpu.roll(dvd, HEAD_DIM, 1)
            if hk == 0:
                dkt, dvt = dkf, dvf
            else:
                dkt, dvt = jnp.where(lo2, dkt, dkf), jnp.where(lo2, dvt, dvf)
            sd = ps * dr
            for j in range(pairs):
                col = hk * pairs + j
                dqa = dqg[(2 * j) * WINDOW:(2 * j + 1) * WINDOW]
                dqb = dqg[(2 * j + 1) * WINDOW:(2 * j + 2) * WINDOW]
                dq_ref[:, col * LANES:(col + 1) * LANES] = _rope_t(jnp.where(lo, dqa, dqb), tq).astype(dq_ref.dtype)
                for t in range(2):
                    part = sd[(2 * j + t) * WINDOW:(2 * j + t + 1) * WINDOW]
                    val = jnp.sum(part, axis=0, keepdims=True)
                    dsink = dsink - jnp.where(lane1 == 2 * col + t, val, 0.0)
        dk_ref[...] = jnp.concatenate([_rope_t(dkt[:WINDOW], tp), _rope_t(dkt[WINDOW:], tq)], axis=0)
        dv_ref[...] = dvt
        ds_ref[...] += dsink

    blk = pl.BlockSpec((WINDOW, D), lambda n: (n, 0))
    band = pl.BlockSpec((None, 2 * WINDOW, LANES), lambda n: (n, 0, 0))
    return hbm_call(
        body, name=name, grid=(NB,), in_specs=_attn_specs(D, NB) + [blk, blk],
        out_specs=[blk, band, band, pl.BlockSpec((1, LANES), lambda n: (0, 0))],
        out_shape=[jax.ShapeDtypeStruct((S, D), MXU_DTYPE), jax.ShapeDtypeStruct((NB, 2 * WINDOW, LANES), F32),
                   jax.ShapeDtypeStruct((NB, 2 * WINDOW, LANES), F32), jax.ShapeDtypeStruct((1, LANES), F32)],
        compiler_params=_params(("arbitrary",)),
    )(sinks, proj, proj, proj, proj, proj, tab, tab, o, do)


def band_fold(dkb, dvb, name):
    NB = dkb.shape[0]
    k4 = dkb.reshape(NB, 2, WINDOW, LANES)
    v4 = dvb.reshape(NB, 2, WINDOW, LANES)

    def body(kc_ref, kn_ref, vc_ref, vn_ref, dk_ref, dv_ref):
        more = pl.program_id(0) < NB - 1
        dk_ref[...] = (kc_ref[...] + jnp.where(more, kn_ref[...], 0.0)).astype(dk_ref.dtype)
        dv_ref[...] = (vc_ref[...] + jnp.where(more, vn_ref[...], 0.0)).astype(dv_ref.dtype)

    cur = pl.BlockSpec((None, None, WINDOW, LANES), lambda n: (n, 1, 0, 0))
    nxt = pl.BlockSpec((None, None, WINDOW, LANES), lambda n: (jnp.minimum(n + 1, NB - 1), 0, 0, 0))
    out = pl.BlockSpec((WINDOW, LANES), lambda n: (n, 0))
    sds = jax.ShapeDtypeStruct((NB * WINDOW, LANES), MXU_DTYPE)
    return hbm_call(body, name=name, grid=(NB,), in_specs=[cur, nxt, cur, nxt], out_specs=[out, out], out_shape=[sds, sds],
                          compiler_params=_params(("parallel",)))(k4, k4, v4, v4)


CROSS_ROWS = 512


def _cross_probs(q, k, scale):
    s = lax.dot_general(q.astype(MXU_DTYPE), k.astype(MXU_DTYPE), (((1,), (1,)), ((), ())), preferred_element_type=F32) * scale
    e = jnp.exp(s - jnp.max(s, axis=1, keepdims=True))
    return e / jnp.sum(e, axis=1, keepdims=True)


def cross_fwd(qc, kv, name):
    S, D = qc.shape
    M = kv.shape[0]
    hd = D // CROSS_HEADS
    tq = min(CROSS_ROWS, S)

    def body(q_ref, kv_ref, o_ref):
        for h in range(CROSS_HEADS):
            p = _cross_probs(q_ref[:, h * hd:(h + 1) * hd], kv_ref[:, h * hd:(h + 1) * hd], hd ** -0.5)
            v = kv_ref[:, D + h * hd:D + (h + 1) * hd].astype(MXU_DTYPE)
            o_ref[:, h * hd:(h + 1) * hd] = jnp.dot(p.astype(MXU_DTYPE), v, preferred_element_type=F32).astype(o_ref.dtype)

    return hbm_call(
        body, name=name, grid=(S // tq,), in_specs=[pl.BlockSpec((tq, D), lambda i: (i, 0)), pl.BlockSpec((M, 2 * D), lambda i: (0, 0))],
        out_specs=pl.BlockSpec((tq, D), lambda i: (i, 0)), out_shape=jax.ShapeDtypeStruct((S, D), MXU_DTYPE),
        compiler_params=_params(("parallel",)),
    )(qc, kv)


def cross_bwd(qc, kv, do, name):
    S, D = qc.shape
    M = kv.shape[0]
    hd = D // CROSS_HEADS
    tq = min(CROSS_ROWS, S)

    def body(q_ref, kv_ref, do_ref, dq_ref, dkv_ref):
        @pl.when(pl.program_id(0) == 0)
        def _():
            dkv_ref[...] = jnp.zeros_like(dkv_ref)

        for h in range(CROSS_HEADS):
            q = q_ref[:, h * hd:(h + 1) * hd]
            k = kv_ref[:, h * hd:(h + 1) * hd]
            v = kv_ref[:, D + h * hd:D + (h + 1) * hd].astype(MXU_DTYPE)
            dom = do_ref[:, h * hd:(h + 1) * hd].astype(MXU_DTYPE)
            p = _cross_probs(q, k, hd ** -0.5)
            dp = lax.dot_general(dom, v, (((1,), (1,)), ((), ())), preferred_element_type=F32)
            ds = p * (dp - jnp.sum(p * dp, axis=1, keepdims=True)) * (hd ** -0.5)
            dq_ref[:, h * hd:(h + 1) * hd] = jnp.dot(ds.astype(MXU_DTYPE), k.astype(MXU_DTYPE),
                                                     preferred_element_type=F32).astype(dq_ref.dtype)
            dkv_ref[:, h * hd:(h + 1) * hd] += jnp.dot(ds.T.astype(MXU_DTYPE), q.astype(MXU_DTYPE), preferred_element_type=F32)
            dkv_ref[:, D + h * hd:D + (h + 1) * hd] += jnp.dot(p.T.astype(MXU_DTYPE), dom, preferred_element_type=F32)

    row = pl.BlockSpec((tq, D), lambda i: (i, 0))
    full = pl.BlockSpec((M, 2 * D), lambda i: (0, 0))
    return hbm_call(
        body, name=name, grid=(S // tq,), in_specs=[row, full, row], out_specs=[row, full],
        out_shape=[jax.ShapeDtypeStruct((S, D), MXU_DTYPE), jax.ShapeDtypeStruct((M, 2 * D), F32)],
        compiler_params=_params(("arbitrary",)),
    )(qc, kv, do)


def adamw(w, g, m, v, name):
    shape = w.shape
    cols = shape[-1]
    lead = shape[0] if len(shape) > 2 else 1
    rows = int(np.prod(shape[:-1])) // lead
    w2, g2, m2, v2 = (t.reshape(lead, rows, cols) for t in (w, g, m, v))
    tr = _divisors(rows, SUBLANES, max(SUBLANES, (1 << 20) // (cols * 4) // SUBLANES * SUBLANES))[0]

    def body(w_ref, g_ref, m_ref, v_ref, d_ref, mo_ref, vo_ref, go_ref):
        gg = g_ref[...]
        mn = ADAM_B1 * m_ref[...] + (1.0 - ADAM_B1) * gg
        vn = ADAM_B2 * v_ref[...] + (1.0 - ADAM_B2) * (gg * gg)
        m_hat = mn / (1.0 - ADAM_B1 ** ADAM_STEP)
        v_hat = vn / (1.0 - ADAM_B2 ** ADAM_STEP)
        d_ref[...] = -ADAM_LR * (m_hat / (jnp.sqrt(v_hat) + ADAM_EPS) + ADAM_WD * w_ref[...])
        mo_ref[...] = mn
        vo_ref[...] = vn
        go_ref[...] = gg

    blk = pl.BlockSpec((None, tr, cols), lambda l, i: (l, i, 0))
    sds = jax.ShapeDtypeStruct((lead, rows, cols), F32)
    d, mn, vn, go = hbm_call(body, name=name, grid=(lead, rows // tr), in_specs=[blk] * 4, out_specs=[blk] * 4, out_shape=[sds] * 4,
                             compiler_params=_params(("parallel", "parallel")))(w2, g2, m2, v2)
    return d.reshape(shape), mn.reshape(shape), vn.reshape(shape), go.reshape(shape)


def sum_devices(parts, name):
    n, rows, cols = parts.shape

    def body(p_ref, o_ref):
        acc = p_ref[0]
        for k in range(1, n):
            acc = acc + p_ref[k]
        o_ref[...] = acc

    return pl.pallas_call(body, name=name, in_specs=[pl.BlockSpec(memory_space=pltpu.VMEM)],
                          out_specs=pl.BlockSpec(memory_space=pltpu.VMEM), out_shape=jax.ShapeDtypeStruct((rows, cols), F32))(parts)


HBM_SPEC = pl.BlockSpec(memory_space=pltpu.HBM)


def _place():
    return lax.axis_index("x"), lax.axis_index("y"), lax.axis_index("c")


def _remote(src, dst, send_sems, recv_sems, k, to):
    return pltpu.make_async_remote_copy(src_ref=src, dst_ref=dst, send_sem=send_sems.at[k], recv_sem=recv_sems.at[k],
                                        device_id=to, device_id_type=MESH_ID)


SEM_SPEC = pl.BlockSpec(memory_space=pltpu.SEMAPHORE)
ANY_SPEC = pl.BlockSpec(memory_space=pl.ANY)
SPLIT_COPY = pltpu.CompilerParams(has_side_effects=pltpu.SideEffectType.DATAFLOW_SIDE_EFFECTING)


def _in_hbm(arrays):
    return [pltpu.with_memory_space_constraint(a, pltpu.HBM) for a in arrays]


def _split_start(copies, sources, lands, after, n_sems, name):
    n = len(sources)

    def body(*refs):
        for cp in copies(refs[:n], refs[n:2 * n], refs[2 * n + 1], refs[2 * n + 2]):
            cp.start()
        refs[-1][...] = jnp.zeros_like(refs[-1])

    through = [pltpu.HBM(a.shape, a.dtype) for a in list(sources) + list(lands)]
    outs = pl.pallas_call(
        body, name=name, in_specs=[HBM_SPEC] * (2 * n) + [ANY_SPEC],
        out_specs=[SEM_SPEC, SEM_SPEC] + [HBM_SPEC] * (2 * n) + [pl.BlockSpec(memory_space=pltpu.VMEM)],
        out_shape=[pltpu.SemaphoreType.DMA((n_sems,)), pltpu.SemaphoreType.DMA((n_sems,))] + through
        + [jax.ShapeDtypeStruct((SUBLANES, LANES), F32)],
        input_output_aliases={i: 2 + i for i in range(2 * n)}, compiler_params=SPLIT_COPY,
    )(*_in_hbm(sources), *_in_hbm(lands), after)
    return outs[0], outs[1], outs[2:2 + n], outs[2 + n:2 + 2 * n], outs[-1]


def _split_wait(copies, send_sems, recv_sems, sources, lands, after, name):
    n = len(sources)

    def body(*refs):
        for cp in copies(refs[:n], refs[n:2 * n], refs[2 * n], refs[2 * n + 1]):
            cp.wait_send()
            cp.wait_recv()

    through = [pltpu.HBM(a.shape, a.dtype) for a in list(sources) + list(lands)]
    outs = pl.pallas_call(
        body, name=name, in_specs=[HBM_SPEC] * (2 * n) + [SEM_SPEC, SEM_SPEC, ANY_SPEC], out_specs=[HBM_SPEC] * (2 * n),
        out_shape=through, input_output_aliases={i: i for i in range(2 * n)}, compiler_params=SPLIT_COPY,
    )(*sources, *lands, send_sems, recv_sems, after)
    return outs[:n], outs[n:]


def _chip_slab(land, slot, rows):
    return land.at[slot, rows] if len(land.shape) == 3 else land.at[rows, slot]


def _gather_copies(w_refs, land_refs, send_sems, recv_sems):
    n = len(w_refs)
    x, y, c = _place()
    chips = [(1 - x, y), (x, 1 - y), (1 - x, 1 - y)]
    cps = []
    for a in range(n):
        hr = w_refs[a].shape[0] // 2
        mine, every = pl.ds(c * hr, hr), pl.ds(0, 2 * hr)
        cps.append(_remote(w_refs[a], _chip_slab(land_refs[a], 2 * x + y, every), send_sems, recv_sems, 3 * n + a, (x, y, 1 - c)))
        for k, chip in enumerate(chips):
            cps.append(_remote(w_refs[a].at[mine], _chip_slab(land_refs[a], 2 * x + y, mine), send_sems, recv_sems, 3 * a + k, (*chip, c)))
    return cps


def gather_start(shards, after, name):
    lands = [lax.empty(s.shape[:-2] + (N_CHIPS,) + s.shape[-2:], s.dtype) for s in shards]
    return _split_start(_gather_copies, shards, lands, after, 4 * len(shards), name)


def gather_wait(state, after, name):
    send_sems, recv_sems, sources, lands, _ = state
    return _split_wait(_gather_copies, send_sems, recv_sems, sources, lands, after, name)[1]


def gather_pass(lands, name):
    n = len(lands)

    def body(*refs):
        out_refs, send_sems, recv_sems = refs[n:2 * n], refs[2 * n], refs[2 * n + 1]
        x, y, c = _place()
        chips = [(1 - x, y), (x, 1 - y), (1 - x, 1 - y)]
        sent = []
        for a in range(n):
            hr = out_refs[a].shape[0 if len(out_refs[a].shape) == 4 else 1] // 2
            for k, (px, py) in enumerate(chips):
                landed = _chip_slab(out_refs[a], 2 * px + py, pl.ds(c * hr, hr))
                sent.append(_remote(landed, landed, send_sems, recv_sems, 3 * a + k, (x, y, 1 - c)))
        for cp in sent:
            cp.start()
        for a in range(n):
            hr = out_refs[a].shape[0 if len(out_refs[a].shape) == 4 else 1] // 2
            for k, (px, py) in enumerate(chips):
                theirs = _chip_slab(out_refs[a], 2 * px + py, pl.ds((1 - c) * hr, hr))
                _remote(theirs, theirs, send_sems, recv_sems, 3 * a + k, (x, y, 1 - c)).wait_recv()
        for cp in sent:
            cp.wait_send()

    return hbm_call(
        body, name=name, in_specs=[HBM_SPEC] * n, out_specs=[HBM_SPEC] * n,
        out_shape=[jax.ShapeDtypeStruct(a.shape, a.dtype) for a in lands], input_output_aliases={a: a for a in range(n)},
        scratch_shapes=[pltpu.SemaphoreType.DMA((3 * n,))] * 2,
    )(*lands)


def _scatter_copies(t_refs, land_refs, send_sems, recv_sems):
    x, y, c = _place()
    chips = [(1 - x, y), (x, 1 - y), (1 - x, 1 - y)]
    return [_remote(t_refs[a].at[:, 2 * px + py], land_refs[a].at[:, k], send_sems, recv_sems, 3 * a + k, (px, py, c))
            for a in range(len(t_refs)) for k, (px, py) in enumerate(chips)]


def scatter_start(parts, after, name):
    lands = [lax.empty((t.shape[0], N_CHIPS - 1) + t.shape[2:], t.dtype) for t in parts]
    return _split_start(_scatter_copies, parts, lands, after, 3 * len(parts), name)


def scatter_wait(state, after, name):
    send_sems, recv_sems, sources, lands, _ = state
    return _split_wait(_scatter_copies, send_sems, recv_sems, sources, lands, after, name)


def swap_sibling(parts, name):
    n = len(parts)

    def body(*refs):
        v_refs, out_refs, send_sems, recv_sems = refs[:n], refs[n:2 * n], refs[2 * n], refs[2 * n + 1]
        x, y, c = _place()
        cps = []
        for a in range(n):
            hr = v_refs[a].shape[2] // 2
            cps.append(_remote(v_refs[a].at[:, :, pl.ds((1 - c) * hr, hr)], out_refs[a], send_sems, recv_sems, a, (x, y, 1 - c)))
        for cp in cps:
            cp.start()
        for cp in cps:
            cp.wait()

    return hbm_call(
        body, name=name, in_specs=[HBM_SPEC] * n, out_specs=[HBM_SPEC] * n,
        out_shape=[jax.ShapeDtypeStruct(v.shape[:2] + (v.shape[2] // 2, v.shape[3]), v.dtype) for v in parts],
        scratch_shapes=[pltpu.SemaphoreType.DMA((n,))] * 2,
    )(*parts)


def join_halves(halves, layer, name):
    n = len(halves)

    def body(*refs):
        out_refs, send_sems, recv_sems = refs[n:2 * n], refs[2 * n], refs[2 * n + 1]
        x, y, c = _place()
        cps = []
        for a in range(n):
            hr = out_refs[a].shape[1] // 2
            mine = out_refs[a].at[layer, pl.ds(c * hr, hr)]
            cps.append(_remote(mine, mine, send_sems, recv_sems, a, (x, y, 1 - c)))
        for cp in cps:
            cp.start()
        for a in range(n):
            hr = out_refs[a].shape[1] // 2
            theirs = out_refs[a].at[layer, pl.ds((1 - c) * hr, hr)]
            _remote(theirs, theirs, send_sems, recv_sems, a, (x, y, 1 - c)).wait_recv()
        for cp in cps:
            cp.wait_send()

    return hbm_call(
        body, name=name, in_specs=[HBM_SPEC] * n, out_specs=[HBM_SPEC] * n,
        out_shape=[jax.ShapeDtypeStruct(f.shape, f.dtype) for f in halves], input_output_aliases={a: a for a in range(n)},
        scratch_shapes=[pltpu.SemaphoreType.DMA((n,))] * 2,
    )(*halves)


def gather_devices(v, name):
    def body(v_ref, out_ref, send_sems, recv_sems, local_sem):
        x, y, c = _place()
        me = 4 * x + 2 * y + c
        own = pltpu.make_async_copy(v_ref, out_ref.at[me], local_sem)
        own.start()
        peers = [((x + dx) % 2, (y + dy) % 2, (c + dc) % 2) for dx in (0, 1) for dy in (0, 1) for dc in (0, 1)][1:]
        sent = []
        for k, peer in enumerate(peers):
            cp = pltpu.make_async_remote_copy(src_ref=v_ref, dst_ref=out_ref.at[me], send_sem=send_sems.at[k], recv_sem=recv_sems.at[k],
                                              device_id=peer, device_id_type=MESH_ID)
            cp.start()
            sent.append(cp)
        for k, (px, py, pc) in enumerate(peers):
            slot = out_ref.at[4 * px + 2 * py + pc]
            pltpu.make_async_remote_copy(src_ref=slot, dst_ref=slot, send_sem=send_sems.at[k], recv_sem=recv_sems.at[k],
                                         device_id=(px, py, pc), device_id_type=MESH_ID).wait_recv()
        for cp in sent:
            cp.wait_send()
        own.wait()

    vm = pl.BlockSpec(memory_space=pltpu.VMEM)
    return pl.pallas_call(body, name=name, in_specs=[vm], out_specs=vm, out_shape=jax.ShapeDtypeStruct((N_DEV,) + v.shape, v.dtype),
                          scratch_shapes=[pltpu.SemaphoreType.DMA((N_DEV - 1,)), pltpu.SemaphoreType.DMA((N_DEV - 1,)),
                                          pltpu.SemaphoreType.DMA])(v)


ADD_ROWS = 512


def add_pair(place, a, b, name):
    L, n, hr, cols = b.shape
    tr = _divisors(hr, 2 * SUBLANES, ADD_ROWS)[0]
    nb = hr // tr

    def body(p_ref, a_ref, b_ref, o_ref):
        del p_ref
        o_ref[...] = (a_ref[...].astype(F32) + b_ref[...].astype(F32)).astype(o_ref.dtype)

    blk = pl.BlockSpec((None, None, tr, cols), lambda l, d, i, p: (l, d, i, 0))
    grid_spec = pltpu.PrefetchScalarGridSpec(
        num_scalar_prefetch=1, grid=(L, n, nb),
        in_specs=[pl.BlockSpec((None, None, tr, cols), lambda l, d, i, p: (l, d, p[0] * nb + i, 0)), blk], out_specs=blk)
    return hbm_call(body, name=name, grid_spec=grid_spec, out_shape=jax.ShapeDtypeStruct(b.shape, b.dtype),
                          compiler_params=_params(("parallel", "parallel", "parallel")))(place, a, b)


def add_chips(place, own, others, layer, stacked, name):
    _, n, hr, cols = others.shape
    tr = _divisors(hr, 2 * SUBLANES, ADD_ROWS)[0]
    nb = hr // tr
    create = isinstance(stacked, tuple)

    def body(p_ref, own_ref, *refs):
        del p_ref
        acc = own_ref[...].astype(F32)
        for k in range(n):
            acc = acc + refs[k][...].astype(F32)
        refs[-1][...] = acc

    ins = [pl.BlockSpec((None, None, tr, cols), lambda i, p: (0, p[1], i, 0))]
    ins += [pl.BlockSpec((None, None, tr, cols), functools.partial(lambda k, i, p: (0, k, i, 0), k)) for k in range(n)]
    grid_spec = pltpu.PrefetchScalarGridSpec(num_scalar_prefetch=1, grid=(nb,), in_specs=ins + ([] if create else [ANY_SPEC]),
                                             out_specs=pl.BlockSpec((None, tr, cols), lambda i, p: (layer, p[0] * nb + i, 0)))
    shape = stacked if create else stacked.shape
    return hbm_call(body, name=name, grid_spec=grid_spec, out_shape=jax.ShapeDtypeStruct(shape, F32),
                          input_output_aliases={} if create else {n + 2: 0},
                          compiler_params=_params(("parallel",)))(place, own, *([others] * n), *([] if create else [stacked]))


def _alpha(depth):
    return (2 * depth) ** 0.25


def _wmm(a, weight, mode, name, deps=(), **more):
    arr, how = weight
    return mm(a, arr, mode, name, deps=deps, **how, **more)


def layer_fwd(h, mem, w, tab, alpha, deps=(), late=None):
    D = h.shape[1]
    proj = _wmm(h, w["w_in"], "nn", "mm_proj", deps)
    xc, r, ig, a, b = rg_gates_fwd(proj, w["conv_w"], w["conv_b"], w["w_rg"], w["b_rg"], w["w_ig"], w["b_ig"], w["lru_lambda"], "rg_gates_fwd")
    hs, y_rnn = rg_scan_fwd(proj, a, b, "rg_scan_fwd")
    y_attn = attn_fwd(proj, w["sinks"], tab, D, "attn_fwd")
    deps = ()
    if late is not None:
        rest, deps = late(y_attn)
        w = {**w, **rest}
    pr = _wmm(y_rnn, w["w_br_rnn"], "nn", "mm_br_rnn", deps)
    pa = _wmm(y_attn, w["w_br_attn"], "nn", "mm_br_attn")
    merged = merge_fwd(proj, pr, pa, "merge_fwd")
    h1, xh1, rs1 = _wmm(merged, w["w_out"], "nn", "mm_out_ln1", post_norm=(h, w["ln1_g"], w["ln1_b"], alpha))
    qc = _wmm(h1, w["cq_w"], "nn", "mm_cq", out_dtype=MXU_DTYPE)
    kv = _wmm(mem, w["ckv_w"], "nn", "mm_ckv", out_dtype=MXU_DTYPE)
    o = cross_fwd(qc, kv, "cross_fwd")
    h2, xh2, rs2 = _wmm(o, w["co_w"], "nn", "mm_co_ln2", post_norm=(h1, w["ln2_g"], w["ln2_b"], alpha))
    gu = _wmm(h2, w["ffn_wi"], "nn", "mm_ffn_wi", out_blocks=2)
    act = swiglu_fwd(gu, "swiglu_fwd")
    h3, xh3, rs3 = _wmm(act, w["ffn_wo"], "nn", "mm_ffn_wo_ln3", post_norm=(h2, w["ln3_g"], w["ln3_b"], alpha))
    saved = dict(h=h, proj=proj, xc=xc, r=r, ig=ig, a=a, hs=hs, y_rnn=y_rnn, y_attn=y_attn, pr=pr, pa=pa, xh1=xh1, rs1=rs1, h1=h1,
                 qc=qc, kv=kv, o=o, xh2=xh2, rs2=rs2, h2=h2, gu=gu, xh3=xh3, rs3=rs3)
    return h3, saved, w


def layer_bwd(dh, mem, w, s, tab, alpha, deps=(), halfway=None):
    D = dh.shape[1]
    g = {}
    wg = dict(out_dtype=MXU_DTYPE)
    dz3, g["ln3_g"], g["ln3_b"] = ln_bwd(dh, None, s["xh3"], s["rs3"], w["ln3_g"], 1.0, "ln3_bwd")
    act = swiglu_fwd(s["gu"], "swiglu_refwd")
    g["ffn_wo"] = mm(act, dz3, "tn", "mm_d_ffn_wo", deps=deps, **wg)
    dact = _wmm(dz3, w["ffn_wo"], "nt", "mm_dact")
    dgu = swiglu_bwd(s["gu"], dact, "swiglu_bwd")
    g["ffn_wi"] = mm(s["h2"], dgu, "tn", "mm_d_ffn_wi", b_blocks=2, out_blocks=N_CHIPS, **wg)
    dh2 = _wmm(dgu, w["ffn_wi"], "nt", "mm_dh2", a_blocks=2)
    dz2, g["ln2_g"], g["ln2_b"] = ln_bwd(dz3, dh2, s["xh2"], s["rs2"], w["ln2_g"], alpha, "ln2_bwd")
    g["co_w"] = mm(s["o"], dz2, "tn", "mm_d_co", **wg)
    do = _wmm(dz2, w["co_w"], "nt", "mm_do", out_dtype=MXU_DTYPE)
    dqc, dkv = cross_bwd(s["qc"], s["kv"], do, "cross_bwd")
    g["cq_w"] = mm(s["h1"], dqc, "tn", "mm_d_cq", **wg)
    g["ckv_w"] = mm(mem, dkv, "tn", "mm_d_ckv", out_blocks=N_CHIPS, **wg)
    dh1 = _wmm(dqc, w["cq_w"], "nt", "mm_dh1")
    deps = halfway(g, dh1) if halfway is not None else ()
    dz1, g["ln1_g"], g["ln1_b"] = ln_bwd(dz2, dh1, s["xh1"], s["rs1"], w["ln1_g"], alpha, "ln1_bwd")
    merged = merge_fwd(s["proj"], s["pr"], s["pa"], "merge_refwd")
    g["w_out"] = mm(merged, dz1, "tn", "mm_d_out", deps=deps, **wg)
    dm = _wmm(dz1, w["w_out"], "nt", "mm_dmerged")
    dpr, dpa, dg_rnn, dg_attn = merge_bwd(s["proj"], s["pr"], s["pa"], dm, "merge_bwd")
    g["w_br_rnn"] = mm(s["y_rnn"], dpr, "tn", "mm_d_br_rnn", **wg)
    g["w_br_attn"] = mm(s["y_attn"], dpa, "tn", "mm_d_br_attn", **wg)
    dy_rnn = _wmm(dpr, w["w_br_rnn"], "nt", "mm_dy_rnn")
    dy_attn = _wmm(dpa, w["w_br_attn"], "nt", "mm_dy_attn")
    dq, dkb, dvb, dsink = attn_bwd(s["proj"], w["sinks"], tab, s["y_attn"], dy_attn, D, "attn_bwd")
    dk, dv = band_fold(dkb, dvb, "band_fold")
    g["sinks"] = dsink[:, :w["sinks"].shape[0]]
    dgr, gt = rg_scan_bwd(s["proj"], dy_rnn, s["hs"], s["a"], "rg_scan_bwd")
    dxc, g["w_rg"], g["w_ig"], g["b_rg"], g["b_ig"], g["lru_lambda"] = rg_gates_bwd(
        gt, s["hs"], s["xc"], s["r"], s["ig"], w["w_rg"], w["w_ig"], w["lru_lambda"], "rg_gates_bwd")
    dxr, g["conv_w"], g["conv_b"] = rg_conv_bwd(s["proj"], dxc, w["conv_w"], "rg_conv_bwd")
    dproj = jnp.concatenate([dxr, dgr, dq, dk, dv, dg_rnn, dg_attn], axis=1)
    g["w_in"] = mm(s["h"], dproj, "tn", "mm_d_in")
    dhm = _wmm(dproj, w["w_in"], "nt", "mm_dh")
    return axpby(dz1, dhm, alpha, "layer_dx"), g


def local_step(x, mem, target, depth, weights_of, grads_halfway, grads_done):
    alpha = _alpha(depth)
    tab = rope_table(x.shape[0])
    h, saved, layers = x, [], []
    for l in range(depth):
        wl, deps, late = weights_of(l, h)
        h, s, wl = layer_fwd(h, mem, wl, tab, alpha, deps, late)
        layers.append(wl)
        saved.append(s)
    dh, loss = loss_head(h, target, "loss_head")
    deps = ()
    for l in reversed(range(depth)):
        dh, g = layer_bwd(dh, mem, layers[l], saved[l], tab, alpha, deps, grads_halfway(l))
        deps = grads_done(l, g, dh)
    return loss, dh


def _pad_rows(flat):
    n = flat.shape[0]
    rows = -(-n // (LANES * SUBLANES)) * SUBLANES
    return jnp.pad(flat, (0, rows * LANES - n)).reshape(rows, LANES)


def kernel(x, mem, w_in, conv_w, conv_b, w_rg, b_rg, w_ig, b_ig, lru_lambda, w_br_rnn, w_br_attn, sinks, w_out, ln1_g, ln1_b, cq_w, ckv_w, co_w, ln2_g, ln2_b, ffn_wi, ffn_wo, ln3_g, ln3_b, loss_target, m_w_in, m_conv_w, m_conv_b, m_w_rg, m_b_rg, m_w_ig, m_b_ig, m_lru_lambda, m_w_br_rnn, m_w_br_attn, m_sinks, m_w_out, m_ln1_g, m_ln1_b, m_cq_w, m_ckv_w, m_co_w, m_ln2_g, m_ln2_b, m_ffn_wi, m_ffn_wo, m_ln3_g, m_ln3_b, v_w_in, v_conv_w, v_conv_b, v_w_rg, v_b_rg, v_w_ig, v_b_ig, v_lru_lambda, v_w_br_rnn, v_w_br_attn, v_sinks, v_w_out, v_ln1_g, v_ln1_b, v_cq_w, v_ckv_w, v_co_w, v_ln2_g, v_ln2_b, v_ffn_wi, v_ffn_wo, v_ln3_g, v_ln3_b):
    args = dict(locals())
    w = {n: args[n] for n in WEIGHTS}
    m = {n: args["m_" + n] for n in WEIGHTS}
    v = {n: args["v_" + n] for n in WEIGHTS}
    cx, cy, cc = _place()
    chip = 2 * cx + cy
    L = w_in.shape[0]

    place = jnp.stack([cc, chip]).astype(jnp.int32)
    cw_rows = _pad_rows(conv_w.reshape(-1))
    cw_all = gather_devices(cw_rows, "gather_conv_w")[0::2]
    cw_parts = cw_all.reshape(N_CHIPS, -1)[:, :conv_w.size].reshape((N_CHIPS,) + conv_w.shape)
    conv_full = jnp.concatenate([cw_parts[k] for k in range(N_CHIPS)], axis=2)

    shards = [{n: w[n][l].astype(MXU_DTYPE) for n in BIG} for l in range(L)]
    late_names = tuple(n for n in BIG if n not in GATHER_FIRST)
    gathering = {(0, GATHER_FIRST): gather_start([shards[0][n] for n in GATHER_FIRST], cw_rows, "gather_start_0a")}
    gathering[0, late_names] = gather_start([shards[0][n] for n in late_names], gathering[0, GATHER_FIRST][4], "gather_start_0b")

    def gathered(l, names, after, tag):
        lands = gather_pass(gather_wait(gathering.pop((l, names)), after, f"gather_wait_{tag}"), f"gather_pass_{tag}")
        wl = {}
        for n, gw in zip(names, lands):
            rows_joined = gw.reshape(gw.shape[:-3] + (-1, gw.shape[-1]))
            if n == "w_in":
                wl[n] = (jnp.concatenate([gw[k] for k in range(N_CHIPS)], axis=1), {})
            elif n in COL_BLOCKED:
                wl[n] = (gw, dict(b_blocks=N_CHIPS))
            elif n in GATE_WEIGHTS:
                wl[n] = rows_joined
            else:
                wl[n] = (rows_joined, {})
        return wl, lands

    def start_next(l, after):
        if l + 1 == L:
            return ()
        gathering[l + 1, BIG] = gather_start([shards[l + 1][n] for n in BIG], after, f"gather_start_{l + 1}")
        return (gathering[l + 1, BIG][4],)

    def weights_of(l, h):
        deps, late = (), None
        if l == 0:
            wl, _ = gathered(0, GATHER_FIRST, h, "0a")

            def late(after):
                rest, lands = gathered(0, late_names, after, "0b")
                return rest, start_next(0, lands[0])
        else:
            wl, lands = gathered(l, BIG, h, str(l))
            deps = start_next(l, lands[0])
        for n in SMALL:
            wl[n] = conv_full[l] if n == "conv_w" else w[n][l] if n == "sinks" else w[n][l][None, :]
        return wl, deps, late

    def for_chips(n, g):
        if n in COL_BLOCKED:
            return g
        if n in GATE_WEIGHTS:
            nb, bw, _ = g.shape
            g = g.reshape(nb, N_CHIPS, bw // N_CHIPS, bw).transpose(1, 0, 2, 3).reshape(N_CHIPS, nb * bw // N_CHIPS, bw)
        elif SHARD_AXIS[n] == 0:
            g = g.reshape(N_CHIPS, g.shape[0] // N_CHIPS, g.shape[1])
        else:
            g = jnp.stack(jnp.split(g, N_CHIPS, axis=1))
        return g.astype(MXU_DTYPE)

    reduced, scattering, small_grads = {}, {}, [None] * L
    late_grads = tuple(n for n in BIG if n not in SCATTER_FIRST)

    def start_scatter(l, names, g, after, tag):
        partial_sums = [for_chips(n, g[n])[None] for n in names]
        from_sibling = swap_sibling(partial_sums, f"grad_to_sibling_{tag}")
        chip_sums = [add_pair(place, a, b, f"grad_add_pair_{n}_{l}") for n, a, b in zip(names, partial_sums, from_sibling)]
        scattering[l, names] = scatter_start(chip_sums, after, f"grad_scatter_start_{tag}")
        return (scattering[l, names][4],)

    def finish_layer(l, after):
        for names in [k[1] for k in list(scattering) if k[0] == l]:
            tag = str(l) if names == BIG else f"{l}{'a' if names == SCATTER_FIRST else 'b'}"
            chip_sums, from_chips = scatter_wait(scattering.pop((l, names)), after, f"grad_scatter_wait_{tag}")
            for n, own, others in zip(names, chip_sums, from_chips):
                target = reduced.get(n, (L, 2 * own.shape[2], own.shape[3]))
                reduced[n] = add_chips(place, own, others, l, target, f"grad_add_chips_{n}_{l}")
        reduced.update(zip(BIG, join_halves([reduced[n] for n in BIG], l, f"grad_join_{l}")))

    def grads_halfway(l):
        if l > 0:
            return None

        def halfway(g, after):
            if L > 1:
                finish_layer(1, after)
            return start_scatter(0, SCATTER_FIRST, g, after, "0a")

        return halfway

    def grads_done(l, g, dh):
        small_grads[l] = {n: g[n] for n in SMALL}
        if l == 0:
            return start_scatter(0, late_grads, g, dh, "0b")
        if l + 1 < L:
            finish_layer(l + 1, dh)
        return start_scatter(l, BIG, g, dh, str(l))

    loss11, dx = local_step(x[0], mem[0], loss_target[0], L, weights_of, grads_halfway, grads_done)
    finish_layer(0, dx)
    loss = lax.psum(loss11[0, 0], ("x", "y", "c"))
    gshard = {n: reduced[n].reshape(w[n].shape) for n in BIG}

    small_full = {n: jnp.stack([gl[n] for gl in small_grads]).reshape(w[n].shape[:1] + ((CONV_WIDTH, -1) if n == "conv_w" else (-1,)))
                  for n in SMALL}
    small_flat = jnp.concatenate([small_full[n].reshape(-1) for n in SMALL])
    small_sum = sum_devices(gather_devices(_pad_rows(small_flat), "gather_small_grads"), "sum_small_grads").reshape(-1)
    off = 0
    for n in SMALL:
        gfull = small_sum[off:off + small_full[n].size].reshape(small_full[n].shape)
        off += small_full[n].size
        if n == "conv_w":
            width = conv_w.shape[2]
            gfull = lax.dynamic_slice_in_dim(gfull, chip * width, width, axis=2)
        gshard[n] = gfull

    delta, new_m, new_v, grad = {}, {}, {}, {}
    for n in WEIGHTS:
        delta[n], new_m[n], new_v[n], grad[n] = adamw(w[n], gshard[n], m[n], v[n], "adamw_" + n)
    return (loss, dx[None], *[grad[n] for n in WEIGHTS], *[delta[n] for n in WEIGHTS], *[new_m[n] for n in WEIGHTS],
            *[new_v[n] for n in WEIGHTS])
```

```python
import functools
import math

import jax
import jax.numpy as jnp
import numpy as np
from jax import lax
from jax.experimental import pallas as pl
from jax.experimental.pallas import tpu as pltpu

F32 = jnp.float32
BF16 = jnp.bfloat16
MXU_DTYPE = BF16

HEAD_DIM = 64
N_KV_HEADS = 2
WINDOW = 128
ROT_DIM = HEAD_DIM // 4
ROPE_THETA = 500000.0
CROSS_HEADS = 4
RNN_BLOCKS = 4
CONV_WIDTH = 4
LRU_C = 8.0
LN_EPS = 1e-5
NEG_INF = -1e30
ADAM_LR = 0.001
ADAM_B1 = 0.9
ADAM_B2 = 0.999
ADAM_EPS = 1e-08
ADAM_WD = 0.01
ADAM_STEP = 10

VMEM_BYTES_V7X = 64 * 1024 * 1024
VMEM_BLOCK_BUDGET = 36 * 1024 * 1024
LANES = 128
SUBLANES = 8

MESH_ID = pl.DeviceIdType.MESH
N_CHIPS = 4
N_DEV = 8

BIG = ("w_in", "w_rg", "w_ig", "w_br_rnn", "w_br_attn", "w_out", "cq_w", "ckv_w", "co_w", "ffn_wi", "ffn_wo")
SHARD_AXIS = {"w_in": 1, "w_rg": 1, "w_ig": 1, "w_br_rnn": 0, "w_br_attn": 0, "w_out": 0, "cq_w": 0, "ckv_w": 1,
              "co_w": 0, "ffn_wi": 1, "ffn_wo": 0}
SMALL = ("conv_w", "conv_b", "b_rg", "b_ig", "lru_lambda", "sinks", "ln1_g", "ln1_b", "ln2_g", "ln2_b", "ln3_g", "ln3_b")
WEIGHTS = ("w_in", "conv_w", "conv_b", "w_rg", "b_rg", "w_ig", "b_ig", "lru_lambda", "w_br_rnn", "w_br_attn", "sinks",
           "w_out", "ln1_g", "ln1_b", "cq_w", "ckv_w", "co_w", "ln2_g", "ln2_b", "ffn_wi", "ffn_wo", "ln3_g", "ln3_b")
GATE_WEIGHTS = ("w_rg", "w_ig")
COL_BLOCKED = ("ckv_w", "ffn_wi")
GATHER_FIRST = ("w_in", "w_rg", "w_ig")
SCATTER_FIRST = ("ffn_wo", "ffn_wi", "co_w", "cq_w", "ckv_w")


def _params(dims=None, vmem=None):
    return pltpu.CompilerParams(dimension_semantics=dims, vmem_limit_bytes=vmem)


def _vmem_limit(block_bytes, temp_bytes=0):
    want = int(2 * block_bytes + temp_bytes) + (6 << 20)
    return max(32 << 20, min(want, VMEM_BYTES_V7X - (6 << 20)))


def _divisors(n, align, cap):
    out = [d for d in range(align, min(n, cap) + 1, align) if n % d == 0]
    if n <= cap and n not in out:
        out.append(n)
    return sorted(out, reverse=True) or [n]


PIN_MIN_ELEMENTS = 1 << 18


def hbm_call(body, **kw):
    def in_hbm(s):
        return pltpu.HBM(s.shape, s.dtype) if math.prod(s.shape) >= PIN_MIN_ELEMENTS else s

    shapes = kw.pop("out_shape")
    shapes = [in_hbm(s) for s in shapes] if isinstance(shapes, (list, tuple)) else in_hbm(shapes)
    call = pl.pallas_call(body, out_shape=shapes, **kw)

    def run(*args):
        return call(*[pltpu.with_memory_space_constraint(a, pltpu.HBM) if a.size >= PIN_MIN_ELEMENTS else a for a in args])

    return run


def _sigmoid(x):
    return 1.0 / (1.0 + jnp.exp(-x))


def _gelu_parts(x):
    c = math.sqrt(2.0 / math.pi)
    u = c * (x + 0.044715 * x * x * x)
    t = jnp.tanh(u)
    return t, c * (1.0 + 3 * 0.044715 * x * x)


def _gelu(x):
    t, _ = _gelu_parts(x)
    return 0.5 * x * (1.0 + t)


def _gelu_grad(x):
    t, du = _gelu_parts(x)
    return 0.5 * (1.0 + t) + 0.5 * x * (1.0 - t * t) * du


def _neg_expm1(x):
    series = x * (1.0 + x * (0.5 + x * (1.0 / 6 + x * (1.0 / 24 + x * (1.0 / 120)))))
    return -jnp.where(x > -0.1, series, jnp.exp(x) - 1.0)


def _softplus_neg(lam):
    x = -lam
    return jnp.maximum(x, 0.0) + jnp.log1p(jnp.exp(-jnp.abs(x)))


STEP_US = 0.35
HBM_BYTES_PER_US = 2.5e6
MXU_FLOPS_PER_US = 7e8


def _layer_norm(z, g, b):
    mu = jnp.mean(z, axis=-1, keepdims=True)
    zc = z - mu
    rs = lax.rsqrt(jnp.mean(zc * zc, axis=-1, keepdims=True) + LN_EPS)
    xh = zc * rs
    return xh * g + b, xh, rs


def mm(a, b, mode, name, *, b_index=(), a_blocks=0, b_blocks=0, out_blocks=0, out_dtype=F32, deps=(), post_norm=None):
    nlead = len(b_index) + (1 if b_blocks else 0)
    bk, bn = b.shape[nlead:]
    M, K = (a.shape[-1], a.shape[-2]) if mode == "tn" else (a.shape[-2], a.shape[-1] * max(a_blocks, 1))
    N = bk if mode == "nt" else bn * max(b_blocks, 1) if mode == "nn" or mode == "tn" else bn
    asz, bsz, osz = a.dtype.itemsize, b.dtype.itemsize, jnp.dtype(out_dtype).itemsize
    n_unit = math.gcd(N // max(out_blocks, 1), N // max(b_blocks, 1) if mode != "nt" else N)
    k_unit = math.gcd(K // max(a_blocks, 1), K // max(b_blocks, 1) if mode == "nt" else K)
    tms = _divisors(M, LANES if mode == "tn" else SUBLANES, 2048)
    tns = [N] if post_norm else _divisors(n_unit, LANES, 2048)
    tks = _divisors(k_unit, LANES, k_unit)
    best = None
    for tm in tms:
        for tn in tns:
            for tk in tks:
                nk = K // tk
                scratch = tm * tn * 4 if (nk > 1 and osz != 4) else 0
                blocks = tm * tk * asz + tn * tk * bsz + tm * tn * osz * (3 if post_norm else 1)
                temps = tm * tk * (2 + (4 if mode == "tn" else 0)) + tn * tk * 2 + tm * tn * 4 + scratch
                if 2 * blocks + temps > VMEM_BLOCK_BUDGET + (8 << 20):
                    continue
                ni, nj = M // tm, N // tn
                traffic = M * K * asz * (nj if nk > 1 else 1) + N * K * bsz * (1 if nj * nk == 1 else ni) + M * N * osz
                busy = max(traffic / HBM_BYTES_PER_US, 2.0 * M * N * K / MXU_FLOPS_PER_US)
                cost = ni * nj * nk * STEP_US + busy + blocks / HBM_BYTES_PER_US
                if best is None or cost < best[0]:
                    best = (cost, tm, tn, tk, blocks, temps)
    _, tm, tn, tk, blocks, temps = best
    nk = K // tk
    use_scratch = nk > 1 and osz != 4

    def split(index, total, blocks, tile):
        per = total // blocks // tile
        return index // per, index % per

    def body(a_ref, b_ref, *rest):
        rest = rest[len(deps):]
        if post_norm:
            h_ref, g_ref, beta_ref, o_ref, xh_ref, rs_ref = rest[:6]
            acc = rest[6:]
        else:
            o_ref, acc = rest[0], rest[1:]
        av = a_ref[...].astype(MXU_DTYPE)
        bv = b_ref[...].astype(MXU_DTYPE)
        dn = {"nn": (((1,), (0,)), ((), ())), "nt": (((1,), (1,)), ((), ())), "tn": (((0,), (0,)), ((), ()))}[mode]
        r = lax.dot_general(av, bv, dn, preferred_element_type=F32)

        def normalise(f):
            o_ref[...], xh_ref[...], rs_ref[...] = _layer_norm(post_norm[3] * h_ref[...] + f, g_ref[...], beta_ref[...])

        if nk == 1 and post_norm:
            normalise(r)
        elif nk == 1:
            o_ref[...] = r.astype(o_ref.dtype)
        else:
            acc_ref = acc[0] if use_scratch else o_ref

            @pl.when(pl.program_id(2) == 0)
            def _():
                acc_ref[...] = r

            @pl.when(pl.program_id(2) > 0)
            def _():
                acc_ref[...] += r

            if use_scratch:
                @pl.when(pl.program_id(2) == nk - 1)
                def _():
                    o_ref[...] = acc_ref[...].astype(o_ref.dtype)
            elif post_norm:
                @pl.when(pl.program_id(2) == nk - 1)
                def _():
                    normalise(o_ref[...])

    if mode == "tn":
        a_spec = pl.BlockSpec((tk, tm), lambda i, j, k: (k, i))
    elif a_blocks:
        a_spec = pl.BlockSpec((None, tm, tk), lambda i, j, k: (split(k, K, a_blocks, tk)[0], i, split(k, K, a_blocks, tk)[1]))
    else:
        a_spec = pl.BlockSpec((tm, tk), lambda i, j, k: (i, k))
    lead = (None,) * nlead
    if mode == "nt":
        bmap = ((lambda i, j, k: b_index + (split(k, K, b_blocks, tk)[0], j, split(k, K, b_blocks, tk)[1])) if b_blocks
                else (lambda i, j, k: b_index + (j, k)))
        b_spec = pl.BlockSpec(lead + (tn, tk), bmap)
    else:
        bmap = ((lambda i, j, k: b_index + (split(j, N, b_blocks, tn)[0], k, split(j, N, b_blocks, tn)[1])) if b_blocks
                else (lambda i, j, k: b_index + (k, j)))
        b_spec = pl.BlockSpec(lead + (tk, tn), bmap)
    if out_blocks:
        o_spec = pl.BlockSpec((None, tm, tn), lambda i, j, k: (split(j, N, out_blocks, tn)[0], i, split(j, N, out_blocks, tn)[1]))
        o_shape = jax.ShapeDtypeStruct((out_blocks, M, N // out_blocks), out_dtype)
    else:
        o_spec = pl.BlockSpec((tm, tn), lambda i, j, k: (i, j))
        o_shape = jax.ShapeDtypeStruct((M, N), out_dtype)
    in_specs, extra = [a_spec, b_spec] + [pl.BlockSpec(memory_space=pl.ANY)] * len(deps), ()
    if post_norm:
        vec = pl.BlockSpec((1, N), lambda i, j, k: (0, 0))
        in_specs += [pl.BlockSpec((tm, N), lambda i, j, k: (i, 0)), vec, vec]
        o_spec = [o_spec, pl.BlockSpec((tm, N), lambda i, j, k: (i, 0)), pl.BlockSpec((tm, 1), lambda i, j, k: (i, 0))]
        o_shape = [o_shape, jax.ShapeDtypeStruct((M, N), F32), jax.ShapeDtypeStruct((M, 1), F32)]
        extra = post_norm[:3]
    return hbm_call(
        body, name=name, grid=(M // tm, N // tn, nk), in_specs=in_specs, out_specs=o_spec, out_shape=o_shape,
        scratch_shapes=[pltpu.VMEM((tm, tn), F32)] if use_scratch else [],
        compiler_params=_params(("parallel", "parallel", "arbitrary"), _vmem_limit(blocks, temps)),
    )(a, b, *deps, *extra)


ROW_TILE = 512
GATE_ROWS = 1024


def ln_bwd(dy_a, dy_b, xh, rs, g, c1, name):
    S, D = xh.shape
    tr = min(ROW_TILE, S)
    two = dy_b is not None

    def body(*refs):
        if two:
            a_ref, b_ref, xh_ref, rs_ref, g_ref, dz_ref, dg_ref, db_ref = refs
            dy = c1 * a_ref[...] + b_ref[...]
        else:
            a_ref, xh_ref, rs_ref, g_ref, dz_ref, dg_ref, db_ref = refs
            dy = a_ref[...]
        x = xh_ref[...]
        dyg = dy * g_ref[...]
        m1 = jnp.mean(dyg, axis=-1, keepdims=True)
        m2 = jnp.mean(dyg * x, axis=-1, keepdims=True)
        dz_ref[...] = rs_ref[...] * (dyg - m1 - x * m2)

        @pl.when(pl.program_id(0) == 0)
        def _():
            dg_ref[...] = jnp.zeros_like(dg_ref)
            db_ref[...] = jnp.zeros_like(db_ref)

        dg_ref[...] += jnp.sum(dy * x, axis=0, keepdims=True)
        db_ref[...] += jnp.sum(dy, axis=0, keepdims=True)

    row = pl.BlockSpec((tr, D), lambda i: (i, 0))
    vec = pl.BlockSpec((1, D), lambda i: (0, 0))
    ins = [row, row] if two else [row]
    args = (dy_a, dy_b) if two else (dy_a,)
    return hbm_call(
        body, name=name, grid=(S // tr,), in_specs=ins + [row, pl.BlockSpec((tr, 1), lambda i: (i, 0)), vec],
        out_specs=[row, vec, vec],
        out_shape=[jax.ShapeDtypeStruct((S, D), F32), jax.ShapeDtypeStruct((1, D), F32), jax.ShapeDtypeStruct((1, D), F32)],
        compiler_params=_params(("arbitrary",), 48 << 20),
    )(*args, xh, rs, g)


def axpby(a, b, c1, name):
    S, D = a.shape
    tr = min(ROW_TILE, S)

    def body(a_ref, b_ref, o_ref):
        o_ref[...] = c1 * a_ref[...] + b_ref[...]

    row = pl.BlockSpec((tr, D), lambda i: (i, 0))
    return hbm_call(body, name=name, grid=(S // tr,), in_specs=[row, row], out_specs=row,
                          out_shape=jax.ShapeDtypeStruct((S, D), F32), compiler_params=_params(("parallel",)))(a, b)


def loss_head(y, t, name):
    S, D = y.shape
    tr = min(ROW_TILE, S)
    nsteps = S // tr

    def body(y_ref, t_ref, dy_ref, l_ref, acc_ref):
        i = pl.program_id(0)

        @pl.when(i == 0)
        def _():
            acc_ref[...] = jnp.zeros_like(acc_ref)

        e = y_ref[...] - t_ref[...]
        dy_ref[...] = e * (1.0 / D)
        acc_ref[...] += jnp.sum(e * e, axis=0, keepdims=True)

        @pl.when(i == nsteps - 1)
        def _():
            l_ref[...] = jnp.sum(acc_ref[...], axis=1, keepdims=True) * (0.5 / D)

    row = pl.BlockSpec((tr, D), lambda i: (i, 0))
    return hbm_call(
        body, name=name, grid=(nsteps,), in_specs=[row, row],
        out_specs=[row, pl.BlockSpec((1, 1), lambda i: (0, 0))],
        out_shape=[jax.ShapeDtypeStruct((S, D), F32), jax.ShapeDtypeStruct((1, 1), F32)],
        scratch_shapes=[pltpu.VMEM((1, D), F32)], compiler_params=_params(("arbitrary",)),
    )(y, t)


SWIGLU_ROWS = 256


def swiglu_fwd(gu, name):
    _, S, Fh = gu.shape
    tc = _divisors(Fh, LANES, 1536)[0]
    tr = min(SWIGLU_ROWS, S)

    def body(gu_ref, o_ref):
        g = gu_ref[0]
        o_ref[...] = (g * _sigmoid(g) * gu_ref[1]).astype(o_ref.dtype)

    return hbm_call(
        body, name=name, grid=(S // tr, Fh // tc), in_specs=[pl.BlockSpec((2, tr, tc), lambda i, j: (0, i, j))],
        out_specs=pl.BlockSpec((tr, tc), lambda i, j: (i, j)), out_shape=jax.ShapeDtypeStruct((S, Fh), MXU_DTYPE),
        compiler_params=_params(("parallel", "parallel")),
    )(gu)


def swiglu_bwd(gu, dact, name):
    _, S, Fh = gu.shape
    tc = _divisors(Fh, LANES, 1536)[0]
    tr = min(SWIGLU_ROWS, S)

    def body(gu_ref, d_ref, o_ref):
        g, u, d = gu_ref[0], gu_ref[1], d_ref[...]
        s = _sigmoid(g)
        o_ref[0] = (d * u * (s * (1.0 + g * (1.0 - s)))).astype(o_ref.dtype)
        o_ref[1] = (d * (g * s)).astype(o_ref.dtype)

    both = pl.BlockSpec((2, tr, tc), lambda i, j: (0, i, j))
    return hbm_call(
        body, name=name, grid=(S // tr, Fh // tc), in_specs=[both, pl.BlockSpec((tr, tc), lambda i, j: (i, j))],
        out_specs=both, out_shape=jax.ShapeDtypeStruct((2, S, Fh), MXU_DTYPE), compiler_params=_params(("parallel", "parallel")),
    )(gu, dact)


GATE_COLS = 256


def merge_fwd(proj, pr, pa, name):
    S, D = pr.shape
    tr = min(GATE_ROWS, S)
    c0 = (3 * D + 2 * N_KV_HEADS * HEAD_DIM) // GATE_COLS
    c1 = c0 + D // GATE_COLS

    def body(gr_ref, ga_ref, pr_ref, pa_ref, o_ref):
        o_ref[...] = (_sigmoid(gr_ref[...]) * pr_ref[...] + _sigmoid(ga_ref[...]) * pa_ref[...]).astype(o_ref.dtype)

    blk = pl.BlockSpec((tr, GATE_COLS), lambda i, j: (i, j))
    return hbm_call(
        body, name=name, grid=(S // tr, D // GATE_COLS),
        in_specs=[pl.BlockSpec((tr, GATE_COLS), lambda i, j: (i, c0 + j)), pl.BlockSpec((tr, GATE_COLS), lambda i, j: (i, c1 + j)),
                  blk, blk],
        out_specs=blk, out_shape=jax.ShapeDtypeStruct((S, D), MXU_DTYPE), compiler_params=_params(("parallel", "parallel")),
    )(proj, proj, pr, pa)


def merge_bwd(proj, pr, pa, dm, name):
    S, D = pr.shape
    tr = min(GATE_ROWS, S)
    c0 = (3 * D + 2 * N_KV_HEADS * HEAD_DIM) // GATE_COLS
    c1 = c0 + D // GATE_COLS

    def body(gr_ref, ga_ref, pr_ref, pa_ref, dm_ref, dpr_ref, dpa_ref, dgr_ref, dga_ref):
        sr, sa, d = _sigmoid(gr_ref[...]), _sigmoid(ga_ref[...]), dm_ref[...]
        dpr_ref[...] = (d * sr).astype(dpr_ref.dtype)
        dpa_ref[...] = (d * sa).astype(dpa_ref.dtype)
        dgr_ref[...] = (d * pr_ref[...] * (sr * (1.0 - sr))).astype(dgr_ref.dtype)
        dga_ref[...] = (d * pa_ref[...] * (sa * (1.0 - sa))).astype(dga_ref.dtype)

    blk = pl.BlockSpec((tr, GATE_COLS), lambda i, j: (i, j))
    sds = jax.ShapeDtypeStruct((S, D), MXU_DTYPE)
    return hbm_call(
        body, name=name, grid=(S // tr, D // GATE_COLS),
        in_specs=[pl.BlockSpec((tr, GATE_COLS), lambda i, j: (i, c0 + j)), pl.BlockSpec((tr, GATE_COLS), lambda i, j: (i, c1 + j)),
                  blk, blk, blk],
        out_specs=[blk, blk, blk, blk], out_shape=[sds, sds, sds, sds], compiler_params=_params(("parallel", "parallel")),
    )(proj, proj, pr, pa, dm)


RG_ROWS = 512


def _shift_down(cur, prev, d, row, first):
    halo = jnp.where(first, 0.0, pltpu.roll(prev, d, 0))
    return jnp.where(row >= d, pltpu.roll(cur, d, 0), halo)


def _shift_up(cur, nxt, d, row, last, tr):
    halo = jnp.where(last, 0.0, pltpu.roll(nxt, tr - d, 0))
    return jnp.where(row < tr - d, pltpu.roll(cur, tr - d, 0), halo)


def _lru_coeffs(r, lam):
    sp = _softplus_neg(lam)
    la = -LRU_C * r * sp
    return sp, la, jnp.exp(la), _neg_expm1(2.0 * la)


def rg_gates_fwd(proj, conv_w, conv_b, w_rg, b_rg, w_ig, b_ig, lam, name):
    S = proj.shape[0]
    nblk, bw, _ = w_rg.shape
    D = nblk * bw
    tr = min(RG_ROWS, S)

    def body(xr_ref, xp_ref, cw_ref, cb_ref, wr_ref, br_ref, wi_ref, bi_ref, lam_ref, xc_ref, r_ref, i_ref, a_ref, b_ref):
        first = pl.program_id(1) == 0
        cur, prev = xr_ref[...], xp_ref[...]
        row = lax.broadcasted_iota(jnp.int32, cur.shape, 0)
        xc = cb_ref[...]
        for k in range(CONV_WIDTH - 1):
            xc = xc + _shift_down(cur, prev, CONV_WIDTH - 1 - k, row, first) * cw_ref[k:k + 1, :]
        xc = xc + cur * cw_ref[CONV_WIDTH - 1:CONV_WIDTH, :]
        xm = xc.astype(MXU_DTYPE)
        r = _sigmoid(jnp.dot(xm, wr_ref[...].astype(MXU_DTYPE), preferred_element_type=F32) + br_ref[...])
        ig = _sigmoid(jnp.dot(xm, wi_ref[...].astype(MXU_DTYPE), preferred_element_type=F32) + bi_ref[...])
        _, _, a, em = _lru_coeffs(r, lam_ref[...])
        xc_ref[...] = xc
        r_ref[...] = r
        i_ref[...] = ig
        a_ref[...] = a
        b_ref[...] = jnp.sqrt(em) * (ig * xc)

    tile = pl.BlockSpec((tr, bw), lambda n, i: (i, n))
    vec = pl.BlockSpec((1, bw), lambda n, i: (0, n))
    wblk = pl.BlockSpec((None, bw, bw), lambda n, i: (n, 0, 0))
    sds = jax.ShapeDtypeStruct((S, D), F32)
    return hbm_call(
        body, name=name, grid=(nblk, S // tr),
        in_specs=[tile, pl.BlockSpec((tr, bw), lambda n, i: (jnp.maximum(i - 1, 0), n)),
                  pl.BlockSpec((CONV_WIDTH, bw), lambda n, i: (0, n)), vec, wblk, vec, wblk, vec, vec],
        out_specs=[tile] * 5, out_shape=[sds] * 5, compiler_params=_params(("parallel", "parallel")),
    )(proj, proj, conv_w, conv_b, w_rg, b_rg, w_ig, b_ig, lam)


SCAN_COLS = 256
CHUNK = SUBLANES
SCAN_UNROLL = 4


def rg_scan_fwd(proj, a, b, name):
    S, D = a.shape
    cb = min(SCAN_COLS, D)
    goff = D // cb

    def body(a_ref, b_ref, g_ref, hs_ref, y_ref):
        row = lax.broadcasted_iota(jnp.int32, (CHUNK, cb), 0)

        def step(c, carry):
            r0 = pl.multiple_of(c * CHUNK, CHUNK)
            A = a_ref[pl.ds(r0, CHUNK), :]
            B = b_ref[pl.ds(r0, CHUNK), :]
            for d in (1, 2, 4):
                As = jnp.where(row >= d, pltpu.roll(A, d, 0), 1.0)
                Bs = jnp.where(row >= d, pltpu.roll(B, d, 0), 0.0)
                B = A * Bs + B
                A = A * As
            hs_ref[pl.ds(r0, CHUNK), :] = B + A * carry
            a_end = jnp.sum(jnp.where(row == CHUNK - 1, A, 0.0), axis=0, keepdims=True)
            b_end = jnp.sum(jnp.where(row == CHUNK - 1, B, 0.0), axis=0, keepdims=True)
            return b_end + a_end * carry

        lax.fori_loop(0, S // CHUNK, step, jnp.zeros((1, cb), F32), unroll=SCAN_UNROLL)
        y_ref[...] = (hs_ref[...] * _gelu(g_ref[...])).astype(y_ref.dtype)

    col = pl.BlockSpec((S, cb), lambda j: (0, j))
    return hbm_call(
        body, name=name, grid=(D // cb,), in_specs=[col, col, pl.BlockSpec((S, cb), lambda j: (0, goff + j))],
        out_specs=[col, col], out_shape=[jax.ShapeDtypeStruct((S, D), F32), jax.ShapeDtypeStruct((S, D), MXU_DTYPE)],
        compiler_params=_params(("parallel",), _vmem_limit(5 * S * cb * 4, 4 * S * cb * 4)),
    )(a, b, proj)


def rg_scan_bwd(proj, dy, hs, a, name):
    S, D = a.shape
    cb = min(SCAN_COLS, D)
    goff = D // cb
    nchunks = S // CHUNK

    def body(g_ref, dy_ref, hs_ref, a_ref, dg_ref, gt_ref):
        gate, dy = g_ref[...], dy_ref[...]
        dg_ref[...] = (dy * hs_ref[...] * _gelu_grad(gate)).astype(dg_ref.dtype)
        gt_ref[...] = dy * _gelu(gate)
        row = lax.broadcasted_iota(jnp.int32, (CHUNK, cb), 0)

        def step(k, carry):
            c = nchunks - 1 - k
            r0 = pl.multiple_of(c * CHUNK, CHUNK)
            rn = pl.multiple_of(jnp.minimum(c + 1, nchunks - 1) * CHUNK, CHUNK)
            last = c == nchunks - 1
            nxt = jnp.where(last, 0.0, pltpu.roll(a_ref[pl.ds(rn, CHUNK), :], CHUNK - 1, 0))
            A = jnp.where(row < CHUNK - 1, pltpu.roll(a_ref[pl.ds(r0, CHUNK), :], CHUNK - 1, 0), nxt)
            B = gt_ref[pl.ds(r0, CHUNK), :]
            for d in (1, 2, 4):
                As = jnp.where(row < CHUNK - d, pltpu.roll(A, CHUNK - d, 0), 1.0)
                Bs = jnp.where(row < CHUNK - d, pltpu.roll(B, CHUNK - d, 0), 0.0)
                B = A * Bs + B
                A = A * As
            gt_ref[pl.ds(r0, CHUNK), :] = B + A * carry
            a_end = jnp.sum(jnp.where(row == 0, A, 0.0), axis=0, keepdims=True)
            b_end = jnp.sum(jnp.where(row == 0, B, 0.0), axis=0, keepdims=True)
            return b_end + a_end * carry

        lax.fori_loop(0, nchunks, step, jnp.zeros((1, cb), F32), unroll=SCAN_UNROLL)

    col = pl.BlockSpec((S, cb), lambda j: (0, j))
    return hbm_call(
        body, name=name, grid=(D // cb,), in_specs=[pl.BlockSpec((S, cb), lambda j: (0, goff + j)), col, col, col],
        out_specs=[col, col], out_shape=[jax.ShapeDtypeStruct((S, D), MXU_DTYPE), jax.ShapeDtypeStruct((S, D), F32)],
        compiler_params=_params(("parallel",), _vmem_limit(6 * S * cb * 4, 6 * S * cb * 4)),
    )(proj, dy, hs, a)


def rg_gates_bwd(gt, hs, xc, r, ig, w_rg, w_ig, lam, name):
    S, D = xc.shape
    nblk, bw, _ = w_rg.shape
    tr = min(RG_ROWS, S)

    def body(gt_ref, hs_ref, hp_ref, xc_ref, r_ref, i_ref, wr_ref, wi_ref, lam_ref,
             dxc_ref, dwr_ref, dwi_ref, dbr_ref, dbi_ref, dl_ref):
        step = pl.program_id(1)
        g, hs, xc, r, ig, lam = gt_ref[...], hs_ref[...], xc_ref[...], r_ref[...], i_ref[...], lam_ref[...]
        row = lax.broadcasted_iota(jnp.int32, g.shape, 0)
        hprev = _shift_down(hs, hp_ref[...], 1, row, step == 0)
        sp, _, a, em = _lru_coeffs(r, lam)
        mult = jnp.sqrt(em)
        du = g * mult
        dla = g * hprev * a - (g * (ig * xc)) * (a * a) / mult
        dpr = (dla * (-LRU_C * sp)) * (r * (1.0 - r))
        dpi = (du * xc) * (ig * (1.0 - ig))
        dprm, dpim = dpr.astype(MXU_DTYPE), dpi.astype(MXU_DTYPE)
        nt = (((1,), (1,)), ((), ()))
        dxc_ref[...] = (du * ig + lax.dot_general(dprm, wr_ref[...].astype(MXU_DTYPE), nt, preferred_element_type=F32)
                        + lax.dot_general(dpim, wi_ref[...].astype(MXU_DTYPE), nt, preferred_element_type=F32))

        @pl.when(step == 0)
        def _():
            for ref in (dwr_ref, dwi_ref, dbr_ref, dbi_ref, dl_ref):
                ref[...] = jnp.zeros_like(ref)

        xct = xc.T.astype(MXU_DTYPE)
        dwr_ref[...] += jnp.dot(xct, dprm, preferred_element_type=F32)
        dwi_ref[...] += jnp.dot(xct, dpim, preferred_element_type=F32)
        dbr_ref[...] += jnp.sum(dpr, axis=0, keepdims=True)
        dbi_ref[...] += jnp.sum(dpi, axis=0, keepdims=True)
        dl_ref[...] += jnp.sum(dla * (-LRU_C * r), axis=0, keepdims=True) * (-_sigmoid(-lam))

    tile = pl.BlockSpec((tr, bw), lambda n, i: (i, n))
    vec = pl.BlockSpec((1, bw), lambda n, i: (0, n))
    wblk = pl.BlockSpec((None, bw, bw), lambda n, i: (n, 0, 0))
    return hbm_call(
        body, name=name, grid=(nblk, S // tr),
        in_specs=[tile, tile, pl.BlockSpec((tr, bw), lambda n, i: (jnp.maximum(i - 1, 0), n)), tile, tile, tile, wblk, wblk, vec],
        out_specs=[tile, wblk, wblk, vec, vec, vec],
        out_shape=[jax.ShapeDtypeStruct((S, D), F32), jax.ShapeDtypeStruct((nblk, bw, bw), F32), jax.ShapeDtypeStruct((nblk, bw, bw), F32),
                   jax.ShapeDtypeStruct((1, D), F32), jax.ShapeDtypeStruct((1, D), F32), jax.ShapeDtypeStruct((1, D), F32)],
        compiler_params=_params(("parallel", "arbitrary")),
    )(gt, hs, hs, xc, r, ig, w_rg, w_ig, lam)


def rg_conv_bwd(proj, dxc, conv_w, name):
    S, D = dxc.shape
    bw = min(SCAN_COLS, D)
    tr = min(RG_ROWS, S)
    nsteps = S // tr

    def body(d_ref, dn_ref, xr_ref, xp_ref, cw_ref, dxr_ref, dcw_ref, dcb_ref):
        step = pl.program_id(1)
        d, xr = d_ref[...], xr_ref[...]
        row = lax.broadcasted_iota(jnp.int32, d.shape, 0)
        dxr = d * cw_ref[CONV_WIDTH - 1:CONV_WIDTH, :]
        for k in range(CONV_WIDTH - 1):
            dxr = dxr + _shift_up(d, dn_ref[...], CONV_WIDTH - 1 - k, row, step == nsteps - 1, tr) * cw_ref[k:k + 1, :]
        dxr_ref[...] = dxr.astype(dxr_ref.dtype)

        @pl.when(step == 0)
        def _():
            dcw_ref[...] = jnp.zeros_like(dcw_ref)
            dcb_ref[...] = jnp.zeros_like(dcb_ref)

        for k in range(CONV_WIDTH - 1):
            xs = _shift_down(xr, xp_ref[...], CONV_WIDTH - 1 - k, row, step == 0)
            dcw_ref[k:k + 1, :] += jnp.sum(d * xs, axis=0, keepdims=True)
        dcw_ref[CONV_WIDTH - 1:CONV_WIDTH, :] += jnp.sum(d * xr, axis=0, keepdims=True)
        dcb_ref[...] += jnp.sum(d, axis=0, keepdims=True)

    tile = pl.BlockSpec((tr, bw), lambda n, i: (i, n))
    cwb = pl.BlockSpec((CONV_WIDTH, bw), lambda n, i: (0, n))
    return hbm_call(
        body, name=name, grid=(D // bw, nsteps),
        in_specs=[tile, pl.BlockSpec((tr, bw), lambda n, i: (jnp.minimum(i + 1, nsteps - 1), n)), tile,
                  pl.BlockSpec((tr, bw), lambda n, i: (jnp.maximum(i - 1, 0), n)), cwb],
        out_specs=[tile, cwb, pl.BlockSpec((1, bw), lambda n, i: (0, n))],
        out_shape=[jax.ShapeDtypeStruct((S, D), MXU_DTYPE), jax.ShapeDtypeStruct((CONV_WIDTH, D), F32), jax.ShapeDtypeStruct((1, D), F32)],
        compiler_params=_params(("parallel", "arbitrary")),
    )(dxc, dxc, proj, proj, conv_w)


def rope_table(S):
    half = ROT_DIM // 2
    pos = jnp.arange(S, dtype=F32)
    inv = ROPE_THETA ** (-jnp.arange(0, ROT_DIM, 2, dtype=F32) / ROT_DIM)
    ang = pos[:, None] * inv[None, :]
    cos, sin = jnp.cos(ang), jnp.sin(ang)
    zero = jnp.zeros((S, HEAD_DIM - ROT_DIM), F32)
    c = jnp.concatenate([cos, cos, zero + 1.0], axis=1)
    a = jnp.concatenate([-sin, jnp.zeros((S, half), F32), zero], axis=1)
    b = jnp.concatenate([jnp.zeros((S, half), F32), sin, zero], axis=1)
    return jnp.stack([jnp.tile(t, (1, LANES // HEAD_DIM)) for t in (c, a, b)])


def _rope(t, tab):
    half = ROT_DIM // 2
    return t * tab[0] + pltpu.roll(t, LANES - half, 1) * tab[1] + pltpu.roll(t, half, 1) * tab[2]


def _rope_t(d, tab):
    half = ROT_DIM // 2
    return d * tab[0] + pltpu.roll(d * tab[1], half, 1) + pltpu.roll(d * tab[2], LANES - half, 1)


def _dup_head(t, hk, lo):
    sw = pltpu.roll(t, HEAD_DIM, 1)
    return jnp.where(lo, t, sw) if hk == 0 else jnp.where(lo, sw, t)


def _attn_common(n, sink_ref, q_ref, kp_ref, kc_ref, vp_ref, vc_ref, tc_ref, tp_ref, hk, pairs):
    tq = (tc_ref[0], tc_ref[1], tc_ref[2])
    tp = (tp_ref[0], tp_ref[1], tp_ref[2])
    lo = lax.broadcasted_iota(jnp.int32, (WINDOW, LANES), 1) < HEAD_DIM
    lo2 = lax.broadcasted_iota(jnp.int32, (2 * WINDOW, LANES), 1) < HEAD_DIM
    kband = jnp.concatenate([_rope(kp_ref[...], tp), _rope(kc_ref[...], tq)], axis=0)
    vband = jnp.concatenate([vp_ref[...], vc_ref[...]], axis=0)
    kd = _dup_head(kband, hk, lo2).astype(MXU_DTYPE)
    vd = _dup_head(vband, hk, lo2).astype(MXU_DTYPE)
    rows, sks = [], []
    for j in range(pairs):
        col = hk * pairs + j
        qp = _rope(q_ref[:, col * LANES:(col + 1) * LANES], tq)
        rows += [jnp.where(lo, qp, 0.0), jnp.where(lo, 0.0, qp)]
        sks += [jnp.full((WINDOW, 1), sink_ref[2 * col], F32), jnp.full((WINDOW, 1), sink_ref[2 * col + 1], F32)]
    qg = jnp.concatenate(rows, axis=0)
    sk = jnp.concatenate(sks, axis=0)
    G = 2 * pairs * WINDOW
    own = lax.broadcasted_iota(jnp.int32, (G, WINDOW), 1) <= (lax.broadcasted_iota(jnp.int32, (G, WINDOW), 0) & (WINDOW - 1))
    s = lax.dot_general(qg.astype(MXU_DTYPE), kd, (((1,), (1,)), ((), ())), preferred_element_type=F32) * (HEAD_DIM ** -0.5)
    s = jnp.where(own, s[:, WINDOW:], s[:, :WINDOW] + jnp.where(n > 0, 0.0, NEG_INF))
    m = jnp.maximum(jnp.max(s, axis=1, keepdims=True), sk)
    e = jnp.exp(s - m)
    es = jnp.exp(sk - m)
    inv = 1.0 / (jnp.sum(e, axis=1, keepdims=True) + es)
    return qg, kd, vd, e * inv, es * inv, own, lo, lo2, tq, tp


def _unfold_band(t, own):
    return jnp.concatenate([jnp.where(own, 0.0, t), jnp.where(own, t, 0.0)], axis=1)


def _attn_specs(D, NB):
    kcol = 3 * D // LANES
    q = pl.BlockSpec((WINDOW, D), lambda n: (n, 2))
    kc = pl.BlockSpec((WINDOW, LANES), lambda n: (n, kcol))
    kp = pl.BlockSpec((WINDOW, LANES), lambda n: (jnp.maximum(n - 1, 0), kcol))
    vc = pl.BlockSpec((WINDOW, LANES), lambda n: (n, kcol + 1))
    vp = pl.BlockSpec((WINDOW, LANES), lambda n: (jnp.maximum(n - 1, 0), kcol + 1))
    tc = pl.BlockSpec((3, WINDOW, LANES), lambda n: (0, n, 0))
    tp = pl.BlockSpec((3, WINDOW, LANES), lambda n: (0, jnp.maximum(n - 1, 0), 0))
    sink = pl.BlockSpec(memory_space=pltpu.SMEM)
    return [sink, q, kp, kc, vp, vc, tc, tp]


def attn_fwd(proj, sinks, tab, D, name):
    S = proj.shape[0]
    NB = S // WINDOW
    pairs = D // HEAD_DIM // N_KV_HEADS // 2

    def body(sink_ref, q_ref, kp_ref, kc_ref, vp_ref, vc_ref, tc_ref, tp_ref, o_ref):
        n = pl.program_id(0)
        for hk in range(N_KV_HEADS):
            _, _, vd, p, _, own, lo, _, _, _ = _attn_common(n, sink_ref, q_ref, kp_ref, kc_ref, vp_ref, vc_ref, tc_ref, tp_ref, hk, pairs)
            o = jnp.dot(_unfold_band(p, own).astype(MXU_DTYPE), vd, preferred_element_type=F32)
            for j in range(pairs):
                col = hk * pairs + j
                oa = o[(2 * j) * WINDOW:(2 * j + 1) * WINDOW]
                ob = o[(2 * j + 1) * WINDOW:(2 * j + 2) * WINDOW]
                o_ref[:, col * LANES:(col + 1) * LANES] = jnp.where(lo, oa, ob)

    return hbm_call(
        body, name=name, grid=(NB,), in_specs=_attn_specs(D, NB),
        out_specs=pl.BlockSpec((WINDOW, D), lambda n: (n, 0)), out_shape=jax.ShapeDtypeStruct((S, D), F32),
        compiler_params=_params(("parallel",)),
    )(sinks, proj, proj, proj, proj, proj, tab, tab)


def attn_bwd(proj, sinks, tab, o, do, D, name):
    S = proj.shape[0]
    NB = S // WINDOW
    pairs = D // HEAD_DIM // N_KV_HEADS // 2

    def body(sink_ref, q_ref, kp_ref, kc_ref, vp_ref, vc_ref, tc_ref, tp_ref, o_ref, do_ref, dq_ref, dk_ref, dv_ref, ds_ref):
        n = pl.program_id(0)

        @pl.when(n == 0)
        def _():
            ds_ref[...] = jnp.zeros_like(ds_ref)

        lane1 = lax.broadcasted_iota(jnp.int32, (1, LANES), 1)
        dsink = jnp.zeros((1, LANES), F32)
        dkt = dvt = None
        for hk in range(N_KV_HEADS):
            qg, kd, vd, p, ps, own, lo, lo2, tq, tp = _attn_common(n, sink_ref, q_ref, kp_ref, kc_ref, vp_ref, vc_ref, tc_ref, tp_ref, hk, pairs)
            dos, os_ = [], []
            for j in range(pairs):
                col = hk * pairs + j
                dop = do_ref[:, col * LANES:(col + 1) * LANES]
                op = o_ref[:, col * LANES:(col + 1) * LANES]
                dos += [jnp.where(lo, dop, 0.0), jnp.where(lo, 0.0, dop)]
                os_ += [jnp.where(lo, op, 0.0), jnp.where(lo, 0.0, op)]
            dog = jnp.concatenate(dos, axis=0)
            og = jnp.concatenate(os_, axis=0)
            dogm = dog.astype(MXU_DTYPE)
            dp = lax.dot_general(dogm, vd, (((1,), (1,)), ((), ())), preferred_element_type=F32)
            dp = jnp.where(own, dp[:, WINDOW:], dp[:, :WINDOW])
            dr = jnp.sum(dog * og, axis=1, keepdims=True)
            ds = _unfold_band(p * (dp - dr) * (HEAD_DIM ** -0.5), own)
            dsm = ds.astype(MXU_DTYPE)
            dqg = jnp.dot(dsm, kd, preferred_element_type=F32)
            dkd = jnp.dot(ds.T.astype(MXU_DTYPE), qg.astype(MXU_DTYPE), preferred_element_type=F32)
            dvd = jnp.dot(_unfold_band(p, own).T.astype(MXU_DTYPE), dogm, preferred_element_type=F32)
            dkf = dkd + pltpu.roll(dkd, HEAD_DIM, 1)
            dvf = dvd + pltpu.roll(dvd, HEAD_DIM, 1)
            if hk == 0:
                dkt, dvt = dkf, dvf
            else:
                dkt, dvt = jnp.where(lo2, dkt, dkf), jnp.where(lo2, dvt, dvf)
            sd = ps * dr
            for j in range(pairs):
                col = hk * pairs + j
                dqa = dqg[(2 * j) * WINDOW:(2 * j + 1) * WINDOW]
                dqb = dqg[(2 * j + 1) * WINDOW:(2 * j + 2) * WINDOW]
                dq_ref[:, col * LANES:(col + 1) * LANES] = _rope_t(jnp.where(lo, dqa, dqb), tq).astype(dq_ref.dtype)
                for t in range(2):
                    part = sd[(2 * j + t) * WINDOW:(2 * j + t + 1) * WINDOW]
                    val = jnp.sum(part, axis=0, keepdims=True)
                    dsink = dsink - jnp.where(lane1 == 2 * col + t, val, 0.0)
        dk_ref[...] = jnp.concatenate([_rope_t(dkt[:WINDOW], tp), _rope_t(dkt[WINDOW:], tq)], axis=0)
        dv_ref[...] = dvt
        ds_ref[...] += dsink

    blk = pl.BlockSpec((WINDOW, D), lambda n: (n, 0))
    band = pl.BlockSpec((None, 2 * WINDOW, LANES), lambda n: (n, 0, 0))
    return hbm_call(
        body, name=name, grid=(NB,), in_specs=_attn_specs(D, NB) + [blk, blk],
        out_specs=[blk, band, band, pl.BlockSpec((1, LANES), lambda n: (0, 0))],
        out_shape=[jax.ShapeDtypeStruct((S, D), MXU_DTYPE), jax.ShapeDtypeStruct((NB, 2 * WINDOW, LANES), F32),
                   jax.ShapeDtypeStruct((NB, 2 * WINDOW, LANES), F32), jax.ShapeDtypeStruct((1, LANES), F32)],
        compiler_params=_params(("arbitrary",)),
    )(sinks, proj, proj, proj, proj, proj, tab, tab, o, do)


def band_fold(dkb, dvb, name):
    NB = dkb.shape[0]
    k4 = dkb.reshape(NB, 2, WINDOW, LANES)
    v4 = dvb.reshape(NB, 2, WINDOW, LANES)

    def body(kc_ref, kn_ref, vc_ref, vn_ref, dk_ref, dv_ref):
        more = pl.program_id(0) < NB - 1
        dk_ref[...] = (kc_ref[...] + jnp.where(more, kn_ref[...], 0.0)).astype(dk_ref.dtype)
        dv_ref[...] = (vc_ref[...] + jnp.where(more, vn_ref[...], 0.0)).astype(dv_ref.dtype)

    cur = pl.BlockSpec((None, None, WINDOW, LANES), lambda n: (n, 1, 0, 0))
    nxt = pl.BlockSpec((None, None, WINDOW, LANES), lambda n: (jnp.minimum(n + 1, NB - 1), 0, 0, 0))
    out = pl.BlockSpec((WINDOW, LANES), lambda n: (n, 0))
    sds = jax.ShapeDtypeStruct((NB * WINDOW, LANES), MXU_DTYPE)
    return hbm_call(body, name=name, grid=(NB,), in_specs=[cur, nxt, cur, nxt], out_specs=[out, out], out_shape=[sds, sds],
                          compiler_params=_params(("parallel",)))(k4, k4, v4, v4)


CROSS_ROWS = 512


def _cross_probs(q, k, scale):
    s = lax.dot_general(q.astype(MXU_DTYPE), k.astype(MXU_DTYPE), (((1,), (1,)), ((), ())), preferred_element_type=F32) * scale
    e = jnp.exp(s - jnp.max(s, axis=1, keepdims=True))
    return e / jnp.sum(e, axis=1, keepdims=True)


def cross_fwd(qc, kv, name):
    S, D = qc.shape
    M = kv.shape[0]
    hd = D // CROSS_HEADS
    tq = min(CROSS_ROWS, S)

    def body(q_ref, kv_ref, o_ref):
        for h in range(CROSS_HEADS):
            p = _cross_probs(q_ref[:, h * hd:(h + 1) * hd], kv_ref[:, h * hd:(h + 1) * hd], hd ** -0.5)
            v = kv_ref[:, D + h * hd:D + (h + 1) * hd].astype(MXU_DTYPE)
            o_ref[:, h * hd:(h + 1) * hd] = jnp.dot(p.astype(MXU_DTYPE), v, preferred_element_type=F32).astype(o_ref.dtype)

    return hbm_call(
        body, name=name, grid=(S // tq,), in_specs=[pl.BlockSpec((tq, D), lambda i: (i, 0)), pl.BlockSpec((M, 2 * D), lambda i: (0, 0))],
        out_specs=pl.BlockSpec((tq, D), lambda i: (i, 0)), out_shape=jax.ShapeDtypeStruct((S, D), MXU_DTYPE),
        compiler_params=_params(("parallel",)),
    )(qc, kv)


def cross_bwd(qc, kv, do, name):
    S, D = qc.shape
    M = kv.shape[0]
    hd = D // CROSS_HEADS
    tq = min(CROSS_ROWS, S)

    def body(q_ref, kv_ref, do_ref, dq_ref, dkv_ref):
        @pl.when(pl.program_id(0) == 0)
        def _():
            dkv_ref[...] = jnp.zeros_like(dkv_ref)

        for h in range(CROSS_HEADS):
            q = q_ref[:, h * hd:(h + 1) * hd]
            k = kv_ref[:, h * hd:(h + 1) * hd]
            v = kv_ref[:, D + h * hd:D + (h + 1) * hd].astype(MXU_DTYPE)
            dom = do_ref[:, h * hd:(h + 1) * hd].astype(MXU_DTYPE)
            p = _cross_probs(q, k, hd ** -0.5)
            dp = lax.dot_general(dom, v, (((1,), (1,)), ((), ())), preferred_element_type=F32)
            ds = p * (dp - jnp.sum(p * dp, axis=1, keepdims=True)) * (hd ** -0.5)
            dq_ref[:, h * hd:(h + 1) * hd] = jnp.dot(ds.astype(MXU_DTYPE), k.astype(MXU_DTYPE),
                                                     preferred_element_type=F32).astype(dq_ref.dtype)
            dkv_ref[:, h * hd:(h + 1) * hd] += jnp.dot(ds.T.astype(MXU_DTYPE), q.astype(MXU_DTYPE), preferred_element_type=F32)
            dkv_ref[:, D + h * hd:D + (h + 1) * hd] += jnp.dot(p.T.astype(MXU_DTYPE), dom, preferred_element_type=F32)

    row = pl.BlockSpec((tq, D), lambda i: (i, 0))
    full = pl.BlockSpec((M, 2 * D), lambda i: (0, 0))
    return hbm_call(
        body, name=name, grid=(S // tq,), in_specs=[row, full, row], out_specs=[row, full],
        out_shape=[jax.ShapeDtypeStruct((S, D), MXU_DTYPE), jax.ShapeDtypeStruct((M, 2 * D), F32)],
        compiler_params=_params(("arbitrary",)),
    )(qc, kv, do)


def adamw(w, g, m, v, name, layers=None, into=None):
    shape = w.shape
    cols = shape[-1]
    lead = shape[0] if len(shape) > 2 else 1
    rows = int(np.prod(shape[:-1])) // lead
    w2, g2, m2, v2 = (t.reshape(lead, rows, cols) for t in (w, g, m, v))
    tr = _divisors(rows, SUBLANES, max(SUBLANES, (1 << 20) // (cols * 4) // SUBLANES * SUBLANES))[0]
    lo, hi = layers or (0, lead)
    done = [t.reshape(lead, rows, cols) for t in into] if into else []

    def body(w_ref, g_ref, m_ref, v_ref, *refs):
        d_ref, mo_ref, vo_ref, go_ref = refs[len(done):]
        gg = g_ref[...]
        mn = ADAM_B1 * m_ref[...] + (1.0 - ADAM_B1) * gg
        vn = ADAM_B2 * v_ref[...] + (1.0 - ADAM_B2) * (gg * gg)
        m_hat = mn / (1.0 - ADAM_B1 ** ADAM_STEP)
        v_hat = vn / (1.0 - ADAM_B2 ** ADAM_STEP)
        d_ref[...] = -ADAM_LR * (m_hat / (jnp.sqrt(v_hat) + ADAM_EPS) + ADAM_WD * w_ref[...])
        mo_ref[...] = mn
        vo_ref[...] = vn
        go_ref[...] = gg

    blk = pl.BlockSpec((None, tr, cols), lambda l, i: (l + lo, i, 0))
    sds = jax.ShapeDtypeStruct((lead, rows, cols), F32)
    d, mn, vn, go = hbm_call(body, name=name, grid=(hi - lo, rows // tr), in_specs=[blk] * 4 + [pl.BlockSpec(memory_space=pl.ANY)] * len(done),
                             out_specs=[blk] * 4, out_shape=[sds] * 4, input_output_aliases={4 + k: k for k in range(len(done))},
                             compiler_params=_params(("parallel", "parallel")))(w2, g2, m2, v2, *done)
    return d.reshape(shape), mn.reshape(shape), vn.reshape(shape), go.reshape(shape)


def sum_devices(parts, name):
    n, rows, cols = parts.shape

    def body(p_ref, o_ref):
        acc = p_ref[0]
        for k in range(1, n):
            acc = acc + p_ref[k]
        o_ref[...] = acc

    return pl.pallas_call(body, name=name, in_specs=[pl.BlockSpec(memory_space=pltpu.VMEM)],
                          out_specs=pl.BlockSpec(memory_space=pltpu.VMEM), out_shape=jax.ShapeDtypeStruct((rows, cols), F32))(parts)


HBM_SPEC = pl.BlockSpec(memory_space=pltpu.HBM)


def _place():
    return lax.axis_index("x"), lax.axis_index("y"), lax.axis_index("c")


def _remote(src, dst, send_sems, recv_sems, k, to):
    return pltpu.make_async_remote_copy(src_ref=src, dst_ref=dst, send_sem=send_sems.at[k], recv_sem=recv_sems.at[k],
                                        device_id=to, device_id_type=MESH_ID)


SEM_SPEC = pl.BlockSpec(memory_space=pltpu.SEMAPHORE)
ANY_SPEC = pl.BlockSpec(memory_space=pl.ANY)
SPLIT_COPY = pltpu.CompilerParams(has_side_effects=pltpu.SideEffectType.DATAFLOW_SIDE_EFFECTING)


def _in_hbm(arrays):
    return [pltpu.with_memory_space_constraint(a, pltpu.HBM) for a in arrays]


def _split_start(copies, sources, lands, after, n_sems, name):
    n = len(sources)

    def body(*refs):
        for cp in copies(refs[:n], refs[n:2 * n], refs[2 * n + 1], refs[2 * n + 2]):
            cp.start()
        refs[-1][...] = jnp.zeros_like(refs[-1])

    through = [pltpu.HBM(a.shape, a.dtype) for a in list(sources) + list(lands)]
    outs = pl.pallas_call(
        body, name=name, in_specs=[HBM_SPEC] * (2 * n) + [ANY_SPEC],
        out_specs=[SEM_SPEC, SEM_SPEC] + [HBM_SPEC] * (2 * n) + [pl.BlockSpec(memory_space=pltpu.VMEM)],
        out_shape=[pltpu.SemaphoreType.DMA((n_sems,)), pltpu.SemaphoreType.DMA((n_sems,))] + through
        + [jax.ShapeDtypeStruct((SUBLANES, LANES), F32)],
        input_output_aliases={i: 2 + i for i in range(2 * n)}, compiler_params=SPLIT_COPY,
    )(*_in_hbm(sources), *_in_hbm(lands), after)
    return outs[0], outs[1], outs[2:2 + n], outs[2 + n:2 + 2 * n], outs[-1]


def _split_wait(copies, send_sems, recv_sems, sources, lands, after, name):
    n = len(sources)

    def body(*refs):
        for cp in copies(refs[:n], refs[n:2 * n], refs[2 * n], refs[2 * n + 1]):
            cp.wait_send()
            cp.wait_recv()

    through = [pltpu.HBM(a.shape, a.dtype) for a in list(sources) + list(lands)]
    outs = pl.pallas_call(
        body, name=name, in_specs=[HBM_SPEC] * (2 * n) + [SEM_SPEC, SEM_SPEC, ANY_SPEC], out_specs=[HBM_SPEC] * (2 * n),
        out_shape=through, input_output_aliases={i: i for i in range(2 * n)}, compiler_params=SPLIT_COPY,
    )(*sources, *lands, send_sems, recv_sems, after)
    return outs[:n], outs[n:]


def _chip_slab(land, slot, rows):
    return land.at[slot, rows] if len(land.shape) == 3 else land.at[rows, slot]


def _gather_copies(w_refs, land_refs, send_sems, recv_sems):
    n = len(w_refs)
    x, y, c = _place()
    chips = [(1 - x, y), (x, 1 - y), (1 - x, 1 - y)]
    cps = []
    for a in range(n):
        hr = w_refs[a].shape[0] // 2
        mine, every = pl.ds(c * hr, hr), pl.ds(0, 2 * hr)
        cps.append(_remote(w_refs[a], _chip_slab(land_refs[a], 2 * x + y, every), send_sems, recv_sems, 3 * n + a, (x, y, 1 - c)))
        for k, chip in enumerate(chips):
            cps.append(_remote(w_refs[a].at[mine], _chip_slab(land_refs[a], 2 * x + y, mine), send_sems, recv_sems, 3 * a + k, (*chip, c)))
    return cps


def gather_start(shards, after, name):
    lands = [lax.empty(s.shape[:-2] + (N_CHIPS,) + s.shape[-2:], s.dtype) for s in shards]
    return _split_start(_gather_copies, shards, lands, after, 4 * len(shards), name)


def gather_wait(state, after, name):
    send_sems, recv_sems, sources, lands, _ = state
    return _split_wait(_gather_copies, send_sems, recv_sems, sources, lands, after, name)[1]


def gather_pass(lands, name):
    n = len(lands)

    def body(*refs):
        out_refs, send_sems, recv_sems = refs[n:2 * n], refs[2 * n], refs[2 * n + 1]
        x, y, c = _place()
        chips = [(1 - x, y), (x, 1 - y), (1 - x, 1 - y)]
        sent = []
        for a in range(n):
            hr = out_refs[a].shape[0 if len(out_refs[a].shape) == 4 else 1] // 2
            for k, (px, py) in enumerate(chips):
                landed = _chip_slab(out_refs[a], 2 * px + py, pl.ds(c * hr, hr))
                sent.append(_remote(landed, landed, send_sems, recv_sems, 3 * a + k, (x, y, 1 - c)))
        for cp in sent:
            cp.start()
        for a in range(n):
            hr = out_refs[a].shape[0 if len(out_refs[a].shape) == 4 else 1] // 2
            for k, (px, py) in enumerate(chips):
                theirs = _chip_slab(out_refs[a], 2 * px + py, pl.ds((1 - c) * hr, hr))
                _remote(theirs, theirs, send_sems, recv_sems, 3 * a + k, (x, y, 1 - c)).wait_recv()
        for cp in sent:
            cp.wait_send()

    return hbm_call(
        body, name=name, in_specs=[HBM_SPEC] * n, out_specs=[HBM_SPEC] * n,
        out_shape=[jax.ShapeDtypeStruct(a.shape, a.dtype) for a in lands], input_output_aliases={a: a for a in range(n)},
        scratch_shapes=[pltpu.SemaphoreType.DMA((3 * n,))] * 2,
    )(*lands)


def _scatter_copies(t_refs, land_refs, send_sems, recv_sems):
    x, y, c = _place()
    chips = [(1 - x, y), (x, 1 - y), (1 - x, 1 - y)]
    return [_remote(t_refs[a].at[:, 2 * px + py], land_refs[a].at[:, k], send_sems, recv_sems, 3 * a + k, (px, py, c))
            for a in range(len(t_refs)) for k, (px, py) in enumerate(chips)]


def scatter_start(parts, after, name):
    lands = [lax.empty((t.shape[0], N_CHIPS - 1) + t.shape[2:], t.dtype) for t in parts]
    return _split_start(_scatter_copies, parts, lands, after, 3 * len(parts), name)


def scatter_wait(state, after, name):
    send_sems, recv_sems, sources, lands, _ = state
    return _split_wait(_scatter_copies, send_sems, recv_sems, sources, lands, after, name)


def swap_sibling(parts, name):
    n = len(parts)

    def body(*refs):
        v_refs, out_refs, send_sems, recv_sems = refs[:n], refs[n:2 * n], refs[2 * n], refs[2 * n + 1]
        x, y, c = _place()
        cps = []
        for a in range(n):
            hr = v_refs[a].shape[2] // 2
            cps.append(_remote(v_refs[a].at[:, :, pl.ds((1 - c) * hr, hr)], out_refs[a], send_sems, recv_sems, a, (x, y, 1 - c)))
        for cp in cps:
            cp.start()
        for cp in cps:
            cp.wait()

    return hbm_call(
        body, name=name, in_specs=[HBM_SPEC] * n, out_specs=[HBM_SPEC] * n,
        out_shape=[jax.ShapeDtypeStruct(v.shape[:2] + (v.shape[2] // 2, v.shape[3]), v.dtype) for v in parts],
        scratch_shapes=[pltpu.SemaphoreType.DMA((n,))] * 2,
    )(*parts)


def join_halves(halves, layer, name):
    n = len(halves)

    def body(*refs):
        out_refs, send_sems, recv_sems = refs[n:2 * n], refs[2 * n], refs[2 * n + 1]
        x, y, c = _place()
        cps = []
        for a in range(n):
            hr = out_refs[a].shape[1] // 2
            mine = out_refs[a].at[layer, pl.ds(c * hr, hr)]
            cps.append(_remote(mine, mine, send_sems, recv_sems, a, (x, y, 1 - c)))
        for cp in cps:
            cp.start()
        for a in range(n):
            hr = out_refs[a].shape[1] // 2
            theirs = out_refs[a].at[layer, pl.ds((1 - c) * hr, hr)]
            _remote(theirs, theirs, send_sems, recv_sems, a, (x, y, 1 - c)).wait_recv()
        for cp in cps:
            cp.wait_send()

    return hbm_call(
        body, name=name, in_specs=[HBM_SPEC] * n, out_specs=[HBM_SPEC] * n,
        out_shape=[jax.ShapeDtypeStruct(f.shape, f.dtype) for f in halves], input_output_aliases={a: a for a in range(n)},
        scratch_shapes=[pltpu.SemaphoreType.DMA((n,))] * 2,
    )(*halves)


def gather_devices(v, name):
    def body(v_ref, out_ref, send_sems, recv_sems, local_sem):
        x, y, c = _place()
        me = 4 * x + 2 * y + c
        own = pltpu.make_async_copy(v_ref, out_ref.at[me], local_sem)
        own.start()
        peers = [((x + dx) % 2, (y + dy) % 2, (c + dc) % 2) for dx in (0, 1) for dy in (0, 1) for dc in (0, 1)][1:]
        sent = []
        for k, peer in enumerate(peers):
            cp = pltpu.make_async_remote_copy(src_ref=v_ref, dst_ref=out_ref.at[me], send_sem=send_sems.at[k], recv_sem=recv_sems.at[k],
                                              device_id=peer, device_id_type=MESH_ID)
            cp.start()
            sent.append(cp)
        for k, (px, py, pc) in enumerate(peers):
            slot = out_ref.at[4 * px + 2 * py + pc]
            pltpu.make_async_remote_copy(src_ref=slot, dst_ref=slot, send_sem=send_sems.at[k], recv_sem=recv_sems.at[k],
                                         device_id=(px, py, pc), device_id_type=MESH_ID).wait_recv()
        for cp in sent:
            cp.wait_send()
        own.wait()

    vm = pl.BlockSpec(memory_space=pltpu.VMEM)
    return pl.pallas_call(body, name=name, in_specs=[vm], out_specs=vm, out_shape=jax.ShapeDtypeStruct((N_DEV,) + v.shape, v.dtype),
                          scratch_shapes=[pltpu.SemaphoreType.DMA((N_DEV - 1,)), pltpu.SemaphoreType.DMA((N_DEV - 1,)),
                                          pltpu.SemaphoreType.DMA])(v)


ADD_ROWS = 512


def add_pair(place, a, b, name):
    L, n, hr, cols = b.shape
    tr = _divisors(hr, 2 * SUBLANES, ADD_ROWS)[0]
    nb = hr // tr

    def body(p_ref, a_ref, b_ref, o_ref):
        del p_ref
        o_ref[...] = (a_ref[...].astype(F32) + b_ref[...].astype(F32)).astype(o_ref.dtype)

    blk = pl.BlockSpec((None, None, tr, cols), lambda l, d, i, p: (l, d, i, 0))
    grid_spec = pltpu.PrefetchScalarGridSpec(
        num_scalar_prefetch=1, grid=(L, n, nb),
        in_specs=[pl.BlockSpec((None, None, tr, cols), lambda l, d, i, p: (l, d, p[0] * nb + i, 0)), blk], out_specs=blk)
    return hbm_call(body, name=name, grid_spec=grid_spec, out_shape=jax.ShapeDtypeStruct(b.shape, b.dtype),
                          compiler_params=_params(("parallel", "parallel", "parallel")))(place, a, b)


def add_chips(place, own, others, layer, stacked, name):
    _, n, hr, cols = others.shape
    tr = _divisors(hr, 2 * SUBLANES, ADD_ROWS)[0]
    nb = hr // tr
    create = isinstance(stacked, tuple)

    def body(p_ref, own_ref, *refs):
        del p_ref
        acc = own_ref[...].astype(F32)
        for k in range(n):
            acc = acc + refs[k][...].astype(F32)
        refs[-1][...] = acc

    ins = [pl.BlockSpec((None, None, tr, cols), lambda i, p: (0, p[1], i, 0))]
    ins += [pl.BlockSpec((None, None, tr, cols), functools.partial(lambda k, i, p: (0, k, i, 0), k)) for k in range(n)]
    grid_spec = pltpu.PrefetchScalarGridSpec(num_scalar_prefetch=1, grid=(nb,), in_specs=ins + ([] if create else [ANY_SPEC]),
                                             out_specs=pl.BlockSpec((None, tr, cols), lambda i, p: (layer, p[0] * nb + i, 0)))
    shape = stacked if create else stacked.shape
    return hbm_call(body, name=name, grid_spec=grid_spec, out_shape=jax.ShapeDtypeStruct(shape, F32),
                          input_output_aliases={} if create else {n + 2: 0},
                          compiler_params=_params(("parallel",)))(place, own, *([others] * n), *([] if create else [stacked]))


def _alpha(depth):
    return (2 * depth) ** 0.25


def _wmm(a, weight, mode, name, deps=(), **more):
    arr, how = weight
    return mm(a, arr, mode, name, deps=deps, **how, **more)


def layer_fwd(h, mem, w, tab, alpha, deps=(), late=None):
    D = h.shape[1]
    proj = _wmm(h, w["w_in"], "nn", "mm_proj", deps)
    xc, r, ig, a, b = rg_gates_fwd(proj, w["conv_w"], w["conv_b"], w["w_rg"], w["b_rg"], w["w_ig"], w["b_ig"], w["lru_lambda"], "rg_gates_fwd")
    hs, y_rnn = rg_scan_fwd(proj, a, b, "rg_scan_fwd")
    y_attn = attn_fwd(proj, w["sinks"], tab, D, "attn_fwd")
    deps = ()
    if late is not None:
        rest, deps = late(y_attn)
        w = {**w, **rest}
    pr = _wmm(y_rnn, w["w_br_rnn"], "nn", "mm_br_rnn", deps)
    pa = _wmm(y_attn, w["w_br_attn"], "nn", "mm_br_attn")
    merged = merge_fwd(proj, pr, pa, "merge_fwd")
    h1, xh1, rs1 = _wmm(merged, w["w_out"], "nn", "mm_out_ln1", post_norm=(h, w["ln1_g"], w["ln1_b"], alpha))
    qc = _wmm(h1, w["cq_w"], "nn", "mm_cq", out_dtype=MXU_DTYPE)
    kv = _wmm(mem, w["ckv_w"], "nn", "mm_ckv", out_dtype=MXU_DTYPE)
    o = cross_fwd(qc, kv, "cross_fwd")
    h2, xh2, rs2 = _wmm(o, w["co_w"], "nn", "mm_co_ln2", post_norm=(h1, w["ln2_g"], w["ln2_b"], alpha))
    gu = _wmm(h2, w["ffn_wi"], "nn", "mm_ffn_wi", out_blocks=2)
    act = swiglu_fwd(gu, "swiglu_fwd")
    h3, xh3, rs3 = _wmm(act, w["ffn_wo"], "nn", "mm_ffn_wo_ln3", post_norm=(h2, w["ln3_g"], w["ln3_b"], alpha))
    saved = dict(h=h, proj=proj, xc=xc, r=r, ig=ig, a=a, hs=hs, y_rnn=y_rnn, y_attn=y_attn, pr=pr, pa=pa, xh1=xh1, rs1=rs1, h1=h1,
                 qc=qc, kv=kv, o=o, xh2=xh2, rs2=rs2, h2=h2, gu=gu, xh3=xh3, rs3=rs3)
    return h3, saved, w


def layer_bwd(dh, mem, w, s, tab, alpha, deps=(), halfway=None):
    D = dh.shape[1]
    g = {}
    wg = dict(out_dtype=MXU_DTYPE)
    dz3, g["ln3_g"], g["ln3_b"] = ln_bwd(dh, None, s["xh3"], s["rs3"], w["ln3_g"], 1.0, "ln3_bwd")
    act = swiglu_fwd(s["gu"], "swiglu_refwd")
    g["ffn_wo"] = mm(act, dz3, "tn", "mm_d_ffn_wo", deps=deps, **wg)
    dact = _wmm(dz3, w["ffn_wo"], "nt", "mm_dact")
    dgu = swiglu_bwd(s["gu"], dact, "swiglu_bwd")
    g["ffn_wi"] = mm(s["h2"], dgu, "tn", "mm_d_ffn_wi", b_blocks=2, out_blocks=N_CHIPS, **wg)
    dh2 = _wmm(dgu, w["ffn_wi"], "nt", "mm_dh2", a_blocks=2)
    dz2, g["ln2_g"], g["ln2_b"] = ln_bwd(dz3, dh2, s["xh2"], s["rs2"], w["ln2_g"], alpha, "ln2_bwd")
    g["co_w"] = mm(s["o"], dz2, "tn", "mm_d_co", **wg)
    do = _wmm(dz2, w["co_w"], "nt", "mm_do", out_dtype=MXU_DTYPE)
    dqc, dkv = cross_bwd(s["qc"], s["kv"], do, "cross_bwd")
    g["cq_w"] = mm(s["h1"], dqc, "tn", "mm_d_cq", **wg)
    g["ckv_w"] = mm(mem, dkv, "tn", "mm_d_ckv", out_blocks=N_CHIPS, **wg)
    dh1 = _wmm(dqc, w["cq_w"], "nt", "mm_dh1")
    deps = halfway(g, dh1) if halfway is not None else ()
    dz1, g["ln1_g"], g["ln1_b"] = ln_bwd(dz2, dh1, s["xh1"], s["rs1"], w["ln1_g"], alpha, "ln1_bwd")
    merged = merge_fwd(s["proj"], s["pr"], s["pa"], "merge_refwd")
    g["w_out"] = mm(merged, dz1, "tn", "mm_d_out", deps=deps, **wg)
    dm = _wmm(dz1, w["w_out"], "nt", "mm_dmerged")
    dpr, dpa, dg_rnn, dg_attn = merge_bwd(s["proj"], s["pr"], s["pa"], dm, "merge_bwd")
    g["w_br_rnn"] = mm(s["y_rnn"], dpr, "tn", "mm_d_br_rnn", **wg)
    g["w_br_attn"] = mm(s["y_attn"], dpa, "tn", "mm_d_br_attn", **wg)
    dy_rnn = _wmm(dpr, w["w_br_rnn"], "nt", "mm_dy_rnn")
    dy_attn = _wmm(dpa, w["w_br_attn"], "nt", "mm_dy_attn")
    dq, dkb, dvb, dsink = attn_bwd(s["proj"], w["sinks"], tab, s["y_attn"], dy_attn, D, "attn_bwd")
    dk, dv = band_fold(dkb, dvb, "band_fold")
    g["sinks"] = dsink[:, :w["sinks"].shape[0]]
    dgr, gt = rg_scan_bwd(s["proj"], dy_rnn, s["hs"], s["a"], "rg_scan_bwd")
    dxc, g["w_rg"], g["w_ig"], g["b_rg"], g["b_ig"], g["lru_lambda"] = rg_gates_bwd(
        gt, s["hs"], s["xc"], s["r"], s["ig"], w["w_rg"], w["w_ig"], w["lru_lambda"], "rg_gates_bwd")
    dxr, g["conv_w"], g["conv_b"] = rg_conv_bwd(s["proj"], dxc, w["conv_w"], "rg_conv_bwd")
    dproj = jnp.concatenate([dxr, dgr, dq, dk, dv, dg_rnn, dg_attn], axis=1)
    g["w_in"] = mm(s["h"], dproj, "tn", "mm_d_in")
    dhm = _wmm(dproj, w["w_in"], "nt", "mm_dh")
    return axpby(dz1, dhm, alpha, "layer_dx"), g


def local_step(x, mem, target, depth, weights_of, grads_halfway, grads_done):
    alpha = _alpha(depth)
    tab = rope_table(x.shape[0])
    h, saved, layers = x, [], []
    for l in range(depth):
        wl, deps, late = weights_of(l, h)
        h, s, wl = layer_fwd(h, mem, wl, tab, alpha, deps, late)
        layers.append(wl)
        saved.append(s)
    dh, loss = loss_head(h, target, "loss_head")
    deps = ()
    for l in reversed(range(depth)):
        dh, g = layer_bwd(dh, mem, layers[l], saved[l], tab, alpha, deps, grads_halfway(l))
        deps = grads_done(l, g, dh)
    return loss, dh


def _pad_rows(flat):
    n = flat.shape[0]
    rows = -(-n // (LANES * SUBLANES)) * SUBLANES
    return jnp.pad(flat, (0, rows * LANES - n)).reshape(rows, LANES)


def kernel(x, mem, w_in, conv_w, conv_b, w_rg, b_rg, w_ig, b_ig, lru_lambda, w_br_rnn, w_br_attn, sinks, w_out, ln1_g, ln1_b, cq_w, ckv_w, co_w, ln2_g, ln2_b, ffn_wi, ffn_wo, ln3_g, ln3_b, loss_target, m_w_in, m_conv_w, m_conv_b, m_w_rg, m_b_rg, m_w_ig, m_b_ig, m_lru_lambda, m_w_br_rnn, m_w_br_attn, m_sinks, m_w_out, m_ln1_g, m_ln1_b, m_cq_w, m_ckv_w, m_co_w, m_ln2_g, m_ln2_b, m_ffn_wi, m_ffn_wo, m_ln3_g, m_ln3_b, v_w_in, v_conv_w, v_conv_b, v_w_rg, v_b_rg, v_w_ig, v_b_ig, v_lru_lambda, v_w_br_rnn, v_w_br_attn, v_sinks, v_w_out, v_ln1_g, v_ln1_b, v_cq_w, v_ckv_w, v_co_w, v_ln2_g, v_ln2_b, v_ffn_wi, v_ffn_wo, v_ln3_g, v_ln3_b):
    args = dict(locals())
    w = {n: args[n] for n in WEIGHTS}
    m = {n: args["m_" + n] for n in WEIGHTS}
    v = {n: args["v_" + n] for n in WEIGHTS}
    cx, cy, cc = _place()
    chip = 2 * cx + cy
    L = w_in.shape[0]

    place = jnp.stack([cc, chip]).astype(jnp.int32)
    cw_rows = _pad_rows(conv_w.reshape(-1))
    cw_all = gather_devices(cw_rows, "gather_conv_w")[0::2]
    cw_parts = cw_all.reshape(N_CHIPS, -1)[:, :conv_w.size].reshape((N_CHIPS,) + conv_w.shape)
    conv_full = jnp.concatenate([cw_parts[k] for k in range(N_CHIPS)], axis=2)

    shards = [{n: w[n][l].astype(MXU_DTYPE) for n in BIG} for l in range(L)]
    late_names = tuple(n for n in BIG if n not in GATHER_FIRST)
    gathering = {(0, GATHER_FIRST): gather_start([shards[0][n] for n in GATHER_FIRST], cw_rows, "gather_start_0a")}
    gathering[0, late_names] = gather_start([shards[0][n] for n in late_names], gathering[0, GATHER_FIRST][4], "gather_start_0b")

    def gathered(l, names, after, tag):
        lands = gather_pass(gather_wait(gathering.pop((l, names)), after, f"gather_wait_{tag}"), f"gather_pass_{tag}")
        wl = {}
        for n, gw in zip(names, lands):
            rows_joined = gw.reshape(gw.shape[:-3] + (-1, gw.shape[-1]))
            if n == "w_in":
                wl[n] = (jnp.concatenate([gw[k] for k in range(N_CHIPS)], axis=1), {})
            elif n in COL_BLOCKED:
                wl[n] = (gw, dict(b_blocks=N_CHIPS))
            elif n in GATE_WEIGHTS:
                wl[n] = rows_joined
            else:
                wl[n] = (rows_joined, {})
        return wl, lands

    def start_next(l, after):
        if l + 1 == L:
            return ()
        gathering[l + 1, BIG] = gather_start([shards[l + 1][n] for n in BIG], after, f"gather_start_{l + 1}")
        return (gathering[l + 1, BIG][4],)

    def weights_of(l, h):
        deps, late = (), None
        if l == 0:
            wl, _ = gathered(0, GATHER_FIRST, h, "0a")

            def late(after):
                rest, lands = gathered(0, late_names, after, "0b")
                return rest, start_next(0, lands[0])
        else:
            wl, lands = gathered(l, BIG, h, str(l))
            deps = start_next(l, lands[0])
        for n in SMALL:
            wl[n] = conv_full[l] if n == "conv_w" else w[n][l] if n == "sinks" else w[n][l][None, :]
        return wl, deps, late

    def for_chips(n, g):
        if n in COL_BLOCKED:
            return g
        if n in GATE_WEIGHTS:
            nb, bw, _ = g.shape
            g = g.reshape(nb, N_CHIPS, bw // N_CHIPS, bw).transpose(1, 0, 2, 3).reshape(N_CHIPS, nb * bw // N_CHIPS, bw)
        elif SHARD_AXIS[n] == 0:
            g = g.reshape(N_CHIPS, g.shape[0] // N_CHIPS, g.shape[1])
        else:
            g = jnp.stack(jnp.split(g, N_CHIPS, axis=1))
        return g.astype(MXU_DTYPE)

    reduced, scattering, small_grads = {}, {}, [None] * L
    late_grads = tuple(n for n in BIG if n not in SCATTER_FIRST)

    def start_scatter(l, names, g, after, tag):
        partial_sums = [for_chips(n, g[n])[None] for n in names]
        from_sibling = swap_sibling(partial_sums, f"grad_to_sibling_{tag}")
        chip_sums = [add_pair(place, a, b, f"grad_add_pair_{n}_{l}") for n, a, b in zip(names, partial_sums, from_sibling)]
        scattering[l, names] = scatter_start(chip_sums, after, f"grad_scatter_start_{tag}")
        return (scattering[l, names][4],)

    def finish_layer(l, after):
        for names in [k[1] for k in list(scattering) if k[0] == l]:
            tag = str(l) if names == BIG else f"{l}{'a' if names == SCATTER_FIRST else 'b'}"
            chip_sums, from_chips = scatter_wait(scattering.pop((l, names)), after, f"grad_scatter_wait_{tag}")
            for n, own, others in zip(names, chip_sums, from_chips):
                target = reduced.get(n, (L, 2 * own.shape[2], own.shape[3]))
                reduced[n] = add_chips(place, own, others, l, target, f"grad_add_chips_{n}_{l}")
        reduced.update(zip(BIG, join_halves([reduced[n] for n in BIG], l, f"grad_join_{l}")))

    def grads_halfway(l):
        if l > 0:
            return None

        def halfway(g, after):
            if L > 1:
                finish_layer(1, after)
            return start_scatter(0, SCATTER_FIRST, g, after, "0a")

        return halfway

    def grads_done(l, g, dh):
        small_grads[l] = {n: g[n] for n in SMALL}
        if l == 0:
            return start_scatter(0, late_grads, g, dh, "0b")
        if l + 1 < L:
            finish_layer(l + 1, dh)
        return start_scatter(l, BIG, g, dh, str(l))

    loss11, dx = local_step(x[0], mem[0], loss_target[0], L, weights_of, grads_halfway, grads_done)
    loss = lax.psum(loss11[0, 0], ("x", "y", "c"))
    updated, after = {}, dx
    if L > 1:
        for n in BIG:
            updated[n] = adamw(w[n], reduced[n].reshape(w[n].shape), m[n], v[n], f"adamw_{n}_upper", layers=(1, L))
        after = jnp.stack([updated[n][0][(0,) * w[n].ndim] for n in BIG])
    finish_layer(0, after)
    gshard = {n: reduced[n].reshape(w[n].shape) for n in BIG}

    small_full = {n: jnp.stack([gl[n] for gl in small_grads]).reshape(w[n].shape[:1] + ((CONV_WIDTH, -1) if n == "conv_w" else (-1,)))
                  for n in SMALL}
    small_flat = jnp.concatenate([small_full[n].reshape(-1) for n in SMALL])
    small_sum = sum_devices(gather_devices(_pad_rows(small_flat), "gather_small_grads"), "sum_small_grads").reshape(-1)
    off = 0
    for n in SMALL:
        gfull = small_sum[off:off + small_full[n].size].reshape(small_full[n].shape)
        off += small_full[n].size
        if n == "conv_w":
            width = conv_w.shape[2]
            gfull = lax.dynamic_slice_in_dim(gfull, chip * width, width, axis=2)
        gshard[n] = gfull

    delta, new_m, new_v, grad = {}, {}, {}, {}
    for n in WEIGHTS:
        some = dict(layers=(0, 1), into=updated[n]) if n in updated else {}
        delta[n], new_m[n], new_v[n], grad[n] = adamw(w[n], gshard[n], m[n], v[n], "adamw_" + n, **some)
    return (loss, dx[None], *[grad[n] for n in WEIGHTS], *[delta[n] for n in WEIGHTS], *[new_m[n] for n in WEIGHTS],
            *[new_v[n] for n in WEIGHTS])
```

```python
import functools
import math

import jax
import jax.numpy as jnp
import numpy as np
from jax import lax
from jax.experimental import pallas as pl
from jax.experimental.pallas import tpu as pltpu

F32 = jnp.float32
BF16 = jnp.bfloat16
MXU_DTYPE = BF16

HEAD_DIM = 64
N_KV_HEADS = 2
WINDOW = 128
ROT_DIM = HEAD_DIM // 4
ROPE_THETA = 500000.0
CROSS_HEADS = 4
RNN_BLOCKS = 4
CONV_WIDTH = 4
LRU_C = 8.0
LN_EPS = 1e-5
NEG_INF = -1e30
ADAM_LR = 0.001
ADAM_B1 = 0.9
ADAM_B2 = 0.999
ADAM_EPS = 1e-08
ADAM_WD = 0.01
ADAM_STEP = 10

VMEM_BYTES_V7X = 64 * 1024 * 1024
VMEM_BLOCK_BUDGET = 36 * 1024 * 1024
LANES = 128
SUBLANES = 8

MESH_ID = pl.DeviceIdType.MESH
N_CHIPS = 4
N_DEV = 8

BIG = ("w_in", "w_rg", "w_ig", "w_br_rnn", "w_br_attn", "w_out", "cq_w", "ckv_w", "co_w", "ffn_wi", "ffn_wo")
SHARD_AXIS = {"w_in": 1, "w_rg": 1, "w_ig": 1, "w_br_rnn": 0, "w_br_attn": 0, "w_out": 0, "cq_w": 0, "ckv_w": 1,
              "co_w": 0, "ffn_wi": 1, "ffn_wo": 0}
SMALL = ("conv_w", "conv_b", "b_rg", "b_ig", "lru_lambda", "sinks", "ln1_g", "ln1_b", "ln2_g", "ln2_b", "ln3_g", "ln3_b")
WEIGHTS = ("w_in", "conv_w", "conv_b", "w_rg", "b_rg", "w_ig", "b_ig", "lru_lambda", "w_br_rnn", "w_br_attn", "sinks",
           "w_out", "ln1_g", "ln1_b", "cq_w", "ckv_w", "co_w", "ln2_g", "ln2_b", "ffn_wi", "ffn_wo", "ln3_g", "ln3_b")
GATE_WEIGHTS = ("w_rg", "w_ig")
COL_BLOCKED = ("ckv_w", "ffn_wi")
GATHER_FIRST = ("w_in", "w_rg", "w_ig")
SCATTER_FIRST = ("ffn_wo", "ffn_wi", "co_w", "cq_w", "ckv_w")


def _params(dims=None, vmem=None):
    return pltpu.CompilerParams(dimension_semantics=dims, vmem_limit_bytes=vmem)


def _vmem_limit(block_bytes, temp_bytes=0):
    want = int(2 * block_bytes + temp_bytes) + (6 << 20)
    return max(32 << 20, min(want, VMEM_BYTES_V7X - (6 << 20)))


def _divisors(n, align, cap):
    out = [d for d in range(align, min(n, cap) + 1, align) if n % d == 0]
    if n <= cap and n not in out:
        out.append(n)
    return sorted(out, reverse=True) or [n]


PIN_MIN_ELEMENTS = 1 << 18


def hbm_call(body, **kw):
    def in_hbm(s):
        return pltpu.HBM(s.shape, s.dtype) if math.prod(s.shape) >= PIN_MIN_ELEMENTS else s

    shapes = kw.pop("out_shape")
    shapes = [in_hbm(s) for s in shapes] if isinstance(shapes, (list, tuple)) else in_hbm(shapes)
    call = pl.pallas_call(body, out_shape=shapes, **kw)

    def run(*args):
        return call(*[pltpu.with_memory_space_constraint(a, pltpu.HBM) if a.size >= PIN_MIN_ELEMENTS else a for a in args])

    return run


def _sigmoid(x):
    return 1.0 / (1.0 + jnp.exp(-x))


def _gelu_parts(x):
    c = math.sqrt(2.0 / math.pi)
    u = c * (x + 0.044715 * x * x * x)
    t = jnp.tanh(u)
    return t, c * (1.0 + 3 * 0.044715 * x * x)


def _gelu(x):
    t, _ = _gelu_parts(x)
    return 0.5 * x * (1.0 + t)


def _gelu_grad(x):
    t, du = _gelu_parts(x)
    return 0.5 * (1.0 + t) + 0.5 * x * (1.0 - t * t) * du


def _neg_expm1(x):
    series = x * (1.0 + x * (0.5 + x * (1.0 / 6 + x * (1.0 / 24 + x * (1.0 / 120)))))
    return -jnp.where(x > -0.1, series, jnp.exp(x) - 1.0)


def _softplus_neg(lam):
    x = -lam
    return jnp.maximum(x, 0.0) + jnp.log1p(jnp.exp(-jnp.abs(x)))


STEP_US = 0.35
HBM_BYTES_PER_US = 2.5e6
MXU_FLOPS_PER_US = 7e8


def _layer_norm(z, g, b):
    mu = jnp.mean(z, axis=-1, keepdims=True)
    zc = z - mu
    rs = lax.rsqrt(jnp.mean(zc * zc, axis=-1, keepdims=True) + LN_EPS)
    xh = zc * rs
    return xh * g + b, xh, rs


def mm(a, b, mode, name, *, b_index=(), a_blocks=0, b_blocks=0, out_blocks=0, out_dtype=F32, deps=(), post_norm=None):
    nlead = len(b_index) + (1 if b_blocks else 0)
    bk, bn = b.shape[nlead:]
    M, K = (a.shape[-1], a.shape[-2]) if mode == "tn" else (a.shape[-2], a.shape[-1] * max(a_blocks, 1))
    N = bk if mode == "nt" else bn * max(b_blocks, 1) if mode == "nn" or mode == "tn" else bn
    asz, bsz, osz = a.dtype.itemsize, b.dtype.itemsize, jnp.dtype(out_dtype).itemsize
    n_unit = math.gcd(N // max(out_blocks, 1), N // max(b_blocks, 1) if mode != "nt" else N)
    k_unit = math.gcd(K // max(a_blocks, 1), K // max(b_blocks, 1) if mode == "nt" else K)
    tms = _divisors(M, LANES if mode == "tn" else SUBLANES, 2048)
    tns = [N] if post_norm else _divisors(n_unit, LANES, 2048)
    tks = _divisors(k_unit, LANES, k_unit)
    best = None
    for tm in tms:
        for tn in tns:
            for tk in tks:
                nk = K // tk
                scratch = tm * tn * 4 if (nk > 1 and osz != 4) else 0
                blocks = tm * tk * asz + tn * tk * bsz + tm * tn * osz * (3 if post_norm else 1)
                temps = tm * tk * (2 + (4 if mode == "tn" else 0)) + tn * tk * 2 + tm * tn * 4 + scratch
                if 2 * blocks + temps > VMEM_BLOCK_BUDGET + (8 << 20):
                    continue
                ni, nj = M // tm, N // tn
                traffic = M * K * asz * (nj if nk > 1 else 1) + N * K * bsz * (1 if nj * nk == 1 else ni) + M * N * osz
                busy = max(traffic / HBM_BYTES_PER_US, 2.0 * M * N * K / MXU_FLOPS_PER_US)
                cost = ni * nj * nk * STEP_US + busy + blocks / HBM_BYTES_PER_US
                if best is None or cost < best[0]:
                    best = (cost, tm, tn, tk, blocks, temps)
    _, tm, tn, tk, blocks, temps = best
    nk = K // tk
    use_scratch = nk > 1 and osz != 4

    def split(index, total, blocks, tile):
        per = total // blocks // tile
        return index // per, index % per

    def body(a_ref, b_ref, *rest):
        rest = rest[len(deps):]
        if post_norm:
            h_ref, g_ref, beta_ref, o_ref, xh_ref, rs_ref = rest[:6]
            acc = rest[6:]
        else:
            o_ref, acc = rest[0], rest[1:]
        av = a_ref[...].astype(MXU_DTYPE)
        bv = b_ref[...].astype(MXU_DTYPE)
        dn = {"nn": (((1,), (0,)), ((), ())), "nt": (((1,), (1,)), ((), ())), "tn": (((0,), (0,)), ((), ()))}[mode]
        r = lax.dot_general(av, bv, dn, preferred_element_type=F32)

        def normalise(f):
            o_ref[...], xh_ref[...], rs_ref[...] = _layer_norm(post_norm[3] * h_ref[...] + f, g_ref[...], beta_ref[...])

        if nk == 1 and post_norm:
            normalise(r)
        elif nk == 1:
            o_ref[...] = r.astype(o_ref.dtype)
        else:
            acc_ref = acc[0] if use_scratch else o_ref

            @pl.when(pl.program_id(2) == 0)
            def _():
                acc_ref[...] = r

            @pl.when(pl.program_id(2) > 0)
            def _():
                acc_ref[...] += r

            if use_scratch:
                @pl.when(pl.program_id(2) == nk - 1)
                def _():
                    o_ref[...] = acc_ref[...].astype(o_ref.dtype)
            elif post_norm:
                @pl.when(pl.program_id(2) == nk - 1)
                def _():
                    normalise(o_ref[...])

    if mode == "tn":
        a_spec = pl.BlockSpec((tk, tm), lambda i, j, k: (k, i))
    elif a_blocks:
        a_spec = pl.BlockSpec((None, tm, tk), lambda i, j, k: (split(k, K, a_blocks, tk)[0], i, split(k, K, a_blocks, tk)[1]))
    else:
        a_spec = pl.BlockSpec((tm, tk), lambda i, j, k: (i, k))
    lead = (None,) * nlead
    if mode == "nt":
        bmap = ((lambda i, j, k: b_index + (split(k, K, b_blocks, tk)[0], j, split(k, K, b_blocks, tk)[1])) if b_blocks
                else (lambda i, j, k: b_index + (j, k)))
        b_spec = pl.BlockSpec(lead + (tn, tk), bmap)
    else:
        bmap = ((lambda i, j, k: b_index + (split(j, N, b_blocks, tn)[0], k, split(j, N, b_blocks, tn)[1])) if b_blocks
                else (lambda i, j, k: b_index + (k, j)))
        b_spec = pl.BlockSpec(lead + (tk, tn), bmap)
    if out_blocks:
        o_spec = pl.BlockSpec((None, tm, tn), lambda i, j, k: (split(j, N, out_blocks, tn)[0], i, split(j, N, out_blocks, tn)[1]))
        o_shape = jax.ShapeDtypeStruct((out_blocks, M, N // out_blocks), out_dtype)
    else:
        o_spec = pl.BlockSpec((tm, tn), lambda i, j, k: (i, j))
        o_shape = jax.ShapeDtypeStruct((M, N), out_dtype)
    in_specs, extra = [a_spec, b_spec] + [pl.BlockSpec(memory_space=pl.ANY)] * len(deps), ()
    if post_norm:
        vec = pl.BlockSpec((1, N), lambda i, j, k: (0, 0))
        in_specs += [pl.BlockSpec((tm, N), lambda i, j, k: (i, 0)), vec, vec]
        o_spec = [o_spec, pl.BlockSpec((tm, N), lambda i, j, k: (i, 0)), pl.BlockSpec((tm, 1), lambda i, j, k: (i, 0))]
        o_shape = [o_shape, jax.ShapeDtypeStruct((M, N), F32), jax.ShapeDtypeStruct((M, 1), F32)]
        extra = post_norm[:3]
    return hbm_call(
        body, name=name, grid=(M // tm, N // tn, nk), in_specs=in_specs, out_specs=o_spec, out_shape=o_shape,
        scratch_shapes=[pltpu.VMEM((tm, tn), F32)] if use_scratch else [],
        compiler_params=_params(("parallel", "parallel", "arbitrary"), _vmem_limit(blocks, temps)),
    )(a, b, *deps, *extra)


ROW_TILE = 512
GATE_ROWS = 1024


def ln_bwd(dy_a, dy_b, xh, rs, g, c1, name):
    S, D = xh.shape
    tr = min(ROW_TILE, S)
    two = dy_b is not None

    def body(*refs):
        if two:
            a_ref, b_ref, xh_ref, rs_ref, g_ref, dz_ref, dg_ref, db_ref = refs
            dy = c1 * a_ref[...] + b_ref[...]
        else:
            a_ref, xh_ref, rs_ref, g_ref, dz_ref, dg_ref, db_ref = refs
            dy = a_ref[...]
        x = xh_ref[...]
        dyg = dy * g_ref[...]
        m1 = jnp.mean(dyg, axis=-1, keepdims=True)
        m2 = jnp.mean(dyg * x, axis=-1, keepdims=True)
        dz_ref[...] = rs_ref[...] * (dyg - m1 - x * m2)

        @pl.when(pl.program_id(0) == 0)
        def _():
            dg_ref[...] = jnp.zeros_like(dg_ref)
            db_ref[...] = jnp.zeros_like(db_ref)

        dg_ref[...] += jnp.sum(dy * x, axis=0, keepdims=True)
        db_ref[...] += jnp.sum(dy, axis=0, keepdims=True)

    row = pl.BlockSpec((tr, D), lambda i: (i, 0))
    vec = pl.BlockSpec((1, D), lambda i: (0, 0))
    ins = [row, row] if two else [row]
    args = (dy_a, dy_b) if two else (dy_a,)
    return hbm_call(
        body, name=name, grid=(S // tr,), in_specs=ins + [row, pl.BlockSpec((tr, 1), lambda i: (i, 0)), vec],
        out_specs=[row, vec, vec],
        out_shape=[jax.ShapeDtypeStruct((S, D), F32), jax.ShapeDtypeStruct((1, D), F32), jax.ShapeDtypeStruct((1, D), F32)],
        compiler_params=_params(("arbitrary",), 48 << 20),
    )(*args, xh, rs, g)


def axpby(a, b, c1, name):
    S, D = a.shape
    tr = min(ROW_TILE, S)

    def body(a_ref, b_ref, o_ref):
        o_ref[...] = c1 * a_ref[...] + b_ref[...]

    row = pl.BlockSpec((tr, D), lambda i: (i, 0))
    return hbm_call(body, name=name, grid=(S // tr,), in_specs=[row, row], out_specs=row,
                          out_shape=jax.ShapeDtypeStruct((S, D), F32), compiler_params=_params(("parallel",)))(a, b)


def loss_head(y, t, name):
    S, D = y.shape
    tr = min(ROW_TILE, S)
    nsteps = S // tr

    def body(y_ref, t_ref, dy_ref, l_ref, acc_ref):
        i = pl.program_id(0)

        @pl.when(i == 0)
        def _():
            acc_ref[...] = jnp.zeros_like(acc_ref)

        e = y_ref[...] - t_ref[...]
        dy_ref[...] = e * (1.0 / D)
        acc_ref[...] += jnp.sum(e * e, axis=0, keepdims=True)

        @pl.when(i == nsteps - 1)
        def _():
            l_ref[...] = jnp.sum(acc_ref[...], axis=1, keepdims=True) * (0.5 / D)

    row = pl.BlockSpec((tr, D), lambda i: (i, 0))
    return hbm_call(
        body, name=name, grid=(nsteps,), in_specs=[row, row],
        out_specs=[row, pl.BlockSpec((1, 1), lambda i: (0, 0))],
        out_shape=[jax.ShapeDtypeStruct((S, D), F32), jax.ShapeDtypeStruct((1, 1), F32)],
        scratch_shapes=[pltpu.VMEM((1, D), F32)], compiler_params=_params(("arbitrary",)),
    )(y, t)


SWIGLU_ROWS = 256


def swiglu_fwd(gu, name):
    _, S, Fh = gu.shape
    tc = _divisors(Fh, LANES, 1536)[0]
    tr = min(SWIGLU_ROWS, S)

    def body(gu_ref, o_ref):
        g = gu_ref[0]
        o_ref[...] = (g * _sigmoid(g) * gu_ref[1]).astype(o_ref.dtype)

    return hbm_call(
        body, name=name, grid=(S // tr, Fh // tc), in_specs=[pl.BlockSpec((2, tr, tc), lambda i, j: (0, i, j))],
        out_specs=pl.BlockSpec((tr, tc), lambda i, j: (i, j)), out_shape=jax.ShapeDtypeStruct((S, Fh), MXU_DTYPE),
        compiler_params=_params(("parallel", "parallel")),
    )(gu)


def swiglu_bwd(gu, dact, name):
    _, S, Fh = gu.shape
    tc = _divisors(Fh, LANES, 1536)[0]
    tr = min(SWIGLU_ROWS, S)

    def body(gu_ref, d_ref, o_ref):
        g, u, d = gu_ref[0], gu_ref[1], d_ref[...]
        s = _sigmoid(g)
        o_ref[0] = (d * u * (s * (1.0 + g * (1.0 - s)))).astype(o_ref.dtype)
        o_ref[1] = (d * (g * s)).astype(o_ref.dtype)

    both = pl.BlockSpec((2, tr, tc), lambda i, j: (0, i, j))
    return hbm_call(
        body, name=name, grid=(S // tr, Fh // tc), in_specs=[both, pl.BlockSpec((tr, tc), lambda i, j: (i, j))],
        out_specs=both, out_shape=jax.ShapeDtypeStruct((2, S, Fh), MXU_DTYPE), compiler_params=_params(("parallel", "parallel")),
    )(gu, dact)


GATE_COLS = 256


def merge_fwd(proj, pr, pa, name):
    S, D = pr.shape
    tr = min(GATE_ROWS, S)
    c0 = (3 * D + 2 * N_KV_HEADS * HEAD_DIM) // GATE_COLS
    c1 = c0 + D // GATE_COLS

    def body(gr_ref, ga_ref, pr_ref, pa_ref, o_ref):
        o_ref[...] = (_sigmoid(gr_ref[...]) * pr_ref[...] + _sigmoid(ga_ref[...]) * pa_ref[...]).astype(o_ref.dtype)

    blk = pl.BlockSpec((tr, GATE_COLS), lambda i, j: (i, j))
    return hbm_call(
        body, name=name, grid=(S // tr, D // GATE_COLS),
        in_specs=[pl.BlockSpec((tr, GATE_COLS), lambda i, j: (i, c0 + j)), pl.BlockSpec((tr, GATE_COLS), lambda i, j: (i, c1 + j)),
                  blk, blk],
        out_specs=blk, out_shape=jax.ShapeDtypeStruct((S, D), MXU_DTYPE), compiler_params=_params(("parallel", "parallel")),
    )(proj, proj, pr, pa)


def merge_bwd(proj, pr, pa, dm, name):
    S, D = pr.shape
    tr = min(GATE_ROWS, S)
    c0 = (3 * D + 2 * N_KV_HEADS * HEAD_DIM) // GATE_COLS
    c1 = c0 + D // GATE_COLS

    def body(gr_ref, ga_ref, pr_ref, pa_ref, dm_ref, dpr_ref, dpa_ref, dgr_ref, dga_ref):
        sr, sa, d = _sigmoid(gr_ref[...]), _sigmoid(ga_ref[...]), dm_ref[...]
        dpr_ref[...] = (d * sr).astype(dpr_ref.dtype)
        dpa_ref[...] = (d * sa).astype(dpa_ref.dtype)
        dgr_ref[...] = (d * pr_ref[...] * (sr * (1.0 - sr))).astype(dgr_ref.dtype)
        dga_ref[...] = (d * pa_ref[...] * (sa * (1.0 - sa))).astype(dga_ref.dtype)

    blk = pl.BlockSpec((tr, GATE_COLS), lambda i, j: (i, j))
    sds = jax.ShapeDtypeStruct((S, D), MXU_DTYPE)
    return hbm_call(
        body, name=name, grid=(S // tr, D // GATE_COLS),
        in_specs=[pl.BlockSpec((tr, GATE_COLS), lambda i, j: (i, c0 + j)), pl.BlockSpec((tr, GATE_COLS), lambda i, j: (i, c1 + j)),
                  blk, blk, blk],
        out_specs=[blk, blk, blk, blk], out_shape=[sds, sds, sds, sds], compiler_params=_params(("parallel", "parallel")),
    )(proj, proj, pr, pa, dm)


RG_ROWS = 512


def _shift_down(cur, prev, d, row, first):
    halo = jnp.where(first, 0.0, pltpu.roll(prev, d, 0))
    return jnp.where(row >= d, pltpu.roll(cur, d, 0), halo)


def _shift_up(cur, nxt, d, row, last, tr):
    halo = jnp.where(last, 0.0, pltpu.roll(nxt, tr - d, 0))
    return jnp.where(row < tr - d, pltpu.roll(cur, tr - d, 0), halo)


def _lru_coeffs(r, lam):
    sp = _softplus_neg(lam)
    la = -LRU_C * r * sp
    return sp, la, jnp.exp(la), _neg_expm1(2.0 * la)


def rg_gates_fwd(proj, conv_w, conv_b, w_rg, b_rg, w_ig, b_ig, lam, name):
    S = proj.shape[0]
    nblk, bw, _ = w_rg.shape
    D = nblk * bw
    tr = min(RG_ROWS, S)

    def body(xr_ref, xp_ref, cw_ref, cb_ref, wr_ref, br_ref, wi_ref, bi_ref, lam_ref, xc_ref, r_ref, i_ref, a_ref, b_ref):
        first = pl.program_id(1) == 0
        cur, prev = xr_ref[...], xp_ref[...]
        row = lax.broadcasted_iota(jnp.int32, cur.shape, 0)
        xc = cb_ref[...]
        for k in range(CONV_WIDTH - 1):
            xc = xc + _shift_down(cur, prev, CONV_WIDTH - 1 - k, row, first) * cw_ref[k:k + 1, :]
        xc = xc + cur * cw_ref[CONV_WIDTH - 1:CONV_WIDTH, :]
        xm = xc.astype(MXU_DTYPE)
        r = _sigmoid(jnp.dot(xm, wr_ref[...].astype(MXU_DTYPE), preferred_element_type=F32) + br_ref[...])
        ig = _sigmoid(jnp.dot(xm, wi_ref[...].astype(MXU_DTYPE), preferred_element_type=F32) + bi_ref[...])
        _, _, a, em = _lru_coeffs(r, lam_ref[...])
        xc_ref[...] = xc
        r_ref[...] = r
        i_ref[...] = ig
        a_ref[...] = a
        b_ref[...] = jnp.sqrt(em) * (ig * xc)

    tile = pl.BlockSpec((tr, bw), lambda n, i: (i, n))
    vec = pl.BlockSpec((1, bw), lambda n, i: (0, n))
    wblk = pl.BlockSpec((None, bw, bw), lambda n, i: (n, 0, 0))
    sds = jax.ShapeDtypeStruct((S, D), F32)
    return hbm_call(
        body, name=name, grid=(nblk, S // tr),
        in_specs=[tile, pl.BlockSpec((tr, bw), lambda n, i: (jnp.maximum(i - 1, 0), n)),
                  pl.BlockSpec((CONV_WIDTH, bw), lambda n, i: (0, n)), vec, wblk, vec, wblk, vec, vec],
        out_specs=[tile] * 5, out_shape=[sds] * 5, compiler_params=_params(("parallel", "parallel")),
    )(proj, proj, conv_w, conv_b, w_rg, b_rg, w_ig, b_ig, lam)


SCAN_COLS = 256
CHUNK = SUBLANES
SCAN_UNROLL = 4


def rg_scan_fwd(proj, a, b, name):
    S, D = a.shape
    cb = min(SCAN_COLS, D)
    goff = D // cb

    def body(a_ref, b_ref, g_ref, hs_ref, y_ref):
        row = lax.broadcasted_iota(jnp.int32, (CHUNK, cb), 0)

        def step(c, carry):
            r0 = pl.multiple_of(c * CHUNK, CHUNK)
            A = a_ref[pl.ds(r0, CHUNK), :]
            B = b_ref[pl.ds(r0, CHUNK), :]
            for d in (1, 2, 4):
                As = jnp.where(row >= d, pltpu.roll(A, d, 0), 1.0)
                Bs = jnp.where(row >= d, pltpu.roll(B, d, 0), 0.0)
                B = A * Bs + B
                A = A * As
            hs_ref[pl.ds(r0, CHUNK), :] = B + A * carry
            a_end = jnp.sum(jnp.where(row == CHUNK - 1, A, 0.0), axis=0, keepdims=True)
            b_end = jnp.sum(jnp.where(row == CHUNK - 1, B, 0.0), axis=0, keepdims=True)
            return b_end + a_end * carry

        lax.fori_loop(0, S // CHUNK, step, jnp.zeros((1, cb), F32), unroll=SCAN_UNROLL)
        y_ref[...] = (hs_ref[...] * _gelu(g_ref[...])).astype(y_ref.dtype)

    col = pl.BlockSpec((S, cb), lambda j: (0, j))
    return hbm_call(
        body, name=name, grid=(D // cb,), in_specs=[col, col, pl.BlockSpec((S, cb), lambda j: (0, goff + j))],
        out_specs=[col, col], out_shape=[jax.ShapeDtypeStruct((S, D), F32), jax.ShapeDtypeStruct((S, D), MXU_DTYPE)],
        compiler_params=_params(("parallel",), _vmem_limit(5 * S * cb * 4, 4 * S * cb * 4)),
    )(a, b, proj)


def rg_scan_bwd(proj, dy, hs, a, name):
    S, D = a.shape
    cb = min(SCAN_COLS, D)
    goff = D // cb
    nchunks = S // CHUNK

    def body(g_ref, dy_ref, hs_ref, a_ref, dg_ref, gt_ref):
        gate, dy = g_ref[...], dy_ref[...]
        dg_ref[...] = (dy * hs_ref[...] * _gelu_grad(gate)).astype(dg_ref.dtype)
        gt_ref[...] = dy * _gelu(gate)
        row = lax.broadcasted_iota(jnp.int32, (CHUNK, cb), 0)

        def step(k, carry):
            c = nchunks - 1 - k
            r0 = pl.multiple_of(c * CHUNK, CHUNK)
            rn = pl.multiple_of(jnp.minimum(c + 1, nchunks - 1) * CHUNK, CHUNK)
            last = c == nchunks - 1
            nxt = jnp.where(last, 0.0, pltpu.roll(a_ref[pl.ds(rn, CHUNK), :], CHUNK - 1, 0))
            A = jnp.where(row < CHUNK - 1, pltpu.roll(a_ref[pl.ds(r0, CHUNK), :], CHUNK - 1, 0), nxt)
            B = gt_ref[pl.ds(r0, CHUNK), :]
            for d in (1, 2, 4):
                As = jnp.where(row < CHUNK - d, pltpu.roll(A, CHUNK - d, 0), 1.0)
                Bs = jnp.where(row < CHUNK - d, pltpu.roll(B, CHUNK - d, 0), 0.0)
                B = A * Bs + B
                A = A * As
            gt_ref[pl.ds(r0, CHUNK), :] = B + A * carry
            a_end = jnp.sum(jnp.where(row == 0, A, 0.0), axis=0, keepdims=True)
            b_end = jnp.sum(jnp.where(row == 0, B, 0.0), axis=0, keepdims=True)
            return b_end + a_end * carry

        lax.fori_loop(0, nchunks, step, jnp.zeros((1, cb), F32), unroll=SCAN_UNROLL)

    col = pl.BlockSpec((S, cb), lambda j: (0, j))
    return hbm_call(
        body, name=name, grid=(D // cb,), in_specs=[pl.BlockSpec((S, cb), lambda j: (0, goff + j)), col, col, col],
        out_specs=[col, col], out_shape=[jax.ShapeDtypeStruct((S, D), MXU_DTYPE), jax.ShapeDtypeStruct((S, D), F32)],
        compiler_params=_params(("parallel",), _vmem_limit(6 * S * cb * 4, 6 * S * cb * 4)),
    )(proj, dy, hs, a)


def rg_gates_bwd(gt, hs, xc, r, ig, w_rg, w_ig, lam, name):
    S, D = xc.shape
    nblk, bw, _ = w_rg.shape
    tr = min(RG_ROWS, S)

    def body(gt_ref, hs_ref, hp_ref, xc_ref, r_ref, i_ref, wr_ref, wi_ref, lam_ref,
             dxc_ref, dwr_ref, dwi_ref, dbr_ref, dbi_ref, dl_ref):
        step = pl.program_id(1)
        g, hs, xc, r, ig, lam = gt_ref[...], hs_ref[...], xc_ref[...], r_ref[...], i_ref[...], lam_ref[...]
        row = lax.broadcasted_iota(jnp.int32, g.shape, 0)
        hprev = _shift_down(hs, hp_ref[...], 1, row, step == 0)
        sp, _, a, em = _lru_coeffs(r, lam)
        mult = jnp.sqrt(em)
        du = g * mult
        dla = g * hprev * a - (g * (ig * xc)) * (a * a) / mult
        dpr = (dla * (-LRU_C * sp)) * (r * (1.0 - r))
        dpi = (du * xc) * (ig * (1.0 - ig))
        dprm, dpim = dpr.astype(MXU_DTYPE), dpi.astype(MXU_DTYPE)
        nt = (((1,), (1,)), ((), ()))
        dxc_ref[...] = (du * ig + lax.dot_general(dprm, wr_ref[...].astype(MXU_DTYPE), nt, preferred_element_type=F32)
                        + lax.dot_general(dpim, wi_ref[...].astype(MXU_DTYPE), nt, preferred_element_type=F32))

        @pl.when(step == 0)
        def _():
            for ref in (dwr_ref, dwi_ref, dbr_ref, dbi_ref, dl_ref):
                ref[...] = jnp.zeros_like(ref)

        xct = xc.T.astype(MXU_DTYPE)
        dwr_ref[...] += jnp.dot(xct, dprm, preferred_element_type=F32)
        dwi_ref[...] += jnp.dot(xct, dpim, preferred_element_type=F32)
        dbr_ref[...] += jnp.sum(dpr, axis=0, keepdims=True)
        dbi_ref[...] += jnp.sum(dpi, axis=0, keepdims=True)
        dl_ref[...] += jnp.sum(dla * (-LRU_C * r), axis=0, keepdims=True) * (-_sigmoid(-lam))

    tile = pl.BlockSpec((tr, bw), lambda n, i: (i, n))
    vec = pl.BlockSpec((1, bw), lambda n, i: (0, n))
    wblk = pl.BlockSpec((None, bw, bw), lambda n, i: (n, 0, 0))
    return hbm_call(
        body, name=name, grid=(nblk, S // tr),
        in_specs=[tile, tile, pl.BlockSpec((tr, bw), lambda n, i: (jnp.maximum(i - 1, 0), n)), tile, tile, tile, wblk, wblk, vec],
        out_specs=[tile, wblk, wblk, vec, vec, vec],
        out_shape=[jax.ShapeDtypeStruct((S, D), F32), jax.ShapeDtypeStruct((nblk, bw, bw), F32), jax.ShapeDtypeStruct((nblk, bw, bw), F32),
                   jax.ShapeDtypeStruct((1, D), F32), jax.ShapeDtypeStruct((1, D), F32), jax.ShapeDtypeStruct((1, D), F32)],
        compiler_params=_params(("parallel", "arbitrary")),
    )(gt, hs, hs, xc, r, ig, w_rg, w_ig, lam)


def rg_conv_bwd(proj, dxc, conv_w, name):
    S, D = dxc.shape
    bw = min(SCAN_COLS, D)
    tr = min(RG_ROWS, S)
    nsteps = S // tr

    def body(d_ref, dn_ref, xr_ref, xp_ref, cw_ref, dxr_ref, dcw_ref, dcb_ref):
        step = pl.program_id(1)
        d, xr = d_ref[...], xr_ref[...]
        row = lax.broadcasted_iota(jnp.int32, d.shape, 0)
        dxr = d * cw_ref[CONV_WIDTH - 1:CONV_WIDTH, :]
        for k in range(CONV_WIDTH - 1):
            dxr = dxr + _shift_up(d, dn_ref[...], CONV_WIDTH - 1 - k, row, step == nsteps - 1, tr) * cw_ref[k:k + 1, :]
        dxr_ref[...] = dxr.astype(dxr_ref.dtype)

        @pl.when(step == 0)
        def _():
            dcw_ref[...] = jnp.zeros_like(dcw_ref)
            dcb_ref[...] = jnp.zeros_like(dcb_ref)

        for k in range(CONV_WIDTH - 1):
            xs = _shift_down(xr, xp_ref[...], CONV_WIDTH - 1 - k, row, step == 0)
            dcw_ref[k:k + 1, :] += jnp.sum(d * xs, axis=0, keepdims=True)
        dcw_ref[CONV_WIDTH - 1:CONV_WIDTH, :] += jnp.sum(d * xr, axis=0, keepdims=True)
        dcb_ref[...] += jnp.sum(d, axis=0, keepdims=True)

    tile = pl.BlockSpec((tr, bw), lambda n, i: (i, n))
    cwb = pl.BlockSpec((CONV_WIDTH, bw), lambda n, i: (0, n))
    return hbm_call(
        body, name=name, grid=(D // bw, nsteps),
        in_specs=[tile, pl.BlockSpec((tr, bw), lambda n, i: (jnp.minimum(i + 1, nsteps - 1), n)), tile,
                  pl.BlockSpec((tr, bw), lambda n, i: (jnp.maximum(i - 1, 0), n)), cwb],
        out_specs=[tile, cwb, pl.BlockSpec((1, bw), lambda n, i: (0, n))],
        out_shape=[jax.ShapeDtypeStruct((S, D), MXU_DTYPE), jax.ShapeDtypeStruct((CONV_WIDTH, D), F32), jax.ShapeDtypeStruct((1, D), F32)],
        compiler_params=_params(("parallel", "arbitrary")),
    )(dxc, dxc, proj, proj, conv_w)


def rope_table(S):
    half = ROT_DIM // 2
    pos = jnp.arange(S, dtype=F32)
    inv = ROPE_THETA ** (-jnp.arange(0, ROT_DIM, 2, dtype=F32) / ROT_DIM)
    ang = pos[:, None] * inv[None, :]
    cos, sin = jnp.cos(ang), jnp.sin(ang)
    zero = jnp.zeros((S, HEAD_DIM - ROT_DIM), F32)
    c = jnp.concatenate([cos, cos, zero + 1.0], axis=1)
    a = jnp.concatenate([-sin, jnp.zeros((S, half), F32), zero], axis=1)
    b = jnp.concatenate([jnp.zeros((S, half), F32), sin, zero], axis=1)
    return jnp.stack([jnp.tile(t, (1, LANES // HEAD_DIM)) for t in (c, a, b)])


def _rope(t, tab):
    half = ROT_DIM // 2
    return t * tab[0] + pltpu.roll(t, LANES - half, 1) * tab[1] + pltpu.roll(t, half, 1) * tab[2]


def _rope_t(d, tab):
    half = ROT_DIM // 2
    return d * tab[0] + pltpu.roll(d * tab[1], half, 1) + pltpu.roll(d * tab[2], LANES - half, 1)


def _dup_head(t, hk, lo):
    sw = pltpu.roll(t, HEAD_DIM, 1)
    return jnp.where(lo, t, sw) if hk == 0 else jnp.where(lo, sw, t)


def _attn_common(n, sink_ref, q_ref, kp_ref, kc_ref, vp_ref, vc_ref, tc_ref, tp_ref, hk, pairs):
    tq = (tc_ref[0], tc_ref[1], tc_ref[2])
    tp = (tp_ref[0], tp_ref[1], tp_ref[2])
    lo = lax.broadcasted_iota(jnp.int32, (WINDOW, LANES), 1) < HEAD_DIM
    lo2 = lax.broadcasted_iota(jnp.int32, (2 * WINDOW, LANES), 1) < HEAD_DIM
    kband = jnp.concatenate([_rope(kp_ref[...], tp), _rope(kc_ref[...], tq)], axis=0)
    vband = jnp.concatenate([vp_ref[...], vc_ref[...]], axis=0)
    kd = _dup_head(kband, hk, lo2).astype(MXU_DTYPE)
    vd = _dup_head(vband, hk, lo2).astype(MXU_DTYPE)
    rows, sks = [], []
    for j in range(pairs):
        col = hk * pairs + j
        qp = _rope(q_ref[:, col * LANES:(col + 1) * LANES], tq)
        rows += [jnp.where(lo, qp, 0.0), jnp.where(lo, 0.0, qp)]
        sks += [jnp.full((WINDOW, 1), sink_ref[2 * col], F32), jnp.full((WINDOW, 1), sink_ref[2 * col + 1], F32)]
    qg = jnp.concatenate(rows, axis=0)
    sk = jnp.concatenate(sks, axis=0)
    G = 2 * pairs * WINDOW
    own = lax.broadcasted_iota(jnp.int32, (G, WINDOW), 1) <= (lax.broadcasted_iota(jnp.int32, (G, WINDOW), 0) & (WINDOW - 1))
    s = lax.dot_general(qg.astype(MXU_DTYPE), kd, (((1,), (1,)), ((), ())), preferred_element_type=F32) * (HEAD_DIM ** -0.5)
    s = jnp.where(own, s[:, WINDOW:], s[:, :WINDOW] + jnp.where(n > 0, 0.0, NEG_INF))
    m = jnp.maximum(jnp.max(s, axis=1, keepdims=True), sk)
    e = jnp.exp(s - m)
    es = jnp.exp(sk - m)
    inv = 1.0 / (jnp.sum(e, axis=1, keepdims=True) + es)
    return qg, kd, vd, e * inv, es * inv, own, lo, lo2, tq, tp


def _unfold_band(t, own):
    return jnp.concatenate([jnp.where(own, 0.0, t), jnp.where(own, t, 0.0)], axis=1)


def _attn_specs(D, NB):
    kcol = 3 * D // LANES
    q = pl.BlockSpec((WINDOW, D), lambda n: (n, 2))
    kc = pl.BlockSpec((WINDOW, LANES), lambda n: (n, kcol))
    kp = pl.BlockSpec((WINDOW, LANES), lambda n: (jnp.maximum(n - 1, 0), kcol))
    vc = pl.BlockSpec((WINDOW, LANES), lambda n: (n, kcol + 1))
    vp = pl.BlockSpec((WINDOW, LANES), lambda n: (jnp.maximum(n - 1, 0), kcol + 1))
    tc = pl.BlockSpec((3, WINDOW, LANES), lambda n: (0, n, 0))
    tp = pl.BlockSpec((3, WINDOW, LANES), lambda n: (0, jnp.maximum(n - 1, 0), 0))
    sink = pl.BlockSpec(memory_space=pltpu.SMEM)
    return [sink, q, kp, kc, vp, vc, tc, tp]


def attn_fwd(proj, sinks, tab, D, name):
    S = proj.shape[0]
    NB = S // WINDOW
    pairs = D // HEAD_DIM // N_KV_HEADS // 2

    def body(sink_ref, q_ref, kp_ref, kc_ref, vp_ref, vc_ref, tc_ref, tp_ref, o_ref):
        n = pl.program_id(0)
        for hk in range(N_KV_HEADS):
            _, _, vd, p, _, own, lo, _, _, _ = _attn_common(n, sink_ref, q_ref, kp_ref, kc_ref, vp_ref, vc_ref, tc_ref, tp_ref, hk, pairs)
            o = jnp.dot(_unfold_band(p, own).astype(MXU_DTYPE), vd, preferred_element_type=F32)
            for j in range(pairs):
                col = hk * pairs + j
                oa = o[(2 * j) * WINDOW:(2 * j + 1) * WINDOW]
                ob = o[(2 * j + 1) * WINDOW:(2 * j + 2) * WINDOW]
                o_ref[:, col * LANES:(col + 1) * LANES] = jnp.where(lo, oa, ob)

    return hbm_call(
        body, name=name, grid=(NB,), in_specs=_attn_specs(D, NB),
        out_specs=pl.BlockSpec((WINDOW, D), lambda n: (n, 0)), out_shape=jax.ShapeDtypeStruct((S, D), F32),
        compiler_params=_params(("parallel",)),
    )(sinks, proj, proj, proj, proj, proj, tab, tab)


def attn_bwd(proj, sinks, tab, o, do, D, name):
    S = proj.shape[0]
    NB = S // WINDOW
    pairs = D // HEAD_DIM // N_KV_HEADS // 2

    def body(sink_ref, q_ref, kp_ref, kc_ref, vp_ref, vc_ref, tc_ref, tp_ref, o_ref, do_ref, dq_ref, dk_ref, dv_ref, ds_ref):
        n = pl.program_id(0)

        @pl.when(n == 0)
        def _():
            ds_ref[...] = jnp.zeros_like(ds_ref)

        lane1 = lax.broadcasted_iota(jnp.int32, (1, LANES), 1)
        dsink = jnp.zeros((1, LANES), F32)
        dkt = dvt = None
        for hk in range(N_KV_HEADS):
            qg, kd, vd, p, ps, own, lo, lo2, tq, tp = _attn_common(n, sink_ref, q_ref, kp_ref, kc_ref, vp_ref, vc_ref, tc_ref, tp_ref, hk, pairs)
            dos, os_ = [], []
            for j in range(pairs):
                col = hk * pairs + j
                dop = do_ref[:, col * LANES:(col + 1) * LANES]
                op = o_ref[:, col * LANES:(col + 1) * LANES]
                dos += [jnp.where(lo, dop, 0.0), jnp.where(lo, 0.0, dop)]
                os_ += [jnp.where(lo, op, 0.0), jnp.where(lo, 0.0, op)]
            dog = jnp.concatenate(dos, axis=0)
            og = jnp.concatenate(os_, axis=0)
            dogm = dog.astype(MXU_DTYPE)
            dp = lax.dot_general(dogm, vd, (((1,), (1,)), ((), ())), preferred_element_type=F32)
            dp = jnp.where(own, dp[:, WINDOW:], dp[:, :WINDOW])
            dr = jnp.sum(dog * og, axis=1, keepdims=True)
            ds = _unfold_band(p * (dp - dr) * (HEAD_DIM ** -0.5), own)
            dsm = ds.astype(MXU_DTYPE)
            dqg = jnp.dot(dsm, kd, preferred_element_type=F32)
            dkd = jnp.dot(ds.T.astype(MXU_DTYPE), qg.astype(MXU_DTYPE), preferred_element_type=F32)
            dvd = jnp.dot(_unfold_band(p, own).T.astype(MXU_DTYPE), dogm, preferred_element_type=F32)
            dkf = dkd + pltpu.roll(dkd, HEAD_DIM, 1)
            dvf = dvd + pltpu.roll(dvd, HEAD_DIM, 1)
            if hk == 0:
                dkt, dvt = dkf, dvf
            else:
                dkt, dvt = jnp.where(lo2, dkt, dkf), jnp.where(lo2, dvt, dvf)
            sd = ps * dr
            for j in range(pairs):
                col = hk * pairs + j
                dqa = dqg[(2 * j) * WINDOW:(2 * j + 1) * WINDOW]
                dqb = dqg[(2 * j + 1) * WINDOW:(2 * j + 2) * WINDOW]
                dq_ref[:, col * LANES:(col + 1) * LANES] = _rope_t(jnp.where(lo, dqa, dqb), tq).astype(dq_ref.dtype)
                for t in range(2):
                    part = sd[(2 * j + t) * WINDOW:(2 * j + t + 1) * WINDOW]
                    val = jnp.sum(part, axis=0, keepdims=True)
                    dsink = dsink - jnp.where(lane1 == 2 * col + t, val, 0.0)
        dk_ref[...] = jnp.concatenate([_rope_t(dkt[:WINDOW], tp), _rope_t(dkt[WINDOW:], tq)], axis=0)
        dv_ref[...] = dvt
        ds_ref[...] += dsink

    blk = pl.BlockSpec((WINDOW, D), lambda n: (n, 0))
    band = pl.BlockSpec((None, 2 * WINDOW, LANES), lambda n: (n, 0, 0))
    return hbm_call(
        body, name=name, grid=(NB,), in_specs=_attn_specs(D, NB) + [blk, blk],
        out_specs=[blk, band, band, pl.BlockSpec((1, LANES), lambda n: (0, 0))],
        out_shape=[jax.ShapeDtypeStruct((S, D), MXU_DTYPE), jax.ShapeDtypeStruct((NB, 2 * WINDOW, LANES), F32),
                   jax.ShapeDtypeStruct((NB, 2 * WINDOW, LANES), F32), jax.ShapeDtypeStruct((1, LANES), F32)],
        compiler_params=_params(("arbitrary",)),
    )(sinks, proj, proj, proj, proj, proj, tab, tab, o, do)


def band_fold(dkb, dvb, name):
    NB = dkb.shape[0]
    k4 = dkb.reshape(NB, 2, WINDOW, LANES)
    v4 = dvb.reshape(NB, 2, WINDOW, LANES)

    def body(kc_ref, kn_ref, vc_ref, vn_ref, dk_ref, dv_ref):
        more = pl.program_id(0) < NB - 1
        dk_ref[...] = (kc_ref[...] + jnp.where(more, kn_ref[...], 0.0)).astype(dk_ref.dtype)
        dv_ref[...] = (vc_ref[...] + jnp.where(more, vn_ref[...], 0.0)).astype(dv_ref.dtype)

    cur = pl.BlockSpec((None, None, WINDOW, LANES), lambda n: (n, 1, 0, 0))
    nxt = pl.BlockSpec((None, None, WINDOW, LANES), lambda n: (jnp.minimum(n + 1, NB - 1), 0, 0, 0))
    out = pl.BlockSpec((WINDOW, LANES), lambda n: (n, 0))
    sds = jax.ShapeDtypeStruct((NB * WINDOW, LANES), MXU_DTYPE)
    return hbm_call(body, name=name, grid=(NB,), in_specs=[cur, nxt, cur, nxt], out_specs=[out, out], out_shape=[sds, sds],
                          compiler_params=_params(("parallel",)))(k4, k4, v4, v4)


CROSS_ROWS = 512


def _cross_probs(q, k, scale):
    s = lax.dot_general(q.astype(MXU_DTYPE), k.astype(MXU_DTYPE), (((1,), (1,)), ((), ())), preferred_element_type=F32) * scale
    e = jnp.exp(s - jnp.max(s, axis=1, keepdims=True))
    return e / jnp.sum(e, axis=1, keepdims=True)


def cross_fwd(qc, kv, name):
    S, D = qc.shape
    M = kv.shape[0]
    hd = D // CROSS_HEADS
    tq = min(CROSS_ROWS, S)

    def body(q_ref, kv_ref, o_ref):
        for h in range(CROSS_HEADS):
            p = _cross_probs(q_ref[:, h * hd:(h + 1) * hd], kv_ref[:, h * hd:(h + 1) * hd], hd ** -0.5)
            v = kv_ref[:, D + h * hd:D + (h + 1) * hd].astype(MXU_DTYPE)
            o_ref[:, h * hd:(h + 1) * hd] = jnp.dot(p.astype(MXU_DTYPE), v, preferred_element_type=F32).astype(o_ref.dtype)

    return hbm_call(
        body, name=name, grid=(S // tq,), in_specs=[pl.BlockSpec((tq, D), lambda i: (i, 0)), pl.BlockSpec((M, 2 * D), lambda i: (0, 0))],
        out_specs=pl.BlockSpec((tq, D), lambda i: (i, 0)), out_shape=jax.ShapeDtypeStruct((S, D), MXU_DTYPE),
        compiler_params=_params(("parallel",)),
    )(qc, kv)


def cross_bwd(qc, kv, do, name):
    S, D = qc.shape
    M = kv.shape[0]
    hd = D // CROSS_HEADS
    tq = min(CROSS_ROWS, S)

    def body(q_ref, kv_ref, do_ref, dq_ref, dkv_ref):
        @pl.when(pl.program_id(0) == 0)
        def _():
            dkv_ref[...] = jnp.zeros_like(dkv_ref)

        for h in range(CROSS_HEADS):
            q = q_ref[:, h * hd:(h + 1) * hd]
            k = kv_ref[:, h * hd:(h + 1) * hd]
            v = kv_ref[:, D + h * hd:D + (h + 1) * hd].astype(MXU_DTYPE)
            dom = do_ref[:, h * hd:(h + 1) * hd].astype(MXU_DTYPE)
            p = _cross_probs(q, k, hd ** -0.5)
            dp = lax.dot_general(dom, v, (((1,), (1,)), ((), ())), preferred_element_type=F32)
            ds = p * (dp - jnp.sum(p * dp, axis=1, keepdims=True)) * (hd ** -0.5)
            dq_ref[:, h * hd:(h + 1) * hd] = jnp.dot(ds.astype(MXU_DTYPE), k.astype(MXU_DTYPE),
                                                     preferred_element_type=F32).astype(dq_ref.dtype)
            dkv_ref[:, h * hd:(h + 1) * hd] += jnp.dot(ds.T.astype(MXU_DTYPE), q.astype(MXU_DTYPE), preferred_element_type=F32)
            dkv_ref[:, D + h * hd:D + (h + 1) * hd] += jnp.dot(p.T.astype(MXU_DTYPE), dom, preferred_element_type=F32)

    row = pl.BlockSpec((tq, D), lambda i: (i, 0))
    full = pl.BlockSpec((M, 2 * D), lambda i: (0, 0))
    return hbm_call(
        body, name=name, grid=(S // tq,), in_specs=[row, full, row], out_specs=[row, full],
        out_shape=[jax.ShapeDtypeStruct((S, D), MXU_DTYPE), jax.ShapeDtypeStruct((M, 2 * D), F32)],
        compiler_params=_params(("arbitrary",)),
    )(qc, kv, do)


def adamw(w, g, m, v, name, layers=None, into=None):
    shape = w.shape
    cols = shape[-1]
    lead = shape[0] if len(shape) > 2 else 1
    rows = int(np.prod(shape[:-1])) // lead
    w2, g2, m2, v2 = (t.reshape(lead, rows, cols) for t in (w, g, m, v))
    tr = _divisors(rows, SUBLANES, max(SUBLANES, (1 << 20) // (cols * 4) // SUBLANES * SUBLANES))[0]
    lo, hi = layers or (0, lead)
    done = [t.reshape(lead, rows, cols) for t in into] if into else []

    def body(w_ref, g_ref, m_ref, v_ref, *refs):
        d_ref, mo_ref, vo_ref, go_ref = refs[len(done):]
        gg = g_ref[...]
        mn = ADAM_B1 * m_ref[...] + (1.0 - ADAM_B1) * gg
        vn = ADAM_B2 * v_ref[...] + (1.0 - ADAM_B2) * (gg * gg)
        m_hat = mn / (1.0 - ADAM_B1 ** ADAM_STEP)
        v_hat = vn / (1.0 - ADAM_B2 ** ADAM_STEP)
        d_ref[...] = -ADAM_LR * (m_hat / (jnp.sqrt(v_hat) + ADAM_EPS) + ADAM_WD * w_ref[...])
        mo_ref[...] = mn
        vo_ref[...] = vn
        go_ref[...] = gg

    blk = pl.BlockSpec((None, tr, cols), lambda l, i: (l + lo, i, 0))
    sds = jax.ShapeDtypeStruct((lead, rows, cols), F32)
    d, mn, vn, go = hbm_call(body, name=name, grid=(hi - lo, rows // tr), in_specs=[blk] * 4 + [pl.BlockSpec(memory_space=pl.ANY)] * len(done),
                             out_specs=[blk] * 4, out_shape=[sds] * 4, input_output_aliases={4 + k: k for k in range(len(done))},
                             compiler_params=_params(("parallel", "parallel")))(w2, g2, m2, v2, *done)
    return d.reshape(shape), mn.reshape(shape), vn.reshape(shape), go.reshape(shape)


def sum_devices(parts, name):
    n, rows, cols = parts.shape

    def body(p_ref, o_ref):
        acc = p_ref[0]
        for k in range(1, n):
            acc = acc + p_ref[k]
        o_ref[...] = acc

    return pl.pallas_call(body, name=name, in_specs=[pl.BlockSpec(memory_space=pltpu.VMEM)],
                          out_specs=pl.BlockSpec(memory_space=pltpu.VMEM), out_shape=jax.ShapeDtypeStruct((rows, cols), F32))(parts)


HBM_SPEC = pl.BlockSpec(memory_space=pltpu.HBM)


def _place():
    return lax.axis_index("x"), lax.axis_index("y"), lax.axis_index("c")


def _remote(src, dst, send_sems, recv_sems, k, to):
    return pltpu.make_async_remote_copy(src_ref=src, dst_ref=dst, send_sem=send_sems.at[k], recv_sem=recv_sems.at[k],
                                        device_id=to, device_id_type=MESH_ID)


SEM_SPEC = pl.BlockSpec(memory_space=pltpu.SEMAPHORE)
ANY_SPEC = pl.BlockSpec(memory_space=pl.ANY)
SPLIT_COPY = pltpu.CompilerParams(has_side_effects=pltpu.SideEffectType.DATAFLOW_SIDE_EFFECTING)


def _in_hbm(arrays):
    return [pltpu.with_memory_space_constraint(a, pltpu.HBM) for a in arrays]


def _split_start(copies, sources, lands, after, n_sems, name):
    n = len(sources)

    def body(*refs):
        for cp in copies(refs[:n], refs[n:2 * n], refs[2 * n + 1], refs[2 * n + 2]):
            cp.start()
        refs[-1][...] = jnp.zeros_like(refs[-1])

    through = [pltpu.HBM(a.shape, a.dtype) for a in list(sources) + list(lands)]
    outs = pl.pallas_call(
        body, name=name, in_specs=[HBM_SPEC] * (2 * n) + [ANY_SPEC],
        out_specs=[SEM_SPEC, SEM_SPEC] + [HBM_SPEC] * (2 * n) + [pl.BlockSpec(memory_space=pltpu.VMEM)],
        out_shape=[pltpu.SemaphoreType.DMA((n_sems,)), pltpu.SemaphoreType.DMA((n_sems,))] + through
        + [jax.ShapeDtypeStruct((SUBLANES, LANES), F32)],
        input_output_aliases={i: 2 + i for i in range(2 * n)}, compiler_params=SPLIT_COPY,
    )(*_in_hbm(sources), *_in_hbm(lands), after)
    return outs[0], outs[1], outs[2:2 + n], outs[2 + n:2 + 2 * n], outs[-1]


def _split_wait(copies, send_sems, recv_sems, sources, lands, after, name):
    n = len(sources)

    def body(*refs):
        for cp in copies(refs[:n], refs[n:2 * n], refs[2 * n], refs[2 * n + 1]):
            cp.wait_send()
            cp.wait_recv()

    through = [pltpu.HBM(a.shape, a.dtype) for a in list(sources) + list(lands)]
    outs = pl.pallas_call(
        body, name=name, in_specs=[HBM_SPEC] * (2 * n) + [SEM_SPEC, SEM_SPEC, ANY_SPEC], out_specs=[HBM_SPEC] * (2 * n),
        out_shape=through, input_output_aliases={i: i for i in range(2 * n)}, compiler_params=SPLIT_COPY,
    )(*sources, *lands, send_sems, recv_sems, after)
    return outs[:n], outs[n:]


def _chip_slab(land, slot, rows):
    return land.at[slot, rows] if len(land.shape) == 3 else land.at[rows, slot]


def _gather_copies(w_refs, land_refs, send_sems, recv_sems):
    n = len(w_refs)
    x, y, c = _place()
    chips = [(1 - x, y), (x, 1 - y), (1 - x, 1 - y)]
    cps = []
    for a in range(n):
        hr = w_refs[a].shape[0] // 2
        mine, every = pl.ds(c * hr, hr), pl.ds(0, 2 * hr)
        cps.append(_remote(w_refs[a], _chip_slab(land_refs[a], 2 * x + y, every), send_sems, recv_sems, 3 * n + a, (x, y, 1 - c)))
        for k, chip in enumerate(chips):
            cps.append(_remote(w_refs[a].at[mine], _chip_slab(land_refs[a], 2 * x + y, mine), send_sems, recv_sems, 3 * a + k, (*chip, c)))
    return cps


def gather_start(shards, after, name):
    lands = [lax.empty(s.shape[:-2] + (N_CHIPS,) + s.shape[-2:], s.dtype) for s in shards]
    return _split_start(_gather_copies, shards, lands, after, 4 * len(shards), name)


def gather_wait(state, after, name):
    send_sems, recv_sems, sources, lands, _ = state
    return _split_wait(_gather_copies, send_sems, recv_sems, sources, lands, after, name)[1]


def gather_pass(lands, name):
    n = len(lands)

    def body(*refs):
        out_refs, send_sems, recv_sems = refs[n:2 * n], refs[2 * n], refs[2 * n + 1]
        x, y, c = _place()
        chips = [(1 - x, y), (x, 1 - y), (1 - x, 1 - y)]
        sent = []
        for a in range(n):
            hr = out_refs[a].shape[0 if len(out_refs[a].shape) == 4 else 1] // 2
            for k, (px, py) in enumerate(chips):
                landed = _chip_slab(out_refs[a], 2 * px + py, pl.ds(c * hr, hr))
                sent.append(_remote(landed, landed, send_sems, recv_sems, 3 * a + k, (x, y, 1 - c)))
        for cp in sent:
            cp.start()
        for a in range(n):
            hr = out_refs[a].shape[0 if len(out_refs[a].shape) == 4 else 1] // 2
            for k, (px, py) in enumerate(chips):
                theirs = _chip_slab(out_refs[a], 2 * px + py, pl.ds((1 - c) * hr, hr))
                _remote(theirs, theirs, send_sems, recv_sems, 3 * a + k, (x, y, 1 - c)).wait_recv()
        for cp in sent:
            cp.wait_send()

    return hbm_call(
        body, name=name, in_specs=[HBM_SPEC] * n, out_specs=[HBM_SPEC] * n,
        out_shape=[jax.ShapeDtypeStruct(a.shape, a.dtype) for a in lands], input_output_aliases={a: a for a in range(n)},
        scratch_shapes=[pltpu.SemaphoreType.DMA((3 * n,))] * 2,
    )(*lands)


def _scatter_copies(t_refs, land_refs, send_sems, recv_sems):
    x, y, c = _place()
    chips = [(1 - x, y), (x, 1 - y), (1 - x, 1 - y)]
    return [_remote(t_refs[a].at[:, 2 * px + py], land_refs[a].at[:, k], send_sems, recv_sems, 3 * a + k, (px, py, c))
            for a in range(len(t_refs)) for k, (px, py) in enumerate(chips)]


def scatter_start(parts, after, name):
    lands = [lax.empty((t.shape[0], N_CHIPS - 1) + t.shape[2:], t.dtype) for t in parts]
    return _split_start(_scatter_copies, parts, lands, after, 3 * len(parts), name)


def scatter_wait(state, after, name):
    send_sems, recv_sems, sources, lands, _ = state
    return _split_wait(_scatter_copies, send_sems, recv_sems, sources, lands, after, name)


def swap_sibling(parts, name):
    n = len(parts)

    def body(*refs):
        v_refs, out_refs, send_sems, recv_sems = refs[:n], refs[n:2 * n], refs[2 * n], refs[2 * n + 1]
        x, y, c = _place()
        cps = []
        for a in range(n):
            hr = v_refs[a].shape[2] // 2
            cps.append(_remote(v_refs[a].at[:, :, pl.ds((1 - c) * hr, hr)], out_refs[a], send_sems, recv_sems, a, (x, y, 1 - c)))
        for cp in cps:
            cp.start()
        for cp in cps:
            cp.wait()

    return hbm_call(
        body, name=name, in_specs=[HBM_SPEC] * n, out_specs=[HBM_SPEC] * n,
        out_shape=[jax.ShapeDtypeStruct(v.shape[:2] + (v.shape[2] // 2, v.shape[3]), v.dtype) for v in parts],
        scratch_shapes=[pltpu.SemaphoreType.DMA((n,))] * 2,
    )(*parts)


def join_halves(halves, layer, name):
    n = len(halves)

    def body(*refs):
        out_refs, send_sems, recv_sems = refs[n:2 * n], refs[2 * n], refs[2 * n + 1]
        x, y, c = _place()
        cps = []
        for a in range(n):
            hr = out_refs[a].shape[1] // 2
            mine = out_refs[a].at[layer, pl.ds(c * hr, hr)]
            cps.append(_remote(mine, mine, send_sems, recv_sems, a, (x, y, 1 - c)))
        for cp in cps:
            cp.start()
        for a in range(n):
            hr = out_refs[a].shape[1] // 2
            theirs = out_refs[a].at[layer, pl.ds((1 - c) * hr, hr)]
            _remote(theirs, theirs, send_sems, recv_sems, a, (x, y, 1 - c)).wait_recv()
        for cp in cps:
            cp.wait_send()

    return hbm_call(
        body, name=name, in_specs=[HBM_SPEC] * n, out_specs=[HBM_SPEC] * n,
        out_shape=[jax.ShapeDtypeStruct(f.shape, f.dtype) for f in halves], input_output_aliases={a: a for a in range(n)},
        scratch_shapes=[pltpu.SemaphoreType.DMA((n,))] * 2,
    )(*halves)


def gather_devices(v, name):
    def body(v_ref, out_ref, send_sems, recv_sems, local_sem):
        x, y, c = _place()
        me = 4 * x + 2 * y + c
        own = pltpu.make_async_copy(v_ref, out_ref.at[me], local_sem)
        own.start()
        peers = [((x + dx) % 2, (y + dy) % 2, (c + dc) % 2) for dx in (0, 1) for dy in (0, 1) for dc in (0, 1)][1:]
        sent = []
        for k, peer in enumerate(peers):
            cp = pltpu.make_async_remote_copy(src_ref=v_ref, dst_ref=out_ref.at[me], send_sem=send_sems.at[k], recv_sem=recv_sems.at[k],
                                              device_id=peer, device_id_type=MESH_ID)
            cp.start()
            sent.append(cp)
        for k, (px, py, pc) in enumerate(peers):
            slot = out_ref.at[4 * px + 2 * py + pc]
            pltpu.make_async_remote_copy(src_ref=slot, dst_ref=slot, send_sem=send_sems.at[k], recv_sem=recv_sems.at[k],
                                         device_id=(px, py, pc), device_id_type=MESH_ID).wait_recv()
        for cp in sent:
            cp.wait_send()
        own.wait()

    vm = pl.BlockSpec(memory_space=pltpu.VMEM)
    return pl.pallas_call(body, name=name, in_specs=[vm], out_specs=vm, out_shape=jax.ShapeDtypeStruct((N_DEV,) + v.shape, v.dtype),
                          scratch_shapes=[pltpu.SemaphoreType.DMA((N_DEV - 1,)), pltpu.SemaphoreType.DMA((N_DEV - 1,)),
                                          pltpu.SemaphoreType.DMA])(v)


ADD_ROWS = 512


def add_pair(place, a, b, name):
    L, n, hr, cols = b.shape
    tr = _divisors(hr, 2 * SUBLANES, ADD_ROWS)[0]
    nb = hr // tr

    def body(p_ref, a_ref, b_ref, o_ref):
        del p_ref
        o_ref[...] = (a_ref[...].astype(F32) + b_ref[...].astype(F32)).astype(o_ref.dtype)

    blk = pl.BlockSpec((None, None, tr, cols), lambda l, d, i, p: (l, d, i, 0))
    grid_spec = pltpu.PrefetchScalarGridSpec(
        num_scalar_prefetch=1, grid=(L, n, nb),
        in_specs=[pl.BlockSpec((None, None, tr, cols), lambda l, d, i, p: (l, d, p[0] * nb + i, 0)), blk], out_specs=blk)
    return hbm_call(body, name=name, grid_spec=grid_spec, out_shape=jax.ShapeDtypeStruct(b.shape, b.dtype),
                          compiler_params=_params(("parallel", "parallel", "parallel")))(place, a, b)


def add_chips(place, own, others, layer, stacked, name):
    _, n, hr, cols = others.shape
    tr = _divisors(hr, 2 * SUBLANES, ADD_ROWS)[0]
    nb = hr // tr
    create = isinstance(stacked, tuple)

    def body(p_ref, own_ref, *refs):
        del p_ref
        acc = own_ref[...].astype(F32)
        for k in range(n):
            acc = acc + refs[k][...].astype(F32)
        refs[-1][...] = acc

    ins = [pl.BlockSpec((None, None, tr, cols), lambda i, p: (0, p[1], i, 0))]
    ins += [pl.BlockSpec((None, None, tr, cols), functools.partial(lambda k, i, p: (0, k, i, 0), k)) for k in range(n)]
    grid_spec = pltpu.PrefetchScalarGridSpec(num_scalar_prefetch=1, grid=(nb,), in_specs=ins + ([] if create else [ANY_SPEC]),
                                             out_specs=pl.BlockSpec((None, tr, cols), lambda i, p: (layer, p[0] * nb + i, 0)))
    shape = stacked if create else stacked.shape
    return hbm_call(body, name=name, grid_spec=grid_spec, out_shape=jax.ShapeDtypeStruct(shape, F32),
                          input_output_aliases={} if create else {n + 2: 0},
                          compiler_params=_params(("parallel",)))(place, own, *([others] * n), *([] if create else [stacked]))


def _alpha(depth):
    return (2 * depth) ** 0.25


def _wmm(a, weight, mode, name, deps=(), **more):
    arr, how = weight
    return mm(a, arr, mode, name, deps=deps, **how, **more)


def layer_fwd(h, mem, w, tab, alpha, deps=(), late=None):
    D = h.shape[1]
    proj = _wmm(h, w["w_in"], "nn", "mm_proj", deps)
    xc, r, ig, a, b = rg_gates_fwd(proj, w["conv_w"], w["conv_b"], w["w_rg"], w["b_rg"], w["w_ig"], w["b_ig"], w["lru_lambda"], "rg_gates_fwd")
    hs, y_rnn = rg_scan_fwd(proj, a, b, "rg_scan_fwd")
    y_attn = attn_fwd(proj, w["sinks"], tab, D, "attn_fwd")
    deps = ()
    if late is not None:
        rest, deps = late(y_attn)
        w = {**w, **rest}
    pr = _wmm(y_rnn, w["w_br_rnn"], "nn", "mm_br_rnn", deps)
    pa = _wmm(y_attn, w["w_br_attn"], "nn", "mm_br_attn")
    merged = merge_fwd(proj, pr, pa, "merge_fwd")
    h1, xh1, rs1 = _wmm(merged, w["w_out"], "nn", "mm_out_ln1", post_norm=(h, w["ln1_g"], w["ln1_b"], alpha))
    qc = _wmm(h1, w["cq_w"], "nn", "mm_cq", out_dtype=MXU_DTYPE)
    kv = _wmm(mem, w["ckv_w"], "nn", "mm_ckv", out_dtype=MXU_DTYPE)
    o = cross_fwd(qc, kv, "cross_fwd")
    h2, xh2, rs2 = _wmm(o, w["co_w"], "nn", "mm_co_ln2", post_norm=(h1, w["ln2_g"], w["ln2_b"], alpha))
    gu = _wmm(h2, w["ffn_wi"], "nn", "mm_ffn_wi", out_blocks=2)
    act = swiglu_fwd(gu, "swiglu_fwd")
    h3, xh3, rs3 = _wmm(act, w["ffn_wo"], "nn", "mm_ffn_wo_ln3", post_norm=(h2, w["ln3_g"], w["ln3_b"], alpha))
    saved = dict(h=h, proj=proj, xc=xc, r=r, ig=ig, a=a, hs=hs, y_rnn=y_rnn, y_attn=y_attn, pr=pr, pa=pa, xh1=xh1, rs1=rs1, h1=h1,
                 qc=qc, kv=kv, o=o, xh2=xh2, rs2=rs2, h2=h2, gu=gu, xh3=xh3, rs3=rs3)
    return h3, saved, w


def layer_bwd(dh, mem, w, s, tab, alpha, deps=(), halfway=None):
    D = dh.shape[1]
    g = {}
    wg = dict(out_dtype=MXU_DTYPE)
    dz3, g["ln3_g"], g["ln3_b"] = ln_bwd(dh, None, s["xh3"], s["rs3"], w["ln3_g"], 1.0, "ln3_bwd")
    act = swiglu_fwd(s["gu"], "swiglu_refwd")
    g["ffn_wo"] = mm(act, dz3, "tn", "mm_d_ffn_wo", deps=deps, **wg)
    dact = _wmm(dz3, w["ffn_wo"], "nt", "mm_dact")
    dgu = swiglu_bwd(s["gu"], dact, "swiglu_bwd")
    g["ffn_wi"] = mm(s["h2"], dgu, "tn", "mm_d_ffn_wi", b_blocks=2, out_blocks=N_CHIPS, **wg)
    dh2 = _wmm(dgu, w["ffn_wi"], "nt", "mm_dh2", a_blocks=2)
    dz2, g["ln2_g"], g["ln2_b"] = ln_bwd(dz3, dh2, s["xh2"], s["rs2"], w["ln2_g"], alpha, "ln2_bwd")
    g["co_w"] = mm(s["o"], dz2, "tn", "mm_d_co", **wg)
    do = _wmm(dz2, w["co_w"], "nt", "mm_do", out_dtype=MXU_DTYPE)
    dqc, dkv = cross_bwd(s["qc"], s["kv"], do, "cross_bwd")
    g["cq_w"] = mm(s["h1"], dqc, "tn", "mm_d_cq", **wg)
    g["ckv_w"] = mm(mem, dkv, "tn", "mm_d_ckv", out_blocks=N_CHIPS, **wg)
    dh1 = _wmm(dqc, w["cq_w"], "nt", "mm_dh1")
    deps = halfway(g, dh1) if halfway is not None else ()
    dz1, g["ln1_g"], g["ln1_b"] = ln_bwd(dz2, dh1, s["xh1"], s["rs1"], w["ln1_g"], alpha, "ln1_bwd")
    merged = merge_fwd(s["proj"], s["pr"], s["pa"], "merge_refwd")
    g["w_out"] = mm(merged, dz1, "tn", "mm_d_out", deps=deps, **wg)
    dm = _wmm(dz1, w["w_out"], "nt", "mm_dmerged")
    dpr, dpa, dg_rnn, dg_attn = merge_bwd(s["proj"], s["pr"], s["pa"], dm, "merge_bwd")
    g["w_br_rnn"] = mm(s["y_rnn"], dpr, "tn", "mm_d_br_rnn", **wg)
    g["w_br_attn"] = mm(s["y_attn"], dpa, "tn", "mm_d_br_attn", **wg)
    dy_rnn = _wmm(dpr, w["w_br_rnn"], "nt", "mm_dy_rnn")
    dy_attn = _wmm(dpa, w["w_br_attn"], "nt", "mm_dy_attn")
    dq, dkb, dvb, dsink = attn_bwd(s["proj"], w["sinks"], tab, s["y_attn"], dy_attn, D, "attn_bwd")
    dk, dv = band_fold(dkb, dvb, "band_fold")
    g["sinks"] = dsink[:, :w["sinks"].shape[0]]
    dgr, gt = rg_scan_bwd(s["proj"], dy_rnn, s["hs"], s["a"], "rg_scan_bwd")
    dxc, g["w_rg"], g["w_ig"], g["b_rg"], g["b_ig"], g["lru_lambda"] = rg_gates_bwd(
        gt, s["hs"], s["xc"], s["r"], s["ig"], w["w_rg"], w["w_ig"], w["lru_lambda"], "rg_gates_bwd")
    dxr, g["conv_w"], g["conv_b"] = rg_conv_bwd(s["proj"], dxc, w["conv_w"], "rg_conv_bwd")
    dproj = jnp.concatenate([dxr, dgr, dq, dk, dv, dg_rnn, dg_attn], axis=1)
    g["w_in"] = mm(s["h"], dproj, "tn", "mm_d_in")
    dhm = _wmm(dproj, w["w_in"], "nt", "mm_dh")
    return axpby(dz1, dhm, alpha, "layer_dx"), g


def local_step(x, mem, target, depth, weights_of, grads_halfway, grads_done):
    alpha = _alpha(depth)
    tab = rope_table(x.shape[0])
    h, saved, layers = x, [], []
    for l in range(depth):
        wl, deps, late = weights_of(l, h)
        h, s, wl = layer_fwd(h, mem, wl, tab, alpha, deps, late)
        layers.append(wl)
        saved.append(s)
    dh, loss = loss_head(h, target, "loss_head")
    deps = ()
    for l in reversed(range(depth)):
        dh, g = layer_bwd(dh, mem, layers[l], saved[l], tab, alpha, deps, grads_halfway(l))
        deps = grads_done(l, g, dh)
    return loss, dh


def _pad_rows(flat):
    n = flat.shape[0]
    rows = -(-n // (LANES * SUBLANES)) * SUBLANES
    return jnp.pad(flat, (0, rows * LANES - n)).reshape(rows, LANES)


def kernel(x, mem, w_in, conv_w, conv_b, w_rg, b_rg, w_ig, b_ig, lru_lambda, w_br_rnn, w_br_attn, sinks, w_out, ln1_g, ln1_b, cq_w, ckv_w, co_w, ln2_g, ln2_b, ffn_wi, ffn_wo, ln3_g, ln3_b, loss_target, m_w_in, m_conv_w, m_conv_b, m_w_rg, m_b_rg, m_w_ig, m_b_ig, m_lru_lambda, m_w_br_rnn, m_w_br_attn, m_sinks, m_w_out, m_ln1_g, m_ln1_b, m_cq_w, m_ckv_w, m_co_w, m_ln2_g, m_ln2_b, m_ffn_wi, m_ffn_wo, m_ln3_g, m_ln3_b, v_w_in, v_conv_w, v_conv_b, v_w_rg, v_b_rg, v_w_ig, v_b_ig, v_lru_lambda, v_w_br_rnn, v_w_br_attn, v_sinks, v_w_out, v_ln1_g, v_ln1_b, v_cq_w, v_ckv_w, v_co_w, v_ln2_g, v_ln2_b, v_ffn_wi, v_ffn_wo, v_ln3_g, v_ln3_b):
    args = dict(locals())
    w = {n: args[n] for n in WEIGHTS}
    m = {n: args["m_" + n] for n in WEIGHTS}
    v = {n: args["v_" + n] for n in WEIGHTS}
    cx, cy, cc = _place()
    chip = 2 * cx + cy
    L = w_in.shape[0]

    place = jnp.stack([cc, chip]).astype(jnp.int32)
    cw_rows = _pad_rows(conv_w.reshape(-1))
    cw_all = gather_devices(cw_rows, "gather_conv_w")[0::2]
    cw_parts = cw_all.reshape(N_CHIPS, -1)[:, :conv_w.size].reshape((N_CHIPS,) + conv_w.shape)
    conv_full = jnp.concatenate([cw_parts[k] for k in range(N_CHIPS)], axis=2)

    shards = [{n: w[n][l].astype(MXU_DTYPE) for n in BIG} for l in range(L)]
    late_names = tuple(n for n in BIG if n not in GATHER_FIRST)
    gathering = {(0, GATHER_FIRST): gather_start([shards[0][n] for n in GATHER_FIRST], cw_rows, "gather_start_0a")}
    gathering[0, late_names] = gather_start([shards[0][n] for n in late_names], gathering[0, GATHER_FIRST][4], "gather_start_0b")

    def gathered(l, names, after, tag):
        lands = gather_pass(gather_wait(gathering.pop((l, names)), after, f"gather_wait_{tag}"), f"gather_pass_{tag}")
        wl = {}
        for n, gw in zip(names, lands):
            rows_joined = gw.reshape(gw.shape[:-3] + (-1, gw.shape[-1]))
            if n == "w_in":
                wl[n] = (jnp.concatenate([gw[k] for k in range(N_CHIPS)], axis=1), {})
            elif n in COL_BLOCKED:
                wl[n] = (gw, dict(b_blocks=N_CHIPS))
            elif n in GATE_WEIGHTS:
                wl[n] = rows_joined
            else:
                wl[n] = (rows_joined, {})
        return wl, lands

    def start_layer(l, after):
        if l >= L:
            return ()
        gathering[l, BIG] = gather_start([shards[l][n] for n in BIG], after, f"gather_start_{l}")
        return (gathering[l, BIG][4],)

    start_layer(1, gathering[0, late_names][4])

    def weights_of(l, h):
        deps, late = (), None
        if l == 0:
            wl, _ = gathered(0, GATHER_FIRST, h, "0a")

            def late(after):
                rest, lands = gathered(0, late_names, after, "0b")
                return rest, start_layer(2, lands[0])
        else:
            wl, lands = gathered(l, BIG, h, str(l))
            deps = start_layer(l + 2, lands[0])
        for n in SMALL:
            wl[n] = conv_full[l] if n == "conv_w" else w[n][l] if n == "sinks" else w[n][l][None, :]
        return wl, deps, late

    def for_chips(n, g):
        if n in COL_BLOCKED:
            return g
        if n in GATE_WEIGHTS:
            nb, bw, _ = g.shape
            g = g.reshape(nb, N_CHIPS, bw // N_CHIPS, bw).transpose(1, 0, 2, 3).reshape(N_CHIPS, nb * bw // N_CHIPS, bw)
        elif SHARD_AXIS[n] == 0:
            g = g.reshape(N_CHIPS, g.shape[0] // N_CHIPS, g.shape[1])
        else:
            g = jnp.stack(jnp.split(g, N_CHIPS, axis=1))
        return g.astype(MXU_DTYPE)

    reduced, scattering, small_grads = {}, {}, [None] * L
    late_grads = tuple(n for n in BIG if n not in SCATTER_FIRST)

    def start_scatter(l, names, g, after, tag):
        partial_sums = [for_chips(n, g[n])[None] for n in names]
        from_sibling = swap_sibling(partial_sums, f"grad_to_sibling_{tag}")
        chip_sums = [add_pair(place, a, b, f"grad_add_pair_{n}_{l}") for n, a, b in zip(names, partial_sums, from_sibling)]
        scattering[l, names] = scatter_start(chip_sums, after, f"grad_scatter_start_{tag}")
        return (scattering[l, names][4],)

    def finish_layer(l, after):
        for names in [k[1] for k in list(scattering) if k[0] == l]:
            tag = str(l) if names == BIG else f"{l}{'a' if names == SCATTER_FIRST else 'b'}"
            chip_sums, from_chips = scatter_wait(scattering.pop((l, names)), after, f"grad_scatter_wait_{tag}")
            for n, own, others in zip(names, chip_sums, from_chips):
                target = reduced.get(n, (L, 2 * own.shape[2], own.shape[3]))
                reduced[n] = add_chips(place, own, others, l, target, f"grad_add_chips_{n}_{l}")
        reduced.update(zip(BIG, join_halves([reduced[n] for n in BIG], l, f"grad_join_{l}")))

    def grads_halfway(l):
        if l > 0:
            return None

        def halfway(g, after):
            return start_scatter(0, SCATTER_FIRST, g, after, "0a")

        return halfway

    def grads_done(l, g, dh):
        small_grads[l] = {n: g[n] for n in SMALL}
        deps = start_scatter(l, late_grads if l == 0 else BIG, g, dh, "0b" if l == 0 else str(l))
        if l + 1 < L:
            finish_layer(l + 1, dh)
        return deps

    loss11, dx = local_step(x[0], mem[0], loss_target[0], L, weights_of, grads_halfway, grads_done)
    loss = lax.psum(loss11[0, 0], ("x", "y", "c"))
    updated, after = {}, dx
    if L > 1:
        for n in BIG:
            updated[n] = adamw(w[n], reduced[n].reshape(w[n].shape), m[n], v[n], f"adamw_{n}_upper", layers=(1, L))
        after = jnp.stack([updated[n][0][(0,) * w[n].ndim] for n in BIG])
    finish_layer(0, after)
    gshard = {n: reduced[n].reshape(w[n].shape) for n in BIG}

    small_full = {n: jnp.stack([gl[n] for gl in small_grads]).reshape(w[n].shape[:1] + ((CONV_WIDTH, -1) if n == "conv_w" else (-1,)))
                  for n in SMALL}
    small_flat = jnp.concatenate([small_full[n].reshape(-1) for n in SMALL])
    small_sum = sum_devices(gather_devices(_pad_rows(small_flat), "gather_small_grads"), "sum_small_grads").reshape(-1)
    off = 0
    for n in SMALL:
        gfull = small_sum[off:off + small_full[n].size].reshape(small_full[n].shape)
        off += small_full[n].size
        if n == "conv_w":
            width = conv_w.shape[2]
            gfull = lax.dynamic_slice_in_dim(gfull, chip * width, width, axis=2)
        gshard[n] = gfull

    delta, new_m, new_v, grad = {}, {}, {}, {}
    for n in WEIGHTS:
        some = dict(layers=(0, 1), into=updated[n]) if n in updated else {}
        delta[n], new_m[n], new_v[n], grad[n] = adamw(w[n], gshard[n], m[n], v[n], "adamw_" + n, **some)
    return (loss, dx[None], *[grad[n] for n in WEIGHTS], *[delta[n] for n in WEIGHTS], *[new_m[n] for n in WEIGHTS],
            *[new_v[n] for n in WEIGHTS])
```

```python
import functools
import math

import jax
import jax.numpy as jnp
import numpy as np
from jax import lax
from jax.experimental import pallas as pl
from jax.experimental.pallas import tpu as pltpu

F32 = jnp.float32
BF16 = jnp.bfloat16
MXU_DTYPE = BF16

HEAD_DIM = 64
N_KV_HEADS = 2
WINDOW = 128
ROT_DIM = HEAD_DIM // 4
ROPE_THETA = 500000.0
CROSS_HEADS = 4
RNN_BLOCKS = 4
CONV_WIDTH = 4
LRU_C = 8.0
LN_EPS = 1e-5
NEG_INF = -1e30
ADAM_LR = 0.001
ADAM_B1 = 0.9
ADAM_B2 = 0.999
ADAM_EPS = 1e-08
ADAM_WD = 0.01
ADAM_STEP = 10

VMEM_BYTES_V7X = 64 * 1024 * 1024
VMEM_BLOCK_BUDGET = 36 * 1024 * 1024
LANES = 128
SUBLANES = 8

MESH_ID = pl.DeviceIdType.MESH
N_CHIPS = 4
N_DEV = 8

BIG = ("w_in", "w_rg", "w_ig", "w_br_rnn", "w_br_attn", "w_out", "cq_w", "ckv_w", "co_w", "ffn_wi", "ffn_wo")
SHARD_AXIS = {"w_in": 0, "w_rg": 1, "w_ig": 1, "w_br_rnn": 0, "w_br_attn": 0, "w_out": 0, "cq_w": 0, "ckv_w": 1,
              "co_w": 0, "ffn_wi": 1, "ffn_wo": 0}
SMALL = ("conv_w", "conv_b", "b_rg", "b_ig", "lru_lambda", "sinks", "ln1_g", "ln1_b", "ln2_g", "ln2_b", "ln3_g", "ln3_b")
WEIGHTS = ("w_in", "conv_w", "conv_b", "w_rg", "b_rg", "w_ig", "b_ig", "lru_lambda", "w_br_rnn", "w_br_attn", "sinks",
           "w_out", "ln1_g", "ln1_b", "cq_w", "ckv_w", "co_w", "ln2_g", "ln2_b", "ffn_wi", "ffn_wo", "ln3_g", "ln3_b")
GATE_WEIGHTS = ("w_rg", "w_ig")
COL_BLOCKED = ("ckv_w", "ffn_wi")
GATHER_FIRST = ("w_in", "w_rg", "w_ig")
SCATTER_FIRST = ("ffn_wo", "ffn_wi", "co_w", "cq_w", "ckv_w")


def _params(dims=None, vmem=None):
    return pltpu.CompilerParams(dimension_semantics=dims, vmem_limit_bytes=vmem)


def _vmem_limit(block_bytes, temp_bytes=0):
    want = int(2 * block_bytes + temp_bytes) + (6 << 20)
    return max(32 << 20, min(want, VMEM_BYTES_V7X - (6 << 20)))


def _divisors(n, align, cap):
    out = [d for d in range(align, min(n, cap) + 1, align) if n % d == 0]
    if n <= cap and n not in out:
        out.append(n)
    return sorted(out, reverse=True) or [n]


PIN_MIN_ELEMENTS = 1 << 18


def hbm_call(body, **kw):
    def in_hbm(s):
        return pltpu.HBM(s.shape, s.dtype) if math.prod(s.shape) >= PIN_MIN_ELEMENTS else s

    shapes = kw.pop("out_shape")
    shapes = [in_hbm(s) for s in shapes] if isinstance(shapes, (list, tuple)) else in_hbm(shapes)
    call = pl.pallas_call(body, out_shape=shapes, **kw)

    def run(*args):
        return call(*[pltpu.with_memory_space_constraint(a, pltpu.HBM) if a.size >= PIN_MIN_ELEMENTS else a for a in args])

    return run


def _sigmoid(x):
    return 1.0 / (1.0 + jnp.exp(-x))


def _gelu_parts(x):
    c = math.sqrt(2.0 / math.pi)
    u = c * (x + 0.044715 * x * x * x)
    t = jnp.tanh(u)
    return t, c * (1.0 + 3 * 0.044715 * x * x)


def _gelu(x):
    t, _ = _gelu_parts(x)
    return 0.5 * x * (1.0 + t)


def _gelu_grad(x):
    t, du = _gelu_parts(x)
    return 0.5 * (1.0 + t) + 0.5 * x * (1.0 - t * t) * du


def _neg_expm1(x):
    series = x * (1.0 + x * (0.5 + x * (1.0 / 6 + x * (1.0 / 24 + x * (1.0 / 120)))))
    return -jnp.where(x > -0.1, series, jnp.exp(x) - 1.0)


def _softplus_neg(lam):
    x = -lam
    return jnp.maximum(x, 0.0) + jnp.log1p(jnp.exp(-jnp.abs(x)))


STEP_US = 0.35
HBM_BYTES_PER_US = 2.5e6
MXU_FLOPS_PER_US = 7e8


def _layer_norm(z, g, b):
    mu = jnp.mean(z, axis=-1, keepdims=True)
    zc = z - mu
    rs = lax.rsqrt(jnp.mean(zc * zc, axis=-1, keepdims=True) + LN_EPS)
    xh = zc * rs
    return xh * g + b, xh, rs


def mm(a, b, mode, name, *, b_index=(), a_blocks=0, b_blocks=0, out_blocks=0, out_dtype=F32, deps=(), post_norm=None):
    nlead = len(b_index) + (1 if b_blocks else 0)
    bk, bn = b.shape[nlead:]
    M, K = (a.shape[-1], a.shape[-2]) if mode == "tn" else (a.shape[-2], a.shape[-1] * max(a_blocks, 1))
    N = bk if mode == "nt" else bn * max(b_blocks, 1) if mode == "nn" or mode == "tn" else bn
    asz, bsz, osz = a.dtype.itemsize, b.dtype.itemsize, jnp.dtype(out_dtype).itemsize
    n_unit = math.gcd(N // max(out_blocks, 1), N // max(b_blocks, 1) if mode != "nt" else N)
    k_unit = math.gcd(K // max(a_blocks, 1), K // max(b_blocks, 1) if mode == "nt" else K)
    tms = _divisors(M, LANES if mode == "tn" else SUBLANES, 2048)
    tns = [N] if post_norm else _divisors(n_unit, LANES, 2048)
    tks = _divisors(k_unit, LANES, k_unit)
    best = None
    for tm in tms:
        for tn in tns:
            for tk in tks:
                nk = K // tk
                scratch = tm * tn * 4 if (nk > 1 and osz != 4) else 0
                blocks = tm * tk * asz + tn * tk * bsz + tm * tn * osz * (3 if post_norm else 1)
                temps = tm * tk * (2 + (4 if mode == "tn" else 0)) + tn * tk * 2 + tm * tn * 4 + scratch
                if 2 * blocks + temps > VMEM_BLOCK_BUDGET + (8 << 20):
                    continue
                ni, nj = M // tm, N // tn
                traffic = M * K * asz * (nj if nk > 1 else 1) + N * K * bsz * (1 if nj * nk == 1 else ni) + M * N * osz
                busy = max(traffic / HBM_BYTES_PER_US, 2.0 * M * N * K / MXU_FLOPS_PER_US)
                cost = ni * nj * nk * STEP_US + busy + blocks / HBM_BYTES_PER_US
                if best is None or cost < best[0]:
                    best = (cost, tm, tn, tk, blocks, temps)
    _, tm, tn, tk, blocks, temps = best
    nk = K // tk
    use_scratch = nk > 1 and osz != 4

    def split(index, total, blocks, tile):
        per = total // blocks // tile
        return index // per, index % per

    def body(a_ref, b_ref, *rest):
        rest = rest[len(deps):]
        if post_norm:
            h_ref, g_ref, beta_ref, o_ref, xh_ref, rs_ref = rest[:6]
            acc = rest[6:]
        else:
            o_ref, acc = rest[0], rest[1:]
        av = a_ref[...].astype(MXU_DTYPE)
        bv = b_ref[...].astype(MXU_DTYPE)
        dn = {"nn": (((1,), (0,)), ((), ())), "nt": (((1,), (1,)), ((), ())), "tn": (((0,), (0,)), ((), ()))}[mode]
        r = lax.dot_general(av, bv, dn, preferred_element_type=F32)

        def normalise(f):
            o_ref[...], xh_ref[...], rs_ref[...] = _layer_norm(post_norm[3] * h_ref[...] + f, g_ref[...], beta_ref[...])

        if nk == 1 and post_norm:
            normalise(r)
        elif nk == 1:
            o_ref[...] = r.astype(o_ref.dtype)
        else:
            acc_ref = acc[0] if use_scratch else o_ref

            @pl.when(pl.program_id(2) == 0)
            def _():
                acc_ref[...] = r

            @pl.when(pl.program_id(2) > 0)
            def _():
                acc_ref[...] += r

            if use_scratch:
                @pl.when(pl.program_id(2) == nk - 1)
                def _():
                    o_ref[...] = acc_ref[...].astype(o_ref.dtype)
            elif post_norm:
                @pl.when(pl.program_id(2) == nk - 1)
                def _():
                    normalise(o_ref[...])

    if mode == "tn":
        a_spec = pl.BlockSpec((tk, tm), lambda i, j, k: (k, i))
    elif a_blocks:
        a_spec = pl.BlockSpec((None, tm, tk), lambda i, j, k: (split(k, K, a_blocks, tk)[0], i, split(k, K, a_blocks, tk)[1]))
    else:
        a_spec = pl.BlockSpec((tm, tk), lambda i, j, k: (i, k))
    lead = (None,) * nlead
    if mode == "nt":
        bmap = ((lambda i, j, k: b_index + (split(k, K, b_blocks, tk)[0], j, split(k, K, b_blocks, tk)[1])) if b_blocks
                else (lambda i, j, k: b_index + (j, k)))
        b_spec = pl.BlockSpec(lead + (tn, tk), bmap)
    else:
        bmap = ((lambda i, j, k: b_index + (split(j, N, b_blocks, tn)[0], k, split(j, N, b_blocks, tn)[1])) if b_blocks
                else (lambda i, j, k: b_index + (k, j)))
        b_spec = pl.BlockSpec(lead + (tk, tn), bmap)
    if out_blocks:
        o_spec = pl.BlockSpec((None, tm, tn), lambda i, j, k: (split(j, N, out_blocks, tn)[0], i, split(j, N, out_blocks, tn)[1]))
        o_shape = jax.ShapeDtypeStruct((out_blocks, M, N // out_blocks), out_dtype)
    else:
        o_spec = pl.BlockSpec((tm, tn), lambda i, j, k: (i, j))
        o_shape = jax.ShapeDtypeStruct((M, N), out_dtype)
    in_specs, extra = [a_spec, b_spec] + [pl.BlockSpec(memory_space=pl.ANY)] * len(deps), ()
    if post_norm:
        vec = pl.BlockSpec((1, N), lambda i, j, k: (0, 0))
        in_specs += [pl.BlockSpec((tm, N), lambda i, j, k: (i, 0)), vec, vec]
        o_spec = [o_spec, pl.BlockSpec((tm, N), lambda i, j, k: (i, 0)), pl.BlockSpec((tm, 1), lambda i, j, k: (i, 0))]
        o_shape = [o_shape, jax.ShapeDtypeStruct((M, N), F32), jax.ShapeDtypeStruct((M, 1), F32)]
        extra = post_norm[:3]
    return hbm_call(
        body, name=name, grid=(M // tm, N // tn, nk), in_specs=in_specs, out_specs=o_spec, out_shape=o_shape,
        scratch_shapes=[pltpu.VMEM((tm, tn), F32)] if use_scratch else [],
        compiler_params=_params(("parallel", "parallel", "arbitrary"), _vmem_limit(blocks, temps)),
    )(a, b, *deps, *extra)


ROW_TILE = 512
GATE_ROWS = 1024


def ln_bwd(dy_a, dy_b, xh, rs, g, c1, name):
    S, D = xh.shape
    tr = min(ROW_TILE, S)
    two = dy_b is not None

    def body(*refs):
        if two:
            a_ref, b_ref, xh_ref, rs_ref, g_ref, dz_ref, dg_ref, db_ref = refs
            dy = c1 * a_ref[...] + b_ref[...]
        else:
            a_ref, xh_ref, rs_ref, g_ref, dz_ref, dg_ref, db_ref = refs
            dy = a_ref[...]
        x = xh_ref[...]
        dyg = dy * g_ref[...]
        m1 = jnp.mean(dyg, axis=-1, keepdims=True)
        m2 = jnp.mean(dyg * x, axis=-1, keepdims=True)
        dz_ref[...] = rs_ref[...] * (dyg - m1 - x * m2)

        @pl.when(pl.program_id(0) == 0)
        def _():
            dg_ref[...] = jnp.zeros_like(dg_ref)
            db_ref[...] = jnp.zeros_like(db_ref)

        dg_ref[...] += jnp.sum(dy * x, axis=0, keepdims=True)
        db_ref[...] += jnp.sum(dy, axis=0, keepdims=True)

    row = pl.BlockSpec((tr, D), lambda i: (i, 0))
    vec = pl.BlockSpec((1, D), lambda i: (0, 0))
    ins = [row, row] if two else [row]
    args = (dy_a, dy_b) if two else (dy_a,)
    return hbm_call(
        body, name=name, grid=(S // tr,), in_specs=ins + [row, pl.BlockSpec((tr, 1), lambda i: (i, 0)), vec],
        out_specs=[row, vec, vec],
        out_shape=[jax.ShapeDtypeStruct((S, D), F32), jax.ShapeDtypeStruct((1, D), F32), jax.ShapeDtypeStruct((1, D), F32)],
        compiler_params=_params(("arbitrary",), 48 << 20),
    )(*args, xh, rs, g)


def axpby(a, b, c1, name):
    S, D = a.shape
    tr = min(ROW_TILE, S)

    def body(a_ref, b_ref, o_ref):
        o_ref[...] = c1 * a_ref[...] + b_ref[...]

    row = pl.BlockSpec((tr, D), lambda i: (i, 0))
    return hbm_call(body, name=name, grid=(S // tr,), in_specs=[row, row], out_specs=row,
                          out_shape=jax.ShapeDtypeStruct((S, D), F32), compiler_params=_params(("parallel",)))(a, b)


def loss_head(y, t, name):
    S, D = y.shape
    tr = min(ROW_TILE, S)
    nsteps = S // tr

    def body(y_ref, t_ref, dy_ref, l_ref, acc_ref):
        i = pl.program_id(0)

        @pl.when(i == 0)
        def _():
            acc_ref[...] = jnp.zeros_like(acc_ref)

        e = y_ref[...] - t_ref[...]
        dy_ref[...] = e * (1.0 / D)
        acc_ref[...] += jnp.sum(e * e, axis=0, keepdims=True)

        @pl.when(i == nsteps - 1)
        def _():
            l_ref[...] = jnp.sum(acc_ref[...], axis=1, keepdims=True) * (0.5 / D)

    row = pl.BlockSpec((tr, D), lambda i: (i, 0))
    return hbm_call(
        body, name=name, grid=(nsteps,), in_specs=[row, row],
        out_specs=[row, pl.BlockSpec((1, 1), lambda i: (0, 0))],
        out_shape=[jax.ShapeDtypeStruct((S, D), F32), jax.ShapeDtypeStruct((1, 1), F32)],
        scratch_shapes=[pltpu.VMEM((1, D), F32)], compiler_params=_params(("arbitrary",)),
    )(y, t)


SWIGLU_ROWS = 256


def swiglu_fwd(gu, name):
    _, S, Fh = gu.shape
    tc = _divisors(Fh, LANES, 1536)[0]
    tr = min(SWIGLU_ROWS, S)

    def body(gu_ref, o_ref):
        g = gu_ref[0]
        o_ref[...] = (g * _sigmoid(g) * gu_ref[1]).astype(o_ref.dtype)

    return hbm_call(
        body, name=name, grid=(S // tr, Fh // tc), in_specs=[pl.BlockSpec((2, tr, tc), lambda i, j: (0, i, j))],
        out_specs=pl.BlockSpec((tr, tc), lambda i, j: (i, j)), out_shape=jax.ShapeDtypeStruct((S, Fh), MXU_DTYPE),
        compiler_params=_params(("parallel", "parallel")),
    )(gu)


def swiglu_bwd(gu, dact, name):
    _, S, Fh = gu.shape
    tc = _divisors(Fh, LANES, 1536)[0]
    tr = min(SWIGLU_ROWS, S)

    def body(gu_ref, d_ref, o_ref):
        g, u, d = gu_ref[0], gu_ref[1], d_ref[...]
        s = _sigmoid(g)
        o_ref[0] = (d * u * (s * (1.0 + g * (1.0 - s)))).astype(o_ref.dtype)
        o_ref[1] = (d * (g * s)).astype(o_ref.dtype)

    both = pl.BlockSpec((2, tr, tc), lambda i, j: (0, i, j))
    return hbm_call(
        body, name=name, grid=(S // tr, Fh // tc), in_specs=[both, pl.BlockSpec((tr, tc), lambda i, j: (i, j))],
        out_specs=both, out_shape=jax.ShapeDtypeStruct((2, S, Fh), MXU_DTYPE), compiler_params=_params(("parallel", "parallel")),
    )(gu, dact)


GATE_COLS = 256


def merge_fwd(proj, pr, pa, name):
    S, D = pr.shape
    tr = min(GATE_ROWS, S)
    c0 = (3 * D + 2 * N_KV_HEADS * HEAD_DIM) // GATE_COLS
    c1 = c0 + D // GATE_COLS

    def body(gr_ref, ga_ref, pr_ref, pa_ref, o_ref):
        o_ref[...] = (_sigmoid(gr_ref[...]) * pr_ref[...] + _sigmoid(ga_ref[...]) * pa_ref[...]).astype(o_ref.dtype)

    blk = pl.BlockSpec((tr, GATE_COLS), lambda i, j: (i, j))
    return hbm_call(
        body, name=name, grid=(S // tr, D // GATE_COLS),
        in_specs=[pl.BlockSpec((tr, GATE_COLS), lambda i, j: (i, c0 + j)), pl.BlockSpec((tr, GATE_COLS), lambda i, j: (i, c1 + j)),
                  blk, blk],
        out_specs=blk, out_shape=jax.ShapeDtypeStruct((S, D), MXU_DTYPE), compiler_params=_params(("parallel", "parallel")),
    )(proj, proj, pr, pa)


def merge_bwd(proj, pr, pa, dm, name):
    S, D = pr.shape
    tr = min(GATE_ROWS, S)
    c0 = (3 * D + 2 * N_KV_HEADS * HEAD_DIM) // GATE_COLS
    c1 = c0 + D // GATE_COLS

    def body(gr_ref, ga_ref, pr_ref, pa_ref, dm_ref, dpr_ref, dpa_ref, dgr_ref, dga_ref):
        sr, sa, d = _sigmoid(gr_ref[...]), _sigmoid(ga_ref[...]), dm_ref[...]
        dpr_ref[...] = (d * sr).astype(dpr_ref.dtype)
        dpa_ref[...] = (d * sa).astype(dpa_ref.dtype)
        dgr_ref[...] = (d * pr_ref[...] * (sr * (1.0 - sr))).astype(dgr_ref.dtype)
        dga_ref[...] = (d * pa_ref[...] * (sa * (1.0 - sa))).astype(dga_ref.dtype)

    blk = pl.BlockSpec((tr, GATE_COLS), lambda i, j: (i, j))
    sds = jax.ShapeDtypeStruct((S, D), MXU_DTYPE)
    return hbm_call(
        body, name=name, grid=(S // tr, D // GATE_COLS),
        in_specs=[pl.BlockSpec((tr, GATE_COLS), lambda i, j: (i, c0 + j)), pl.BlockSpec((tr, GATE_COLS), lambda i, j: (i, c1 + j)),
                  blk, blk, blk],
        out_specs=[blk, blk, blk, blk], out_shape=[sds, sds, sds, sds], compiler_params=_params(("parallel", "parallel")),
    )(proj, proj, pr, pa, dm)


RG_ROWS = 512


def _shift_down(cur, prev, d, row, first):
    halo = jnp.where(first, 0.0, pltpu.roll(prev, d, 0))
    return jnp.where(row >= d, pltpu.roll(cur, d, 0), halo)


def _shift_up(cur, nxt, d, row, last, tr):
    halo = jnp.where(last, 0.0, pltpu.roll(nxt, tr - d, 0))
    return jnp.where(row < tr - d, pltpu.roll(cur, tr - d, 0), halo)


def _lru_coeffs(r, lam):
    sp = _softplus_neg(lam)
    la = -LRU_C * r * sp
    return sp, la, jnp.exp(la), _neg_expm1(2.0 * la)


def rg_gates_fwd(proj, conv_w, conv_b, w_rg, b_rg, w_ig, b_ig, lam, name):
    S = proj.shape[0]
    nblk, bw, _ = w_rg.shape
    D = nblk * bw
    tr = min(RG_ROWS, S)

    def body(xr_ref, xp_ref, cw_ref, cb_ref, wr_ref, br_ref, wi_ref, bi_ref, lam_ref, xc_ref, r_ref, i_ref, a_ref, b_ref):
        first = pl.program_id(1) == 0
        cur, prev = xr_ref[...], xp_ref[...]
        row = lax.broadcasted_iota(jnp.int32, cur.shape, 0)
        xc = cb_ref[...]
        for k in range(CONV_WIDTH - 1):
            xc = xc + _shift_down(cur, prev, CONV_WIDTH - 1 - k, row, first) * cw_ref[k:k + 1, :]
        xc = xc + cur * cw_ref[CONV_WIDTH - 1:CONV_WIDTH, :]
        xm = xc.astype(MXU_DTYPE)
        r = _sigmoid(jnp.dot(xm, wr_ref[...].astype(MXU_DTYPE), preferred_element_type=F32) + br_ref[...])
        ig = _sigmoid(jnp.dot(xm, wi_ref[...].astype(MXU_DTYPE), preferred_element_type=F32) + bi_ref[...])
        _, _, a, em = _lru_coeffs(r, lam_ref[...])
        xc_ref[...] = xc
        r_ref[...] = r
        i_ref[...] = ig
        a_ref[...] = a
        b_ref[...] = jnp.sqrt(em) * (ig * xc)

    tile = pl.BlockSpec((tr, bw), lambda n, i: (i, n))
    vec = pl.BlockSpec((1, bw), lambda n, i: (0, n))
    wblk = pl.BlockSpec((None, bw, bw), lambda n, i: (n, 0, 0))
    sds = jax.ShapeDtypeStruct((S, D), F32)
    return hbm_call(
        body, name=name, grid=(nblk, S // tr),
        in_specs=[tile, pl.BlockSpec((tr, bw), lambda n, i: (jnp.maximum(i - 1, 0), n)),
                  pl.BlockSpec((CONV_WIDTH, bw), lambda n, i: (0, n)), vec, wblk, vec, wblk, vec, vec],
        out_specs=[tile] * 5, out_shape=[sds] * 5, compiler_params=_params(("parallel", "parallel")),
    )(proj, proj, conv_w, conv_b, w_rg, b_rg, w_ig, b_ig, lam)


SCAN_COLS = 256
CHUNK = SUBLANES
SCAN_UNROLL = 4


def rg_scan_fwd(proj, a, b, name):
    S, D = a.shape
    cb = min(SCAN_COLS, D)
    goff = D // cb

    def body(a_ref, b_ref, g_ref, hs_ref, y_ref):
        row = lax.broadcasted_iota(jnp.int32, (CHUNK, cb), 0)

        def step(c, carry):
            r0 = pl.multiple_of(c * CHUNK, CHUNK)
            A = a_ref[pl.ds(r0, CHUNK), :]
            B = b_ref[pl.ds(r0, CHUNK), :]
            for d in (1, 2, 4):
                As = jnp.where(row >= d, pltpu.roll(A, d, 0), 1.0)
                Bs = jnp.where(row >= d, pltpu.roll(B, d, 0), 0.0)
                B = A * Bs + B
                A = A * As
            hs_ref[pl.ds(r0, CHUNK), :] = B + A * carry
            a_end = jnp.sum(jnp.where(row == CHUNK - 1, A, 0.0), axis=0, keepdims=True)
            b_end = jnp.sum(jnp.where(row == CHUNK - 1, B, 0.0), axis=0, keepdims=True)
            return b_end + a_end * carry

        lax.fori_loop(0, S // CHUNK, step, jnp.zeros((1, cb), F32), unroll=SCAN_UNROLL)
        y_ref[...] = (hs_ref[...] * _gelu(g_ref[...])).astype(y_ref.dtype)

    col = pl.BlockSpec((S, cb), lambda j: (0, j))
    return hbm_call(
        body, name=name, grid=(D // cb,), in_specs=[col, col, pl.BlockSpec((S, cb), lambda j: (0, goff + j))],
        out_specs=[col, col], out_shape=[jax.ShapeDtypeStruct((S, D), F32), jax.ShapeDtypeStruct((S, D), MXU_DTYPE)],
        compiler_params=_params(("parallel",), _vmem_limit(5 * S * cb * 4, 4 * S * cb * 4)),
    )(a, b, proj)


def rg_scan_bwd(proj, dy, hs, a, name):
    S, D = a.shape
    cb = min(SCAN_COLS, D)
    goff = D // cb
    nchunks = S // CHUNK

    def body(g_ref, dy_ref, hs_ref, a_ref, dg_ref, gt_ref):
        gate, dy = g_ref[...], dy_ref[...]
        dg_ref[...] = (dy * hs_ref[...] * _gelu_grad(gate)).astype(dg_ref.dtype)
        gt_ref[...] = dy * _gelu(gate)
        row = lax.broadcasted_iota(jnp.int32, (CHUNK, cb), 0)

        def step(k, carry):
            c = nchunks - 1 - k
            r0 = pl.multiple_of(c * CHUNK, CHUNK)
            rn = pl.multiple_of(jnp.minimum(c + 1, nchunks - 1) * CHUNK, CHUNK)
            last = c == nchunks - 1
            nxt = jnp.where(last, 0.0, pltpu.roll(a_ref[pl.ds(rn, CHUNK), :], CHUNK - 1, 0))
            A = jnp.where(row < CHUNK - 1, pltpu.roll(a_ref[pl.ds(r0, CHUNK), :], CHUNK - 1, 0), nxt)
            B = gt_ref[pl.ds(r0, CHUNK), :]
            for d in (1, 2, 4):
                As = jnp.where(row < CHUNK - d, pltpu.roll(A, CHUNK - d, 0), 1.0)
                Bs = jnp.where(row < CHUNK - d, pltpu.roll(B, CHUNK - d, 0), 0.0)
                B = A * Bs + B
                A = A * As
            gt_ref[pl.ds(r0, CHUNK), :] = B + A * carry
            a_end = jnp.sum(jnp.where(row == 0, A, 0.0), axis=0, keepdims=True)
            b_end = jnp.sum(jnp.where(row == 0, B, 0.0), axis=0, keepdims=True)
            return b_end + a_end * carry

        lax.fori_loop(0, nchunks, step, jnp.zeros((1, cb), F32), unroll=SCAN_UNROLL)

    col = pl.BlockSpec((S, cb), lambda j: (0, j))
    return hbm_call(
        body, name=name, grid=(D // cb,), in_specs=[pl.BlockSpec((S, cb), lambda j: (0, goff + j)), col, col, col],
        out_specs=[col, col], out_shape=[jax.ShapeDtypeStruct((S, D), MXU_DTYPE), jax.ShapeDtypeStruct((S, D), F32)],
        compiler_params=_params(("parallel",), _vmem_limit(6 * S * cb * 4, 6 * S * cb * 4)),
    )(proj, dy, hs, a)


def rg_gates_bwd(gt, hs, xc, r, ig, w_rg, w_ig, lam, name):
    S, D = xc.shape
    nblk, bw, _ = w_rg.shape
    tr = min(RG_ROWS, S)

    def body(gt_ref, hs_ref, hp_ref, xc_ref, r_ref, i_ref, wr_ref, wi_ref, lam_ref,
             dxc_ref, dwr_ref, dwi_ref, dbr_ref, dbi_ref, dl_ref):
        step = pl.program_id(1)
        g, hs, xc, r, ig, lam = gt_ref[...], hs_ref[...], xc_ref[...], r_ref[...], i_ref[...], lam_ref[...]
        row = lax.broadcasted_iota(jnp.int32, g.shape, 0)
        hprev = _shift_down(hs, hp_ref[...], 1, row, step == 0)
        sp, _, a, em = _lru_coeffs(r, lam)
        mult = jnp.sqrt(em)
        du = g * mult
        dla = g * hprev * a - (g * (ig * xc)) * (a * a) / mult
        dpr = (dla * (-LRU_C * sp)) * (r * (1.0 - r))
        dpi = (du * xc) * (ig * (1.0 - ig))
        dprm, dpim = dpr.astype(MXU_DTYPE), dpi.astype(MXU_DTYPE)
        nt = (((1,), (1,)), ((), ()))
        dxc_ref[...] = (du * ig + lax.dot_general(dprm, wr_ref[...].astype(MXU_DTYPE), nt, preferred_element_type=F32)
                        + lax.dot_general(dpim, wi_ref[...].astype(MXU_DTYPE), nt, preferred_element_type=F32))

        @pl.when(step == 0)
        def _():
            for ref in (dwr_ref, dwi_ref, dbr_ref, dbi_ref, dl_ref):
                ref[...] = jnp.zeros_like(ref)

        xct = xc.T.astype(MXU_DTYPE)
        dwr_ref[...] += jnp.dot(xct, dprm, preferred_element_type=F32)
        dwi_ref[...] += jnp.dot(xct, dpim, preferred_element_type=F32)
        dbr_ref[...] += jnp.sum(dpr, axis=0, keepdims=True)
        dbi_ref[...] += jnp.sum(dpi, axis=0, keepdims=True)
        dl_ref[...] += jnp.sum(dla * (-LRU_C * r), axis=0, keepdims=True) * (-_sigmoid(-lam))

    tile = pl.BlockSpec((tr, bw), lambda n, i: (i, n))
    vec = pl.BlockSpec((1, bw), lambda n, i: (0, n))
    wblk = pl.BlockSpec((None, bw, bw), lambda n, i: (n, 0, 0))
    return hbm_call(
        body, name=name, grid=(nblk, S // tr),
        in_specs=[tile, tile, pl.BlockSpec((tr, bw), lambda n, i: (jnp.maximum(i - 1, 0), n)), tile, tile, tile, wblk, wblk, vec],
        out_specs=[tile, wblk, wblk, vec, vec, vec],
        out_shape=[jax.ShapeDtypeStruct((S, D), F32), jax.ShapeDtypeStruct((nblk, bw, bw), F32), jax.ShapeDtypeStruct((nblk, bw, bw), F32),
                   jax.ShapeDtypeStruct((1, D), F32), jax.ShapeDtypeStruct((1, D), F32), jax.ShapeDtypeStruct((1, D), F32)],
        compiler_params=_params(("parallel", "arbitrary")),
    )(gt, hs, hs, xc, r, ig, w_rg, w_ig, lam)


def rg_conv_bwd(proj, dxc, conv_w, name):
    S, D = dxc.shape
    bw = min(SCAN_COLS, D)
    tr = min(RG_ROWS, S)
    nsteps = S // tr

    def body(d_ref, dn_ref, xr_ref, xp_ref, cw_ref, dxr_ref, dcw_ref, dcb_ref):
        step = pl.program_id(1)
        d, xr = d_ref[...], xr_ref[...]
        row = lax.broadcasted_iota(jnp.int32, d.shape, 0)
        dxr = d * cw_ref[CONV_WIDTH - 1:CONV_WIDTH, :]
        for k in range(CONV_WIDTH - 1):
            dxr = dxr + _shift_up(d, dn_ref[...], CONV_WIDTH - 1 - k, row, step == nsteps - 1, tr) * cw_ref[k:k + 1, :]
        dxr_ref[...] = dxr.astype(dxr_ref.dtype)

        @pl.when(step == 0)
        def _():
            dcw_ref[...] = jnp.zeros_like(dcw_ref)
            dcb_ref[...] = jnp.zeros_like(dcb_ref)

        for k in range(CONV_WIDTH - 1):
            xs = _shift_down(xr, xp_ref[...], CONV_WIDTH - 1 - k, row, step == 0)
            dcw_ref[k:k + 1, :] += jnp.sum(d * xs, axis=0, keepdims=True)
        dcw_ref[CONV_WIDTH - 1:CONV_WIDTH, :] += jnp.sum(d * xr, axis=0, keepdims=True)
        dcb_ref[...] += jnp.sum(d, axis=0, keepdims=True)

    tile = pl.BlockSpec((tr, bw), lambda n, i: (i, n))
    cwb = pl.BlockSpec((CONV_WIDTH, bw), lambda n, i: (0, n))
    return hbm_call(
        body, name=name, grid=(D // bw, nsteps),
        in_specs=[tile, pl.BlockSpec((tr, bw), lambda n, i: (jnp.minimum(i + 1, nsteps - 1), n)), tile,
                  pl.BlockSpec((tr, bw), lambda n, i: (jnp.maximum(i - 1, 0), n)), cwb],
        out_specs=[tile, cwb, pl.BlockSpec((1, bw), lambda n, i: (0, n))],
        out_shape=[jax.ShapeDtypeStruct((S, D), MXU_DTYPE), jax.ShapeDtypeStruct((CONV_WIDTH, D), F32), jax.ShapeDtypeStruct((1, D), F32)],
        compiler_params=_params(("parallel", "arbitrary")),
    )(dxc, dxc, proj, proj, conv_w)


def rope_table(S):
    half = ROT_DIM // 2
    pos = jnp.arange(S, dtype=F32)
    inv = ROPE_THETA ** (-jnp.arange(0, ROT_DIM, 2, dtype=F32) / ROT_DIM)
    ang = pos[:, None] * inv[None, :]
    cos, sin = jnp.cos(ang), jnp.sin(ang)
    zero = jnp.zeros((S, HEAD_DIM - ROT_DIM), F32)
    c = jnp.concatenate([cos, cos, zero + 1.0], axis=1)
    a = jnp.concatenate([-sin, jnp.zeros((S, half), F32), zero], axis=1)
    b = jnp.concatenate([jnp.zeros((S, half), F32), sin, zero], axis=1)
    return jnp.stack([jnp.tile(t, (1, LANES // HEAD_DIM)) for t in (c, a, b)])


def _rope(t, tab):
    half = ROT_DIM // 2
    return t * tab[0] + pltpu.roll(t, LANES - half, 1) * tab[1] + pltpu.roll(t, half, 1) * tab[2]


def _rope_t(d, tab):
    half = ROT_DIM // 2
    return d * tab[0] + pltpu.roll(d * tab[1], half, 1) + pltpu.roll(d * tab[2], LANES - half, 1)


def _dup_head(t, hk, lo):
    sw = pltpu.roll(t, HEAD_DIM, 1)
    return jnp.where(lo, t, sw) if hk == 0 else jnp.where(lo, sw, t)


def _attn_common(n, sink_ref, q_ref, kp_ref, kc_ref, vp_ref, vc_ref, tc_ref, tp_ref, hk, pairs):
    tq = (tc_ref[0], tc_ref[1], tc_ref[2])
    tp = (tp_ref[0], tp_ref[1], tp_ref[2])
    lo = lax.broadcasted_iota(jnp.int32, (WINDOW, LANES), 1) < HEAD_DIM
    lo2 = lax.broadcasted_iota(jnp.int32, (2 * WINDOW, LANES), 1) < HEAD_DIM
    kband = jnp.concatenate([_rope(kp_ref[...], tp), _rope(kc_ref[...], tq)], axis=0)
    vband = jnp.concatenate([vp_ref[...], vc_ref[...]], axis=0)
    kd = _dup_head(kband, hk, lo2).astype(MXU_DTYPE)
    vd = _dup_head(vband, hk, lo2).astype(MXU_DTYPE)
    rows, sks = [], []
    for j in range(pairs):
        col = hk * pairs + j
        qp = _rope(q_ref[:, col * LANES:(col + 1) * LANES], tq)
        rows += [jnp.where(lo, qp, 0.0), jnp.where(lo, 0.0, qp)]
        sks += [jnp.full((WINDOW, 1), sink_ref[2 * col], F32), jnp.full((WINDOW, 1), sink_ref[2 * col + 1], F32)]
    qg = jnp.concatenate(rows, axis=0)
    sk = jnp.concatenate(sks, axis=0)
    G = 2 * pairs * WINDOW
    own = lax.broadcasted_iota(jnp.int32, (G, WINDOW), 1) <= (lax.broadcasted_iota(jnp.int32, (G, WINDOW), 0) & (WINDOW - 1))
    s = lax.dot_general(qg.astype(MXU_DTYPE), kd, (((1,), (1,)), ((), ())), preferred_element_type=F32) * (HEAD_DIM ** -0.5)
    s = jnp.where(own, s[:, WINDOW:], s[:, :WINDOW] + jnp.where(n > 0, 0.0, NEG_INF))
    m = jnp.maximum(jnp.max(s, axis=1, keepdims=True), sk)
    e = jnp.exp(s - m)
    es = jnp.exp(sk - m)
    inv = 1.0 / (jnp.sum(e, axis=1, keepdims=True) + es)
    return qg, kd, vd, e * inv, es * inv, own, lo, lo2, tq, tp


def _unfold_band(t, own):
    return jnp.concatenate([jnp.where(own, 0.0, t), jnp.where(own, t, 0.0)], axis=1)


def _attn_specs(D, NB):
    kcol = 3 * D // LANES
    q = pl.BlockSpec((WINDOW, D), lambda n: (n, 2))
    kc = pl.BlockSpec((WINDOW, LANES), lambda n: (n, kcol))
    kp = pl.BlockSpec((WINDOW, LANES), lambda n: (jnp.maximum(n - 1, 0), kcol))
    vc = pl.BlockSpec((WINDOW, LANES), lambda n: (n, kcol + 1))
    vp = pl.BlockSpec((WINDOW, LANES), lambda n: (jnp.maximum(n - 1, 0), kcol + 1))
    tc = pl.BlockSpec((3, WINDOW, LANES), lambda n: (0, n, 0))
    tp = pl.BlockSpec((3, WINDOW, LANES), lambda n: (0, jnp.maximum(n - 1, 0), 0))
    sink = pl.BlockSpec(memory_space=pltpu.SMEM)
    return [sink, q, kp, kc, vp, vc, tc, tp]


def attn_fwd(proj, sinks, tab, D, name):
    S = proj.shape[0]
    NB = S // WINDOW
    pairs = D // HEAD_DIM // N_KV_HEADS // 2

    def body(sink_ref, q_ref, kp_ref, kc_ref, vp_ref, vc_ref, tc_ref, tp_ref, o_ref):
        n = pl.program_id(0)
        for hk in range(N_KV_HEADS):
            _, _, vd, p, _, own, lo, _, _, _ = _attn_common(n, sink_ref, q_ref, kp_ref, kc_ref, vp_ref, vc_ref, tc_ref, tp_ref, hk, pairs)
            o = jnp.dot(_unfold_band(p, own).astype(MXU_DTYPE), vd, preferred_element_type=F32)
            for j in range(pairs):
                col = hk * pairs + j
                oa = o[(2 * j) * WINDOW:(2 * j + 1) * WINDOW]
                ob = o[(2 * j + 1) * WINDOW:(2 * j + 2) * WINDOW]
                o_ref[:, col * LANES:(col + 1) * LANES] = jnp.where(lo, oa, ob)

    return hbm_call(
        body, name=name, grid=(NB,), in_specs=_attn_specs(D, NB),
        out_specs=pl.BlockSpec((WINDOW, D), lambda n: (n, 0)), out_shape=jax.ShapeDtypeStruct((S, D), F32),
        compiler_params=_params(("parallel",)),
    )(sinks, proj, proj, proj, proj, proj, tab, tab)


def attn_bwd(proj, sinks, tab, o, do, D, name):
    S = proj.shape[0]
    NB = S // WINDOW
    pairs = D // HEAD_DIM // N_KV_HEADS // 2

    def body(sink_ref, q_ref, kp_ref, kc_ref, vp_ref, vc_ref, tc_ref, tp_ref, o_ref, do_ref, dq_ref, dk_ref, dv_ref, ds_ref):
        n = pl.program_id(0)

        @pl.when(n == 0)
        def _():
            ds_ref[...] = jnp.zeros_like(ds_ref)

        lane1 = lax.broadcasted_iota(jnp.int32, (1, LANES), 1)
        dsink = jnp.zeros((1, LANES), F32)
        dkt = dvt = None
        for hk in range(N_KV_HEADS):
            qg, kd, vd, p, ps, own, lo, lo2, tq, tp = _attn_common(n, sink_ref, q_ref, kp_ref, kc_ref, vp_ref, vc_ref, tc_ref, tp_ref, hk, pairs)
            dos, os_ = [], []
            for j in range(pairs):
                col = hk * pairs + j
                dop = do_ref[:, col * LANES:(col + 1) * LANES]
                op = o_ref[:, col * LANES:(col + 1) * LANES]
                dos += [jnp.where(lo, dop, 0.0), jnp.where(lo, 0.0, dop)]
                os_ += [jnp.where(lo, op, 0.0), jnp.where(lo, 0.0, op)]
            dog = jnp.concatenate(dos, axis=0)
            og = jnp.concatenate(os_, axis=0)
            dogm = dog.astype(MXU_DTYPE)
            dp = lax.dot_general(dogm, vd, (((1,), (1,)), ((), ())), preferred_element_type=F32)
            dp = jnp.where(own, dp[:, WINDOW:], dp[:, :WINDOW])
            dr = jnp.sum(dog * og, axis=1, keepdims=True)
            ds = _unfold_band(p * (dp - dr) * (HEAD_DIM ** -0.5), own)
            dsm = ds.astype(MXU_DTYPE)
            dqg = jnp.dot(dsm, kd, preferred_element_type=F32)
            dkd = jnp.dot(ds.T.astype(MXU_DTYPE), qg.astype(MXU_DTYPE), preferred_element_type=F32)
            dvd = jnp.dot(_unfold_band(p, own).T.astype(MXU_DTYPE), dogm, preferred_element_type=F32)
            dkf = dkd + pltpu.roll(dkd, HEAD_DIM, 1)
            dvf = dvd + pltpu.roll(dvd, HEAD_DIM, 1)
            if hk == 0:
                dkt, dvt = dkf, dvf
            else:
                dkt, dvt = jnp.where(lo2, dkt, dkf), jnp.where(lo2, dvt, dvf)
            sd = ps * dr
            for j in range(pairs):
                col = hk * pairs + j
                dqa = dqg[(2 * j) * WINDOW:(2 * j + 1) * WINDOW]
                dqb = dqg[(2 * j + 1) * WINDOW:(2 * j + 2) * WINDOW]
                dq_ref[:, col * LANES:(col + 1) * LANES] = _rope_t(jnp.where(lo, dqa, dqb), tq).astype(dq_ref.dtype)
                for t in range(2):
                    part = sd[(2 * j + t) * WINDOW:(2 * j + t + 1) * WINDOW]
                    val = jnp.sum(part, axis=0, keepdims=True)
                    dsink = dsink - jnp.where(lane1 == 2 * col + t, val, 0.0)
        dk_ref[...] = jnp.concatenate([_rope_t(dkt[:WINDOW], tp), _rope_t(dkt[WINDOW:], tq)], axis=0)
        dv_ref[...] = dvt
        ds_ref[...] += dsink

    blk = pl.BlockSpec((WINDOW, D), lambda n: (n, 0))
    band = pl.BlockSpec((None, 2 * WINDOW, LANES), lambda n: (n, 0, 0))
    return hbm_call(
        body, name=name, grid=(NB,), in_specs=_attn_specs(D, NB) + [blk, blk],
        out_specs=[blk, band, band, pl.BlockSpec((1, LANES), lambda n: (0, 0))],
        out_shape=[jax.ShapeDtypeStruct((S, D), MXU_DTYPE), jax.ShapeDtypeStruct((NB, 2 * WINDOW, LANES), F32),
                   jax.ShapeDtypeStruct((NB, 2 * WINDOW, LANES), F32), jax.ShapeDtypeStruct((1, LANES), F32)],
        compiler_params=_params(("arbitrary",)),
    )(sinks, proj, proj, proj, proj, proj, tab, tab, o, do)


def band_fold(dkb, dvb, name):
    NB = dkb.shape[0]
    k4 = dkb.reshape(NB, 2, WINDOW, LANES)
    v4 = dvb.reshape(NB, 2, WINDOW, LANES)

    def body(kc_ref, kn_ref, vc_ref, vn_ref, dk_ref, dv_ref):
        more = pl.program_id(0) < NB - 1
        dk_ref[...] = (kc_ref[...] + jnp.where(more, kn_ref[...], 0.0)).astype(dk_ref.dtype)
        dv_ref[...] = (vc_ref[...] + jnp.where(more, vn_ref[...], 0.0)).astype(dv_ref.dtype)

    cur = pl.BlockSpec((None, None, WINDOW, LANES), lambda n: (n, 1, 0, 0))
    nxt = pl.BlockSpec((None, None, WINDOW, LANES), lambda n: (jnp.minimum(n + 1, NB - 1), 0, 0, 0))
    out = pl.BlockSpec((WINDOW, LANES), lambda n: (n, 0))
    sds = jax.ShapeDtypeStruct((NB * WINDOW, LANES), MXU_DTYPE)
    return hbm_call(body, name=name, grid=(NB,), in_specs=[cur, nxt, cur, nxt], out_specs=[out, out], out_shape=[sds, sds],
                          compiler_params=_params(("parallel",)))(k4, k4, v4, v4)


CROSS_ROWS = 512


def _cross_probs(q, k, scale):
    s = lax.dot_general(q.astype(MXU_DTYPE), k.astype(MXU_DTYPE), (((1,), (1,)), ((), ())), preferred_element_type=F32) * scale
    e = jnp.exp(s - jnp.max(s, axis=1, keepdims=True))
    return e / jnp.sum(e, axis=1, keepdims=True)


def cross_fwd(qc, kv, name):
    S, D = qc.shape
    M = kv.shape[0]
    hd = D // CROSS_HEADS
    tq = min(CROSS_ROWS, S)

    def body(q_ref, kv_ref, o_ref):
        for h in range(CROSS_HEADS):
            p = _cross_probs(q_ref[:, h * hd:(h + 1) * hd], kv_ref[:, h * hd:(h + 1) * hd], hd ** -0.5)
            v = kv_ref[:, D + h * hd:D + (h + 1) * hd].astype(MXU_DTYPE)
            o_ref[:, h * hd:(h + 1) * hd] = jnp.dot(p.astype(MXU_DTYPE), v, preferred_element_type=F32).astype(o_ref.dtype)

    return hbm_call(
        body, name=name, grid=(S // tq,), in_specs=[pl.BlockSpec((tq, D), lambda i: (i, 0)), pl.BlockSpec((M, 2 * D), lambda i: (0, 0))],
        out_specs=pl.BlockSpec((tq, D), lambda i: (i, 0)), out_shape=jax.ShapeDtypeStruct((S, D), MXU_DTYPE),
        compiler_params=_params(("parallel",)),
    )(qc, kv)


def cross_bwd(qc, kv, do, name):
    S, D = qc.shape
    M = kv.shape[0]
    hd = D // CROSS_HEADS
    tq = min(CROSS_ROWS, S)

    def body(q_ref, kv_ref, do_ref, dq_ref, dkv_ref):
        @pl.when(pl.program_id(0) == 0)
        def _():
            dkv_ref[...] = jnp.zeros_like(dkv_ref)

        for h in range(CROSS_HEADS):
            q = q_ref[:, h * hd:(h + 1) * hd]
            k = kv_ref[:, h * hd:(h + 1) * hd]
            v = kv_ref[:, D + h * hd:D + (h + 1) * hd].astype(MXU_DTYPE)
            dom = do_ref[:, h * hd:(h + 1) * hd].astype(MXU_DTYPE)
            p = _cross_probs(q, k, hd ** -0.5)
            dp = lax.dot_general(dom, v, (((1,), (1,)), ((), ())), preferred_element_type=F32)
            ds = p * (dp - jnp.sum(p * dp, axis=1, keepdims=True)) * (hd ** -0.5)
            dq_ref[:, h * hd:(h + 1) * hd] = jnp.dot(ds.astype(MXU_DTYPE), k.astype(MXU_DTYPE),
                                                     preferred_element_type=F32).astype(dq_ref.dtype)
            dkv_ref[:, h * hd:(h + 1) * hd] += jnp.dot(ds.T.astype(MXU_DTYPE), q.astype(MXU_DTYPE), preferred_element_type=F32)
            dkv_ref[:, D + h * hd:D + (h + 1) * hd] += jnp.dot(p.T.astype(MXU_DTYPE), dom, preferred_element_type=F32)

    row = pl.BlockSpec((tq, D), lambda i: (i, 0))
    full = pl.BlockSpec((M, 2 * D), lambda i: (0, 0))
    return hbm_call(
        body, name=name, grid=(S // tq,), in_specs=[row, full, row], out_specs=[row, full],
        out_shape=[jax.ShapeDtypeStruct((S, D), MXU_DTYPE), jax.ShapeDtypeStruct((M, 2 * D), F32)],
        compiler_params=_params(("arbitrary",)),
    )(qc, kv, do)


def adamw(w, g, m, v, name, layers=None, into=None):
    shape = w.shape
    cols = shape[-1]
    lead = shape[0] if len(shape) > 2 else 1
    rows = int(np.prod(shape[:-1])) // lead
    w2, g2, m2, v2 = (t.reshape(lead, rows, cols) for t in (w, g, m, v))
    tr = _divisors(rows, SUBLANES, max(SUBLANES, (1 << 20) // (cols * 4) // SUBLANES * SUBLANES))[0]
    lo, hi = layers or (0, lead)
    done = [t.reshape(lead, rows, cols) for t in into] if into else []

    def body(w_ref, g_ref, m_ref, v_ref, *refs):
        d_ref, mo_ref, vo_ref, go_ref = refs[len(done):]
        gg = g_ref[...]
        mn = ADAM_B1 * m_ref[...] + (1.0 - ADAM_B1) * gg
        vn = ADAM_B2 * v_ref[...] + (1.0 - ADAM_B2) * (gg * gg)
        m_hat = mn / (1.0 - ADAM_B1 ** ADAM_STEP)
        v_hat = vn / (1.0 - ADAM_B2 ** ADAM_STEP)
        d_ref[...] = -ADAM_LR * (m_hat / (jnp.sqrt(v_hat) + ADAM_EPS) + ADAM_WD * w_ref[...])
        mo_ref[...] = mn
        vo_ref[...] = vn
        go_ref[...] = gg

    blk = pl.BlockSpec((None, tr, cols), lambda l, i: (l + lo, i, 0))
    sds = jax.ShapeDtypeStruct((lead, rows, cols), F32)
    d, mn, vn, go = hbm_call(body, name=name, grid=(hi - lo, rows // tr), in_specs=[blk] * 4 + [pl.BlockSpec(memory_space=pl.ANY)] * len(done),
                             out_specs=[blk] * 4, out_shape=[sds] * 4, input_output_aliases={4 + k: k for k in range(len(done))},
                             compiler_params=_params(("parallel", "parallel")))(w2, g2, m2, v2, *done)
    return d.reshape(shape), mn.reshape(shape), vn.reshape(shape), go.reshape(shape)


def sum_devices(parts, name):
    n, rows, cols = parts.shape

    def body(p_ref, o_ref):
        acc = p_ref[0]
        for k in range(1, n):
            acc = acc + p_ref[k]
        o_ref[...] = acc

    return pl.pallas_call(body, name=name, in_specs=[pl.BlockSpec(memory_space=pltpu.VMEM)],
                          out_specs=pl.BlockSpec(memory_space=pltpu.VMEM), out_shape=jax.ShapeDtypeStruct((rows, cols), F32))(parts)


HBM_SPEC = pl.BlockSpec(memory_space=pltpu.HBM)


def _place():
    return lax.axis_index("x"), lax.axis_index("y"), lax.axis_index("c")


def _remote(src, dst, send_sems, recv_sems, k, to):
    return pltpu.make_async_remote_copy(src_ref=src, dst_ref=dst, send_sem=send_sems.at[k], recv_sem=recv_sems.at[k],
                                        device_id=to, device_id_type=MESH_ID)


SEM_SPEC = pl.BlockSpec(memory_space=pltpu.SEMAPHORE)
ANY_SPEC = pl.BlockSpec(memory_space=pl.ANY)
SPLIT_COPY = pltpu.CompilerParams(has_side_effects=pltpu.SideEffectType.DATAFLOW_SIDE_EFFECTING)


def _in_hbm(arrays):
    return [pltpu.with_memory_space_constraint(a, pltpu.HBM) for a in arrays]


def _split_start(copies, sources, lands, after, n_sems, name):
    n = len(sources)

    def body(*refs):
        for cp in copies(refs[:n], refs[n:2 * n], refs[2 * n + 1], refs[2 * n + 2]):
            cp.start()
        refs[-1][...] = jnp.zeros_like(refs[-1])

    through = [pltpu.HBM(a.shape, a.dtype) for a in list(sources) + list(lands)]
    outs = pl.pallas_call(
        body, name=name, in_specs=[HBM_SPEC] * (2 * n) + [ANY_SPEC],
        out_specs=[SEM_SPEC, SEM_SPEC] + [HBM_SPEC] * (2 * n) + [pl.BlockSpec(memory_space=pltpu.VMEM)],
        out_shape=[pltpu.SemaphoreType.DMA((n_sems,)), pltpu.SemaphoreType.DMA((n_sems,))] + through
        + [jax.ShapeDtypeStruct((SUBLANES, LANES), F32)],
        input_output_aliases={i: 2 + i for i in range(2 * n)}, compiler_params=SPLIT_COPY,
    )(*_in_hbm(sources), *_in_hbm(lands), after)
    return outs[0], outs[1], outs[2:2 + n], outs[2 + n:2 + 2 * n], outs[-1]


def _split_wait(copies, send_sems, recv_sems, sources, lands, after, name):
    n = len(sources)

    def body(*refs):
        for cp in copies(refs[:n], refs[n:2 * n], refs[2 * n], refs[2 * n + 1]):
            cp.wait_send()
            cp.wait_recv()

    through = [pltpu.HBM(a.shape, a.dtype) for a in list(sources) + list(lands)]
    outs = pl.pallas_call(
        body, name=name, in_specs=[HBM_SPEC] * (2 * n) + [SEM_SPEC, SEM_SPEC, ANY_SPEC], out_specs=[HBM_SPEC] * (2 * n),
        out_shape=through, input_output_aliases={i: i for i in range(2 * n)}, compiler_params=SPLIT_COPY,
    )(*sources, *lands, send_sems, recv_sems, after)
    return outs[:n], outs[n:]


def _chip_slab(land, slot, rows):
    return land.at[slot, rows] if len(land.shape) == 3 else land.at[rows, slot]


def _gather_copies(w_refs, land_refs, send_sems, recv_sems):
    n = len(w_refs)
    x, y, c = _place()
    chips = [(1 - x, y), (x, 1 - y), (1 - x, 1 - y)]
    cps = []
    for a in range(n):
        hr = w_refs[a].shape[0] // 2
        mine, every = pl.ds(c * hr, hr), pl.ds(0, 2 * hr)
        cps.append(_remote(w_refs[a], _chip_slab(land_refs[a], 2 * x + y, every), send_sems, recv_sems, 3 * n + a, (x, y, 1 - c)))
        for k, chip in enumerate(chips):
            cps.append(_remote(w_refs[a].at[mine], _chip_slab(land_refs[a], 2 * x + y, mine), send_sems, recv_sems, 3 * a + k, (*chip, c)))
    return cps


def gather_start(shards, after, name):
    lands = [lax.empty(s.shape[:-2] + (N_CHIPS,) + s.shape[-2:], s.dtype) for s in shards]
    return _split_start(_gather_copies, shards, lands, after, 4 * len(shards), name)


def gather_wait(state, after, name):
    send_sems, recv_sems, sources, lands, _ = state
    return _split_wait(_gather_copies, send_sems, recv_sems, sources, lands, after, name)[1]


def gather_pass(lands, name):
    n = len(lands)

    def body(*refs):
        out_refs, send_sems, recv_sems = refs[n:2 * n], refs[2 * n], refs[2 * n + 1]
        x, y, c = _place()
        chips = [(1 - x, y), (x, 1 - y), (1 - x, 1 - y)]
        sent = []
        for a in range(n):
            hr = out_refs[a].shape[0 if len(out_refs[a].shape) == 4 else 1] // 2
            for k, (px, py) in enumerate(chips):
                landed = _chip_slab(out_refs[a], 2 * px + py, pl.ds(c * hr, hr))
                sent.append(_remote(landed, landed, send_sems, recv_sems, 3 * a + k, (x, y, 1 - c)))
        for cp in sent:
            cp.start()
        for a in range(n):
            hr = out_refs[a].shape[0 if len(out_refs[a].shape) == 4 else 1] // 2
            for k, (px, py) in enumerate(chips):
                theirs = _chip_slab(out_refs[a], 2 * px + py, pl.ds((1 - c) * hr, hr))
                _remote(theirs, theirs, send_sems, recv_sems, 3 * a + k, (x, y, 1 - c)).wait_recv()
        for cp in sent:
            cp.wait_send()

    return hbm_call(
        body, name=name, in_specs=[HBM_SPEC] * n, out_specs=[HBM_SPEC] * n,
        out_shape=[jax.ShapeDtypeStruct(a.shape, a.dtype) for a in lands], input_output_aliases={a: a for a in range(n)},
        scratch_shapes=[pltpu.SemaphoreType.DMA((3 * n,))] * 2,
    )(*lands)


def _scatter_copies(t_refs, land_refs, send_sems, recv_sems):
    x, y, c = _place()
    chips = [(1 - x, y), (x, 1 - y), (1 - x, 1 - y)]
    return [_remote(t_refs[a].at[:, 2 * px + py], land_refs[a].at[:, k], send_sems, recv_sems, 3 * a + k, (px, py, c))
            for a in range(len(t_refs)) for k, (px, py) in enumerate(chips)]


def scatter_start(parts, after, name):
    lands = [lax.empty((t.shape[0], N_CHIPS - 1) + t.shape[2:], t.dtype) for t in parts]
    return _split_start(_scatter_copies, parts, lands, after, 3 * len(parts), name)


def scatter_wait(state, after, name):
    send_sems, recv_sems, sources, lands, _ = state
    return _split_wait(_scatter_copies, send_sems, recv_sems, sources, lands, after, name)


def swap_sibling(parts, name):
    n = len(parts)

    def body(*refs):
        v_refs, out_refs, send_sems, recv_sems = refs[:n], refs[n:2 * n], refs[2 * n], refs[2 * n + 1]
        x, y, c = _place()
        cps = []
        for a in range(n):
            hr = v_refs[a].shape[2] // 2
            cps.append(_remote(v_refs[a].at[:, :, pl.ds((1 - c) * hr, hr)], out_refs[a], send_sems, recv_sems, a, (x, y, 1 - c)))
        for cp in cps:
            cp.start()
        for cp in cps:
            cp.wait()

    return hbm_call(
        body, name=name, in_specs=[HBM_SPEC] * n, out_specs=[HBM_SPEC] * n,
        out_shape=[jax.ShapeDtypeStruct(v.shape[:2] + (v.shape[2] // 2, v.shape[3]), v.dtype) for v in parts],
        scratch_shapes=[pltpu.SemaphoreType.DMA((n,))] * 2,
    )(*parts)


def join_halves(halves, layer, name):
    n = len(halves)

    def body(*refs):
        out_refs, send_sems, recv_sems = refs[n:2 * n], refs[2 * n], refs[2 * n + 1]
        x, y, c = _place()
        cps = []
        for a in range(n):
            hr = out_refs[a].shape[1] // 2
            mine = out_refs[a].at[layer, pl.ds(c * hr, hr)]
            cps.append(_remote(mine, mine, send_sems, recv_sems, a, (x, y, 1 - c)))
        for cp in cps:
            cp.start()
        for a in range(n):
            hr = out_refs[a].shape[1] // 2
            theirs = out_refs[a].at[layer, pl.ds((1 - c) * hr, hr)]
            _remote(theirs, theirs, send_sems, recv_sems, a, (x, y, 1 - c)).wait_recv()
        for cp in cps:
            cp.wait_send()

    return hbm_call(
        body, name=name, in_specs=[HBM_SPEC] * n, out_specs=[HBM_SPEC] * n,
        out_shape=[jax.ShapeDtypeStruct(f.shape, f.dtype) for f in halves], input_output_aliases={a: a for a in range(n)},
        scratch_shapes=[pltpu.SemaphoreType.DMA((n,))] * 2,
    )(*halves)


def gather_devices(v, name):
    def body(v_ref, out_ref, send_sems, recv_sems, local_sem):
        x, y, c = _place()
        me = 4 * x + 2 * y + c
        own = pltpu.make_async_copy(v_ref, out_ref.at[me], local_sem)
        own.start()
        peers = [((x + dx) % 2, (y + dy) % 2, (c + dc) % 2) for dx in (0, 1) for dy in (0, 1) for dc in (0, 1)][1:]
        sent = []
        for k, peer in enumerate(peers):
            cp = pltpu.make_async_remote_copy(src_ref=v_ref, dst_ref=out_ref.at[me], send_sem=send_sems.at[k], recv_sem=recv_sems.at[k],
                                              device_id=peer, device_id_type=MESH_ID)
            cp.start()
            sent.append(cp)
        for k, (px, py, pc) in enumerate(peers):
            slot = out_ref.at[4 * px + 2 * py + pc]
            pltpu.make_async_remote_copy(src_ref=slot, dst_ref=slot, send_sem=send_sems.at[k], recv_sem=recv_sems.at[k],
                                         device_id=(px, py, pc), device_id_type=MESH_ID).wait_recv()
        for cp in sent:
            cp.wait_send()
        own.wait()

    vm = pl.BlockSpec(memory_space=pltpu.VMEM)
    return pl.pallas_call(body, name=name, in_specs=[vm], out_specs=vm, out_shape=jax.ShapeDtypeStruct((N_DEV,) + v.shape, v.dtype),
                          scratch_shapes=[pltpu.SemaphoreType.DMA((N_DEV - 1,)), pltpu.SemaphoreType.DMA((N_DEV - 1,)),
                                          pltpu.SemaphoreType.DMA])(v)


ADD_ROWS = 512


def add_pair(place, a, b, name):
    L, n, hr, cols = b.shape
    tr = _divisors(hr, 2 * SUBLANES, ADD_ROWS)[0]
    nb = hr // tr

    def body(p_ref, a_ref, b_ref, o_ref):
        del p_ref
        o_ref[...] = (a_ref[...].astype(F32) + b_ref[...].astype(F32)).astype(o_ref.dtype)

    blk = pl.BlockSpec((None, None, tr, cols), lambda l, d, i, p: (l, d, i, 0))
    grid_spec = pltpu.PrefetchScalarGridSpec(
        num_scalar_prefetch=1, grid=(L, n, nb),
        in_specs=[pl.BlockSpec((None, None, tr, cols), lambda l, d, i, p: (l, d, p[0] * nb + i, 0)), blk], out_specs=blk)
    return hbm_call(body, name=name, grid_spec=grid_spec, out_shape=jax.ShapeDtypeStruct(b.shape, b.dtype),
                          compiler_params=_params(("parallel", "parallel", "parallel")))(place, a, b)


def add_chips(place, own, others, layer, stacked, name):
    _, n, hr, cols = others.shape
    tr = _divisors(hr, 2 * SUBLANES, ADD_ROWS)[0]
    nb = hr // tr
    create = isinstance(stacked, tuple)

    def body(p_ref, own_ref, *refs):
        del p_ref
        acc = own_ref[...].astype(F32)
        for k in range(n):
            acc = acc + refs[k][...].astype(F32)
        refs[-1][...] = acc

    ins = [pl.BlockSpec((None, None, tr, cols), lambda i, p: (0, p[1], i, 0))]
    ins += [pl.BlockSpec((None, None, tr, cols), functools.partial(lambda k, i, p: (0, k, i, 0), k)) for k in range(n)]
    grid_spec = pltpu.PrefetchScalarGridSpec(num_scalar_prefetch=1, grid=(nb,), in_specs=ins + ([] if create else [ANY_SPEC]),
                                             out_specs=pl.BlockSpec((None, tr, cols), lambda i, p: (layer, p[0] * nb + i, 0)))
    shape = stacked if create else stacked.shape
    return hbm_call(body, name=name, grid_spec=grid_spec, out_shape=jax.ShapeDtypeStruct(shape, F32),
                          input_output_aliases={} if create else {n + 2: 0},
                          compiler_params=_params(("parallel",)))(place, own, *([others] * n), *([] if create else [stacked]))


def _alpha(depth):
    return (2 * depth) ** 0.25


def _wmm(a, weight, mode, name, deps=(), **more):
    arr, how = weight
    return mm(a, arr, mode, name, deps=deps, **how, **more)


def layer_fwd(h, mem, w, tab, alpha, deps=(), late=None):
    D = h.shape[1]
    proj = _wmm(h, w["w_in"], "nt", "mm_proj", deps)
    xc, r, ig, a, b = rg_gates_fwd(proj, w["conv_w"], w["conv_b"], w["w_rg"], w["b_rg"], w["w_ig"], w["b_ig"], w["lru_lambda"], "rg_gates_fwd")
    hs, y_rnn = rg_scan_fwd(proj, a, b, "rg_scan_fwd")
    y_attn = attn_fwd(proj, w["sinks"], tab, D, "attn_fwd")
    deps = ()
    if late is not None:
        rest, deps = late(y_attn)
        w = {**w, **rest}
    pr = _wmm(y_rnn, w["w_br_rnn"], "nn", "mm_br_rnn", deps)
    pa = _wmm(y_attn, w["w_br_attn"], "nn", "mm_br_attn")
    merged = merge_fwd(proj, pr, pa, "merge_fwd")
    h1, xh1, rs1 = _wmm(merged, w["w_out"], "nn", "mm_out_ln1", post_norm=(h, w["ln1_g"], w["ln1_b"], alpha))
    qc = _wmm(h1, w["cq_w"], "nn", "mm_cq", out_dtype=MXU_DTYPE)
    kv = _wmm(mem, w["ckv_w"], "nn", "mm_ckv", out_dtype=MXU_DTYPE)
    o = cross_fwd(qc, kv, "cross_fwd")
    h2, xh2, rs2 = _wmm(o, w["co_w"], "nn", "mm_co_ln2", post_norm=(h1, w["ln2_g"], w["ln2_b"], alpha))
    gu = _wmm(h2, w["ffn_wi"], "nn", "mm_ffn_wi", out_blocks=2)
    act = swiglu_fwd(gu, "swiglu_fwd")
    h3, xh3, rs3 = _wmm(act, w["ffn_wo"], "nn", "mm_ffn_wo_ln3", post_norm=(h2, w["ln3_g"], w["ln3_b"], alpha))
    saved = dict(h=h, proj=proj, xc=xc, r=r, ig=ig, a=a, hs=hs, y_rnn=y_rnn, y_attn=y_attn, pr=pr, pa=pa, xh1=xh1, rs1=rs1, h1=h1,
                 qc=qc, kv=kv, o=o, xh2=xh2, rs2=rs2, h2=h2, gu=gu, xh3=xh3, rs3=rs3)
    return h3, saved, w


def layer_bwd(dh, mem, w, s, tab, alpha, deps=(), halfway=None):
    D = dh.shape[1]
    g = {}
    wg = dict(out_dtype=MXU_DTYPE)
    dz3, g["ln3_g"], g["ln3_b"] = ln_bwd(dh, None, s["xh3"], s["rs3"], w["ln3_g"], 1.0, "ln3_bwd")
    act = swiglu_fwd(s["gu"], "swiglu_refwd")
    g["ffn_wo"] = mm(act, dz3, "tn", "mm_d_ffn_wo", deps=deps, **wg)
    dact = _wmm(dz3, w["ffn_wo"], "nt", "mm_dact")
    dgu = swiglu_bwd(s["gu"], dact, "swiglu_bwd")
    g["ffn_wi"] = mm(s["h2"], dgu, "tn", "mm_d_ffn_wi", b_blocks=2, out_blocks=N_CHIPS, **wg)
    dh2 = _wmm(dgu, w["ffn_wi"], "nt", "mm_dh2", a_blocks=2)
    dz2, g["ln2_g"], g["ln2_b"] = ln_bwd(dz3, dh2, s["xh2"], s["rs2"], w["ln2_g"], alpha, "ln2_bwd")
    g["co_w"] = mm(s["o"], dz2, "tn", "mm_d_co", **wg)
    do = _wmm(dz2, w["co_w"], "nt", "mm_do", out_dtype=MXU_DTYPE)
    dqc, dkv = cross_bwd(s["qc"], s["kv"], do, "cross_bwd")
    g["cq_w"] = mm(s["h1"], dqc, "tn", "mm_d_cq", **wg)
    g["ckv_w"] = mm(mem, dkv, "tn", "mm_d_ckv", out_blocks=N_CHIPS, **wg)
    dh1 = _wmm(dqc, w["cq_w"], "nt", "mm_dh1")
    deps = halfway(g, dh1) if halfway is not None else ()
    dz1, g["ln1_g"], g["ln1_b"] = ln_bwd(dz2, dh1, s["xh1"], s["rs1"], w["ln1_g"], alpha, "ln1_bwd")
    merged = merge_fwd(s["proj"], s["pr"], s["pa"], "merge_refwd")
    g["w_out"] = mm(merged, dz1, "tn", "mm_d_out", deps=deps, **wg)
    dm = _wmm(dz1, w["w_out"], "nt", "mm_dmerged")
    dpr, dpa, dg_rnn, dg_attn = merge_bwd(s["proj"], s["pr"], s["pa"], dm, "merge_bwd")
    g["w_br_rnn"] = mm(s["y_rnn"], dpr, "tn", "mm_d_br_rnn", **wg)
    g["w_br_attn"] = mm(s["y_attn"], dpa, "tn", "mm_d_br_attn", **wg)
    dy_rnn = _wmm(dpr, w["w_br_rnn"], "nt", "mm_dy_rnn")
    dy_attn = _wmm(dpa, w["w_br_attn"], "nt", "mm_dy_attn")
    dq, dkb, dvb, dsink = attn_bwd(s["proj"], w["sinks"], tab, s["y_attn"], dy_attn, D, "attn_bwd")
    dk, dv = band_fold(dkb, dvb, "band_fold")
    g["sinks"] = dsink[:, :w["sinks"].shape[0]]
    dgr, gt = rg_scan_bwd(s["proj"], dy_rnn, s["hs"], s["a"], "rg_scan_bwd")
    dxc, g["w_rg"], g["w_ig"], g["b_rg"], g["b_ig"], g["lru_lambda"] = rg_gates_bwd(
        gt, s["hs"], s["xc"], s["r"], s["ig"], w["w_rg"], w["w_ig"], w["lru_lambda"], "rg_gates_bwd")
    dxr, g["conv_w"], g["conv_b"] = rg_conv_bwd(s["proj"], dxc, w["conv_w"], "rg_conv_bwd")
    dproj = jnp.concatenate([dxr, dgr, dq, dk, dv, dg_rnn, dg_attn], axis=1)
    g["w_in"] = mm(dproj, s["h"], "tn", "mm_d_in", **wg)
    dhm = _wmm(dproj, w["w_in"], "nn", "mm_dh")
    return axpby(dz1, dhm, alpha, "layer_dx"), g


def local_step(x, mem, target, depth, weights_of, grads_halfway, grads_done):
    alpha = _alpha(depth)
    tab = rope_table(x.shape[0])
    h, saved, layers = x, [], []
    for l in range(depth):
        wl, deps, late = weights_of(l, h)
        h, s, wl = layer_fwd(h, mem, wl, tab, alpha, deps, late)
        layers.append(wl)
        saved.append(s)
    dh, loss = loss_head(h, target, "loss_head")
    deps = ()
    for l in reversed(range(depth)):
        dh, g = layer_bwd(dh, mem, layers[l], saved[l], tab, alpha, deps, grads_halfway(l))
        deps = grads_done(l, g, dh)
    return loss, dh


def _pad_rows(flat):
    n = flat.shape[0]
    rows = -(-n // (LANES * SUBLANES)) * SUBLANES
    return jnp.pad(flat, (0, rows * LANES - n)).reshape(rows, LANES)


def kernel(x, mem, w_in, conv_w, conv_b, w_rg, b_rg, w_ig, b_ig, lru_lambda, w_br_rnn, w_br_attn, sinks, w_out, ln1_g, ln1_b, cq_w, ckv_w, co_w, ln2_g, ln2_b, ffn_wi, ffn_wo, ln3_g, ln3_b, loss_target, m_w_in, m_conv_w, m_conv_b, m_w_rg, m_b_rg, m_w_ig, m_b_ig, m_lru_lambda, m_w_br_rnn, m_w_br_attn, m_sinks, m_w_out, m_ln1_g, m_ln1_b, m_cq_w, m_ckv_w, m_co_w, m_ln2_g, m_ln2_b, m_ffn_wi, m_ffn_wo, m_ln3_g, m_ln3_b, v_w_in, v_conv_w, v_conv_b, v_w_rg, v_b_rg, v_w_ig, v_b_ig, v_lru_lambda, v_w_br_rnn, v_w_br_attn, v_sinks, v_w_out, v_ln1_g, v_ln1_b, v_cq_w, v_ckv_w, v_co_w, v_ln2_g, v_ln2_b, v_ffn_wi, v_ffn_wo, v_ln3_g, v_ln3_b):
    args = dict(locals())
    w = {n: args[n] for n in WEIGHTS}
    m = {n: args["m_" + n] for n in WEIGHTS}
    v = {n: args["v_" + n] for n in WEIGHTS}
    for group in (w, m, v):
        group["w_in"] = jnp.swapaxes(group["w_in"], 1, 2)
    cx, cy, cc = _place()
    chip = 2 * cx + cy
    L = w_in.shape[0]

    place = jnp.stack([cc, chip]).astype(jnp.int32)
    cw_rows = _pad_rows(conv_w.reshape(-1))
    cw_all = gather_devices(cw_rows, "gather_conv_w")[0::2]
    cw_parts = cw_all.reshape(N_CHIPS, -1)[:, :conv_w.size].reshape((N_CHIPS,) + conv_w.shape)
    conv_full = jnp.concatenate([cw_parts[k] for k in range(N_CHIPS)], axis=2)

    shards = [{n: w[n][l].astype(MXU_DTYPE) for n in BIG} for l in range(L)]
    late_names = tuple(n for n in BIG if n not in GATHER_FIRST)
    gathering = {(0, GATHER_FIRST): gather_start([shards[0][n] for n in GATHER_FIRST], cw_rows, "gather_start_0a")}
    gathering[0, late_names] = gather_start([shards[0][n] for n in late_names], gathering[0, GATHER_FIRST][4], "gather_start_0b")

    def gathered(l, names, after, tag):
        lands = gather_pass(gather_wait(gathering.pop((l, names)), after, f"gather_wait_{tag}"), f"gather_pass_{tag}")
        wl = {}
        for n, gw in zip(names, lands):
            rows_joined = gw.reshape(gw.shape[:-3] + (-1, gw.shape[-1]))
            if n in COL_BLOCKED:
                wl[n] = (gw, dict(b_blocks=N_CHIPS))
            elif n in GATE_WEIGHTS:
                wl[n] = rows_joined
            else:
                wl[n] = (rows_joined, {})
        return wl, lands

    def start_layer(l, after):
        if l >= L:
            return ()
        gathering[l, BIG] = gather_start([shards[l][n] for n in BIG], after, f"gather_start_{l}")
        return (gathering[l, BIG][4],)

    def weights_of(l, h):
        deps, late = (), None
        if l == 0:
            wl, _ = gathered(0, GATHER_FIRST, h, "0a")

            def late(after):
                rest, lands = gathered(0, late_names, after, "0b")
                return rest, start_layer(1, lands[0])
        else:
            wl, lands = gathered(l, BIG, h, str(l))
            deps = start_layer(l + 1, lands[0])
        for n in SMALL:
            wl[n] = conv_full[l] if n == "conv_w" else w[n][l] if n == "sinks" else w[n][l][None, :]
        return wl, deps, late

    def for_chips(n, g):
        if n in COL_BLOCKED:
            return g
        if n in GATE_WEIGHTS:
            nb, bw, _ = g.shape
            g = g.reshape(nb, N_CHIPS, bw // N_CHIPS, bw).transpose(1, 0, 2, 3).reshape(N_CHIPS, nb * bw // N_CHIPS, bw)
        else:
            g = g.reshape(N_CHIPS, g.shape[0] // N_CHIPS, g.shape[1])
        return g.astype(MXU_DTYPE)

    reduced, scattering, small_grads = {}, {}, [None] * L
    late_grads = tuple(n for n in BIG if n not in SCATTER_FIRST)

    def start_scatter(l, names, g, after, tag):
        partial_sums = [for_chips(n, g[n])[None] for n in names]
        from_sibling = swap_sibling(partial_sums, f"grad_to_sibling_{tag}")
        chip_sums = [add_pair(place, a, b, f"grad_add_pair_{n}_{l}") for n, a, b in zip(names, partial_sums, from_sibling)]
        scattering[l, names] = scatter_start(chip_sums, after, f"grad_scatter_start_{tag}")
        return (scattering[l, names][4],)

    def finish_layer(l, after):
        for names in [k[1] for k in list(scattering) if k[0] == l]:
            tag = str(l) if names == BIG else f"{l}{'a' if names == SCATTER_FIRST else 'b'}"
            chip_sums, from_chips = scatter_wait(scattering.pop((l, names)), after, f"grad_scatter_wait_{tag}")
            for n, own, others in zip(names, chip_sums, from_chips):
                target = reduced.get(n, (L, 2 * own.shape[2], own.shape[3]))
                reduced[n] = add_chips(place, own, others, l, target, f"grad_add_chips_{n}_{l}")
        reduced.update(zip(BIG, join_halves([reduced[n] for n in BIG], l, f"grad_join_{l}")))

    def grads_halfway(l):
        if l > 0:
            return None

        def halfway(g, after):
            return start_scatter(0, SCATTER_FIRST, g, after, "0a")

        return halfway

    def grads_done(l, g, dh):
        small_grads[l] = {n: g[n] for n in SMALL}
        deps = start_scatter(l, late_grads if l == 0 else BIG, g, dh, "0b" if l == 0 else str(l))
        if l + 1 < L:
            finish_layer(l + 1, dh)
        return deps

    loss11, dx = local_step(x[0], mem[0], loss_target[0], L, weights_of, grads_halfway, grads_done)
    loss = lax.psum(loss11[0, 0], ("x", "y", "c"))
    updated, after = {}, dx
    if L > 1:
        for n in BIG:
            updated[n] = adamw(w[n], reduced[n].reshape(w[n].shape), m[n], v[n], f"adamw_{n}_upper", layers=(1, L))
        after = jnp.stack([updated[n][0][(0,) * w[n].ndim] for n in BIG])
    finish_layer(0, after)
    gshard = {n: reduced[n].reshape(w[n].shape) for n in BIG}

    small_full = {n: jnp.stack([gl[n] for gl in small_grads]).reshape(w[n].shape[:1] + ((CONV_WIDTH, -1) if n == "conv_w" else (-1,)))
                  for n in SMALL}
    small_flat = jnp.concatenate([small_full[n].reshape(-1) for n in SMALL])
    small_sum = sum_devices(gather_devices(_pad_rows(small_flat), "gather_small_grads"), "sum_small_grads").reshape(-1)
    off = 0
    for n in SMALL:
        gfull = small_sum[off:off + small_full[n].size].reshape(small_full[n].shape)
        off += small_full[n].size
        if n == "conv_w":
            width = conv_w.shape[2]
            gfull = lax.dynamic_slice_in_dim(gfull, chip * width, width, axis=2)
        gshard[n] = gfull

    delta, new_m, new_v, grad = {}, {}, {}, {}
    for n in WEIGHTS:
        some = dict(layers=(0, 1), into=updated[n]) if n in updated else {}
        delta[n], new_m[n], new_v[n], grad[n] = adamw(w[n], gshard[n], m[n], v[n], "adamw_" + n, **some)
    for group in (delta, new_m, new_v, grad):
        group["w_in"] = jnp.swapaxes(group["w_in"], 1, 2)
    return (loss, dx[None], *[grad[n] for n in WEIGHTS], *[delta[n] for n in WEIGHTS], *[new_m[n] for n in WEIGHTS],
            *[new_v[n] for n in WEIGHTS])
```

```python
import functools
import math

import jax
import jax.numpy as jnp
import numpy as np
from jax import lax
from jax.experimental import pallas as pl
from jax.experimental.pallas import tpu as pltpu

F32 = jnp.float32
BF16 = jnp.bfloat16
MXU_DTYPE = BF16

HEAD_DIM = 64
N_KV_HEADS = 2
WINDOW = 128
ROT_DIM = HEAD_DIM // 4
ROPE_THETA = 500000.0
CROSS_HEADS = 4
RNN_BLOCKS = 4
CONV_WIDTH = 4
LRU_C = 8.0
LN_EPS = 1e-5
NEG_INF = -1e30
ADAM_LR = 0.001
ADAM_B1 = 0.9
ADAM_B2 = 0.999
ADAM_EPS = 1e-08
ADAM_WD = 0.01
ADAM_STEP = 10

VMEM_BYTES_V7X = 64 * 1024 * 1024
VMEM_BLOCK_BUDGET = 36 * 1024 * 1024
LANES = 128
SUBLANES = 8

MESH_ID = pl.DeviceIdType.MESH
N_CHIPS = 4
N_DEV = 8

BIG = ("w_in", "w_rg", "w_ig", "w_br_rnn", "w_br_attn", "w_out", "cq_w", "ckv_w", "co_w", "ffn_wi", "ffn_wo")
SHARD_AXIS = {"w_in": 0, "w_rg": 1, "w_ig": 1, "w_br_rnn": 0, "w_br_attn": 0, "w_out": 0, "cq_w": 0, "ckv_w": 1,
              "co_w": 0, "ffn_wi": 1, "ffn_wo": 0}
SMALL = ("conv_w", "conv_b", "b_rg", "b_ig", "lru_lambda", "sinks", "ln1_g", "ln1_b", "ln2_g", "ln2_b", "ln3_g", "ln3_b")
WEIGHTS = ("w_in", "conv_w", "conv_b", "w_rg", "b_rg", "w_ig", "b_ig", "lru_lambda", "w_br_rnn", "w_br_attn", "sinks",
           "w_out", "ln1_g", "ln1_b", "cq_w", "ckv_w", "co_w", "ln2_g", "ln2_b", "ffn_wi", "ffn_wo", "ln3_g", "ln3_b")
GATE_WEIGHTS = ("w_rg", "w_ig")
COL_BLOCKED = ("ckv_w", "ffn_wi")
GATHER_FIRST = ("w_in", "w_rg", "w_ig")
SCATTER_FIRST = ("ffn_wo", "ffn_wi", "co_w", "cq_w", "ckv_w")


def _params(dims=None, vmem=None):
    return pltpu.CompilerParams(dimension_semantics=dims, vmem_limit_bytes=vmem)


def _vmem_limit(block_bytes, temp_bytes=0):
    want = int(2 * block_bytes + temp_bytes) + (6 << 20)
    return max(32 << 20, min(want, VMEM_BYTES_V7X - (6 << 20)))


def _divisors(n, align, cap):
    out = [d for d in range(align, min(n, cap) + 1, align) if n % d == 0]
    if n <= cap and n not in out:
        out.append(n)
    return sorted(out, reverse=True) or [n]


PIN_MIN_ELEMENTS = 1 << 18


def hbm_call(body, **kw):
    def in_hbm(s):
        return pltpu.HBM(s.shape, s.dtype) if math.prod(s.shape) >= PIN_MIN_ELEMENTS else s

    shapes = kw.pop("out_shape")
    shapes = [in_hbm(s) for s in shapes] if isinstance(shapes, (list, tuple)) else in_hbm(shapes)
    call = pl.pallas_call(body, out_shape=shapes, **kw)

    def run(*args):
        return call(*[pltpu.with_memory_space_constraint(a, pltpu.HBM) if a.size >= PIN_MIN_ELEMENTS else a for a in args])

    return run


def _sigmoid(x):
    return 1.0 / (1.0 + jnp.exp(-x))


def _gelu_parts(x):
    c = math.sqrt(2.0 / math.pi)
    u = c * (x + 0.044715 * x * x * x)
    t = jnp.tanh(u)
    return t, c * (1.0 + 3 * 0.044715 * x * x)


def _gelu(x):
    t, _ = _gelu_parts(x)
    return 0.5 * x * (1.0 + t)


def _gelu_grad(x):
    t, du = _gelu_parts(x)
    return 0.5 * (1.0 + t) + 0.5 * x * (1.0 - t * t) * du


def _neg_expm1(x):
    series = x * (1.0 + x * (0.5 + x * (1.0 / 6 + x * (1.0 / 24 + x * (1.0 / 120)))))
    return -jnp.where(x > -0.1, series, jnp.exp(x) - 1.0)


def _softplus_neg(lam):
    x = -lam
    return jnp.maximum(x, 0.0) + jnp.log1p(jnp.exp(-jnp.abs(x)))


STEP_US = 0.35
HBM_BYTES_PER_US = 2.5e6
MXU_FLOPS_PER_US = 7e8


def _layer_norm(z, g, b):
    mu = jnp.mean(z, axis=-1, keepdims=True)
    zc = z - mu
    rs = lax.rsqrt(jnp.mean(zc * zc, axis=-1, keepdims=True) + LN_EPS)
    xh = zc * rs
    return xh * g + b, xh, rs


def mm(a, b, mode, name, *, b_index=(), a_blocks=0, b_blocks=0, out_blocks=0, out_dtype=F32, deps=(), post_norm=None):
    nlead = len(b_index) + (1 if b_blocks else 0)
    bk, bn = b.shape[nlead:]
    M, K = (a.shape[-1], a.shape[-2]) if mode == "tn" else (a.shape[-2], a.shape[-1] * max(a_blocks, 1))
    N = bk if mode == "nt" else bn * max(b_blocks, 1) if mode == "nn" or mode == "tn" else bn
    asz, bsz, osz = a.dtype.itemsize, b.dtype.itemsize, jnp.dtype(out_dtype).itemsize
    n_unit = math.gcd(N // max(out_blocks, 1), N // max(b_blocks, 1) if mode != "nt" else N)
    k_unit = math.gcd(K // max(a_blocks, 1), K // max(b_blocks, 1) if mode == "nt" else K)
    tms = _divisors(M, LANES if mode == "tn" else SUBLANES, 2048)
    tns = [N] if post_norm else _divisors(n_unit, LANES, 2048)
    tks = _divisors(k_unit, LANES, k_unit)
    best = None
    for tm in tms:
        for tn in tns:
            for tk in tks:
                nk = K // tk
                scratch = tm * tn * 4 if (nk > 1 and osz != 4) else 0
                blocks = tm * tk * asz + tn * tk * bsz + tm * tn * osz * (3 if post_norm else 1)
                temps = tm * tk * (2 + (4 if mode == "tn" else 0)) + tn * tk * 2 + tm * tn * 4 + scratch
                if 2 * blocks + temps > VMEM_BLOCK_BUDGET + (8 << 20):
                    continue
                ni, nj = M // tm, N // tn
                traffic = M * K * asz * (nj if nk > 1 else 1) + N * K * bsz * (1 if nj * nk == 1 else ni) + M * N * osz
                busy = max(traffic / HBM_BYTES_PER_US, 2.0 * M * N * K / MXU_FLOPS_PER_US)
                cost = ni * nj * nk * STEP_US + busy + blocks / HBM_BYTES_PER_US
                if best is None or cost < best[0]:
                    best = (cost, tm, tn, tk, blocks, temps)
    _, tm, tn, tk, blocks, temps = best
    nk = K // tk
    use_scratch = nk > 1 and osz != 4

    def split(index, total, blocks, tile):
        per = total // blocks // tile
        return index // per, index % per

    def body(a_ref, b_ref, *rest):
        rest = rest[len(deps):]
        if post_norm:
            h_ref, g_ref, beta_ref, o_ref, xh_ref, rs_ref = rest[:6]
            acc = rest[6:]
        else:
            o_ref, acc = rest[0], rest[1:]
        av = a_ref[...].astype(MXU_DTYPE)
        bv = b_ref[...].astype(MXU_DTYPE)
        dn = {"nn": (((1,), (0,)), ((), ())), "nt": (((1,), (1,)), ((), ())), "tn": (((0,), (0,)), ((), ()))}[mode]
        r = lax.dot_general(av, bv, dn, preferred_element_type=F32)

        def normalise(f):
            o_ref[...], xh_ref[...], rs_ref[...] = _layer_norm(post_norm[3] * h_ref[...] + f, g_ref[...], beta_ref[...])

        if nk == 1 and post_norm:
            normalise(r)
        elif nk == 1:
            o_ref[...] = r.astype(o_ref.dtype)
        else:
            acc_ref = acc[0] if use_scratch else o_ref

            @pl.when(pl.program_id(2) == 0)
            def _():
                acc_ref[...] = r

            @pl.when(pl.program_id(2) > 0)
            def _():
                acc_ref[...] += r

            if use_scratch:
                @pl.when(pl.program_id(2) == nk - 1)
                def _():
                    o_ref[...] = acc_ref[...].astype(o_ref.dtype)
            elif post_norm:
                @pl.when(pl.program_id(2) == nk - 1)
                def _():
                    normalise(o_ref[...])

    if mode == "tn":
        a_spec = pl.BlockSpec((tk, tm), lambda i, j, k: (k, i))
    elif a_blocks:
        a_spec = pl.BlockSpec((None, tm, tk), lambda i, j, k: (split(k, K, a_blocks, tk)[0], i, split(k, K, a_blocks, tk)[1]))
    else:
        a_spec = pl.BlockSpec((tm, tk), lambda i, j, k: (i, k))
    lead = (None,) * nlead
    if mode == "nt":
        bmap = ((lambda i, j, k: b_index + (split(k, K, b_blocks, tk)[0], j, split(k, K, b_blocks, tk)[1])) if b_blocks
                else (lambda i, j, k: b_index + (j, k)))
        b_spec = pl.BlockSpec(lead + (tn, tk), bmap)
    else:
        bmap = ((lambda i, j, k: b_index + (split(j, N, b_blocks, tn)[0], k, split(j, N, b_blocks, tn)[1])) if b_blocks
                else (lambda i, j, k: b_index + (k, j)))
        b_spec = pl.BlockSpec(lead + (tk, tn), bmap)
    if out_blocks:
        o_spec = pl.BlockSpec((None, tm, tn), lambda i, j, k: (split(j, N, out_blocks, tn)[0], i, split(j, N, out_blocks, tn)[1]))
        o_shape = jax.ShapeDtypeStruct((out_blocks, M, N // out_blocks), out_dtype)
    else:
        o_spec = pl.BlockSpec((tm, tn), lambda i, j, k: (i, j))
        o_shape = jax.ShapeDtypeStruct((M, N), out_dtype)
    in_specs, extra = [a_spec, b_spec] + [pl.BlockSpec(memory_space=pl.ANY)] * len(deps), ()
    if post_norm:
        vec = pl.BlockSpec((1, N), lambda i, j, k: (0, 0))
        in_specs += [pl.BlockSpec((tm, N), lambda i, j, k: (i, 0)), vec, vec]
        o_spec = [o_spec, pl.BlockSpec((tm, N), lambda i, j, k: (i, 0)), pl.BlockSpec((tm, 1), lambda i, j, k: (i, 0))]
        o_shape = [o_shape, jax.ShapeDtypeStruct((M, N), F32), jax.ShapeDtypeStruct((M, 1), F32)]
        extra = post_norm[:3]
    return hbm_call(
        body, name=name, grid=(M // tm, N // tn, nk), in_specs=in_specs, out_specs=o_spec, out_shape=o_shape,
        scratch_shapes=[pltpu.VMEM((tm, tn), F32)] if use_scratch else [],
        compiler_params=_params(("parallel", "parallel", "arbitrary"), _vmem_limit(blocks, temps)),
    )(a, b, *deps, *extra)


ROW_TILE = 512
GATE_ROWS = 1024


def ln_bwd(dy_a, dy_b, xh, rs, g, c1, name):
    S, D = xh.shape
    tr = min(ROW_TILE, S)
    two = dy_b is not None

    def body(*refs):
        if two:
            a_ref, b_ref, xh_ref, rs_ref, g_ref, dz_ref, dg_ref, db_ref = refs
            dy = c1 * a_ref[...] + b_ref[...]
        else:
            a_ref, xh_ref, rs_ref, g_ref, dz_ref, dg_ref, db_ref = refs
            dy = a_ref[...]
        x = xh_ref[...]
        dyg = dy * g_ref[...]
        m1 = jnp.mean(dyg, axis=-1, keepdims=True)
        m2 = jnp.mean(dyg * x, axis=-1, keepdims=True)
        dz_ref[...] = rs_ref[...] * (dyg - m1 - x * m2)

        @pl.when(pl.program_id(0) == 0)
        def _():
            dg_ref[...] = jnp.zeros_like(dg_ref)
            db_ref[...] = jnp.zeros_like(db_ref)

        dg_ref[...] += jnp.sum(dy * x, axis=0, keepdims=True)
        db_ref[...] += jnp.sum(dy, axis=0, keepdims=True)

    row = pl.BlockSpec((tr, D), lambda i: (i, 0))
    vec = pl.BlockSpec((1, D), lambda i: (0, 0))
    ins = [row, row] if two else [row]
    args = (dy_a, dy_b) if two else (dy_a,)
    return hbm_call(
        body, name=name, grid=(S // tr,), in_specs=ins + [row, pl.BlockSpec((tr, 1), lambda i: (i, 0)), vec],
        out_specs=[row, vec, vec],
        out_shape=[jax.ShapeDtypeStruct((S, D), F32), jax.ShapeDtypeStruct((1, D), F32), jax.ShapeDtypeStruct((1, D), F32)],
        compiler_params=_params(("arbitrary",), 48 << 20),
    )(*args, xh, rs, g)


def axpby(a, b, c1, name):
    S, D = a.shape
    tr = min(ROW_TILE, S)

    def body(a_ref, b_ref, o_ref):
        o_ref[...] = c1 * a_ref[...] + b_ref[...]

    row = pl.BlockSpec((tr, D), lambda i: (i, 0))
    return hbm_call(body, name=name, grid=(S // tr,), in_specs=[row, row], out_specs=row,
                          out_shape=jax.ShapeDtypeStruct((S, D), F32), compiler_params=_params(("parallel",)))(a, b)


def loss_head(y, t, name):
    S, D = y.shape
    tr = min(ROW_TILE, S)
    nsteps = S // tr

    def body(y_ref, t_ref, dy_ref, l_ref, acc_ref):
        i = pl.program_id(0)

        @pl.when(i == 0)
        def _():
            acc_ref[...] = jnp.zeros_like(acc_ref)

        e = y_ref[...] - t_ref[...]
        dy_ref[...] = e * (1.0 / D)
        acc_ref[...] += jnp.sum(e * e, axis=0, keepdims=True)

        @pl.when(i == nsteps - 1)
        def _():
            l_ref[...] = jnp.sum(acc_ref[...], axis=1, keepdims=True) * (0.5 / D)

    row = pl.BlockSpec((tr, D), lambda i: (i, 0))
    return hbm_call(
        body, name=name, grid=(nsteps,), in_specs=[row, row],
        out_specs=[row, pl.BlockSpec((1, 1), lambda i: (0, 0))],
        out_shape=[jax.ShapeDtypeStruct((S, D), F32), jax.ShapeDtypeStruct((1, 1), F32)],
        scratch_shapes=[pltpu.VMEM((1, D), F32)], compiler_params=_params(("arbitrary",)),
    )(y, t)


SWIGLU_ROWS = 256


def swiglu_fwd(gu, name):
    _, S, Fh = gu.shape
    tc = _divisors(Fh, LANES, 1536)[0]
    tr = min(SWIGLU_ROWS, S)

    def body(gu_ref, o_ref):
        g = gu_ref[0]
        o_ref[...] = (g * _sigmoid(g) * gu_ref[1]).astype(o_ref.dtype)

    return hbm_call(
        body, name=name, grid=(S // tr, Fh // tc), in_specs=[pl.BlockSpec((2, tr, tc), lambda i, j: (0, i, j))],
        out_specs=pl.BlockSpec((tr, tc), lambda i, j: (i, j)), out_shape=jax.ShapeDtypeStruct((S, Fh), MXU_DTYPE),
        compiler_params=_params(("parallel", "parallel")),
    )(gu)


def swiglu_bwd(gu, dact, name):
    _, S, Fh = gu.shape
    tc = _divisors(Fh, LANES, 1536)[0]
    tr = min(SWIGLU_ROWS, S)

    def body(gu_ref, d_ref, o_ref):
        g, u, d = gu_ref[0], gu_ref[1], d_ref[...]
        s = _sigmoid(g)
        o_ref[0] = (d * u * (s * (1.0 + g * (1.0 - s)))).astype(o_ref.dtype)
        o_ref[1] = (d * (g * s)).astype(o_ref.dtype)

    both = pl.BlockSpec((2, tr, tc), lambda i, j: (0, i, j))
    return hbm_call(
        body, name=name, grid=(S // tr, Fh // tc), in_specs=[both, pl.BlockSpec((tr, tc), lambda i, j: (i, j))],
        out_specs=both, out_shape=jax.ShapeDtypeStruct((2, S, Fh), MXU_DTYPE), compiler_params=_params(("parallel", "parallel")),
    )(gu, dact)


GATE_COLS = 256


def merge_fwd(proj, pr, pa, name):
    S, D = pr.shape
    tr = min(GATE_ROWS, S)
    c0 = (3 * D + 2 * N_KV_HEADS * HEAD_DIM) // GATE_COLS
    c1 = c0 + D // GATE_COLS

    def body(gr_ref, ga_ref, pr_ref, pa_ref, o_ref):
        o_ref[...] = (_sigmoid(gr_ref[...]) * pr_ref[...] + _sigmoid(ga_ref[...]) * pa_ref[...]).astype(o_ref.dtype)

    blk = pl.BlockSpec((tr, GATE_COLS), lambda i, j: (i, j))
    return hbm_call(
        body, name=name, grid=(S // tr, D // GATE_COLS),
        in_specs=[pl.BlockSpec((tr, GATE_COLS), lambda i, j: (i, c0 + j)), pl.BlockSpec((tr, GATE_COLS), lambda i, j: (i, c1 + j)),
                  blk, blk],
        out_specs=blk, out_shape=jax.ShapeDtypeStruct((S, D), MXU_DTYPE), compiler_params=_params(("parallel", "parallel")),
    )(proj, proj, pr, pa)


def merge_bwd(proj, pr, pa, dm, name):
    S, D = pr.shape
    tr = min(GATE_ROWS, S)
    c0 = (3 * D + 2 * N_KV_HEADS * HEAD_DIM) // GATE_COLS
    c1 = c0 + D // GATE_COLS

    def body(gr_ref, ga_ref, pr_ref, pa_ref, dm_ref, dpr_ref, dpa_ref, dgr_ref, dga_ref):
        sr, sa, d = _sigmoid(gr_ref[...]), _sigmoid(ga_ref[...]), dm_ref[...]
        dpr_ref[...] = (d * sr).astype(dpr_ref.dtype)
        dpa_ref[...] = (d * sa).astype(dpa_ref.dtype)
        dgr_ref[...] = (d * pr_ref[...] * (sr * (1.0 - sr))).astype(dgr_ref.dtype)
        dga_ref[...] = (d * pa_ref[...] * (sa * (1.0 - sa))).astype(dga_ref.dtype)

    blk = pl.BlockSpec((tr, GATE_COLS), lambda i, j: (i, j))
    sds = jax.ShapeDtypeStruct((S, D), MXU_DTYPE)
    return hbm_call(
        body, name=name, grid=(S // tr, D // GATE_COLS),
        in_specs=[pl.BlockSpec((tr, GATE_COLS), lambda i, j: (i, c0 + j)), pl.BlockSpec((tr, GATE_COLS), lambda i, j: (i, c1 + j)),
                  blk, blk, blk],
        out_specs=[blk, blk, blk, blk], out_shape=[sds, sds, sds, sds], compiler_params=_params(("parallel", "parallel")),
    )(proj, proj, pr, pa, dm)


RG_ROWS = 512


def _shift_down(cur, prev, d, row, first):
    halo = jnp.where(first, 0.0, pltpu.roll(prev, d, 0))
    return jnp.where(row >= d, pltpu.roll(cur, d, 0), halo)


def _shift_up(cur, nxt, d, row, last, tr):
    halo = jnp.where(last, 0.0, pltpu.roll(nxt, tr - d, 0))
    return jnp.where(row < tr - d, pltpu.roll(cur, tr - d, 0), halo)


def _lru_coeffs(r, lam):
    sp = _softplus_neg(lam)
    la = -LRU_C * r * sp
    return sp, la, jnp.exp(la), _neg_expm1(2.0 * la)


def rg_gates_fwd(proj, conv_w, conv_b, w_rg, b_rg, w_ig, b_ig, lam, name):
    S = proj.shape[0]
    nblk, bw, _ = w_rg.shape
    D = nblk * bw
    tr = min(RG_ROWS, S)

    def body(xr_ref, xp_ref, cw_ref, cb_ref, wr_ref, br_ref, wi_ref, bi_ref, lam_ref, xc_ref, r_ref, i_ref, a_ref, b_ref):
        first = pl.program_id(1) == 0
        cur, prev = xr_ref[...], xp_ref[...]
        row = lax.broadcasted_iota(jnp.int32, cur.shape, 0)
        xc = cb_ref[...]
        for k in range(CONV_WIDTH - 1):
            xc = xc + _shift_down(cur, prev, CONV_WIDTH - 1 - k, row, first) * cw_ref[k:k + 1, :]
        xc = xc + cur * cw_ref[CONV_WIDTH - 1:CONV_WIDTH, :]
        xm = xc.astype(MXU_DTYPE)
        r = _sigmoid(jnp.dot(xm, wr_ref[...].astype(MXU_DTYPE), preferred_element_type=F32) + br_ref[...])
        ig = _sigmoid(jnp.dot(xm, wi_ref[...].astype(MXU_DTYPE), preferred_element_type=F32) + bi_ref[...])
        _, _, a, em = _lru_coeffs(r, lam_ref[...])
        xc_ref[...] = xc
        r_ref[...] = r
        i_ref[...] = ig
        a_ref[...] = a
        b_ref[...] = jnp.sqrt(em) * (ig * xc)

    tile = pl.BlockSpec((tr, bw), lambda n, i: (i, n))
    vec = pl.BlockSpec((1, bw), lambda n, i: (0, n))
    wblk = pl.BlockSpec((None, bw, bw), lambda n, i: (n, 0, 0))
    sds = jax.ShapeDtypeStruct((S, D), F32)
    return hbm_call(
        body, name=name, grid=(nblk, S // tr),
        in_specs=[tile, pl.BlockSpec((tr, bw), lambda n, i: (jnp.maximum(i - 1, 0), n)),
                  pl.BlockSpec((CONV_WIDTH, bw), lambda n, i: (0, n)), vec, wblk, vec, wblk, vec, vec],
        out_specs=[tile] * 5, out_shape=[sds] * 5, compiler_params=_params(("parallel", "parallel")),
    )(proj, proj, conv_w, conv_b, w_rg, b_rg, w_ig, b_ig, lam)


SCAN_COLS = 256
CHUNK = SUBLANES
SCAN_UNROLL = 4


def rg_scan_fwd(proj, a, b, name):
    S, D = a.shape
    cb = min(SCAN_COLS, D)
    goff = D // cb

    def body(a_ref, b_ref, g_ref, hs_ref, y_ref):
        row = lax.broadcasted_iota(jnp.int32, (CHUNK, cb), 0)

        def step(c, carry):
            r0 = pl.multiple_of(c * CHUNK, CHUNK)
            A = a_ref[pl.ds(r0, CHUNK), :]
            B = b_ref[pl.ds(r0, CHUNK), :]
            for d in (1, 2, 4):
                As = jnp.where(row >= d, pltpu.roll(A, d, 0), 1.0)
                Bs = jnp.where(row >= d, pltpu.roll(B, d, 0), 0.0)
                B = A * Bs + B
                A = A * As
            hs_ref[pl.ds(r0, CHUNK), :] = B + A * carry
            a_end = jnp.sum(jnp.where(row == CHUNK - 1, A, 0.0), axis=0, keepdims=True)
            b_end = jnp.sum(jnp.where(row == CHUNK - 1, B, 0.0), axis=0, keepdims=True)
            return b_end + a_end * carry

        lax.fori_loop(0, S // CHUNK, step, jnp.zeros((1, cb), F32), unroll=SCAN_UNROLL)
        y_ref[...] = (hs_ref[...] * _gelu(g_ref[...])).astype(y_ref.dtype)

    col = pl.BlockSpec((S, cb), lambda j: (0, j))
    return hbm_call(
        body, name=name, grid=(D // cb,), in_specs=[col, col, pl.BlockSpec((S, cb), lambda j: (0, goff + j))],
        out_specs=[col, col], out_shape=[jax.ShapeDtypeStruct((S, D), F32), jax.ShapeDtypeStruct((S, D), MXU_DTYPE)],
        compiler_params=_params(("parallel",), _vmem_limit(5 * S * cb * 4, 4 * S * cb * 4)),
    )(a, b, proj)


def rg_scan_bwd(proj, dy, hs, a, name):
    S, D = a.shape
    cb = min(SCAN_COLS, D)
    goff = D // cb
    nchunks = S // CHUNK

    def body(g_ref, dy_ref, hs_ref, a_ref, dg_ref, gt_ref):
        gate, dy = g_ref[...], dy_ref[...]
        dg_ref[...] = (dy * hs_ref[...] * _gelu_grad(gate)).astype(dg_ref.dtype)
        gt_ref[...] = dy * _gelu(gate)
        row = lax.broadcasted_iota(jnp.int32, (CHUNK, cb), 0)

        def step(k, carry):
            c = nchunks - 1 - k
            r0 = pl.multiple_of(c * CHUNK, CHUNK)
            rn = pl.multiple_of(jnp.minimum(c + 1, nchunks - 1) * CHUNK, CHUNK)
            last = c == nchunks - 1
            nxt = jnp.where(last, 0.0, pltpu.roll(a_ref[pl.ds(rn, CHUNK), :], CHUNK - 1, 0))
            A = jnp.where(row < CHUNK - 1, pltpu.roll(a_ref[pl.ds(r0, CHUNK), :], CHUNK - 1, 0), nxt)
            B = gt_ref[pl.ds(r0, CHUNK), :]
            for d in (1, 2, 4):
                As = jnp.where(row < CHUNK - d, pltpu.roll(A, CHUNK - d, 0), 1.0)
                Bs = jnp.where(row < CHUNK - d, pltpu.roll(B, CHUNK - d, 0), 0.0)
                B = A * Bs + B
                A = A * As
            gt_ref[pl.ds(r0, CHUNK), :] = B + A * carry
            a_end = jnp.sum(jnp.where(row == 0, A, 0.0), axis=0, keepdims=True)
            b_end = jnp.sum(jnp.where(row == 0, B, 0.0), axis=0, keepdims=True)
            return b_end + a_end * carry

        lax.fori_loop(0, nchunks, step, jnp.zeros((1, cb), F32), unroll=SCAN_UNROLL)

    col = pl.BlockSpec((S, cb), lambda j: (0, j))
    return hbm_call(
        body, name=name, grid=(D // cb,), in_specs=[pl.BlockSpec((S, cb), lambda j: (0, goff + j)), col, col, col],
        out_specs=[col, col], out_shape=[jax.ShapeDtypeStruct((S, D), MXU_DTYPE), jax.ShapeDtypeStruct((S, D), F32)],
        compiler_params=_params(("parallel",), _vmem_limit(6 * S * cb * 4, 6 * S * cb * 4)),
    )(proj, dy, hs, a)


def rg_gates_bwd(gt, hs, xc, r, ig, w_rg, w_ig, lam, name):
    S, D = xc.shape
    nblk, bw, _ = w_rg.shape
    tr = min(RG_ROWS, S)

    def body(gt_ref, hs_ref, hp_ref, xc_ref, r_ref, i_ref, wr_ref, wi_ref, lam_ref,
             dxc_ref, dwr_ref, dwi_ref, dbr_ref, dbi_ref, dl_ref):
        step = pl.program_id(1)
        g, hs, xc, r, ig, lam = gt_ref[...], hs_ref[...], xc_ref[...], r_ref[...], i_ref[...], lam_ref[...]
        row = lax.broadcasted_iota(jnp.int32, g.shape, 0)
        hprev = _shift_down(hs, hp_ref[...], 1, row, step == 0)
        sp, _, a, em = _lru_coeffs(r, lam)
        mult = jnp.sqrt(em)
        du = g * mult
        dla = g * hprev * a - (g * (ig * xc)) * (a * a) / mult
        dpr = (dla * (-LRU_C * sp)) * (r * (1.0 - r))
        dpi = (du * xc) * (ig * (1.0 - ig))
        dprm, dpim = dpr.astype(MXU_DTYPE), dpi.astype(MXU_DTYPE)
        nt = (((1,), (1,)), ((), ()))
        dxc_ref[...] = (du * ig + lax.dot_general(dprm, wr_ref[...].astype(MXU_DTYPE), nt, preferred_element_type=F32)
                        + lax.dot_general(dpim, wi_ref[...].astype(MXU_DTYPE), nt, preferred_element_type=F32))

        @pl.when(step == 0)
        def _():
            for ref in (dwr_ref, dwi_ref, dbr_ref, dbi_ref, dl_ref):
                ref[...] = jnp.zeros_like(ref)

        xct = xc.T.astype(MXU_DTYPE)
        dwr_ref[...] += jnp.dot(xct, dprm, preferred_element_type=F32)
        dwi_ref[...] += jnp.dot(xct, dpim, preferred_element_type=F32)
        dbr_ref[...] += jnp.sum(dpr, axis=0, keepdims=True)
        dbi_ref[...] += jnp.sum(dpi, axis=0, keepdims=True)
        dl_ref[...] += jnp.sum(dla * (-LRU_C * r), axis=0, keepdims=True) * (-_sigmoid(-lam))

    tile = pl.BlockSpec((tr, bw), lambda n, i: (i, n))
    vec = pl.BlockSpec((1, bw), lambda n, i: (0, n))
    wblk = pl.BlockSpec((None, bw, bw), lambda n, i: (n, 0, 0))
    return hbm_call(
        body, name=name, grid=(nblk, S // tr),
        in_specs=[tile, tile, pl.BlockSpec((tr, bw), lambda n, i: (jnp.maximum(i - 1, 0), n)), tile, tile, tile, wblk, wblk, vec],
        out_specs=[tile, wblk, wblk, vec, vec, vec],
        out_shape=[jax.ShapeDtypeStruct((S, D), F32), jax.ShapeDtypeStruct((nblk, bw, bw), F32), jax.ShapeDtypeStruct((nblk, bw, bw), F32),
                   jax.ShapeDtypeStruct((1, D), F32), jax.ShapeDtypeStruct((1, D), F32), jax.ShapeDtypeStruct((1, D), F32)],
        compiler_params=_params(("parallel", "arbitrary")),
    )(gt, hs, hs, xc, r, ig, w_rg, w_ig, lam)


def rg_conv_bwd(proj, dxc, conv_w, name):
    S, D = dxc.shape
    bw = min(SCAN_COLS, D)
    tr = min(RG_ROWS, S)
    nsteps = S // tr

    def body(d_ref, dn_ref, xr_ref, xp_ref, cw_ref, dxr_ref, dcw_ref, dcb_ref):
        step = pl.program_id(1)
        d, xr = d_ref[...], xr_ref[...]
        row = lax.broadcasted_iota(jnp.int32, d.shape, 0)
        dxr = d * cw_ref[CONV_WIDTH - 1:CONV_WIDTH, :]
        for k in range(CONV_WIDTH - 1):
            dxr = dxr + _shift_up(d, dn_ref[...], CONV_WIDTH - 1 - k, row, step == nsteps - 1, tr) * cw_ref[k:k + 1, :]
        dxr_ref[...] = dxr.astype(dxr_ref.dtype)

        @pl.when(step == 0)
        def _():
            dcw_ref[...] = jnp.zeros_like(dcw_ref)
            dcb_ref[...] = jnp.zeros_like(dcb_ref)

        for k in range(CONV_WIDTH - 1):
            xs = _shift_down(xr, xp_ref[...], CONV_WIDTH - 1 - k, row, step == 0)
            dcw_ref[k:k + 1, :] += jnp.sum(d * xs, axis=0, keepdims=True)
        dcw_ref[CONV_WIDTH - 1:CONV_WIDTH, :] += jnp.sum(d * xr, axis=0, keepdims=True)
        dcb_ref[...] += jnp.sum(d, axis=0, keepdims=True)

    tile = pl.BlockSpec((tr, bw), lambda n, i: (i, n))
    cwb = pl.BlockSpec((CONV_WIDTH, bw), lambda n, i: (0, n))
    return hbm_call(
        body, name=name, grid=(D // bw, nsteps),
        in_specs=[tile, pl.BlockSpec((tr, bw), lambda n, i: (jnp.minimum(i + 1, nsteps - 1), n)), tile,
                  pl.BlockSpec((tr, bw), lambda n, i: (jnp.maximum(i - 1, 0), n)), cwb],
        out_specs=[tile, cwb, pl.BlockSpec((1, bw), lambda n, i: (0, n))],
        out_shape=[jax.ShapeDtypeStruct((S, D), MXU_DTYPE), jax.ShapeDtypeStruct((CONV_WIDTH, D), F32), jax.ShapeDtypeStruct((1, D), F32)],
        compiler_params=_params(("parallel", "arbitrary")),
    )(dxc, dxc, proj, proj, conv_w)


def rope_table(S):
    half = ROT_DIM // 2
    pos = jnp.arange(S, dtype=F32)
    inv = ROPE_THETA ** (-jnp.arange(0, ROT_DIM, 2, dtype=F32) / ROT_DIM)
    ang = pos[:, None] * inv[None, :]
    cos, sin = jnp.cos(ang), jnp.sin(ang)
    zero = jnp.zeros((S, HEAD_DIM - ROT_DIM), F32)
    c = jnp.concatenate([cos, cos, zero + 1.0], axis=1)
    a = jnp.concatenate([-sin, jnp.zeros((S, half), F32), zero], axis=1)
    b = jnp.concatenate([jnp.zeros((S, half), F32), sin, zero], axis=1)
    return jnp.stack([jnp.tile(t, (1, LANES // HEAD_DIM)) for t in (c, a, b)])


def _rope(t, tab):
    half = ROT_DIM // 2
    return t * tab[0] + pltpu.roll(t, LANES - half, 1) * tab[1] + pltpu.roll(t, half, 1) * tab[2]


def _rope_t(d, tab):
    half = ROT_DIM // 2
    return d * tab[0] + pltpu.roll(d * tab[1], half, 1) + pltpu.roll(d * tab[2], LANES - half, 1)


def _dup_head(t, hk, lo):
    sw = pltpu.roll(t, HEAD_DIM, 1)
    return jnp.where(lo, t, sw) if hk == 0 else jnp.where(lo, sw, t)


def _attn_common(n, sink_ref, q_ref, kp_ref, kc_ref, vp_ref, vc_ref, tc_ref, tp_ref, hk, pairs):
    tq = (tc_ref[0], tc_ref[1], tc_ref[2])
    tp = (tp_ref[0], tp_ref[1], tp_ref[2])
    lo = lax.broadcasted_iota(jnp.int32, (WINDOW, LANES), 1) < HEAD_DIM
    lo2 = lax.broadcasted_iota(jnp.int32, (2 * WINDOW, LANES), 1) < HEAD_DIM
    kband = jnp.concatenate([_rope(kp_ref[...], tp), _rope(kc_ref[...], tq)], axis=0)
    vband = jnp.concatenate([vp_ref[...], vc_ref[...]], axis=0)
    kd = _dup_head(kband, hk, lo2).astype(MXU_DTYPE)
    vd = _dup_head(vband, hk, lo2).astype(MXU_DTYPE)
    rows, sks = [], []
    for j in range(pairs):
        col = hk * pairs + j
        qp = _rope(q_ref[:, col * LANES:(col + 1) * LANES], tq)
        rows += [jnp.where(lo, qp, 0.0), jnp.where(lo, 0.0, qp)]
        sks += [jnp.full((WINDOW, 1), sink_ref[2 * col], F32), jnp.full((WINDOW, 1), sink_ref[2 * col + 1], F32)]
    qg = jnp.concatenate(rows, axis=0)
    sk = jnp.concatenate(sks, axis=0)
    G = 2 * pairs * WINDOW
    own = lax.broadcasted_iota(jnp.int32, (G, WINDOW), 1) <= (lax.broadcasted_iota(jnp.int32, (G, WINDOW), 0) & (WINDOW - 1))
    s = lax.dot_general(qg.astype(MXU_DTYPE), kd, (((1,), (1,)), ((), ())), preferred_element_type=F32) * (HEAD_DIM ** -0.5)
    s = jnp.where(own, s[:, WINDOW:], s[:, :WINDOW] + jnp.where(n > 0, 0.0, NEG_INF))
    m = jnp.maximum(jnp.max(s, axis=1, keepdims=True), sk)
    e = jnp.exp(s - m)
    es = jnp.exp(sk - m)
    inv = 1.0 / (jnp.sum(e, axis=1, keepdims=True) + es)
    return qg, kd, vd, e * inv, es * inv, own, lo, lo2, tq, tp


def _unfold_band(t, own):
    return jnp.concatenate([jnp.where(own, 0.0, t), jnp.where(own, t, 0.0)], axis=1)


def _attn_specs(D, NB):
    kcol = 3 * D // LANES
    q = pl.BlockSpec((WINDOW, D), lambda n: (n, 2))
    kc = pl.BlockSpec((WINDOW, LANES), lambda n: (n, kcol))
    kp = pl.BlockSpec((WINDOW, LANES), lambda n: (jnp.maximum(n - 1, 0), kcol))
    vc = pl.BlockSpec((WINDOW, LANES), lambda n: (n, kcol + 1))
    vp = pl.BlockSpec((WINDOW, LANES), lambda n: (jnp.maximum(n - 1, 0), kcol + 1))
    tc = pl.BlockSpec((3, WINDOW, LANES), lambda n: (0, n, 0))
    tp = pl.BlockSpec((3, WINDOW, LANES), lambda n: (0, jnp.maximum(n - 1, 0), 0))
    sink = pl.BlockSpec(memory_space=pltpu.SMEM)
    return [sink, q, kp, kc, vp, vc, tc, tp]


def attn_fwd(proj, sinks, tab, D, name):
    S = proj.shape[0]
    NB = S // WINDOW
    pairs = D // HEAD_DIM // N_KV_HEADS // 2

    def body(sink_ref, q_ref, kp_ref, kc_ref, vp_ref, vc_ref, tc_ref, tp_ref, o_ref):
        n = pl.program_id(0)
        for hk in range(N_KV_HEADS):
            _, _, vd, p, _, own, lo, _, _, _ = _attn_common(n, sink_ref, q_ref, kp_ref, kc_ref, vp_ref, vc_ref, tc_ref, tp_ref, hk, pairs)
            o = jnp.dot(_unfold_band(p, own).astype(MXU_DTYPE), vd, preferred_element_type=F32)
            for j in range(pairs):
                col = hk * pairs + j
                oa = o[(2 * j) * WINDOW:(2 * j + 1) * WINDOW]
                ob = o[(2 * j + 1) * WINDOW:(2 * j + 2) * WINDOW]
                o_ref[:, col * LANES:(col + 1) * LANES] = jnp.where(lo, oa, ob)

    return hbm_call(
        body, name=name, grid=(NB,), in_specs=_attn_specs(D, NB),
        out_specs=pl.BlockSpec((WINDOW, D), lambda n: (n, 0)), out_shape=jax.ShapeDtypeStruct((S, D), F32),
        compiler_params=_params(("parallel",)),
    )(sinks, proj, proj, proj, proj, proj, tab, tab)


def attn_bwd(proj, sinks, tab, o, do, D, name):
    S = proj.shape[0]
    NB = S // WINDOW
    pairs = D // HEAD_DIM // N_KV_HEADS // 2

    def body(sink_ref, q_ref, kp_ref, kc_ref, vp_ref, vc_ref, tc_ref, tp_ref, o_ref, do_ref, dq_ref, dk_ref, dv_ref, ds_ref):
        n = pl.program_id(0)

        @pl.when(n == 0)
        def _():
            ds_ref[...] = jnp.zeros_like(ds_ref)

        lane1 = lax.broadcasted_iota(jnp.int32, (1, LANES), 1)
        dsink = jnp.zeros((1, LANES), F32)
        dkt = dvt = None
        for hk in range(N_KV_HEADS):
            qg, kd, vd, p, ps, own, lo, lo2, tq, tp = _attn_common(n, sink_ref, q_ref, kp_ref, kc_ref, vp_ref, vc_ref, tc_ref, tp_ref, hk, pairs)
            dos, os_ = [], []
            for j in range(pairs):
                col = hk * pairs + j
                dop = do_ref[:, col * LANES:(col + 1) * LANES]
                op = o_ref[:, col * LANES:(col + 1) * LANES]
                dos += [jnp.where(lo, dop, 0.0), jnp.where(lo, 0.0, dop)]
                os_ += [jnp.where(lo, op, 0.0), jnp.where(lo, 0.0, op)]
            dog = jnp.concatenate(dos, axis=0)
            og = jnp.concatenate(os_, axis=0)
            dogm = dog.astype(MXU_DTYPE)
            dp = lax.dot_general(dogm, vd, (((1,), (1,)), ((), ())), preferred_element_type=F32)
            dp = jnp.where(own, dp[:, WINDOW:], dp[:, :WINDOW])
            dr = jnp.sum(dog * og, axis=1, keepdims=True)
            ds = _unfold_band(p * (dp - dr) * (HEAD_DIM ** -0.5), own)
            dsm = ds.astype(MXU_DTYPE)
            dqg = jnp.dot(dsm, kd, preferred_element_type=F32)
            dkd = jnp.dot(ds.T.astype(MXU_DTYPE), qg.astype(MXU_DTYPE), preferred_element_type=F32)
            dvd = jnp.dot(_unfold_band(p, own).T.astype(MXU_DTYPE), dogm, preferred_element_type=F32)
            dkf = dkd + pltpu.roll(dkd, HEAD_DIM, 1)
            dvf = dvd + pltpu.roll(dvd, HEAD_DIM, 1)
            if hk == 0:
                dkt, dvt = dkf, dvf
            else:
                dkt, dvt = jnp.where(lo2, dkt, dkf), jnp.where(lo2, dvt, dvf)
            sd = ps * dr
            for j in range(pairs):
                col = hk * pairs + j
                dqa = dqg[(2 * j) * WINDOW:(2 * j + 1) * WINDOW]
                dqb = dqg[(2 * j + 1) * WINDOW:(2 * j + 2) * WINDOW]
                dq_ref[:, col * LANES:(col + 1) * LANES] = _rope_t(jnp.where(lo, dqa, dqb), tq).astype(dq_ref.dtype)
                for t in range(2):
                    part = sd[(2 * j + t) * WINDOW:(2 * j + t + 1) * WINDOW]
                    val = jnp.sum(part, axis=0, keepdims=True)
                    dsink = dsink - jnp.where(lane1 == 2 * col + t, val, 0.0)
        dk_ref[...] = jnp.concatenate([_rope_t(dkt[:WINDOW], tp), _rope_t(dkt[WINDOW:], tq)], axis=0)
        dv_ref[...] = dvt
        ds_ref[...] += dsink

    blk = pl.BlockSpec((WINDOW, D), lambda n: (n, 0))
    band = pl.BlockSpec((None, 2 * WINDOW, LANES), lambda n: (n, 0, 0))
    return hbm_call(
        body, name=name, grid=(NB,), in_specs=_attn_specs(D, NB) + [blk, blk],
        out_specs=[blk, band, band, pl.BlockSpec((1, LANES), lambda n: (0, 0))],
        out_shape=[jax.ShapeDtypeStruct((S, D), MXU_DTYPE), jax.ShapeDtypeStruct((NB, 2 * WINDOW, LANES), F32),
                   jax.ShapeDtypeStruct((NB, 2 * WINDOW, LANES), F32), jax.ShapeDtypeStruct((1, LANES), F32)],
        compiler_params=_params(("arbitrary",)),
    )(sinks, proj, proj, proj, proj, proj, tab, tab, o, do)


def band_fold(dkb, dvb, name):
    NB = dkb.shape[0]
    k4 = dkb.reshape(NB, 2, WINDOW, LANES)
    v4 = dvb.reshape(NB, 2, WINDOW, LANES)

    def body(kc_ref, kn_ref, vc_ref, vn_ref, dk_ref, dv_ref):
        more = pl.program_id(0) < NB - 1
        dk_ref[...] = (kc_ref[...] + jnp.where(more, kn_ref[...], 0.0)).astype(dk_ref.dtype)
        dv_ref[...] = (vc_ref[...] + jnp.where(more, vn_ref[...], 0.0)).astype(dv_ref.dtype)

    cur = pl.BlockSpec((None, None, WINDOW, LANES), lambda n: (n, 1, 0, 0))
    nxt = pl.BlockSpec((None, None, WINDOW, LANES), lambda n: (jnp.minimum(n + 1, NB - 1), 0, 0, 0))
    out = pl.BlockSpec((WINDOW, LANES), lambda n: (n, 0))
    sds = jax.ShapeDtypeStruct((NB * WINDOW, LANES), MXU_DTYPE)
    return hbm_call(body, name=name, grid=(NB,), in_specs=[cur, nxt, cur, nxt], out_specs=[out, out], out_shape=[sds, sds],
                          compiler_params=_params(("parallel",)))(k4, k4, v4, v4)


CROSS_ROWS = 512


def _cross_probs(q, k, scale):
    s = lax.dot_general(q.astype(MXU_DTYPE), k.astype(MXU_DTYPE), (((1,), (1,)), ((), ())), preferred_element_type=F32) * scale
    e = jnp.exp(s - jnp.max(s, axis=1, keepdims=True))
    return e / jnp.sum(e, axis=1, keepdims=True)


def cross_fwd(qc, kv, name):
    S, D = qc.shape
    M = kv.shape[0]
    hd = D // CROSS_HEADS
    tq = min(CROSS_ROWS, S)

    def body(q_ref, kv_ref, o_ref):
        for h in range(CROSS_HEADS):
            p = _cross_probs(q_ref[:, h * hd:(h + 1) * hd], kv_ref[:, h * hd:(h + 1) * hd], hd ** -0.5)
            v = kv_ref[:, D + h * hd:D + (h + 1) * hd].astype(MXU_DTYPE)
            o_ref[:, h * hd:(h + 1) * hd] = jnp.dot(p.astype(MXU_DTYPE), v, preferred_element_type=F32).astype(o_ref.dtype)

    return hbm_call(
        body, name=name, grid=(S // tq,), in_specs=[pl.BlockSpec((tq, D), lambda i: (i, 0)), pl.BlockSpec((M, 2 * D), lambda i: (0, 0))],
        out_specs=pl.BlockSpec((tq, D), lambda i: (i, 0)), out_shape=jax.ShapeDtypeStruct((S, D), MXU_DTYPE),
        compiler_params=_params(("parallel",)),
    )(qc, kv)


def cross_bwd(qc, kv, do, name):
    S, D = qc.shape
    M = kv.shape[0]
    hd = D // CROSS_HEADS
    tq = min(CROSS_ROWS, S)

    def body(q_ref, kv_ref, do_ref, dq_ref, dkv_ref):
        @pl.when(pl.program_id(0) == 0)
        def _():
            dkv_ref[...] = jnp.zeros_like(dkv_ref)

        for h in range(CROSS_HEADS):
            q = q_ref[:, h * hd:(h + 1) * hd]
            k = kv_ref[:, h * hd:(h + 1) * hd]
            v = kv_ref[:, D + h * hd:D + (h + 1) * hd].astype(MXU_DTYPE)
            dom = do_ref[:, h * hd:(h + 1) * hd].astype(MXU_DTYPE)
            p = _cross_probs(q, k, hd ** -0.5)
            dp = lax.dot_general(dom, v, (((1,), (1,)), ((), ())), preferred_element_type=F32)
            ds = p * (dp - jnp.sum(p * dp, axis=1, keepdims=True)) * (hd ** -0.5)
            dq_ref[:, h * hd:(h + 1) * hd] = jnp.dot(ds.astype(MXU_DTYPE), k.astype(MXU_DTYPE),
                                                     preferred_element_type=F32).astype(dq_ref.dtype)
            dkv_ref[:, h * hd:(h + 1) * hd] += jnp.dot(ds.T.astype(MXU_DTYPE), q.astype(MXU_DTYPE), preferred_element_type=F32)
            dkv_ref[:, D + h * hd:D + (h + 1) * hd] += jnp.dot(p.T.astype(MXU_DTYPE), dom, preferred_element_type=F32)

    row = pl.BlockSpec((tq, D), lambda i: (i, 0))
    full = pl.BlockSpec((M, 2 * D), lambda i: (0, 0))
    return hbm_call(
        body, name=name, grid=(S // tq,), in_specs=[row, full, row], out_specs=[row, full],
        out_shape=[jax.ShapeDtypeStruct((S, D), MXU_DTYPE), jax.ShapeDtypeStruct((M, 2 * D), F32)],
        compiler_params=_params(("arbitrary",)),
    )(qc, kv, do)


def adamw(w, g, m, v, name, layers=None, into=None):
    shape = w.shape
    cols = shape[-1]
    lead = shape[0] if len(shape) > 2 else 1
    rows = int(np.prod(shape[:-1])) // lead
    w2, g2, m2, v2 = (t.reshape(lead, rows, cols) for t in (w, g, m, v))
    tr = _divisors(rows, SUBLANES, max(SUBLANES, (1 << 20) // (cols * 4) // SUBLANES * SUBLANES))[0]
    lo, hi = layers or (0, lead)
    done = [t.reshape(lead, rows, cols) for t in into] if into else []

    def body(w_ref, g_ref, m_ref, v_ref, *refs):
        d_ref, mo_ref, vo_ref, go_ref = refs[len(done):]
        gg = g_ref[...]
        mn = ADAM_B1 * m_ref[...] + (1.0 - ADAM_B1) * gg
        vn = ADAM_B2 * v_ref[...] + (1.0 - ADAM_B2) * (gg * gg)
        m_hat = mn / (1.0 - ADAM_B1 ** ADAM_STEP)
        v_hat = vn / (1.0 - ADAM_B2 ** ADAM_STEP)
        d_ref[...] = -ADAM_LR * (m_hat / (jnp.sqrt(v_hat) + ADAM_EPS) + ADAM_WD * w_ref[...])
        mo_ref[...] = mn
        vo_ref[...] = vn
        go_ref[...] = gg

    blk = pl.BlockSpec((None, tr, cols), lambda l, i: (l + lo, i, 0))
    sds = jax.ShapeDtypeStruct((lead, rows, cols), F32)
    d, mn, vn, go = hbm_call(body, name=name, grid=(hi - lo, rows // tr), in_specs=[blk] * 4 + [pl.BlockSpec(memory_space=pl.ANY)] * len(done),
                             out_specs=[blk] * 4, out_shape=[sds] * 4, input_output_aliases={4 + k: k for k in range(len(done))},
                             compiler_params=_params(("parallel", "parallel")))(w2, g2, m2, v2, *done)
    return d.reshape(shape), mn.reshape(shape), vn.reshape(shape), go.reshape(shape)


def sum_devices(parts, name):
    n, rows, cols = parts.shape

    def body(p_ref, o_ref):
        acc = p_ref[0]
        for k in range(1, n):
            acc = acc + p_ref[k]
        o_ref[...] = acc

    return pl.pallas_call(body, name=name, in_specs=[pl.BlockSpec(memory_space=pltpu.VMEM)],
                          out_specs=pl.BlockSpec(memory_space=pltpu.VMEM), out_shape=jax.ShapeDtypeStruct((rows, cols), F32))(parts)


HBM_SPEC = pl.BlockSpec(memory_space=pltpu.HBM)


def _place():
    return lax.axis_index("x"), lax.axis_index("y"), lax.axis_index("c")


def _remote(src, dst, send_sems, recv_sems, k, to):
    return pltpu.make_async_remote_copy(src_ref=src, dst_ref=dst, send_sem=send_sems.at[k], recv_sem=recv_sems.at[k],
                                        device_id=to, device_id_type=MESH_ID)


SEM_SPEC = pl.BlockSpec(memory_space=pltpu.SEMAPHORE)
ANY_SPEC = pl.BlockSpec(memory_space=pl.ANY)
SPLIT_COPY = pltpu.CompilerParams(has_side_effects=pltpu.SideEffectType.DATAFLOW_SIDE_EFFECTING)


def _in_hbm(arrays):
    return [pltpu.with_memory_space_constraint(a, pltpu.HBM) for a in arrays]


def _split_start(copies, sources, lands, after, n_sems, name):
    n = len(sources)

    def body(*refs):
        for cp in copies(refs[:n], refs[n:2 * n], refs[2 * n + 1], refs[2 * n + 2]):
            cp.start()
        refs[-1][...] = jnp.zeros_like(refs[-1])

    through = [pltpu.HBM(a.shape, a.dtype) for a in list(sources) + list(lands)]
    outs = pl.pallas_call(
        body, name=name, in_specs=[HBM_SPEC] * (2 * n) + [ANY_SPEC],
        out_specs=[SEM_SPEC, SEM_SPEC] + [HBM_SPEC] * (2 * n) + [pl.BlockSpec(memory_space=pltpu.VMEM)],
        out_shape=[pltpu.SemaphoreType.DMA((n_sems,)), pltpu.SemaphoreType.DMA((n_sems,))] + through
        + [jax.ShapeDtypeStruct((SUBLANES, LANES), F32)],
        input_output_aliases={i: 2 + i for i in range(2 * n)}, compiler_params=SPLIT_COPY,
    )(*_in_hbm(sources), *_in_hbm(lands), after)
    return outs[0], outs[1], outs[2:2 + n], outs[2 + n:2 + 2 * n], outs[-1]


def _split_wait(copies, send_sems, recv_sems, sources, lands, after, name):
    n = len(sources)

    def body(*refs):
        for cp in copies(refs[:n], refs[n:2 * n], refs[2 * n], refs[2 * n + 1]):
            cp.wait_send()
            cp.wait_recv()

    through = [pltpu.HBM(a.shape, a.dtype) for a in list(sources) + list(lands)]
    outs = pl.pallas_call(
        body, name=name, in_specs=[HBM_SPEC] * (2 * n) + [SEM_SPEC, SEM_SPEC, ANY_SPEC], out_specs=[HBM_SPEC] * (2 * n),
        out_shape=through, input_output_aliases={i: i for i in range(2 * n)}, compiler_params=SPLIT_COPY,
    )(*sources, *lands, send_sems, recv_sems, after)
    return outs[:n], outs[n:]


def _chip_slab(land, slot, rows):
    return land.at[slot, rows] if len(land.shape) == 3 else land.at[rows, slot]


def _gather_copies(w_refs, land_refs, send_sems, recv_sems):
    n = len(w_refs)
    x, y, c = _place()
    chips = [(1 - x, y), (x, 1 - y), (1 - x, 1 - y)]
    cps = []
    for a in range(n):
        hr = w_refs[a].shape[0] // 2
        mine, every = pl.ds(c * hr, hr), pl.ds(0, 2 * hr)
        cps.append(_remote(w_refs[a], _chip_slab(land_refs[a], 2 * x + y, every), send_sems, recv_sems, 3 * n + a, (x, y, 1 - c)))
        for k, chip in enumerate(chips):
            cps.append(_remote(w_refs[a].at[mine], _chip_slab(land_refs[a], 2 * x + y, mine), send_sems, recv_sems, 3 * a + k, (*chip, c)))
    return cps


def gather_start(shards, after, name):
    lands = [lax.empty(s.shape[:-2] + (N_CHIPS,) + s.shape[-2:], s.dtype) for s in shards]
    return _split_start(_gather_copies, shards, lands, after, 4 * len(shards), name)


def gather_wait(state, after, name):
    send_sems, recv_sems, sources, lands, _ = state
    return _split_wait(_gather_copies, send_sems, recv_sems, sources, lands, after, name)[1]


def gather_pass(lands, name):
    n = len(lands)

    def body(*refs):
        out_refs, send_sems, recv_sems = refs[n:2 * n], refs[2 * n], refs[2 * n + 1]
        x, y, c = _place()
        chips = [(1 - x, y), (x, 1 - y), (1 - x, 1 - y)]
        sent = []
        for a in range(n):
            hr = out_refs[a].shape[0 if len(out_refs[a].shape) == 4 else 1] // 2
            for k, (px, py) in enumerate(chips):
                landed = _chip_slab(out_refs[a], 2 * px + py, pl.ds(c * hr, hr))
                sent.append(_remote(landed, landed, send_sems, recv_sems, 3 * a + k, (x, y, 1 - c)))
        for cp in sent:
            cp.start()
        for a in range(n):
            hr = out_refs[a].shape[0 if len(out_refs[a].shape) == 4 else 1] // 2
            for k, (px, py) in enumerate(chips):
                theirs = _chip_slab(out_refs[a], 2 * px + py, pl.ds((1 - c) * hr, hr))
                _remote(theirs, theirs, send_sems, recv_sems, 3 * a + k, (x, y, 1 - c)).wait_recv()
        for cp in sent:
            cp.wait_send()

    return hbm_call(
        body, name=name, in_specs=[HBM_SPEC] * n, out_specs=[HBM_SPEC] * n,
        out_shape=[jax.ShapeDtypeStruct(a.shape, a.dtype) for a in lands], input_output_aliases={a: a for a in range(n)},
        scratch_shapes=[pltpu.SemaphoreType.DMA((3 * n,))] * 2,
    )(*lands)


def _scatter_copies(t_refs, land_refs, send_sems, recv_sems):
    x, y, c = _place()
    chips = [(1 - x, y), (x, 1 - y), (1 - x, 1 - y)]
    return [_remote(t_refs[a].at[:, 2 * px + py], land_refs[a].at[:, k], send_sems, recv_sems, 3 * a + k, (px, py, c))
            for a in range(len(t_refs)) for k, (px, py) in enumerate(chips)]


def scatter_start(parts, after, name):
    lands = [lax.empty((t.shape[0], N_CHIPS - 1) + t.shape[2:], t.dtype) for t in parts]
    return _split_start(_scatter_copies, parts, lands, after, 3 * len(parts), name)


def scatter_wait(state, after, name):
    send_sems, recv_sems, sources, lands, _ = state
    return _split_wait(_scatter_copies, send_sems, recv_sems, sources, lands, after, name)


def swap_sibling(parts, name):
    n = len(parts)

    def body(*refs):
        v_refs, out_refs, send_sems, recv_sems = refs[:n], refs[n:2 * n], refs[2 * n], refs[2 * n + 1]
        x, y, c = _place()
        cps = []
        for a in range(n):
            hr = v_refs[a].shape[2] // 2
            cps.append(_remote(v_refs[a].at[:, :, pl.ds((1 - c) * hr, hr)], out_refs[a], send_sems, recv_sems, a, (x, y, 1 - c)))
        for cp in cps:
            cp.start()
        for cp in cps:
            cp.wait()

    return hbm_call(
        body, name=name, in_specs=[HBM_SPEC] * n, out_specs=[HBM_SPEC] * n,
        out_shape=[jax.ShapeDtypeStruct(v.shape[:2] + (v.shape[2] // 2, v.shape[3]), v.dtype) for v in parts],
        scratch_shapes=[pltpu.SemaphoreType.DMA((n,))] * 2,
    )(*parts)


def join_halves(halves, layer, name):
    n = len(halves)

    def body(*refs):
        out_refs, send_sems, recv_sems = refs[n:2 * n], refs[2 * n], refs[2 * n + 1]
        x, y, c = _place()
        cps = []
        for a in range(n):
            hr = out_refs[a].shape[1] // 2
            mine = out_refs[a].at[layer, pl.ds(c * hr, hr)]
            cps.append(_remote(mine, mine, send_sems, recv_sems, a, (x, y, 1 - c)))
        for cp in cps:
            cp.start()
        for a in range(n):
            hr = out_refs[a].shape[1] // 2
            theirs = out_refs[a].at[layer, pl.ds((1 - c) * hr, hr)]
            _remote(theirs, theirs, send_sems, recv_sems, a, (x, y, 1 - c)).wait_recv()
        for cp in cps:
            cp.wait_send()

    return hbm_call(
        body, name=name, in_specs=[HBM_SPEC] * n, out_specs=[HBM_SPEC] * n,
        out_shape=[jax.ShapeDtypeStruct(f.shape, f.dtype) for f in halves], input_output_aliases={a: a for a in range(n)},
        scratch_shapes=[pltpu.SemaphoreType.DMA((n,))] * 2,
    )(*halves)


def gather_devices(v, name):
    def body(v_ref, out_ref, send_sems, recv_sems, local_sem):
        x, y, c = _place()
        me = 4 * x + 2 * y + c
        own = pltpu.make_async_copy(v_ref, out_ref.at[me], local_sem)
        own.start()
        peers = [((x + dx) % 2, (y + dy) % 2, (c + dc) % 2) for dx in (0, 1) for dy in (0, 1) for dc in (0, 1)][1:]
        sent = []
        for k, peer in enumerate(peers):
            cp = pltpu.make_async_remote_copy(src_ref=v_ref, dst_ref=out_ref.at[me], send_sem=send_sems.at[k], recv_sem=recv_sems.at[k],
                                              device_id=peer, device_id_type=MESH_ID)
            cp.start()
            sent.append(cp)
        for k, (px, py, pc) in enumerate(peers):
            slot = out_ref.at[4 * px + 2 * py + pc]
            pltpu.make_async_remote_copy(src_ref=slot, dst_ref=slot, send_sem=send_sems.at[k], recv_sem=recv_sems.at[k],
                                         device_id=(px, py, pc), device_id_type=MESH_ID).wait_recv()
        for cp in sent:
            cp.wait_send()
        own.wait()

    vm = pl.BlockSpec(memory_space=pltpu.VMEM)
    return pl.pallas_call(body, name=name, in_specs=[vm], out_specs=vm, out_shape=jax.ShapeDtypeStruct((N_DEV,) + v.shape, v.dtype),
                          scratch_shapes=[pltpu.SemaphoreType.DMA((N_DEV - 1,)), pltpu.SemaphoreType.DMA((N_DEV - 1,)),
                                          pltpu.SemaphoreType.DMA])(v)


ADD_ROWS = 512


def add_pair(place, a, b, name):
    L, n, hr, cols = b.shape
    tr = _divisors(hr, 2 * SUBLANES, ADD_ROWS)[0]
    nb = hr // tr

    def body(p_ref, a_ref, b_ref, o_ref):
        del p_ref
        o_ref[...] = (a_ref[...].astype(F32) + b_ref[...].astype(F32)).astype(o_ref.dtype)

    blk = pl.BlockSpec((None, None, tr, cols), lambda l, d, i, p: (l, d, i, 0))
    grid_spec = pltpu.PrefetchScalarGridSpec(
        num_scalar_prefetch=1, grid=(L, n, nb),
        in_specs=[pl.BlockSpec((None, None, tr, cols), lambda l, d, i, p: (l, d, p[0] * nb + i, 0)), blk], out_specs=blk)
    return hbm_call(body, name=name, grid_spec=grid_spec, out_shape=jax.ShapeDtypeStruct(b.shape, b.dtype),
                          compiler_params=_params(("parallel", "parallel", "parallel")))(place, a, b)


def add_chips(place, own, others, layer, stacked, name):
    _, n, hr, cols = others.shape
    tr = _divisors(hr, 2 * SUBLANES, ADD_ROWS)[0]
    nb = hr // tr
    create = isinstance(stacked, tuple)

    def body(p_ref, own_ref, *refs):
        del p_ref
        acc = own_ref[...].astype(F32)
        for k in range(n):
            acc = acc + refs[k][...].astype(F32)
        refs[-1][...] = acc

    ins = [pl.BlockSpec((None, None, tr, cols), lambda i, p: (0, p[1], i, 0))]
    ins += [pl.BlockSpec((None, None, tr, cols), functools.partial(lambda k, i, p: (0, k, i, 0), k)) for k in range(n)]
    grid_spec = pltpu.PrefetchScalarGridSpec(num_scalar_prefetch=1, grid=(nb,), in_specs=ins + ([] if create else [ANY_SPEC]),
                                             out_specs=pl.BlockSpec((None, tr, cols), lambda i, p: (layer, p[0] * nb + i, 0)))
    shape = stacked if create else stacked.shape
    return hbm_call(body, name=name, grid_spec=grid_spec, out_shape=jax.ShapeDtypeStruct(shape, F32),
                          input_output_aliases={} if create else {n + 2: 0},
                          compiler_params=_params(("parallel",)))(place, own, *([others] * n), *([] if create else [stacked]))


def _alpha(depth):
    return (2 * depth) ** 0.25


def _wmm(a, weight, mode, name, deps=(), **more):
    arr, how = weight
    return mm(a, arr, mode, name, deps=deps, **how, **more)


def layer_fwd(h, mem, w, tab, alpha, deps=(), late=None):
    D = h.shape[1]
    proj = _wmm(h, w["w_in"], "nt", "mm_proj", deps)
    xc, r, ig, a, b = rg_gates_fwd(proj, w["conv_w"], w["conv_b"], w["w_rg"], w["b_rg"], w["w_ig"], w["b_ig"], w["lru_lambda"], "rg_gates_fwd")
    hs, y_rnn = rg_scan_fwd(proj, a, b, "rg_scan_fwd")
    y_attn = attn_fwd(proj, w["sinks"], tab, D, "attn_fwd")
    deps = ()
    if late is not None:
        rest, deps = late(y_attn)
        w = {**w, **rest}
    pr = _wmm(y_rnn, w["w_br_rnn"], "nn", "mm_br_rnn", deps)
    pa = _wmm(y_attn, w["w_br_attn"], "nn", "mm_br_attn")
    merged = merge_fwd(proj, pr, pa, "merge_fwd")
    h1, xh1, rs1 = _wmm(merged, w["w_out"], "nn", "mm_out_ln1", post_norm=(h, w["ln1_g"], w["ln1_b"], alpha))
    qc = _wmm(h1, w["cq_w"], "nn", "mm_cq", out_dtype=MXU_DTYPE)
    kv = _wmm(mem, w["ckv_w"], "nn", "mm_ckv", out_dtype=MXU_DTYPE)
    o = cross_fwd(qc, kv, "cross_fwd")
    h2, xh2, rs2 = _wmm(o, w["co_w"], "nn", "mm_co_ln2", post_norm=(h1, w["ln2_g"], w["ln2_b"], alpha))
    gu = _wmm(h2, w["ffn_wi"], "nn", "mm_ffn_wi", out_blocks=2)
    act = swiglu_fwd(gu, "swiglu_fwd")
    h3, xh3, rs3 = _wmm(act, w["ffn_wo"], "nn", "mm_ffn_wo_ln3", post_norm=(h2, w["ln3_g"], w["ln3_b"], alpha))
    saved = dict(h=h, proj=proj, xc=xc, r=r, ig=ig, a=a, hs=hs, y_rnn=y_rnn, y_attn=y_attn, pr=pr, pa=pa, xh1=xh1, rs1=rs1, h1=h1,
                 qc=qc, kv=kv, o=o, xh2=xh2, rs2=rs2, h2=h2, gu=gu, xh3=xh3, rs3=rs3)
    return h3, saved, w


def layer_bwd(dh, mem, w, s, tab, alpha, deps=(), halfway=None):
    D = dh.shape[1]
    g = {}
    wg = dict(out_dtype=MXU_DTYPE)
    dz3, g["ln3_g"], g["ln3_b"] = ln_bwd(dh, None, s["xh3"], s["rs3"], w["ln3_g"], 1.0, "ln3_bwd")
    act = swiglu_fwd(s["gu"], "swiglu_refwd")
    g["ffn_wo"] = mm(act, dz3, "tn", "mm_d_ffn_wo", deps=deps, **wg)
    dact = _wmm(dz3, w["ffn_wo"], "nt", "mm_dact")
    dgu = swiglu_bwd(s["gu"], dact, "swiglu_bwd")
    g["ffn_wi"] = mm(s["h2"], dgu, "tn", "mm_d_ffn_wi", b_blocks=2, out_blocks=N_CHIPS, **wg)
    dh2 = _wmm(dgu, w["ffn_wi"], "nt", "mm_dh2", a_blocks=2)
    dz2, g["ln2_g"], g["ln2_b"] = ln_bwd(dz3, dh2, s["xh2"], s["rs2"], w["ln2_g"], alpha, "ln2_bwd")
    g["co_w"] = mm(s["o"], dz2, "tn", "mm_d_co", **wg)
    do = _wmm(dz2, w["co_w"], "nt", "mm_do", out_dtype=MXU_DTYPE)
    dqc, dkv = cross_bwd(s["qc"], s["kv"], do, "cross_bwd")
    g["cq_w"] = mm(s["h1"], dqc, "tn", "mm_d_cq", **wg)
    g["ckv_w"] = mm(mem, dkv, "tn", "mm_d_ckv", out_blocks=N_CHIPS, **wg)
    dh1 = _wmm(dqc, w["cq_w"], "nt", "mm_dh1")
    deps = halfway(g, dh1) if halfway is not None else ()
    dz1, g["ln1_g"], g["ln1_b"] = ln_bwd(dz2, dh1, s["xh1"], s["rs1"], w["ln1_g"], alpha, "ln1_bwd")
    merged = merge_fwd(s["proj"], s["pr"], s["pa"], "merge_refwd")
    g["w_out"] = mm(merged, dz1, "tn", "mm_d_out", deps=deps, **wg)
    dm = _wmm(dz1, w["w_out"], "nt", "mm_dmerged")
    dpr, dpa, dg_rnn, dg_attn = merge_bwd(s["proj"], s["pr"], s["pa"], dm, "merge_bwd")
    g["w_br_rnn"] = mm(s["y_rnn"], dpr, "tn", "mm_d_br_rnn", **wg)
    g["w_br_attn"] = mm(s["y_attn"], dpa, "tn", "mm_d_br_attn", **wg)
    dy_rnn = _wmm(dpr, w["w_br_rnn"], "nt", "mm_dy_rnn")
    dy_attn = _wmm(dpa, w["w_br_attn"], "nt", "mm_dy_attn")
    dq, dkb, dvb, dsink = attn_bwd(s["proj"], w["sinks"], tab, s["y_attn"], dy_attn, D, "attn_bwd")
    dk, dv = band_fold(dkb, dvb, "band_fold")
    g["sinks"] = dsink[:, :w["sinks"].shape[0]]
    dgr, gt = rg_scan_bwd(s["proj"], dy_rnn, s["hs"], s["a"], "rg_scan_bwd")
    dxc, g["w_rg"], g["w_ig"], g["b_rg"], g["b_ig"], g["lru_lambda"] = rg_gates_bwd(
        gt, s["hs"], s["xc"], s["r"], s["ig"], w["w_rg"], w["w_ig"], w["lru_lambda"], "rg_gates_bwd")
    dxr, g["conv_w"], g["conv_b"] = rg_conv_bwd(s["proj"], dxc, w["conv_w"], "rg_conv_bwd")
    dproj = jnp.concatenate([dxr, dgr, dq, dk, dv, dg_rnn, dg_attn], axis=1)
    g["w_in"] = mm(dproj, s["h"], "tn", "mm_d_in", **wg)
    dhm = _wmm(dproj, w["w_in"], "nn", "mm_dh")
    return axpby(dz1, dhm, alpha, "layer_dx"), g


def local_step(x, mem, target, depth, weights_of, grads_halfway, grads_done):
    alpha = _alpha(depth)
    tab = rope_table(x.shape[0])
    h, saved, layers = x, [], []
    for l in range(depth):
        wl, deps, late = weights_of(l, h)
        h, s, wl = layer_fwd(h, mem, wl, tab, alpha, deps, late)
        layers.append(wl)
        saved.append(s)
    dh, loss = loss_head(h, target, "loss_head")
    deps = ()
    for l in reversed(range(depth)):
        dh, g = layer_bwd(dh, mem, layers[l], saved[l], tab, alpha, deps, grads_halfway(l))
        deps = grads_done(l, g, dh)
    return loss, dh


def _pad_rows(flat):
    n = flat.shape[0]
    rows = -(-n // (LANES * SUBLANES)) * SUBLANES
    return jnp.pad(flat, (0, rows * LANES - n)).reshape(rows, LANES)


def kernel(x, mem, w_in, conv_w, conv_b, w_rg, b_rg, w_ig, b_ig, lru_lambda, w_br_rnn, w_br_attn, sinks, w_out, ln1_g, ln1_b, cq_w, ckv_w, co_w, ln2_g, ln2_b, ffn_wi, ffn_wo, ln3_g, ln3_b, loss_target, m_w_in, m_conv_w, m_conv_b, m_w_rg, m_b_rg, m_w_ig, m_b_ig, m_lru_lambda, m_w_br_rnn, m_w_br_attn, m_sinks, m_w_out, m_ln1_g, m_ln1_b, m_cq_w, m_ckv_w, m_co_w, m_ln2_g, m_ln2_b, m_ffn_wi, m_ffn_wo, m_ln3_g, m_ln3_b, v_w_in, v_conv_w, v_conv_b, v_w_rg, v_b_rg, v_w_ig, v_b_ig, v_lru_lambda, v_w_br_rnn, v_w_br_attn, v_sinks, v_w_out, v_ln1_g, v_ln1_b, v_cq_w, v_ckv_w, v_co_w, v_ln2_g, v_ln2_b, v_ffn_wi, v_ffn_wo, v_ln3_g, v_ln3_b):
    args = dict(locals())
    w = {n: args[n] for n in WEIGHTS}
    m = {n: args["m_" + n] for n in WEIGHTS}
    v = {n: args["v_" + n] for n in WEIGHTS}
    for group in (w, m, v):
        group["w_in"] = jnp.swapaxes(group["w_in"], 1, 2)
    cx, cy, cc = _place()
    chip = 2 * cx + cy
    L = w_in.shape[0]

    place = jnp.stack([cc, chip]).astype(jnp.int32)
    cw_rows = _pad_rows(conv_w.reshape(-1))
    cw_all = gather_devices(cw_rows, "gather_conv_w")[0::2]
    cw_parts = cw_all.reshape(N_CHIPS, -1)[:, :conv_w.size].reshape((N_CHIPS,) + conv_w.shape)
    conv_full = jnp.concatenate([cw_parts[k] for k in range(N_CHIPS)], axis=2)

    shards = [{n: w[n][l].astype(MXU_DTYPE) for n in BIG} for l in range(L)]
    late_names = tuple(n for n in BIG if n not in GATHER_FIRST)
    gathering = {(0, GATHER_FIRST): gather_start([shards[0][n] for n in GATHER_FIRST], cw_rows, "gather_start_0a")}
    gathering[0, late_names] = gather_start([shards[0][n] for n in late_names], gathering[0, GATHER_FIRST][4], "gather_start_0b")

    def gathered(l, names, after, tag):
        lands = gather_pass(gather_wait(gathering.pop((l, names)), after, f"gather_wait_{tag}"), f"gather_pass_{tag}")
        wl = {}
        for n, gw in zip(names, lands):
            rows_joined = gw.reshape(gw.shape[:-3] + (-1, gw.shape[-1]))
            if n in COL_BLOCKED:
                wl[n] = (gw, dict(b_blocks=N_CHIPS))
            elif n in GATE_WEIGHTS:
                wl[n] = rows_joined
            else:
                wl[n] = (rows_joined, {})
        return wl, lands

    def start_layer(l, after):
        if l >= L:
            return ()
        gathering[l, BIG] = gather_start([shards[l][n] for n in BIG], after, f"gather_start_{l}")
        return (gathering[l, BIG][4],)

    def weights_of(l, h):
        deps, late = (), None
        if l == 0:
            wl, _ = gathered(0, GATHER_FIRST, h, "0a")

            def late(after):
                rest, lands = gathered(0, late_names, after, "0b")
                return rest, start_layer(1, lands[0])
        else:
            wl, lands = gathered(l, BIG, h, str(l))
            deps = start_layer(l + 1, lands[0])
        for n in SMALL:
            wl[n] = conv_full[l] if n == "conv_w" else w[n][l] if n == "sinks" else w[n][l][None, :]
        return wl, deps, late

    def for_chips(n, g):
        if n in COL_BLOCKED:
            return g
        if n in GATE_WEIGHTS:
            nb, bw, _ = g.shape
            g = g.reshape(nb, N_CHIPS, bw // N_CHIPS, bw).transpose(1, 0, 2, 3).reshape(N_CHIPS, nb * bw // N_CHIPS, bw)
        else:
            g = g.reshape(N_CHIPS, g.shape[0] // N_CHIPS, g.shape[1])
        return g.astype(MXU_DTYPE)

    reduced, scattering, small_grads = {}, {}, [None] * L
    late_grads = tuple(n for n in BIG if n not in SCATTER_FIRST)

    def start_scatter(l, names, g, after, tag):
        partial_sums = [for_chips(n, g[n])[None] for n in names]
        from_sibling = swap_sibling(partial_sums, f"grad_to_sibling_{tag}")
        chip_sums = [add_pair(place, a, b, f"grad_add_pair_{n}_{l}") for n, a, b in zip(names, partial_sums, from_sibling)]
        scattering[l, names] = scatter_start(chip_sums, after, f"grad_scatter_start_{tag}")
        return (scattering[l, names][4],)

    def finish_layer(l, after):
        for names in [k[1] for k in list(scattering) if k[0] == l]:
            tag = str(l) if names == BIG else f"{l}{'a' if names == SCATTER_FIRST else 'b'}"
            chip_sums, from_chips = scatter_wait(scattering.pop((l, names)), after, f"grad_scatter_wait_{tag}")
            for n, own, others in zip(names, chip_sums, from_chips):
                target = reduced.get(n, (L, 2 * own.shape[2], own.shape[3]))
                reduced[n] = add_chips(place, own, others, l, target, f"grad_add_chips_{n}_{l}")
        reduced.update(zip(BIG, join_halves([reduced[n] for n in BIG], l, f"grad_join_{l}")))

    def grads_halfway(l):
        if l > 0:
            return None

        def halfway(g, after):
            return start_scatter(0, SCATTER_FIRST, g, after, "0a")

        return halfway

    def grads_done(l, g, dh):
        small_grads[l] = {n: g[n] for n in SMALL}
        deps = start_scatter(l, late_grads if l == 0 else BIG, g, dh, "0b" if l == 0 else str(l))
        if 1 < l + 1 < L:
            finish_layer(l + 1, dh)
        return deps

    loss11, dx = local_step(x[0], mem[0], loss_target[0], L, weights_of, grads_halfway, grads_done)
    loss = lax.psum(loss11[0, 0], ("x", "y", "c"))

    small_full = {n: jnp.stack([gl[n] for gl in small_grads]).reshape(w[n].shape[:1] + ((CONV_WIDTH, -1) if n == "conv_w" else (-1,)))
                  for n in SMALL}
    small_flat = jnp.concatenate([small_full[n].reshape(-1) for n in SMALL])
    small_sum = sum_devices(gather_devices(_pad_rows(small_flat), "gather_small_grads"), "sum_small_grads").reshape(-1)
    delta, new_m, new_v, grad = {}, {}, {}, {}
    off = 0
    for n in SMALL:
        gfull = small_sum[off:off + small_full[n].size].reshape(small_full[n].shape)
        off += small_full[n].size
        if n == "conv_w":
            width = conv_w.shape[2]
            gfull = lax.dynamic_slice_in_dim(gfull, chip * width, width, axis=2)
        delta[n], new_m[n], new_v[n], grad[n] = adamw(w[n], gfull, m[n], v[n], "adamw_" + n)
    first = min(2, L)
    updated = {}
    if first < L:
        for n in BIG:
            updated[n] = adamw(w[n], reduced[n].reshape(w[n].shape), m[n], v[n], f"adamw_{n}_upper", layers=(first, L))
    done = [delta[n] for n in SMALL] + [updated[n][0] for n in updated]
    after = jnp.stack([d[(0,) * d.ndim] for d in done])
    for l in reversed(range(first)):
        finish_layer(l, after)

    for n in BIG:
        some = dict(layers=(0, first), into=updated[n]) if updated else {}
        delta[n], new_m[n], new_v[n], grad[n] = adamw(w[n], reduced[n].reshape(w[n].shape), m[n], v[n], "adamw_" + n, **some)
    for group in (delta, new_m, new_v, grad):
        group["w_in"] = jnp.swapaxes(group["w_in"], 1, 2)
    return (loss, dx[None], *[grad[n] for n in WEIGHTS], *[delta[n] for n in WEIGHTS], *[new_m[n] for n in WEIGHTS],
            *[new_v[n] for n in WEIGHTS])
```

```python
import functools
import math

import jax
import jax.numpy as jnp
import numpy as np
from jax import lax
from jax.experimental import pallas as pl
from jax.experimental.pallas import tpu as pltpu

F32 = jnp.float32
BF16 = jnp.bfloat16
MXU_DTYPE = BF16

HEAD_DIM = 64
N_KV_HEADS = 2
WINDOW = 128
ROT_DIM = HEAD_DIM // 4
ROPE_THETA = 500000.0
CROSS_HEADS = 4
RNN_BLOCKS = 4
CONV_WIDTH = 4
LRU_C = 8.0
LN_EPS = 1e-5
NEG_INF = -1e30
ADAM_LR = 0.001
ADAM_B1 = 0.9
ADAM_B2 = 0.999
ADAM_EPS = 1e-08
ADAM_WD = 0.01
ADAM_STEP = 10

VMEM_BYTES_V7X = 64 * 1024 * 1024
VMEM_BLOCK_BUDGET = 36 * 1024 * 1024
LANES = 128
SUBLANES = 8

MESH_ID = pl.DeviceIdType.MESH
N_CHIPS = 4
N_DEV = 8

BIG = ("w_in", "w_rg", "w_ig", "w_br_rnn", "w_br_attn", "w_out", "cq_w", "ckv_w", "co_w", "ffn_wi", "ffn_wo")
SHARD_AXIS = {"w_in": 0, "w_rg": 1, "w_ig": 1, "w_br_rnn": 0, "w_br_attn": 0, "w_out": 0, "cq_w": 0, "ckv_w": 1,
              "co_w": 0, "ffn_wi": 1, "ffn_wo": 0}
SMALL = ("conv_w", "conv_b", "b_rg", "b_ig", "lru_lambda", "sinks", "ln1_g", "ln1_b", "ln2_g", "ln2_b", "ln3_g", "ln3_b")
WEIGHTS = ("w_in", "conv_w", "conv_b", "w_rg", "b_rg", "w_ig", "b_ig", "lru_lambda", "w_br_rnn", "w_br_attn", "sinks",
           "w_out", "ln1_g", "ln1_b", "cq_w", "ckv_w", "co_w", "ln2_g", "ln2_b", "ffn_wi", "ffn_wo", "ln3_g", "ln3_b")
GATE_WEIGHTS = ("w_rg", "w_ig")
COL_BLOCKED = ("ckv_w", "ffn_wi")
GATHER_FIRST = ("w_in", "w_rg", "w_ig")
SCATTER_FIRST = ("ffn_wo", "ffn_wi", "co_w", "cq_w", "ckv_w")


def _params(dims=None, vmem=None):
    return pltpu.CompilerParams(dimension_semantics=dims, vmem_limit_bytes=vmem)


def _vmem_limit(block_bytes, temp_bytes=0):
    want = int(2 * block_bytes + temp_bytes) + (6 << 20)
    return max(32 << 20, min(want, VMEM_BYTES_V7X - (6 << 20)))


def _divisors(n, align, cap):
    out = [d for d in range(align, min(n, cap) + 1, align) if n % d == 0]
    if n <= cap and n not in out:
        out.append(n)
    return sorted(out, reverse=True) or [n]


PIN_MIN_ELEMENTS = 1 << 18


def hbm_call(body, **kw):
    def in_hbm(s):
        return pltpu.HBM(s.shape, s.dtype) if math.prod(s.shape) >= PIN_MIN_ELEMENTS else s

    shapes = kw.pop("out_shape")
    shapes = [in_hbm(s) for s in shapes] if isinstance(shapes, (list, tuple)) else in_hbm(shapes)
    call = pl.pallas_call(body, out_shape=shapes, **kw)

    def run(*args):
        return call(*[pltpu.with_memory_space_constraint(a, pltpu.HBM) if a.size >= PIN_MIN_ELEMENTS else a for a in args])

    return run


def _sigmoid(x):
    return 1.0 / (1.0 + jnp.exp(-x))


def _gelu_parts(x):
    c = math.sqrt(2.0 / math.pi)
    u = c * (x + 0.044715 * x * x * x)
    t = jnp.tanh(u)
    return t, c * (1.0 + 3 * 0.044715 * x * x)


def _gelu(x):
    t, _ = _gelu_parts(x)
    return 0.5 * x * (1.0 + t)


def _gelu_grad(x):
    t, du = _gelu_parts(x)
    return 0.5 * (1.0 + t) + 0.5 * x * (1.0 - t * t) * du


def _neg_expm1(x):
    series = x * (1.0 + x * (0.5 + x * (1.0 / 6 + x * (1.0 / 24 + x * (1.0 / 120)))))
    return -jnp.where(x > -0.1, series, jnp.exp(x) - 1.0)


def _softplus_neg(lam):
    x = -lam
    return jnp.maximum(x, 0.0) + jnp.log1p(jnp.exp(-jnp.abs(x)))


STEP_US = 0.35
HBM_BYTES_PER_US = 2.5e6
MXU_FLOPS_PER_US = 7e8


def _layer_norm(z, g, b):
    mu = jnp.mean(z, axis=-1, keepdims=True)
    zc = z - mu
    rs = lax.rsqrt(jnp.mean(zc * zc, axis=-1, keepdims=True) + LN_EPS)
    xh = zc * rs
    return xh * g + b, xh, rs


def mm(a, b, mode, name, *, b_index=(), a_blocks=0, b_blocks=0, out_blocks=0, out_dtype=F32, deps=(), post_norm=None):
    nlead = len(b_index) + (1 if b_blocks else 0)
    bk, bn = b.shape[nlead:]
    M, K = (a.shape[-1], a.shape[-2]) if mode == "tn" else (a.shape[-2], a.shape[-1] * max(a_blocks, 1))
    N = bk if mode == "nt" else bn * max(b_blocks, 1) if mode == "nn" or mode == "tn" else bn
    asz, bsz, osz = a.dtype.itemsize, b.dtype.itemsize, jnp.dtype(out_dtype).itemsize
    n_unit = math.gcd(N // max(out_blocks, 1), N // max(b_blocks, 1) if mode != "nt" else N)
    k_unit = math.gcd(K // max(a_blocks, 1), K // max(b_blocks, 1) if mode == "nt" else K)
    tms = _divisors(M, LANES if mode == "tn" else SUBLANES, 2048)
    tns = [N] if post_norm else _divisors(n_unit, LANES, 2048)
    tks = _divisors(k_unit, LANES, k_unit)
    best = None
    for tm in tms:
        for tn in tns:
            for tk in tks:
                nk = K // tk
                scratch = tm * tn * 4 if (nk > 1 and osz != 4) else 0
                blocks = tm * tk * asz + tn * tk * bsz + tm * tn * osz * (3 if post_norm else 1)
                temps = tm * tk * (2 + (4 if mode == "tn" else 0)) + tn * tk * 2 + tm * tn * 4 + scratch
                if 2 * blocks + temps > VMEM_BLOCK_BUDGET + (8 << 20):
                    continue
                ni, nj = M // tm, N // tn
                traffic = M * K * asz * (nj if nk > 1 else 1) + N * K * bsz * (1 if nj * nk == 1 else ni) + M * N * osz
                busy = max(traffic / HBM_BYTES_PER_US, 2.0 * M * N * K / MXU_FLOPS_PER_US)
                cost = ni * nj * nk * STEP_US + busy + blocks / HBM_BYTES_PER_US
                if best is None or cost < best[0]:
                    best = (cost, tm, tn, tk, blocks, temps)
    _, tm, tn, tk, blocks, temps = best
    nk = K // tk
    use_scratch = nk > 1 and osz != 4

    def split(index, total, blocks, tile):
        per = total // blocks // tile
        return index // per, index % per

    def body(a_ref, b_ref, *rest):
        rest = rest[len(deps):]
        if post_norm:
            h_ref, g_ref, beta_ref, o_ref, xh_ref, rs_ref = rest[:6]
            acc = rest[6:]
        else:
            o_ref, acc = rest[0], rest[1:]
        av = a_ref[...].astype(MXU_DTYPE)
        bv = b_ref[...].astype(MXU_DTYPE)
        dn = {"nn": (((1,), (0,)), ((), ())), "nt": (((1,), (1,)), ((), ())), "tn": (((0,), (0,)), ((), ()))}[mode]
        r = lax.dot_general(av, bv, dn, preferred_element_type=F32)

        def normalise(f):
            o_ref[...], xh_ref[...], rs_ref[...] = _layer_norm(post_norm[3] * h_ref[...] + f, g_ref[...], beta_ref[...])

        if nk == 1 and post_norm:
            normalise(r)
        elif nk == 1:
            o_ref[...] = r.astype(o_ref.dtype)
        else:
            acc_ref = acc[0] if use_scratch else o_ref

            @pl.when(pl.program_id(2) == 0)
            def _():
                acc_ref[...] = r

            @pl.when(pl.program_id(2) > 0)
            def _():
                acc_ref[...] += r

            if use_scratch:
                @pl.when(pl.program_id(2) == nk - 1)
                def _():
                    o_ref[...] = acc_ref[...].astype(o_ref.dtype)
            elif post_norm:
                @pl.when(pl.program_id(2) == nk - 1)
                def _():
                    normalise(o_ref[...])

    if mode == "tn":
        a_spec = pl.BlockSpec((tk, tm), lambda i, j, k: (k, i))
    elif a_blocks:
        a_spec = pl.BlockSpec((None, tm, tk), lambda i, j, k: (split(k, K, a_blocks, tk)[0], i, split(k, K, a_blocks, tk)[1]))
    else:
        a_spec = pl.BlockSpec((tm, tk), lambda i, j, k: (i, k))
    lead = (None,) * nlead
    if mode == "nt":
        bmap = ((lambda i, j, k: b_index + (split(k, K, b_blocks, tk)[0], j, split(k, K, b_blocks, tk)[1])) if b_blocks
                else (lambda i, j, k: b_index + (j, k)))
        b_spec = pl.BlockSpec(lead + (tn, tk), bmap)
    else:
        bmap = ((lambda i, j, k: b_index + (split(j, N, b_blocks, tn)[0], k, split(j, N, b_blocks, tn)[1])) if b_blocks
                else (lambda i, j, k: b_index + (k, j)))
        b_spec = pl.BlockSpec(lead + (tk, tn), bmap)
    if out_blocks:
        o_spec = pl.BlockSpec((None, tm, tn), lambda i, j, k: (split(j, N, out_blocks, tn)[0], i, split(j, N, out_blocks, tn)[1]))
        o_shape = jax.ShapeDtypeStruct((out_blocks, M, N // out_blocks), out_dtype)
    else:
        o_spec = pl.BlockSpec((tm, tn), lambda i, j, k: (i, j))
        o_shape = jax.ShapeDtypeStruct((M, N), out_dtype)
    in_specs, extra = [a_spec, b_spec] + [pl.BlockSpec(memory_space=pl.ANY)] * len(deps), ()
    if post_norm:
        vec = pl.BlockSpec((1, N), lambda i, j, k: (0, 0))
        in_specs += [pl.BlockSpec((tm, N), lambda i, j, k: (i, 0)), vec, vec]
        o_spec = [o_spec, pl.BlockSpec((tm, N), lambda i, j, k: (i, 0)), pl.BlockSpec((tm, 1), lambda i, j, k: (i, 0))]
        o_shape = [o_shape, jax.ShapeDtypeStruct((M, N), F32), jax.ShapeDtypeStruct((M, 1), F32)]
        extra = post_norm[:3]
    return hbm_call(
        body, name=name, grid=(M // tm, N // tn, nk), in_specs=in_specs, out_specs=o_spec, out_shape=o_shape,
        scratch_shapes=[pltpu.VMEM((tm, tn), F32)] if use_scratch else [],
        compiler_params=_params(("parallel", "parallel", "arbitrary"), _vmem_limit(blocks, temps)),
    )(a, b, *deps, *extra)


ROW_TILE = 512
GATE_ROWS = 1024


def ln_bwd(dy_a, dy_b, xh, rs, g, c1, name):
    S, D = xh.shape
    tr = min(ROW_TILE, S)
    two = dy_b is not None

    def body(*refs):
        if two:
            a_ref, b_ref, xh_ref, rs_ref, g_ref, dz_ref, dg_ref, db_ref = refs
            dy = c1 * a_ref[...] + b_ref[...]
        else:
            a_ref, xh_ref, rs_ref, g_ref, dz_ref, dg_ref, db_ref = refs
            dy = a_ref[...]
        x = xh_ref[...]
        dyg = dy * g_ref[...]
        m1 = jnp.mean(dyg, axis=-1, keepdims=True)
        m2 = jnp.mean(dyg * x, axis=-1, keepdims=True)
        dz_ref[...] = rs_ref[...] * (dyg - m1 - x * m2)

        @pl.when(pl.program_id(0) == 0)
        def _():
            dg_ref[...] = jnp.zeros_like(dg_ref)
            db_ref[...] = jnp.zeros_like(db_ref)

        dg_ref[...] += jnp.sum(dy * x, axis=0, keepdims=True)
        db_ref[...] += jnp.sum(dy, axis=0, keepdims=True)

    row = pl.BlockSpec((tr, D), lambda i: (i, 0))
    vec = pl.BlockSpec((1, D), lambda i: (0, 0))
    ins = [row, row] if two else [row]
    args = (dy_a, dy_b) if two else (dy_a,)
    return hbm_call(
        body, name=name, grid=(S // tr,), in_specs=ins + [row, pl.BlockSpec((tr, 1), lambda i: (i, 0)), vec],
        out_specs=[row, vec, vec],
        out_shape=[jax.ShapeDtypeStruct((S, D), F32), jax.ShapeDtypeStruct((1, D), F32), jax.ShapeDtypeStruct((1, D), F32)],
        compiler_params=_params(("arbitrary",), 48 << 20),
    )(*args, xh, rs, g)


def axpby(a, b, c1, name):
    S, D = a.shape
    tr = min(ROW_TILE, S)

    def body(a_ref, b_ref, o_ref):
        o_ref[...] = c1 * a_ref[...] + b_ref[...]

    row = pl.BlockSpec((tr, D), lambda i: (i, 0))
    return hbm_call(body, name=name, grid=(S // tr,), in_specs=[row, row], out_specs=row,
                          out_shape=jax.ShapeDtypeStruct((S, D), F32), compiler_params=_params(("parallel",)))(a, b)


def loss_head(y, t, name):
    S, D = y.shape
    tr = min(ROW_TILE, S)
    nsteps = S // tr

    def body(y_ref, t_ref, dy_ref, l_ref, acc_ref):
        i = pl.program_id(0)

        @pl.when(i == 0)
        def _():
            acc_ref[...] = jnp.zeros_like(acc_ref)

        e = y_ref[...] - t_ref[...]
        dy_ref[...] = e * (1.0 / D)
        acc_ref[...] += jnp.sum(e * e, axis=0, keepdims=True)

        @pl.when(i == nsteps - 1)
        def _():
            l_ref[...] = jnp.sum(acc_ref[...], axis=1, keepdims=True) * (0.5 / D)

    row = pl.BlockSpec((tr, D), lambda i: (i, 0))
    return hbm_call(
        body, name=name, grid=(nsteps,), in_specs=[row, row],
        out_specs=[row, pl.BlockSpec((1, 1), lambda i: (0, 0))],
        out_shape=[jax.ShapeDtypeStruct((S, D), F32), jax.ShapeDtypeStruct((1, 1), F32)],
        scratch_shapes=[pltpu.VMEM((1, D), F32)], compiler_params=_params(("arbitrary",)),
    )(y, t)


SWIGLU_ROWS = 256


def swiglu_fwd(gu, name):
    _, S, Fh = gu.shape
    tc = _divisors(Fh, LANES, 1536)[0]
    tr = min(SWIGLU_ROWS, S)

    def body(gu_ref, o_ref):
        g = gu_ref[0]
        o_ref[...] = (g * _sigmoid(g) * gu_ref[1]).astype(o_ref.dtype)

    return hbm_call(
        body, name=name, grid=(S // tr, Fh // tc), in_specs=[pl.BlockSpec((2, tr, tc), lambda i, j: (0, i, j))],
        out_specs=pl.BlockSpec((tr, tc), lambda i, j: (i, j)), out_shape=jax.ShapeDtypeStruct((S, Fh), MXU_DTYPE),
        compiler_params=_params(("parallel", "parallel")),
    )(gu)


def swiglu_bwd(gu, dact, name):
    _, S, Fh = gu.shape
    tc = _divisors(Fh, LANES, 1536)[0]
    tr = min(SWIGLU_ROWS, S)

    def body(gu_ref, d_ref, o_ref):
        g, u, d = gu_ref[0], gu_ref[1], d_ref[...]
        s = _sigmoid(g)
        o_ref[0] = (d * u * (s * (1.0 + g * (1.0 - s)))).astype(o_ref.dtype)
        o_ref[1] = (d * (g * s)).astype(o_ref.dtype)

    both = pl.BlockSpec((2, tr, tc), lambda i, j: (0, i, j))
    return hbm_call(
        body, name=name, grid=(S // tr, Fh // tc), in_specs=[both, pl.BlockSpec((tr, tc), lambda i, j: (i, j))],
        out_specs=both, out_shape=jax.ShapeDtypeStruct((2, S, Fh), MXU_DTYPE), compiler_params=_params(("parallel", "parallel")),
    )(gu, dact)


GATE_COLS = 256


def merge_fwd(proj, pr, pa, name):
    S, D = pr.shape
    tr = min(GATE_ROWS, S)
    c0 = (3 * D + 2 * N_KV_HEADS * HEAD_DIM) // GATE_COLS
    c1 = c0 + D // GATE_COLS

    def body(gr_ref, ga_ref, pr_ref, pa_ref, o_ref):
        o_ref[...] = (_sigmoid(gr_ref[...]) * pr_ref[...] + _sigmoid(ga_ref[...]) * pa_ref[...]).astype(o_ref.dtype)

    blk = pl.BlockSpec((tr, GATE_COLS), lambda i, j: (i, j))
    return hbm_call(
        body, name=name, grid=(S // tr, D // GATE_COLS),
        in_specs=[pl.BlockSpec((tr, GATE_COLS), lambda i, j: (i, c0 + j)), pl.BlockSpec((tr, GATE_COLS), lambda i, j: (i, c1 + j)),
                  blk, blk],
        out_specs=blk, out_shape=jax.ShapeDtypeStruct((S, D), MXU_DTYPE), compiler_params=_params(("parallel", "parallel")),
    )(proj, proj, pr, pa)


def merge_bwd(proj, pr, pa, dm, name):
    S, D = pr.shape
    tr = min(GATE_ROWS, S)
    c0 = (3 * D + 2 * N_KV_HEADS * HEAD_DIM) // GATE_COLS
    c1 = c0 + D // GATE_COLS

    def body(gr_ref, ga_ref, pr_ref, pa_ref, dm_ref, dpr_ref, dpa_ref, dgr_ref, dga_ref):
        sr, sa, d = _sigmoid(gr_ref[...]), _sigmoid(ga_ref[...]), dm_ref[...]
        dpr_ref[...] = (d * sr).astype(dpr_ref.dtype)
        dpa_ref[...] = (d * sa).astype(dpa_ref.dtype)
        dgr_ref[...] = (d * pr_ref[...] * (sr * (1.0 - sr))).astype(dgr_ref.dtype)
        dga_ref[...] = (d * pa_ref[...] * (sa * (1.0 - sa))).astype(dga_ref.dtype)

    blk = pl.BlockSpec((tr, GATE_COLS), lambda i, j: (i, j))
    sds = jax.ShapeDtypeStruct((S, D), MXU_DTYPE)
    return hbm_call(
        body, name=name, grid=(S // tr, D // GATE_COLS),
        in_specs=[pl.BlockSpec((tr, GATE_COLS), lambda i, j: (i, c0 + j)), pl.BlockSpec((tr, GATE_COLS), lambda i, j: (i, c1 + j)),
                  blk, blk, blk],
        out_specs=[blk, blk, blk, blk], out_shape=[sds, sds, sds, sds], compiler_params=_params(("parallel", "parallel")),
    )(proj, proj, pr, pa, dm)


RG_ROWS = 512


def _shift_down(cur, prev, d, row, first):
    halo = jnp.where(first, 0.0, pltpu.roll(prev, d, 0))
    return jnp.where(row >= d, pltpu.roll(cur, d, 0), halo)


def _shift_up(cur, nxt, d, row, last, tr):
    halo = jnp.where(last, 0.0, pltpu.roll(nxt, tr - d, 0))
    return jnp.where(row < tr - d, pltpu.roll(cur, tr - d, 0), halo)


def _lru_coeffs(r, lam):
    sp = _softplus_neg(lam)
    la = -LRU_C * r * sp
    return sp, la, jnp.exp(la), _neg_expm1(2.0 * la)


def rg_gates_fwd(proj, conv_w, conv_b, w_rg, b_rg, w_ig, b_ig, lam, name):
    S = proj.shape[0]
    nblk, bw, _ = w_rg.shape
    D = nblk * bw
    tr = min(RG_ROWS, S)

    def body(xr_ref, xp_ref, cw_ref, cb_ref, wr_ref, br_ref, wi_ref, bi_ref, lam_ref, xc_ref, r_ref, i_ref, a_ref, b_ref):
        first = pl.program_id(1) == 0
        cur, prev = xr_ref[...], xp_ref[...]
        row = lax.broadcasted_iota(jnp.int32, cur.shape, 0)
        xc = cb_ref[...]
        for k in range(CONV_WIDTH - 1):
            xc = xc + _shift_down(cur, prev, CONV_WIDTH - 1 - k, row, first) * cw_ref[k:k + 1, :]
        xc = xc + cur * cw_ref[CONV_WIDTH - 1:CONV_WIDTH, :]
        xm = xc.astype(MXU_DTYPE)
        r = _sigmoid(jnp.dot(xm, wr_ref[...].astype(MXU_DTYPE), preferred_element_type=F32) + br_ref[...])
        ig = _sigmoid(jnp.dot(xm, wi_ref[...].astype(MXU_DTYPE), preferred_element_type=F32) + bi_ref[...])
        _, _, a, em = _lru_coeffs(r, lam_ref[...])
        xc_ref[...] = xc
        r_ref[...] = r
        i_ref[...] = ig
        a_ref[...] = a
        b_ref[...] = jnp.sqrt(em) * (ig * xc)

    tile = pl.BlockSpec((tr, bw), lambda n, i: (i, n))
    vec = pl.BlockSpec((1, bw), lambda n, i: (0, n))
    wblk = pl.BlockSpec((None, bw, bw), lambda n, i: (n, 0, 0))
    sds = jax.ShapeDtypeStruct((S, D), F32)
    return hbm_call(
        body, name=name, grid=(nblk, S // tr),
        in_specs=[tile, pl.BlockSpec((tr, bw), lambda n, i: (jnp.maximum(i - 1, 0), n)),
                  pl.BlockSpec((CONV_WIDTH, bw), lambda n, i: (0, n)), vec, wblk, vec, wblk, vec, vec],
        out_specs=[tile] * 5, out_shape=[sds] * 5, compiler_params=_params(("parallel", "parallel")),
    )(proj, proj, conv_w, conv_b, w_rg, b_rg, w_ig, b_ig, lam)


SCAN_COLS = 256
CHUNK = SUBLANES
SCAN_UNROLL = 4


def rg_scan_fwd(proj, a, b, name):
    S, D = a.shape
    cb = min(SCAN_COLS, D)
    goff = D // cb

    def body(a_ref, b_ref, g_ref, hs_ref, y_ref):
        row = lax.broadcasted_iota(jnp.int32, (CHUNK, cb), 0)

        def step(c, carry):
            r0 = pl.multiple_of(c * CHUNK, CHUNK)
            A = a_ref[pl.ds(r0, CHUNK), :]
            B = b_ref[pl.ds(r0, CHUNK), :]
            for d in (1, 2, 4):
                As = jnp.where(row >= d, pltpu.roll(A, d, 0), 1.0)
                Bs = jnp.where(row >= d, pltpu.roll(B, d, 0), 0.0)
                B = A * Bs + B
                A = A * As
            hs_ref[pl.ds(r0, CHUNK), :] = B + A * carry
            a_end = jnp.sum(jnp.where(row == CHUNK - 1, A, 0.0), axis=0, keepdims=True)
            b_end = jnp.sum(jnp.where(row == CHUNK - 1, B, 0.0), axis=0, keepdims=True)
            return b_end + a_end * carry

        lax.fori_loop(0, S // CHUNK, step, jnp.zeros((1, cb), F32), unroll=SCAN_UNROLL)
        y_ref[...] = (hs_ref[...] * _gelu(g_ref[...])).astype(y_ref.dtype)

    col = pl.BlockSpec((S, cb), lambda j: (0, j))
    return hbm_call(
        body, name=name, grid=(D // cb,), in_specs=[col, col, pl.BlockSpec((S, cb), lambda j: (0, goff + j))],
        out_specs=[col, col], out_shape=[jax.ShapeDtypeStruct((S, D), F32), jax.ShapeDtypeStruct((S, D), MXU_DTYPE)],
        compiler_params=_params(("parallel",), _vmem_limit(5 * S * cb * 4, 4 * S * cb * 4)),
    )(a, b, proj)


def rg_scan_bwd(proj, dy, hs, a, name):
    S, D = a.shape
    cb = min(SCAN_COLS, D)
    goff = D // cb
    nchunks = S // CHUNK

    def body(g_ref, dy_ref, hs_ref, a_ref, dg_ref, gt_ref):
        gate, dy = g_ref[...], dy_ref[...]
        dg_ref[...] = (dy * hs_ref[...] * _gelu_grad(gate)).astype(dg_ref.dtype)
        gt_ref[...] = dy * _gelu(gate)
        row = lax.broadcasted_iota(jnp.int32, (CHUNK, cb), 0)

        def step(k, carry):
            c = nchunks - 1 - k
            r0 = pl.multiple_of(c * CHUNK, CHUNK)
            rn = pl.multiple_of(jnp.minimum(c + 1, nchunks - 1) * CHUNK, CHUNK)
            last = c == nchunks - 1
            nxt = jnp.where(last, 0.0, pltpu.roll(a_ref[pl.ds(rn, CHUNK), :], CHUNK - 1, 0))
            A = jnp.where(row < CHUNK - 1, pltpu.roll(a_ref[pl.ds(r0, CHUNK), :], CHUNK - 1, 0), nxt)
            B = gt_ref[pl.ds(r0, CHUNK), :]
            for d in (1, 2, 4):
                As = jnp.where(row < CHUNK - d, pltpu.roll(A, CHUNK - d, 0), 1.0)
                Bs = jnp.where(row < CHUNK - d, pltpu.roll(B, CHUNK - d, 0), 0.0)
                B = A * Bs + B
                A = A * As
            gt_ref[pl.ds(r0, CHUNK), :] = B + A * carry
            a_end = jnp.sum(jnp.where(row == 0, A, 0.0), axis=0, keepdims=True)
            b_end = jnp.sum(jnp.where(row == 0, B, 0.0), axis=0, keepdims=True)
            return b_end + a_end * carry

        lax.fori_loop(0, nchunks, step, jnp.zeros((1, cb), F32), unroll=SCAN_UNROLL)

    col = pl.BlockSpec((S, cb), lambda j: (0, j))
    return hbm_call(
        body, name=name, grid=(D // cb,), in_specs=[pl.BlockSpec((S, cb), lambda j: (0, goff + j)), col, col, col],
        out_specs=[col, col], out_shape=[jax.ShapeDtypeStruct((S, D), MXU_DTYPE), jax.ShapeDtypeStruct((S, D), F32)],
        compiler_params=_params(("parallel",), _vmem_limit(6 * S * cb * 4, 6 * S * cb * 4)),
    )(proj, dy, hs, a)


def rg_gates_bwd(gt, hs, xc, r, ig, w_rg, w_ig, lam, name):
    S, D = xc.shape
    nblk, bw, _ = w_rg.shape
    tr = min(RG_ROWS, S)

    def body(gt_ref, hs_ref, hp_ref, xc_ref, r_ref, i_ref, wr_ref, wi_ref, lam_ref,
             dxc_ref, dwr_ref, dwi_ref, dbr_ref, dbi_ref, dl_ref):
        step = pl.program_id(1)
        g, hs, xc, r, ig, lam = gt_ref[...], hs_ref[...], xc_ref[...], r_ref[...], i_ref[...], lam_ref[...]
        row = lax.broadcasted_iota(jnp.int32, g.shape, 0)
        hprev = _shift_down(hs, hp_ref[...], 1, row, step == 0)
        sp, _, a, em = _lru_coeffs(r, lam)
        mult = jnp.sqrt(em)
        du = g * mult
        dla = g * hprev * a - (g * (ig * xc)) * (a * a) / mult
        dpr = (dla * (-LRU_C * sp)) * (r * (1.0 - r))
        dpi = (du * xc) * (ig * (1.0 - ig))
        dprm, dpim = dpr.astype(MXU_DTYPE), dpi.astype(MXU_DTYPE)
        nt = (((1,), (1,)), ((), ()))
        dxc_ref[...] = (du * ig + lax.dot_general(dprm, wr_ref[...].astype(MXU_DTYPE), nt, preferred_element_type=F32)
                        + lax.dot_general(dpim, wi_ref[...].astype(MXU_DTYPE), nt, preferred_element_type=F32))

        @pl.when(step == 0)
        def _():
            for ref in (dwr_ref, dwi_ref, dbr_ref, dbi_ref, dl_ref):
                ref[...] = jnp.zeros_like(ref)

        xct = xc.T.astype(MXU_DTYPE)
        dwr_ref[...] += jnp.dot(xct, dprm, preferred_element_type=F32)
        dwi_ref[...] += jnp.dot(xct, dpim, preferred_element_type=F32)
        dbr_ref[...] += jnp.sum(dpr, axis=0, keepdims=True)
        dbi_ref[...] += jnp.sum(dpi, axis=0, keepdims=True)
        dl_ref[...] += jnp.sum(dla * (-LRU_C * r), axis=0, keepdims=True) * (-_sigmoid(-lam))

    tile = pl.BlockSpec((tr, bw), lambda n, i: (i, n))
    vec = pl.BlockSpec((1, bw), lambda n, i: (0, n))
    wblk = pl.BlockSpec((None, bw, bw), lambda n, i: (n, 0, 0))
    return hbm_call(
        body, name=name, grid=(nblk, S // tr),
        in_specs=[tile, tile, pl.BlockSpec((tr, bw), lambda n, i: (jnp.maximum(i - 1, 0), n)), tile, tile, tile, wblk, wblk, vec],
        out_specs=[tile, wblk, wblk, vec, vec, vec],
        out_shape=[jax.ShapeDtypeStruct((S, D), F32), jax.ShapeDtypeStruct((nblk, bw, bw), F32), jax.ShapeDtypeStruct((nblk, bw, bw), F32),
                   jax.ShapeDtypeStruct((1, D), F32), jax.ShapeDtypeStruct((1, D), F32), jax.ShapeDtypeStruct((1, D), F32)],
        compiler_params=_params(("parallel", "arbitrary")),
    )(gt, hs, hs, xc, r, ig, w_rg, w_ig, lam)


def rg_conv_bwd(proj, dxc, conv_w, name):
    S, D = dxc.shape
    bw = min(SCAN_COLS, D)
    tr = min(RG_ROWS, S)
    nsteps = S // tr

    def body(d_ref, dn_ref, xr_ref, xp_ref, cw_ref, dxr_ref, dcw_ref, dcb_ref):
        step = pl.program_id(1)
        d, xr = d_ref[...], xr_ref[...]
        row = lax.broadcasted_iota(jnp.int32, d.shape, 0)
        dxr = d * cw_ref[CONV_WIDTH - 1:CONV_WIDTH, :]
        for k in range(CONV_WIDTH - 1):
            dxr = dxr + _shift_up(d, dn_ref[...], CONV_WIDTH - 1 - k, row, step == nsteps - 1, tr) * cw_ref[k:k + 1, :]
        dxr_ref[...] = dxr.astype(dxr_ref.dtype)

        @pl.when(step == 0)
        def _():
            dcw_ref[...] = jnp.zeros_like(dcw_ref)
            dcb_ref[...] = jnp.zeros_like(dcb_ref)

        for k in range(CONV_WIDTH - 1):
            xs = _shift_down(xr, xp_ref[...], CONV_WIDTH - 1 - k, row, step == 0)
            dcw_ref[k:k + 1, :] += jnp.sum(d * xs, axis=0, keepdims=True)
        dcw_ref[CONV_WIDTH - 1:CONV_WIDTH, :] += jnp.sum(d * xr, axis=0, keepdims=True)
        dcb_ref[...] += jnp.sum(d, axis=0, keepdims=True)

    tile = pl.BlockSpec((tr, bw), lambda n, i: (i, n))
    cwb = pl.BlockSpec((CONV_WIDTH, bw), lambda n, i: (0, n))
    return hbm_call(
        body, name=name, grid=(D // bw, nsteps),
        in_specs=[tile, pl.BlockSpec((tr, bw), lambda n, i: (jnp.minimum(i + 1, nsteps - 1), n)), tile,
                  pl.BlockSpec((tr, bw), lambda n, i: (jnp.maximum(i - 1, 0), n)), cwb],
        out_specs=[tile, cwb, pl.BlockSpec((1, bw), lambda n, i: (0, n))],
        out_shape=[jax.ShapeDtypeStruct((S, D), MXU_DTYPE), jax.ShapeDtypeStruct((CONV_WIDTH, D), F32), jax.ShapeDtypeStruct((1, D), F32)],
        compiler_params=_params(("parallel", "arbitrary")),
    )(dxc, dxc, proj, proj, conv_w)


def rope_table(S):
    half = ROT_DIM // 2
    pos = jnp.arange(S, dtype=F32)
    inv = ROPE_THETA ** (-jnp.arange(0, ROT_DIM, 2, dtype=F32) / ROT_DIM)
    ang = pos[:, None] * inv[None, :]
    cos, sin = jnp.cos(ang), jnp.sin(ang)
    zero = jnp.zeros((S, HEAD_DIM - ROT_DIM), F32)
    c = jnp.concatenate([cos, cos, zero + 1.0], axis=1)
    a = jnp.concatenate([-sin, jnp.zeros((S, half), F32), zero], axis=1)
    b = jnp.concatenate([jnp.zeros((S, half), F32), sin, zero], axis=1)
    return jnp.stack([jnp.tile(t, (1, LANES // HEAD_DIM)) for t in (c, a, b)])


def _rope(t, tab):
    half = ROT_DIM // 2
    return t * tab[0] + pltpu.roll(t, LANES - half, 1) * tab[1] + pltpu.roll(t, half, 1) * tab[2]


def _rope_t(d, tab):
    half = ROT_DIM // 2
    return d * tab[0] + pltpu.roll(d * tab[1], half, 1) + pltpu.roll(d * tab[2], LANES - half, 1)


def _dup_head(t, hk, lo):
    sw = pltpu.roll(t, HEAD_DIM, 1)
    return jnp.where(lo, t, sw) if hk == 0 else jnp.where(lo, sw, t)


def _attn_common(n, sink_ref, q_ref, kp_ref, kc_ref, vp_ref, vc_ref, tc_ref, tp_ref, hk, pairs):
    tq = (tc_ref[0], tc_ref[1], tc_ref[2])
    tp = (tp_ref[0], tp_ref[1], tp_ref[2])
    lo = lax.broadcasted_iota(jnp.int32, (WINDOW, LANES), 1) < HEAD_DIM
    lo2 = lax.broadcasted_iota(jnp.int32, (2 * WINDOW, LANES), 1) < HEAD_DIM
    kband = jnp.concatenate([_rope(kp_ref[...], tp), _rope(kc_ref[...], tq)], axis=0)
    vband = jnp.concatenate([vp_ref[...], vc_ref[...]], axis=0)
    kd = _dup_head(kband, hk, lo2).astype(MXU_DTYPE)
    vd = _dup_head(vband, hk, lo2).astype(MXU_DTYPE)
    rows, sks = [], []
    for j in range(pairs):
        col = hk * pairs + j
        qp = _rope(q_ref[:, col * LANES:(col + 1) * LANES], tq)
        rows += [jnp.where(lo, qp, 0.0), jnp.where(lo, 0.0, qp)]
        sks += [jnp.full((WINDOW, 1), sink_ref[2 * col], F32), jnp.full((WINDOW, 1), sink_ref[2 * col + 1], F32)]
    qg = jnp.concatenate(rows, axis=0)
    sk = jnp.concatenate(sks, axis=0)
    G = 2 * pairs * WINDOW
    own = lax.broadcasted_iota(jnp.int32, (G, WINDOW), 1) <= (lax.broadcasted_iota(jnp.int32, (G, WINDOW), 0) & (WINDOW - 1))
    s = lax.dot_general(qg.astype(MXU_DTYPE), kd, (((1,), (1,)), ((), ())), preferred_element_type=F32) * (HEAD_DIM ** -0.5)
    s = jnp.where(own, s[:, WINDOW:], s[:, :WINDOW] + jnp.where(n > 0, 0.0, NEG_INF))
    m = jnp.maximum(jnp.max(s, axis=1, keepdims=True), sk)
    e = jnp.exp(s - m)
    es = jnp.exp(sk - m)
    inv = 1.0 / (jnp.sum(e, axis=1, keepdims=True) + es)
    return qg, kd, vd, e * inv, es * inv, own, lo, lo2, tq, tp


def _unfold_band(t, own):
    return jnp.concatenate([jnp.where(own, 0.0, t), jnp.where(own, t, 0.0)], axis=1)


def _attn_specs(D, NB):
    kcol = 3 * D // LANES
    q = pl.BlockSpec((WINDOW, D), lambda n: (n, 2))
    kc = pl.BlockSpec((WINDOW, LANES), lambda n: (n, kcol))
    kp = pl.BlockSpec((WINDOW, LANES), lambda n: (jnp.maximum(n - 1, 0), kcol))
    vc = pl.BlockSpec((WINDOW, LANES), lambda n: (n, kcol + 1))
    vp = pl.BlockSpec((WINDOW, LANES), lambda n: (jnp.maximum(n - 1, 0), kcol + 1))
    tc = pl.BlockSpec((3, WINDOW, LANES), lambda n: (0, n, 0))
    tp = pl.BlockSpec((3, WINDOW, LANES), lambda n: (0, jnp.maximum(n - 1, 0), 0))
    sink = pl.BlockSpec(memory_space=pltpu.SMEM)
    return [sink, q, kp, kc, vp, vc, tc, tp]


def attn_fwd(proj, sinks, tab, D, name):
    S = proj.shape[0]
    NB = S // WINDOW
    pairs = D // HEAD_DIM // N_KV_HEADS // 2

    def body(sink_ref, q_ref, kp_ref, kc_ref, vp_ref, vc_ref, tc_ref, tp_ref, o_ref):
        n = pl.program_id(0)
        for hk in range(N_KV_HEADS):
            _, _, vd, p, _, own, lo, _, _, _ = _attn_common(n, sink_ref, q_ref, kp_ref, kc_ref, vp_ref, vc_ref, tc_ref, tp_ref, hk, pairs)
            o = jnp.dot(_unfold_band(p, own).astype(MXU_DTYPE), vd, preferred_element_type=F32)
            for j in range(pairs):
                col = hk * pairs + j
                oa = o[(2 * j) * WINDOW:(2 * j + 1) * WINDOW]
                ob = o[(2 * j + 1) * WINDOW:(2 * j + 2) * WINDOW]
                o_ref[:, col * LANES:(col + 1) * LANES] = jnp.where(lo, oa, ob)

    return hbm_call(
        body, name=name, grid=(NB,), in_specs=_attn_specs(D, NB),
        out_specs=pl.BlockSpec((WINDOW, D), lambda n: (n, 0)), out_shape=jax.ShapeDtypeStruct((S, D), F32),
        compiler_params=_params(("parallel",)),
    )(sinks, proj, proj, proj, proj, proj, tab, tab)


def attn_bwd(proj, sinks, tab, o, do, D, name):
    S = proj.shape[0]
    NB = S // WINDOW
    pairs = D // HEAD_DIM // N_KV_HEADS // 2

    def body(sink_ref, q_ref, kp_ref, kc_ref, vp_ref, vc_ref, tc_ref, tp_ref, o_ref, do_ref, dq_ref, dk_ref, dv_ref, ds_ref):
        n = pl.program_id(0)

        @pl.when(n == 0)
        def _():
            ds_ref[...] = jnp.zeros_like(ds_ref)

        lane1 = lax.broadcasted_iota(jnp.int32, (1, LANES), 1)
        dsink = jnp.zeros((1, LANES), F32)
        dkt = dvt = None
        for hk in range(N_KV_HEADS):
            qg, kd, vd, p, ps, own, lo, lo2, tq, tp = _attn_common(n, sink_ref, q_ref, kp_ref, kc_ref, vp_ref, vc_ref, tc_ref, tp_ref, hk, pairs)
            dos, os_ = [], []
            for j in range(pairs):
                col = hk * pairs + j
                dop = do_ref[:, col * LANES:(col + 1) * LANES]
                op = o_ref[:, col * LANES:(col + 1) * LANES]
                dos += [jnp.where(lo, dop, 0.0), jnp.where(lo, 0.0, dop)]
                os_ += [jnp.where(lo, op, 0.0), jnp.where(lo, 0.0, op)]
            dog = jnp.concatenate(dos, axis=0)
            og = jnp.concatenate(os_, axis=0)
            dogm = dog.astype(MXU_DTYPE)
            dp = lax.dot_general(dogm, vd, (((1,), (1,)), ((), ())), preferred_element_type=F32)
            dp = jnp.where(own, dp[:, WINDOW:], dp[:, :WINDOW])
            dr = jnp.sum(dog * og, axis=1, keepdims=True)
            ds = _unfold_band(p * (dp - dr) * (HEAD_DIM ** -0.5), own)
            dsm = ds.astype(MXU_DTYPE)
            dqg = jnp.dot(dsm, kd, preferred_element_type=F32)
            dkd = jnp.dot(ds.T.astype(MXU_DTYPE), qg.astype(MXU_DTYPE), preferred_element_type=F32)
            dvd = jnp.dot(_unfold_band(p, own).T.astype(MXU_DTYPE), dogm, preferred_element_type=F32)
            dkf = dkd + pltpu.roll(dkd, HEAD_DIM, 1)
            dvf = dvd + pltpu.roll(dvd, HEAD_DIM, 1)
            if hk == 0:
                dkt, dvt = dkf, dvf
            else:
                dkt, dvt = jnp.where(lo2, dkt, dkf), jnp.where(lo2, dvt, dvf)
            sd = ps * dr
            for j in range(pairs):
                col = hk * pairs + j
                dqa = dqg[(2 * j) * WINDOW:(2 * j + 1) * WINDOW]
                dqb = dqg[(2 * j + 1) * WINDOW:(2 * j + 2) * WINDOW]
                dq_ref[:, col * LANES:(col + 1) * LANES] = _rope_t(jnp.where(lo, dqa, dqb), tq).astype(dq_ref.dtype)
                for t in range(2):
                    part = sd[(2 * j + t) * WINDOW:(2 * j + t + 1) * WINDOW]
                    val = jnp.sum(part, axis=0, keepdims=True)
                    dsink = dsink - jnp.where(lane1 == 2 * col + t, val, 0.0)
        dk_ref[...] = jnp.concatenate([_rope_t(dkt[:WINDOW], tp), _rope_t(dkt[WINDOW:], tq)], axis=0)
        dv_ref[...] = dvt
        ds_ref[...] += dsink

    blk = pl.BlockSpec((WINDOW, D), lambda n: (n, 0))
    band = pl.BlockSpec((None, 2 * WINDOW, LANES), lambda n: (n, 0, 0))
    return hbm_call(
        body, name=name, grid=(NB,), in_specs=_attn_specs(D, NB) + [blk, blk],
        out_specs=[blk, band, band, pl.BlockSpec((1, LANES), lambda n: (0, 0))],
        out_shape=[jax.ShapeDtypeStruct((S, D), MXU_DTYPE), jax.ShapeDtypeStruct((NB, 2 * WINDOW, LANES), F32),
                   jax.ShapeDtypeStruct((NB, 2 * WINDOW, LANES), F32), jax.ShapeDtypeStruct((1, LANES), F32)],
        compiler_params=_params(("arbitrary",)),
    )(sinks, proj, proj, proj, proj, proj, tab, tab, o, do)


def band_fold(dkb, dvb, name):
    NB = dkb.shape[0]
    k4 = dkb.reshape(NB, 2, WINDOW, LANES)
    v4 = dvb.reshape(NB, 2, WINDOW, LANES)

    def body(kc_ref, kn_ref, vc_ref, vn_ref, dk_ref, dv_ref):
        more = pl.program_id(0) < NB - 1
        dk_ref[...] = (kc_ref[...] + jnp.where(more, kn_ref[...], 0.0)).astype(dk_ref.dtype)
        dv_ref[...] = (vc_ref[...] + jnp.where(more, vn_ref[...], 0.0)).astype(dv_ref.dtype)

    cur = pl.BlockSpec((None, None, WINDOW, LANES), lambda n: (n, 1, 0, 0))
    nxt = pl.BlockSpec((None, None, WINDOW, LANES), lambda n: (jnp.minimum(n + 1, NB - 1), 0, 0, 0))
    out = pl.BlockSpec((WINDOW, LANES), lambda n: (n, 0))
    sds = jax.ShapeDtypeStruct((NB * WINDOW, LANES), MXU_DTYPE)
    return hbm_call(body, name=name, grid=(NB,), in_specs=[cur, nxt, cur, nxt], out_specs=[out, out], out_shape=[sds, sds],
                          compiler_params=_params(("parallel",)))(k4, k4, v4, v4)


CROSS_ROWS = 512


def _cross_probs(q, k, scale):
    s = lax.dot_general(q.astype(MXU_DTYPE), k.astype(MXU_DTYPE), (((1,), (1,)), ((), ())), preferred_element_type=F32) * scale
    e = jnp.exp(s - jnp.max(s, axis=1, keepdims=True))
    return e / jnp.sum(e, axis=1, keepdims=True)


def cross_fwd(qc, kv, name):
    S, D = qc.shape
    M = kv.shape[0]
    hd = D // CROSS_HEADS
    tq = min(CROSS_ROWS, S)

    def body(q_ref, kv_ref, o_ref):
        for h in range(CROSS_HEADS):
            p = _cross_probs(q_ref[:, h * hd:(h + 1) * hd], kv_ref[:, h * hd:(h + 1) * hd], hd ** -0.5)
            v = kv_ref[:, D + h * hd:D + (h + 1) * hd].astype(MXU_DTYPE)
            o_ref[:, h * hd:(h + 1) * hd] = jnp.dot(p.astype(MXU_DTYPE), v, preferred_element_type=F32).astype(o_ref.dtype)

    return hbm_call(
        body, name=name, grid=(S // tq,), in_specs=[pl.BlockSpec((tq, D), lambda i: (i, 0)), pl.BlockSpec((M, 2 * D), lambda i: (0, 0))],
        out_specs=pl.BlockSpec((tq, D), lambda i: (i, 0)), out_shape=jax.ShapeDtypeStruct((S, D), MXU_DTYPE),
        compiler_params=_params(("parallel",)),
    )(qc, kv)


def cross_bwd(qc, kv, do, name):
    S, D = qc.shape
    M = kv.shape[0]
    hd = D // CROSS_HEADS
    tq = min(CROSS_ROWS, S)

    def body(q_ref, kv_ref, do_ref, dq_ref, dkv_ref):
        @pl.when(pl.program_id(0) == 0)
        def _():
            dkv_ref[...] = jnp.zeros_like(dkv_ref)

        for h in range(CROSS_HEADS):
            q = q_ref[:, h * hd:(h + 1) * hd]
            k = kv_ref[:, h * hd:(h + 1) * hd]
            v = kv_ref[:, D + h * hd:D + (h + 1) * hd].astype(MXU_DTYPE)
            dom = do_ref[:, h * hd:(h + 1) * hd].astype(MXU_DTYPE)
            p = _cross_probs(q, k, hd ** -0.5)
            dp = lax.dot_general(dom, v, (((1,), (1,)), ((), ())), preferred_element_type=F32)
            ds = p * (dp - jnp.sum(p * dp, axis=1, keepdims=True)) * (hd ** -0.5)
            dq_ref[:, h * hd:(h + 1) * hd] = jnp.dot(ds.astype(MXU_DTYPE), k.astype(MXU_DTYPE),
                                                     preferred_element_type=F32).astype(dq_ref.dtype)
            dkv_ref[:, h * hd:(h + 1) * hd] += jnp.dot(ds.T.astype(MXU_DTYPE), q.astype(MXU_DTYPE), preferred_element_type=F32)
            dkv_ref[:, D + h * hd:D + (h + 1) * hd] += jnp.dot(p.T.astype(MXU_DTYPE), dom, preferred_element_type=F32)

    row = pl.BlockSpec((tq, D), lambda i: (i, 0))
    full = pl.BlockSpec((M, 2 * D), lambda i: (0, 0))
    return hbm_call(
        body, name=name, grid=(S // tq,), in_specs=[row, full, row], out_specs=[row, full],
        out_shape=[jax.ShapeDtypeStruct((S, D), MXU_DTYPE), jax.ShapeDtypeStruct((M, 2 * D), F32)],
        compiler_params=_params(("arbitrary",)),
    )(qc, kv, do)


def adamw(w, g, m, v, name, layers=None, into=None):
    shape = w.shape
    cols = shape[-1]
    lead = shape[0] if len(shape) > 2 else 1
    rows = int(np.prod(shape[:-1])) // lead
    w2, g2, m2, v2 = (t.reshape(lead, rows, cols) for t in (w, g, m, v))
    tr = _divisors(rows, SUBLANES, max(SUBLANES, (1 << 20) // (cols * 4) // SUBLANES * SUBLANES))[0]
    lo, hi = layers or (0, lead)
    done = [t.reshape(lead, rows, cols) for t in into] if into else []

    def body(w_ref, g_ref, m_ref, v_ref, *refs):
        d_ref, mo_ref, vo_ref, go_ref = refs[len(done):]
        gg = g_ref[...]
        mn = ADAM_B1 * m_ref[...] + (1.0 - ADAM_B1) * gg
        vn = ADAM_B2 * v_ref[...] + (1.0 - ADAM_B2) * (gg * gg)
        m_hat = mn / (1.0 - ADAM_B1 ** ADAM_STEP)
        v_hat = vn / (1.0 - ADAM_B2 ** ADAM_STEP)
        d_ref[...] = -ADAM_LR * (m_hat / (jnp.sqrt(v_hat) + ADAM_EPS) + ADAM_WD * w_ref[...])
        mo_ref[...] = mn
        vo_ref[...] = vn
        go_ref[...] = gg

    blk = pl.BlockSpec((None, tr, cols), lambda l, i: (l + lo, i, 0))
    sds = jax.ShapeDtypeStruct((lead, rows, cols), F32)
    d, mn, vn, go = hbm_call(body, name=name, grid=(hi - lo, rows // tr), in_specs=[blk] * 4 + [pl.BlockSpec(memory_space=pl.ANY)] * len(done),
                             out_specs=[blk] * 4, out_shape=[sds] * 4, input_output_aliases={4 + k: k for k in range(len(done))},
                             compiler_params=_params(("parallel", "parallel")))(w2, g2, m2, v2, *done)
    return d.reshape(shape), mn.reshape(shape), vn.reshape(shape), go.reshape(shape)


def sum_devices(parts, name):
    n, rows, cols = parts.shape

    def body(p_ref, o_ref):
        acc = p_ref[0]
        for k in range(1, n):
            acc = acc + p_ref[k]
        o_ref[...] = acc

    return pl.pallas_call(body, name=name, in_specs=[pl.BlockSpec(memory_space=pltpu.VMEM)],
                          out_specs=pl.BlockSpec(memory_space=pltpu.VMEM), out_shape=jax.ShapeDtypeStruct((rows, cols), F32))(parts)


HBM_SPEC = pl.BlockSpec(memory_space=pltpu.HBM)


def _place():
    return lax.axis_index("x"), lax.axis_index("y"), lax.axis_index("c")


def _remote(src, dst, send_sems, recv_sems, k, to):
    return pltpu.make_async_remote_copy(src_ref=src, dst_ref=dst, send_sem=send_sems.at[k], recv_sem=recv_sems.at[k],
                                        device_id=to, device_id_type=MESH_ID)


SEM_SPEC = pl.BlockSpec(memory_space=pltpu.SEMAPHORE)
ANY_SPEC = pl.BlockSpec(memory_space=pl.ANY)
SPLIT_COPY = pltpu.CompilerParams(has_side_effects=pltpu.SideEffectType.DATAFLOW_SIDE_EFFECTING)


def _in_hbm(arrays):
    return [pltpu.with_memory_space_constraint(a, pltpu.HBM) for a in arrays]


def _split_start(copies, sources, lands, after, n_sems, name):
    n = len(sources)

    def body(*refs):
        for cp in copies(refs[:n], refs[n:2 * n], refs[2 * n + 1], refs[2 * n + 2]):
            cp.start()
        refs[-1][...] = jnp.zeros_like(refs[-1])

    through = [pltpu.HBM(a.shape, a.dtype) for a in list(sources) + list(lands)]
    outs = pl.pallas_call(
        body, name=name, in_specs=[HBM_SPEC] * (2 * n) + [ANY_SPEC],
        out_specs=[SEM_SPEC, SEM_SPEC] + [HBM_SPEC] * (2 * n) + [pl.BlockSpec(memory_space=pltpu.VMEM)],
        out_shape=[pltpu.SemaphoreType.DMA((n_sems,)), pltpu.SemaphoreType.DMA((n_sems,))] + through
        + [jax.ShapeDtypeStruct((SUBLANES, LANES), F32)],
        input_output_aliases={i: 2 + i for i in range(2 * n)}, compiler_params=SPLIT_COPY,
    )(*_in_hbm(sources), *_in_hbm(lands), after)
    return outs[0], outs[1], outs[2:2 + n], outs[2 + n:2 + 2 * n], outs[-1]


def _split_wait(copies, send_sems, recv_sems, sources, lands, after, name):
    n = len(sources)

    def body(*refs):
        for cp in copies(refs[:n], refs[n:2 * n], refs[2 * n], refs[2 * n + 1]):
            cp.wait_send()
            cp.wait_recv()

    through = [pltpu.HBM(a.shape, a.dtype) for a in list(sources) + list(lands)]
    outs = pl.pallas_call(
        body, name=name, in_specs=[HBM_SPEC] * (2 * n) + [SEM_SPEC, SEM_SPEC, ANY_SPEC], out_specs=[HBM_SPEC] * (2 * n),
        out_shape=through, input_output_aliases={i: i for i in range(2 * n)}, compiler_params=SPLIT_COPY,
    )(*sources, *lands, send_sems, recv_sems, after)
    return outs[:n], outs[n:]


def _chip_slab(land, slot, rows):
    return land.at[slot, rows] if len(land.shape) == 3 else land.at[rows, slot]


def _gather_copies(w_refs, land_refs, send_sems, recv_sems):
    n = len(w_refs)
    x, y, c = _place()
    chips = [(1 - x, y), (x, 1 - y), (1 - x, 1 - y)]
    cps = []
    for a in range(n):
        hr = w_refs[a].shape[0] // 2
        mine, every = pl.ds(c * hr, hr), pl.ds(0, 2 * hr)
        cps.append(_remote(w_refs[a], _chip_slab(land_refs[a], 2 * x + y, every), send_sems, recv_sems, 3 * n + a, (x, y, 1 - c)))
        for k, chip in enumerate(chips):
            cps.append(_remote(w_refs[a].at[mine], _chip_slab(land_refs[a], 2 * x + y, mine), send_sems, recv_sems, 3 * a + k, (*chip, c)))
    return cps


def gather_start(shards, after, name):
    lands = [lax.empty(s.shape[:-2] + (N_CHIPS,) + s.shape[-2:], s.dtype) for s in shards]
    return _split_start(_gather_copies, shards, lands, after, 4 * len(shards), name)


def gather_wait(state, after, name):
    send_sems, recv_sems, sources, lands, _ = state
    return _split_wait(_gather_copies, send_sems, recv_sems, sources, lands, after, name)[1]


def gather_pass(lands, name):
    n = len(lands)

    def body(*refs):
        out_refs, send_sems, recv_sems = refs[n:2 * n], refs[2 * n], refs[2 * n + 1]
        x, y, c = _place()
        chips = [(1 - x, y), (x, 1 - y), (1 - x, 1 - y)]
        sent = []
        for a in range(n):
            hr = out_refs[a].shape[0 if len(out_refs[a].shape) == 4 else 1] // 2
            for k, (px, py) in enumerate(chips):
                landed = _chip_slab(out_refs[a], 2 * px + py, pl.ds(c * hr, hr))
                sent.append(_remote(landed, landed, send_sems, recv_sems, 3 * a + k, (x, y, 1 - c)))
        for cp in sent:
            cp.start()
        for a in range(n):
            hr = out_refs[a].shape[0 if len(out_refs[a].shape) == 4 else 1] // 2
            for k, (px, py) in enumerate(chips):
                theirs = _chip_slab(out_refs[a], 2 * px + py, pl.ds((1 - c) * hr, hr))
                _remote(theirs, theirs, send_sems, recv_sems, 3 * a + k, (x, y, 1 - c)).wait_recv()
        for cp in sent:
            cp.wait_send()

    return hbm_call(
        body, name=name, in_specs=[HBM_SPEC] * n, out_specs=[HBM_SPEC] * n,
        out_shape=[jax.ShapeDtypeStruct(a.shape, a.dtype) for a in lands], input_output_aliases={a: a for a in range(n)},
        scratch_shapes=[pltpu.SemaphoreType.DMA((3 * n,))] * 2,
    )(*lands)


def _scatter_copies(t_refs, land_refs, send_sems, recv_sems):
    x, y, c = _place()
    chips = [(1 - x, y), (x, 1 - y), (1 - x, 1 - y)]
    return [_remote(t_refs[a].at[:, 2 * px + py], land_refs[a].at[:, k], send_sems, recv_sems, 3 * a + k, (px, py, c))
            for a in range(len(t_refs)) for k, (px, py) in enumerate(chips)]


def scatter_start(parts, after, name):
    lands = [lax.empty((t.shape[0], N_CHIPS - 1) + t.shape[2:], t.dtype) for t in parts]
    return _split_start(_scatter_copies, parts, lands, after, 3 * len(parts), name)


def scatter_wait(state, after, name):
    send_sems, recv_sems, sources, lands, _ = state
    return _split_wait(_scatter_copies, send_sems, recv_sems, sources, lands, after, name)


def swap_sibling(parts, name):
    n = len(parts)

    def body(*refs):
        v_refs, out_refs, send_sems, recv_sems = refs[:n], refs[n:2 * n], refs[2 * n], refs[2 * n + 1]
        x, y, c = _place()
        cps = []
        for a in range(n):
            hr = v_refs[a].shape[2] // 2
            cps.append(_remote(v_refs[a].at[:, :, pl.ds((1 - c) * hr, hr)], out_refs[a], send_sems, recv_sems, a, (x, y, 1 - c)))
        for cp in cps:
            cp.start()
        for cp in cps:
            cp.wait()

    return hbm_call(
        body, name=name, in_specs=[HBM_SPEC] * n, out_specs=[HBM_SPEC] * n,
        out_shape=[jax.ShapeDtypeStruct(v.shape[:2] + (v.shape[2] // 2, v.shape[3]), v.dtype) for v in parts],
        scratch_shapes=[pltpu.SemaphoreType.DMA((n,))] * 2,
    )(*parts)


def join_halves(halves, layer, name):
    n = len(halves)

    def body(*refs):
        out_refs, send_sems, recv_sems = refs[n:2 * n], refs[2 * n], refs[2 * n + 1]
        x, y, c = _place()
        cps = []
        for a in range(n):
            hr = out_refs[a].shape[1] // 2
            mine = out_refs[a].at[layer, pl.ds(c * hr, hr)]
            cps.append(_remote(mine, mine, send_sems, recv_sems, a, (x, y, 1 - c)))
        for cp in cps:
            cp.start()
        for a in range(n):
            hr = out_refs[a].shape[1] // 2
            theirs = out_refs[a].at[layer, pl.ds((1 - c) * hr, hr)]
            _remote(theirs, theirs, send_sems, recv_sems, a, (x, y, 1 - c)).wait_recv()
        for cp in cps:
            cp.wait_send()

    return hbm_call(
        body, name=name, in_specs=[HBM_SPEC] * n, out_specs=[HBM_SPEC] * n,
        out_shape=[jax.ShapeDtypeStruct(f.shape, f.dtype) for f in halves], input_output_aliases={a: a for a in range(n)},
        scratch_shapes=[pltpu.SemaphoreType.DMA((n,))] * 2,
    )(*halves)


def gather_devices(v, name, after=()):
    def body(v_ref, *refs):
        out_ref, send_sems, recv_sems, local_sem = refs[len(after):]
        x, y, c = _place()
        me = 4 * x + 2 * y + c
        own = pltpu.make_async_copy(v_ref, out_ref.at[me], local_sem)
        own.start()
        peers = [((x + dx) % 2, (y + dy) % 2, (c + dc) % 2) for dx in (0, 1) for dy in (0, 1) for dc in (0, 1)][1:]
        sent = []
        for k, peer in enumerate(peers):
            cp = pltpu.make_async_remote_copy(src_ref=v_ref, dst_ref=out_ref.at[me], send_sem=send_sems.at[k], recv_sem=recv_sems.at[k],
                                              device_id=peer, device_id_type=MESH_ID)
            cp.start()
            sent.append(cp)
        for k, (px, py, pc) in enumerate(peers):
            slot = out_ref.at[4 * px + 2 * py + pc]
            pltpu.make_async_remote_copy(src_ref=slot, dst_ref=slot, send_sem=send_sems.at[k], recv_sem=recv_sems.at[k],
                                         device_id=(px, py, pc), device_id_type=MESH_ID).wait_recv()
        for cp in sent:
            cp.wait_send()
        own.wait()

    vm = pl.BlockSpec(memory_space=pltpu.VMEM)
    return pl.pallas_call(body, name=name, in_specs=[vm] + [ANY_SPEC] * len(after), out_specs=vm,
                          out_shape=jax.ShapeDtypeStruct((N_DEV,) + v.shape, v.dtype),
                          scratch_shapes=[pltpu.SemaphoreType.DMA((N_DEV - 1,)), pltpu.SemaphoreType.DMA((N_DEV - 1,)),
                                          pltpu.SemaphoreType.DMA])(v, *after)


ADD_ROWS = 512


def add_pair(place, a, b, name):
    L, n, hr, cols = b.shape
    tr = _divisors(hr, 2 * SUBLANES, ADD_ROWS)[0]
    nb = hr // tr

    def body(p_ref, a_ref, b_ref, o_ref):
        del p_ref
        o_ref[...] = (a_ref[...].astype(F32) + b_ref[...].astype(F32)).astype(o_ref.dtype)

    blk = pl.BlockSpec((None, None, tr, cols), lambda l, d, i, p: (l, d, i, 0))
    grid_spec = pltpu.PrefetchScalarGridSpec(
        num_scalar_prefetch=1, grid=(L, n, nb),
        in_specs=[pl.BlockSpec((None, None, tr, cols), lambda l, d, i, p: (l, d, p[0] * nb + i, 0)), blk], out_specs=blk)
    return hbm_call(body, name=name, grid_spec=grid_spec, out_shape=jax.ShapeDtypeStruct(b.shape, b.dtype),
                          compiler_params=_params(("parallel", "parallel", "parallel")))(place, a, b)


def add_chips(place, own, others, layer, stacked, name):
    _, n, hr, cols = others.shape
    tr = _divisors(hr, 2 * SUBLANES, ADD_ROWS)[0]
    nb = hr // tr
    create = isinstance(stacked, tuple)

    def body(p_ref, own_ref, *refs):
        del p_ref
        acc = own_ref[...].astype(F32)
        for k in range(n):
            acc = acc + refs[k][...].astype(F32)
        refs[-1][...] = acc

    ins = [pl.BlockSpec((None, None, tr, cols), lambda i, p: (0, p[1], i, 0))]
    ins += [pl.BlockSpec((None, None, tr, cols), functools.partial(lambda k, i, p: (0, k, i, 0), k)) for k in range(n)]
    grid_spec = pltpu.PrefetchScalarGridSpec(num_scalar_prefetch=1, grid=(nb,), in_specs=ins + ([] if create else [ANY_SPEC]),
                                             out_specs=pl.BlockSpec((None, tr, cols), lambda i, p: (layer, p[0] * nb + i, 0)))
    shape = stacked if create else stacked.shape
    return hbm_call(body, name=name, grid_spec=grid_spec, out_shape=jax.ShapeDtypeStruct(shape, F32),
                          input_output_aliases={} if create else {n + 2: 0},
                          compiler_params=_params(("parallel",)))(place, own, *([others] * n), *([] if create else [stacked]))


def _alpha(depth):
    return (2 * depth) ** 0.25


def _wmm(a, weight, mode, name, deps=(), **more):
    arr, how = weight
    return mm(a, arr, mode, name, deps=deps, **how, **more)


def layer_fwd(h, mem, w, tab, alpha, deps=(), late=None):
    D = h.shape[1]
    proj = _wmm(h, w["w_in"], "nt", "mm_proj", deps)
    xc, r, ig, a, b = rg_gates_fwd(proj, w["conv_w"], w["conv_b"], w["w_rg"], w["b_rg"], w["w_ig"], w["b_ig"], w["lru_lambda"], "rg_gates_fwd")
    hs, y_rnn = rg_scan_fwd(proj, a, b, "rg_scan_fwd")
    y_attn = attn_fwd(proj, w["sinks"], tab, D, "attn_fwd")
    deps = ()
    if late is not None:
        rest, deps = late(y_attn)
        w = {**w, **rest}
    pr = _wmm(y_rnn, w["w_br_rnn"], "nn", "mm_br_rnn", deps)
    pa = _wmm(y_attn, w["w_br_attn"], "nn", "mm_br_attn")
    merged = merge_fwd(proj, pr, pa, "merge_fwd")
    h1, xh1, rs1 = _wmm(merged, w["w_out"], "nn", "mm_out_ln1", post_norm=(h, w["ln1_g"], w["ln1_b"], alpha))
    qc = _wmm(h1, w["cq_w"], "nn", "mm_cq", out_dtype=MXU_DTYPE)
    kv = _wmm(mem, w["ckv_w"], "nn", "mm_ckv", out_dtype=MXU_DTYPE)
    o = cross_fwd(qc, kv, "cross_fwd")
    h2, xh2, rs2 = _wmm(o, w["co_w"], "nn", "mm_co_ln2", post_norm=(h1, w["ln2_g"], w["ln2_b"], alpha))
    gu = _wmm(h2, w["ffn_wi"], "nn", "mm_ffn_wi", out_blocks=2)
    act = swiglu_fwd(gu, "swiglu_fwd")
    h3, xh3, rs3 = _wmm(act, w["ffn_wo"], "nn", "mm_ffn_wo_ln3", post_norm=(h2, w["ln3_g"], w["ln3_b"], alpha))
    saved = dict(h=h, proj=proj, xc=xc, r=r, ig=ig, a=a, hs=hs, y_rnn=y_rnn, y_attn=y_attn, pr=pr, pa=pa, xh1=xh1, rs1=rs1, h1=h1,
                 qc=qc, kv=kv, o=o, xh2=xh2, rs2=rs2, h2=h2, gu=gu, xh3=xh3, rs3=rs3)
    return h3, saved, w


def layer_bwd(dh, mem, w, s, tab, alpha, deps=(), halfway=None):
    D = dh.shape[1]
    g = {}
    wg = dict(out_dtype=MXU_DTYPE)
    dz3, g["ln3_g"], g["ln3_b"] = ln_bwd(dh, None, s["xh3"], s["rs3"], w["ln3_g"], 1.0, "ln3_bwd")
    act = swiglu_fwd(s["gu"], "swiglu_refwd")
    g["ffn_wo"] = mm(act, dz3, "tn", "mm_d_ffn_wo", deps=deps, **wg)
    dact = _wmm(dz3, w["ffn_wo"], "nt", "mm_dact")
    dgu = swiglu_bwd(s["gu"], dact, "swiglu_bwd")
    g["ffn_wi"] = mm(s["h2"], dgu, "tn", "mm_d_ffn_wi", b_blocks=2, out_blocks=N_CHIPS, **wg)
    dh2 = _wmm(dgu, w["ffn_wi"], "nt", "mm_dh2", a_blocks=2)
    dz2, g["ln2_g"], g["ln2_b"] = ln_bwd(dz3, dh2, s["xh2"], s["rs2"], w["ln2_g"], alpha, "ln2_bwd")
    g["co_w"] = mm(s["o"], dz2, "tn", "mm_d_co", **wg)
    do = _wmm(dz2, w["co_w"], "nt", "mm_do", out_dtype=MXU_DTYPE)
    dqc, dkv = cross_bwd(s["qc"], s["kv"], do, "cross_bwd")
    g["cq_w"] = mm(s["h1"], dqc, "tn", "mm_d_cq", **wg)
    g["ckv_w"] = mm(mem, dkv, "tn", "mm_d_ckv", out_blocks=N_CHIPS, **wg)
    dh1 = _wmm(dqc, w["cq_w"], "nt", "mm_dh1")
    deps = halfway(g, dh1) if halfway is not None else ()
    dz1, g["ln1_g"], g["ln1_b"] = ln_bwd(dz2, dh1, s["xh1"], s["rs1"], w["ln1_g"], alpha, "ln1_bwd")
    merged = merge_fwd(s["proj"], s["pr"], s["pa"], "merge_refwd")
    g["w_out"] = mm(merged, dz1, "tn", "mm_d_out", deps=deps, **wg)
    dm = _wmm(dz1, w["w_out"], "nt", "mm_dmerged")
    dpr, dpa, dg_rnn, dg_attn = merge_bwd(s["proj"], s["pr"], s["pa"], dm, "merge_bwd")
    g["w_br_rnn"] = mm(s["y_rnn"], dpr, "tn", "mm_d_br_rnn", **wg)
    g["w_br_attn"] = mm(s["y_attn"], dpa, "tn", "mm_d_br_attn", **wg)
    dy_rnn = _wmm(dpr, w["w_br_rnn"], "nt", "mm_dy_rnn")
    dy_attn = _wmm(dpa, w["w_br_attn"], "nt", "mm_dy_attn")
    dq, dkb, dvb, dsink = attn_bwd(s["proj"], w["sinks"], tab, s["y_attn"], dy_attn, D, "attn_bwd")
    dk, dv = band_fold(dkb, dvb, "band_fold")
    g["sinks"] = dsink[:, :w["sinks"].shape[0]]
    dgr, gt = rg_scan_bwd(s["proj"], dy_rnn, s["hs"], s["a"], "rg_scan_bwd")
    dxc, g["w_rg"], g["w_ig"], g["b_rg"], g["b_ig"], g["lru_lambda"] = rg_gates_bwd(
        gt, s["hs"], s["xc"], s["r"], s["ig"], w["w_rg"], w["w_ig"], w["lru_lambda"], "rg_gates_bwd")
    dxr, g["conv_w"], g["conv_b"] = rg_conv_bwd(s["proj"], dxc, w["conv_w"], "rg_conv_bwd")
    dproj = jnp.concatenate([dxr, dgr, dq, dk, dv, dg_rnn, dg_attn], axis=1)
    g["w_in"] = mm(dproj, s["h"], "tn", "mm_d_in", **wg)
    dhm = _wmm(dproj, w["w_in"], "nn", "mm_dh")
    return axpby(dz1, dhm, alpha, "layer_dx"), g


def local_step(x, mem, target, depth, weights_of, grads_halfway, grads_done):
    alpha = _alpha(depth)
    tab = rope_table(x.shape[0])
    h, saved, layers = x, [], []
    for l in range(depth):
        wl, deps, late = weights_of(l, h)
        h, s, wl = layer_fwd(h, mem, wl, tab, alpha, deps, late)
        layers.append(wl)
        saved.append(s)
    dh, loss = loss_head(h, target, "loss_head")
    deps = ()
    for l in reversed(range(depth)):
        dh, g = layer_bwd(dh, mem, layers[l], saved[l], tab, alpha, deps, grads_halfway(l))
        deps = grads_done(l, g, dh)
    return loss, dh


def _pad_rows(flat):
    n = flat.shape[0]
    rows = -(-n // (LANES * SUBLANES)) * SUBLANES
    return jnp.pad(flat, (0, rows * LANES - n)).reshape(rows, LANES)


def kernel(x, mem, w_in, conv_w, conv_b, w_rg, b_rg, w_ig, b_ig, lru_lambda, w_br_rnn, w_br_attn, sinks, w_out, ln1_g, ln1_b, cq_w, ckv_w, co_w, ln2_g, ln2_b, ffn_wi, ffn_wo, ln3_g, ln3_b, loss_target, m_w_in, m_conv_w, m_conv_b, m_w_rg, m_b_rg, m_w_ig, m_b_ig, m_lru_lambda, m_w_br_rnn, m_w_br_attn, m_sinks, m_w_out, m_ln1_g, m_ln1_b, m_cq_w, m_ckv_w, m_co_w, m_ln2_g, m_ln2_b, m_ffn_wi, m_ffn_wo, m_ln3_g, m_ln3_b, v_w_in, v_conv_w, v_conv_b, v_w_rg, v_b_rg, v_w_ig, v_b_ig, v_lru_lambda, v_w_br_rnn, v_w_br_attn, v_sinks, v_w_out, v_ln1_g, v_ln1_b, v_cq_w, v_ckv_w, v_co_w, v_ln2_g, v_ln2_b, v_ffn_wi, v_ffn_wo, v_ln3_g, v_ln3_b):
    args = dict(locals())
    w = {n: args[n] for n in WEIGHTS}
    m = {n: args["m_" + n] for n in WEIGHTS}
    v = {n: args["v_" + n] for n in WEIGHTS}
    for group in (w, m, v):
        group["w_in"] = jnp.swapaxes(group["w_in"], 1, 2)
    cx, cy, cc = _place()
    chip = 2 * cx + cy
    L = w_in.shape[0]

    place = jnp.stack([cc, chip]).astype(jnp.int32)
    cw_rows = _pad_rows(conv_w.reshape(-1))
    cw_all = gather_devices(cw_rows, "gather_conv_w")[0::2]
    cw_parts = cw_all.reshape(N_CHIPS, -1)[:, :conv_w.size].reshape((N_CHIPS,) + conv_w.shape)
    conv_full = jnp.concatenate([cw_parts[k] for k in range(N_CHIPS)], axis=2)

    shards = [{n: w[n][l].astype(MXU_DTYPE) for n in BIG} for l in range(L)]
    late_names = tuple(n for n in BIG if n not in GATHER_FIRST)
    gathering = {(0, GATHER_FIRST): gather_start([shards[0][n] for n in GATHER_FIRST], cw_rows, "gather_start_0a")}
    gathering[0, late_names] = gather_start([shards[0][n] for n in late_names], gathering[0, GATHER_FIRST][4], "gather_start_0b")

    def gathered(l, names, after, tag):
        lands = gather_pass(gather_wait(gathering.pop((l, names)), after, f"gather_wait_{tag}"), f"gather_pass_{tag}")
        wl = {}
        for n, gw in zip(names, lands):
            rows_joined = gw.reshape(gw.shape[:-3] + (-1, gw.shape[-1]))
            if n in COL_BLOCKED:
                wl[n] = (gw, dict(b_blocks=N_CHIPS))
            elif n in GATE_WEIGHTS:
                wl[n] = rows_joined
            else:
                wl[n] = (rows_joined, {})
        return wl, lands

    def start_layer(l, after):
        if l >= L:
            return ()
        gathering[l, BIG] = gather_start([shards[l][n] for n in BIG], after, f"gather_start_{l}")
        return (gathering[l, BIG][4],)

    def weights_of(l, h):
        deps, late = (), None
        if l == 0:
            wl, _ = gathered(0, GATHER_FIRST, h, "0a")

            def late(after):
                rest, lands = gathered(0, late_names, after, "0b")
                return rest, start_layer(1, lands[0])
        else:
            wl, lands = gathered(l, BIG, h, str(l))
            deps = start_layer(l + 1, lands[0])
        for n in SMALL:
            wl[n] = conv_full[l] if n == "conv_w" else w[n][l] if n == "sinks" else w[n][l][None, :]
        return wl, deps, late

    def for_chips(n, g):
        if n in COL_BLOCKED:
            return g
        if n in GATE_WEIGHTS:
            nb, bw, _ = g.shape
            g = g.reshape(nb, N_CHIPS, bw // N_CHIPS, bw).transpose(1, 0, 2, 3).reshape(N_CHIPS, nb * bw // N_CHIPS, bw)
        else:
            g = g.reshape(N_CHIPS, g.shape[0] // N_CHIPS, g.shape[1])
        return g.astype(MXU_DTYPE)

    reduced, scattering, small_grads = {}, {}, [None] * L
    late_grads = tuple(n for n in BIG if n not in SCATTER_FIRST)

    def start_scatter(l, names, g, after, tag):
        partial_sums = [for_chips(n, g[n])[None] for n in names]
        from_sibling = swap_sibling(partial_sums, f"grad_to_sibling_{tag}")
        chip_sums = [add_pair(place, a, b, f"grad_add_pair_{n}_{l}") for n, a, b in zip(names, partial_sums, from_sibling)]
        scattering[l, names] = scatter_start(chip_sums, after, f"grad_scatter_start_{tag}")
        return (scattering[l, names][4],)

    def finish_layer(l, after):
        for names in [k[1] for k in list(scattering) if k[0] == l]:
            tag = str(l) if names == BIG else f"{l}{'a' if names == SCATTER_FIRST else 'b'}"
            chip_sums, from_chips = scatter_wait(scattering.pop((l, names)), after, f"grad_scatter_wait_{tag}")
            for n, own, others in zip(names, chip_sums, from_chips):
                target = reduced.get(n, (L, 2 * own.shape[2], own.shape[3]))
                reduced[n] = add_chips(place, own, others, l, target, f"grad_add_chips_{n}_{l}")
        reduced.update(zip(BIG, join_halves([reduced[n] for n in BIG], l, f"grad_join_{l}")))

    def grads_halfway(l):
        def halfway(g, after):
            return start_scatter(l, SCATTER_FIRST, g, after, f"{l}a")

        return halfway

    def grads_done(l, g, dh):
        small_grads[l] = {n: g[n] for n in SMALL}
        deps = start_scatter(l, late_grads, g, dh, f"{l}b")
        if 1 < l + 1 < L:
            finish_layer(l + 1, dh)
        return deps

    loss11, dx = local_step(x[0], mem[0], loss_target[0], L, weights_of, grads_halfway, grads_done)
    loss = lax.psum(loss11[0, 0], ("x", "y", "c"))

    first = min(2, L)
    updated = {}
    if first < L:
        for n in BIG:
            updated[n] = adamw(w[n], reduced[n].reshape(w[n].shape), m[n], v[n], f"adamw_{n}_upper", layers=(first, L))
    behind = (jnp.stack([updated[n][0][(0,) * w[n].ndim] for n in updated]),) if updated else ()
    small_full = {n: jnp.stack([gl[n] for gl in small_grads]).reshape(w[n].shape[:1] + ((CONV_WIDTH, -1) if n == "conv_w" else (-1,)))
                  for n in SMALL}
    small_flat = jnp.concatenate([small_full[n].reshape(-1) for n in SMALL])
    small_sum = sum_devices(gather_devices(_pad_rows(small_flat), "gather_small_grads", behind), "sum_small_grads").reshape(-1)
    delta, new_m, new_v, grad = {}, {}, {}, {}
    off = 0
    for n in SMALL:
        gfull = small_sum[off:off + small_full[n].size].reshape(small_full[n].shape)
        off += small_full[n].size
        if n == "conv_w":
            width = conv_w.shape[2]
            gfull = lax.dynamic_slice_in_dim(gfull, chip * width, width, axis=2)
        delta[n], new_m[n], new_v[n], grad[n] = adamw(w[n], gfull, m[n], v[n], "adamw_" + n)
    after = jnp.stack([delta[n][(0,) * delta[n].ndim] for n in SMALL])
    for l in reversed(range(first)):
        finish_layer(l, after)

    for n in BIG:
        some = dict(layers=(0, first), into=updated[n]) if updated else {}
        delta[n], new_m[n], new_v[n], grad[n] = adamw(w[n], reduced[n].reshape(w[n].shape), m[n], v[n], "adamw_" + n, **some)
    for group in (delta, new_m, new_v, grad):
        group["w_in"] = jnp.swapaxes(group["w_in"], 1, 2)
    return (loss, dx[None], *[grad[n] for n in WEIGHTS], *[delta[n] for n in WEIGHTS], *[new_m[n] for n in WEIGHTS],
            *[new_v[n] for n in WEIGHTS])
```

```python
import functools
import math

import jax
import jax.numpy as jnp
import numpy as np
from jax import lax
from jax.experimental import pallas as pl
from jax.experimental.pallas import tpu as pltpu

F32 = jnp.float32
BF16 = jnp.bfloat16
MXU_DTYPE = BF16

HEAD_DIM = 64
N_KV_HEADS = 2
WINDOW = 128
ROT_DIM = HEAD_DIM // 4
ROPE_THETA = 500000.0
CROSS_HEADS = 4
RNN_BLOCKS = 4
CONV_WIDTH = 4
LRU_C = 8.0
LN_EPS = 1e-5
NEG_INF = -1e30
ADAM_LR = 0.001
ADAM_B1 = 0.9
ADAM_B2 = 0.999
ADAM_EPS = 1e-08
ADAM_WD = 0.01
ADAM_STEP = 10

VMEM_BYTES_V7X = 64 * 1024 * 1024
VMEM_BLOCK_BUDGET = 36 * 1024 * 1024
LANES = 128
SUBLANES = 8

MESH_ID = pl.DeviceIdType.MESH
N_CHIPS = 4
N_DEV = 8

BIG = ("w_in", "w_rg", "w_ig", "w_br_rnn", "w_br_attn", "w_out", "cq_w", "ckv_w", "co_w", "ffn_wi", "ffn_wo")
SHARD_AXIS = {"w_in": 0, "w_rg": 1, "w_ig": 1, "w_br_rnn": 0, "w_br_attn": 0, "w_out": 0, "cq_w": 0, "ckv_w": 1,
              "co_w": 0, "ffn_wi": 1, "ffn_wo": 0}
SMALL = ("conv_w", "conv_b", "b_rg", "b_ig", "lru_lambda", "sinks", "ln1_g", "ln1_b", "ln2_g", "ln2_b", "ln3_g", "ln3_b")
WEIGHTS = ("w_in", "conv_w", "conv_b", "w_rg", "b_rg", "w_ig", "b_ig", "lru_lambda", "w_br_rnn", "w_br_attn", "sinks",
           "w_out", "ln1_g", "ln1_b", "cq_w", "ckv_w", "co_w", "ln2_g", "ln2_b", "ffn_wi", "ffn_wo", "ln3_g", "ln3_b")
GATE_WEIGHTS = ("w_rg", "w_ig")
COL_BLOCKED = ("ckv_w", "ffn_wi")
GATHER_FIRST = ("w_in", "w_rg", "w_ig")
SCATTER_FIRST = ("ffn_wo", "ffn_wi", "co_w", "cq_w", "ckv_w")


def _params(dims=None, vmem=None):
    return pltpu.CompilerParams(dimension_semantics=dims, vmem_limit_bytes=vmem)


def _vmem_limit(block_bytes, temp_bytes=0):
    want = int(2 * block_bytes + temp_bytes) + (6 << 20)
    return max(32 << 20, min(want, VMEM_BYTES_V7X - (6 << 20)))


def _divisors(n, align, cap):
    out = [d for d in range(align, min(n, cap) + 1, align) if n % d == 0]
    if n <= cap and n not in out:
        out.append(n)
    return sorted(out, reverse=True) or [n]


PIN_MIN_ELEMENTS = 1 << 18


def hbm_call(body, **kw):
    def in_hbm(s):
        return pltpu.HBM(s.shape, s.dtype) if math.prod(s.shape) >= PIN_MIN_ELEMENTS else s

    shapes = kw.pop("out_shape")
    shapes = [in_hbm(s) for s in shapes] if isinstance(shapes, (list, tuple)) else in_hbm(shapes)
    call = pl.pallas_call(body, out_shape=shapes, **kw)

    def run(*args):
        return call(*[pltpu.with_memory_space_constraint(a, pltpu.HBM) if a.size >= PIN_MIN_ELEMENTS else a for a in args])

    return run


def _sigmoid(x):
    return 1.0 / (1.0 + jnp.exp(-x))


def _gelu_parts(x):
    c = math.sqrt(2.0 / math.pi)
    u = c * (x + 0.044715 * x * x * x)
    t = jnp.tanh(u)
    return t, c * (1.0 + 3 * 0.044715 * x * x)


def _gelu(x):
    t, _ = _gelu_parts(x)
    return 0.5 * x * (1.0 + t)


def _gelu_grad(x):
    t, du = _gelu_parts(x)
    return 0.5 * (1.0 + t) + 0.5 * x * (1.0 - t * t) * du


def _neg_expm1(x):
    series = x * (1.0 + x * (0.5 + x * (1.0 / 6 + x * (1.0 / 24 + x * (1.0 / 120)))))
    return -jnp.where(x > -0.1, series, jnp.exp(x) - 1.0)


def _softplus_neg(lam):
    x = -lam
    return jnp.maximum(x, 0.0) + jnp.log1p(jnp.exp(-jnp.abs(x)))


STEP_US = 0.35
HBM_BYTES_PER_US = 2.5e6
MXU_FLOPS_PER_US = 7e8


def _layer_norm(z, g, b):
    mu = jnp.mean(z, axis=-1, keepdims=True)
    zc = z - mu
    rs = lax.rsqrt(jnp.mean(zc * zc, axis=-1, keepdims=True) + LN_EPS)
    xh = zc * rs
    return xh * g + b, xh, rs


def mm(a, b, mode, name, *, b_index=(), a_blocks=0, b_blocks=0, out_blocks=0, out_dtype=F32, deps=(), post_norm=None, plus=None):
    nlead = len(b_index) + (1 if b_blocks else 0)
    bk, bn = b.shape[nlead:]
    M, K = (a.shape[-1], a.shape[-2]) if mode == "tn" else (a.shape[-2], a.shape[-1] * max(a_blocks, 1))
    N = bk if mode == "nt" else bn * max(b_blocks, 1) if mode == "nn" or mode == "tn" else bn
    asz, bsz, osz = a.dtype.itemsize, b.dtype.itemsize, jnp.dtype(out_dtype).itemsize
    n_unit = math.gcd(N // max(out_blocks, 1), N // max(b_blocks, 1) if mode != "nt" else N)
    k_unit = math.gcd(K // max(a_blocks, 1), K // max(b_blocks, 1) if mode == "nt" else K)
    tms = _divisors(M, LANES if mode == "tn" else SUBLANES, 2048)
    tns = [N] if post_norm else _divisors(n_unit, LANES, 2048)
    tks = _divisors(k_unit, LANES, k_unit)
    best = None
    for tm in tms:
        for tn in tns:
            for tk in tks:
                nk = K // tk
                scratch = tm * tn * 4 if (nk > 1 and osz != 4) else 0
                blocks = tm * tk * asz + tn * tk * bsz + tm * tn * osz * (3 if post_norm else 1)
                temps = tm * tk * (2 + (4 if mode == "tn" else 0)) + tn * tk * 2 + tm * tn * 4 + scratch
                if 2 * blocks + temps > VMEM_BLOCK_BUDGET + (8 << 20):
                    continue
                ni, nj = M // tm, N // tn
                traffic = M * K * asz * (nj if nk > 1 else 1) + N * K * bsz * (1 if nj * nk == 1 else ni) + M * N * osz
                busy = max(traffic / HBM_BYTES_PER_US, 2.0 * M * N * K / MXU_FLOPS_PER_US)
                cost = ni * nj * nk * STEP_US + busy + blocks / HBM_BYTES_PER_US
                if best is None or cost < best[0]:
                    best = (cost, tm, tn, tk, blocks, temps)
    _, tm, tn, tk, blocks, temps = best
    nk = K // tk
    use_scratch = nk > 1 and osz != 4

    def split(index, total, blocks, tile):
        per = total // blocks // tile
        return index // per, index % per

    def body(a_ref, b_ref, *rest):
        rest = rest[len(deps):]
        if post_norm:
            h_ref, g_ref, beta_ref, o_ref, xh_ref, rs_ref = rest[:6]
            acc = rest[6:]
        elif plus:
            plus_ref, o_ref, acc = rest[0], rest[1], rest[2:]
        else:
            o_ref, acc = rest[0], rest[1:]
        av = a_ref[...].astype(MXU_DTYPE)
        bv = b_ref[...].astype(MXU_DTYPE)
        dn = {"nn": (((1,), (0,)), ((), ())), "nt": (((1,), (1,)), ((), ())), "tn": (((0,), (0,)), ((), ()))}[mode]
        r = lax.dot_general(av, bv, dn, preferred_element_type=F32)

        def normalise(f):
            o_ref[...], xh_ref[...], rs_ref[...] = _layer_norm(post_norm[3] * h_ref[...] + f, g_ref[...], beta_ref[...])

        if nk == 1 and post_norm:
            normalise(r)
        elif nk == 1 and plus:
            o_ref[...] = plus[1] * plus_ref[...] + r
        elif nk == 1:
            o_ref[...] = r.astype(o_ref.dtype)
        else:
            acc_ref = acc[0] if use_scratch else o_ref

            @pl.when(pl.program_id(2) == 0)
            def _():
                acc_ref[...] = r

            @pl.when(pl.program_id(2) > 0)
            def _():
                acc_ref[...] += r

            if use_scratch:
                @pl.when(pl.program_id(2) == nk - 1)
                def _():
                    o_ref[...] = acc_ref[...].astype(o_ref.dtype)
            elif post_norm:
                @pl.when(pl.program_id(2) == nk - 1)
                def _():
                    normalise(o_ref[...])
            elif plus:
                @pl.when(pl.program_id(2) == nk - 1)
                def _():
                    o_ref[...] = plus[1] * plus_ref[...] + o_ref[...]

    if mode == "tn":
        a_spec = pl.BlockSpec((tk, tm), lambda i, j, k: (k, i))
    elif a_blocks:
        a_spec = pl.BlockSpec((None, tm, tk), lambda i, j, k: (split(k, K, a_blocks, tk)[0], i, split(k, K, a_blocks, tk)[1]))
    else:
        a_spec = pl.BlockSpec((tm, tk), lambda i, j, k: (i, k))
    lead = (None,) * nlead
    if mode == "nt":
        bmap = ((lambda i, j, k: b_index + (split(k, K, b_blocks, tk)[0], j, split(k, K, b_blocks, tk)[1])) if b_blocks
                else (lambda i, j, k: b_index + (j, k)))
        b_spec = pl.BlockSpec(lead + (tn, tk), bmap)
    else:
        bmap = ((lambda i, j, k: b_index + (split(j, N, b_blocks, tn)[0], k, split(j, N, b_blocks, tn)[1])) if b_blocks
                else (lambda i, j, k: b_index + (k, j)))
        b_spec = pl.BlockSpec(lead + (tk, tn), bmap)
    if out_blocks:
        o_spec = pl.BlockSpec((None, tm, tn), lambda i, j, k: (split(j, N, out_blocks, tn)[0], i, split(j, N, out_blocks, tn)[1]))
        o_shape = jax.ShapeDtypeStruct((out_blocks, M, N // out_blocks), out_dtype)
    else:
        o_spec = pl.BlockSpec((tm, tn), lambda i, j, k: (i, j))
        o_shape = jax.ShapeDtypeStruct((M, N), out_dtype)
    in_specs, extra = [a_spec, b_spec] + [pl.BlockSpec(memory_space=pl.ANY)] * len(deps), ()
    if post_norm:
        vec = pl.BlockSpec((1, N), lambda i, j, k: (0, 0))
        in_specs += [pl.BlockSpec((tm, N), lambda i, j, k: (i, 0)), vec, vec]
        o_spec = [o_spec, pl.BlockSpec((tm, N), lambda i, j, k: (i, 0)), pl.BlockSpec((tm, 1), lambda i, j, k: (i, 0))]
        o_shape = [o_shape, jax.ShapeDtypeStruct((M, N), F32), jax.ShapeDtypeStruct((M, 1), F32)]
        extra = post_norm[:3]
    elif plus:
        in_specs += [pl.BlockSpec((tm, tn), lambda i, j, k: (i, j))]
        extra = plus[:1]
    return hbm_call(
        body, name=name, grid=(M // tm, N // tn, nk), in_specs=in_specs, out_specs=o_spec, out_shape=o_shape,
        scratch_shapes=[pltpu.VMEM((tm, tn), F32)] if use_scratch else [],
        compiler_params=_params(("parallel", "parallel", "arbitrary"), _vmem_limit(blocks, temps)),
    )(a, b, *deps, *extra)


ROW_TILE = 512
GATE_ROWS = 1024


def ln_bwd(dy_a, dy_b, xh, rs, g, c1, name):
    S, D = xh.shape
    tr = min(ROW_TILE, S)
    two = dy_b is not None

    def body(*refs):
        if two:
            a_ref, b_ref, xh_ref, rs_ref, g_ref, dz_ref, dg_ref, db_ref = refs
            dy = c1 * a_ref[...] + b_ref[...]
        else:
            a_ref, xh_ref, rs_ref, g_ref, dz_ref, dg_ref, db_ref = refs
            dy = a_ref[...]
        x = xh_ref[...]
        dyg = dy * g_ref[...]
        m1 = jnp.mean(dyg, axis=-1, keepdims=True)
        m2 = jnp.mean(dyg * x, axis=-1, keepdims=True)
        dz_ref[...] = rs_ref[...] * (dyg - m1 - x * m2)

        @pl.when(pl.program_id(0) == 0)
        def _():
            dg_ref[...] = jnp.zeros_like(dg_ref)
            db_ref[...] = jnp.zeros_like(db_ref)

        dg_ref[...] += jnp.sum(dy * x, axis=0, keepdims=True)
        db_ref[...] += jnp.sum(dy, axis=0, keepdims=True)

    row = pl.BlockSpec((tr, D), lambda i: (i, 0))
    vec = pl.BlockSpec((1, D), lambda i: (0, 0))
    ins = [row, row] if two else [row]
    args = (dy_a, dy_b) if two else (dy_a,)
    return hbm_call(
        body, name=name, grid=(S // tr,), in_specs=ins + [row, pl.BlockSpec((tr, 1), lambda i: (i, 0)), vec],
        out_specs=[row, vec, vec],
        out_shape=[jax.ShapeDtypeStruct((S, D), F32), jax.ShapeDtypeStruct((1, D), F32), jax.ShapeDtypeStruct((1, D), F32)],
        compiler_params=_params(("arbitrary",), 48 << 20),
    )(*args, xh, rs, g)


def loss_head(y, t, name):
    S, D = y.shape
    tr = min(ROW_TILE, S)
    nsteps = S // tr

    def body(y_ref, t_ref, dy_ref, l_ref, acc_ref):
        i = pl.program_id(0)

        @pl.when(i == 0)
        def _():
            acc_ref[...] = jnp.zeros_like(acc_ref)

        e = y_ref[...] - t_ref[...]
        dy_ref[...] = e * (1.0 / D)
        acc_ref[...] += jnp.sum(e * e, axis=0, keepdims=True)

        @pl.when(i == nsteps - 1)
        def _():
            l_ref[...] = jnp.sum(acc_ref[...], axis=1, keepdims=True) * (0.5 / D)

    row = pl.BlockSpec((tr, D), lambda i: (i, 0))
    return hbm_call(
        body, name=name, grid=(nsteps,), in_specs=[row, row],
        out_specs=[row, pl.BlockSpec((1, 1), lambda i: (0, 0))],
        out_shape=[jax.ShapeDtypeStruct((S, D), F32), jax.ShapeDtypeStruct((1, 1), F32)],
        scratch_shapes=[pltpu.VMEM((1, D), F32)], compiler_params=_params(("arbitrary",)),
    )(y, t)


SWIGLU_ROWS = 256


def swiglu_fwd(gu, name):
    _, S, Fh = gu.shape
    tc = _divisors(Fh, LANES, 1536)[0]
    tr = min(SWIGLU_ROWS, S)

    def body(gu_ref, o_ref):
        g = gu_ref[0]
        o_ref[...] = (g * _sigmoid(g) * gu_ref[1]).astype(o_ref.dtype)

    return hbm_call(
        body, name=name, grid=(S // tr, Fh // tc), in_specs=[pl.BlockSpec((2, tr, tc), lambda i, j: (0, i, j))],
        out_specs=pl.BlockSpec((tr, tc), lambda i, j: (i, j)), out_shape=jax.ShapeDtypeStruct((S, Fh), MXU_DTYPE),
        compiler_params=_params(("parallel", "parallel")),
    )(gu)


def swiglu_bwd(gu, dact, name):
    _, S, Fh = gu.shape
    tc = _divisors(Fh, LANES, 1536)[0]
    tr = min(SWIGLU_ROWS, S)

    def body(gu_ref, d_ref, o_ref):
        g, u, d = gu_ref[0], gu_ref[1], d_ref[...]
        s = _sigmoid(g)
        o_ref[0] = (d * u * (s * (1.0 + g * (1.0 - s)))).astype(o_ref.dtype)
        o_ref[1] = (d * (g * s)).astype(o_ref.dtype)

    both = pl.BlockSpec((2, tr, tc), lambda i, j: (0, i, j))
    return hbm_call(
        body, name=name, grid=(S // tr, Fh // tc), in_specs=[both, pl.BlockSpec((tr, tc), lambda i, j: (i, j))],
        out_specs=both, out_shape=jax.ShapeDtypeStruct((2, S, Fh), MXU_DTYPE), compiler_params=_params(("parallel", "parallel")),
    )(gu, dact)


GATE_COLS = 256


def merge_fwd(proj, pr, pa, name):
    S, D = pr.shape
    tr = min(GATE_ROWS, S)
    c0 = (3 * D + 2 * N_KV_HEADS * HEAD_DIM) // GATE_COLS
    c1 = c0 + D // GATE_COLS

    def body(gr_ref, ga_ref, pr_ref, pa_ref, o_ref):
        o_ref[...] = (_sigmoid(gr_ref[...]) * pr_ref[...] + _sigmoid(ga_ref[...]) * pa_ref[...]).astype(o_ref.dtype)

    blk = pl.BlockSpec((tr, GATE_COLS), lambda i, j: (i, j))
    return hbm_call(
        body, name=name, grid=(S // tr, D // GATE_COLS),
        in_specs=[pl.BlockSpec((tr, GATE_COLS), lambda i, j: (i, c0 + j)), pl.BlockSpec((tr, GATE_COLS), lambda i, j: (i, c1 + j)),
                  blk, blk],
        out_specs=blk, out_shape=jax.ShapeDtypeStruct((S, D), MXU_DTYPE), compiler_params=_params(("parallel", "parallel")),
    )(proj, proj, pr, pa)


def merge_bwd(proj, pr, pa, dm, name):
    S, D = pr.shape
    tr = min(GATE_ROWS, S)
    c0 = (3 * D + 2 * N_KV_HEADS * HEAD_DIM) // GATE_COLS
    c1 = c0 + D // GATE_COLS

    def body(gr_ref, ga_ref, pr_ref, pa_ref, dm_ref, dpr_ref, dpa_ref, dgr_ref, dga_ref):
        sr, sa, d = _sigmoid(gr_ref[...]), _sigmoid(ga_ref[...]), dm_ref[...]
        dpr_ref[...] = (d * sr).astype(dpr_ref.dtype)
        dpa_ref[...] = (d * sa).astype(dpa_ref.dtype)
        dgr_ref[...] = (d * pr_ref[...] * (sr * (1.0 - sr))).astype(dgr_ref.dtype)
        dga_ref[...] = (d * pa_ref[...] * (sa * (1.0 - sa))).astype(dga_ref.dtype)

    blk = pl.BlockSpec((tr, GATE_COLS), lambda i, j: (i, j))
    sds = jax.ShapeDtypeStruct((S, D), MXU_DTYPE)
    return hbm_call(
        body, name=name, grid=(S // tr, D // GATE_COLS),
        in_specs=[pl.BlockSpec((tr, GATE_COLS), lambda i, j: (i, c0 + j)), pl.BlockSpec((tr, GATE_COLS), lambda i, j: (i, c1 + j)),
                  blk, blk, blk],
        out_specs=[blk, blk, blk, blk], out_shape=[sds, sds, sds, sds], compiler_params=_params(("parallel", "parallel")),
    )(proj, proj, pr, pa, dm)


RG_ROWS = 512


def _shift_down(cur, prev, d, row, first):
    halo = jnp.where(first, 0.0, pltpu.roll(prev, d, 0))
    return jnp.where(row >= d, pltpu.roll(cur, d, 0), halo)


def _shift_up(cur, nxt, d, row, last, tr):
    halo = jnp.where(last, 0.0, pltpu.roll(nxt, tr - d, 0))
    return jnp.where(row < tr - d, pltpu.roll(cur, tr - d, 0), halo)


def _lru_coeffs(r, lam):
    sp = _softplus_neg(lam)
    la = -LRU_C * r * sp
    return sp, la, jnp.exp(la), _neg_expm1(2.0 * la)


def rg_gates_fwd(proj, conv_w, conv_b, w_rg, b_rg, w_ig, b_ig, lam, name):
    S = proj.shape[0]
    nblk, bw, _ = w_rg.shape
    D = nblk * bw
    tr = min(RG_ROWS, S)

    def body(xr_ref, xp_ref, cw_ref, cb_ref, wr_ref, br_ref, wi_ref, bi_ref, lam_ref, xc_ref, r_ref, i_ref, a_ref, b_ref):
        first = pl.program_id(1) == 0
        cur, prev = xr_ref[...], xp_ref[...]
        row = lax.broadcasted_iota(jnp.int32, cur.shape, 0)
        xc = cb_ref[...]
        for k in range(CONV_WIDTH - 1):
            xc = xc + _shift_down(cur, prev, CONV_WIDTH - 1 - k, row, first) * cw_ref[k:k + 1, :]
        xc = xc + cur * cw_ref[CONV_WIDTH - 1:CONV_WIDTH, :]
        xm = xc.astype(MXU_DTYPE)
        r = _sigmoid(jnp.dot(xm, wr_ref[...].astype(MXU_DTYPE), preferred_element_type=F32) + br_ref[...])
        ig = _sigmoid(jnp.dot(xm, wi_ref[...].astype(MXU_DTYPE), preferred_element_type=F32) + bi_ref[...])
        _, _, a, em = _lru_coeffs(r, lam_ref[...])
        xc_ref[...] = xc
        r_ref[...] = r
        i_ref[...] = ig
        a_ref[...] = a
        b_ref[...] = jnp.sqrt(em) * (ig * xc)

    tile = pl.BlockSpec((tr, bw), lambda n, i: (i, n))
    vec = pl.BlockSpec((1, bw), lambda n, i: (0, n))
    wblk = pl.BlockSpec((None, bw, bw), lambda n, i: (n, 0, 0))
    sds = jax.ShapeDtypeStruct((S, D), F32)
    return hbm_call(
        body, name=name, grid=(nblk, S // tr),
        in_specs=[tile, pl.BlockSpec((tr, bw), lambda n, i: (jnp.maximum(i - 1, 0), n)),
                  pl.BlockSpec((CONV_WIDTH, bw), lambda n, i: (0, n)), vec, wblk, vec, wblk, vec, vec],
        out_specs=[tile] * 5, out_shape=[sds] * 5, compiler_params=_params(("parallel", "parallel")),
    )(proj, proj, conv_w, conv_b, w_rg, b_rg, w_ig, b_ig, lam)


SCAN_COLS = 256
CHUNK = SUBLANES
SCAN_UNROLL = 4


def rg_scan_fwd(proj, a, b, name):
    S, D = a.shape
    cb = min(SCAN_COLS, D)
    goff = D // cb

    def body(a_ref, b_ref, g_ref, hs_ref, y_ref):
        row = lax.broadcasted_iota(jnp.int32, (CHUNK, cb), 0)

        def step(c, carry):
            r0 = pl.multiple_of(c * CHUNK, CHUNK)
            A = a_ref[pl.ds(r0, CHUNK), :]
            B = b_ref[pl.ds(r0, CHUNK), :]
            for d in (1, 2, 4):
                As = jnp.where(row >= d, pltpu.roll(A, d, 0), 1.0)
                Bs = jnp.where(row >= d, pltpu.roll(B, d, 0), 0.0)
                B = A * Bs + B
                A = A * As
            hs_ref[pl.ds(r0, CHUNK), :] = B + A * carry
            a_end = jnp.sum(jnp.where(row == CHUNK - 1, A, 0.0), axis=0, keepdims=True)
            b_end = jnp.sum(jnp.where(row == CHUNK - 1, B, 0.0), axis=0, keepdims=True)
            return b_end + a_end * carry

        lax.fori_loop(0, S // CHUNK, step, jnp.zeros((1, cb), F32), unroll=SCAN_UNROLL)
        y_ref[...] = (hs_ref[...] * _gelu(g_ref[...])).astype(y_ref.dtype)

    col = pl.BlockSpec((S, cb), lambda j: (0, j))
    return hbm_call(
        body, name=name, grid=(D // cb,), in_specs=[col, col, pl.BlockSpec((S, cb), lambda j: (0, goff + j))],
        out_specs=[col, col], out_shape=[jax.ShapeDtypeStruct((S, D), F32), jax.ShapeDtypeStruct((S, D), MXU_DTYPE)],
        compiler_params=_params(("parallel",), _vmem_limit(5 * S * cb * 4, 4 * S * cb * 4)),
    )(a, b, proj)


def rg_scan_bwd(proj, dy, hs, a, name):
    S, D = a.shape
    cb = min(SCAN_COLS, D)
    goff = D // cb
    nchunks = S // CHUNK

    def body(g_ref, dy_ref, hs_ref, a_ref, dg_ref, gt_ref):
        gate, dy = g_ref[...], dy_ref[...]
        dg_ref[...] = (dy * hs_ref[...] * _gelu_grad(gate)).astype(dg_ref.dtype)
        gt_ref[...] = dy * _gelu(gate)
        row = lax.broadcasted_iota(jnp.int32, (CHUNK, cb), 0)

        def step(k, carry):
            c = nchunks - 1 - k
            r0 = pl.multiple_of(c * CHUNK, CHUNK)
            rn = pl.multiple_of(jnp.minimum(c + 1, nchunks - 1) * CHUNK, CHUNK)
            last = c == nchunks - 1
            nxt = jnp.where(last, 0.0, pltpu.roll(a_ref[pl.ds(rn, CHUNK), :], CHUNK - 1, 0))
            A = jnp.where(row < CHUNK - 1, pltpu.roll(a_ref[pl.ds(r0, CHUNK), :], CHUNK - 1, 0), nxt)
            B = gt_ref[pl.ds(r0, CHUNK), :]
            for d in (1, 2, 4):
                As = jnp.where(row < CHUNK - d, pltpu.roll(A, CHUNK - d, 0), 1.0)
                Bs = jnp.where(row < CHUNK - d, pltpu.roll(B, CHUNK - d, 0), 0.0)
                B = A * Bs + B
                A = A * As
            gt_ref[pl.ds(r0, CHUNK), :] = B + A * carry
            a_end = jnp.sum(jnp.where(row == 0, A, 0.0), axis=0, keepdims=True)
            b_end = jnp.sum(jnp.where(row == 0, B, 0.0), axis=0, keepdims=True)
            return b_end + a_end * carry

        lax.fori_loop(0, nchunks, step, jnp.zeros((1, cb), F32), unroll=SCAN_UNROLL)

    col = pl.BlockSpec((S, cb), lambda j: (0, j))
    return hbm_call(
        body, name=name, grid=(D // cb,), in_specs=[pl.BlockSpec((S, cb), lambda j: (0, goff + j)), col, col, col],
        out_specs=[col, col], out_shape=[jax.ShapeDtypeStruct((S, D), MXU_DTYPE), jax.ShapeDtypeStruct((S, D), F32)],
        compiler_params=_params(("parallel",), _vmem_limit(6 * S * cb * 4, 6 * S * cb * 4)),
    )(proj, dy, hs, a)


def rg_gates_bwd(gt, hs, xc, r, ig, w_rg, w_ig, lam, name):
    S, D = xc.shape
    nblk, bw, _ = w_rg.shape
    tr = min(RG_ROWS, S)

    def body(gt_ref, hs_ref, hp_ref, xc_ref, r_ref, i_ref, wr_ref, wi_ref, lam_ref,
             dxc_ref, dwr_ref, dwi_ref, dbr_ref, dbi_ref, dl_ref):
        step = pl.program_id(1)
        g, hs, xc, r, ig, lam = gt_ref[...], hs_ref[...], xc_ref[...], r_ref[...], i_ref[...], lam_ref[...]
        row = lax.broadcasted_iota(jnp.int32, g.shape, 0)
        hprev = _shift_down(hs, hp_ref[...], 1, row, step == 0)
        sp, _, a, em = _lru_coeffs(r, lam)
        mult = jnp.sqrt(em)
        du = g * mult
        dla = g * hprev * a - (g * (ig * xc)) * (a * a) / mult
        dpr = (dla * (-LRU_C * sp)) * (r * (1.0 - r))
        dpi = (du * xc) * (ig * (1.0 - ig))
        dprm, dpim = dpr.astype(MXU_DTYPE), dpi.astype(MXU_DTYPE)
        nt = (((1,), (1,)), ((), ()))
        dxc_ref[...] = (du * ig + lax.dot_general(dprm, wr_ref[...].astype(MXU_DTYPE), nt, preferred_element_type=F32)
                        + lax.dot_general(dpim, wi_ref[...].astype(MXU_DTYPE), nt, preferred_element_type=F32))

        @pl.when(step == 0)
        def _():
            for ref in (dwr_ref, dwi_ref, dbr_ref, dbi_ref, dl_ref):
                ref[...] = jnp.zeros_like(ref)

        xct = xc.T.astype(MXU_DTYPE)
        dwr_ref[...] += jnp.dot(xct, dprm, preferred_element_type=F32)
        dwi_ref[...] += jnp.dot(xct, dpim, preferred_element_type=F32)
        dbr_ref[...] += jnp.sum(dpr, axis=0, keepdims=True)
        dbi_ref[...] += jnp.sum(dpi, axis=0, keepdims=True)
        dl_ref[...] += jnp.sum(dla * (-LRU_C * r), axis=0, keepdims=True) * (-_sigmoid(-lam))

    tile = pl.BlockSpec((tr, bw), lambda n, i: (i, n))
    vec = pl.BlockSpec((1, bw), lambda n, i: (0, n))
    wblk = pl.BlockSpec((None, bw, bw), lambda n, i: (n, 0, 0))
    return hbm_call(
        body, name=name, grid=(nblk, S // tr),
        in_specs=[tile, tile, pl.BlockSpec((tr, bw), lambda n, i: (jnp.maximum(i - 1, 0), n)), tile, tile, tile, wblk, wblk, vec],
        out_specs=[tile, wblk, wblk, vec, vec, vec],
        out_shape=[jax.ShapeDtypeStruct((S, D), F32), jax.ShapeDtypeStruct((nblk, bw, bw), F32), jax.ShapeDtypeStruct((nblk, bw, bw), F32),
                   jax.ShapeDtypeStruct((1, D), F32), jax.ShapeDtypeStruct((1, D), F32), jax.ShapeDtypeStruct((1, D), F32)],
        compiler_params=_params(("parallel", "arbitrary")),
    )(gt, hs, hs, xc, r, ig, w_rg, w_ig, lam)


def rg_conv_bwd(proj, dxc, conv_w, name):
    S, D = dxc.shape
    bw = min(SCAN_COLS, D)
    tr = min(RG_ROWS, S)
    nsteps = S // tr

    def body(d_ref, dn_ref, xr_ref, xp_ref, cw_ref, dxr_ref, dcw_ref, dcb_ref):
        step = pl.program_id(1)
        d, xr = d_ref[...], xr_ref[...]
        row = lax.broadcasted_iota(jnp.int32, d.shape, 0)
        dxr = d * cw_ref[CONV_WIDTH - 1:CONV_WIDTH, :]
        for k in range(CONV_WIDTH - 1):
            dxr = dxr + _shift_up(d, dn_ref[...], CONV_WIDTH - 1 - k, row, step == nsteps - 1, tr) * cw_ref[k:k + 1, :]
        dxr_ref[...] = dxr.astype(dxr_ref.dtype)

        @pl.when(step == 0)
        def _():
            dcw_ref[...] = jnp.zeros_like(dcw_ref)
            dcb_ref[...] = jnp.zeros_like(dcb_ref)

        for k in range(CONV_WIDTH - 1):
            xs = _shift_down(xr, xp_ref[...], CONV_WIDTH - 1 - k, row, step == 0)
            dcw_ref[k:k + 1, :] += jnp.sum(d * xs, axis=0, keepdims=True)
        dcw_ref[CONV_WIDTH - 1:CONV_WIDTH, :] += jnp.sum(d * xr, axis=0, keepdims=True)
        dcb_ref[...] += jnp.sum(d, axis=0, keepdims=True)

    tile = pl.BlockSpec((tr, bw), lambda n, i: (i, n))
    cwb = pl.BlockSpec((CONV_WIDTH, bw), lambda n, i: (0, n))
    return hbm_call(
        body, name=name, grid=(D // bw, nsteps),
        in_specs=[tile, pl.BlockSpec((tr, bw), lambda n, i: (jnp.minimum(i + 1, nsteps - 1), n)), tile,
                  pl.BlockSpec((tr, bw), lambda n, i: (jnp.maximum(i - 1, 0), n)), cwb],
        out_specs=[tile, cwb, pl.BlockSpec((1, bw), lambda n, i: (0, n))],
        out_shape=[jax.ShapeDtypeStruct((S, D), MXU_DTYPE), jax.ShapeDtypeStruct((CONV_WIDTH, D), F32), jax.ShapeDtypeStruct((1, D), F32)],
        compiler_params=_params(("parallel", "arbitrary")),
    )(dxc, dxc, proj, proj, conv_w)


def rope_table(S):
    half = ROT_DIM // 2
    pos = jnp.arange(S, dtype=F32)
    inv = ROPE_THETA ** (-jnp.arange(0, ROT_DIM, 2, dtype=F32) / ROT_DIM)
    ang = pos[:, None] * inv[None, :]
    cos, sin = jnp.cos(ang), jnp.sin(ang)
    zero = jnp.zeros((S, HEAD_DIM - ROT_DIM), F32)
    c = jnp.concatenate([cos, cos, zero + 1.0], axis=1)
    a = jnp.concatenate([-sin, jnp.zeros((S, half), F32), zero], axis=1)
    b = jnp.concatenate([jnp.zeros((S, half), F32), sin, zero], axis=1)
    return jnp.stack([jnp.tile(t, (1, LANES // HEAD_DIM)) for t in (c, a, b)])


def _rope(t, tab):
    half = ROT_DIM // 2
    return t * tab[0] + pltpu.roll(t, LANES - half, 1) * tab[1] + pltpu.roll(t, half, 1) * tab[2]


def _rope_t(d, tab):
    half = ROT_DIM // 2
    return d * tab[0] + pltpu.roll(d * tab[1], half, 1) + pltpu.roll(d * tab[2], LANES - half, 1)


def _dup_head(t, hk, lo):
    sw = pltpu.roll(t, HEAD_DIM, 1)
    return jnp.where(lo, t, sw) if hk == 0 else jnp.where(lo, sw, t)


def _attn_common(n, sink_ref, q_ref, kp_ref, kc_ref, vp_ref, vc_ref, tc_ref, tp_ref, hk, pairs):
    tq = (tc_ref[0], tc_ref[1], tc_ref[2])
    tp = (tp_ref[0], tp_ref[1], tp_ref[2])
    lo = lax.broadcasted_iota(jnp.int32, (WINDOW, LANES), 1) < HEAD_DIM
    lo2 = lax.broadcasted_iota(jnp.int32, (2 * WINDOW, LANES), 1) < HEAD_DIM
    kband = jnp.concatenate([_rope(kp_ref[...], tp), _rope(kc_ref[...], tq)], axis=0)
    vband = jnp.concatenate([vp_ref[...], vc_ref[...]], axis=0)
    kd = _dup_head(kband, hk, lo2).astype(MXU_DTYPE)
    vd = _dup_head(vband, hk, lo2).astype(MXU_DTYPE)
    rows, sks = [], []
    for j in range(pairs):
        col = hk * pairs + j
        qp = _rope(q_ref[:, col * LANES:(col + 1) * LANES], tq)
        rows += [jnp.where(lo, qp, 0.0), jnp.where(lo, 0.0, qp)]
        sks += [jnp.full((WINDOW, 1), sink_ref[2 * col], F32), jnp.full((WINDOW, 1), sink_ref[2 * col + 1], F32)]
    qg = jnp.concatenate(rows, axis=0)
    sk = jnp.concatenate(sks, axis=0)
    G = 2 * pairs * WINDOW
    own = lax.broadcasted_iota(jnp.int32, (G, WINDOW), 1) <= (lax.broadcasted_iota(jnp.int32, (G, WINDOW), 0) & (WINDOW - 1))
    s = lax.dot_general(qg.astype(MXU_DTYPE), kd, (((1,), (1,)), ((), ())), preferred_element_type=F32) * (HEAD_DIM ** -0.5)
    s = jnp.where(own, s[:, WINDOW:], s[:, :WINDOW] + jnp.where(n > 0, 0.0, NEG_INF))
    m = jnp.maximum(jnp.max(s, axis=1, keepdims=True), sk)
    e = jnp.exp(s - m)
    es = jnp.exp(sk - m)
    inv = 1.0 / (jnp.sum(e, axis=1, keepdims=True) + es)
    return qg, kd, vd, e * inv, es * inv, own, lo, lo2, tq, tp


def _unfold_band(t, own):
    return jnp.concatenate([jnp.where(own, 0.0, t), jnp.where(own, t, 0.0)], axis=1)


def _attn_specs(D, NB):
    kcol = 3 * D // LANES
    q = pl.BlockSpec((WINDOW, D), lambda n: (n, 2))
    kc = pl.BlockSpec((WINDOW, LANES), lambda n: (n, kcol))
    kp = pl.BlockSpec((WINDOW, LANES), lambda n: (jnp.maximum(n - 1, 0), kcol))
    vc = pl.BlockSpec((WINDOW, LANES), lambda n: (n, kcol + 1))
    vp = pl.BlockSpec((WINDOW, LANES), lambda n: (jnp.maximum(n - 1, 0), kcol + 1))
    tc = pl.BlockSpec((3, WINDOW, LANES), lambda n: (0, n, 0))
    tp = pl.BlockSpec((3, WINDOW, LANES), lambda n: (0, jnp.maximum(n - 1, 0), 0))
    sink = pl.BlockSpec(memory_space=pltpu.SMEM)
    return [sink, q, kp, kc, vp, vc, tc, tp]


def attn_fwd(proj, sinks, tab, D, name):
    S = proj.shape[0]
    NB = S // WINDOW
    pairs = D // HEAD_DIM // N_KV_HEADS // 2

    def body(sink_ref, q_ref, kp_ref, kc_ref, vp_ref, vc_ref, tc_ref, tp_ref, o_ref):
        n = pl.program_id(0)
        for hk in range(N_KV_HEADS):
            _, _, vd, p, _, own, lo, _, _, _ = _attn_common(n, sink_ref, q_ref, kp_ref, kc_ref, vp_ref, vc_ref, tc_ref, tp_ref, hk, pairs)
            o = jnp.dot(_unfold_band(p, own).astype(MXU_DTYPE), vd, preferred_element_type=F32)
            for j in range(pairs):
                col = hk * pairs + j
                oa = o[(2 * j) * WINDOW:(2 * j + 1) * WINDOW]
                ob = o[(2 * j + 1) * WINDOW:(2 * j + 2) * WINDOW]
                o_ref[:, col * LANES:(col + 1) * LANES] = jnp.where(lo, oa, ob)

    return hbm_call(
        body, name=name, grid=(NB,), in_specs=_attn_specs(D, NB),
        out_specs=pl.BlockSpec((WINDOW, D), lambda n: (n, 0)), out_shape=jax.ShapeDtypeStruct((S, D), F32),
        compiler_params=_params(("parallel",)),
    )(sinks, proj, proj, proj, proj, proj, tab, tab)


def attn_bwd(proj, sinks, tab, o, do, D, name):
    S = proj.shape[0]
    NB = S // WINDOW
    pairs = D // HEAD_DIM // N_KV_HEADS // 2

    def body(sink_ref, q_ref, kp_ref, kc_ref, vp_ref, vc_ref, tc_ref, tp_ref, o_ref, do_ref, dq_ref, dk_ref, dv_ref, ds_ref):
        n = pl.program_id(0)

        @pl.when(n == 0)
        def _():
            ds_ref[...] = jnp.zeros_like(ds_ref)

        lane1 = lax.broadcasted_iota(jnp.int32, (1, LANES), 1)
        dsink = jnp.zeros((1, LANES), F32)
        dkt = dvt = None
        for hk in range(N_KV_HEADS):
            qg, kd, vd, p, ps, own, lo, lo2, tq, tp = _attn_common(n, sink_ref, q_ref, kp_ref, kc_ref, vp_ref, vc_ref, tc_ref, tp_ref, hk, pairs)
            dos, os_ = [], []
            for j in range(pairs):
                col = hk * pairs + j
                dop = do_ref[:, col * LANES:(col + 1) * LANES]
                op = o_ref[:, col * LANES:(col + 1) * LANES]
                dos += [jnp.where(lo, dop, 0.0), jnp.where(lo, 0.0, dop)]
                os_ += [jnp.where(lo, op, 0.0), jnp.where(lo, 0.0, op)]
            dog = jnp.concatenate(dos, axis=0)
            og = jnp.concatenate(os_, axis=0)
            dogm = dog.astype(MXU_DTYPE)
            dp = lax.dot_general(dogm, vd, (((1,), (1,)), ((), ())), preferred_element_type=F32)
            dp = jnp.where(own, dp[:, WINDOW:], dp[:, :WINDOW])
            dr = jnp.sum(dog * og, axis=1, keepdims=True)
            ds = _unfold_band(p * (dp - dr) * (HEAD_DIM ** -0.5), own)
            dsm = ds.astype(MXU_DTYPE)
            dqg = jnp.dot(dsm, kd, preferred_element_type=F32)
            dkd = jnp.dot(ds.T.astype(MXU_DTYPE), qg.astype(MXU_DTYPE), preferred_element_type=F32)
            dvd = jnp.dot(_unfold_band(p, own).T.astype(MXU_DTYPE), dogm, preferred_element_type=F32)
            dkf = dkd + pltpu.roll(dkd, HEAD_DIM, 1)
            dvf = dvd + pltpu.roll(dvd, HEAD_DIM, 1)
            if hk == 0:
                dkt, dvt = dkf, dvf
            else:
                dkt, dvt = jnp.where(lo2, dkt, dkf), jnp.where(lo2, dvt, dvf)
            sd = ps * dr
            for j in range(pairs):
                col = hk * pairs + j
                dqa = dqg[(2 * j) * WINDOW:(2 * j + 1) * WINDOW]
                dqb = dqg[(2 * j + 1) * WINDOW:(2 * j + 2) * WINDOW]
                dq_ref[:, col * LANES:(col + 1) * LANES] = _rope_t(jnp.where(lo, dqa, dqb), tq).astype(dq_ref.dtype)
                for t in range(2):
                    part = sd[(2 * j + t) * WINDOW:(2 * j + t + 1) * WINDOW]
                    val = jnp.sum(part, axis=0, keepdims=True)
                    dsink = dsink - jnp.where(lane1 == 2 * col + t, val, 0.0)
        dk_ref[...] = jnp.concatenate([_rope_t(dkt[:WINDOW], tp), _rope_t(dkt[WINDOW:], tq)], axis=0)
        dv_ref[...] = dvt
        ds_ref[...] += dsink

    blk = pl.BlockSpec((WINDOW, D), lambda n: (n, 0))
    band = pl.BlockSpec((None, 2 * WINDOW, LANES), lambda n: (n, 0, 0))
    return hbm_call(
        body, name=name, grid=(NB,), in_specs=_attn_specs(D, NB) + [blk, blk],
        out_specs=[blk, band, band, pl.BlockSpec((1, LANES), lambda n: (0, 0))],
        out_shape=[jax.ShapeDtypeStruct((S, D), MXU_DTYPE), jax.ShapeDtypeStruct((NB, 2 * WINDOW, LANES), F32),
                   jax.ShapeDtypeStruct((NB, 2 * WINDOW, LANES), F32), jax.ShapeDtypeStruct((1, LANES), F32)],
        compiler_params=_params(("arbitrary",)),
    )(sinks, proj, proj, proj, proj, proj, tab, tab, o, do)


def band_fold(dkb, dvb, name):
    NB = dkb.shape[0]
    k4 = dkb.reshape(NB, 2, WINDOW, LANES)
    v4 = dvb.reshape(NB, 2, WINDOW, LANES)

    def body(kc_ref, kn_ref, vc_ref, vn_ref, dk_ref, dv_ref):
        more = pl.program_id(0) < NB - 1
        dk_ref[...] = (kc_ref[...] + jnp.where(more, kn_ref[...], 0.0)).astype(dk_ref.dtype)
        dv_ref[...] = (vc_ref[...] + jnp.where(more, vn_ref[...], 0.0)).astype(dv_ref.dtype)

    cur = pl.BlockSpec((None, None, WINDOW, LANES), lambda n: (n, 1, 0, 0))
    nxt = pl.BlockSpec((None, None, WINDOW, LANES), lambda n: (jnp.minimum(n + 1, NB - 1), 0, 0, 0))
    out = pl.BlockSpec((WINDOW, LANES), lambda n: (n, 0))
    sds = jax.ShapeDtypeStruct((NB * WINDOW, LANES), MXU_DTYPE)
    return hbm_call(body, name=name, grid=(NB,), in_specs=[cur, nxt, cur, nxt], out_specs=[out, out], out_shape=[sds, sds],
                          compiler_params=_params(("parallel",)))(k4, k4, v4, v4)


CROSS_ROWS = 512


def _cross_probs(q, k, scale):
    s = lax.dot_general(q.astype(MXU_DTYPE), k.astype(MXU_DTYPE), (((1,), (1,)), ((), ())), preferred_element_type=F32) * scale
    e = jnp.exp(s - jnp.max(s, axis=1, keepdims=True))
    return e / jnp.sum(e, axis=1, keepdims=True)


def cross_fwd(qc, kv, name):
    S, D = qc.shape
    M = kv.shape[0]
    hd = D // CROSS_HEADS
    tq = min(CROSS_ROWS, S)

    def body(q_ref, kv_ref, o_ref):
        for h in range(CROSS_HEADS):
            p = _cross_probs(q_ref[:, h * hd:(h + 1) * hd], kv_ref[:, h * hd:(h + 1) * hd], hd ** -0.5)
            v = kv_ref[:, D + h * hd:D + (h + 1) * hd].astype(MXU_DTYPE)
            o_ref[:, h * hd:(h + 1) * hd] = jnp.dot(p.astype(MXU_DTYPE), v, preferred_element_type=F32).astype(o_ref.dtype)

    return hbm_call(
        body, name=name, grid=(S // tq,), in_specs=[pl.BlockSpec((tq, D), lambda i: (i, 0)), pl.BlockSpec((M, 2 * D), lambda i: (0, 0))],
        out_specs=pl.BlockSpec((tq, D), lambda i: (i, 0)), out_shape=jax.ShapeDtypeStruct((S, D), MXU_DTYPE),
        compiler_params=_params(("parallel",)),
    )(qc, kv)


def cross_bwd(qc, kv, do, name):
    S, D = qc.shape
    M = kv.shape[0]
    hd = D // CROSS_HEADS
    tq = min(CROSS_ROWS, S)

    def body(q_ref, kv_ref, do_ref, dq_ref, dkv_ref):
        @pl.when(pl.program_id(0) == 0)
        def _():
            dkv_ref[...] = jnp.zeros_like(dkv_ref)

        for h in range(CROSS_HEADS):
            q = q_ref[:, h * hd:(h + 1) * hd]
            k = kv_ref[:, h * hd:(h + 1) * hd]
            v = kv_ref[:, D + h * hd:D + (h + 1) * hd].astype(MXU_DTYPE)
            dom = do_ref[:, h * hd:(h + 1) * hd].astype(MXU_DTYPE)
            p = _cross_probs(q, k, hd ** -0.5)
            dp = lax.dot_general(dom, v, (((1,), (1,)), ((), ())), preferred_element_type=F32)
            ds = p * (dp - jnp.sum(p * dp, axis=1, keepdims=True)) * (hd ** -0.5)
            dq_ref[:, h * hd:(h + 1) * hd] = jnp.dot(ds.astype(MXU_DTYPE), k.astype(MXU_DTYPE),
                                                     preferred_element_type=F32).astype(dq_ref.dtype)
            dkv_ref[:, h * hd:(h + 1) * hd] += jnp.dot(ds.T.astype(MXU_DTYPE), q.astype(MXU_DTYPE), preferred_element_type=F32)
            dkv_ref[:, D + h * hd:D + (h + 1) * hd] += jnp.dot(p.T.astype(MXU_DTYPE), dom, preferred_element_type=F32)

    row = pl.BlockSpec((tq, D), lambda i: (i, 0))
    full = pl.BlockSpec((M, 2 * D), lambda i: (0, 0))
    return hbm_call(
        body, name=name, grid=(S // tq,), in_specs=[row, full, row], out_specs=[row, full],
        out_shape=[jax.ShapeDtypeStruct((S, D), MXU_DTYPE), jax.ShapeDtypeStruct((M, 2 * D), F32)],
        compiler_params=_params(("arbitrary",)),
    )(qc, kv, do)


def adamw(w, g, m, v, name, layers=None, into=None):
    shape = w.shape
    cols = shape[-1]
    lead = shape[0] if len(shape) > 2 else 1
    rows = int(np.prod(shape[:-1])) // lead
    w2, g2, m2, v2 = (t.reshape(lead, rows, cols) for t in (w, g, m, v))
    tr = _divisors(rows, SUBLANES, max(SUBLANES, (1 << 20) // (cols * 4) // SUBLANES * SUBLANES))[0]
    lo, hi = layers or (0, lead)
    done = [t.reshape(lead, rows, cols) for t in into] if into else []

    def body(w_ref, g_ref, m_ref, v_ref, *refs):
        d_ref, mo_ref, vo_ref, go_ref = refs[len(done):]
        gg = g_ref[...]
        mn = ADAM_B1 * m_ref[...] + (1.0 - ADAM_B1) * gg
        vn = ADAM_B2 * v_ref[...] + (1.0 - ADAM_B2) * (gg * gg)
        m_hat = mn / (1.0 - ADAM_B1 ** ADAM_STEP)
        v_hat = vn / (1.0 - ADAM_B2 ** ADAM_STEP)
        d_ref[...] = -ADAM_LR * (m_hat / (jnp.sqrt(v_hat) + ADAM_EPS) + ADAM_WD * w_ref[...])
        mo_ref[...] = mn
        vo_ref[...] = vn
        go_ref[...] = gg

    blk = pl.BlockSpec((None, tr, cols), lambda l, i: (l + lo, i, 0))
    sds = jax.ShapeDtypeStruct((lead, rows, cols), F32)
    d, mn, vn, go = hbm_call(body, name=name, grid=(hi - lo, rows // tr), in_specs=[blk] * 4 + [pl.BlockSpec(memory_space=pl.ANY)] * len(done),
                             out_specs=[blk] * 4, out_shape=[sds] * 4, input_output_aliases={4 + k: k for k in range(len(done))},
                             compiler_params=_params(("parallel", "parallel")))(w2, g2, m2, v2, *done)
    return d.reshape(shape), mn.reshape(shape), vn.reshape(shape), go.reshape(shape)


def sum_devices(parts, name):
    n, rows, cols = parts.shape

    def body(p_ref, o_ref):
        acc = p_ref[0]
        for k in range(1, n):
            acc = acc + p_ref[k]
        o_ref[...] = acc

    return pl.pallas_call(body, name=name, in_specs=[pl.BlockSpec(memory_space=pltpu.VMEM)],
                          out_specs=pl.BlockSpec(memory_space=pltpu.VMEM), out_shape=jax.ShapeDtypeStruct((rows, cols), F32))(parts)


HBM_SPEC = pl.BlockSpec(memory_space=pltpu.HBM)


def _place():
    return lax.axis_index("x"), lax.axis_index("y"), lax.axis_index("c")


def _remote(src, dst, send_sems, recv_sems, k, to):
    return pltpu.make_async_remote_copy(src_ref=src, dst_ref=dst, send_sem=send_sems.at[k], recv_sem=recv_sems.at[k],
                                        device_id=to, device_id_type=MESH_ID)


SEM_SPEC = pl.BlockSpec(memory_space=pltpu.SEMAPHORE)
ANY_SPEC = pl.BlockSpec(memory_space=pl.ANY)
SPLIT_COPY = pltpu.CompilerParams(has_side_effects=pltpu.SideEffectType.DATAFLOW_SIDE_EFFECTING)


def _in_hbm(arrays):
    return [pltpu.with_memory_space_constraint(a, pltpu.HBM) for a in arrays]


def _split_start(copies, sources, lands, after, n_sems, name):
    n = len(sources)

    def body(*refs):
        for cp in copies(refs[:n], refs[n:2 * n], refs[2 * n + 1], refs[2 * n + 2]):
            cp.start()
        refs[-1][...] = jnp.zeros_like(refs[-1])

    through = [pltpu.HBM(a.shape, a.dtype) for a in list(sources) + list(lands)]
    outs = pl.pallas_call(
        body, name=name, in_specs=[HBM_SPEC] * (2 * n) + [ANY_SPEC],
        out_specs=[SEM_SPEC, SEM_SPEC] + [HBM_SPEC] * (2 * n) + [pl.BlockSpec(memory_space=pltpu.VMEM)],
        out_shape=[pltpu.SemaphoreType.DMA((n_sems,)), pltpu.SemaphoreType.DMA((n_sems,))] + through
        + [jax.ShapeDtypeStruct((SUBLANES, LANES), F32)],
        input_output_aliases={i: 2 + i for i in range(2 * n)}, compiler_params=SPLIT_COPY,
    )(*_in_hbm(sources), *_in_hbm(lands), after)
    return outs[0], outs[1], outs[2:2 + n], outs[2 + n:2 + 2 * n], outs[-1]


def _split_wait(copies, send_sems, recv_sems, sources, lands, after, name):
    n = len(sources)

    def body(*refs):
        for cp in copies(refs[:n], refs[n:2 * n], refs[2 * n], refs[2 * n + 1]):
            cp.wait_send()
            cp.wait_recv()

    through = [pltpu.HBM(a.shape, a.dtype) for a in list(sources) + list(lands)]
    outs = pl.pallas_call(
        body, name=name, in_specs=[HBM_SPEC] * (2 * n) + [SEM_SPEC, SEM_SPEC, ANY_SPEC], out_specs=[HBM_SPEC] * (2 * n),
        out_shape=through, input_output_aliases={i: i for i in range(2 * n)}, compiler_params=SPLIT_COPY,
    )(*sources, *lands, send_sems, recv_sems, after)
    return outs[:n], outs[n:]


def _chip_slab(land, slot, rows):
    return land.at[slot, rows] if len(land.shape) == 3 else land.at[rows, slot]


def _gather_copies(w_refs, land_refs, send_sems, recv_sems):
    n = len(w_refs)
    x, y, c = _place()
    chips = [(1 - x, y), (x, 1 - y), (1 - x, 1 - y)]
    cps = []
    for a in range(n):
        hr = w_refs[a].shape[0] // 2
        mine, every = pl.ds(c * hr, hr), pl.ds(0, 2 * hr)
        cps.append(_remote(w_refs[a], _chip_slab(land_refs[a], 2 * x + y, every), send_sems, recv_sems, 3 * n + a, (x, y, 1 - c)))
        for k, chip in enumerate(chips):
            cps.append(_remote(w_refs[a].at[mine], _chip_slab(land_refs[a], 2 * x + y, mine), send_sems, recv_sems, 3 * a + k, (*chip, c)))
    return cps


def gather_start(shards, after, name):
    lands = [lax.empty(s.shape[:-2] + (N_CHIPS,) + s.shape[-2:], s.dtype) for s in shards]
    return _split_start(_gather_copies, shards, lands, after, 4 * len(shards), name)


def gather_wait(state, after, name):
    send_sems, recv_sems, sources, lands, _ = state
    return _split_wait(_gather_copies, send_sems, recv_sems, sources, lands, after, name)[1]


def gather_pass(lands, name):
    n = len(lands)

    def body(*refs):
        out_refs, send_sems, recv_sems = refs[n:2 * n], refs[2 * n], refs[2 * n + 1]
        x, y, c = _place()
        chips = [(1 - x, y), (x, 1 - y), (1 - x, 1 - y)]
        sent = []
        for a in range(n):
            hr = out_refs[a].shape[0 if len(out_refs[a].shape) == 4 else 1] // 2
            for k, (px, py) in enumerate(chips):
                landed = _chip_slab(out_refs[a], 2 * px + py, pl.ds(c * hr, hr))
                sent.append(_remote(landed, landed, send_sems, recv_sems, 3 * a + k, (x, y, 1 - c)))
        for cp in sent:
            cp.start()
        for a in range(n):
            hr = out_refs[a].shape[0 if len(out_refs[a].shape) == 4 else 1] // 2
            for k, (px, py) in enumerate(chips):
                theirs = _chip_slab(out_refs[a], 2 * px + py, pl.ds((1 - c) * hr, hr))
                _remote(theirs, theirs, send_sems, recv_sems, 3 * a + k, (x, y, 1 - c)).wait_recv()
        for cp in sent:
            cp.wait_send()

    return hbm_call(
        body, name=name, in_specs=[HBM_SPEC] * n, out_specs=[HBM_SPEC] * n,
        out_shape=[jax.ShapeDtypeStruct(a.shape, a.dtype) for a in lands], input_output_aliases={a: a for a in range(n)},
        scratch_shapes=[pltpu.SemaphoreType.DMA((3 * n,))] * 2,
    )(*lands)


def _scatter_copies(t_refs, land_refs, send_sems, recv_sems):
    x, y, c = _place()
    chips = [(1 - x, y), (x, 1 - y), (1 - x, 1 - y)]
    return [_remote(t_refs[a].at[:, 2 * px + py], land_refs[a].at[:, k], send_sems, recv_sems, 3 * a + k, (px, py, c))
            for a in range(len(t_refs)) for k, (px, py) in enumerate(chips)]


def scatter_start(parts, after, name):
    lands = [lax.empty((t.shape[0], N_CHIPS - 1) + t.shape[2:], t.dtype) for t in parts]
    return _split_start(_scatter_copies, parts, lands, after, 3 * len(parts), name)


def scatter_wait(state, after, name):
    send_sems, recv_sems, sources, lands, _ = state
    return _split_wait(_scatter_copies, send_sems, recv_sems, sources, lands, after, name)


def swap_sibling(parts, name):
    n = len(parts)

    def body(*refs):
        v_refs, out_refs, send_sems, recv_sems = refs[:n], refs[n:2 * n], refs[2 * n], refs[2 * n + 1]
        x, y, c = _place()
        cps = []
        for a in range(n):
            hr = v_refs[a].shape[2] // 2
            cps.append(_remote(v_refs[a].at[:, :, pl.ds((1 - c) * hr, hr)], out_refs[a], send_sems, recv_sems, a, (x, y, 1 - c)))
        for cp in cps:
            cp.start()
        for cp in cps:
            cp.wait()

    return hbm_call(
        body, name=name, in_specs=[HBM_SPEC] * n, out_specs=[HBM_SPEC] * n,
        out_shape=[jax.ShapeDtypeStruct(v.shape[:2] + (v.shape[2] // 2, v.shape[3]), v.dtype) for v in parts],
        scratch_shapes=[pltpu.SemaphoreType.DMA((n,))] * 2,
    )(*parts)


def join_halves(halves, layer, name):
    n = len(halves)

    def body(*refs):
        out_refs, send_sems, recv_sems = refs[n:2 * n], refs[2 * n], refs[2 * n + 1]
        x, y, c = _place()
        cps = []
        for a in range(n):
            hr = out_refs[a].shape[1] // 2
            mine = out_refs[a].at[layer, pl.ds(c * hr, hr)]
            cps.append(_remote(mine, mine, send_sems, recv_sems, a, (x, y, 1 - c)))
        for cp in cps:
            cp.start()
        for a in range(n):
            hr = out_refs[a].shape[1] // 2
            theirs = out_refs[a].at[layer, pl.ds((1 - c) * hr, hr)]
            _remote(theirs, theirs, send_sems, recv_sems, a, (x, y, 1 - c)).wait_recv()
        for cp in cps:
            cp.wait_send()

    return hbm_call(
        body, name=name, in_specs=[HBM_SPEC] * n, out_specs=[HBM_SPEC] * n,
        out_shape=[jax.ShapeDtypeStruct(f.shape, f.dtype) for f in halves], input_output_aliases={a: a for a in range(n)},
        scratch_shapes=[pltpu.SemaphoreType.DMA((n,))] * 2,
    )(*halves)


def gather_devices(v, name, after=()):
    def body(v_ref, *refs):
        out_ref, send_sems, recv_sems, local_sem = refs[len(after):]
        x, y, c = _place()
        me = 4 * x + 2 * y + c
        own = pltpu.make_async_copy(v_ref, out_ref.at[me], local_sem)
        own.start()
        peers = [((x + dx) % 2, (y + dy) % 2, (c + dc) % 2) for dx in (0, 1) for dy in (0, 1) for dc in (0, 1)][1:]
        sent = []
        for k, peer in enumerate(peers):
            cp = pltpu.make_async_remote_copy(src_ref=v_ref, dst_ref=out_ref.at[me], send_sem=send_sems.at[k], recv_sem=recv_sems.at[k],
                                              device_id=peer, device_id_type=MESH_ID)
            cp.start()
            sent.append(cp)
        for k, (px, py, pc) in enumerate(peers):
            slot = out_ref.at[4 * px + 2 * py + pc]
            pltpu.make_async_remote_copy(src_ref=slot, dst_ref=slot, send_sem=send_sems.at[k], recv_sem=recv_sems.at[k],
                                         device_id=(px, py, pc), device_id_type=MESH_ID).wait_recv()
        for cp in sent:
            cp.wait_send()
        own.wait()

    vm = pl.BlockSpec(memory_space=pltpu.VMEM)
    return pl.pallas_call(body, name=name, in_specs=[vm] + [ANY_SPEC] * len(after), out_specs=vm,
                          out_shape=jax.ShapeDtypeStruct((N_DEV,) + v.shape, v.dtype),
                          scratch_shapes=[pltpu.SemaphoreType.DMA((N_DEV - 1,)), pltpu.SemaphoreType.DMA((N_DEV - 1,)),
                                          pltpu.SemaphoreType.DMA])(v, *after)


ADD_ROWS = 512


def add_pair(place, a, b, name):
    L, n, hr, cols = b.shape
    tr = _divisors(hr, 2 * SUBLANES, ADD_ROWS)[0]
    nb = hr // tr

    def body(p_ref, a_ref, b_ref, o_ref):
        del p_ref
        o_ref[...] = (a_ref[...].astype(F32) + b_ref[...].astype(F32)).astype(o_ref.dtype)

    blk = pl.BlockSpec((None, None, tr, cols), lambda l, d, i, p: (l, d, i, 0))
    grid_spec = pltpu.PrefetchScalarGridSpec(
        num_scalar_prefetch=1, grid=(L, n, nb),
        in_specs=[pl.BlockSpec((None, None, tr, cols), lambda l, d, i, p: (l, d, p[0] * nb + i, 0)), blk], out_specs=blk)
    return hbm_call(body, name=name, grid_spec=grid_spec, out_shape=jax.ShapeDtypeStruct(b.shape, b.dtype),
                          compiler_params=_params(("parallel", "parallel", "parallel")))(place, a, b)


def add_chips(place, own, others, layer, stacked, name):
    _, n, hr, cols = others.shape
    tr = _divisors(hr, 2 * SUBLANES, ADD_ROWS)[0]
    nb = hr // tr
    create = isinstance(stacked, tuple)

    def body(p_ref, own_ref, *refs):
        del p_ref
        acc = own_ref[...].astype(F32)
        for k in range(n):
            acc = acc + refs[k][...].astype(F32)
        refs[-1][...] = acc

    ins = [pl.BlockSpec((None, None, tr, cols), lambda i, p: (0, p[1], i, 0))]
    ins += [pl.BlockSpec((None, None, tr, cols), functools.partial(lambda k, i, p: (0, k, i, 0), k)) for k in range(n)]
    grid_spec = pltpu.PrefetchScalarGridSpec(num_scalar_prefetch=1, grid=(nb,), in_specs=ins + ([] if create else [ANY_SPEC]),
                                             out_specs=pl.BlockSpec((None, tr, cols), lambda i, p: (layer, p[0] * nb + i, 0)))
    shape = stacked if create else stacked.shape
    return hbm_call(body, name=name, grid_spec=grid_spec, out_shape=jax.ShapeDtypeStruct(shape, F32),
                          input_output_aliases={} if create else {n + 2: 0},
                          compiler_params=_params(("parallel",)))(place, own, *([others] * n), *([] if create else [stacked]))


def _alpha(depth):
    return (2 * depth) ** 0.25


def _wmm(a, weight, mode, name, deps=(), **more):
    arr, how = weight
    return mm(a, arr, mode, name, deps=deps, **how, **more)


def layer_fwd(h, mem, w, tab, alpha, deps=(), late=None):
    D = h.shape[1]
    proj = _wmm(h, w["w_in"], "nt", "mm_proj", deps)
    xc, r, ig, a, b = rg_gates_fwd(proj, w["conv_w"], w["conv_b"], w["w_rg"], w["b_rg"], w["w_ig"], w["b_ig"], w["lru_lambda"], "rg_gates_fwd")
    hs, y_rnn = rg_scan_fwd(proj, a, b, "rg_scan_fwd")
    y_attn = attn_fwd(proj, w["sinks"], tab, D, "attn_fwd")
    deps = ()
    if late is not None:
        rest, deps = late(y_attn)
        w = {**w, **rest}
    pr = _wmm(y_rnn, w["w_br_rnn"], "nn", "mm_br_rnn", deps)
    pa = _wmm(y_attn, w["w_br_attn"], "nn", "mm_br_attn")
    merged = merge_fwd(proj, pr, pa, "merge_fwd")
    h1, xh1, rs1 = _wmm(merged, w["w_out"], "nn", "mm_out_ln1", post_norm=(h, w["ln1_g"], w["ln1_b"], alpha))
    qc = _wmm(h1, w["cq_w"], "nn", "mm_cq", out_dtype=MXU_DTYPE)
    kv = _wmm(mem, w["ckv_w"], "nn", "mm_ckv", out_dtype=MXU_DTYPE)
    o = cross_fwd(qc, kv, "cross_fwd")
    h2, xh2, rs2 = _wmm(o, w["co_w"], "nn", "mm_co_ln2", post_norm=(h1, w["ln2_g"], w["ln2_b"], alpha))
    gu = _wmm(h2, w["ffn_wi"], "nn", "mm_ffn_wi", out_blocks=2)
    act = swiglu_fwd(gu, "swiglu_fwd")
    h3, xh3, rs3 = _wmm(act, w["ffn_wo"], "nn", "mm_ffn_wo_ln3", post_norm=(h2, w["ln3_g"], w["ln3_b"], alpha))
    saved = dict(h=h, proj=proj, xc=xc, r=r, ig=ig, a=a, hs=hs, y_rnn=y_rnn, y_attn=y_attn, pr=pr, pa=pa, xh1=xh1, rs1=rs1, h1=h1,
                 qc=qc, kv=kv, o=o, xh2=xh2, rs2=rs2, h2=h2, gu=gu, xh3=xh3, rs3=rs3)
    return h3, saved, w


def layer_bwd(dh, mem, w, s, tab, alpha, deps=(), halfway=None):
    D = dh.shape[1]
    g = {}
    wg = dict(out_dtype=MXU_DTYPE)
    dz3, g["ln3_g"], g["ln3_b"] = ln_bwd(dh, None, s["xh3"], s["rs3"], w["ln3_g"], 1.0, "ln3_bwd")
    act = swiglu_fwd(s["gu"], "swiglu_refwd")
    g["ffn_wo"] = mm(act, dz3, "tn", "mm_d_ffn_wo", deps=deps, **wg)
    dact = _wmm(dz3, w["ffn_wo"], "nt", "mm_dact")
    dgu = swiglu_bwd(s["gu"], dact, "swiglu_bwd")
    g["ffn_wi"] = mm(s["h2"], dgu, "tn", "mm_d_ffn_wi", b_blocks=2, out_blocks=N_CHIPS, **wg)
    dh2 = _wmm(dgu, w["ffn_wi"], "nt", "mm_dh2", a_blocks=2)
    dz2, g["ln2_g"], g["ln2_b"] = ln_bwd(dz3, dh2, s["xh2"], s["rs2"], w["ln2_g"], alpha, "ln2_bwd")
    g["co_w"] = mm(s["o"], dz2, "tn", "mm_d_co", **wg)
    do = _wmm(dz2, w["co_w"], "nt", "mm_do", out_dtype=MXU_DTYPE)
    dqc, dkv = cross_bwd(s["qc"], s["kv"], do, "cross_bwd")
    g["cq_w"] = mm(s["h1"], dqc, "tn", "mm_d_cq", **wg)
    g["ckv_w"] = mm(mem, dkv, "tn", "mm_d_ckv", out_blocks=N_CHIPS, **wg)
    dh1 = _wmm(dqc, w["cq_w"], "nt", "mm_dh1")
    deps = halfway(g, dh1) if halfway is not None else ()
    dz1, g["ln1_g"], g["ln1_b"] = ln_bwd(dz2, dh1, s["xh1"], s["rs1"], w["ln1_g"], alpha, "ln1_bwd")
    merged = merge_fwd(s["proj"], s["pr"], s["pa"], "merge_refwd")
    g["w_out"] = mm(merged, dz1, "tn", "mm_d_out", deps=deps, **wg)
    dm = _wmm(dz1, w["w_out"], "nt", "mm_dmerged")
    dpr, dpa, dg_rnn, dg_attn = merge_bwd(s["proj"], s["pr"], s["pa"], dm, "merge_bwd")
    g["w_br_rnn"] = mm(s["y_rnn"], dpr, "tn", "mm_d_br_rnn", **wg)
    g["w_br_attn"] = mm(s["y_attn"], dpa, "tn", "mm_d_br_attn", **wg)
    dy_rnn = _wmm(dpr, w["w_br_rnn"], "nt", "mm_dy_rnn")
    dy_attn = _wmm(dpa, w["w_br_attn"], "nt", "mm_dy_attn")
    dq, dkb, dvb, dsink = attn_bwd(s["proj"], w["sinks"], tab, s["y_attn"], dy_attn, D, "attn_bwd")
    dk, dv = band_fold(dkb, dvb, "band_fold")
    g["sinks"] = dsink[:, :w["sinks"].shape[0]]
    dgr, gt = rg_scan_bwd(s["proj"], dy_rnn, s["hs"], s["a"], "rg_scan_bwd")
    dxc, g["w_rg"], g["w_ig"], g["b_rg"], g["b_ig"], g["lru_lambda"] = rg_gates_bwd(
        gt, s["hs"], s["xc"], s["r"], s["ig"], w["w_rg"], w["w_ig"], w["lru_lambda"], "rg_gates_bwd")
    dxr, g["conv_w"], g["conv_b"] = rg_conv_bwd(s["proj"], dxc, w["conv_w"], "rg_conv_bwd")
    dproj = jnp.concatenate([dxr, dgr, dq, dk, dv, dg_rnn, dg_attn], axis=1)
    g["w_in"] = mm(dproj, s["h"], "tn", "mm_d_in", **wg)
    return _wmm(dproj, w["w_in"], "nn", "mm_dh", plus=(dz1, alpha)), g


def local_step(x, mem, target, depth, weights_of, grads_halfway, grads_done):
    alpha = _alpha(depth)
    tab = rope_table(x.shape[0])
    h, saved, layers = x, [], []
    for l in range(depth):
        wl, deps, late = weights_of(l, h)
        h, s, wl = layer_fwd(h, mem, wl, tab, alpha, deps, late)
        layers.append(wl)
        saved.append(s)
    dh, loss = loss_head(h, target, "loss_head")
    deps = ()
    for l in reversed(range(depth)):
        dh, g = layer_bwd(dh, mem, layers[l], saved[l], tab, alpha, deps, grads_halfway(l))
        deps = grads_done(l, g, dh)
    return loss, dh


def _pad_rows(flat):
    n = flat.shape[0]
    rows = -(-n // (LANES * SUBLANES)) * SUBLANES
    return jnp.pad(flat, (0, rows * LANES - n)).reshape(rows, LANES)


def kernel(x, mem, w_in, conv_w, conv_b, w_rg, b_rg, w_ig, b_ig, lru_lambda, w_br_rnn, w_br_attn, sinks, w_out, ln1_g, ln1_b, cq_w, ckv_w, co_w, ln2_g, ln2_b, ffn_wi, ffn_wo, ln3_g, ln3_b, loss_target, m_w_in, m_conv_w, m_conv_b, m_w_rg, m_b_rg, m_w_ig, m_b_ig, m_lru_lambda, m_w_br_rnn, m_w_br_attn, m_sinks, m_w_out, m_ln1_g, m_ln1_b, m_cq_w, m_ckv_w, m_co_w, m_ln2_g, m_ln2_b, m_ffn_wi, m_ffn_wo, m_ln3_g, m_ln3_b, v_w_in, v_conv_w, v_conv_b, v_w_rg, v_b_rg, v_w_ig, v_b_ig, v_lru_lambda, v_w_br_rnn, v_w_br_attn, v_sinks, v_w_out, v_ln1_g, v_ln1_b, v_cq_w, v_ckv_w, v_co_w, v_ln2_g, v_ln2_b, v_ffn_wi, v_ffn_wo, v_ln3_g, v_ln3_b):
    args = dict(locals())
    w = {n: args[n] for n in WEIGHTS}
    m = {n: args["m_" + n] for n in WEIGHTS}
    v = {n: args["v_" + n] for n in WEIGHTS}
    for group in (w, m, v):
        group["w_in"] = jnp.swapaxes(group["w_in"], 1, 2)
    cx, cy, cc = _place()
    chip = 2 * cx + cy
    L = w_in.shape[0]

    place = jnp.stack([cc, chip]).astype(jnp.int32)
    cw_rows = _pad_rows(conv_w.reshape(-1))
    cw_all = gather_devices(cw_rows, "gather_conv_w")[0::2]
    cw_parts = cw_all.reshape(N_CHIPS, -1)[:, :conv_w.size].reshape((N_CHIPS,) + conv_w.shape)
    conv_full = jnp.concatenate([cw_parts[k] for k in range(N_CHIPS)], axis=2)

    shards = [{n: w[n][l].astype(MXU_DTYPE) for n in BIG} for l in range(L)]
    late_names = tuple(n for n in BIG if n not in GATHER_FIRST)
    gathering = {(0, GATHER_FIRST): gather_start([shards[0][n] for n in GATHER_FIRST], cw_rows, "gather_start_0a")}
    gathering[0, late_names] = gather_start([shards[0][n] for n in late_names], gathering[0, GATHER_FIRST][4], "gather_start_0b")

    def gathered(l, names, after, tag):
        lands = gather_pass(gather_wait(gathering.pop((l, names)), after, f"gather_wait_{tag}"), f"gather_pass_{tag}")
        wl = {}
        for n, gw in zip(names, lands):
            rows_joined = gw.reshape(gw.shape[:-3] + (-1, gw.shape[-1]))
            if n in COL_BLOCKED:
                wl[n] = (gw, dict(b_blocks=N_CHIPS))
            elif n in GATE_WEIGHTS:
                wl[n] = rows_joined
            else:
                wl[n] = (rows_joined, {})
        return wl, lands

    def start_layer(l, after):
        if l >= L:
            return ()
        gathering[l, BIG] = gather_start([shards[l][n] for n in BIG], after, f"gather_start_{l}")
        return (gathering[l, BIG][4],)

    start_layer(1, gathering[0, late_names][4])

    def weights_of(l, h):
        deps, late = (), None
        if l == 0:
            wl, _ = gathered(0, GATHER_FIRST, h, "0a")

            def late(after):
                return gathered(0, late_names, after, "0b")[0], ()
        else:
            wl, lands = gathered(l, BIG, h, str(l))
            deps = start_layer(l + 1, lands[0])
        for n in SMALL:
            wl[n] = conv_full[l] if n == "conv_w" else w[n][l] if n == "sinks" else w[n][l][None, :]
        return wl, deps, late

    def for_chips(n, g):
        if n in COL_BLOCKED:
            return g
        if n in GATE_WEIGHTS:
            nb, bw, _ = g.shape
            g = g.reshape(nb, N_CHIPS, bw // N_CHIPS, bw).transpose(1, 0, 2, 3).reshape(N_CHIPS, nb * bw // N_CHIPS, bw)
        else:
            g = g.reshape(N_CHIPS, g.shape[0] // N_CHIPS, g.shape[1])
        return g.astype(MXU_DTYPE)

    reduced, scattering, small_grads = {}, {}, [None] * L
    late_grads = tuple(n for n in BIG if n not in SCATTER_FIRST)

    def start_scatter(l, names, g, after, tag):
        partial_sums = [for_chips(n, g[n])[None] for n in names]
        from_sibling = swap_sibling(partial_sums, f"grad_to_sibling_{tag}")
        chip_sums = [add_pair(place, a, b, f"grad_add_pair_{n}_{l}") for n, a, b in zip(names, partial_sums, from_sibling)]
        scattering[l, names] = scatter_start(chip_sums, after, f"grad_scatter_start_{tag}")
        return (scattering[l, names][4],)

    def finish_layer(l, after):
        for names in [k[1] for k in list(scattering) if k[0] == l]:
            tag = str(l) if names == BIG else f"{l}{'a' if names == SCATTER_FIRST else 'b'}"
            chip_sums, from_chips = scatter_wait(scattering.pop((l, names)), after, f"grad_scatter_wait_{tag}")
            for n, own, others in zip(names, chip_sums, from_chips):
                target = reduced.get(n, (L, 2 * own.shape[2], own.shape[3]))
                reduced[n] = add_chips(place, own, others, l, target, f"grad_add_chips_{n}_{l}")
        reduced.update(zip(BIG, join_halves([reduced[n] for n in BIG], l, f"grad_join_{l}")))

    def grads_halfway(l):
        def halfway(g, after):
            return start_scatter(l, SCATTER_FIRST, g, after, f"{l}a")

        return halfway

    def grads_done(l, g, dh):
        small_grads[l] = {n: g[n] for n in SMALL}
        deps = start_scatter(l, late_grads, g, dh, f"{l}b")
        if 1 < l + 1 < L:
            finish_layer(l + 1, dh)
        return deps

    loss11, dx = local_step(x[0], mem[0], loss_target[0], L, weights_of, grads_halfway, grads_done)
    loss = lax.psum(loss11[0, 0], ("x", "y", "c"))

    first = min(2, L)
    updated = {}
    if first < L:
        for n in BIG:
            updated[n] = adamw(w[n], reduced[n].reshape(w[n].shape), m[n], v[n], f"adamw_{n}_upper", layers=(first, L))
    behind = (jnp.stack([updated[n][0][(0,) * w[n].ndim] for n in updated]),) if updated else ()
    small_full = {n: jnp.stack([gl[n] for gl in small_grads]).reshape(w[n].shape[:1] + ((CONV_WIDTH, -1) if n == "conv_w" else (-1,)))
                  for n in SMALL}
    small_flat = jnp.concatenate([small_full[n].reshape(-1) for n in SMALL])
    small_sum = sum_devices(gather_devices(_pad_rows(small_flat), "gather_small_grads", behind), "sum_small_grads").reshape(-1)
    delta, new_m, new_v, grad = {}, {}, {}, {}
    off = 0
    for n in SMALL:
        gfull = small_sum[off:off + small_full[n].size].reshape(small_full[n].shape)
        off += small_full[n].size
        if n == "conv_w":
            width = conv_w.shape[2]
            gfull = lax.dynamic_slice_in_dim(gfull, chip * width, width, axis=2)
        delta[n], new_m[n], new_v[n], grad[n] = adamw(w[n], gfull, m[n], v[n], "adamw_" + n)
    after = jnp.stack([delta[n][(0,) * delta[n].ndim] for n in SMALL])
    for l in reversed(range(first)):
        finish_layer(l, after)

    for n in BIG:
        some = dict(layers=(0, first), into=updated[n]) if updated else {}
        delta[n], new_m[n], new_v[n], grad[n] = adamw(w[n], reduced[n].reshape(w[n].shape), m[n], v[n], "adamw_" + n, **some)
    for group in (delta, new_m, new_v, grad):
        group["w_in"] = jnp.swapaxes(group["w_in"], 1, 2)
    return (loss, dx[None], *[grad[n] for n in WEIGHTS], *[delta[n] for n in WEIGHTS], *[new_m[n] for n in WEIGHTS],
            *[new_v[n] for n in WEIGHTS])
```

```python
import functools
import math

import jax
import jax.numpy as jnp
import numpy as np
from jax import lax
from jax.experimental import pallas as pl
from jax.experimental.pallas import tpu as pltpu

F32 = jnp.float32
BF16 = jnp.bfloat16
MXU_DTYPE = BF16

HEAD_DIM = 64
N_KV_HEADS = 2
WINDOW = 128
ROT_DIM = HEAD_DIM // 4
ROPE_THETA = 500000.0
CROSS_HEADS = 4
RNN_BLOCKS = 4
CONV_WIDTH = 4
LRU_C = 8.0
LN_EPS = 1e-5
NEG_INF = -1e30
ADAM_LR = 0.001
ADAM_B1 = 0.9
ADAM_B2 = 0.999
ADAM_EPS = 1e-08
ADAM_WD = 0.01
ADAM_STEP = 10

VMEM_BYTES_V7X = 64 * 1024 * 1024
VMEM_BLOCK_BUDGET = 36 * 1024 * 1024
LANES = 128
SUBLANES = 8

MESH_ID = pl.DeviceIdType.MESH
N_CHIPS = 4
N_DEV = 8

BIG = ("w_in", "w_rg", "w_ig", "w_br_rnn", "w_br_attn", "w_out", "cq_w", "ckv_w", "co_w", "ffn_wi", "ffn_wo")
SHARD_AXIS = {"w_in": 0, "w_rg": 1, "w_ig": 1, "w_br_rnn": 0, "w_br_attn": 0, "w_out": 0, "cq_w": 0, "ckv_w": 1,
              "co_w": 0, "ffn_wi": 1, "ffn_wo": 0}
SMALL = ("conv_w", "conv_b", "b_rg", "b_ig", "lru_lambda", "sinks", "ln1_g", "ln1_b", "ln2_g", "ln2_b", "ln3_g", "ln3_b")
WEIGHTS = ("w_in", "conv_w", "conv_b", "w_rg", "b_rg", "w_ig", "b_ig", "lru_lambda", "w_br_rnn", "w_br_attn", "sinks",
           "w_out", "ln1_g", "ln1_b", "cq_w", "ckv_w", "co_w", "ln2_g", "ln2_b", "ffn_wi", "ffn_wo", "ln3_g", "ln3_b")
GATE_WEIGHTS = ("w_rg", "w_ig")
COL_BLOCKED = ("ckv_w", "ffn_wi")
GATHER_FIRST = ("w_in", "w_rg", "w_ig")
SCATTER_FIRST = ("ffn_wo", "ffn_wi", "co_w", "cq_w", "ckv_w")


def _params(dims=None, vmem=None):
    return pltpu.CompilerParams(dimension_semantics=dims, vmem_limit_bytes=vmem)


def _vmem_limit(block_bytes, temp_bytes=0):
    want = int(2 * block_bytes + temp_bytes) + (6 << 20)
    return max(32 << 20, min(want, VMEM_BYTES_V7X - (6 << 20)))


def _divisors(n, align, cap):
    out = [d for d in range(align, min(n, cap) + 1, align) if n % d == 0]
    if n <= cap and n not in out:
        out.append(n)
    return sorted(out, reverse=True) or [n]


PIN_MIN_ELEMENTS = 1 << 18


def hbm_call(body, **kw):
    def in_hbm(s):
        return pltpu.HBM(s.shape, s.dtype) if math.prod(s.shape) >= PIN_MIN_ELEMENTS else s

    shapes = kw.pop("out_shape")
    shapes = [in_hbm(s) for s in shapes] if isinstance(shapes, (list, tuple)) else in_hbm(shapes)
    call = pl.pallas_call(body, out_shape=shapes, **kw)

    def run(*args):
        return call(*[pltpu.with_memory_space_constraint(a, pltpu.HBM) if a.size >= PIN_MIN_ELEMENTS else a for a in args])

    return run


def _sigmoid(x):
    return 1.0 / (1.0 + jnp.exp(-x))


def _gelu_parts(x):
    c = math.sqrt(2.0 / math.pi)
    u = c * (x + 0.044715 * x * x * x)
    t = jnp.tanh(u)
    return t, c * (1.0 + 3 * 0.044715 * x * x)


def _gelu(x):
    t, _ = _gelu_parts(x)
    return 0.5 * x * (1.0 + t)


def _gelu_grad(x):
    t, du = _gelu_parts(x)
    return 0.5 * (1.0 + t) + 0.5 * x * (1.0 - t * t) * du


def _neg_expm1(x):
    series = x * (1.0 + x * (0.5 + x * (1.0 / 6 + x * (1.0 / 24 + x * (1.0 / 120)))))
    return -jnp.where(x > -0.1, series, jnp.exp(x) - 1.0)


def _softplus_neg(lam):
    x = -lam
    return jnp.maximum(x, 0.0) + jnp.log1p(jnp.exp(-jnp.abs(x)))


STEP_US = 0.35
HBM_BYTES_PER_US = 2.5e6
MXU_FLOPS_PER_US = 7e8


def _layer_norm(z, g, b):
    mu = jnp.mean(z, axis=-1, keepdims=True)
    zc = z - mu
    rs = lax.rsqrt(jnp.mean(zc * zc, axis=-1, keepdims=True) + LN_EPS)
    xh = zc * rs
    return xh * g + b, xh, rs


def mm(a, b, mode, name, *, b_index=(), a_blocks=0, b_blocks=0, out_blocks=0, out_dtype=F32, deps=(), post_norm=None, plus=None):
    nlead = len(b_index) + (1 if b_blocks else 0)
    bk, bn = b.shape[nlead:]
    M, K = (a.shape[-1], a.shape[-2]) if mode == "tn" else (a.shape[-2], a.shape[-1] * max(a_blocks, 1))
    N = bk if mode == "nt" else bn * max(b_blocks, 1) if mode == "nn" or mode == "tn" else bn
    asz, bsz, osz = a.dtype.itemsize, b.dtype.itemsize, jnp.dtype(out_dtype).itemsize
    n_unit = math.gcd(N // max(out_blocks, 1), N // max(b_blocks, 1) if mode != "nt" else N)
    k_unit = math.gcd(K // max(a_blocks, 1), K // max(b_blocks, 1) if mode == "nt" else K)
    tms = _divisors(M, LANES if mode == "tn" else SUBLANES, 2048)
    tns = [N] if post_norm else _divisors(n_unit, LANES, 2048)
    tks = _divisors(k_unit, LANES, k_unit)
    best = None
    for tm in tms:
        for tn in tns:
            for tk in tks:
                nk = K // tk
                scratch = tm * tn * 4 if (nk > 1 and osz != 4) else 0
                blocks = tm * tk * asz + tn * tk * bsz + tm * tn * osz * (3 if post_norm else 1)
                temps = tm * tk * (2 + (4 if mode == "tn" else 0)) + tn * tk * 2 + tm * tn * 4 + scratch
                if 2 * blocks + temps > VMEM_BLOCK_BUDGET + (8 << 20):
                    continue
                ni, nj = M // tm, N // tn
                traffic = M * K * asz * (nj if nk > 1 else 1) + N * K * bsz * (1 if nj * nk == 1 else ni) + M * N * osz
                busy = max(traffic / HBM_BYTES_PER_US, 2.0 * M * N * K / MXU_FLOPS_PER_US)
                cost = ni * nj * nk * STEP_US + busy + blocks / HBM_BYTES_PER_US
                if best is None or cost < best[0]:
                    best = (cost, tm, tn, tk, blocks, temps)
    _, tm, tn, tk, blocks, temps = best
    nk = K // tk
    use_scratch = nk > 1 and osz != 4

    def split(index, total, blocks, tile):
        per = total // blocks // tile
        return index // per, index % per

    def body(a_ref, b_ref, *rest):
        rest = rest[len(deps):]
        if post_norm:
            h_ref, g_ref, beta_ref, o_ref, xh_ref, rs_ref = rest[:6]
            acc = rest[6:]
        elif plus:
            plus_ref, o_ref, acc = rest[0], rest[1], rest[2:]
        else:
            o_ref, acc = rest[0], rest[1:]
        av = a_ref[...].astype(MXU_DTYPE)
        bv = b_ref[...].astype(MXU_DTYPE)
        dn = {"nn": (((1,), (0,)), ((), ())), "nt": (((1,), (1,)), ((), ())), "tn": (((0,), (0,)), ((), ()))}[mode]
        r = lax.dot_general(av, bv, dn, preferred_element_type=F32)

        def normalise(f):
            o_ref[...], xh_ref[...], rs_ref[...] = _layer_norm(post_norm[3] * h_ref[...] + f, g_ref[...], beta_ref[...])

        if nk == 1 and post_norm:
            normalise(r)
        elif nk == 1 and plus:
            o_ref[...] = plus[1] * plus_ref[...] + r
        elif nk == 1:
            o_ref[...] = r.astype(o_ref.dtype)
        else:
            acc_ref = acc[0] if use_scratch else o_ref

            @pl.when(pl.program_id(2) == 0)
            def _():
                acc_ref[...] = r

            @pl.when(pl.program_id(2) > 0)
            def _():
                acc_ref[...] += r

            if use_scratch:
                @pl.when(pl.program_id(2) == nk - 1)
                def _():
                    o_ref[...] = acc_ref[...].astype(o_ref.dtype)
            elif post_norm:
                @pl.when(pl.program_id(2) == nk - 1)
                def _():
                    normalise(o_ref[...])
            elif plus:
                @pl.when(pl.program_id(2) == nk - 1)
                def _():
                    o_ref[...] = plus[1] * plus_ref[...] + o_ref[...]

    if mode == "tn":
        a_spec = pl.BlockSpec((tk, tm), lambda i, j, k: (k, i))
    elif a_blocks:
        a_spec = pl.BlockSpec((None, tm, tk), lambda i, j, k: (split(k, K, a_blocks, tk)[0], i, split(k, K, a_blocks, tk)[1]))
    else:
        a_spec = pl.BlockSpec((tm, tk), lambda i, j, k: (i, k))
    lead = (None,) * nlead
    if mode == "nt":
        bmap = ((lambda i, j, k: b_index + (split(k, K, b_blocks, tk)[0], j, split(k, K, b_blocks, tk)[1])) if b_blocks
                else (lambda i, j, k: b_index + (j, k)))
        b_spec = pl.BlockSpec(lead + (tn, tk), bmap)
    else:
        bmap = ((lambda i, j, k: b_index + (split(j, N, b_blocks, tn)[0], k, split(j, N, b_blocks, tn)[1])) if b_blocks
                else (lambda i, j, k: b_index + (k, j)))
        b_spec = pl.BlockSpec(lead + (tk, tn), bmap)
    if out_blocks:
        o_spec = pl.BlockSpec((None, tm, tn), lambda i, j, k: (split(j, N, out_blocks, tn)[0], i, split(j, N, out_blocks, tn)[1]))
        o_shape = jax.ShapeDtypeStruct((out_blocks, M, N // out_blocks), out_dtype)
    else:
        o_spec = pl.BlockSpec((tm, tn), lambda i, j, k: (i, j))
        o_shape = jax.ShapeDtypeStruct((M, N), out_dtype)
    in_specs, extra = [a_spec, b_spec] + [pl.BlockSpec(memory_space=pl.ANY)] * len(deps), ()
    if post_norm:
        vec = pl.BlockSpec((1, N), lambda i, j, k: (0, 0))
        in_specs += [pl.BlockSpec((tm, N), lambda i, j, k: (i, 0)), vec, vec]
        o_spec = [o_spec, pl.BlockSpec((tm, N), lambda i, j, k: (i, 0)), pl.BlockSpec((tm, 1), lambda i, j, k: (i, 0))]
        o_shape = [o_shape, jax.ShapeDtypeStruct((M, N), F32), jax.ShapeDtypeStruct((M, 1), F32)]
        extra = post_norm[:3]
    elif plus:
        in_specs += [pl.BlockSpec((tm, tn), lambda i, j, k: (i, j))]
        extra = plus[:1]
    return hbm_call(
        body, name=name, grid=(M // tm, N // tn, nk), in_specs=in_specs, out_specs=o_spec, out_shape=o_shape,
        scratch_shapes=[pltpu.VMEM((tm, tn), F32)] if use_scratch else [],
        compiler_params=_params(("parallel", "parallel", "arbitrary"), _vmem_limit(blocks, temps)),
    )(a, b, *deps, *extra)


ROW_TILE = 512
GATE_ROWS = 1024


def ln_bwd(dy_a, dy_b, xh, rs, g, c1, name):
    S, D = xh.shape
    tr = min(ROW_TILE, S)
    two = dy_b is not None

    def body(*refs):
        if two:
            a_ref, b_ref, xh_ref, rs_ref, g_ref, dz_ref, dg_ref, db_ref = refs
            dy = c1 * a_ref[...] + b_ref[...]
        else:
            a_ref, xh_ref, rs_ref, g_ref, dz_ref, dg_ref, db_ref = refs
            dy = a_ref[...]
        x = xh_ref[...]
        dyg = dy * g_ref[...]
        m1 = jnp.mean(dyg, axis=-1, keepdims=True)
        m2 = jnp.mean(dyg * x, axis=-1, keepdims=True)
        dz_ref[...] = rs_ref[...] * (dyg - m1 - x * m2)

        @pl.when(pl.program_id(0) == 0)
        def _():
            dg_ref[...] = jnp.zeros_like(dg_ref)
            db_ref[...] = jnp.zeros_like(db_ref)

        dg_ref[...] += jnp.sum(dy * x, axis=0, keepdims=True)
        db_ref[...] += jnp.sum(dy, axis=0, keepdims=True)

    row = pl.BlockSpec((tr, D), lambda i: (i, 0))
    vec = pl.BlockSpec((1, D), lambda i: (0, 0))
    ins = [row, row] if two else [row]
    args = (dy_a, dy_b) if two else (dy_a,)
    return hbm_call(
        body, name=name, grid=(S // tr,), in_specs=ins + [row, pl.BlockSpec((tr, 1), lambda i: (i, 0)), vec],
        out_specs=[row, vec, vec],
        out_shape=[jax.ShapeDtypeStruct((S, D), F32), jax.ShapeDtypeStruct((1, D), F32), jax.ShapeDtypeStruct((1, D), F32)],
        compiler_params=_params(("arbitrary",), 48 << 20),
    )(*args, xh, rs, g)


def loss_head(y, t, name):
    S, D = y.shape
    tr = min(ROW_TILE, S)
    nsteps = S // tr

    def body(y_ref, t_ref, dy_ref, l_ref, acc_ref):
        i = pl.program_id(0)

        @pl.when(i == 0)
        def _():
            acc_ref[...] = jnp.zeros_like(acc_ref)

        e = y_ref[...] - t_ref[...]
        dy_ref[...] = e * (1.0 / D)
        acc_ref[...] += jnp.sum(e * e, axis=0, keepdims=True)

        @pl.when(i == nsteps - 1)
        def _():
            l_ref[...] = jnp.sum(acc_ref[...], axis=1, keepdims=True) * (0.5 / D)

    row = pl.BlockSpec((tr, D), lambda i: (i, 0))
    return hbm_call(
        body, name=name, grid=(nsteps,), in_specs=[row, row],
        out_specs=[row, pl.BlockSpec((1, 1), lambda i: (0, 0))],
        out_shape=[jax.ShapeDtypeStruct((S, D), F32), jax.ShapeDtypeStruct((1, 1), F32)],
        scratch_shapes=[pltpu.VMEM((1, D), F32)], compiler_params=_params(("arbitrary",)),
    )(y, t)


SWIGLU_ROWS = 256


def swiglu_fwd(gu, name):
    _, S, Fh = gu.shape
    tc = _divisors(Fh, LANES, 1536)[0]
    tr = min(SWIGLU_ROWS, S)

    def body(gu_ref, o_ref):
        g = gu_ref[0]
        o_ref[...] = (g * _sigmoid(g) * gu_ref[1]).astype(o_ref.dtype)

    return hbm_call(
        body, name=name, grid=(S // tr, Fh // tc), in_specs=[pl.BlockSpec((2, tr, tc), lambda i, j: (0, i, j))],
        out_specs=pl.BlockSpec((tr, tc), lambda i, j: (i, j)), out_shape=jax.ShapeDtypeStruct((S, Fh), MXU_DTYPE),
        compiler_params=_params(("parallel", "parallel")),
    )(gu)


def swiglu_bwd(gu, dact, name):
    _, S, Fh = gu.shape
    tc = _divisors(Fh, LANES, 1536)[0]
    tr = min(SWIGLU_ROWS, S)

    def body(gu_ref, d_ref, o_ref):
        g, u, d = gu_ref[0], gu_ref[1], d_ref[...]
        s = _sigmoid(g)
        o_ref[0] = (d * u * (s * (1.0 + g * (1.0 - s)))).astype(o_ref.dtype)
        o_ref[1] = (d * (g * s)).astype(o_ref.dtype)

    both = pl.BlockSpec((2, tr, tc), lambda i, j: (0, i, j))
    return hbm_call(
        body, name=name, grid=(S // tr, Fh // tc), in_specs=[both, pl.BlockSpec((tr, tc), lambda i, j: (i, j))],
        out_specs=both, out_shape=jax.ShapeDtypeStruct((2, S, Fh), MXU_DTYPE), compiler_params=_params(("parallel", "parallel")),
    )(gu, dact)


GATE_COLS = 256


def merge_fwd(proj, pr, pa, name):
    S, D = pr.shape
    tr = min(GATE_ROWS, S)
    c0 = (3 * D + 2 * N_KV_HEADS * HEAD_DIM) // GATE_COLS
    c1 = c0 + D // GATE_COLS

    def body(gr_ref, ga_ref, pr_ref, pa_ref, o_ref):
        o_ref[...] = (_sigmoid(gr_ref[...]) * pr_ref[...] + _sigmoid(ga_ref[...]) * pa_ref[...]).astype(o_ref.dtype)

    blk = pl.BlockSpec((tr, GATE_COLS), lambda i, j: (i, j))
    return hbm_call(
        body, name=name, grid=(S // tr, D // GATE_COLS),
        in_specs=[pl.BlockSpec((tr, GATE_COLS), lambda i, j: (i, c0 + j)), pl.BlockSpec((tr, GATE_COLS), lambda i, j: (i, c1 + j)),
                  blk, blk],
        out_specs=blk, out_shape=jax.ShapeDtypeStruct((S, D), MXU_DTYPE), compiler_params=_params(("parallel", "parallel")),
    )(proj, proj, pr, pa)


def merge_bwd(proj, pr, pa, dm, name):
    S, D = pr.shape
    tr = min(GATE_ROWS, S)
    c0 = (3 * D + 2 * N_KV_HEADS * HEAD_DIM) // GATE_COLS
    c1 = c0 + D // GATE_COLS

    def body(gr_ref, ga_ref, pr_ref, pa_ref, dm_ref, dpr_ref, dpa_ref, dgr_ref, dga_ref):
        sr, sa, d = _sigmoid(gr_ref[...]), _sigmoid(ga_ref[...]), dm_ref[...]
        dpr_ref[...] = (d * sr).astype(dpr_ref.dtype)
        dpa_ref[...] = (d * sa).astype(dpa_ref.dtype)
        dgr_ref[...] = (d * pr_ref[...] * (sr * (1.0 - sr))).astype(dgr_ref.dtype)
        dga_ref[...] = (d * pa_ref[...] * (sa * (1.0 - sa))).astype(dga_ref.dtype)

    blk = pl.BlockSpec((tr, GATE_COLS), lambda i, j: (i, j))
    sds = jax.ShapeDtypeStruct((S, D), MXU_DTYPE)
    return hbm_call(
        body, name=name, grid=(S // tr, D // GATE_COLS),
        in_specs=[pl.BlockSpec((tr, GATE_COLS), lambda i, j: (i, c0 + j)), pl.BlockSpec((tr, GATE_COLS), lambda i, j: (i, c1 + j)),
                  blk, blk, blk],
        out_specs=[blk, blk, blk, blk], out_shape=[sds, sds, sds, sds], compiler_params=_params(("parallel", "parallel")),
    )(proj, proj, pr, pa, dm)


RG_ROWS = 512


def _shift_down(cur, prev, d, row, first):
    halo = jnp.where(first, 0.0, pltpu.roll(prev, d, 0))
    return jnp.where(row >= d, pltpu.roll(cur, d, 0), halo)


def _shift_up(cur, nxt, d, row, last, tr):
    halo = jnp.where(last, 0.0, pltpu.roll(nxt, tr - d, 0))
    return jnp.where(row < tr - d, pltpu.roll(cur, tr - d, 0), halo)


def _lru_coeffs(r, lam):
    sp = _softplus_neg(lam)
    la = -LRU_C * r * sp
    return sp, la, jnp.exp(la), _neg_expm1(2.0 * la)


def rg_gates_fwd(proj, conv_w, conv_b, w_rg, b_rg, w_ig, b_ig, lam, name):
    S = proj.shape[0]
    nblk, bw, _ = w_rg.shape
    D = nblk * bw
    tr = min(RG_ROWS, S)

    def body(xr_ref, xp_ref, cw_ref, cb_ref, wr_ref, br_ref, wi_ref, bi_ref, lam_ref, xc_ref, r_ref, i_ref, a_ref, b_ref):
        first = pl.program_id(1) == 0
        cur, prev = xr_ref[...], xp_ref[...]
        row = lax.broadcasted_iota(jnp.int32, cur.shape, 0)
        xc = cb_ref[...]
        for k in range(CONV_WIDTH - 1):
            xc = xc + _shift_down(cur, prev, CONV_WIDTH - 1 - k, row, first) * cw_ref[k:k + 1, :]
        xc = xc + cur * cw_ref[CONV_WIDTH - 1:CONV_WIDTH, :]
        xm = xc.astype(MXU_DTYPE)
        r = _sigmoid(jnp.dot(xm, wr_ref[...].astype(MXU_DTYPE), preferred_element_type=F32) + br_ref[...])
        ig = _sigmoid(jnp.dot(xm, wi_ref[...].astype(MXU_DTYPE), preferred_element_type=F32) + bi_ref[...])
        _, _, a, em = _lru_coeffs(r, lam_ref[...])
        xc_ref[...] = xc
        r_ref[...] = r
        i_ref[...] = ig
        a_ref[...] = a
        b_ref[...] = jnp.sqrt(em) * (ig * xc)

    tile = pl.BlockSpec((tr, bw), lambda n, i: (i, n))
    vec = pl.BlockSpec((1, bw), lambda n, i: (0, n))
    wblk = pl.BlockSpec((None, bw, bw), lambda n, i: (n, 0, 0))
    sds = jax.ShapeDtypeStruct((S, D), F32)
    return hbm_call(
        body, name=name, grid=(nblk, S // tr),
        in_specs=[tile, pl.BlockSpec((tr, bw), lambda n, i: (jnp.maximum(i - 1, 0), n)),
                  pl.BlockSpec((CONV_WIDTH, bw), lambda n, i: (0, n)), vec, wblk, vec, wblk, vec, vec],
        out_specs=[tile] * 5, out_shape=[sds] * 5, compiler_params=_params(("parallel", "parallel")),
    )(proj, proj, conv_w, conv_b, w_rg, b_rg, w_ig, b_ig, lam)


SCAN_COLS = 256
CHUNK = SUBLANES
SCAN_UNROLL = 4


def rg_scan_fwd(proj, a, b, name):
    S, D = a.shape
    cb = min(SCAN_COLS, D)
    goff = D // cb

    def body(a_ref, b_ref, g_ref, hs_ref, y_ref):
        row = lax.broadcasted_iota(jnp.int32, (CHUNK, cb), 0)

        def step(c, carry):
            r0 = pl.multiple_of(c * CHUNK, CHUNK)
            A = a_ref[pl.ds(r0, CHUNK), :]
            B = b_ref[pl.ds(r0, CHUNK), :]
            for d in (1, 2, 4):
                As = jnp.where(row >= d, pltpu.roll(A, d, 0), 1.0)
                Bs = jnp.where(row >= d, pltpu.roll(B, d, 0), 0.0)
                B = A * Bs + B
                A = A * As
            hs_ref[pl.ds(r0, CHUNK), :] = B + A * carry
            a_end = jnp.sum(jnp.where(row == CHUNK - 1, A, 0.0), axis=0, keepdims=True)
            b_end = jnp.sum(jnp.where(row == CHUNK - 1, B, 0.0), axis=0, keepdims=True)
            return b_end + a_end * carry

        lax.fori_loop(0, S // CHUNK, step, jnp.zeros((1, cb), F32), unroll=SCAN_UNROLL)
        y_ref[...] = (hs_ref[...] * _gelu(g_ref[...])).astype(y_ref.dtype)

    col = pl.BlockSpec((S, cb), lambda j: (0, j))
    return hbm_call(
        body, name=name, grid=(D // cb,), in_specs=[col, col, pl.BlockSpec((S, cb), lambda j: (0, goff + j))],
        out_specs=[col, col], out_shape=[jax.ShapeDtypeStruct((S, D), F32), jax.ShapeDtypeStruct((S, D), MXU_DTYPE)],
        compiler_params=_params(("parallel",), _vmem_limit(5 * S * cb * 4, 4 * S * cb * 4)),
    )(a, b, proj)


def rg_scan_bwd(proj, dy, hs, a, name):
    S, D = a.shape
    cb = min(SCAN_COLS, D)
    goff = D // cb
    nchunks = S // CHUNK

    def body(g_ref, dy_ref, hs_ref, a_ref, dg_ref, gt_ref):
        gate, dy = g_ref[...], dy_ref[...]
        dg_ref[...] = (dy * hs_ref[...] * _gelu_grad(gate)).astype(dg_ref.dtype)
        gt_ref[...] = dy * _gelu(gate)
        row = lax.broadcasted_iota(jnp.int32, (CHUNK, cb), 0)

        def step(k, carry):
            c = nchunks - 1 - k
            r0 = pl.multiple_of(c * CHUNK, CHUNK)
            rn = pl.multiple_of(jnp.minimum(c + 1, nchunks - 1) * CHUNK, CHUNK)
            last = c == nchunks - 1
            nxt = jnp.where(last, 0.0, pltpu.roll(a_ref[pl.ds(rn, CHUNK), :], CHUNK - 1, 0))
            A = jnp.where(row < CHUNK - 1, pltpu.roll(a_ref[pl.ds(r0, CHUNK), :], CHUNK - 1, 0), nxt)
            B = gt_ref[pl.ds(r0, CHUNK), :]
            for d in (1, 2, 4):
                As = jnp.where(row < CHUNK - d, pltpu.roll(A, CHUNK - d, 0), 1.0)
                Bs = jnp.where(row < CHUNK - d, pltpu.roll(B, CHUNK - d, 0), 0.0)
                B = A * Bs + B
                A = A * As
            gt_ref[pl.ds(r0, CHUNK), :] = B + A * carry
            a_end = jnp.sum(jnp.where(row == 0, A, 0.0), axis=0, keepdims=True)
            b_end = jnp.sum(jnp.where(row == 0, B, 0.0), axis=0, keepdims=True)
            return b_end + a_end * carry

        lax.fori_loop(0, nchunks, step, jnp.zeros((1, cb), F32), unroll=SCAN_UNROLL)

    col = pl.BlockSpec((S, cb), lambda j: (0, j))
    return hbm_call(
        body, name=name, grid=(D // cb,), in_specs=[pl.BlockSpec((S, cb), lambda j: (0, goff + j)), col, col, col],
        out_specs=[col, col], out_shape=[jax.ShapeDtypeStruct((S, D), MXU_DTYPE), jax.ShapeDtypeStruct((S, D), F32)],
        compiler_params=_params(("parallel",), _vmem_limit(6 * S * cb * 4, 6 * S * cb * 4)),
    )(proj, dy, hs, a)


def rg_gates_bwd(gt, hs, xc, r, ig, w_rg, w_ig, lam, name):
    S, D = xc.shape
    nblk, bw, _ = w_rg.shape
    tr = min(RG_ROWS, S)

    def body(gt_ref, hs_ref, hp_ref, xc_ref, r_ref, i_ref, wr_ref, wi_ref, lam_ref,
             dxc_ref, dwr_ref, dwi_ref, dbr_ref, dbi_ref, dl_ref):
        step = pl.program_id(1)
        g, hs, xc, r, ig, lam = gt_ref[...], hs_ref[...], xc_ref[...], r_ref[...], i_ref[...], lam_ref[...]
        row = lax.broadcasted_iota(jnp.int32, g.shape, 0)
        hprev = _shift_down(hs, hp_ref[...], 1, row, step == 0)
        sp, _, a, em = _lru_coeffs(r, lam)
        mult = jnp.sqrt(em)
        du = g * mult
        dla = g * hprev * a - (g * (ig * xc)) * (a * a) / mult
        dpr = (dla * (-LRU_C * sp)) * (r * (1.0 - r))
        dpi = (du * xc) * (ig * (1.0 - ig))
        dprm, dpim = dpr.astype(MXU_DTYPE), dpi.astype(MXU_DTYPE)
        nt = (((1,), (1,)), ((), ()))
        dxc_ref[...] = (du * ig + lax.dot_general(dprm, wr_ref[...].astype(MXU_DTYPE), nt, preferred_element_type=F32)
                        + lax.dot_general(dpim, wi_ref[...].astype(MXU_DTYPE), nt, preferred_element_type=F32))

        @pl.when(step == 0)
        def _():
            for ref in (dwr_ref, dwi_ref, dbr_ref, dbi_ref, dl_ref):
                ref[...] = jnp.zeros_like(ref)

        xct = xc.T.astype(MXU_DTYPE)
        dwr_ref[...] += jnp.dot(xct, dprm, preferred_element_type=F32)
        dwi_ref[...] += jnp.dot(xct, dpim, preferred_element_type=F32)
        dbr_ref[...] += jnp.sum(dpr, axis=0, keepdims=True)
        dbi_ref[...] += jnp.sum(dpi, axis=0, keepdims=True)
        dl_ref[...] += jnp.sum(dla * (-LRU_C * r), axis=0, keepdims=True) * (-_sigmoid(-lam))

    tile = pl.BlockSpec((tr, bw), lambda n, i: (i, n))
    vec = pl.BlockSpec((1, bw), lambda n, i: (0, n))
    wblk = pl.BlockSpec((None, bw, bw), lambda n, i: (n, 0, 0))
    return hbm_call(
        body, name=name, grid=(nblk, S // tr),
        in_specs=[tile, tile, pl.BlockSpec((tr, bw), lambda n, i: (jnp.maximum(i - 1, 0), n)), tile, tile, tile, wblk, wblk, vec],
        out_specs=[tile, wblk, wblk, vec, vec, vec],
        out_shape=[jax.ShapeDtypeStruct((S, D), F32), jax.ShapeDtypeStruct((nblk, bw, bw), F32), jax.ShapeDtypeStruct((nblk, bw, bw), F32),
                   jax.ShapeDtypeStruct((1, D), F32), jax.ShapeDtypeStruct((1, D), F32), jax.ShapeDtypeStruct((1, D), F32)],
        compiler_params=_params(("parallel", "arbitrary")),
    )(gt, hs, hs, xc, r, ig, w_rg, w_ig, lam)


def rg_conv_bwd(proj, dxc, conv_w, name):
    S, D = dxc.shape
    bw = min(SCAN_COLS, D)
    tr = min(RG_ROWS, S)
    nsteps = S // tr

    def body(d_ref, dn_ref, xr_ref, xp_ref, cw_ref, dxr_ref, dcw_ref, dcb_ref):
        step = pl.program_id(1)
        d, xr = d_ref[...], xr_ref[...]
        row = lax.broadcasted_iota(jnp.int32, d.shape, 0)
        dxr = d * cw_ref[CONV_WIDTH - 1:CONV_WIDTH, :]
        for k in range(CONV_WIDTH - 1):
            dxr = dxr + _shift_up(d, dn_ref[...], CONV_WIDTH - 1 - k, row, step == nsteps - 1, tr) * cw_ref[k:k + 1, :]
        dxr_ref[...] = dxr.astype(dxr_ref.dtype)

        @pl.when(step == 0)
        def _():
            dcw_ref[...] = jnp.zeros_like(dcw_ref)
            dcb_ref[...] = jnp.zeros_like(dcb_ref)

        for k in range(CONV_WIDTH - 1):
            xs = _shift_down(xr, xp_ref[...], CONV_WIDTH - 1 - k, row, step == 0)
            dcw_ref[k:k + 1, :] += jnp.sum(d * xs, axis=0, keepdims=True)
        dcw_ref[CONV_WIDTH - 1:CONV_WIDTH, :] += jnp.sum(d * xr, axis=0, keepdims=True)
        dcb_ref[...] += jnp.sum(d, axis=0, keepdims=True)

    tile = pl.BlockSpec((tr, bw), lambda n, i: (i, n))
    cwb = pl.BlockSpec((CONV_WIDTH, bw), lambda n, i: (0, n))
    return hbm_call(
        body, name=name, grid=(D // bw, nsteps),
        in_specs=[tile, pl.BlockSpec((tr, bw), lambda n, i: (jnp.minimum(i + 1, nsteps - 1), n)), tile,
                  pl.BlockSpec((tr, bw), lambda n, i: (jnp.maximum(i - 1, 0), n)), cwb],
        out_specs=[tile, cwb, pl.BlockSpec((1, bw), lambda n, i: (0, n))],
        out_shape=[jax.ShapeDtypeStruct((S, D), MXU_DTYPE), jax.ShapeDtypeStruct((CONV_WIDTH, D), F32), jax.ShapeDtypeStruct((1, D), F32)],
        compiler_params=_params(("parallel", "arbitrary")),
    )(dxc, dxc, proj, proj, conv_w)


def rope_table(S):
    half = ROT_DIM // 2
    pos = jnp.arange(S, dtype=F32)
    inv = ROPE_THETA ** (-jnp.arange(0, ROT_DIM, 2, dtype=F32) / ROT_DIM)
    ang = pos[:, None] * inv[None, :]
    cos, sin = jnp.cos(ang), jnp.sin(ang)
    zero = jnp.zeros((S, HEAD_DIM - ROT_DIM), F32)
    c = jnp.concatenate([cos, cos, zero + 1.0], axis=1)
    a = jnp.concatenate([-sin, jnp.zeros((S, half), F32), zero], axis=1)
    b = jnp.concatenate([jnp.zeros((S, half), F32), sin, zero], axis=1)
    return jnp.stack([jnp.tile(t, (1, LANES // HEAD_DIM)) for t in (c, a, b)])


def _rope(t, tab):
    half = ROT_DIM // 2
    return t * tab[0] + pltpu.roll(t, LANES - half, 1) * tab[1] + pltpu.roll(t, half, 1) * tab[2]


def _rope_t(d, tab):
    half = ROT_DIM // 2
    return d * tab[0] + pltpu.roll(d * tab[1], half, 1) + pltpu.roll(d * tab[2], LANES - half, 1)


def _dup_head(t, hk, lo):
    sw = pltpu.roll(t, HEAD_DIM, 1)
    return jnp.where(lo, t, sw) if hk == 0 else jnp.where(lo, sw, t)


def _attn_common(n, sink_ref, q_ref, kp_ref, kc_ref, vp_ref, vc_ref, tc_ref, tp_ref, hk, pairs):
    tq = (tc_ref[0], tc_ref[1], tc_ref[2])
    tp = (tp_ref[0], tp_ref[1], tp_ref[2])
    lo = lax.broadcasted_iota(jnp.int32, (WINDOW, LANES), 1) < HEAD_DIM
    lo2 = lax.broadcasted_iota(jnp.int32, (2 * WINDOW, LANES), 1) < HEAD_DIM
    kband = jnp.concatenate([_rope(kp_ref[...], tp), _rope(kc_ref[...], tq)], axis=0)
    vband = jnp.concatenate([vp_ref[...], vc_ref[...]], axis=0)
    kd = _dup_head(kband, hk, lo2).astype(MXU_DTYPE)
    vd = _dup_head(vband, hk, lo2).astype(MXU_DTYPE)
    rows, sks = [], []
    for j in range(pairs):
        col = hk * pairs + j
        qp = _rope(q_ref[:, col * LANES:(col + 1) * LANES], tq)
        rows += [jnp.where(lo, qp, 0.0), jnp.where(lo, 0.0, qp)]
        sks += [jnp.full((WINDOW, 1), sink_ref[2 * col], F32), jnp.full((WINDOW, 1), sink_ref[2 * col + 1], F32)]
    qg = jnp.concatenate(rows, axis=0)
    sk = jnp.concatenate(sks, axis=0)
    G = 2 * pairs * WINDOW
    own = lax.broadcasted_iota(jnp.int32, (G, WINDOW), 1) <= (lax.broadcasted_iota(jnp.int32, (G, WINDOW), 0) & (WINDOW - 1))
    s = lax.dot_general(qg.astype(MXU_DTYPE), kd, (((1,), (1,)), ((), ())), preferred_element_type=F32) * (HEAD_DIM ** -0.5)
    s = jnp.where(own, s[:, WINDOW:], s[:, :WINDOW] + jnp.where(n > 0, 0.0, NEG_INF))
    m = jnp.maximum(jnp.max(s, axis=1, keepdims=True), sk)
    e = jnp.exp(s - m)
    es = jnp.exp(sk - m)
    inv = 1.0 / (jnp.sum(e, axis=1, keepdims=True) + es)
    return qg, kd, vd, e * inv, es * inv, own, lo, lo2, tq, tp


def _unfold_band(t, own):
    return jnp.concatenate([jnp.where(own, 0.0, t), jnp.where(own, t, 0.0)], axis=1)


def _attn_specs(D, NB):
    kcol = 3 * D // LANES
    q = pl.BlockSpec((WINDOW, D), lambda n: (n, 2))
    kc = pl.BlockSpec((WINDOW, LANES), lambda n: (n, kcol))
    kp = pl.BlockSpec((WINDOW, LANES), lambda n: (jnp.maximum(n - 1, 0), kcol))
    vc = pl.BlockSpec((WINDOW, LANES), lambda n: (n, kcol + 1))
    vp = pl.BlockSpec((WINDOW, LANES), lambda n: (jnp.maximum(n - 1, 0), kcol + 1))
    tc = pl.BlockSpec((3, WINDOW, LANES), lambda n: (0, n, 0))
    tp = pl.BlockSpec((3, WINDOW, LANES), lambda n: (0, jnp.maximum(n - 1, 0), 0))
    sink = pl.BlockSpec(memory_space=pltpu.SMEM)
    return [sink, q, kp, kc, vp, vc, tc, tp]


def attn_fwd(proj, sinks, tab, D, name):
    S = proj.shape[0]
    NB = S // WINDOW
    pairs = D // HEAD_DIM // N_KV_HEADS // 2

    def body(sink_ref, q_ref, kp_ref, kc_ref, vp_ref, vc_ref, tc_ref, tp_ref, o_ref):
        n = pl.program_id(0)
        for hk in range(N_KV_HEADS):
            _, _, vd, p, _, own, lo, _, _, _ = _attn_common(n, sink_ref, q_ref, kp_ref, kc_ref, vp_ref, vc_ref, tc_ref, tp_ref, hk, pairs)
            o = jnp.dot(_unfold_band(p, own).astype(MXU_DTYPE), vd, preferred_element_type=F32)
            for j in range(pairs):
                col = hk * pairs + j
                oa = o[(2 * j) * WINDOW:(2 * j + 1) * WINDOW]
                ob = o[(2 * j + 1) * WINDOW:(2 * j + 2) * WINDOW]
                o_ref[:, col * LANES:(col + 1) * LANES] = jnp.where(lo, oa, ob)

    return hbm_call(
        body, name=name, grid=(NB,), in_specs=_attn_specs(D, NB),
        out_specs=pl.BlockSpec((WINDOW, D), lambda n: (n, 0)), out_shape=jax.ShapeDtypeStruct((S, D), F32),
        compiler_params=_params(("parallel",)),
    )(sinks, proj, proj, proj, proj, proj, tab, tab)


def attn_bwd(proj, sinks, tab, o, do, D, name):
    S = proj.shape[0]
    NB = S // WINDOW
    pairs = D // HEAD_DIM // N_KV_HEADS // 2

    def body(sink_ref, q_ref, kp_ref, kc_ref, vp_ref, vc_ref, tc_ref, tp_ref, o_ref, do_ref, dq_ref, dk_ref, dv_ref, ds_ref):
        n = pl.program_id(0)

        @pl.when(n == 0)
        def _():
            ds_ref[...] = jnp.zeros_like(ds_ref)

        lane1 = lax.broadcasted_iota(jnp.int32, (1, LANES), 1)
        dsink = jnp.zeros((1, LANES), F32)
        dkt = dvt = None
        for hk in range(N_KV_HEADS):
            qg, kd, vd, p, ps, own, lo, lo2, tq, tp = _attn_common(n, sink_ref, q_ref, kp_ref, kc_ref, vp_ref, vc_ref, tc_ref, tp_ref, hk, pairs)
            dos, os_ = [], []
            for j in range(pairs):
                col = hk * pairs + j
                dop = do_ref[:, col * LANES:(col + 1) * LANES]
                op = o_ref[:, col * LANES:(col + 1) * LANES]
                dos += [jnp.where(lo, dop, 0.0), jnp.where(lo, 0.0, dop)]
                os_ += [jnp.where(lo, op, 0.0), jnp.where(lo, 0.0, op)]
            dog = jnp.concatenate(dos, axis=0)
            og = jnp.concatenate(os_, axis=0)
            dogm = dog.astype(MXU_DTYPE)
            dp = lax.dot_general(dogm, vd, (((1,), (1,)), ((), ())), preferred_element_type=F32)
            dp = jnp.where(own, dp[:, WINDOW:], dp[:, :WINDOW])
            dr = jnp.sum(dog * og, axis=1, keepdims=True)
            ds = _unfold_band(p * (dp - dr) * (HEAD_DIM ** -0.5), own)
            dsm = ds.astype(MXU_DTYPE)
            dqg = jnp.dot(dsm, kd, preferred_element_type=F32)
            dkd = jnp.dot(ds.T.astype(MXU_DTYPE), qg.astype(MXU_DTYPE), preferred_element_type=F32)
            dvd = jnp.dot(_unfold_band(p, own).T.astype(MXU_DTYPE), dogm, preferred_element_type=F32)
            dkf = dkd + pltpu.roll(dkd, HEAD_DIM, 1)
            dvf = dvd + pltpu.roll(dvd, HEAD_DIM, 1)
            if hk == 0:
                dkt, dvt = dkf, dvf
            else:
                dkt, dvt = jnp.where(lo2, dkt, dkf), jnp.where(lo2, dvt, dvf)
            sd = ps * dr
            for j in range(pairs):
                col = hk * pairs + j
                dqa = dqg[(2 * j) * WINDOW:(2 * j + 1) * WINDOW]
                dqb = dqg[(2 * j + 1) * WINDOW:(2 * j + 2) * WINDOW]
                dq_ref[:, col * LANES:(col + 1) * LANES] = _rope_t(jnp.where(lo, dqa, dqb), tq).astype(dq_ref.dtype)
                for t in range(2):
                    part = sd[(2 * j + t) * WINDOW:(2 * j + t + 1) * WINDOW]
                    val = jnp.sum(part, axis=0, keepdims=True)
                    dsink = dsink - jnp.where(lane1 == 2 * col + t, val, 0.0)
        dk_ref[...] = jnp.concatenate([_rope_t(dkt[:WINDOW], tp), _rope_t(dkt[WINDOW:], tq)], axis=0)
        dv_ref[...] = dvt
        ds_ref[...] += dsink

    blk = pl.BlockSpec((WINDOW, D), lambda n: (n, 0))
    band = pl.BlockSpec((None, 2 * WINDOW, LANES), lambda n: (n, 0, 0))
    return hbm_call(
        body, name=name, grid=(NB,), in_specs=_attn_specs(D, NB) + [blk, blk],
        out_specs=[blk, band, band, pl.BlockSpec((1, LANES), lambda n: (0, 0))],
        out_shape=[jax.ShapeDtypeStruct((S, D), MXU_DTYPE), jax.ShapeDtypeStruct((NB, 2 * WINDOW, LANES), F32),
                   jax.ShapeDtypeStruct((NB, 2 * WINDOW, LANES), F32), jax.ShapeDtypeStruct((1, LANES), F32)],
        compiler_params=_params(("arbitrary",)),
    )(sinks, proj, proj, proj, proj, proj, tab, tab, o, do)


def band_fold(dkb, dvb, name):
    NB = dkb.shape[0]
    k4 = dkb.reshape(NB, 2, WINDOW, LANES)
    v4 = dvb.reshape(NB, 2, WINDOW, LANES)

    def body(kc_ref, kn_ref, vc_ref, vn_ref, dk_ref, dv_ref):
        more = pl.program_id(0) < NB - 1
        dk_ref[...] = (kc_ref[...] + jnp.where(more, kn_ref[...], 0.0)).astype(dk_ref.dtype)
        dv_ref[...] = (vc_ref[...] + jnp.where(more, vn_ref[...], 0.0)).astype(dv_ref.dtype)

    cur = pl.BlockSpec((None, None, WINDOW, LANES), lambda n: (n, 1, 0, 0))
    nxt = pl.BlockSpec((None, None, WINDOW, LANES), lambda n: (jnp.minimum(n + 1, NB - 1), 0, 0, 0))
    out = pl.BlockSpec((WINDOW, LANES), lambda n: (n, 0))
    sds = jax.ShapeDtypeStruct((NB * WINDOW, LANES), MXU_DTYPE)
    return hbm_call(body, name=name, grid=(NB,), in_specs=[cur, nxt, cur, nxt], out_specs=[out, out], out_shape=[sds, sds],
                          compiler_params=_params(("parallel",)))(k4, k4, v4, v4)


CROSS_ROWS = 512


def _cross_probs(q, k, scale):
    s = lax.dot_general(q.astype(MXU_DTYPE), k.astype(MXU_DTYPE), (((1,), (1,)), ((), ())), preferred_element_type=F32) * scale
    e = jnp.exp(s - jnp.max(s, axis=1, keepdims=True))
    return e / jnp.sum(e, axis=1, keepdims=True)


def cross_fwd(qc, kv, name):
    S, D = qc.shape
    M = kv.shape[0]
    hd = D // CROSS_HEADS
    tq = min(CROSS_ROWS, S)

    def body(q_ref, kv_ref, o_ref):
        for h in range(CROSS_HEADS):
            p = _cross_probs(q_ref[:, h * hd:(h + 1) * hd], kv_ref[:, h * hd:(h + 1) * hd], hd ** -0.5)
            v = kv_ref[:, D + h * hd:D + (h + 1) * hd].astype(MXU_DTYPE)
            o_ref[:, h * hd:(h + 1) * hd] = jnp.dot(p.astype(MXU_DTYPE), v, preferred_element_type=F32).astype(o_ref.dtype)

    return hbm_call(
        body, name=name, grid=(S // tq,), in_specs=[pl.BlockSpec((tq, D), lambda i: (i, 0)), pl.BlockSpec((M, 2 * D), lambda i: (0, 0))],
        out_specs=pl.BlockSpec((tq, D), lambda i: (i, 0)), out_shape=jax.ShapeDtypeStruct((S, D), MXU_DTYPE),
        compiler_params=_params(("parallel",)),
    )(qc, kv)


def cross_bwd(qc, kv, do, name):
    S, D = qc.shape
    M = kv.shape[0]
    hd = D // CROSS_HEADS
    tq = min(CROSS_ROWS, S)

    def body(q_ref, kv_ref, do_ref, dq_ref, dkv_ref):
        @pl.when(pl.program_id(0) == 0)
        def _():
            dkv_ref[...] = jnp.zeros_like(dkv_ref)

        for h in range(CROSS_HEADS):
            q = q_ref[:, h * hd:(h + 1) * hd]
            k = kv_ref[:, h * hd:(h + 1) * hd]
            v = kv_ref[:, D + h * hd:D + (h + 1) * hd].astype(MXU_DTYPE)
            dom = do_ref[:, h * hd:(h + 1) * hd].astype(MXU_DTYPE)
            p = _cross_probs(q, k, hd ** -0.5)
            dp = lax.dot_general(dom, v, (((1,), (1,)), ((), ())), preferred_element_type=F32)
            ds = p * (dp - jnp.sum(p * dp, axis=1, keepdims=True)) * (hd ** -0.5)
            dq_ref[:, h * hd:(h + 1) * hd] = jnp.dot(ds.astype(MXU_DTYPE), k.astype(MXU_DTYPE),
                                                     preferred_element_type=F32).astype(dq_ref.dtype)
            dkv_ref[:, h * hd:(h + 1) * hd] += jnp.dot(ds.T.astype(MXU_DTYPE), q.astype(MXU_DTYPE), preferred_element_type=F32)
            dkv_ref[:, D + h * hd:D + (h + 1) * hd] += jnp.dot(p.T.astype(MXU_DTYPE), dom, preferred_element_type=F32)

    row = pl.BlockSpec((tq, D), lambda i: (i, 0))
    full = pl.BlockSpec((M, 2 * D), lambda i: (0, 0))
    return hbm_call(
        body, name=name, grid=(S // tq,), in_specs=[row, full, row], out_specs=[row, full],
        out_shape=[jax.ShapeDtypeStruct((S, D), MXU_DTYPE), jax.ShapeDtypeStruct((M, 2 * D), F32)],
        compiler_params=_params(("arbitrary",)),
    )(qc, kv, do)


def adamw(w, g, m, v, name, layers=None, into=None):
    shape = w.shape
    cols = shape[-1]
    lead = shape[0] if len(shape) > 2 else 1
    rows = int(np.prod(shape[:-1])) // lead
    w2, g2, m2, v2 = (t.reshape(lead, rows, cols) for t in (w, g, m, v))
    tr = _divisors(rows, SUBLANES, max(SUBLANES, (1 << 20) // (cols * 4) // SUBLANES * SUBLANES))[0]
    lo, hi = layers or (0, lead)
    done = [t.reshape(lead, rows, cols) for t in into] if into else []

    def body(w_ref, g_ref, m_ref, v_ref, *refs):
        d_ref, mo_ref, vo_ref, go_ref = refs[len(done):]
        gg = g_ref[...]
        mn = ADAM_B1 * m_ref[...] + (1.0 - ADAM_B1) * gg
        vn = ADAM_B2 * v_ref[...] + (1.0 - ADAM_B2) * (gg * gg)
        m_hat = mn / (1.0 - ADAM_B1 ** ADAM_STEP)
        v_hat = vn / (1.0 - ADAM_B2 ** ADAM_STEP)
        d_ref[...] = -ADAM_LR * (m_hat / (jnp.sqrt(v_hat) + ADAM_EPS) + ADAM_WD * w_ref[...])
        mo_ref[...] = mn
        vo_ref[...] = vn
        go_ref[...] = gg

    blk = pl.BlockSpec((None, tr, cols), lambda l, i: (l + lo, i, 0))
    sds = jax.ShapeDtypeStruct((lead, rows, cols), F32)
    d, mn, vn, go = hbm_call(body, name=name, grid=(hi - lo, rows // tr), in_specs=[blk] * 4 + [pl.BlockSpec(memory_space=pl.ANY)] * len(done),
                             out_specs=[blk] * 4, out_shape=[sds] * 4, input_output_aliases={4 + k: k for k in range(len(done))},
                             compiler_params=_params(("parallel", "parallel")))(w2, g2, m2, v2, *done)
    return d.reshape(shape), mn.reshape(shape), vn.reshape(shape), go.reshape(shape)


def sum_devices(parts, name):
    n, rows, cols = parts.shape

    def body(p_ref, o_ref):
        acc = p_ref[0]
        for k in range(1, n):
            acc = acc + p_ref[k]
        o_ref[...] = acc

    return pl.pallas_call(body, name=name, in_specs=[pl.BlockSpec(memory_space=pltpu.VMEM)],
                          out_specs=pl.BlockSpec(memory_space=pltpu.VMEM), out_shape=jax.ShapeDtypeStruct((rows, cols), F32))(parts)


HBM_SPEC = pl.BlockSpec(memory_space=pltpu.HBM)


def _place():
    return lax.axis_index("x"), lax.axis_index("y"), lax.axis_index("c")


def _remote(src, dst, send_sems, recv_sems, k, to):
    return pltpu.make_async_remote_copy(src_ref=src, dst_ref=dst, send_sem=send_sems.at[k], recv_sem=recv_sems.at[k],
                                        device_id=to, device_id_type=MESH_ID)


SEM_SPEC = pl.BlockSpec(memory_space=pltpu.SEMAPHORE)
ANY_SPEC = pl.BlockSpec(memory_space=pl.ANY)
SPLIT_COPY = pltpu.CompilerParams(has_side_effects=pltpu.SideEffectType.DATAFLOW_SIDE_EFFECTING)


def _in_hbm(arrays):
    return [pltpu.with_memory_space_constraint(a, pltpu.HBM) for a in arrays]


def _split_start(copies, sources, lands, after, n_sems, name):
    n = len(sources)

    def body(*refs):
        for cp in copies(refs[:n], refs[n:2 * n], refs[2 * n + 1], refs[2 * n + 2]):
            cp.start()
        refs[-1][...] = jnp.zeros_like(refs[-1])

    through = [pltpu.HBM(a.shape, a.dtype) for a in list(sources) + list(lands)]
    outs = pl.pallas_call(
        body, name=name, in_specs=[HBM_SPEC] * (2 * n) + [ANY_SPEC],
        out_specs=[SEM_SPEC, SEM_SPEC] + [HBM_SPEC] * (2 * n) + [pl.BlockSpec(memory_space=pltpu.VMEM)],
        out_shape=[pltpu.SemaphoreType.DMA((n_sems,)), pltpu.SemaphoreType.DMA((n_sems,))] + through
        + [jax.ShapeDtypeStruct((SUBLANES, LANES), F32)],
        input_output_aliases={i: 2 + i for i in range(2 * n)}, compiler_params=SPLIT_COPY,
    )(*_in_hbm(sources), *_in_hbm(lands), after)
    return outs[0], outs[1], outs[2:2 + n], outs[2 + n:2 + 2 * n], outs[-1]


def _split_wait(copies, send_sems, recv_sems, sources, lands, after, name):
    n = len(sources)

    def body(*refs):
        for cp in copies(refs[:n], refs[n:2 * n], refs[2 * n], refs[2 * n + 1]):
            cp.wait_send()
            cp.wait_recv()

    through = [pltpu.HBM(a.shape, a.dtype) for a in list(sources) + list(lands)]
    outs = pl.pallas_call(
        body, name=name, in_specs=[HBM_SPEC] * (2 * n) + [SEM_SPEC, SEM_SPEC, ANY_SPEC], out_specs=[HBM_SPEC] * (2 * n),
        out_shape=through, input_output_aliases={i: i for i in range(2 * n)}, compiler_params=SPLIT_COPY,
    )(*sources, *lands, send_sems, recv_sems, after)
    return outs[:n], outs[n:]


def _chip_slab(land, slot, rows):
    return land.at[slot, rows] if len(land.shape) == 3 else land.at[rows, slot]


def _gather_copies(w_refs, land_refs, send_sems, recv_sems):
    n = len(w_refs)
    x, y, c = _place()
    chips = [(1 - x, y), (x, 1 - y), (1 - x, 1 - y)]
    cps = []
    for a in range(n):
        hr = w_refs[a].shape[0] // 2
        mine, every = pl.ds(c * hr, hr), pl.ds(0, 2 * hr)
        cps.append(_remote(w_refs[a], _chip_slab(land_refs[a], 2 * x + y, every), send_sems, recv_sems, 3 * n + a, (x, y, 1 - c)))
        for k, chip in enumerate(chips):
            cps.append(_remote(w_refs[a].at[mine], _chip_slab(land_refs[a], 2 * x + y, mine), send_sems, recv_sems, 3 * a + k, (*chip, c)))
    return cps


def gather_start(shards, after, name):
    lands = [lax.empty(s.shape[:-2] + (N_CHIPS,) + s.shape[-2:], s.dtype) for s in shards]
    return _split_start(_gather_copies, shards, lands, after, 4 * len(shards), name)


def gather_wait(state, after, name):
    send_sems, recv_sems, sources, lands, _ = state
    return _split_wait(_gather_copies, send_sems, recv_sems, sources, lands, after, name)[1]


def gather_pass(lands, name):
    n = len(lands)

    def body(*refs):
        out_refs, send_sems, recv_sems = refs[n:2 * n], refs[2 * n], refs[2 * n + 1]
        x, y, c = _place()
        chips = [(1 - x, y), (x, 1 - y), (1 - x, 1 - y)]
        sent = []
        for a in range(n):
            hr = out_refs[a].shape[0 if len(out_refs[a].shape) == 4 else 1] // 2
            for k, (px, py) in enumerate(chips):
                landed = _chip_slab(out_refs[a], 2 * px + py, pl.ds(c * hr, hr))
                sent.append(_remote(landed, landed, send_sems, recv_sems, 3 * a + k, (x, y, 1 - c)))
        for cp in sent:
            cp.start()
        for a in range(n):
            hr = out_refs[a].shape[0 if len(out_refs[a].shape) == 4 else 1] // 2
            for k, (px, py) in enumerate(chips):
                theirs = _chip_slab(out_refs[a], 2 * px + py, pl.ds((1 - c) * hr, hr))
                _remote(theirs, theirs, send_sems, recv_sems, 3 * a + k, (x, y, 1 - c)).wait_recv()
        for cp in sent:
            cp.wait_send()

    return hbm_call(
        body, name=name, in_specs=[HBM_SPEC] * n, out_specs=[HBM_SPEC] * n,
        out_shape=[jax.ShapeDtypeStruct(a.shape, a.dtype) for a in lands], input_output_aliases={a: a for a in range(n)},
        scratch_shapes=[pltpu.SemaphoreType.DMA((3 * n,))] * 2,
    )(*lands)


def _scatter_copies(t_refs, land_refs, send_sems, recv_sems):
    x, y, c = _place()
    chips = [(1 - x, y), (x, 1 - y), (1 - x, 1 - y)]
    return [_remote(t_refs[a].at[:, 2 * px + py], land_refs[a].at[:, k], send_sems, recv_sems, 3 * a + k, (px, py, c))
            for a in range(len(t_refs)) for k, (px, py) in enumerate(chips)]


def scatter_start(parts, after, name):
    lands = [lax.empty((t.shape[0], N_CHIPS - 1) + t.shape[2:], t.dtype) for t in parts]
    return _split_start(_scatter_copies, parts, lands, after, 3 * len(parts), name)


def scatter_wait(state, after, name):
    send_sems, recv_sems, sources, lands, _ = state
    return _split_wait(_scatter_copies, send_sems, recv_sems, sources, lands, after, name)


def swap_sibling(parts, name):
    n = len(parts)

    def body(*refs):
        v_refs, out_refs, send_sems, recv_sems = refs[:n], refs[n:2 * n], refs[2 * n], refs[2 * n + 1]
        x, y, c = _place()
        cps = []
        for a in range(n):
            hr = v_refs[a].shape[2] // 2
            cps.append(_remote(v_refs[a].at[:, :, pl.ds((1 - c) * hr, hr)], out_refs[a], send_sems, recv_sems, a, (x, y, 1 - c)))
        for cp in cps:
            cp.start()
        for cp in cps:
            cp.wait()

    return hbm_call(
        body, name=name, in_specs=[HBM_SPEC] * n, out_specs=[HBM_SPEC] * n,
        out_shape=[jax.ShapeDtypeStruct(v.shape[:2] + (v.shape[2] // 2, v.shape[3]), v.dtype) for v in parts],
        scratch_shapes=[pltpu.SemaphoreType.DMA((n,))] * 2,
    )(*parts)


def join_halves(halves, layer, name):
    n = len(halves)

    def body(*refs):
        out_refs, send_sems, recv_sems = refs[n:2 * n], refs[2 * n], refs[2 * n + 1]
        x, y, c = _place()
        cps = []
        for a in range(n):
            hr = out_refs[a].shape[1] // 2
            mine = out_refs[a].at[layer, pl.ds(c * hr, hr)]
            cps.append(_remote(mine, mine, send_sems, recv_sems, a, (x, y, 1 - c)))
        for cp in cps:
            cp.start()
        for a in range(n):
            hr = out_refs[a].shape[1] // 2
            theirs = out_refs[a].at[layer, pl.ds((1 - c) * hr, hr)]
            _remote(theirs, theirs, send_sems, recv_sems, a, (x, y, 1 - c)).wait_recv()
        for cp in cps:
            cp.wait_send()

    return hbm_call(
        body, name=name, in_specs=[HBM_SPEC] * n, out_specs=[HBM_SPEC] * n,
        out_shape=[jax.ShapeDtypeStruct(f.shape, f.dtype) for f in halves], input_output_aliases={a: a for a in range(n)},
        scratch_shapes=[pltpu.SemaphoreType.DMA((n,))] * 2,
    )(*halves)


def gather_devices(v, name, after=()):
    def body(v_ref, *refs):
        out_ref, send_sems, recv_sems, local_sem = refs[len(after):]
        x, y, c = _place()
        me = 4 * x + 2 * y + c
        own = pltpu.make_async_copy(v_ref, out_ref.at[me], local_sem)
        own.start()
        peers = [((x + dx) % 2, (y + dy) % 2, (c + dc) % 2) for dx in (0, 1) for dy in (0, 1) for dc in (0, 1)][1:]
        sent = []
        for k, peer in enumerate(peers):
            cp = pltpu.make_async_remote_copy(src_ref=v_ref, dst_ref=out_ref.at[me], send_sem=send_sems.at[k], recv_sem=recv_sems.at[k],
                                              device_id=peer, device_id_type=MESH_ID)
            cp.start()
            sent.append(cp)
        for k, (px, py, pc) in enumerate(peers):
            slot = out_ref.at[4 * px + 2 * py + pc]
            pltpu.make_async_remote_copy(src_ref=slot, dst_ref=slot, send_sem=send_sems.at[k], recv_sem=recv_sems.at[k],
                                         device_id=(px, py, pc), device_id_type=MESH_ID).wait_recv()
        for cp in sent:
            cp.wait_send()
        own.wait()

    vm = pl.BlockSpec(memory_space=pltpu.VMEM)
    return pl.pallas_call(body, name=name, in_specs=[vm] + [ANY_SPEC] * len(after), out_specs=vm,
                          out_shape=jax.ShapeDtypeStruct((N_DEV,) + v.shape, v.dtype),
                          scratch_shapes=[pltpu.SemaphoreType.DMA((N_DEV - 1,)), pltpu.SemaphoreType.DMA((N_DEV - 1,)),
                                          pltpu.SemaphoreType.DMA])(v, *after)


ADD_ROWS = 512


def add_pair(place, a, b, name):
    L, n, hr, cols = b.shape
    tr = _divisors(hr, 2 * SUBLANES, ADD_ROWS)[0]
    nb = hr // tr

    def body(p_ref, a_ref, b_ref, o_ref):
        del p_ref
        o_ref[...] = (a_ref[...].astype(F32) + b_ref[...].astype(F32)).astype(o_ref.dtype)

    blk = pl.BlockSpec((None, None, tr, cols), lambda l, d, i, p: (l, d, i, 0))
    grid_spec = pltpu.PrefetchScalarGridSpec(
        num_scalar_prefetch=1, grid=(L, n, nb),
        in_specs=[pl.BlockSpec((None, None, tr, cols), lambda l, d, i, p: (l, d, p[0] * nb + i, 0)), blk], out_specs=blk)
    return hbm_call(body, name=name, grid_spec=grid_spec, out_shape=jax.ShapeDtypeStruct(b.shape, b.dtype),
                          compiler_params=_params(("parallel", "parallel", "parallel")))(place, a, b)


def add_chips(place, own, others, layer, stacked, name):
    _, n, hr, cols = others.shape
    tr = _divisors(hr, 2 * SUBLANES, ADD_ROWS)[0]
    nb = hr // tr
    create = isinstance(stacked, tuple)

    def body(p_ref, own_ref, *refs):
        del p_ref
        acc = own_ref[...].astype(F32)
        for k in range(n):
            acc = acc + refs[k][...].astype(F32)
        refs[-1][...] = acc

    ins = [pl.BlockSpec((None, None, tr, cols), lambda i, p: (0, p[1], i, 0))]
    ins += [pl.BlockSpec((None, None, tr, cols), functools.partial(lambda k, i, p: (0, k, i, 0), k)) for k in range(n)]
    grid_spec = pltpu.PrefetchScalarGridSpec(num_scalar_prefetch=1, grid=(nb,), in_specs=ins + ([] if create else [ANY_SPEC]),
                                             out_specs=pl.BlockSpec((None, tr, cols), lambda i, p: (layer, p[0] * nb + i, 0)))
    shape = stacked if create else stacked.shape
    return hbm_call(body, name=name, grid_spec=grid_spec, out_shape=jax.ShapeDtypeStruct(shape, F32),
                          input_output_aliases={} if create else {n + 2: 0},
                          compiler_params=_params(("parallel",)))(place, own, *([others] * n), *([] if create else [stacked]))


def _alpha(depth):
    return (2 * depth) ** 0.25


def _wmm(a, weight, mode, name, deps=(), **more):
    arr, how = weight
    return mm(a, arr, mode, name, deps=deps, **how, **more)


def layer_fwd(h, mem, w, tab, alpha, deps=(), late=None):
    D = h.shape[1]
    proj = _wmm(h, w["w_in"], "nt", "mm_proj", deps)
    xc, r, ig, a, b = rg_gates_fwd(proj, w["conv_w"], w["conv_b"], w["w_rg"], w["b_rg"], w["w_ig"], w["b_ig"], w["lru_lambda"], "rg_gates_fwd")
    hs, y_rnn = rg_scan_fwd(proj, a, b, "rg_scan_fwd")
    y_attn = attn_fwd(proj, w["sinks"], tab, D, "attn_fwd")
    deps = ()
    if late is not None:
        rest, deps = late(y_attn)
        w = {**w, **rest}
    pr = _wmm(y_rnn, w["w_br_rnn"], "nn", "mm_br_rnn", deps)
    pa = _wmm(y_attn, w["w_br_attn"], "nn", "mm_br_attn")
    merged = merge_fwd(proj, pr, pa, "merge_fwd")
    h1, xh1, rs1 = _wmm(merged, w["w_out"], "nn", "mm_out_ln1", post_norm=(h, w["ln1_g"], w["ln1_b"], alpha))
    qc = _wmm(h1, w["cq_w"], "nn", "mm_cq", out_dtype=MXU_DTYPE)
    kv = _wmm(mem, w["ckv_w"], "nn", "mm_ckv", out_dtype=MXU_DTYPE)
    o = cross_fwd(qc, kv, "cross_fwd")
    h2, xh2, rs2 = _wmm(o, w["co_w"], "nn", "mm_co_ln2", post_norm=(h1, w["ln2_g"], w["ln2_b"], alpha))
    gu = _wmm(h2, w["ffn_wi"], "nn", "mm_ffn_wi", out_blocks=2)
    act = swiglu_fwd(gu, "swiglu_fwd")
    h3, xh3, rs3 = _wmm(act, w["ffn_wo"], "nn", "mm_ffn_wo_ln3", post_norm=(h2, w["ln3_g"], w["ln3_b"], alpha))
    saved = dict(h=h, proj=proj, xc=xc, r=r, ig=ig, a=a, hs=hs, y_rnn=y_rnn, y_attn=y_attn, pr=pr, pa=pa, xh1=xh1, rs1=rs1, h1=h1,
                 qc=qc, kv=kv, o=o, xh2=xh2, rs2=rs2, h2=h2, gu=gu, xh3=xh3, rs3=rs3)
    return h3, saved, w


def layer_bwd(dh, mem, w, s, tab, alpha, deps=(), halfway=None):
    D = dh.shape[1]
    g = {}
    wg = dict(out_dtype=MXU_DTYPE)
    dz3, g["ln3_g"], g["ln3_b"] = ln_bwd(dh, None, s["xh3"], s["rs3"], w["ln3_g"], 1.0, "ln3_bwd")
    act = swiglu_fwd(s["gu"], "swiglu_refwd")
    g["ffn_wo"] = mm(act, dz3, "tn", "mm_d_ffn_wo", deps=deps, **wg)
    dact = _wmm(dz3, w["ffn_wo"], "nt", "mm_dact")
    dgu = swiglu_bwd(s["gu"], dact, "swiglu_bwd")
    g["ffn_wi"] = mm(s["h2"], dgu, "tn", "mm_d_ffn_wi", b_blocks=2, out_blocks=N_CHIPS, **wg)
    dh2 = _wmm(dgu, w["ffn_wi"], "nt", "mm_dh2", a_blocks=2)
    dz2, g["ln2_g"], g["ln2_b"] = ln_bwd(dz3, dh2, s["xh2"], s["rs2"], w["ln2_g"], alpha, "ln2_bwd")
    g["co_w"] = mm(s["o"], dz2, "tn", "mm_d_co", **wg)
    do = _wmm(dz2, w["co_w"], "nt", "mm_do", out_dtype=MXU_DTYPE)
    dqc, dkv = cross_bwd(s["qc"], s["kv"], do, "cross_bwd")
    g["cq_w"] = mm(s["h1"], dqc, "tn", "mm_d_cq", **wg)
    g["ckv_w"] = mm(mem, dkv, "tn", "mm_d_ckv", out_blocks=N_CHIPS, **wg)
    dh1 = _wmm(dqc, w["cq_w"], "nt", "mm_dh1")
    deps = halfway(g, dh1) if halfway is not None else ()
    dz1, g["ln1_g"], g["ln1_b"] = ln_bwd(dz2, dh1, s["xh1"], s["rs1"], w["ln1_g"], alpha, "ln1_bwd")
    merged = merge_fwd(s["proj"], s["pr"], s["pa"], "merge_refwd")
    g["w_out"] = mm(merged, dz1, "tn", "mm_d_out", deps=deps, **wg)
    dm = _wmm(dz1, w["w_out"], "nt", "mm_dmerged")
    dpr, dpa, dg_rnn, dg_attn = merge_bwd(s["proj"], s["pr"], s["pa"], dm, "merge_bwd")
    g["w_br_rnn"] = mm(s["y_rnn"], dpr, "tn", "mm_d_br_rnn", **wg)
    g["w_br_attn"] = mm(s["y_attn"], dpa, "tn", "mm_d_br_attn", **wg)
    dy_rnn = _wmm(dpr, w["w_br_rnn"], "nt", "mm_dy_rnn")
    dy_attn = _wmm(dpa, w["w_br_attn"], "nt", "mm_dy_attn")
    dq, dkb, dvb, dsink = attn_bwd(s["proj"], w["sinks"], tab, s["y_attn"], dy_attn, D, "attn_bwd")
    dk, dv = band_fold(dkb, dvb, "band_fold")
    g["sinks"] = dsink[:, :w["sinks"].shape[0]]
    dgr, gt = rg_scan_bwd(s["proj"], dy_rnn, s["hs"], s["a"], "rg_scan_bwd")
    dxc, g["w_rg"], g["w_ig"], g["b_rg"], g["b_ig"], g["lru_lambda"] = rg_gates_bwd(
        gt, s["hs"], s["xc"], s["r"], s["ig"], w["w_rg"], w["w_ig"], w["lru_lambda"], "rg_gates_bwd")
    dxr, g["conv_w"], g["conv_b"] = rg_conv_bwd(s["proj"], dxc, w["conv_w"], "rg_conv_bwd")
    dproj = jnp.concatenate([dxr, dgr, dq, dk, dv, dg_rnn, dg_attn], axis=1)
    g["w_in"] = mm(dproj, s["h"], "tn", "mm_d_in", **wg)
    return _wmm(dproj, w["w_in"], "nn", "mm_dh", plus=(dz1, alpha)), g


def local_step(x, mem, target, depth, weights_of, grads_halfway, grads_done):
    alpha = _alpha(depth)
    tab = rope_table(x.shape[0])
    h, saved, layers = x, [], []
    for l in range(depth):
        wl, deps, late = weights_of(l, h)
        h, s, wl = layer_fwd(h, mem, wl, tab, alpha, deps, late)
        layers.append(wl)
        saved.append(s)
    dh, loss = loss_head(h, target, "loss_head")
    deps = ()
    for l in reversed(range(depth)):
        dh, g = layer_bwd(dh, mem, layers[l], saved[l], tab, alpha, deps, grads_halfway(l))
        deps = grads_done(l, g, dh)
    return loss, dh


def _pad_rows(flat):
    n = flat.shape[0]
    rows = -(-n // (LANES * SUBLANES)) * SUBLANES
    return jnp.pad(flat, (0, rows * LANES - n)).reshape(rows, LANES)


def kernel(x, mem, w_in, conv_w, conv_b, w_rg, b_rg, w_ig, b_ig, lru_lambda, w_br_rnn, w_br_attn, sinks, w_out, ln1_g, ln1_b, cq_w, ckv_w, co_w, ln2_g, ln2_b, ffn_wi, ffn_wo, ln3_g, ln3_b, loss_target, m_w_in, m_conv_w, m_conv_b, m_w_rg, m_b_rg, m_w_ig, m_b_ig, m_lru_lambda, m_w_br_rnn, m_w_br_attn, m_sinks, m_w_out, m_ln1_g, m_ln1_b, m_cq_w, m_ckv_w, m_co_w, m_ln2_g, m_ln2_b, m_ffn_wi, m_ffn_wo, m_ln3_g, m_ln3_b, v_w_in, v_conv_w, v_conv_b, v_w_rg, v_b_rg, v_w_ig, v_b_ig, v_lru_lambda, v_w_br_rnn, v_w_br_attn, v_sinks, v_w_out, v_ln1_g, v_ln1_b, v_cq_w, v_ckv_w, v_co_w, v_ln2_g, v_ln2_b, v_ffn_wi, v_ffn_wo, v_ln3_g, v_ln3_b):
    args = dict(locals())
    w = {n: args[n] for n in WEIGHTS}
    m = {n: args["m_" + n] for n in WEIGHTS}
    v = {n: args["v_" + n] for n in WEIGHTS}
    for group in (w, m, v):
        group["w_in"] = jnp.swapaxes(group["w_in"], 1, 2)
    cx, cy, cc = _place()
    chip = 2 * cx + cy
    L = w_in.shape[0]

    place = jnp.stack([cc, chip]).astype(jnp.int32)
    cw_rows = _pad_rows(conv_w.reshape(-1))
    cw_all = gather_devices(cw_rows, "gather_conv_w")[0::2]
    cw_parts = cw_all.reshape(N_CHIPS, -1)[:, :conv_w.size].reshape((N_CHIPS,) + conv_w.shape)
    conv_full = jnp.concatenate([cw_parts[k] for k in range(N_CHIPS)], axis=2)

    shards = [{n: w[n][l].astype(MXU_DTYPE) for n in BIG} for l in range(L)]
    late_names = tuple(n for n in BIG if n not in GATHER_FIRST)
    gathering = {(0, GATHER_FIRST): gather_start([shards[0][n] for n in GATHER_FIRST], cw_rows, "gather_start_0a")}
    gathering[0, late_names] = gather_start([shards[0][n] for n in late_names], gathering[0, GATHER_FIRST][4], "gather_start_0b")

    def gathered(l, names, after, tag):
        lands = gather_pass(gather_wait(gathering.pop((l, names)), after, f"gather_wait_{tag}"), f"gather_pass_{tag}")
        wl = {}
        for n, gw in zip(names, lands):
            rows_joined = gw.reshape(gw.shape[:-3] + (-1, gw.shape[-1]))
            if n in COL_BLOCKED:
                wl[n] = (gw, dict(b_blocks=N_CHIPS))
            elif n in GATE_WEIGHTS:
                wl[n] = rows_joined
            else:
                wl[n] = (rows_joined, {})
        return wl, lands

    def start_layer(l, after):
        if l >= L:
            return ()
        gathering[l, BIG] = gather_start([shards[l][n] for n in BIG], after, f"gather_start_{l}")
        return (gathering[l, BIG][4],)

    def weights_of(l, h):
        deps, late = (), None
        if l == 0:
            wl, _ = gathered(0, GATHER_FIRST, h, "0a")

            def late(after):
                rest, lands = gathered(0, late_names, after, "0b")
                return rest, start_layer(1, lands[0])
        else:
            wl, lands = gathered(l, BIG, h, str(l))
            deps = start_layer(l + 1, lands[0])
        for n in SMALL:
            wl[n] = conv_full[l] if n == "conv_w" else w[n][l] if n == "sinks" else w[n][l][None, :]
        return wl, deps, late

    def for_chips(n, g):
        if n in COL_BLOCKED:
            return g
        if n in GATE_WEIGHTS:
            nb, bw, _ = g.shape
            g = g.reshape(nb, N_CHIPS, bw // N_CHIPS, bw).transpose(1, 0, 2, 3).reshape(N_CHIPS, nb * bw // N_CHIPS, bw)
        else:
            g = g.reshape(N_CHIPS, g.shape[0] // N_CHIPS, g.shape[1])
        return g.astype(MXU_DTYPE)

    reduced, scattering, small_grads = {}, {}, [None] * L
    late_grads = tuple(n for n in BIG if n not in SCATTER_FIRST)

    def start_scatter(l, names, g, after, tag):
        partial_sums = [for_chips(n, g[n])[None] for n in names]
        from_sibling = swap_sibling(partial_sums, f"grad_to_sibling_{tag}")
        chip_sums = [add_pair(place, a, b, f"grad_add_pair_{n}_{l}") for n, a, b in zip(names, partial_sums, from_sibling)]
        scattering[l, names] = scatter_start(chip_sums, after, f"grad_scatter_start_{tag}")
        return (scattering[l, names][4],)

    def finish_layer(l, after):
        for names in [k[1] for k in list(scattering) if k[0] == l]:
            tag = str(l) if names == BIG else f"{l}{'a' if names == SCATTER_FIRST else 'b'}"
            chip_sums, from_chips = scatter_wait(scattering.pop((l, names)), after, f"grad_scatter_wait_{tag}")
            for n, own, others in zip(names, chip_sums, from_chips):
                target = reduced.get(n, (L, 2 * own.shape[2], own.shape[3]))
                reduced[n] = add_chips(place, own, others, l, target, f"grad_add_chips_{n}_{l}")
        reduced.update(zip(BIG, join_halves([reduced[n] for n in BIG], l, f"grad_join_{l}")))

    def grads_halfway(l):
        def halfway(g, after):
            return start_scatter(l, SCATTER_FIRST, g, after, f"{l}a")

        return halfway

    def grads_done(l, g, dh):
        small_grads[l] = {n: g[n] for n in SMALL}
        deps = start_scatter(l, late_grads, g, dh, f"{l}b")
        if 1 < l + 1 < L:
            finish_layer(l + 1, dh)
        return deps

    loss11, dx = local_step(x[0], mem[0], loss_target[0], L, weights_of, grads_halfway, grads_done)
    loss = lax.psum(loss11[0, 0], ("x", "y", "c"))

    first = min(2, L)
    updated = {}
    if first < L:
        for n in BIG:
            updated[n] = adamw(w[n], reduced[n].reshape(w[n].shape), m[n], v[n], f"adamw_{n}_upper", layers=(first, L))
    behind = (jnp.stack([updated[n][0][(0,) * w[n].ndim] for n in updated]),) if updated else ()
    small_full = {n: jnp.stack([gl[n] for gl in small_grads]).reshape(w[n].shape[:1] + ((CONV_WIDTH, -1) if n == "conv_w" else (-1,)))
                  for n in SMALL}
    small_flat = jnp.concatenate([small_full[n].reshape(-1) for n in SMALL])
    small_sum = sum_devices(gather_devices(_pad_rows(small_flat), "gather_small_grads", behind), "sum_small_grads").reshape(-1)
    delta, new_m, new_v, grad = {}, {}, {}, {}
    off = 0
    for n in SMALL:
        gfull = small_sum[off:off + small_full[n].size].reshape(small_full[n].shape)
        off += small_full[n].size
        if n == "conv_w":
            width = conv_w.shape[2]
            gfull = lax.dynamic_slice_in_dim(gfull, chip * width, width, axis=2)
        delta[n], new_m[n], new_v[n], grad[n] = adamw(w[n], gfull, m[n], v[n], "adamw_" + n)
    after = jnp.stack([delta[n][(0,) * delta[n].ndim] for n in SMALL])
    for l in reversed(range(first)):
        finish_layer(l, after)

    for n in BIG:
        some = dict(layers=(0, first), into=updated[n]) if updated else {}
        delta[n], new_m[n], new_v[n], grad[n] = adamw(w[n], reduced[n].reshape(w[n].shape), m[n], v[n], "adamw_" + n, **some)
    for group in (delta, new_m, new_v, grad):
        group["w_in"] = jnp.swapaxes(group["w_in"], 1, 2)
    return (loss, dx[None], *[grad[n] for n in WEIGHTS], *[delta[n] for n in WEIGHTS], *[new_m[n] for n in WEIGHTS],
            *[new_v[n] for n in WEIGHTS])
```

```python
import functools
import math

import jax
import jax.numpy as jnp
import numpy as np
from jax import lax
from jax.experimental import pallas as pl
from jax.experimental.pallas import tpu as pltpu

F32 = jnp.float32
BF16 = jnp.bfloat16
MXU_DTYPE = BF16

HEAD_DIM = 64
N_KV_HEADS = 2
WINDOW = 128
ROT_DIM = HEAD_DIM // 4
ROPE_THETA = 500000.0
CROSS_HEADS = 4
CONV_WIDTH = 4
LRU_C = 8.0
LN_EPS = 1e-5
NEG_INF = -1e30
ADAM_LR = 0.001
ADAM_B1 = 0.9
ADAM_B2 = 0.999
ADAM_EPS = 1e-08
ADAM_WD = 0.01
ADAM_STEP = 10

VMEM_BYTES_V7X = 64 * 1024 * 1024
VMEM_BLOCK_BUDGET = 36 * 1024 * 1024
LANES = 128
SUBLANES = 8

MESH_ID = pl.DeviceIdType.MESH
N_CHIPS = 4
N_DEV = 8

BIG = ("w_in", "w_rg", "w_ig", "w_br_rnn", "w_br_attn", "w_out", "cq_w", "ckv_w", "co_w", "ffn_wi", "ffn_wo")
SHARD_AXIS = {"w_in": 0, "w_rg": 1, "w_ig": 1, "w_br_rnn": 0, "w_br_attn": 0, "w_out": 0, "cq_w": 0, "ckv_w": 1,
              "co_w": 0, "ffn_wi": 1, "ffn_wo": 0}
SMALL = ("conv_w", "conv_b", "b_rg", "b_ig", "lru_lambda", "sinks", "ln1_g", "ln1_b", "ln2_g", "ln2_b", "ln3_g", "ln3_b")
WEIGHTS = ("w_in", "conv_w", "conv_b", "w_rg", "b_rg", "w_ig", "b_ig", "lru_lambda", "w_br_rnn", "w_br_attn", "sinks",
           "w_out", "ln1_g", "ln1_b", "cq_w", "ckv_w", "co_w", "ln2_g", "ln2_b", "ffn_wi", "ffn_wo", "ln3_g", "ln3_b")
GATE_WEIGHTS = ("w_rg", "w_ig")
COL_BLOCKED = ("ckv_w", "ffn_wi")
GATHER_FIRST = ("w_in", "w_rg", "w_ig")
SCATTER_FIRST = ("ffn_wo", "ffn_wi", "co_w", "cq_w", "ckv_w")


def _params(dims=None, vmem=None):
    return pltpu.CompilerParams(dimension_semantics=dims, vmem_limit_bytes=vmem)


def _vmem_limit(block_bytes, temp_bytes=0):
    want = int(2 * block_bytes + temp_bytes) + (6 << 20)
    return max(32 << 20, min(want, VMEM_BYTES_V7X - (6 << 20)))


def _divisors(n, align, cap):
    out = [d for d in range(align, min(n, cap) + 1, align) if n % d == 0]
    if n <= cap and n not in out:
        out.append(n)
    return sorted(out, reverse=True) or [n]


PIN_MIN_ELEMENTS = 1 << 18


def hbm_call(body, **kw):
    def in_hbm(s):
        return pltpu.HBM(s.shape, s.dtype) if math.prod(s.shape) >= PIN_MIN_ELEMENTS else s

    shapes = kw.pop("out_shape")
    shapes = [in_hbm(s) for s in shapes] if isinstance(shapes, (list, tuple)) else in_hbm(shapes)
    call = pl.pallas_call(body, out_shape=shapes, **kw)

    def run(*args):
        return call(*[pltpu.with_memory_space_constraint(a, pltpu.HBM) if a.size >= PIN_MIN_ELEMENTS else a for a in args])

    return run


def _sigmoid(x):
    return 1.0 / (1.0 + jnp.exp(-x))


def _gelu_parts(x):
    c = math.sqrt(2.0 / math.pi)
    u = c * (x + 0.044715 * x * x * x)
    t = jnp.tanh(u)
    return t, c * (1.0 + 3 * 0.044715 * x * x)


def _gelu(x):
    t, _ = _gelu_parts(x)
    return 0.5 * x * (1.0 + t)


def _gelu_grad(x):
    t, du = _gelu_parts(x)
    return 0.5 * (1.0 + t) + 0.5 * x * (1.0 - t * t) * du


def _neg_expm1(x):
    series = x * (1.0 + x * (0.5 + x * (1.0 / 6 + x * (1.0 / 24 + x * (1.0 / 120)))))
    return -jnp.where(x > -0.1, series, jnp.exp(x) - 1.0)


def _softplus_neg(lam):
    x = -lam
    return jnp.maximum(x, 0.0) + jnp.log1p(jnp.exp(-jnp.abs(x)))


STEP_US = 0.35
HBM_BYTES_PER_US = 2.5e6
MXU_FLOPS_PER_US = 7e8


def _layer_norm(z, g, b):
    mu = jnp.mean(z, axis=-1, keepdims=True)
    zc = z - mu
    rs = lax.rsqrt(jnp.mean(zc * zc, axis=-1, keepdims=True) + LN_EPS)
    xh = zc * rs
    return xh * g + b, xh, rs


def mm(a, b, mode, name, *, b_index=(), a_blocks=0, b_blocks=0, out_blocks=0, out_dtype=F32, deps=(), post_norm=None, plus=None):
    nlead = len(b_index) + (1 if b_blocks else 0)
    bk, bn = b.shape[nlead:]
    M, K = (a.shape[-1], a.shape[-2]) if mode == "tn" else (a.shape[-2], a.shape[-1] * max(a_blocks, 1))
    N = bk if mode == "nt" else bn * max(b_blocks, 1) if mode == "nn" or mode == "tn" else bn
    asz, bsz, osz = a.dtype.itemsize, b.dtype.itemsize, jnp.dtype(out_dtype).itemsize
    n_unit = math.gcd(N // max(out_blocks, 1), N // max(b_blocks, 1) if mode != "nt" else N)
    k_unit = math.gcd(K // max(a_blocks, 1), K // max(b_blocks, 1) if mode == "nt" else K)
    tms = _divisors(M, LANES if mode == "tn" else SUBLANES, 2048)
    tns = [N] if post_norm else _divisors(n_unit, LANES, 2048)
    tks = _divisors(k_unit, LANES, k_unit)
    best = None
    for tm in tms:
        for tn in tns:
            for tk in tks:
                nk = K // tk
                scratch = tm * tn * 4 if (nk > 1 and osz != 4) else 0
                blocks = tm * tk * asz + tn * tk * bsz + tm * tn * osz * (3 if post_norm else 1)
                temps = tm * tk * (2 + (4 if mode == "tn" else 0)) + tn * tk * 2 + tm * tn * 4 + scratch
                if 2 * blocks + temps > VMEM_BLOCK_BUDGET + (8 << 20):
                    continue
                ni, nj = M // tm, N // tn
                traffic = M * K * asz * (nj if nk > 1 else 1) + N * K * bsz * (1 if nj * nk == 1 else ni) + M * N * osz
                busy = max(traffic / HBM_BYTES_PER_US, 2.0 * M * N * K / MXU_FLOPS_PER_US)
                cost = ni * nj * nk * STEP_US + busy + blocks / HBM_BYTES_PER_US
                if best is None or cost < best[0]:
                    best = (cost, tm, tn, tk, blocks, temps)
    _, tm, tn, tk, blocks, temps = best
    nk = K // tk
    use_scratch = nk > 1 and osz != 4

    def split(index, total, blocks, tile):
        per = total // blocks // tile
        return index // per, index % per

    def body(a_ref, b_ref, *rest):
        rest = rest[len(deps):]
        if post_norm:
            h_ref, g_ref, beta_ref, o_ref, xh_ref, rs_ref = rest[:6]
            acc = rest[6:]
        elif plus:
            plus_ref, o_ref, acc = rest[0], rest[1], rest[2:]
        else:
            o_ref, acc = rest[0], rest[1:]
        av = a_ref[...].astype(MXU_DTYPE)
        bv = b_ref[...].astype(MXU_DTYPE)
        dn = {"nn": (((1,), (0,)), ((), ())), "nt": (((1,), (1,)), ((), ())), "tn": (((0,), (0,)), ((), ()))}[mode]
        r = lax.dot_general(av, bv, dn, preferred_element_type=F32)

        def normalise(f):
            o_ref[...], xh_ref[...], rs_ref[...] = _layer_norm(post_norm[3] * h_ref[...] + f, g_ref[...], beta_ref[...])

        if nk == 1 and post_norm:
            normalise(r)
        elif nk == 1 and plus:
            o_ref[...] = plus[1] * plus_ref[...] + r
        elif nk == 1:
            o_ref[...] = r.astype(o_ref.dtype)
        else:
            acc_ref = acc[0] if use_scratch else o_ref

            @pl.when(pl.program_id(2) == 0)
            def _():
                acc_ref[...] = r

            @pl.when(pl.program_id(2) > 0)
            def _():
                acc_ref[...] += r

            if use_scratch:
                @pl.when(pl.program_id(2) == nk - 1)
                def _():
                    o_ref[...] = acc_ref[...].astype(o_ref.dtype)
            elif post_norm:
                @pl.when(pl.program_id(2) == nk - 1)
                def _():
                    normalise(o_ref[...])
            elif plus:
                @pl.when(pl.program_id(2) == nk - 1)
                def _():
                    o_ref[...] = plus[1] * plus_ref[...] + o_ref[...]

    if mode == "tn":
        a_spec = pl.BlockSpec((tk, tm), lambda i, j, k: (k, i))
    elif a_blocks:
        a_spec = pl.BlockSpec((None, tm, tk), lambda i, j, k: (split(k, K, a_blocks, tk)[0], i, split(k, K, a_blocks, tk)[1]))
    else:
        a_spec = pl.BlockSpec((tm, tk), lambda i, j, k: (i, k))
    lead = (None,) * nlead
    if mode == "nt":
        bmap = ((lambda i, j, k: b_index + (split(k, K, b_blocks, tk)[0], j, split(k, K, b_blocks, tk)[1])) if b_blocks
                else (lambda i, j, k: b_index + (j, k)))
        b_spec = pl.BlockSpec(lead + (tn, tk), bmap)
    else:
        bmap = ((lambda i, j, k: b_index + (split(j, N, b_blocks, tn)[0], k, split(j, N, b_blocks, tn)[1])) if b_blocks
                else (lambda i, j, k: b_index + (k, j)))
        b_spec = pl.BlockSpec(lead + (tk, tn), bmap)
    if out_blocks:
        o_spec = pl.BlockSpec((None, tm, tn), lambda i, j, k: (split(j, N, out_blocks, tn)[0], i, split(j, N, out_blocks, tn)[1]))
        o_shape = jax.ShapeDtypeStruct((out_blocks, M, N // out_blocks), out_dtype)
    else:
        o_spec = pl.BlockSpec((tm, tn), lambda i, j, k: (i, j))
        o_shape = jax.ShapeDtypeStruct((M, N), out_dtype)
    in_specs, extra = [a_spec, b_spec] + [pl.BlockSpec(memory_space=pl.ANY)] * len(deps), ()
    if post_norm:
        vec = pl.BlockSpec((1, N), lambda i, j, k: (0, 0))
        in_specs += [pl.BlockSpec((tm, N), lambda i, j, k: (i, 0)), vec, vec]
        o_spec = [o_spec, pl.BlockSpec((tm, N), lambda i, j, k: (i, 0)), pl.BlockSpec((tm, 1), lambda i, j, k: (i, 0))]
        o_shape = [o_shape, jax.ShapeDtypeStruct((M, N), F32), jax.ShapeDtypeStruct((M, 1), F32)]
        extra = post_norm[:3]
    elif plus:
        in_specs += [pl.BlockSpec((tm, tn), lambda i, j, k: (i, j))]
        extra = plus[:1]
    return hbm_call(
        body, name=name, grid=(M // tm, N // tn, nk), in_specs=in_specs, out_specs=o_spec, out_shape=o_shape,
        scratch_shapes=[pltpu.VMEM((tm, tn), F32)] if use_scratch else [],
        compiler_params=_params(("parallel", "parallel", "arbitrary"), _vmem_limit(blocks, temps)),
    )(a, b, *deps, *extra)


ROW_TILE = 512
GATE_ROWS = 1024


def ln_bwd(dy_a, dy_b, xh, rs, g, c1, name):
    S, D = xh.shape
    tr = min(ROW_TILE, S)
    two = dy_b is not None

    def body(*refs):
        if two:
            a_ref, b_ref, xh_ref, rs_ref, g_ref, dz_ref, dg_ref, db_ref = refs
            dy = c1 * a_ref[...] + b_ref[...]
        else:
            a_ref, xh_ref, rs_ref, g_ref, dz_ref, dg_ref, db_ref = refs
            dy = a_ref[...]
        x = xh_ref[...]
        dyg = dy * g_ref[...]
        m1 = jnp.mean(dyg, axis=-1, keepdims=True)
        m2 = jnp.mean(dyg * x, axis=-1, keepdims=True)
        dz_ref[...] = rs_ref[...] * (dyg - m1 - x * m2)

        @pl.when(pl.program_id(0) == 0)
        def _():
            dg_ref[...] = jnp.zeros_like(dg_ref)
            db_ref[...] = jnp.zeros_like(db_ref)

        dg_ref[...] += jnp.sum(dy * x, axis=0, keepdims=True)
        db_ref[...] += jnp.sum(dy, axis=0, keepdims=True)

    row = pl.BlockSpec((tr, D), lambda i: (i, 0))
    vec = pl.BlockSpec((1, D), lambda i: (0, 0))
    ins = [row, row] if two else [row]
    args = (dy_a, dy_b) if two else (dy_a,)
    return hbm_call(
        body, name=name, grid=(S // tr,), in_specs=ins + [row, pl.BlockSpec((tr, 1), lambda i: (i, 0)), vec],
        out_specs=[row, vec, vec],
        out_shape=[jax.ShapeDtypeStruct((S, D), F32), jax.ShapeDtypeStruct((1, D), F32), jax.ShapeDtypeStruct((1, D), F32)],
        compiler_params=_params(("arbitrary",), 48 << 20),
    )(*args, xh, rs, g)


def loss_head(y, t, name):
    S, D = y.shape
    tr = min(ROW_TILE, S)
    nsteps = S // tr

    def body(y_ref, t_ref, dy_ref, l_ref, acc_ref):
        i = pl.program_id(0)

        @pl.when(i == 0)
        def _():
            acc_ref[...] = jnp.zeros_like(acc_ref)

        e = y_ref[...] - t_ref[...]
        dy_ref[...] = e * (1.0 / D)
        acc_ref[...] += jnp.sum(e * e, axis=0, keepdims=True)

        @pl.when(i == nsteps - 1)
        def _():
            l_ref[...] = jnp.sum(acc_ref[...], axis=1, keepdims=True) * (0.5 / D)

    row = pl.BlockSpec((tr, D), lambda i: (i, 0))
    return hbm_call(
        body, name=name, grid=(nsteps,), in_specs=[row, row],
        out_specs=[row, pl.BlockSpec((1, 1), lambda i: (0, 0))],
        out_shape=[jax.ShapeDtypeStruct((S, D), F32), jax.ShapeDtypeStruct((1, 1), F32)],
        scratch_shapes=[pltpu.VMEM((1, D), F32)], compiler_params=_params(("arbitrary",)),
    )(y, t)


SWIGLU_ROWS = 256


def swiglu_fwd(gu, name):
    _, S, Fh = gu.shape
    tc = _divisors(Fh, LANES, 1536)[0]
    tr = min(SWIGLU_ROWS, S)

    def body(gu_ref, o_ref):
        g = gu_ref[0]
        o_ref[...] = (g * _sigmoid(g) * gu_ref[1]).astype(o_ref.dtype)

    return hbm_call(
        body, name=name, grid=(S // tr, Fh // tc), in_specs=[pl.BlockSpec((2, tr, tc), lambda i, j: (0, i, j))],
        out_specs=pl.BlockSpec((tr, tc), lambda i, j: (i, j)), out_shape=jax.ShapeDtypeStruct((S, Fh), MXU_DTYPE),
        compiler_params=_params(("parallel", "parallel")),
    )(gu)


def swiglu_bwd(gu, dact, name):
    _, S, Fh = gu.shape
    tc = _divisors(Fh, LANES, 1536)[0]
    tr = min(SWIGLU_ROWS, S)

    def body(gu_ref, d_ref, o_ref):
        g, u, d = gu_ref[0], gu_ref[1], d_ref[...]
        s = _sigmoid(g)
        o_ref[0] = (d * u * (s * (1.0 + g * (1.0 - s)))).astype(o_ref.dtype)
        o_ref[1] = (d * (g * s)).astype(o_ref.dtype)

    both = pl.BlockSpec((2, tr, tc), lambda i, j: (0, i, j))
    return hbm_call(
        body, name=name, grid=(S // tr, Fh // tc), in_specs=[both, pl.BlockSpec((tr, tc), lambda i, j: (i, j))],
        out_specs=both, out_shape=jax.ShapeDtypeStruct((2, S, Fh), MXU_DTYPE), compiler_params=_params(("parallel", "parallel")),
    )(gu, dact)


GATE_COLS = 256


def merge_fwd(proj, pr, pa, name):
    S, D = pr.shape
    tr = min(GATE_ROWS, S)
    c0 = (3 * D + 2 * N_KV_HEADS * HEAD_DIM) // GATE_COLS
    c1 = c0 + D // GATE_COLS

    def body(gr_ref, ga_ref, pr_ref, pa_ref, o_ref):
        o_ref[...] = (_sigmoid(gr_ref[...]) * pr_ref[...] + _sigmoid(ga_ref[...]) * pa_ref[...]).astype(o_ref.dtype)

    blk = pl.BlockSpec((tr, GATE_COLS), lambda i, j: (i, j))
    return hbm_call(
        body, name=name, grid=(S // tr, D // GATE_COLS),
        in_specs=[pl.BlockSpec((tr, GATE_COLS), lambda i, j: (i, c0 + j)), pl.BlockSpec((tr, GATE_COLS), lambda i, j: (i, c1 + j)),
                  blk, blk],
        out_specs=blk, out_shape=jax.ShapeDtypeStruct((S, D), MXU_DTYPE), compiler_params=_params(("parallel", "parallel")),
    )(proj, proj, pr, pa)


def merge_bwd(proj, pr, pa, dm, name):
    S, D = pr.shape
    tr = min(GATE_ROWS, S)
    c0 = (3 * D + 2 * N_KV_HEADS * HEAD_DIM) // GATE_COLS
    c1 = c0 + D // GATE_COLS

    def body(gr_ref, ga_ref, pr_ref, pa_ref, dm_ref, dpr_ref, dpa_ref, dgr_ref, dga_ref):
        sr, sa, d = _sigmoid(gr_ref[...]), _sigmoid(ga_ref[...]), dm_ref[...]
        dpr_ref[...] = (d * sr).astype(dpr_ref.dtype)
        dpa_ref[...] = (d * sa).astype(dpa_ref.dtype)
        dgr_ref[...] = (d * pr_ref[...] * (sr * (1.0 - sr))).astype(dgr_ref.dtype)
        dga_ref[...] = (d * pa_ref[...] * (sa * (1.0 - sa))).astype(dga_ref.dtype)

    blk = pl.BlockSpec((tr, GATE_COLS), lambda i, j: (i, j))
    sds = jax.ShapeDtypeStruct((S, D), MXU_DTYPE)
    return hbm_call(
        body, name=name, grid=(S // tr, D // GATE_COLS),
        in_specs=[pl.BlockSpec((tr, GATE_COLS), lambda i, j: (i, c0 + j)), pl.BlockSpec((tr, GATE_COLS), lambda i, j: (i, c1 + j)),
                  blk, blk, blk],
        out_specs=[blk, blk, blk, blk], out_shape=[sds, sds, sds, sds], compiler_params=_params(("parallel", "parallel")),
    )(proj, proj, pr, pa, dm)


RG_ROWS = 512


def _shift_down(cur, prev, d, row, first):
    halo = jnp.where(first, 0.0, pltpu.roll(prev, d, 0))
    return jnp.where(row >= d, pltpu.roll(cur, d, 0), halo)


def _shift_up(cur, nxt, d, row, last, tr):
    halo = jnp.where(last, 0.0, pltpu.roll(nxt, tr - d, 0))
    return jnp.where(row < tr - d, pltpu.roll(cur, tr - d, 0), halo)


def _lru_coeffs(r, lam):
    sp = _softplus_neg(lam)
    la = -LRU_C * r * sp
    return sp, la, jnp.exp(la), _neg_expm1(2.0 * la)


def rg_gates_fwd(proj, conv_w, conv_b, w_rg, b_rg, w_ig, b_ig, lam, name):
    S = proj.shape[0]
    nblk, bw, _ = w_rg.shape
    D = nblk * bw
    tr = min(RG_ROWS, S)

    def body(xr_ref, xp_ref, cw_ref, cb_ref, wr_ref, br_ref, wi_ref, bi_ref, lam_ref, xc_ref, r_ref, i_ref, a_ref, b_ref):
        first = pl.program_id(1) == 0
        cur, prev = xr_ref[...], xp_ref[...]
        row = lax.broadcasted_iota(jnp.int32, cur.shape, 0)
        xc = cb_ref[...]
        for k in range(CONV_WIDTH - 1):
            xc = xc + _shift_down(cur, prev, CONV_WIDTH - 1 - k, row, first) * cw_ref[k:k + 1, :]
        xc = xc + cur * cw_ref[CONV_WIDTH - 1:CONV_WIDTH, :]
        xm = xc.astype(MXU_DTYPE)
        r = _sigmoid(jnp.dot(xm, wr_ref[...].astype(MXU_DTYPE), preferred_element_type=F32) + br_ref[...])
        ig = _sigmoid(jnp.dot(xm, wi_ref[...].astype(MXU_DTYPE), preferred_element_type=F32) + bi_ref[...])
        _, _, a, em = _lru_coeffs(r, lam_ref[...])
        xc_ref[...] = xc
        r_ref[...] = r
        i_ref[...] = ig
        a_ref[...] = a
        b_ref[...] = jnp.sqrt(em) * (ig * xc)

    tile = pl.BlockSpec((tr, bw), lambda n, i: (i, n))
    vec = pl.BlockSpec((1, bw), lambda n, i: (0, n))
    wblk = pl.BlockSpec((None, bw, bw), lambda n, i: (n, 0, 0))
    sds = jax.ShapeDtypeStruct((S, D), F32)
    return hbm_call(
        body, name=name, grid=(nblk, S // tr),
        in_specs=[tile, pl.BlockSpec((tr, bw), lambda n, i: (jnp.maximum(i - 1, 0), n)),
                  pl.BlockSpec((CONV_WIDTH, bw), lambda n, i: (0, n)), vec, wblk, vec, wblk, vec, vec],
        out_specs=[tile] * 5, out_shape=[sds] * 5, compiler_params=_params(("parallel", "parallel")),
    )(proj, proj, conv_w, conv_b, w_rg, b_rg, w_ig, b_ig, lam)


SCAN_COLS = 256
CHUNK = SUBLANES
SCAN_UNROLL = 4


def rg_scan_fwd(proj, a, b, name):
    S, D = a.shape
    cb = min(SCAN_COLS, D)
    goff = D // cb

    def body(a_ref, b_ref, g_ref, hs_ref, y_ref):
        row = lax.broadcasted_iota(jnp.int32, (CHUNK, cb), 0)

        def step(c, carry):
            r0 = pl.multiple_of(c * CHUNK, CHUNK)
            A = a_ref[pl.ds(r0, CHUNK), :]
            B = b_ref[pl.ds(r0, CHUNK), :]
            for d in (1, 2, 4):
                As = jnp.where(row >= d, pltpu.roll(A, d, 0), 1.0)
                Bs = jnp.where(row >= d, pltpu.roll(B, d, 0), 0.0)
                B = A * Bs + B
                A = A * As
            hs_ref[pl.ds(r0, CHUNK), :] = B + A * carry
            a_end = jnp.sum(jnp.where(row == CHUNK - 1, A, 0.0), axis=0, keepdims=True)
            b_end = jnp.sum(jnp.where(row == CHUNK - 1, B, 0.0), axis=0, keepdims=True)
            return b_end + a_end * carry

        lax.fori_loop(0, S // CHUNK, step, jnp.zeros((1, cb), F32), unroll=SCAN_UNROLL)
        y_ref[...] = (hs_ref[...] * _gelu(g_ref[...])).astype(y_ref.dtype)

    col = pl.BlockSpec((S, cb), lambda j: (0, j))
    return hbm_call(
        body, name=name, grid=(D // cb,), in_specs=[col, col, pl.BlockSpec((S, cb), lambda j: (0, goff + j))],
        out_specs=[col, col], out_shape=[jax.ShapeDtypeStruct((S, D), F32), jax.ShapeDtypeStruct((S, D), MXU_DTYPE)],
        compiler_params=_params(("parallel",), _vmem_limit(5 * S * cb * 4, 4 * S * cb * 4)),
    )(a, b, proj)


def rg_scan_bwd(proj, dy, hs, a, name):
    S, D = a.shape
    cb = min(SCAN_COLS, D)
    goff = D // cb
    nchunks = S // CHUNK

    def body(g_ref, dy_ref, hs_ref, a_ref, dg_ref, gt_ref):
        gate, dy = g_ref[...], dy_ref[...]
        dg_ref[...] = (dy * hs_ref[...] * _gelu_grad(gate)).astype(dg_ref.dtype)
        gt_ref[...] = dy * _gelu(gate)
        row = lax.broadcasted_iota(jnp.int32, (CHUNK, cb), 0)

        def step(k, carry):
            c = nchunks - 1 - k
            r0 = pl.multiple_of(c * CHUNK, CHUNK)
            rn = pl.multiple_of(jnp.minimum(c + 1, nchunks - 1) * CHUNK, CHUNK)
            last = c == nchunks - 1
            nxt = jnp.where(last, 0.0, pltpu.roll(a_ref[pl.ds(rn, CHUNK), :], CHUNK - 1, 0))
            A = jnp.where(row < CHUNK - 1, pltpu.roll(a_ref[pl.ds(r0, CHUNK), :], CHUNK - 1, 0), nxt)
            B = gt_ref[pl.ds(r0, CHUNK), :]
            for d in (1, 2, 4):
                As = jnp.where(row < CHUNK - d, pltpu.roll(A, CHUNK - d, 0), 1.0)
                Bs = jnp.where(row < CHUNK - d, pltpu.roll(B, CHUNK - d, 0), 0.0)
                B = A * Bs + B
                A = A * As
            gt_ref[pl.ds(r0, CHUNK), :] = B + A * carry
            a_end = jnp.sum(jnp.where(row == 0, A, 0.0), axis=0, keepdims=True)
            b_end = jnp.sum(jnp.where(row == 0, B, 0.0), axis=0, keepdims=True)
            return b_end + a_end * carry

        lax.fori_loop(0, nchunks, step, jnp.zeros((1, cb), F32), unroll=SCAN_UNROLL)

    col = pl.BlockSpec((S, cb), lambda j: (0, j))
    return hbm_call(
        body, name=name, grid=(D // cb,), in_specs=[pl.BlockSpec((S, cb), lambda j: (0, goff + j)), col, col, col],
        out_specs=[col, col], out_shape=[jax.ShapeDtypeStruct((S, D), MXU_DTYPE), jax.ShapeDtypeStruct((S, D), F32)],
        compiler_params=_params(("parallel",), _vmem_limit(6 * S * cb * 4, 6 * S * cb * 4)),
    )(proj, dy, hs, a)


def rg_gates_bwd(gt, hs, xc, r, ig, w_rg, w_ig, lam, name):
    S, D = xc.shape
    nblk, bw, _ = w_rg.shape
    tr = min(RG_ROWS, S)

    def body(gt_ref, hs_ref, hp_ref, xc_ref, r_ref, i_ref, wr_ref, wi_ref, lam_ref,
             dxc_ref, dwr_ref, dwi_ref, dbr_ref, dbi_ref, dl_ref):
        step = pl.program_id(1)
        g, hs, xc, r, ig, lam = gt_ref[...], hs_ref[...], xc_ref[...], r_ref[...], i_ref[...], lam_ref[...]
        row = lax.broadcasted_iota(jnp.int32, g.shape, 0)
        hprev = _shift_down(hs, hp_ref[...], 1, row, step == 0)
        sp, _, a, em = _lru_coeffs(r, lam)
        mult = jnp.sqrt(em)
        du = g * mult
        dla = g * hprev * a - (g * (ig * xc)) * (a * a) / mult
        dpr = (dla * (-LRU_C * sp)) * (r * (1.0 - r))
        dpi = (du * xc) * (ig * (1.0 - ig))
        dprm, dpim = dpr.astype(MXU_DTYPE), dpi.astype(MXU_DTYPE)
        nt = (((1,), (1,)), ((), ()))
        dxc_ref[...] = (du * ig + lax.dot_general(dprm, wr_ref[...].astype(MXU_DTYPE), nt, preferred_element_type=F32)
                        + lax.dot_general(dpim, wi_ref[...].astype(MXU_DTYPE), nt, preferred_element_type=F32))

        @pl.when(step == 0)
        def _():
            for ref in (dwr_ref, dwi_ref, dbr_ref, dbi_ref, dl_ref):
                ref[...] = jnp.zeros_like(ref)

        xct = xc.T.astype(MXU_DTYPE)
        dwr_ref[...] += jnp.dot(xct, dprm, preferred_element_type=F32)
        dwi_ref[...] += jnp.dot(xct, dpim, preferred_element_type=F32)
        dbr_ref[...] += jnp.sum(dpr, axis=0, keepdims=True)
        dbi_ref[...] += jnp.sum(dpi, axis=0, keepdims=True)
        dl_ref[...] += jnp.sum(dla * (-LRU_C * r), axis=0, keepdims=True) * (-_sigmoid(-lam))

    tile = pl.BlockSpec((tr, bw), lambda n, i: (i, n))
    vec = pl.BlockSpec((1, bw), lambda n, i: (0, n))
    wblk = pl.BlockSpec((None, bw, bw), lambda n, i: (n, 0, 0))
    return hbm_call(
        body, name=name, grid=(nblk, S // tr),
        in_specs=[tile, tile, pl.BlockSpec((tr, bw), lambda n, i: (jnp.maximum(i - 1, 0), n)), tile, tile, tile, wblk, wblk, vec],
        out_specs=[tile, wblk, wblk, vec, vec, vec],
        out_shape=[jax.ShapeDtypeStruct((S, D), F32), jax.ShapeDtypeStruct((nblk, bw, bw), F32), jax.ShapeDtypeStruct((nblk, bw, bw), F32),
                   jax.ShapeDtypeStruct((1, D), F32), jax.ShapeDtypeStruct((1, D), F32), jax.ShapeDtypeStruct((1, D), F32)],
        compiler_params=_params(("parallel", "arbitrary")),
    )(gt, hs, hs, xc, r, ig, w_rg, w_ig, lam)


def rg_conv_bwd(proj, dxc, conv_w, name):
    S, D = dxc.shape
    bw = min(SCAN_COLS, D)
    tr = min(RG_ROWS, S)
    nsteps = S // tr

    def body(d_ref, dn_ref, xr_ref, xp_ref, cw_ref, dxr_ref, dcw_ref, dcb_ref):
        step = pl.program_id(1)
        d, xr = d_ref[...], xr_ref[...]
        row = lax.broadcasted_iota(jnp.int32, d.shape, 0)
        dxr = d * cw_ref[CONV_WIDTH - 1:CONV_WIDTH, :]
        for k in range(CONV_WIDTH - 1):
            dxr = dxr + _shift_up(d, dn_ref[...], CONV_WIDTH - 1 - k, row, step == nsteps - 1, tr) * cw_ref[k:k + 1, :]
        dxr_ref[...] = dxr.astype(dxr_ref.dtype)

        @pl.when(step == 0)
        def _():
            dcw_ref[...] = jnp.zeros_like(dcw_ref)
            dcb_ref[...] = jnp.zeros_like(dcb_ref)

        for k in range(CONV_WIDTH - 1):
            xs = _shift_down(xr, xp_ref[...], CONV_WIDTH - 1 - k, row, step == 0)
            dcw_ref[k:k + 1, :] += jnp.sum(d * xs, axis=0, keepdims=True)
        dcw_ref[CONV_WIDTH - 1:CONV_WIDTH, :] += jnp.sum(d * xr, axis=0, keepdims=True)
        dcb_ref[...] += jnp.sum(d, axis=0, keepdims=True)

    tile = pl.BlockSpec((tr, bw), lambda n, i: (i, n))
    cwb = pl.BlockSpec((CONV_WIDTH, bw), lambda n, i: (0, n))
    return hbm_call(
        body, name=name, grid=(D // bw, nsteps),
        in_specs=[tile, pl.BlockSpec((tr, bw), lambda n, i: (jnp.minimum(i + 1, nsteps - 1), n)), tile,
                  pl.BlockSpec((tr, bw), lambda n, i: (jnp.maximum(i - 1, 0), n)), cwb],
        out_specs=[tile, cwb, pl.BlockSpec((1, bw), lambda n, i: (0, n))],
        out_shape=[jax.ShapeDtypeStruct((S, D), MXU_DTYPE), jax.ShapeDtypeStruct((CONV_WIDTH, D), F32), jax.ShapeDtypeStruct((1, D), F32)],
        compiler_params=_params(("parallel", "arbitrary")),
    )(dxc, dxc, proj, proj, conv_w)


def rope_table(S):
    half = ROT_DIM // 2
    pos = jnp.arange(S, dtype=F32)
    inv = ROPE_THETA ** (-jnp.arange(0, ROT_DIM, 2, dtype=F32) / ROT_DIM)
    ang = pos[:, None] * inv[None, :]
    cos, sin = jnp.cos(ang), jnp.sin(ang)
    zero = jnp.zeros((S, HEAD_DIM - ROT_DIM), F32)
    c = jnp.concatenate([cos, cos, zero + 1.0], axis=1)
    a = jnp.concatenate([-sin, jnp.zeros((S, half), F32), zero], axis=1)
    b = jnp.concatenate([jnp.zeros((S, half), F32), sin, zero], axis=1)
    return jnp.stack([jnp.tile(t, (1, LANES // HEAD_DIM)) for t in (c, a, b)])


def _rope(t, tab):
    half = ROT_DIM // 2
    return t * tab[0] + pltpu.roll(t, LANES - half, 1) * tab[1] + pltpu.roll(t, half, 1) * tab[2]


def _rope_t(d, tab):
    half = ROT_DIM // 2
    return d * tab[0] + pltpu.roll(d * tab[1], half, 1) + pltpu.roll(d * tab[2], LANES - half, 1)


def _dup_head(t, hk, lo):
    sw = pltpu.roll(t, HEAD_DIM, 1)
    return jnp.where(lo, t, sw) if hk == 0 else jnp.where(lo, sw, t)


def _attn_common(n, sink_ref, q_ref, kp_ref, kc_ref, vp_ref, vc_ref, tc_ref, tp_ref, hk, pairs):
    tq = (tc_ref[0], tc_ref[1], tc_ref[2])
    tp = (tp_ref[0], tp_ref[1], tp_ref[2])
    lo = lax.broadcasted_iota(jnp.int32, (WINDOW, LANES), 1) < HEAD_DIM
    lo2 = lax.broadcasted_iota(jnp.int32, (2 * WINDOW, LANES), 1) < HEAD_DIM
    kband = jnp.concatenate([_rope(kp_ref[...], tp), _rope(kc_ref[...], tq)], axis=0)
    vband = jnp.concatenate([vp_ref[...], vc_ref[...]], axis=0)
    kd = _dup_head(kband, hk, lo2).astype(MXU_DTYPE)
    vd = _dup_head(vband, hk, lo2).astype(MXU_DTYPE)
    rows, sks = [], []
    for j in range(pairs):
        col = hk * pairs + j
        qp = _rope(q_ref[:, col * LANES:(col + 1) * LANES], tq)
        rows += [jnp.where(lo, qp, 0.0), jnp.where(lo, 0.0, qp)]
        sks += [jnp.full((WINDOW, 1), sink_ref[2 * col], F32), jnp.full((WINDOW, 1), sink_ref[2 * col + 1], F32)]
    qg = jnp.concatenate(rows, axis=0)
    sk = jnp.concatenate(sks, axis=0)
    G = 2 * pairs * WINDOW
    own = lax.broadcasted_iota(jnp.int32, (G, WINDOW), 1) <= (lax.broadcasted_iota(jnp.int32, (G, WINDOW), 0) & (WINDOW - 1))
    s = lax.dot_general(qg.astype(MXU_DTYPE), kd, (((1,), (1,)), ((), ())), preferred_element_type=F32) * (HEAD_DIM ** -0.5)
    s = jnp.where(own, s[:, WINDOW:], s[:, :WINDOW] + jnp.where(n > 0, 0.0, NEG_INF))
    m = jnp.maximum(jnp.max(s, axis=1, keepdims=True), sk)
    e = jnp.exp(s - m)
    es = jnp.exp(sk - m)
    inv = 1.0 / (jnp.sum(e, axis=1, keepdims=True) + es)
    return qg, kd, vd, e * inv, es * inv, own, lo, lo2, tq, tp


def _unfold_band(t, own):
    return jnp.concatenate([jnp.where(own, 0.0, t), jnp.where(own, t, 0.0)], axis=1)


def _attn_specs(D, NB):
    kcol = 3 * D // LANES
    q = pl.BlockSpec((WINDOW, D), lambda n: (n, 2))
    kc = pl.BlockSpec((WINDOW, LANES), lambda n: (n, kcol))
    kp = pl.BlockSpec((WINDOW, LANES), lambda n: (jnp.maximum(n - 1, 0), kcol))
    vc = pl.BlockSpec((WINDOW, LANES), lambda n: (n, kcol + 1))
    vp = pl.BlockSpec((WINDOW, LANES), lambda n: (jnp.maximum(n - 1, 0), kcol + 1))
    tc = pl.BlockSpec((3, WINDOW, LANES), lambda n: (0, n, 0))
    tp = pl.BlockSpec((3, WINDOW, LANES), lambda n: (0, jnp.maximum(n - 1, 0), 0))
    sink = pl.BlockSpec(memory_space=pltpu.SMEM)
    return [sink, q, kp, kc, vp, vc, tc, tp]


def attn_fwd(proj, sinks, tab, D, name):
    S = proj.shape[0]
    NB = S // WINDOW
    pairs = D // HEAD_DIM // N_KV_HEADS // 2

    def body(sink_ref, q_ref, kp_ref, kc_ref, vp_ref, vc_ref, tc_ref, tp_ref, o_ref):
        n = pl.program_id(0)
        for hk in range(N_KV_HEADS):
            _, _, vd, p, _, own, lo, _, _, _ = _attn_common(n, sink_ref, q_ref, kp_ref, kc_ref, vp_ref, vc_ref, tc_ref, tp_ref, hk, pairs)
            o = jnp.dot(_unfold_band(p, own).astype(MXU_DTYPE), vd, preferred_element_type=F32)
            for j in range(pairs):
                col = hk * pairs + j
                oa = o[(2 * j) * WINDOW:(2 * j + 1) * WINDOW]
                ob = o[(2 * j + 1) * WINDOW:(2 * j + 2) * WINDOW]
                o_ref[:, col * LANES:(col + 1) * LANES] = jnp.where(lo, oa, ob)

    return hbm_call(
        body, name=name, grid=(NB,), in_specs=_attn_specs(D, NB),
        out_specs=pl.BlockSpec((WINDOW, D), lambda n: (n, 0)), out_shape=jax.ShapeDtypeStruct((S, D), F32),
        compiler_params=_params(("parallel",)),
    )(sinks, proj, proj, proj, proj, proj, tab, tab)


def attn_bwd(proj, sinks, tab, o, do, D, name):
    S = proj.shape[0]
    NB = S // WINDOW
    pairs = D // HEAD_DIM // N_KV_HEADS // 2

    def body(sink_ref, q_ref, kp_ref, kc_ref, vp_ref, vc_ref, tc_ref, tp_ref, o_ref, do_ref, dq_ref, dk_ref, dv_ref, ds_ref):
        n = pl.program_id(0)

        @pl.when(n == 0)
        def _():
            ds_ref[...] = jnp.zeros_like(ds_ref)

        lane1 = lax.broadcasted_iota(jnp.int32, (1, LANES), 1)
        dsink = jnp.zeros((1, LANES), F32)
        dkt = dvt = None
        for hk in range(N_KV_HEADS):
            qg, kd, vd, p, ps, own, lo, lo2, tq, tp = _attn_common(n, sink_ref, q_ref, kp_ref, kc_ref, vp_ref, vc_ref, tc_ref, tp_ref, hk, pairs)
            dos, os_ = [], []
            for j in range(pairs):
                col = hk * pairs + j
                dop = do_ref[:, col * LANES:(col + 1) * LANES]
                op = o_ref[:, col * LANES:(col + 1) * LANES]
                dos += [jnp.where(lo, dop, 0.0), jnp.where(lo, 0.0, dop)]
                os_ += [jnp.where(lo, op, 0.0), jnp.where(lo, 0.0, op)]
            dog = jnp.concatenate(dos, axis=0)
            og = jnp.concatenate(os_, axis=0)
            dogm = dog.astype(MXU_DTYPE)
            dp = lax.dot_general(dogm, vd, (((1,), (1,)), ((), ())), preferred_element_type=F32)
            dp = jnp.where(own, dp[:, WINDOW:], dp[:, :WINDOW])
            dr = jnp.sum(dog * og, axis=1, keepdims=True)
            ds = _unfold_band(p * (dp - dr) * (HEAD_DIM ** -0.5), own)
            dsm = ds.astype(MXU_DTYPE)
            dqg = jnp.dot(dsm, kd, preferred_element_type=F32)
            dkd = jnp.dot(ds.T.astype(MXU_DTYPE), qg.astype(MXU_DTYPE), preferred_element_type=F32)
            dvd = jnp.dot(_unfold_band(p, own).T.astype(MXU_DTYPE), dogm, preferred_element_type=F32)
            dkf = dkd + pltpu.roll(dkd, HEAD_DIM, 1)
            dvf = dvd + pltpu.roll(dvd, HEAD_DIM, 1)
            if hk == 0:
                dkt, dvt = dkf, dvf
            else:
                dkt, dvt = jnp.where(lo2, dkt, dkf), jnp.where(lo2, dvt, dvf)
            sd = ps * dr
            for j in range(pairs):
                col = hk * pairs + j
                dqa = dqg[(2 * j) * WINDOW:(2 * j + 1) * WINDOW]
                dqb = dqg[(2 * j + 1) * WINDOW:(2 * j + 2) * WINDOW]
                dq_ref[:, col * LANES:(col + 1) * LANES] = _rope_t(jnp.where(lo, dqa, dqb), tq).astype(dq_ref.dtype)
                for t in range(2):
                    part = sd[(2 * j + t) * WINDOW:(2 * j + t + 1) * WINDOW]
                    val = jnp.sum(part, axis=0, keepdims=True)
                    dsink = dsink - jnp.where(lane1 == 2 * col + t, val, 0.0)
        dk_ref[...] = jnp.concatenate([_rope_t(dkt[:WINDOW], tp), _rope_t(dkt[WINDOW:], tq)], axis=0)
        dv_ref[...] = dvt
        ds_ref[...] += dsink

    blk = pl.BlockSpec((WINDOW, D), lambda n: (n, 0))
    band = pl.BlockSpec((None, 2 * WINDOW, LANES), lambda n: (n, 0, 0))
    return hbm_call(
        body, name=name, grid=(NB,), in_specs=_attn_specs(D, NB) + [blk, blk],
        out_specs=[blk, band, band, pl.BlockSpec((1, LANES), lambda n: (0, 0))],
        out_shape=[jax.ShapeDtypeStruct((S, D), MXU_DTYPE), jax.ShapeDtypeStruct((NB, 2 * WINDOW, LANES), F32),
                   jax.ShapeDtypeStruct((NB, 2 * WINDOW, LANES), F32), jax.ShapeDtypeStruct((1, LANES), F32)],
        compiler_params=_params(("arbitrary",)),
    )(sinks, proj, proj, proj, proj, proj, tab, tab, o, do)


def band_fold(dkb, dvb, name):
    NB = dkb.shape[0]
    k4 = dkb.reshape(NB, 2, WINDOW, LANES)
    v4 = dvb.reshape(NB, 2, WINDOW, LANES)

    def body(kc_ref, kn_ref, vc_ref, vn_ref, dk_ref, dv_ref):
        more = pl.program_id(0) < NB - 1
        dk_ref[...] = (kc_ref[...] + jnp.where(more, kn_ref[...], 0.0)).astype(dk_ref.dtype)
        dv_ref[...] = (vc_ref[...] + jnp.where(more, vn_ref[...], 0.0)).astype(dv_ref.dtype)

    cur = pl.BlockSpec((None, None, WINDOW, LANES), lambda n: (n, 1, 0, 0))
    nxt = pl.BlockSpec((None, None, WINDOW, LANES), lambda n: (jnp.minimum(n + 1, NB - 1), 0, 0, 0))
    out = pl.BlockSpec((WINDOW, LANES), lambda n: (n, 0))
    sds = jax.ShapeDtypeStruct((NB * WINDOW, LANES), MXU_DTYPE)
    return hbm_call(body, name=name, grid=(NB,), in_specs=[cur, nxt, cur, nxt], out_specs=[out, out], out_shape=[sds, sds],
                          compiler_params=_params(("parallel",)))(k4, k4, v4, v4)


CROSS_ROWS = 512


def _cross_probs(q, k, scale):
    s = lax.dot_general(q.astype(MXU_DTYPE), k.astype(MXU_DTYPE), (((1,), (1,)), ((), ())), preferred_element_type=F32) * scale
    e = jnp.exp(s - jnp.max(s, axis=1, keepdims=True))
    return e / jnp.sum(e, axis=1, keepdims=True)


def cross_fwd(qc, kv, name):
    S, D = qc.shape
    M = kv.shape[0]
    hd = D // CROSS_HEADS
    tq = min(CROSS_ROWS, S)

    def body(q_ref, kv_ref, o_ref):
        for h in range(CROSS_HEADS):
            p = _cross_probs(q_ref[:, h * hd:(h + 1) * hd], kv_ref[:, h * hd:(h + 1) * hd], hd ** -0.5)
            v = kv_ref[:, D + h * hd:D + (h + 1) * hd].astype(MXU_DTYPE)
            o_ref[:, h * hd:(h + 1) * hd] = jnp.dot(p.astype(MXU_DTYPE), v, preferred_element_type=F32).astype(o_ref.dtype)

    return hbm_call(
        body, name=name, grid=(S // tq,), in_specs=[pl.BlockSpec((tq, D), lambda i: (i, 0)), pl.BlockSpec((M, 2 * D), lambda i: (0, 0))],
        out_specs=pl.BlockSpec((tq, D), lambda i: (i, 0)), out_shape=jax.ShapeDtypeStruct((S, D), MXU_DTYPE),
        compiler_params=_params(("parallel",)),
    )(qc, kv)


def cross_bwd(qc, kv, do, name):
    S, D = qc.shape
    M = kv.shape[0]
    hd = D // CROSS_HEADS
    tq = min(CROSS_ROWS, S)

    def body(q_ref, kv_ref, do_ref, dq_ref, dkv_ref):
        @pl.when(pl.program_id(0) == 0)
        def _():
            dkv_ref[...] = jnp.zeros_like(dkv_ref)

        for h in range(CROSS_HEADS):
            q = q_ref[:, h * hd:(h + 1) * hd]
            k = kv_ref[:, h * hd:(h + 1) * hd]
            v = kv_ref[:, D + h * hd:D + (h + 1) * hd].astype(MXU_DTYPE)
            dom = do_ref[:, h * hd:(h + 1) * hd].astype(MXU_DTYPE)
            p = _cross_probs(q, k, hd ** -0.5)
            dp = lax.dot_general(dom, v, (((1,), (1,)), ((), ())), preferred_element_type=F32)
            ds = p * (dp - jnp.sum(p * dp, axis=1, keepdims=True)) * (hd ** -0.5)
            dq_ref[:, h * hd:(h + 1) * hd] = jnp.dot(ds.astype(MXU_DTYPE), k.astype(MXU_DTYPE),
                                                     preferred_element_type=F32).astype(dq_ref.dtype)
            dkv_ref[:, h * hd:(h + 1) * hd] += jnp.dot(ds.T.astype(MXU_DTYPE), q.astype(MXU_DTYPE), preferred_element_type=F32)
            dkv_ref[:, D + h * hd:D + (h + 1) * hd] += jnp.dot(p.T.astype(MXU_DTYPE), dom, preferred_element_type=F32)

    row = pl.BlockSpec((tq, D), lambda i: (i, 0))
    full = pl.BlockSpec((M, 2 * D), lambda i: (0, 0))
    return hbm_call(
        body, name=name, grid=(S // tq,), in_specs=[row, full, row], out_specs=[row, full],
        out_shape=[jax.ShapeDtypeStruct((S, D), MXU_DTYPE), jax.ShapeDtypeStruct((M, 2 * D), F32)],
        compiler_params=_params(("arbitrary",)),
    )(qc, kv, do)


def adamw(w, g, m, v, name, layers=None, into=None):
    shape = w.shape
    cols = shape[-1]
    lead = shape[0] if len(shape) > 2 else 1
    rows = int(np.prod(shape[:-1])) // lead
    w2, g2, m2, v2 = (t.reshape(lead, rows, cols) for t in (w, g, m, v))
    tr = _divisors(rows, SUBLANES, max(SUBLANES, (1 << 20) // (cols * 4) // SUBLANES * SUBLANES))[0]
    lo, hi = layers or (0, lead)
    done = [t.reshape(lead, rows, cols) for t in into] if into else []

    def body(w_ref, g_ref, m_ref, v_ref, *refs):
        d_ref, mo_ref, vo_ref, go_ref = refs[len(done):]
        gg = g_ref[...]
        mn = ADAM_B1 * m_ref[...] + (1.0 - ADAM_B1) * gg
        vn = ADAM_B2 * v_ref[...] + (1.0 - ADAM_B2) * (gg * gg)
        m_hat = mn / (1.0 - ADAM_B1 ** ADAM_STEP)
        v_hat = vn / (1.0 - ADAM_B2 ** ADAM_STEP)
        d_ref[...] = -ADAM_LR * (m_hat / (jnp.sqrt(v_hat) + ADAM_EPS) + ADAM_WD * w_ref[...])
        mo_ref[...] = mn
        vo_ref[...] = vn
        go_ref[...] = gg

    blk = pl.BlockSpec((None, tr, cols), lambda l, i: (l + lo, i, 0))
    sds = jax.ShapeDtypeStruct((lead, rows, cols), F32)
    d, mn, vn, go = hbm_call(body, name=name, grid=(hi - lo, rows // tr), in_specs=[blk] * 4 + [pl.BlockSpec(memory_space=pl.ANY)] * len(done),
                             out_specs=[blk] * 4, out_shape=[sds] * 4, input_output_aliases={4 + k: k for k in range(len(done))},
                             compiler_params=_params(("parallel", "parallel")))(w2, g2, m2, v2, *done)
    return d.reshape(shape), mn.reshape(shape), vn.reshape(shape), go.reshape(shape)


def sum_devices(parts, name):
    n, rows, cols = parts.shape

    def body(p_ref, o_ref):
        acc = p_ref[0]
        for k in range(1, n):
            acc = acc + p_ref[k]
        o_ref[...] = acc

    return pl.pallas_call(body, name=name, in_specs=[pl.BlockSpec(memory_space=pltpu.VMEM)],
                          out_specs=pl.BlockSpec(memory_space=pltpu.VMEM), out_shape=jax.ShapeDtypeStruct((rows, cols), F32))(parts)


HBM_SPEC = pl.BlockSpec(memory_space=pltpu.HBM)


def _place():
    return lax.axis_index("x"), lax.axis_index("y"), lax.axis_index("c")


def _remote(src, dst, send_sems, recv_sems, k, to):
    return pltpu.make_async_remote_copy(src_ref=src, dst_ref=dst, send_sem=send_sems.at[k], recv_sem=recv_sems.at[k],
                                        device_id=to, device_id_type=MESH_ID)


SEM_SPEC = pl.BlockSpec(memory_space=pltpu.SEMAPHORE)
ANY_SPEC = pl.BlockSpec(memory_space=pl.ANY)
SPLIT_COPY = pltpu.CompilerParams(has_side_effects=pltpu.SideEffectType.DATAFLOW_SIDE_EFFECTING)


def _in_hbm(arrays):
    return [pltpu.with_memory_space_constraint(a, pltpu.HBM) for a in arrays]


def _split_start(copies, sources, lands, after, n_sems, name):
    n = len(sources)

    def body(*refs):
        for cp in copies(refs[:n], refs[n:2 * n], refs[2 * n + 1], refs[2 * n + 2]):
            cp.start()
        refs[-1][...] = jnp.zeros_like(refs[-1])

    through = [pltpu.HBM(a.shape, a.dtype) for a in list(sources) + list(lands)]
    outs = pl.pallas_call(
        body, name=name, in_specs=[HBM_SPEC] * (2 * n) + [ANY_SPEC],
        out_specs=[SEM_SPEC, SEM_SPEC] + [HBM_SPEC] * (2 * n) + [pl.BlockSpec(memory_space=pltpu.VMEM)],
        out_shape=[pltpu.SemaphoreType.DMA((n_sems,)), pltpu.SemaphoreType.DMA((n_sems,))] + through
        + [jax.ShapeDtypeStruct((SUBLANES, LANES), F32)],
        input_output_aliases={i: 2 + i for i in range(2 * n)}, compiler_params=SPLIT_COPY,
    )(*_in_hbm(sources), *_in_hbm(lands), after)
    return outs[0], outs[1], outs[2:2 + n], outs[2 + n:2 + 2 * n], outs[-1]


def _split_wait(copies, send_sems, recv_sems, sources, lands, after, name):
    n = len(sources)

    def body(*refs):
        for cp in copies(refs[:n], refs[n:2 * n], refs[2 * n], refs[2 * n + 1]):
            cp.wait_send()
            cp.wait_recv()

    through = [pltpu.HBM(a.shape, a.dtype) for a in list(sources) + list(lands)]
    outs = pl.pallas_call(
        body, name=name, in_specs=[HBM_SPEC] * (2 * n) + [SEM_SPEC, SEM_SPEC, ANY_SPEC], out_specs=[HBM_SPEC] * (2 * n),
        out_shape=through, input_output_aliases={i: i for i in range(2 * n)}, compiler_params=SPLIT_COPY,
    )(*sources, *lands, send_sems, recv_sems, after)
    return outs[:n], outs[n:]


def _chip_slab(land, slot, rows):
    return land.at[slot, rows] if len(land.shape) == 3 else land.at[rows, slot]


def _gather_copies(w_refs, land_refs, send_sems, recv_sems):
    n = len(w_refs)
    x, y, c = _place()
    chips = [(1 - x, y), (x, 1 - y), (1 - x, 1 - y)]
    cps = []
    for a in range(n):
        hr = w_refs[a].shape[0] // 2
        mine, every = pl.ds(c * hr, hr), pl.ds(0, 2 * hr)
        cps.append(_remote(w_refs[a], _chip_slab(land_refs[a], 2 * x + y, every), send_sems, recv_sems, 3 * n + a, (x, y, 1 - c)))
        for k, chip in enumerate(chips):
            cps.append(_remote(w_refs[a].at[mine], _chip_slab(land_refs[a], 2 * x + y, mine), send_sems, recv_sems, 3 * a + k, (*chip, c)))
    return cps


def gather_start(shards, after, name):
    lands = [lax.empty(s.shape[:-2] + (N_CHIPS,) + s.shape[-2:], s.dtype) for s in shards]
    return _split_start(_gather_copies, shards, lands, after, 4 * len(shards), name)


def gather_wait(state, after, name):
    send_sems, recv_sems, sources, lands, _ = state
    return _split_wait(_gather_copies, send_sems, recv_sems, sources, lands, after, name)[1]


def gather_pass(lands, name):
    n = len(lands)

    def body(*refs):
        out_refs, send_sems, recv_sems = refs[n:2 * n], refs[2 * n], refs[2 * n + 1]
        x, y, c = _place()
        chips = [(1 - x, y), (x, 1 - y), (1 - x, 1 - y)]
        sent = []
        for a in range(n):
            hr = out_refs[a].shape[0 if len(out_refs[a].shape) == 4 else 1] // 2
            for k, (px, py) in enumerate(chips):
                landed = _chip_slab(out_refs[a], 2 * px + py, pl.ds(c * hr, hr))
                sent.append(_remote(landed, landed, send_sems, recv_sems, 3 * a + k, (x, y, 1 - c)))
        for cp in sent:
            cp.start()
        for a in range(n):
            hr = out_refs[a].shape[0 if len(out_refs[a].shape) == 4 else 1] // 2
            for k, (px, py) in enumerate(chips):
                theirs = _chip_slab(out_refs[a], 2 * px + py, pl.ds((1 - c) * hr, hr))
                _remote(theirs, theirs, send_sems, recv_sems, 3 * a + k, (x, y, 1 - c)).wait_recv()
        for cp in sent:
            cp.wait_send()

    return hbm_call(
        body, name=name, in_specs=[HBM_SPEC] * n, out_specs=[HBM_SPEC] * n,
        out_shape=[jax.ShapeDtypeStruct(a.shape, a.dtype) for a in lands], input_output_aliases={a: a for a in range(n)},
        scratch_shapes=[pltpu.SemaphoreType.DMA((3 * n,))] * 2,
    )(*lands)


def _scatter_copies(t_refs, land_refs, send_sems, recv_sems):
    x, y, c = _place()
    chips = [(1 - x, y), (x, 1 - y), (1 - x, 1 - y)]
    return [_remote(t_refs[a].at[:, 2 * px + py], land_refs[a].at[:, k], send_sems, recv_sems, 3 * a + k, (px, py, c))
            for a in range(len(t_refs)) for k, (px, py) in enumerate(chips)]


def scatter_start(parts, after, name):
    lands = [lax.empty((t.shape[0], N_CHIPS - 1) + t.shape[2:], t.dtype) for t in parts]
    return _split_start(_scatter_copies, parts, lands, after, 3 * len(parts), name)


def scatter_wait(state, after, name):
    send_sems, recv_sems, sources, lands, _ = state
    return _split_wait(_scatter_copies, send_sems, recv_sems, sources, lands, after, name)


def swap_sibling(parts, name):
    n = len(parts)

    def body(*refs):
        v_refs, out_refs, send_sems, recv_sems = refs[:n], refs[n:2 * n], refs[2 * n], refs[2 * n + 1]
        x, y, c = _place()
        cps = []
        for a in range(n):
            hr = v_refs[a].shape[2] // 2
            cps.append(_remote(v_refs[a].at[:, :, pl.ds((1 - c) * hr, hr)], out_refs[a], send_sems, recv_sems, a, (x, y, 1 - c)))
        for cp in cps:
            cp.start()
        for cp in cps:
            cp.wait()

    return hbm_call(
        body, name=name, in_specs=[HBM_SPEC] * n, out_specs=[HBM_SPEC] * n,
        out_shape=[jax.ShapeDtypeStruct(v.shape[:2] + (v.shape[2] // 2, v.shape[3]), v.dtype) for v in parts],
        scratch_shapes=[pltpu.SemaphoreType.DMA((n,))] * 2,
    )(*parts)


def join_halves(halves, layer, name):
    n = len(halves)

    def body(*refs):
        out_refs, send_sems, recv_sems = refs[n:2 * n], refs[2 * n], refs[2 * n + 1]
        x, y, c = _place()
        cps = []
        for a in range(n):
            hr = out_refs[a].shape[1] // 2
            mine = out_refs[a].at[layer, pl.ds(c * hr, hr)]
            cps.append(_remote(mine, mine, send_sems, recv_sems, a, (x, y, 1 - c)))
        for cp in cps:
            cp.start()
        for a in range(n):
            hr = out_refs[a].shape[1] // 2
            theirs = out_refs[a].at[layer, pl.ds((1 - c) * hr, hr)]
            _remote(theirs, theirs, send_sems, recv_sems, a, (x, y, 1 - c)).wait_recv()
        for cp in cps:
            cp.wait_send()

    return hbm_call(
        body, name=name, in_specs=[HBM_SPEC] * n, out_specs=[HBM_SPEC] * n,
        out_shape=[jax.ShapeDtypeStruct(f.shape, f.dtype) for f in halves], input_output_aliases={a: a for a in range(n)},
        scratch_shapes=[pltpu.SemaphoreType.DMA((n,))] * 2,
    )(*halves)


def gather_devices(v, name, after=()):
    def body(v_ref, *refs):
        out_ref, send_sems, recv_sems, local_sem = refs[len(after):]
        x, y, c = _place()
        me = 4 * x + 2 * y + c
        own = pltpu.make_async_copy(v_ref, out_ref.at[me], local_sem)
        own.start()
        peers = [((x + dx) % 2, (y + dy) % 2, (c + dc) % 2) for dx in (0, 1) for dy in (0, 1) for dc in (0, 1)][1:]
        sent = []
        for k, peer in enumerate(peers):
            cp = pltpu.make_async_remote_copy(src_ref=v_ref, dst_ref=out_ref.at[me], send_sem=send_sems.at[k], recv_sem=recv_sems.at[k],
                                              device_id=peer, device_id_type=MESH_ID)
            cp.start()
            sent.append(cp)
        for k, (px, py, pc) in enumerate(peers):
            slot = out_ref.at[4 * px + 2 * py + pc]
            pltpu.make_async_remote_copy(src_ref=slot, dst_ref=slot, send_sem=send_sems.at[k], recv_sem=recv_sems.at[k],
                                         device_id=(px, py, pc), device_id_type=MESH_ID).wait_recv()
        for cp in sent:
            cp.wait_send()
        own.wait()

    vm = pl.BlockSpec(memory_space=pltpu.VMEM)
    return pl.pallas_call(body, name=name, in_specs=[vm] + [ANY_SPEC] * len(after), out_specs=vm,
                          out_shape=jax.ShapeDtypeStruct((N_DEV,) + v.shape, v.dtype),
                          scratch_shapes=[pltpu.SemaphoreType.DMA((N_DEV - 1,)), pltpu.SemaphoreType.DMA((N_DEV - 1,)),
                                          pltpu.SemaphoreType.DMA])(v, *after)


ADD_ROWS = 512


def add_pair(place, a, b, name):
    L, n, hr, cols = b.shape
    tr = _divisors(hr, 2 * SUBLANES, ADD_ROWS)[0]
    nb = hr // tr

    def body(p_ref, a_ref, b_ref, o_ref):
        del p_ref
        o_ref[...] = (a_ref[...].astype(F32) + b_ref[...].astype(F32)).astype(o_ref.dtype)

    blk = pl.BlockSpec((None, None, tr, cols), lambda l, d, i, p: (l, d, i, 0))
    grid_spec = pltpu.PrefetchScalarGridSpec(
        num_scalar_prefetch=1, grid=(L, n, nb),
        in_specs=[pl.BlockSpec((None, None, tr, cols), lambda l, d, i, p: (l, d, p[0] * nb + i, 0)), blk], out_specs=blk)
    return hbm_call(body, name=name, grid_spec=grid_spec, out_shape=jax.ShapeDtypeStruct(b.shape, b.dtype),
                          compiler_params=_params(("parallel", "parallel", "parallel")))(place, a, b)


def add_chips(place, own, others, layer, stacked, name):
    _, n, hr, cols = others.shape
    tr = _divisors(hr, 2 * SUBLANES, ADD_ROWS)[0]
    nb = hr // tr
    create = isinstance(stacked, tuple)

    def body(p_ref, own_ref, *refs):
        del p_ref
        acc = own_ref[...].astype(F32)
        for k in range(n):
            acc = acc + refs[k][...].astype(F32)
        refs[-1][...] = acc

    ins = [pl.BlockSpec((None, None, tr, cols), lambda i, p: (0, p[1], i, 0))]
    ins += [pl.BlockSpec((None, None, tr, cols), functools.partial(lambda k, i, p: (0, k, i, 0), k)) for k in range(n)]
    grid_spec = pltpu.PrefetchScalarGridSpec(num_scalar_prefetch=1, grid=(nb,), in_specs=ins + ([] if create else [ANY_SPEC]),
                                             out_specs=pl.BlockSpec((None, tr, cols), lambda i, p: (layer, p[0] * nb + i, 0)))
    shape = stacked if create else stacked.shape
    return hbm_call(body, name=name, grid_spec=grid_spec, out_shape=jax.ShapeDtypeStruct(shape, F32),
                          input_output_aliases={} if create else {n + 2: 0},
                          compiler_params=_params(("parallel",)))(place, own, *([others] * n), *([] if create else [stacked]))


def _alpha(depth):
    return (2 * depth) ** 0.25


def _wmm(a, weight, mode, name, deps=(), **more):
    arr, how = weight
    return mm(a, arr, mode, name, deps=deps, **how, **more)


def layer_fwd(h, mem, w, tab, alpha, deps=(), late=None):
    D = h.shape[1]
    proj = _wmm(h, w["w_in"], "nt", "mm_proj", deps)
    xc, r, ig, a, b = rg_gates_fwd(proj, w["conv_w"], w["conv_b"], w["w_rg"], w["b_rg"], w["w_ig"], w["b_ig"], w["lru_lambda"], "rg_gates_fwd")
    hs, y_rnn = rg_scan_fwd(proj, a, b, "rg_scan_fwd")
    y_attn = attn_fwd(proj, w["sinks"], tab, D, "attn_fwd")
    deps = ()
    if late is not None:
        rest, deps = late(y_attn)
        w = {**w, **rest}
    pr = _wmm(y_rnn, w["w_br_rnn"], "nn", "mm_br_rnn", deps)
    pa = _wmm(y_attn, w["w_br_attn"], "nn", "mm_br_attn")
    merged = merge_fwd(proj, pr, pa, "merge_fwd")
    h1, xh1, rs1 = _wmm(merged, w["w_out"], "nn", "mm_out_ln1", post_norm=(h, w["ln1_g"], w["ln1_b"], alpha))
    qc = _wmm(h1, w["cq_w"], "nn", "mm_cq", out_dtype=MXU_DTYPE)
    kv = _wmm(mem, w["ckv_w"], "nn", "mm_ckv", out_dtype=MXU_DTYPE)
    o = cross_fwd(qc, kv, "cross_fwd")
    h2, xh2, rs2 = _wmm(o, w["co_w"], "nn", "mm_co_ln2", post_norm=(h1, w["ln2_g"], w["ln2_b"], alpha))
    gu = _wmm(h2, w["ffn_wi"], "nn", "mm_ffn_wi", out_blocks=2)
    act = swiglu_fwd(gu, "swiglu_fwd")
    h3, xh3, rs3 = _wmm(act, w["ffn_wo"], "nn", "mm_ffn_wo_ln3", post_norm=(h2, w["ln3_g"], w["ln3_b"], alpha))
    saved = dict(h=h, proj=proj, xc=xc, r=r, ig=ig, a=a, hs=hs, y_rnn=y_rnn, y_attn=y_attn, pr=pr, pa=pa, xh1=xh1, rs1=rs1, h1=h1,
                 qc=qc, kv=kv, o=o, xh2=xh2, rs2=rs2, h2=h2, gu=gu, xh3=xh3, rs3=rs3, merged=merged, act=act)
    return h3, saved, w


def layer_bwd(dh, mem, w, s, tab, alpha, deps=(), halfway=None):
    D = dh.shape[1]
    g = {}
    wg = dict(out_dtype=MXU_DTYPE)
    dz3, g["ln3_g"], g["ln3_b"] = ln_bwd(dh, None, s["xh3"], s["rs3"], w["ln3_g"], 1.0, "ln3_bwd")
    g["ffn_wo"] = mm(s["act"], dz3, "tn", "mm_d_ffn_wo", deps=deps, **wg)
    dact = _wmm(dz3, w["ffn_wo"], "nt", "mm_dact")
    dgu = swiglu_bwd(s["gu"], dact, "swiglu_bwd")
    g["ffn_wi"] = mm(s["h2"], dgu, "tn", "mm_d_ffn_wi", b_blocks=2, out_blocks=N_CHIPS, **wg)
    dh2 = _wmm(dgu, w["ffn_wi"], "nt", "mm_dh2", a_blocks=2)
    dz2, g["ln2_g"], g["ln2_b"] = ln_bwd(dz3, dh2, s["xh2"], s["rs2"], w["ln2_g"], alpha, "ln2_bwd")
    g["co_w"] = mm(s["o"], dz2, "tn", "mm_d_co", **wg)
    do = _wmm(dz2, w["co_w"], "nt", "mm_do", out_dtype=MXU_DTYPE)
    dqc, dkv = cross_bwd(s["qc"], s["kv"], do, "cross_bwd")
    g["cq_w"] = mm(s["h1"], dqc, "tn", "mm_d_cq", **wg)
    g["ckv_w"] = mm(mem, dkv, "tn", "mm_d_ckv", out_blocks=N_CHIPS, **wg)
    dh1 = _wmm(dqc, w["cq_w"], "nt", "mm_dh1")
    deps = halfway(g, dh1) if halfway is not None else ()
    dz1, g["ln1_g"], g["ln1_b"] = ln_bwd(dz2, dh1, s["xh1"], s["rs1"], w["ln1_g"], alpha, "ln1_bwd")
    g["w_out"] = mm(s["merged"], dz1, "tn", "mm_d_out", deps=deps, **wg)
    dm = _wmm(dz1, w["w_out"], "nt", "mm_dmerged")
    dpr, dpa, dg_rnn, dg_attn = merge_bwd(s["proj"], s["pr"], s["pa"], dm, "merge_bwd")
    g["w_br_rnn"] = mm(s["y_rnn"], dpr, "tn", "mm_d_br_rnn", **wg)
    g["w_br_attn"] = mm(s["y_attn"], dpa, "tn", "mm_d_br_attn", **wg)
    dy_rnn = _wmm(dpr, w["w_br_rnn"], "nt", "mm_dy_rnn")
    dy_attn = _wmm(dpa, w["w_br_attn"], "nt", "mm_dy_attn")
    dq, dkb, dvb, dsink = attn_bwd(s["proj"], w["sinks"], tab, s["y_attn"], dy_attn, D, "attn_bwd")
    dk, dv = band_fold(dkb, dvb, "band_fold")
    g["sinks"] = dsink[:, :w["sinks"].shape[0]]
    dgr, gt = rg_scan_bwd(s["proj"], dy_rnn, s["hs"], s["a"], "rg_scan_bwd")
    dxc, g["w_rg"], g["w_ig"], g["b_rg"], g["b_ig"], g["lru_lambda"] = rg_gates_bwd(
        gt, s["hs"], s["xc"], s["r"], s["ig"], w["w_rg"], w["w_ig"], w["lru_lambda"], "rg_gates_bwd")
    dxr, g["conv_w"], g["conv_b"] = rg_conv_bwd(s["proj"], dxc, w["conv_w"], "rg_conv_bwd")
    dproj = jnp.concatenate([dxr, dgr, dq, dk, dv, dg_rnn, dg_attn], axis=1)
    g["w_in"] = mm(dproj, s["h"], "tn", "mm_d_in", **wg)
    return _wmm(dproj, w["w_in"], "nn", "mm_dh", plus=(dz1, alpha)), g


def local_step(x, mem, target, depth, weights_of, grads_halfway, grads_done):
    alpha = _alpha(depth)
    tab = rope_table(x.shape[0])
    h, saved, layers = x, [], []
    for l in range(depth):
        wl, deps, late = weights_of(l, h)
        h, s, wl = layer_fwd(h, mem, wl, tab, alpha, deps, late)
        layers.append(wl)
        saved.append(s)
    dh, loss = loss_head(h, target, "loss_head")
    deps = ()
    for l in reversed(range(depth)):
        dh, g = layer_bwd(dh, mem, layers[l], saved[l], tab, alpha, deps, grads_halfway(l))
        deps = grads_done(l, g, dh)
    return loss, dh


def _pad_rows(flat):
    n = flat.shape[0]
    rows = -(-n // (LANES * SUBLANES)) * SUBLANES
    return jnp.pad(flat, (0, rows * LANES - n)).reshape(rows, LANES)


def kernel(x, mem, w_in, conv_w, conv_b, w_rg, b_rg, w_ig, b_ig, lru_lambda, w_br_rnn, w_br_attn, sinks, w_out, ln1_g, ln1_b, cq_w, ckv_w, co_w, ln2_g, ln2_b, ffn_wi, ffn_wo, ln3_g, ln3_b, loss_target, m_w_in, m_conv_w, m_conv_b, m_w_rg, m_b_rg, m_w_ig, m_b_ig, m_lru_lambda, m_w_br_rnn, m_w_br_attn, m_sinks, m_w_out, m_ln1_g, m_ln1_b, m_cq_w, m_ckv_w, m_co_w, m_ln2_g, m_ln2_b, m_ffn_wi, m_ffn_wo, m_ln3_g, m_ln3_b, v_w_in, v_conv_w, v_conv_b, v_w_rg, v_b_rg, v_w_ig, v_b_ig, v_lru_lambda, v_w_br_rnn, v_w_br_attn, v_sinks, v_w_out, v_ln1_g, v_ln1_b, v_cq_w, v_ckv_w, v_co_w, v_ln2_g, v_ln2_b, v_ffn_wi, v_ffn_wo, v_ln3_g, v_ln3_b):
    args = dict(locals())
    w = {n: args[n] for n in WEIGHTS}
    m = {n: args["m_" + n] for n in WEIGHTS}
    v = {n: args["v_" + n] for n in WEIGHTS}
    for group in (w, m, v):
        group["w_in"] = jnp.swapaxes(group["w_in"], 1, 2)
    cx, cy, cc = _place()
    chip = 2 * cx + cy
    L = w_in.shape[0]

    place = jnp.stack([cc, chip]).astype(jnp.int32)
    cw_rows = _pad_rows(conv_w.reshape(-1))
    cw_all = gather_devices(cw_rows, "gather_conv_w")[0::2]
    cw_parts = cw_all.reshape(N_CHIPS, -1)[:, :conv_w.size].reshape((N_CHIPS,) + conv_w.shape)
    conv_full = jnp.concatenate([cw_parts[k] for k in range(N_CHIPS)], axis=2)

    shards = [{n: w[n][l].astype(MXU_DTYPE) for n in BIG} for l in range(L)]
    late_names = tuple(n for n in BIG if n not in GATHER_FIRST)
    gathering = {(0, GATHER_FIRST): gather_start([shards[0][n] for n in GATHER_FIRST], cw_rows, "gather_start_0a")}
    gathering[0, late_names] = gather_start([shards[0][n] for n in late_names], gathering[0, GATHER_FIRST][4], "gather_start_0b")

    def gathered(l, names, after, tag):
        lands = gather_pass(gather_wait(gathering.pop((l, names)), after, f"gather_wait_{tag}"), f"gather_pass_{tag}")
        wl = {}
        for n, gw in zip(names, lands):
            rows_joined = gw.reshape(gw.shape[:-3] + (-1, gw.shape[-1]))
            if n in COL_BLOCKED:
                wl[n] = (gw, dict(b_blocks=N_CHIPS))
            elif n in GATE_WEIGHTS:
                wl[n] = rows_joined
            else:
                wl[n] = (rows_joined, {})
        return wl, lands

    def start_layer(l, after):
        if l >= L:
            return ()
        gathering[l, BIG] = gather_start([shards[l][n] for n in BIG], after, f"gather_start_{l}")
        return (gathering[l, BIG][4],)

    def weights_of(l, h):
        deps, late = (), None
        if l == 0:
            wl, _ = gathered(0, GATHER_FIRST, h, "0a")

            def late(after):
                rest, lands = gathered(0, late_names, after, "0b")
                return rest, start_layer(1, lands[0])
        else:
            wl, lands = gathered(l, BIG, h, str(l))
            deps = start_layer(l + 1, lands[0])
        for n in SMALL:
            wl[n] = conv_full[l] if n == "conv_w" else w[n][l] if n == "sinks" else w[n][l][None, :]
        return wl, deps, late

    def for_chips(n, g):
        if n in COL_BLOCKED:
            return g
        if n in GATE_WEIGHTS:
            nb, bw, _ = g.shape
            g = g.reshape(nb, N_CHIPS, bw // N_CHIPS, bw).transpose(1, 0, 2, 3).reshape(N_CHIPS, nb * bw // N_CHIPS, bw)
        else:
            g = g.reshape(N_CHIPS, g.shape[0] // N_CHIPS, g.shape[1])
        return g.astype(MXU_DTYPE)

    reduced, scattering, small_grads = {}, {}, [None] * L
    late_grads = tuple(n for n in BIG if n not in SCATTER_FIRST)

    def start_scatter(l, names, g, after, tag):
        partial_sums = [for_chips(n, g[n])[None] for n in names]
        from_sibling = swap_sibling(partial_sums, f"grad_to_sibling_{tag}")
        chip_sums = [add_pair(place, a, b, f"grad_add_pair_{n}_{l}") for n, a, b in zip(names, partial_sums, from_sibling)]
        scattering[l, names] = scatter_start(chip_sums, after, f"grad_scatter_start_{tag}")
        return (scattering[l, names][4],)

    def finish_layer(l, after):
        for names in [k[1] for k in list(scattering) if k[0] == l]:
            tag = str(l) if names == BIG else f"{l}{'a' if names == SCATTER_FIRST else 'b'}"
            chip_sums, from_chips = scatter_wait(scattering.pop((l, names)), after, f"grad_scatter_wait_{tag}")
            for n, own, others in zip(names, chip_sums, from_chips):
                target = reduced.get(n, (L, 2 * own.shape[2], own.shape[3]))
                reduced[n] = add_chips(place, own, others, l, target, f"grad_add_chips_{n}_{l}")
        reduced.update(zip(BIG, join_halves([reduced[n] for n in BIG], l, f"grad_join_{l}")))

    def grads_halfway(l):
        def halfway(g, after):
            return start_scatter(l, SCATTER_FIRST, g, after, f"{l}a")

        return halfway

    def grads_done(l, g, dh):
        small_grads[l] = {n: g[n] for n in SMALL}
        deps = start_scatter(l, late_grads, g, dh, f"{l}b")
        if 1 < l + 1 < L:
            finish_layer(l + 1, dh)
        return deps

    loss11, dx = local_step(x[0], mem[0], loss_target[0], L, weights_of, grads_halfway, grads_done)
    loss = lax.psum(loss11[0, 0], ("x", "y", "c"))

    first = min(2, L)
    updated = {}
    if first < L:
        for n in BIG:
            updated[n] = adamw(w[n], reduced[n].reshape(w[n].shape), m[n], v[n], f"adamw_{n}_upper", layers=(first, L))
    behind = (jnp.stack([updated[n][0][(0,) * w[n].ndim] for n in updated]),) if updated else ()
    small_full = {n: jnp.stack([gl[n] for gl in small_grads]).reshape(w[n].shape[:1] + ((CONV_WIDTH, -1) if n == "conv_w" else (-1,)))
                  for n in SMALL}
    small_flat = jnp.concatenate([small_full[n].reshape(-1) for n in SMALL])
    small_sum = sum_devices(gather_devices(_pad_rows(small_flat), "gather_small_grads", behind), "sum_small_grads").reshape(-1)
    delta, new_m, new_v, grad = {}, {}, {}, {}
    off = 0
    for n in SMALL:
        gfull = small_sum[off:off + small_full[n].size].reshape(small_full[n].shape)
        off += small_full[n].size
        if n == "conv_w":
            width = conv_w.shape[2]
            gfull = lax.dynamic_slice_in_dim(gfull, chip * width, width, axis=2)
        delta[n], new_m[n], new_v[n], grad[n] = adamw(w[n], gfull, m[n], v[n], "adamw_" + n)
    after = jnp.stack([delta[n][(0,) * delta[n].ndim] for n in SMALL])
    for l in reversed(range(first)):
        finish_layer(l, after)

    for n in BIG:
        some = dict(layers=(0, first), into=updated[n]) if updated else {}
        delta[n], new_m[n], new_v[n], grad[n] = adamw(w[n], reduced[n].reshape(w[n].shape), m[n], v[n], "adamw_" + n, **some)
    for group in (delta, new_m, new_v, grad):
        group["w_in"] = jnp.swapaxes(group["w_in"], 1, 2)
    return (loss, dx[None], *[grad[n] for n in WEIGHTS], *[delta[n] for n in WEIGHTS], *[new_m[n] for n in WEIGHTS],
            *[new_v[n] for n in WEIGHTS])
```

```python
import functools
import math

import jax
import jax.numpy as jnp
import numpy as np
from jax import lax
from jax.experimental import pallas as pl
from jax.experimental.pallas import tpu as pltpu

F32 = jnp.float32
BF16 = jnp.bfloat16
MXU_DTYPE = BF16

HEAD_DIM = 64
N_KV_HEADS = 2
WINDOW = 128
ROT_DIM = HEAD_DIM // 4
ROPE_THETA = 500000.0
CROSS_HEADS = 4
CONV_WIDTH = 4
LRU_C = 8.0
LN_EPS = 1e-5
NEG_INF = -1e30
ADAM_LR = 0.001
ADAM_B1 = 0.9
ADAM_B2 = 0.999
ADAM_EPS = 1e-08
ADAM_WD = 0.01
ADAM_STEP = 10

VMEM_BYTES_V7X = 64 * 1024 * 1024
VMEM_BLOCK_BUDGET = 36 * 1024 * 1024
LANES = 128
SUBLANES = 8

MESH_ID = pl.DeviceIdType.MESH
N_CHIPS = 4
N_DEV = 8

BIG = ("w_in", "w_rg", "w_ig", "w_br_rnn", "w_br_attn", "w_out", "cq_w", "ckv_w", "co_w", "ffn_wi", "ffn_wo")
SHARD_AXIS = {"w_in": 0, "w_rg": 1, "w_ig": 1, "w_br_rnn": 0, "w_br_attn": 0, "w_out": 0, "cq_w": 0, "ckv_w": 1,
              "co_w": 0, "ffn_wi": 1, "ffn_wo": 0}
SMALL = ("conv_w", "conv_b", "b_rg", "b_ig", "lru_lambda", "sinks", "ln1_g", "ln1_b", "ln2_g", "ln2_b", "ln3_g", "ln3_b")
WEIGHTS = ("w_in", "conv_w", "conv_b", "w_rg", "b_rg", "w_ig", "b_ig", "lru_lambda", "w_br_rnn", "w_br_attn", "sinks",
           "w_out", "ln1_g", "ln1_b", "cq_w", "ckv_w", "co_w", "ln2_g", "ln2_b", "ffn_wi", "ffn_wo", "ln3_g", "ln3_b")
GATE_WEIGHTS = ("w_rg", "w_ig")
COL_BLOCKED = ("ckv_w", "ffn_wi")
GATHER_FIRST = ("w_in", "w_rg", "w_ig")
SCATTER_FIRST = ("ffn_wo", "ffn_wi", "co_w", "cq_w", "ckv_w")


def _params(dims=None, vmem=None):
    return pltpu.CompilerParams(dimension_semantics=dims, vmem_limit_bytes=vmem)


def _vmem_limit(block_bytes, temp_bytes=0):
    want = int(2 * block_bytes + temp_bytes) + (6 << 20)
    return max(32 << 20, min(want, VMEM_BYTES_V7X - (6 << 20)))


def _divisors(n, align, cap):
    out = [d for d in range(align, min(n, cap) + 1, align) if n % d == 0]
    if n <= cap and n not in out:
        out.append(n)
    return sorted(out, reverse=True) or [n]


PIN_MIN_ELEMENTS = 1 << 18


def hbm_call(body, **kw):
    def in_hbm(s):
        return pltpu.HBM(s.shape, s.dtype) if math.prod(s.shape) >= PIN_MIN_ELEMENTS else s

    shapes = kw.pop("out_shape")
    shapes = [in_hbm(s) for s in shapes] if isinstance(shapes, (list, tuple)) else in_hbm(shapes)
    call = pl.pallas_call(body, out_shape=shapes, **kw)

    def run(*args):
        return call(*[pltpu.with_memory_space_constraint(a, pltpu.HBM) if a.size >= PIN_MIN_ELEMENTS else a for a in args])

    return run


def _sigmoid(x):
    return 1.0 / (1.0 + jnp.exp(-x))


def _gelu_parts(x):
    c = math.sqrt(2.0 / math.pi)
    u = c * (x + 0.044715 * x * x * x)
    t = jnp.tanh(u)
    return t, c * (1.0 + 3 * 0.044715 * x * x)


def _gelu(x):
    t, _ = _gelu_parts(x)
    return 0.5 * x * (1.0 + t)


def _gelu_grad(x):
    t, du = _gelu_parts(x)
    return 0.5 * (1.0 + t) + 0.5 * x * (1.0 - t * t) * du


def _neg_expm1(x):
    series = x * (1.0 + x * (0.5 + x * (1.0 / 6 + x * (1.0 / 24 + x * (1.0 / 120)))))
    return -jnp.where(x > -0.1, series, jnp.exp(x) - 1.0)


def _softplus_neg(lam):
    x = -lam
    return jnp.maximum(x, 0.0) + jnp.log1p(jnp.exp(-jnp.abs(x)))


STEP_US = 0.35
HBM_BYTES_PER_US = 2.5e6
MXU_FLOPS_PER_US = 7e8


def _layer_norm(z, g, b):
    mu = jnp.mean(z, axis=-1, keepdims=True)
    zc = z - mu
    rs = lax.rsqrt(jnp.mean(zc * zc, axis=-1, keepdims=True) + LN_EPS)
    xh = zc * rs
    return xh * g + b, xh, rs


def mm(a, b, mode, name, *, b_index=(), a_blocks=0, b_blocks=0, out_blocks=0, out_dtype=F32, deps=(), post_norm=None, plus=None):
    nlead = len(b_index) + (1 if b_blocks else 0)
    bk, bn = b.shape[nlead:]
    M, K = (a.shape[-1], a.shape[-2]) if mode == "tn" else (a.shape[-2], a.shape[-1] * max(a_blocks, 1))
    N = bk if mode == "nt" else bn * max(b_blocks, 1) if mode == "nn" or mode == "tn" else bn
    asz, bsz, osz = a.dtype.itemsize, b.dtype.itemsize, jnp.dtype(out_dtype).itemsize
    n_unit = math.gcd(N // max(out_blocks, 1), N // max(b_blocks, 1) if mode != "nt" else N)
    k_unit = math.gcd(K // max(a_blocks, 1), K // max(b_blocks, 1) if mode == "nt" else K)
    tms = _divisors(M, LANES if mode == "tn" else SUBLANES, 2048)
    tns = [N] if post_norm else _divisors(n_unit, LANES, 2048)
    tks = _divisors(k_unit, LANES, k_unit)
    best = None
    for tm in tms:
        for tn in tns:
            for tk in tks:
                nk = K // tk
                scratch = tm * tn * 4 if (nk > 1 and osz != 4) else 0
                blocks = tm * tk * asz + tn * tk * bsz + tm * tn * osz * (3 if post_norm else 1)
                temps = tm * tk * (2 + (4 if mode == "tn" else 0)) + tn * tk * 2 + tm * tn * 4 + scratch
                if 2 * blocks + temps > VMEM_BLOCK_BUDGET + (8 << 20):
                    continue
                ni, nj = M // tm, N // tn
                traffic = M * K * asz * (nj if nk > 1 else 1) + N * K * bsz * (1 if nj * nk == 1 else ni) + M * N * osz
                busy = max(traffic / HBM_BYTES_PER_US, 2.0 * M * N * K / MXU_FLOPS_PER_US)
                cost = ni * nj * nk * STEP_US + busy + blocks / HBM_BYTES_PER_US
                if best is None or cost < best[0]:
                    best = (cost, tm, tn, tk, blocks, temps)
    _, tm, tn, tk, blocks, temps = best
    nk = K // tk
    use_scratch = nk > 1 and osz != 4

    def split(index, total, blocks, tile):
        per = total // blocks // tile
        return index // per, index % per

    def body(a_ref, b_ref, *rest):
        rest = rest[len(deps):]
        if post_norm:
            h_ref, g_ref, beta_ref, o_ref, xh_ref, rs_ref = rest[:6]
            acc = rest[6:]
        elif plus:
            plus_ref, o_ref, acc = rest[0], rest[1], rest[2:]
        else:
            o_ref, acc = rest[0], rest[1:]
        av = a_ref[...].astype(MXU_DTYPE)
        bv = b_ref[...].astype(MXU_DTYPE)
        dn = {"nn": (((1,), (0,)), ((), ())), "nt": (((1,), (1,)), ((), ())), "tn": (((0,), (0,)), ((), ()))}[mode]
        r = lax.dot_general(av, bv, dn, preferred_element_type=F32)

        def normalise(f):
            o_ref[...], xh_ref[...], rs_ref[...] = _layer_norm(post_norm[3] * h_ref[...] + f, g_ref[...], beta_ref[...])

        if nk == 1 and post_norm:
            normalise(r)
        elif nk == 1 and plus:
            o_ref[...] = plus[1] * plus_ref[...] + r
        elif nk == 1:
            o_ref[...] = r.astype(o_ref.dtype)
        else:
            acc_ref = acc[0] if use_scratch else o_ref

            @pl.when(pl.program_id(2) == 0)
            def _():
                acc_ref[...] = r

            @pl.when(pl.program_id(2) > 0)
            def _():
                acc_ref[...] += r

            if use_scratch:
                @pl.when(pl.program_id(2) == nk - 1)
                def _():
                    o_ref[...] = acc_ref[...].astype(o_ref.dtype)
            elif post_norm:
                @pl.when(pl.program_id(2) == nk - 1)
                def _():
                    normalise(o_ref[...])
            elif plus:
                @pl.when(pl.program_id(2) == nk - 1)
                def _():
                    o_ref[...] = plus[1] * plus_ref[...] + o_ref[...]

    if mode == "tn":
        a_spec = pl.BlockSpec((tk, tm), lambda i, j, k: (k, i))
    elif a_blocks:
        a_spec = pl.BlockSpec((None, tm, tk), lambda i, j, k: (split(k, K, a_blocks, tk)[0], i, split(k, K, a_blocks, tk)[1]))
    else:
        a_spec = pl.BlockSpec((tm, tk), lambda i, j, k: (i, k))
    lead = (None,) * nlead
    if mode == "nt":
        bmap = ((lambda i, j, k: b_index + (split(k, K, b_blocks, tk)[0], j, split(k, K, b_blocks, tk)[1])) if b_blocks
                else (lambda i, j, k: b_index + (j, k)))
        b_spec = pl.BlockSpec(lead + (tn, tk), bmap)
    else:
        bmap = ((lambda i, j, k: b_index + (split(j, N, b_blocks, tn)[0], k, split(j, N, b_blocks, tn)[1])) if b_blocks
                else (lambda i, j, k: b_index + (k, j)))
        b_spec = pl.BlockSpec(lead + (tk, tn), bmap)
    if out_blocks:
        o_spec = pl.BlockSpec((None, tm, tn), lambda i, j, k: (split(j, N, out_blocks, tn)[0], i, split(j, N, out_blocks, tn)[1]))
        o_shape = jax.ShapeDtypeStruct((out_blocks, M, N // out_blocks), out_dtype)
    else:
        o_spec = pl.BlockSpec((tm, tn), lambda i, j, k: (i, j))
        o_shape = jax.ShapeDtypeStruct((M, N), out_dtype)
    in_specs, extra = [a_spec, b_spec] + [pl.BlockSpec(memory_space=pl.ANY)] * len(deps), ()
    if post_norm:
        vec = pl.BlockSpec((1, N), lambda i, j, k: (0, 0))
        in_specs += [pl.BlockSpec((tm, N), lambda i, j, k: (i, 0)), vec, vec]
        o_spec = [o_spec, pl.BlockSpec((tm, N), lambda i, j, k: (i, 0)), pl.BlockSpec((tm, 1), lambda i, j, k: (i, 0))]
        o_shape = [o_shape, jax.ShapeDtypeStruct((M, N), F32), jax.ShapeDtypeStruct((M, 1), F32)]
        extra = post_norm[:3]
    elif plus:
        in_specs += [pl.BlockSpec((tm, tn), lambda i, j, k: (i, j))]
        extra = plus[:1]
    return hbm_call(
        body, name=name, grid=(M // tm, N // tn, nk), in_specs=in_specs, out_specs=o_spec, out_shape=o_shape,
        scratch_shapes=[pltpu.VMEM((tm, tn), F32)] if use_scratch else [],
        compiler_params=_params(("parallel", "parallel", "arbitrary"), _vmem_limit(blocks, temps)),
    )(a, b, *deps, *extra)


ROW_TILE = 512
GATE_ROWS = 1024


def ln_bwd(dy_a, dy_b, xh, rs, g, c1, name):
    S, D = xh.shape
    tr = min(ROW_TILE, S)
    two = dy_b is not None

    def body(*refs):
        if two:
            a_ref, b_ref, xh_ref, rs_ref, g_ref, dz_ref, dg_ref, db_ref = refs
            dy = c1 * a_ref[...] + b_ref[...]
        else:
            a_ref, xh_ref, rs_ref, g_ref, dz_ref, dg_ref, db_ref = refs
            dy = a_ref[...]
        x = xh_ref[...]
        dyg = dy * g_ref[...]
        m1 = jnp.mean(dyg, axis=-1, keepdims=True)
        m2 = jnp.mean(dyg * x, axis=-1, keepdims=True)
        dz_ref[...] = rs_ref[...] * (dyg - m1 - x * m2)

        @pl.when(pl.program_id(0) == 0)
        def _():
            dg_ref[...] = jnp.zeros_like(dg_ref)
            db_ref[...] = jnp.zeros_like(db_ref)

        dg_ref[...] += jnp.sum(dy * x, axis=0, keepdims=True)
        db_ref[...] += jnp.sum(dy, axis=0, keepdims=True)

    row = pl.BlockSpec((tr, D), lambda i: (i, 0))
    vec = pl.BlockSpec((1, D), lambda i: (0, 0))
    ins = [row, row] if two else [row]
    args = (dy_a, dy_b) if two else (dy_a,)
    return hbm_call(
        body, name=name, grid=(S // tr,), in_specs=ins + [row, pl.BlockSpec((tr, 1), lambda i: (i, 0)), vec],
        out_specs=[row, vec, vec],
        out_shape=[jax.ShapeDtypeStruct((S, D), F32), jax.ShapeDtypeStruct((1, D), F32), jax.ShapeDtypeStruct((1, D), F32)],
        compiler_params=_params(("arbitrary",), 48 << 20),
    )(*args, xh, rs, g)


def loss_head(y, t, name):
    S, D = y.shape
    tr = min(ROW_TILE, S)
    nsteps = S // tr

    def body(y_ref, t_ref, dy_ref, l_ref, acc_ref):
        i = pl.program_id(0)

        @pl.when(i == 0)
        def _():
            acc_ref[...] = jnp.zeros_like(acc_ref)

        e = y_ref[...] - t_ref[...]
        dy_ref[...] = e * (1.0 / D)
        acc_ref[...] += jnp.sum(e * e, axis=0, keepdims=True)

        @pl.when(i == nsteps - 1)
        def _():
            l_ref[...] = jnp.sum(acc_ref[...], axis=1, keepdims=True) * (0.5 / D)

    row = pl.BlockSpec((tr, D), lambda i: (i, 0))
    return hbm_call(
        body, name=name, grid=(nsteps,), in_specs=[row, row],
        out_specs=[row, pl.BlockSpec((1, 1), lambda i: (0, 0))],
        out_shape=[jax.ShapeDtypeStruct((S, D), F32), jax.ShapeDtypeStruct((1, 1), F32)],
        scratch_shapes=[pltpu.VMEM((1, D), F32)], compiler_params=_params(("arbitrary",)),
    )(y, t)


SWIGLU_ROWS = 256


def swiglu_fwd(gu, name):
    _, S, Fh = gu.shape
    tc = _divisors(Fh, LANES, 1536)[0]
    tr = min(SWIGLU_ROWS, S)

    def body(gu_ref, o_ref):
        g = gu_ref[0]
        o_ref[...] = (g * _sigmoid(g) * gu_ref[1]).astype(o_ref.dtype)

    return hbm_call(
        body, name=name, grid=(S // tr, Fh // tc), in_specs=[pl.BlockSpec((2, tr, tc), lambda i, j: (0, i, j))],
        out_specs=pl.BlockSpec((tr, tc), lambda i, j: (i, j)), out_shape=jax.ShapeDtypeStruct((S, Fh), MXU_DTYPE),
        compiler_params=_params(("parallel", "parallel")),
    )(gu)


def swiglu_bwd(gu, dact, name):
    _, S, Fh = gu.shape
    tc = _divisors(Fh, LANES, 1536)[0]
    tr = min(SWIGLU_ROWS, S)

    def body(gu_ref, d_ref, o_ref):
        g, u, d = gu_ref[0], gu_ref[1], d_ref[...]
        s = _sigmoid(g)
        o_ref[0] = (d * u * (s * (1.0 + g * (1.0 - s)))).astype(o_ref.dtype)
        o_ref[1] = (d * (g * s)).astype(o_ref.dtype)

    both = pl.BlockSpec((2, tr, tc), lambda i, j: (0, i, j))
    return hbm_call(
        body, name=name, grid=(S // tr, Fh // tc), in_specs=[both, pl.BlockSpec((tr, tc), lambda i, j: (i, j))],
        out_specs=both, out_shape=jax.ShapeDtypeStruct((2, S, Fh), MXU_DTYPE), compiler_params=_params(("parallel", "parallel")),
    )(gu, dact)


GATE_COLS = 256


def merge_fwd(proj, pr, pa, name):
    S, D = pr.shape
    tr = min(GATE_ROWS, S)
    c0 = (3 * D + 2 * N_KV_HEADS * HEAD_DIM) // GATE_COLS
    c1 = c0 + D // GATE_COLS

    def body(gr_ref, ga_ref, pr_ref, pa_ref, o_ref):
        o_ref[...] = (_sigmoid(gr_ref[...]) * pr_ref[...] + _sigmoid(ga_ref[...]) * pa_ref[...]).astype(o_ref.dtype)

    blk = pl.BlockSpec((tr, GATE_COLS), lambda i, j: (i, j))
    return hbm_call(
        body, name=name, grid=(S // tr, D // GATE_COLS),
        in_specs=[pl.BlockSpec((tr, GATE_COLS), lambda i, j: (i, c0 + j)), pl.BlockSpec((tr, GATE_COLS), lambda i, j: (i, c1 + j)),
                  blk, blk],
        out_specs=blk, out_shape=jax.ShapeDtypeStruct((S, D), MXU_DTYPE), compiler_params=_params(("parallel", "parallel")),
    )(proj, proj, pr, pa)


def merge_bwd(proj, pr, pa, dm, name):
    S, D = pr.shape
    tr = min(GATE_ROWS, S)
    c0 = (3 * D + 2 * N_KV_HEADS * HEAD_DIM) // GATE_COLS
    c1 = c0 + D // GATE_COLS

    def body(gr_ref, ga_ref, pr_ref, pa_ref, dm_ref, dpr_ref, dpa_ref, dgr_ref, dga_ref):
        sr, sa, d = _sigmoid(gr_ref[...]), _sigmoid(ga_ref[...]), dm_ref[...]
        dpr_ref[...] = (d * sr).astype(dpr_ref.dtype)
        dpa_ref[...] = (d * sa).astype(dpa_ref.dtype)
        dgr_ref[...] = (d * pr_ref[...] * (sr * (1.0 - sr))).astype(dgr_ref.dtype)
        dga_ref[...] = (d * pa_ref[...] * (sa * (1.0 - sa))).astype(dga_ref.dtype)

    blk = pl.BlockSpec((tr, GATE_COLS), lambda i, j: (i, j))
    sds = jax.ShapeDtypeStruct((S, D), MXU_DTYPE)
    return hbm_call(
        body, name=name, grid=(S // tr, D // GATE_COLS),
        in_specs=[pl.BlockSpec((tr, GATE_COLS), lambda i, j: (i, c0 + j)), pl.BlockSpec((tr, GATE_COLS), lambda i, j: (i, c1 + j)),
                  blk, blk, blk],
        out_specs=[blk, blk, blk, blk], out_shape=[sds, sds, sds, sds], compiler_params=_params(("parallel", "parallel")),
    )(proj, proj, pr, pa, dm)


RG_ROWS = 512


def _shift_down(cur, prev, d, row, first):
    halo = jnp.where(first, 0.0, pltpu.roll(prev, d, 0))
    return jnp.where(row >= d, pltpu.roll(cur, d, 0), halo)


def _shift_up(cur, nxt, d, row, last, tr):
    halo = jnp.where(last, 0.0, pltpu.roll(nxt, tr - d, 0))
    return jnp.where(row < tr - d, pltpu.roll(cur, tr - d, 0), halo)


def _lru_coeffs(r, lam):
    sp = _softplus_neg(lam)
    la = -LRU_C * r * sp
    return sp, la, jnp.exp(la), _neg_expm1(2.0 * la)


def rg_gates_fwd(proj, conv_w, conv_b, w_rg, b_rg, w_ig, b_ig, lam, name):
    S = proj.shape[0]
    nblk, bw, _ = w_rg.shape
    D = nblk * bw
    tr = min(RG_ROWS, S)

    def body(xr_ref, xp_ref, cw_ref, cb_ref, wr_ref, br_ref, wi_ref, bi_ref, lam_ref, xc_ref, r_ref, i_ref, a_ref, b_ref):
        first = pl.program_id(1) == 0
        cur, prev = xr_ref[...], xp_ref[...]
        row = lax.broadcasted_iota(jnp.int32, cur.shape, 0)
        xc = cb_ref[...]
        for k in range(CONV_WIDTH - 1):
            xc = xc + _shift_down(cur, prev, CONV_WIDTH - 1 - k, row, first) * cw_ref[k:k + 1, :]
        xc = xc + cur * cw_ref[CONV_WIDTH - 1:CONV_WIDTH, :]
        xm = xc.astype(MXU_DTYPE)
        r = _sigmoid(jnp.dot(xm, wr_ref[...].astype(MXU_DTYPE), preferred_element_type=F32) + br_ref[...])
        ig = _sigmoid(jnp.dot(xm, wi_ref[...].astype(MXU_DTYPE), preferred_element_type=F32) + bi_ref[...])
        _, _, a, em = _lru_coeffs(r, lam_ref[...])
        xc_ref[...] = xc
        r_ref[...] = r
        i_ref[...] = ig
        a_ref[...] = a
        b_ref[...] = jnp.sqrt(em) * (ig * xc)

    tile = pl.BlockSpec((tr, bw), lambda n, i: (i, n))
    vec = pl.BlockSpec((1, bw), lambda n, i: (0, n))
    wblk = pl.BlockSpec((None, bw, bw), lambda n, i: (n, 0, 0))
    sds = jax.ShapeDtypeStruct((S, D), F32)
    return hbm_call(
        body, name=name, grid=(nblk, S // tr),
        in_specs=[tile, pl.BlockSpec((tr, bw), lambda n, i: (jnp.maximum(i - 1, 0), n)),
                  pl.BlockSpec((CONV_WIDTH, bw), lambda n, i: (0, n)), vec, wblk, vec, wblk, vec, vec],
        out_specs=[tile] * 5, out_shape=[sds] * 5, compiler_params=_params(("parallel", "parallel")),
    )(proj, proj, conv_w, conv_b, w_rg, b_rg, w_ig, b_ig, lam)


SCAN_COLS = 256
CHUNK = SUBLANES
SCAN_UNROLL = 8


def rg_scan_fwd(proj, a, b, name):
    S, D = a.shape
    cb = min(SCAN_COLS, D)
    goff = D // cb

    def body(a_ref, b_ref, g_ref, hs_ref, y_ref):
        row = lax.broadcasted_iota(jnp.int32, (CHUNK, cb), 0)

        def step(c, carry):
            r0 = pl.multiple_of(c * CHUNK, CHUNK)
            A = a_ref[pl.ds(r0, CHUNK), :]
            B = b_ref[pl.ds(r0, CHUNK), :]
            for d in (1, 2, 4):
                As = jnp.where(row >= d, pltpu.roll(A, d, 0), 1.0)
                Bs = jnp.where(row >= d, pltpu.roll(B, d, 0), 0.0)
                B = A * Bs + B
                A = A * As
            hs_ref[pl.ds(r0, CHUNK), :] = B + A * carry
            a_end = jnp.sum(jnp.where(row == CHUNK - 1, A, 0.0), axis=0, keepdims=True)
            b_end = jnp.sum(jnp.where(row == CHUNK - 1, B, 0.0), axis=0, keepdims=True)
            return b_end + a_end * carry

        lax.fori_loop(0, S // CHUNK, step, jnp.zeros((1, cb), F32), unroll=SCAN_UNROLL)
        y_ref[...] = (hs_ref[...] * _gelu(g_ref[...])).astype(y_ref.dtype)

    col = pl.BlockSpec((S, cb), lambda j: (0, j))
    return hbm_call(
        body, name=name, grid=(D // cb,), in_specs=[col, col, pl.BlockSpec((S, cb), lambda j: (0, goff + j))],
        out_specs=[col, col], out_shape=[jax.ShapeDtypeStruct((S, D), F32), jax.ShapeDtypeStruct((S, D), MXU_DTYPE)],
        compiler_params=_params(("parallel",), _vmem_limit(5 * S * cb * 4, 4 * S * cb * 4)),
    )(a, b, proj)


def rg_scan_bwd(proj, dy, hs, a, name):
    S, D = a.shape
    cb = min(SCAN_COLS, D)
    goff = D // cb
    nchunks = S // CHUNK

    def body(g_ref, dy_ref, hs_ref, a_ref, dg_ref, gt_ref):
        gate, dy = g_ref[...], dy_ref[...]
        dg_ref[...] = (dy * hs_ref[...] * _gelu_grad(gate)).astype(dg_ref.dtype)
        gt_ref[...] = dy * _gelu(gate)
        row = lax.broadcasted_iota(jnp.int32, (CHUNK, cb), 0)

        def step(k, carry):
            c = nchunks - 1 - k
            r0 = pl.multiple_of(c * CHUNK, CHUNK)
            rn = pl.multiple_of(jnp.minimum(c + 1, nchunks - 1) * CHUNK, CHUNK)
            last = c == nchunks - 1
            nxt = jnp.where(last, 0.0, pltpu.roll(a_ref[pl.ds(rn, CHUNK), :], CHUNK - 1, 0))
            A = jnp.where(row < CHUNK - 1, pltpu.roll(a_ref[pl.ds(r0, CHUNK), :], CHUNK - 1, 0), nxt)
            B = gt_ref[pl.ds(r0, CHUNK), :]
            for d in (1, 2, 4):
                As = jnp.where(row < CHUNK - d, pltpu.roll(A, CHUNK - d, 0), 1.0)
                Bs = jnp.where(row < CHUNK - d, pltpu.roll(B, CHUNK - d, 0), 0.0)
                B = A * Bs + B
                A = A * As
            gt_ref[pl.ds(r0, CHUNK), :] = B + A * carry
            a_end = jnp.sum(jnp.where(row == 0, A, 0.0), axis=0, keepdims=True)
            b_end = jnp.sum(jnp.where(row == 0, B, 0.0), axis=0, keepdims=True)
            return b_end + a_end * carry

        lax.fori_loop(0, nchunks, step, jnp.zeros((1, cb), F32), unroll=SCAN_UNROLL)

    col = pl.BlockSpec((S, cb), lambda j: (0, j))
    return hbm_call(
        body, name=name, grid=(D // cb,), in_specs=[pl.BlockSpec((S, cb), lambda j: (0, goff + j)), col, col, col],
        out_specs=[col, col], out_shape=[jax.ShapeDtypeStruct((S, D), MXU_DTYPE), jax.ShapeDtypeStruct((S, D), F32)],
        compiler_params=_params(("parallel",), _vmem_limit(6 * S * cb * 4, 6 * S * cb * 4)),
    )(proj, dy, hs, a)


def rg_gates_bwd(gt, hs, xc, r, ig, w_rg, w_ig, lam, name):
    S, D = xc.shape
    nblk, bw, _ = w_rg.shape
    tr = min(RG_ROWS, S)

    def body(gt_ref, hs_ref, hp_ref, xc_ref, r_ref, i_ref, wr_ref, wi_ref, lam_ref,
             dxc_ref, dwr_ref, dwi_ref, dbr_ref, dbi_ref, dl_ref):
        step = pl.program_id(1)
        g, hs, xc, r, ig, lam = gt_ref[...], hs_ref[...], xc_ref[...], r_ref[...], i_ref[...], lam_ref[...]
        row = lax.broadcasted_iota(jnp.int32, g.shape, 0)
        hprev = _shift_down(hs, hp_ref[...], 1, row, step == 0)
        sp, _, a, em = _lru_coeffs(r, lam)
        mult = jnp.sqrt(em)
        du = g * mult
        dla = g * hprev * a - (g * (ig * xc)) * (a * a) / mult
        dpr = (dla * (-LRU_C * sp)) * (r * (1.0 - r))
        dpi = (du * xc) * (ig * (1.0 - ig))
        dprm, dpim = dpr.astype(MXU_DTYPE), dpi.astype(MXU_DTYPE)
        nt = (((1,), (1,)), ((), ()))
        dxc_ref[...] = (du * ig + lax.dot_general(dprm, wr_ref[...].astype(MXU_DTYPE), nt, preferred_element_type=F32)
                        + lax.dot_general(dpim, wi_ref[...].astype(MXU_DTYPE), nt, preferred_element_type=F32))

        @pl.when(step == 0)
        def _():
            for ref in (dwr_ref, dwi_ref, dbr_ref, dbi_ref, dl_ref):
                ref[...] = jnp.zeros_like(ref)

        xct = xc.T.astype(MXU_DTYPE)
        dwr_ref[...] += jnp.dot(xct, dprm, preferred_element_type=F32)
        dwi_ref[...] += jnp.dot(xct, dpim, preferred_element_type=F32)
        dbr_ref[...] += jnp.sum(dpr, axis=0, keepdims=True)
        dbi_ref[...] += jnp.sum(dpi, axis=0, keepdims=True)
        dl_ref[...] += jnp.sum(dla * (-LRU_C * r), axis=0, keepdims=True) * (-_sigmoid(-lam))

    tile = pl.BlockSpec((tr, bw), lambda n, i: (i, n))
    vec = pl.BlockSpec((1, bw), lambda n, i: (0, n))
    wblk = pl.BlockSpec((None, bw, bw), lambda n, i: (n, 0, 0))
    return hbm_call(
        body, name=name, grid=(nblk, S // tr),
        in_specs=[tile, tile, pl.BlockSpec((tr, bw), lambda n, i: (jnp.maximum(i - 1, 0), n)), tile, tile, tile, wblk, wblk, vec],
        out_specs=[tile, wblk, wblk, vec, vec, vec],
        out_shape=[jax.ShapeDtypeStruct((S, D), F32), jax.ShapeDtypeStruct((nblk, bw, bw), F32), jax.ShapeDtypeStruct((nblk, bw, bw), F32),
                   jax.ShapeDtypeStruct((1, D), F32), jax.ShapeDtypeStruct((1, D), F32), jax.ShapeDtypeStruct((1, D), F32)],
        compiler_params=_params(("parallel", "arbitrary")),
    )(gt, hs, hs, xc, r, ig, w_rg, w_ig, lam)


def rg_conv_bwd(proj, dxc, conv_w, name):
    S, D = dxc.shape
    bw = min(SCAN_COLS, D)
    tr = min(RG_ROWS, S)
    nsteps = S // tr

    def body(d_ref, dn_ref, xr_ref, xp_ref, cw_ref, dxr_ref, dcw_ref, dcb_ref):
        step = pl.program_id(1)
        d, xr = d_ref[...], xr_ref[...]
        row = lax.broadcasted_iota(jnp.int32, d.shape, 0)
        dxr = d * cw_ref[CONV_WIDTH - 1:CONV_WIDTH, :]
        for k in range(CONV_WIDTH - 1):
            dxr = dxr + _shift_up(d, dn_ref[...], CONV_WIDTH - 1 - k, row, step == nsteps - 1, tr) * cw_ref[k:k + 1, :]
        dxr_ref[...] = dxr.astype(dxr_ref.dtype)

        @pl.when(step == 0)
        def _():
            dcw_ref[...] = jnp.zeros_like(dcw_ref)
            dcb_ref[...] = jnp.zeros_like(dcb_ref)

        for k in range(CONV_WIDTH - 1):
            xs = _shift_down(xr, xp_ref[...], CONV_WIDTH - 1 - k, row, step == 0)
            dcw_ref[k:k + 1, :] += jnp.sum(d * xs, axis=0, keepdims=True)
        dcw_ref[CONV_WIDTH - 1:CONV_WIDTH, :] += jnp.sum(d * xr, axis=0, keepdims=True)
        dcb_ref[...] += jnp.sum(d, axis=0, keepdims=True)

    tile = pl.BlockSpec((tr, bw), lambda n, i: (i, n))
    cwb = pl.BlockSpec((CONV_WIDTH, bw), lambda n, i: (0, n))
    return hbm_call(
        body, name=name, grid=(D // bw, nsteps),
        in_specs=[tile, pl.BlockSpec((tr, bw), lambda n, i: (jnp.minimum(i + 1, nsteps - 1), n)), tile,
                  pl.BlockSpec((tr, bw), lambda n, i: (jnp.maximum(i - 1, 0), n)), cwb],
        out_specs=[tile, cwb, pl.BlockSpec((1, bw), lambda n, i: (0, n))],
        out_shape=[jax.ShapeDtypeStruct((S, D), MXU_DTYPE), jax.ShapeDtypeStruct((CONV_WIDTH, D), F32), jax.ShapeDtypeStruct((1, D), F32)],
        compiler_params=_params(("parallel", "arbitrary")),
    )(dxc, dxc, proj, proj, conv_w)


def rope_table(S):
    half = ROT_DIM // 2
    pos = jnp.arange(S, dtype=F32)
    inv = ROPE_THETA ** (-jnp.arange(0, ROT_DIM, 2, dtype=F32) / ROT_DIM)
    ang = pos[:, None] * inv[None, :]
    cos, sin = jnp.cos(ang), jnp.sin(ang)
    zero = jnp.zeros((S, HEAD_DIM - ROT_DIM), F32)
    c = jnp.concatenate([cos, cos, zero + 1.0], axis=1)
    a = jnp.concatenate([-sin, jnp.zeros((S, half), F32), zero], axis=1)
    b = jnp.concatenate([jnp.zeros((S, half), F32), sin, zero], axis=1)
    return jnp.stack([jnp.tile(t, (1, LANES // HEAD_DIM)) for t in (c, a, b)])


def _rope(t, tab):
    half = ROT_DIM // 2
    return t * tab[0] + pltpu.roll(t, LANES - half, 1) * tab[1] + pltpu.roll(t, half, 1) * tab[2]


def _rope_t(d, tab):
    half = ROT_DIM // 2
    return d * tab[0] + pltpu.roll(d * tab[1], half, 1) + pltpu.roll(d * tab[2], LANES - half, 1)


def _dup_head(t, hk, lo):
    sw = pltpu.roll(t, HEAD_DIM, 1)
    return jnp.where(lo, t, sw) if hk == 0 else jnp.where(lo, sw, t)


def _attn_common(n, sink_ref, q_ref, kp_ref, kc_ref, vp_ref, vc_ref, tc_ref, tp_ref, hk, pairs):
    tq = (tc_ref[0], tc_ref[1], tc_ref[2])
    tp = (tp_ref[0], tp_ref[1], tp_ref[2])
    lo = lax.broadcasted_iota(jnp.int32, (WINDOW, LANES), 1) < HEAD_DIM
    lo2 = lax.broadcasted_iota(jnp.int32, (2 * WINDOW, LANES), 1) < HEAD_DIM
    kband = jnp.concatenate([_rope(kp_ref[...], tp), _rope(kc_ref[...], tq)], axis=0)
    vband = jnp.concatenate([vp_ref[...], vc_ref[...]], axis=0)
    kd = _dup_head(kband, hk, lo2).astype(MXU_DTYPE)
    vd = _dup_head(vband, hk, lo2).astype(MXU_DTYPE)
    rows, sks = [], []
    for j in range(pairs):
        col = hk * pairs + j
        qp = _rope(q_ref[:, col * LANES:(col + 1) * LANES], tq)
        rows += [jnp.where(lo, qp, 0.0), jnp.where(lo, 0.0, qp)]
        sks += [jnp.full((WINDOW, 1), sink_ref[2 * col], F32), jnp.full((WINDOW, 1), sink_ref[2 * col + 1], F32)]
    qg = jnp.concatenate(rows, axis=0)
    sk = jnp.concatenate(sks, axis=0)
    G = 2 * pairs * WINDOW
    own = lax.broadcasted_iota(jnp.int32, (G, WINDOW), 1) <= (lax.broadcasted_iota(jnp.int32, (G, WINDOW), 0) & (WINDOW - 1))
    s = lax.dot_general(qg.astype(MXU_DTYPE), kd, (((1,), (1,)), ((), ())), preferred_element_type=F32) * (HEAD_DIM ** -0.5)
    s = jnp.where(own, s[:, WINDOW:], s[:, :WINDOW] + jnp.where(n > 0, 0.0, NEG_INF))
    m = jnp.maximum(jnp.max(s, axis=1, keepdims=True), sk)
    e = jnp.exp(s - m)
    es = jnp.exp(sk - m)
    inv = 1.0 / (jnp.sum(e, axis=1, keepdims=True) + es)
    return qg, kd, vd, e * inv, es * inv, own, lo, lo2, tq, tp


def _unfold_band(t, own):
    return jnp.concatenate([jnp.where(own, 0.0, t), jnp.where(own, t, 0.0)], axis=1)


def _attn_specs(D, NB):
    kcol = 3 * D // LANES
    q = pl.BlockSpec((WINDOW, D), lambda n: (n, 2))
    kc = pl.BlockSpec((WINDOW, LANES), lambda n: (n, kcol))
    kp = pl.BlockSpec((WINDOW, LANES), lambda n: (jnp.maximum(n - 1, 0), kcol))
    vc = pl.BlockSpec((WINDOW, LANES), lambda n: (n, kcol + 1))
    vp = pl.BlockSpec((WINDOW, LANES), lambda n: (jnp.maximum(n - 1, 0), kcol + 1))
    tc = pl.BlockSpec((3, WINDOW, LANES), lambda n: (0, n, 0))
    tp = pl.BlockSpec((3, WINDOW, LANES), lambda n: (0, jnp.maximum(n - 1, 0), 0))
    sink = pl.BlockSpec(memory_space=pltpu.SMEM)
    return [sink, q, kp, kc, vp, vc, tc, tp]


def attn_fwd(proj, sinks, tab, D, name):
    S = proj.shape[0]
    NB = S // WINDOW
    pairs = D // HEAD_DIM // N_KV_HEADS // 2

    def body(sink_ref, q_ref, kp_ref, kc_ref, vp_ref, vc_ref, tc_ref, tp_ref, o_ref):
        n = pl.program_id(0)
        for hk in range(N_KV_HEADS):
            _, _, vd, p, _, own, lo, _, _, _ = _attn_common(n, sink_ref, q_ref, kp_ref, kc_ref, vp_ref, vc_ref, tc_ref, tp_ref, hk, pairs)
            o = jnp.dot(_unfold_band(p, own).astype(MXU_DTYPE), vd, preferred_element_type=F32)
            for j in range(pairs):
                col = hk * pairs + j
                oa = o[(2 * j) * WINDOW:(2 * j + 1) * WINDOW]
                ob = o[(2 * j + 1) * WINDOW:(2 * j + 2) * WINDOW]
                o_ref[:, col * LANES:(col + 1) * LANES] = jnp.where(lo, oa, ob)

    return hbm_call(
        body, name=name, grid=(NB,), in_specs=_attn_specs(D, NB),
        out_specs=pl.BlockSpec((WINDOW, D), lambda n: (n, 0)), out_shape=jax.ShapeDtypeStruct((S, D), F32),
        compiler_params=_params(("parallel",)),
    )(sinks, proj, proj, proj, proj, proj, tab, tab)


def attn_bwd(proj, sinks, tab, o, do, D, name):
    S = proj.shape[0]
    NB = S // WINDOW
    pairs = D // HEAD_DIM // N_KV_HEADS // 2

    def body(sink_ref, q_ref, kp_ref, kc_ref, vp_ref, vc_ref, tc_ref, tp_ref, o_ref, do_ref, dq_ref, dk_ref, dv_ref, ds_ref):
        n = pl.program_id(0)

        @pl.when(n == 0)
        def _():
            ds_ref[...] = jnp.zeros_like(ds_ref)

        lane1 = lax.broadcasted_iota(jnp.int32, (1, LANES), 1)
        dsink = jnp.zeros((1, LANES), F32)
        dkt = dvt = None
        for hk in range(N_KV_HEADS):
            qg, kd, vd, p, ps, own, lo, lo2, tq, tp = _attn_common(n, sink_ref, q_ref, kp_ref, kc_ref, vp_ref, vc_ref, tc_ref, tp_ref, hk, pairs)
            dos, os_ = [], []
            for j in range(pairs):
                col = hk * pairs + j
                dop = do_ref[:, col * LANES:(col + 1) * LANES]
                op = o_ref[:, col * LANES:(col + 1) * LANES]
                dos += [jnp.where(lo, dop, 0.0), jnp.where(lo, 0.0, dop)]
                os_ += [jnp.where(lo, op, 0.0), jnp.where(lo, 0.0, op)]
            dog = jnp.concatenate(dos, axis=0)
            og = jnp.concatenate(os_, axis=0)
            dogm = dog.astype(MXU_DTYPE)
            dp = lax.dot_general(dogm, vd, (((1,), (1,)), ((), ())), preferred_element_type=F32)
            dp = jnp.where(own, dp[:, WINDOW:], dp[:, :WINDOW])
            dr = jnp.sum(dog * og, axis=1, keepdims=True)
            ds = _unfold_band(p * (dp - dr) * (HEAD_DIM ** -0.5), own)
            dsm = ds.astype(MXU_DTYPE)
            dqg = jnp.dot(dsm, kd, preferred_element_type=F32)
            dkd = jnp.dot(ds.T.astype(MXU_DTYPE), qg.astype(MXU_DTYPE), preferred_element_type=F32)
            dvd = jnp.dot(_unfold_band(p, own).T.astype(MXU_DTYPE), dogm, preferred_element_type=F32)
            dkf = dkd + pltpu.roll(dkd, HEAD_DIM, 1)
            dvf = dvd + pltpu.roll(dvd, HEAD_DIM, 1)
            if hk == 0:
                dkt, dvt = dkf, dvf
            else:
                dkt, dvt = jnp.where(lo2, dkt, dkf), jnp.where(lo2, dvt, dvf)
            sd = ps * dr
            for j in range(pairs):
                col = hk * pairs + j
                dqa = dqg[(2 * j) * WINDOW:(2 * j + 1) * WINDOW]
                dqb = dqg[(2 * j + 1) * WINDOW:(2 * j + 2) * WINDOW]
                dq_ref[:, col * LANES:(col + 1) * LANES] = _rope_t(jnp.where(lo, dqa, dqb), tq).astype(dq_ref.dtype)
                for t in range(2):
                    part = sd[(2 * j + t) * WINDOW:(2 * j + t + 1) * WINDOW]
                    val = jnp.sum(part, axis=0, keepdims=True)
                    dsink = dsink - jnp.where(lane1 == 2 * col + t, val, 0.0)
        dk_ref[...] = jnp.concatenate([_rope_t(dkt[:WINDOW], tp), _rope_t(dkt[WINDOW:], tq)], axis=0)
        dv_ref[...] = dvt
        ds_ref[...] += dsink

    blk = pl.BlockSpec((WINDOW, D), lambda n: (n, 0))
    band = pl.BlockSpec((None, 2 * WINDOW, LANES), lambda n: (n, 0, 0))
    return hbm_call(
        body, name=name, grid=(NB,), in_specs=_attn_specs(D, NB) + [blk, blk],
        out_specs=[blk, band, band, pl.BlockSpec((1, LANES), lambda n: (0, 0))],
        out_shape=[jax.ShapeDtypeStruct((S, D), MXU_DTYPE), jax.ShapeDtypeStruct((NB, 2 * WINDOW, LANES), F32),
                   jax.ShapeDtypeStruct((NB, 2 * WINDOW, LANES), F32), jax.ShapeDtypeStruct((1, LANES), F32)],
        compiler_params=_params(("arbitrary",)),
    )(sinks, proj, proj, proj, proj, proj, tab, tab, o, do)


def band_fold(dkb, dvb, name):
    NB = dkb.shape[0]
    k4 = dkb.reshape(NB, 2, WINDOW, LANES)
    v4 = dvb.reshape(NB, 2, WINDOW, LANES)

    def body(kc_ref, kn_ref, vc_ref, vn_ref, dk_ref, dv_ref):
        more = pl.program_id(0) < NB - 1
        dk_ref[...] = (kc_ref[...] + jnp.where(more, kn_ref[...], 0.0)).astype(dk_ref.dtype)
        dv_ref[...] = (vc_ref[...] + jnp.where(more, vn_ref[...], 0.0)).astype(dv_ref.dtype)

    cur = pl.BlockSpec((None, None, WINDOW, LANES), lambda n: (n, 1, 0, 0))
    nxt = pl.BlockSpec((None, None, WINDOW, LANES), lambda n: (jnp.minimum(n + 1, NB - 1), 0, 0, 0))
    out = pl.BlockSpec((WINDOW, LANES), lambda n: (n, 0))
    sds = jax.ShapeDtypeStruct((NB * WINDOW, LANES), MXU_DTYPE)
    return hbm_call(body, name=name, grid=(NB,), in_specs=[cur, nxt, cur, nxt], out_specs=[out, out], out_shape=[sds, sds],
                          compiler_params=_params(("parallel",)))(k4, k4, v4, v4)


CROSS_ROWS = 512


def _cross_probs(q, k, scale):
    s = lax.dot_general(q.astype(MXU_DTYPE), k.astype(MXU_DTYPE), (((1,), (1,)), ((), ())), preferred_element_type=F32) * scale
    e = jnp.exp(s - jnp.max(s, axis=1, keepdims=True))
    return e / jnp.sum(e, axis=1, keepdims=True)


def cross_fwd(qc, kv, name):
    S, D = qc.shape
    M = kv.shape[0]
    hd = D // CROSS_HEADS
    tq = min(CROSS_ROWS, S)

    def body(q_ref, kv_ref, o_ref):
        for h in range(CROSS_HEADS):
            p = _cross_probs(q_ref[:, h * hd:(h + 1) * hd], kv_ref[:, h * hd:(h + 1) * hd], hd ** -0.5)
            v = kv_ref[:, D + h * hd:D + (h + 1) * hd].astype(MXU_DTYPE)
            o_ref[:, h * hd:(h + 1) * hd] = jnp.dot(p.astype(MXU_DTYPE), v, preferred_element_type=F32).astype(o_ref.dtype)

    return hbm_call(
        body, name=name, grid=(S // tq,), in_specs=[pl.BlockSpec((tq, D), lambda i: (i, 0)), pl.BlockSpec((M, 2 * D), lambda i: (0, 0))],
        out_specs=pl.BlockSpec((tq, D), lambda i: (i, 0)), out_shape=jax.ShapeDtypeStruct((S, D), MXU_DTYPE),
        compiler_params=_params(("parallel",)),
    )(qc, kv)


def cross_bwd(qc, kv, do, name):
    S, D = qc.shape
    M = kv.shape[0]
    hd = D // CROSS_HEADS
    tq = min(CROSS_ROWS, S)

    def body(q_ref, kv_ref, do_ref, dq_ref, dkv_ref):
        @pl.when(pl.program_id(0) == 0)
        def _():
            dkv_ref[...] = jnp.zeros_like(dkv_ref)

        for h in range(CROSS_HEADS):
            q = q_ref[:, h * hd:(h + 1) * hd]
            k = kv_ref[:, h * hd:(h + 1) * hd]
            v = kv_ref[:, D + h * hd:D + (h + 1) * hd].astype(MXU_DTYPE)
            dom = do_ref[:, h * hd:(h + 1) * hd].astype(MXU_DTYPE)
            p = _cross_probs(q, k, hd ** -0.5)
            dp = lax.dot_general(dom, v, (((1,), (1,)), ((), ())), preferred_element_type=F32)
            ds = p * (dp - jnp.sum(p * dp, axis=1, keepdims=True)) * (hd ** -0.5)
            dq_ref[:, h * hd:(h + 1) * hd] = jnp.dot(ds.astype(MXU_DTYPE), k.astype(MXU_DTYPE),
                                                     preferred_element_type=F32).astype(dq_ref.dtype)
            dkv_ref[:, h * hd:(h + 1) * hd] += jnp.dot(ds.T.astype(MXU_DTYPE), q.astype(MXU_DTYPE), preferred_element_type=F32)
            dkv_ref[:, D + h * hd:D + (h + 1) * hd] += jnp.dot(p.T.astype(MXU_DTYPE), dom, preferred_element_type=F32)

    row = pl.BlockSpec((tq, D), lambda i: (i, 0))
    full = pl.BlockSpec((M, 2 * D), lambda i: (0, 0))
    return hbm_call(
        body, name=name, grid=(S // tq,), in_specs=[row, full, row], out_specs=[row, full],
        out_shape=[jax.ShapeDtypeStruct((S, D), MXU_DTYPE), jax.ShapeDtypeStruct((M, 2 * D), F32)],
        compiler_params=_params(("arbitrary",)),
    )(qc, kv, do)


def adamw(w, g, m, v, name, layers=None, into=None):
    shape = w.shape
    cols = shape[-1]
    lead = shape[0] if len(shape) > 2 else 1
    rows = int(np.prod(shape[:-1])) // lead
    w2, g2, m2, v2 = (t.reshape(lead, rows, cols) for t in (w, g, m, v))
    tr = _divisors(rows, SUBLANES, max(SUBLANES, (1 << 20) // (cols * 4) // SUBLANES * SUBLANES))[0]
    lo, hi = layers or (0, lead)
    done = [t.reshape(lead, rows, cols) for t in into] if into else []

    def body(w_ref, g_ref, m_ref, v_ref, *refs):
        d_ref, mo_ref, vo_ref, go_ref = refs[len(done):]
        gg = g_ref[...]
        mn = ADAM_B1 * m_ref[...] + (1.0 - ADAM_B1) * gg
        vn = ADAM_B2 * v_ref[...] + (1.0 - ADAM_B2) * (gg * gg)
        m_hat = mn / (1.0 - ADAM_B1 ** ADAM_STEP)
        v_hat = vn / (1.0 - ADAM_B2 ** ADAM_STEP)
        d_ref[...] = -ADAM_LR * (m_hat / (jnp.sqrt(v_hat) + ADAM_EPS) + ADAM_WD * w_ref[...])
        mo_ref[...] = mn
        vo_ref[...] = vn
        go_ref[...] = gg

    blk = pl.BlockSpec((None, tr, cols), lambda l, i: (l + lo, i, 0))
    sds = jax.ShapeDtypeStruct((lead, rows, cols), F32)
    d, mn, vn, go = hbm_call(body, name=name, grid=(hi - lo, rows // tr), in_specs=[blk] * 4 + [pl.BlockSpec(memory_space=pl.ANY)] * len(done),
                             out_specs=[blk] * 4, out_shape=[sds] * 4, input_output_aliases={4 + k: k for k in range(len(done))},
                             compiler_params=_params(("parallel", "parallel")))(w2, g2, m2, v2, *done)
    return d.reshape(shape), mn.reshape(shape), vn.reshape(shape), go.reshape(shape)


def sum_devices(parts, name):
    n, rows, cols = parts.shape

    def body(p_ref, o_ref):
        acc = p_ref[0]
        for k in range(1, n):
            acc = acc + p_ref[k]
        o_ref[...] = acc

    return pl.pallas_call(body, name=name, in_specs=[pl.BlockSpec(memory_space=pltpu.VMEM)],
                          out_specs=pl.BlockSpec(memory_space=pltpu.VMEM), out_shape=jax.ShapeDtypeStruct((rows, cols), F32))(parts)


HBM_SPEC = pl.BlockSpec(memory_space=pltpu.HBM)


def _place():
    return lax.axis_index("x"), lax.axis_index("y"), lax.axis_index("c")


def _remote(src, dst, send_sems, recv_sems, k, to):
    return pltpu.make_async_remote_copy(src_ref=src, dst_ref=dst, send_sem=send_sems.at[k], recv_sem=recv_sems.at[k],
                                        device_id=to, device_id_type=MESH_ID)


SEM_SPEC = pl.BlockSpec(memory_space=pltpu.SEMAPHORE)
ANY_SPEC = pl.BlockSpec(memory_space=pl.ANY)
SPLIT_COPY = pltpu.CompilerParams(has_side_effects=pltpu.SideEffectType.DATAFLOW_SIDE_EFFECTING)


def _in_hbm(arrays):
    return [pltpu.with_memory_space_constraint(a, pltpu.HBM) for a in arrays]


def _split_start(copies, sources, lands, after, n_sems, name):
    n = len(sources)

    def body(*refs):
        for cp in copies(refs[:n], refs[n:2 * n], refs[2 * n + 1], refs[2 * n + 2]):
            cp.start()
        refs[-1][...] = jnp.zeros_like(refs[-1])

    through = [pltpu.HBM(a.shape, a.dtype) for a in list(sources) + list(lands)]
    outs = pl.pallas_call(
        body, name=name, in_specs=[HBM_SPEC] * (2 * n) + [ANY_SPEC],
        out_specs=[SEM_SPEC, SEM_SPEC] + [HBM_SPEC] * (2 * n) + [pl.BlockSpec(memory_space=pltpu.VMEM)],
        out_shape=[pltpu.SemaphoreType.DMA((n_sems,)), pltpu.SemaphoreType.DMA((n_sems,))] + through
        + [jax.ShapeDtypeStruct((SUBLANES, LANES), F32)],
        input_output_aliases={i: 2 + i for i in range(2 * n)}, compiler_params=SPLIT_COPY,
    )(*_in_hbm(sources), *_in_hbm(lands), after)
    return outs[0], outs[1], outs[2:2 + n], outs[2 + n:2 + 2 * n], outs[-1]


def _split_wait(copies, send_sems, recv_sems, sources, lands, after, name):
    n = len(sources)

    def body(*refs):
        for cp in copies(refs[:n], refs[n:2 * n], refs[2 * n], refs[2 * n + 1]):
            cp.wait_send()
            cp.wait_recv()

    through = [pltpu.HBM(a.shape, a.dtype) for a in list(sources) + list(lands)]
    outs = pl.pallas_call(
        body, name=name, in_specs=[HBM_SPEC] * (2 * n) + [SEM_SPEC, SEM_SPEC, ANY_SPEC], out_specs=[HBM_SPEC] * (2 * n),
        out_shape=through, input_output_aliases={i: i for i in range(2 * n)}, compiler_params=SPLIT_COPY,
    )(*sources, *lands, send_sems, recv_sems, after)
    return outs[:n], outs[n:]


def _chip_slab(land, slot, rows):
    return land.at[slot, rows] if len(land.shape) == 3 else land.at[rows, slot]


def _gather_copies(w_refs, land_refs, send_sems, recv_sems):
    n = len(w_refs)
    x, y, c = _place()
    chips = [(1 - x, y), (x, 1 - y), (1 - x, 1 - y)]
    cps = []
    for a in range(n):
        hr = w_refs[a].shape[0] // 2
        mine, every = pl.ds(c * hr, hr), pl.ds(0, 2 * hr)
        cps.append(_remote(w_refs[a], _chip_slab(land_refs[a], 2 * x + y, every), send_sems, recv_sems, 3 * n + a, (x, y, 1 - c)))
        for k, chip in enumerate(chips):
            cps.append(_remote(w_refs[a].at[mine], _chip_slab(land_refs[a], 2 * x + y, mine), send_sems, recv_sems, 3 * a + k, (*chip, c)))
    return cps


def gather_start(shards, after, name):
    lands = [lax.empty(s.shape[:-2] + (N_CHIPS,) + s.shape[-2:], s.dtype) for s in shards]
    return _split_start(_gather_copies, shards, lands, after, 4 * len(shards), name)


def gather_wait(state, after, name):
    send_sems, recv_sems, sources, lands, _ = state
    return _split_wait(_gather_copies, send_sems, recv_sems, sources, lands, after, name)[1]


def gather_pass(lands, name):
    n = len(lands)

    def body(*refs):
        out_refs, send_sems, recv_sems = refs[n:2 * n], refs[2 * n], refs[2 * n + 1]
        x, y, c = _place()
        chips = [(1 - x, y), (x, 1 - y), (1 - x, 1 - y)]
        sent = []
        for a in range(n):
            hr = out_refs[a].shape[0 if len(out_refs[a].shape) == 4 else 1] // 2
            for k, (px, py) in enumerate(chips):
                landed = _chip_slab(out_refs[a], 2 * px + py, pl.ds(c * hr, hr))
                sent.append(_remote(landed, landed, send_sems, recv_sems, 3 * a + k, (x, y, 1 - c)))
        for cp in sent:
            cp.start()
        for a in range(n):
            hr = out_refs[a].shape[0 if len(out_refs[a].shape) == 4 else 1] // 2
            for k, (px, py) in enumerate(chips):
                theirs = _chip_slab(out_refs[a], 2 * px + py, pl.ds((1 - c) * hr, hr))
                _remote(theirs, theirs, send_sems, recv_sems, 3 * a + k, (x, y, 1 - c)).wait_recv()
        for cp in sent:
            cp.wait_send()

    return hbm_call(
        body, name=name, in_specs=[HBM_SPEC] * n, out_specs=[HBM_SPEC] * n,
        out_shape=[jax.ShapeDtypeStruct(a.shape, a.dtype) for a in lands], input_output_aliases={a: a for a in range(n)},
        scratch_shapes=[pltpu.SemaphoreType.DMA((3 * n,))] * 2,
    )(*lands)


def _scatter_copies(t_refs, land_refs, send_sems, recv_sems):
    x, y, c = _place()
    chips = [(1 - x, y), (x, 1 - y), (1 - x, 1 - y)]
    return [_remote(t_refs[a].at[:, 2 * px + py], land_refs[a].at[:, k], send_sems, recv_sems, 3 * a + k, (px, py, c))
            for a in range(len(t_refs)) for k, (px, py) in enumerate(chips)]


def scatter_start(parts, after, name):
    lands = [lax.empty((t.shape[0], N_CHIPS - 1) + t.shape[2:], t.dtype) for t in parts]
    return _split_start(_scatter_copies, parts, lands, after, 3 * len(parts), name)


def scatter_wait(state, after, name):
    send_sems, recv_sems, sources, lands, _ = state
    return _split_wait(_scatter_copies, send_sems, recv_sems, sources, lands, after, name)


def swap_sibling(parts, name):
    n = len(parts)

    def body(*refs):
        v_refs, out_refs, send_sems, recv_sems = refs[:n], refs[n:2 * n], refs[2 * n], refs[2 * n + 1]
        x, y, c = _place()
        cps = []
        for a in range(n):
            hr = v_refs[a].shape[2] // 2
            cps.append(_remote(v_refs[a].at[:, :, pl.ds((1 - c) * hr, hr)], out_refs[a], send_sems, recv_sems, a, (x, y, 1 - c)))
        for cp in cps:
            cp.start()
        for cp in cps:
            cp.wait()

    return hbm_call(
        body, name=name, in_specs=[HBM_SPEC] * n, out_specs=[HBM_SPEC] * n,
        out_shape=[jax.ShapeDtypeStruct(v.shape[:2] + (v.shape[2] // 2, v.shape[3]), v.dtype) for v in parts],
        scratch_shapes=[pltpu.SemaphoreType.DMA((n,))] * 2,
    )(*parts)


def join_halves(halves, layer, name):
    n = len(halves)

    def body(*refs):
        out_refs, send_sems, recv_sems = refs[n:2 * n], refs[2 * n], refs[2 * n + 1]
        x, y, c = _place()
        cps = []
        for a in range(n):
            hr = out_refs[a].shape[1] // 2
            mine = out_refs[a].at[layer, pl.ds(c * hr, hr)]
            cps.append(_remote(mine, mine, send_sems, recv_sems, a, (x, y, 1 - c)))
        for cp in cps:
            cp.start()
        for a in range(n):
            hr = out_refs[a].shape[1] // 2
            theirs = out_refs[a].at[layer, pl.ds((1 - c) * hr, hr)]
            _remote(theirs, theirs, send_sems, recv_sems, a, (x, y, 1 - c)).wait_recv()
        for cp in cps:
            cp.wait_send()

    return hbm_call(
        body, name=name, in_specs=[HBM_SPEC] * n, out_specs=[HBM_SPEC] * n,
        out_shape=[jax.ShapeDtypeStruct(f.shape, f.dtype) for f in halves], input_output_aliases={a: a for a in range(n)},
        scratch_shapes=[pltpu.SemaphoreType.DMA((n,))] * 2,
    )(*halves)


def gather_devices(v, name, after=()):
    def body(v_ref, *refs):
        out_ref, send_sems, recv_sems, local_sem = refs[len(after):]
        x, y, c = _place()
        me = 4 * x + 2 * y + c
        own = pltpu.make_async_copy(v_ref, out_ref.at[me], local_sem)
        own.start()
        peers = [((x + dx) % 2, (y + dy) % 2, (c + dc) % 2) for dx in (0, 1) for dy in (0, 1) for dc in (0, 1)][1:]
        sent = []
        for k, peer in enumerate(peers):
            cp = pltpu.make_async_remote_copy(src_ref=v_ref, dst_ref=out_ref.at[me], send_sem=send_sems.at[k], recv_sem=recv_sems.at[k],
                                              device_id=peer, device_id_type=MESH_ID)
            cp.start()
            sent.append(cp)
        for k, (px, py, pc) in enumerate(peers):
            slot = out_ref.at[4 * px + 2 * py + pc]
            pltpu.make_async_remote_copy(src_ref=slot, dst_ref=slot, send_sem=send_sems.at[k], recv_sem=recv_sems.at[k],
                                         device_id=(px, py, pc), device_id_type=MESH_ID).wait_recv()
        for cp in sent:
            cp.wait_send()
        own.wait()

    vm = pl.BlockSpec(memory_space=pltpu.VMEM)
    return pl.pallas_call(body, name=name, in_specs=[vm] + [ANY_SPEC] * len(after), out_specs=vm,
                          out_shape=jax.ShapeDtypeStruct((N_DEV,) + v.shape, v.dtype),
                          scratch_shapes=[pltpu.SemaphoreType.DMA((N_DEV - 1,)), pltpu.SemaphoreType.DMA((N_DEV - 1,)),
                                          pltpu.SemaphoreType.DMA])(v, *after)


ADD_ROWS = 512


def add_pair(place, a, b, name):
    L, n, hr, cols = b.shape
    tr = _divisors(hr, 2 * SUBLANES, ADD_ROWS)[0]
    nb = hr // tr

    def body(p_ref, a_ref, b_ref, o_ref):
        del p_ref
        o_ref[...] = (a_ref[...].astype(F32) + b_ref[...].astype(F32)).astype(o_ref.dtype)

    blk = pl.BlockSpec((None, None, tr, cols), lambda l, d, i, p: (l, d, i, 0))
    grid_spec = pltpu.PrefetchScalarGridSpec(
        num_scalar_prefetch=1, grid=(L, n, nb),
        in_specs=[pl.BlockSpec((None, None, tr, cols), lambda l, d, i, p: (l, d, p[0] * nb + i, 0)), blk], out_specs=blk)
    return hbm_call(body, name=name, grid_spec=grid_spec, out_shape=jax.ShapeDtypeStruct(b.shape, b.dtype),
                          compiler_params=_params(("parallel", "parallel", "parallel")))(place, a, b)


def add_chips(place, own, others, layer, stacked, name):
    _, n, hr, cols = others.shape
    tr = _divisors(hr, 2 * SUBLANES, ADD_ROWS)[0]
    nb = hr // tr
    create = isinstance(stacked, tuple)

    def body(p_ref, own_ref, *refs):
        del p_ref
        acc = own_ref[...].astype(F32)
        for k in range(n):
            acc = acc + refs[k][...].astype(F32)
        refs[-1][...] = acc

    ins = [pl.BlockSpec((None, None, tr, cols), lambda i, p: (0, p[1], i, 0))]
    ins += [pl.BlockSpec((None, None, tr, cols), functools.partial(lambda k, i, p: (0, k, i, 0), k)) for k in range(n)]
    grid_spec = pltpu.PrefetchScalarGridSpec(num_scalar_prefetch=1, grid=(nb,), in_specs=ins + ([] if create else [ANY_SPEC]),
                                             out_specs=pl.BlockSpec((None, tr, cols), lambda i, p: (layer, p[0] * nb + i, 0)))
    shape = stacked if create else stacked.shape
    return hbm_call(body, name=name, grid_spec=grid_spec, out_shape=jax.ShapeDtypeStruct(shape, F32),
                          input_output_aliases={} if create else {n + 2: 0},
                          compiler_params=_params(("parallel",)))(place, own, *([others] * n), *([] if create else [stacked]))


def _alpha(depth):
    return (2 * depth) ** 0.25


def _wmm(a, weight, mode, name, deps=(), **more):
    arr, how = weight
    return mm(a, arr, mode, name, deps=deps, **how, **more)


def layer_fwd(h, mem, w, tab, alpha, deps=(), late=None):
    D = h.shape[1]
    proj = _wmm(h, w["w_in"], "nt", "mm_proj", deps)
    xc, r, ig, a, b = rg_gates_fwd(proj, w["conv_w"], w["conv_b"], w["w_rg"], w["b_rg"], w["w_ig"], w["b_ig"], w["lru_lambda"], "rg_gates_fwd")
    hs, y_rnn = rg_scan_fwd(proj, a, b, "rg_scan_fwd")
    y_attn = attn_fwd(proj, w["sinks"], tab, D, "attn_fwd")
    deps = ()
    if late is not None:
        rest, deps = late(y_attn)
        w = {**w, **rest}
    pr = _wmm(y_rnn, w["w_br_rnn"], "nn", "mm_br_rnn", deps)
    pa = _wmm(y_attn, w["w_br_attn"], "nn", "mm_br_attn")
    merged = merge_fwd(proj, pr, pa, "merge_fwd")
    h1, xh1, rs1 = _wmm(merged, w["w_out"], "nn", "mm_out_ln1", post_norm=(h, w["ln1_g"], w["ln1_b"], alpha))
    qc = _wmm(h1, w["cq_w"], "nn", "mm_cq", out_dtype=MXU_DTYPE)
    kv = _wmm(mem, w["ckv_w"], "nn", "mm_ckv", out_dtype=MXU_DTYPE)
    o = cross_fwd(qc, kv, "cross_fwd")
    h2, xh2, rs2 = _wmm(o, w["co_w"], "nn", "mm_co_ln2", post_norm=(h1, w["ln2_g"], w["ln2_b"], alpha))
    gu = _wmm(h2, w["ffn_wi"], "nn", "mm_ffn_wi", out_blocks=2)
    act = swiglu_fwd(gu, "swiglu_fwd")
    h3, xh3, rs3 = _wmm(act, w["ffn_wo"], "nn", "mm_ffn_wo_ln3", post_norm=(h2, w["ln3_g"], w["ln3_b"], alpha))
    saved = dict(h=h, proj=proj, xc=xc, r=r, ig=ig, a=a, hs=hs, y_rnn=y_rnn, y_attn=y_attn, pr=pr, pa=pa, xh1=xh1, rs1=rs1, h1=h1,
                 qc=qc, kv=kv, o=o, xh2=xh2, rs2=rs2, h2=h2, gu=gu, xh3=xh3, rs3=rs3, merged=merged, act=act)
    return h3, saved, w


def layer_bwd(dh, mem, w, s, tab, alpha, deps=(), halfway=None):
    D = dh.shape[1]
    g = {}
    wg = dict(out_dtype=MXU_DTYPE)
    dz3, g["ln3_g"], g["ln3_b"] = ln_bwd(dh, None, s["xh3"], s["rs3"], w["ln3_g"], 1.0, "ln3_bwd")
    g["ffn_wo"] = mm(s["act"], dz3, "tn", "mm_d_ffn_wo", deps=deps, **wg)
    dact = _wmm(dz3, w["ffn_wo"], "nt", "mm_dact")
    dgu = swiglu_bwd(s["gu"], dact, "swiglu_bwd")
    g["ffn_wi"] = mm(s["h2"], dgu, "tn", "mm_d_ffn_wi", b_blocks=2, out_blocks=N_CHIPS, **wg)
    dh2 = _wmm(dgu, w["ffn_wi"], "nt", "mm_dh2", a_blocks=2)
    dz2, g["ln2_g"], g["ln2_b"] = ln_bwd(dz3, dh2, s["xh2"], s["rs2"], w["ln2_g"], alpha, "ln2_bwd")
    g["co_w"] = mm(s["o"], dz2, "tn", "mm_d_co", **wg)
    do = _wmm(dz2, w["co_w"], "nt", "mm_do", out_dtype=MXU_DTYPE)
    dqc, dkv = cross_bwd(s["qc"], s["kv"], do, "cross_bwd")
    g["cq_w"] = mm(s["h1"], dqc, "tn", "mm_d_cq", **wg)
    g["ckv_w"] = mm(mem, dkv, "tn", "mm_d_ckv", out_blocks=N_CHIPS, **wg)
    dh1 = _wmm(dqc, w["cq_w"], "nt", "mm_dh1")
    deps = halfway(g, dh1) if halfway is not None else ()
    dz1, g["ln1_g"], g["ln1_b"] = ln_bwd(dz2, dh1, s["xh1"], s["rs1"], w["ln1_g"], alpha, "ln1_bwd")
    g["w_out"] = mm(s["merged"], dz1, "tn", "mm_d_out", deps=deps, **wg)
    dm = _wmm(dz1, w["w_out"], "nt", "mm_dmerged")
    dpr, dpa, dg_rnn, dg_attn = merge_bwd(s["proj"], s["pr"], s["pa"], dm, "merge_bwd")
    g["w_br_rnn"] = mm(s["y_rnn"], dpr, "tn", "mm_d_br_rnn", **wg)
    g["w_br_attn"] = mm(s["y_attn"], dpa, "tn", "mm_d_br_attn", **wg)
    dy_rnn = _wmm(dpr, w["w_br_rnn"], "nt", "mm_dy_rnn")
    dy_attn = _wmm(dpa, w["w_br_attn"], "nt", "mm_dy_attn")
    dq, dkb, dvb, dsink = attn_bwd(s["proj"], w["sinks"], tab, s["y_attn"], dy_attn, D, "attn_bwd")
    dk, dv = band_fold(dkb, dvb, "band_fold")
    g["sinks"] = dsink[:, :w["sinks"].shape[0]]
    dgr, gt = rg_scan_bwd(s["proj"], dy_rnn, s["hs"], s["a"], "rg_scan_bwd")
    dxc, g["w_rg"], g["w_ig"], g["b_rg"], g["b_ig"], g["lru_lambda"] = rg_gates_bwd(
        gt, s["hs"], s["xc"], s["r"], s["ig"], w["w_rg"], w["w_ig"], w["lru_lambda"], "rg_gates_bwd")
    dxr, g["conv_w"], g["conv_b"] = rg_conv_bwd(s["proj"], dxc, w["conv_w"], "rg_conv_bwd")
    dproj = jnp.concatenate([dxr, dgr, dq, dk, dv, dg_rnn, dg_attn], axis=1)
    g["w_in"] = mm(dproj, s["h"], "tn", "mm_d_in", **wg)
    return _wmm(dproj, w["w_in"], "nn", "mm_dh", plus=(dz1, alpha)), g


def local_step(x, mem, target, depth, weights_of, grads_halfway, grads_done):
    alpha = _alpha(depth)
    tab = rope_table(x.shape[0])
    h, saved, layers = x, [], []
    for l in range(depth):
        wl, deps, late = weights_of(l, h)
        h, s, wl = layer_fwd(h, mem, wl, tab, alpha, deps, late)
        layers.append(wl)
        saved.append(s)
    dh, loss = loss_head(h, target, "loss_head")
    deps = ()
    for l in reversed(range(depth)):
        dh, g = layer_bwd(dh, mem, layers[l], saved[l], tab, alpha, deps, grads_halfway(l))
        deps = grads_done(l, g, dh)
    return loss, dh


def _pad_rows(flat):
    n = flat.shape[0]
    rows = -(-n // (LANES * SUBLANES)) * SUBLANES
    return jnp.pad(flat, (0, rows * LANES - n)).reshape(rows, LANES)


def kernel(x, mem, w_in, conv_w, conv_b, w_rg, b_rg, w_ig, b_ig, lru_lambda, w_br_rnn, w_br_attn, sinks, w_out, ln1_g, ln1_b, cq_w, ckv_w, co_w, ln2_g, ln2_b, ffn_wi, ffn_wo, ln3_g, ln3_b, loss_target, m_w_in, m_conv_w, m_conv_b, m_w_rg, m_b_rg, m_w_ig, m_b_ig, m_lru_lambda, m_w_br_rnn, m_w_br_attn, m_sinks, m_w_out, m_ln1_g, m_ln1_b, m_cq_w, m_ckv_w, m_co_w, m_ln2_g, m_ln2_b, m_ffn_wi, m_ffn_wo, m_ln3_g, m_ln3_b, v_w_in, v_conv_w, v_conv_b, v_w_rg, v_b_rg, v_w_ig, v_b_ig, v_lru_lambda, v_w_br_rnn, v_w_br_attn, v_sinks, v_w_out, v_ln1_g, v_ln1_b, v_cq_w, v_ckv_w, v_co_w, v_ln2_g, v_ln2_b, v_ffn_wi, v_ffn_wo, v_ln3_g, v_ln3_b):
    args = dict(locals())
    w = {n: args[n] for n in WEIGHTS}
    m = {n: args["m_" + n] for n in WEIGHTS}
    v = {n: args["v_" + n] for n in WEIGHTS}
    for group in (w, m, v):
        group["w_in"] = jnp.swapaxes(group["w_in"], 1, 2)
    cx, cy, cc = _place()
    chip = 2 * cx + cy
    L = w_in.shape[0]

    place = jnp.stack([cc, chip]).astype(jnp.int32)
    cw_rows = _pad_rows(conv_w.reshape(-1))
    cw_all = gather_devices(cw_rows, "gather_conv_w")[0::2]
    cw_parts = cw_all.reshape(N_CHIPS, -1)[:, :conv_w.size].reshape((N_CHIPS,) + conv_w.shape)
    conv_full = jnp.concatenate([cw_parts[k] for k in range(N_CHIPS)], axis=2)

    shards = [{n: w[n][l].astype(MXU_DTYPE) for n in BIG} for l in range(L)]
    late_names = tuple(n for n in BIG if n not in GATHER_FIRST)
    gathering = {(0, GATHER_FIRST): gather_start([shards[0][n] for n in GATHER_FIRST], cw_rows, "gather_start_0a")}
    gathering[0, late_names] = gather_start([shards[0][n] for n in late_names], gathering[0, GATHER_FIRST][4], "gather_start_0b")

    def gathered(l, names, after, tag):
        lands = gather_pass(gather_wait(gathering.pop((l, names)), after, f"gather_wait_{tag}"), f"gather_pass_{tag}")
        wl = {}
        for n, gw in zip(names, lands):
            rows_joined = gw.reshape(gw.shape[:-3] + (-1, gw.shape[-1]))
            if n in COL_BLOCKED:
                wl[n] = (gw, dict(b_blocks=N_CHIPS))
            elif n in GATE_WEIGHTS:
                wl[n] = rows_joined
            else:
                wl[n] = (rows_joined, {})
        return wl, lands

    def start_layer(l, after):
        if l >= L:
            return ()
        gathering[l, BIG] = gather_start([shards[l][n] for n in BIG], after, f"gather_start_{l}")
        return (gathering[l, BIG][4],)

    def weights_of(l, h):
        deps, late = (), None
        if l == 0:
            wl, _ = gathered(0, GATHER_FIRST, h, "0a")

            def late(after):
                rest, lands = gathered(0, late_names, after, "0b")
                return rest, start_layer(1, lands[0])
        else:
            wl, lands = gathered(l, BIG, h, str(l))
            deps = start_layer(l + 1, lands[0])
        for n in SMALL:
            wl[n] = conv_full[l] if n == "conv_w" else w[n][l] if n == "sinks" else w[n][l][None, :]
        return wl, deps, late

    def for_chips(n, g):
        if n in COL_BLOCKED:
            return g
        if n in GATE_WEIGHTS:
            nb, bw, _ = g.shape
            g = g.reshape(nb, N_CHIPS, bw // N_CHIPS, bw).transpose(1, 0, 2, 3).reshape(N_CHIPS, nb * bw // N_CHIPS, bw)
        else:
            g = g.reshape(N_CHIPS, g.shape[0] // N_CHIPS, g.shape[1])
        return g.astype(MXU_DTYPE)

    reduced, scattering, small_grads = {}, {}, [None] * L
    late_grads = tuple(n for n in BIG if n not in SCATTER_FIRST)

    def start_scatter(l, names, g, after, tag):
        partial_sums = [for_chips(n, g[n])[None] for n in names]
        from_sibling = swap_sibling(partial_sums, f"grad_to_sibling_{tag}")
        chip_sums = [add_pair(place, a, b, f"grad_add_pair_{n}_{l}") for n, a, b in zip(names, partial_sums, from_sibling)]
        scattering[l, names] = scatter_start(chip_sums, after, f"grad_scatter_start_{tag}")
        return (scattering[l, names][4],)

    def finish_layer(l, after):
        for names in [k[1] for k in list(scattering) if k[0] == l]:
            tag = str(l) if names == BIG else f"{l}{'a' if names == SCATTER_FIRST else 'b'}"
            chip_sums, from_chips = scatter_wait(scattering.pop((l, names)), after, f"grad_scatter_wait_{tag}")
            for n, own, others in zip(names, chip_sums, from_chips):
                target = reduced.get(n, (L, 2 * own.shape[2], own.shape[3]))
                reduced[n] = add_chips(place, own, others, l, target, f"grad_add_chips_{n}_{l}")
        reduced.update(zip(BIG, join_halves([reduced[n] for n in BIG], l, f"grad_join_{l}")))

    def grads_halfway(l):
        def halfway(g, after):
            return start_scatter(l, SCATTER_FIRST, g, after, f"{l}a")

        return halfway

    def grads_done(l, g, dh):
        small_grads[l] = {n: g[n] for n in SMALL}
        deps = start_scatter(l, late_grads, g, dh, f"{l}b")
        if 1 < l + 1 < L:
            finish_layer(l + 1, dh)
        return deps

    loss11, dx = local_step(x[0], mem[0], loss_target[0], L, weights_of, grads_halfway, grads_done)
    loss = lax.psum(loss11[0, 0], ("x", "y", "c"))

    first = min(2, L)
    updated = {}
    if first < L:
        for n in BIG:
            updated[n] = adamw(w[n], reduced[n].reshape(w[n].shape), m[n], v[n], f"adamw_{n}_upper", layers=(first, L))
    behind = (jnp.stack([updated[n][0][(0,) * w[n].ndim] for n in updated]),) if updated else ()
    small_full = {n: jnp.stack([gl[n] for gl in small_grads]).reshape(w[n].shape[:1] + ((CONV_WIDTH, -1) if n == "conv_w" else (-1,)))
                  for n in SMALL}
    small_flat = jnp.concatenate([small_full[n].reshape(-1) for n in SMALL])
    small_sum = sum_devices(gather_devices(_pad_rows(small_flat), "gather_small_grads", behind), "sum_small_grads").reshape(-1)
    delta, new_m, new_v, grad = {}, {}, {}, {}
    off = 0
    for n in SMALL:
        gfull = small_sum[off:off + small_full[n].size].reshape(small_full[n].shape)
        off += small_full[n].size
        if n == "conv_w":
            width = conv_w.shape[2]
            gfull = lax.dynamic_slice_in_dim(gfull, chip * width, width, axis=2)
        delta[n], new_m[n], new_v[n], grad[n] = adamw(w[n], gfull, m[n], v[n], "adamw_" + n)
    after = jnp.stack([delta[n][(0,) * delta[n].ndim] for n in SMALL])
    for l in reversed(range(first)):
        finish_layer(l, after)

    for n in BIG:
        some = dict(layers=(0, first), into=updated[n]) if updated else {}
        delta[n], new_m[n], new_v[n], grad[n] = adamw(w[n], reduced[n].reshape(w[n].shape), m[n], v[n], "adamw_" + n, **some)
    for group in (delta, new_m, new_v, grad):
        group["w_in"] = jnp.swapaxes(group["w_in"], 1, 2)
    return (loss, dx[None], *[grad[n] for n in WEIGHTS], *[delta[n] for n in WEIGHTS], *[new_m[n] for n in WEIGHTS],
            *[new_v[n] for n in WEIGHTS])
```

```python
import functools
import math

import jax
import jax.numpy as jnp
import numpy as np
from jax import lax
from jax.experimental import pallas as pl
from jax.experimental.pallas import tpu as pltpu

F32 = jnp.float32
BF16 = jnp.bfloat16
MXU_DTYPE = BF16

HEAD_DIM = 64
N_KV_HEADS = 2
WINDOW = 128
ROT_DIM = HEAD_DIM // 4
ROPE_THETA = 500000.0
CROSS_HEADS = 4
CONV_WIDTH = 4
LRU_C = 8.0
LN_EPS = 1e-5
NEG_INF = -1e30
ADAM_LR = 0.001
ADAM_B1 = 0.9
ADAM_B2 = 0.999
ADAM_EPS = 1e-08
ADAM_WD = 0.01
ADAM_STEP = 10

VMEM_BYTES_V7X = 64 * 1024 * 1024
VMEM_BLOCK_BUDGET = 36 * 1024 * 1024
LANES = 128
SUBLANES = 8

MESH_ID = pl.DeviceIdType.MESH
N_CHIPS = 4
N_DEV = 8

BIG = ("w_in", "w_rg", "w_ig", "w_br_rnn", "w_br_attn", "w_out", "cq_w", "ckv_w", "co_w", "ffn_wi", "ffn_wo")
SHARD_AXIS = {"w_in": 0, "w_rg": 1, "w_ig": 1, "w_br_rnn": 0, "w_br_attn": 0, "w_out": 0, "cq_w": 0, "ckv_w": 1,
              "co_w": 0, "ffn_wi": 1, "ffn_wo": 0}
SMALL = ("conv_w", "conv_b", "b_rg", "b_ig", "lru_lambda", "sinks", "ln1_g", "ln1_b", "ln2_g", "ln2_b", "ln3_g", "ln3_b")
WEIGHTS = ("w_in", "conv_w", "conv_b", "w_rg", "b_rg", "w_ig", "b_ig", "lru_lambda", "w_br_rnn", "w_br_attn", "sinks",
           "w_out", "ln1_g", "ln1_b", "cq_w", "ckv_w", "co_w", "ln2_g", "ln2_b", "ffn_wi", "ffn_wo", "ln3_g", "ln3_b")
GATE_WEIGHTS = ("w_rg", "w_ig")
COL_BLOCKED = ("ckv_w", "ffn_wi")
GATHER_FIRST = ("w_in", "w_rg", "w_ig")


def _params(dims=None, vmem=None):
    return pltpu.CompilerParams(dimension_semantics=dims, vmem_limit_bytes=vmem)


def _vmem_limit(block_bytes, temp_bytes=0):
    want = int(2 * block_bytes + temp_bytes) + (6 << 20)
    return max(32 << 20, min(want, VMEM_BYTES_V7X - (6 << 20)))


def _divisors(n, align, cap):
    out = [d for d in range(align, min(n, cap) + 1, align) if n % d == 0]
    if n <= cap and n not in out:
        out.append(n)
    return sorted(out, reverse=True) or [n]


PIN_MIN_ELEMENTS = 1 << 18


def hbm_call(body, **kw):
    def in_hbm(s):
        return pltpu.HBM(s.shape, s.dtype) if math.prod(s.shape) >= PIN_MIN_ELEMENTS else s

    shapes = kw.pop("out_shape")
    shapes = [in_hbm(s) for s in shapes] if isinstance(shapes, (list, tuple)) else in_hbm(shapes)
    call = pl.pallas_call(body, out_shape=shapes, **kw)

    def run(*args):
        return call(*[pltpu.with_memory_space_constraint(a, pltpu.HBM) if a.size >= PIN_MIN_ELEMENTS else a for a in args])

    return run


def _sigmoid(x):
    return 1.0 / (1.0 + jnp.exp(-x))


def _gelu_parts(x):
    c = math.sqrt(2.0 / math.pi)
    u = c * (x + 0.044715 * x * x * x)
    t = jnp.tanh(u)
    return t, c * (1.0 + 3 * 0.044715 * x * x)


def _gelu(x):
    t, _ = _gelu_parts(x)
    return 0.5 * x * (1.0 + t)


def _gelu_grad(x):
    t, du = _gelu_parts(x)
    return 0.5 * (1.0 + t) + 0.5 * x * (1.0 - t * t) * du


def _neg_expm1(x):
    series = x * (1.0 + x * (0.5 + x * (1.0 / 6 + x * (1.0 / 24 + x * (1.0 / 120)))))
    return -jnp.where(x > -0.1, series, jnp.exp(x) - 1.0)


def _softplus_neg(lam):
    x = -lam
    return jnp.maximum(x, 0.0) + jnp.log1p(jnp.exp(-jnp.abs(x)))


STEP_US = 0.35
HBM_BYTES_PER_US = 2.5e6
MXU_FLOPS_PER_US = 7e8


def _layer_norm(z, g, b):
    mu = jnp.mean(z, axis=-1, keepdims=True)
    zc = z - mu
    rs = lax.rsqrt(jnp.mean(zc * zc, axis=-1, keepdims=True) + LN_EPS)
    xh = zc * rs
    return xh * g + b, xh, rs


def mm(a, b, mode, name, *, b_index=(), a_blocks=0, b_blocks=0, out_blocks=0, out_dtype=F32, deps=(), post_norm=None, plus=None):
    nlead = len(b_index) + (1 if b_blocks else 0)
    bk, bn = b.shape[nlead:]
    M, K = (a.shape[-1], a.shape[-2]) if mode == "tn" else (a.shape[-2], a.shape[-1] * max(a_blocks, 1))
    N = bk if mode == "nt" else bn * max(b_blocks, 1) if mode == "nn" or mode == "tn" else bn
    asz, bsz, osz = a.dtype.itemsize, b.dtype.itemsize, jnp.dtype(out_dtype).itemsize
    n_unit = math.gcd(N // max(out_blocks, 1), N // max(b_blocks, 1) if mode != "nt" else N)
    k_unit = math.gcd(K // max(a_blocks, 1), K // max(b_blocks, 1) if mode == "nt" else K)
    tms = _divisors(M, LANES if mode == "tn" else SUBLANES, 2048)
    tns = [N] if post_norm else _divisors(n_unit, LANES, 2048)
    tks = _divisors(k_unit, LANES, k_unit)
    best = None
    for tm in tms:
        for tn in tns:
            for tk in tks:
                nk = K // tk
                scratch = tm * tn * 4 if (nk > 1 and osz != 4) else 0
                blocks = tm * tk * asz + tn * tk * bsz + tm * tn * osz * (3 if post_norm else 1)
                temps = tm * tk * (2 + (4 if mode == "tn" else 0)) + tn * tk * 2 + tm * tn * 4 + scratch
                if 2 * blocks + temps > VMEM_BLOCK_BUDGET + (8 << 20):
                    continue
                ni, nj = M // tm, N // tn
                traffic = M * K * asz * (nj if nk > 1 else 1) + N * K * bsz * (1 if nj * nk == 1 else ni) + M * N * osz
                busy = max(traffic / HBM_BYTES_PER_US, 2.0 * M * N * K / MXU_FLOPS_PER_US)
                cost = ni * nj * nk * STEP_US + busy + blocks / HBM_BYTES_PER_US
                if best is None or cost < best[0]:
                    best = (cost, tm, tn, tk, blocks, temps)
    _, tm, tn, tk, blocks, temps = best
    nk = K // tk
    use_scratch = nk > 1 and osz != 4

    def split(index, total, blocks, tile):
        per = total // blocks // tile
        return index // per, index % per

    def body(a_ref, b_ref, *rest):
        rest = rest[len(deps):]
        if post_norm:
            h_ref, g_ref, beta_ref, o_ref, xh_ref, rs_ref = rest[:6]
            acc = rest[6:]
        elif plus:
            plus_ref, o_ref, acc = rest[0], rest[1], rest[2:]
        else:
            o_ref, acc = rest[0], rest[1:]
        av = a_ref[...].astype(MXU_DTYPE)
        bv = b_ref[...].astype(MXU_DTYPE)
        dn = {"nn": (((1,), (0,)), ((), ())), "nt": (((1,), (1,)), ((), ())), "tn": (((0,), (0,)), ((), ()))}[mode]
        r = lax.dot_general(av, bv, dn, preferred_element_type=F32)

        def normalise(f):
            o_ref[...], xh_ref[...], rs_ref[...] = _layer_norm(post_norm[3] * h_ref[...] + f, g_ref[...], beta_ref[...])

        if nk == 1 and post_norm:
            normalise(r)
        elif nk == 1 and plus:
            o_ref[...] = plus[1] * plus_ref[...] + r
        elif nk == 1:
            o_ref[...] = r.astype(o_ref.dtype)
        else:
            acc_ref = acc[0] if use_scratch else o_ref

            @pl.when(pl.program_id(2) == 0)
            def _():
                acc_ref[...] = r

            @pl.when(pl.program_id(2) > 0)
            def _():
                acc_ref[...] += r

            if use_scratch:
                @pl.when(pl.program_id(2) == nk - 1)
                def _():
                    o_ref[...] = acc_ref[...].astype(o_ref.dtype)
            elif post_norm:
                @pl.when(pl.program_id(2) == nk - 1)
                def _():
                    normalise(o_ref[...])
            elif plus:
                @pl.when(pl.program_id(2) == nk - 1)
                def _():
                    o_ref[...] = plus[1] * plus_ref[...] + o_ref[...]

    if mode == "tn":
        a_spec = pl.BlockSpec((tk, tm), lambda i, j, k: (k, i))
    elif a_blocks:
        a_spec = pl.BlockSpec((None, tm, tk), lambda i, j, k: (split(k, K, a_blocks, tk)[0], i, split(k, K, a_blocks, tk)[1]))
    else:
        a_spec = pl.BlockSpec((tm, tk), lambda i, j, k: (i, k))
    lead = (None,) * nlead
    if mode == "nt":
        bmap = ((lambda i, j, k: b_index + (split(k, K, b_blocks, tk)[0], j, split(k, K, b_blocks, tk)[1])) if b_blocks
                else (lambda i, j, k: b_index + (j, k)))
        b_spec = pl.BlockSpec(lead + (tn, tk), bmap)
    else:
        bmap = ((lambda i, j, k: b_index + (split(j, N, b_blocks, tn)[0], k, split(j, N, b_blocks, tn)[1])) if b_blocks
                else (lambda i, j, k: b_index + (k, j)))
        b_spec = pl.BlockSpec(lead + (tk, tn), bmap)
    if out_blocks:
        o_spec = pl.BlockSpec((None, tm, tn), lambda i, j, k: (split(j, N, out_blocks, tn)[0], i, split(j, N, out_blocks, tn)[1]))
        o_shape = jax.ShapeDtypeStruct((out_blocks, M, N // out_blocks), out_dtype)
    else:
        o_spec = pl.BlockSpec((tm, tn), lambda i, j, k: (i, j))
        o_shape = jax.ShapeDtypeStruct((M, N), out_dtype)
    in_specs, extra = [a_spec, b_spec] + [pl.BlockSpec(memory_space=pl.ANY)] * len(deps), ()
    if post_norm:
        vec = pl.BlockSpec((1, N), lambda i, j, k: (0, 0))
        in_specs += [pl.BlockSpec((tm, N), lambda i, j, k: (i, 0)), vec, vec]
        o_spec = [o_spec, pl.BlockSpec((tm, N), lambda i, j, k: (i, 0)), pl.BlockSpec((tm, 1), lambda i, j, k: (i, 0))]
        o_shape = [o_shape, jax.ShapeDtypeStruct((M, N), F32), jax.ShapeDtypeStruct((M, 1), F32)]
        extra = post_norm[:3]
    elif plus:
        in_specs += [pl.BlockSpec((tm, tn), lambda i, j, k: (i, j))]
        extra = plus[:1]
    return hbm_call(
        body, name=name, grid=(M // tm, N // tn, nk), in_specs=in_specs, out_specs=o_spec, out_shape=o_shape,
        scratch_shapes=[pltpu.VMEM((tm, tn), F32)] if use_scratch else [],
        compiler_params=_params(("parallel", "parallel", "arbitrary"), _vmem_limit(blocks, temps)),
    )(a, b, *deps, *extra)


ROW_TILE = 512
GATE_ROWS = 1024


def ln_bwd(dy_a, dy_b, xh, rs, g, c1, name):
    S, D = xh.shape
    tr = min(ROW_TILE, S)
    two = dy_b is not None

    def body(*refs):
        if two:
            a_ref, b_ref, xh_ref, rs_ref, g_ref, dz_ref, dg_ref, db_ref = refs
            dy = c1 * a_ref[...] + b_ref[...]
        else:
            a_ref, xh_ref, rs_ref, g_ref, dz_ref, dg_ref, db_ref = refs
            dy = a_ref[...]
        x = xh_ref[...]
        dyg = dy * g_ref[...]
        m1 = jnp.mean(dyg, axis=-1, keepdims=True)
        m2 = jnp.mean(dyg * x, axis=-1, keepdims=True)
        dz_ref[...] = rs_ref[...] * (dyg - m1 - x * m2)

        @pl.when(pl.program_id(0) == 0)
        def _():
            dg_ref[...] = jnp.zeros_like(dg_ref)
            db_ref[...] = jnp.zeros_like(db_ref)

        dg_ref[...] += jnp.sum(dy * x, axis=0, keepdims=True)
        db_ref[...] += jnp.sum(dy, axis=0, keepdims=True)

    row = pl.BlockSpec((tr, D), lambda i: (i, 0))
    vec = pl.BlockSpec((1, D), lambda i: (0, 0))
    ins = [row, row] if two else [row]
    args = (dy_a, dy_b) if two else (dy_a,)
    return hbm_call(
        body, name=name, grid=(S // tr,), in_specs=ins + [row, pl.BlockSpec((tr, 1), lambda i: (i, 0)), vec],
        out_specs=[row, vec, vec],
        out_shape=[jax.ShapeDtypeStruct((S, D), F32), jax.ShapeDtypeStruct((1, D), F32), jax.ShapeDtypeStruct((1, D), F32)],
        compiler_params=_params(("arbitrary",), 48 << 20),
    )(*args, xh, rs, g)


def loss_head(y, t, name):
    S, D = y.shape
    tr = min(ROW_TILE, S)
    nsteps = S // tr

    def body(y_ref, t_ref, dy_ref, l_ref, acc_ref):
        i = pl.program_id(0)

        @pl.when(i == 0)
        def _():
            acc_ref[...] = jnp.zeros_like(acc_ref)

        e = y_ref[...] - t_ref[...]
        dy_ref[...] = e * (1.0 / D)
        acc_ref[...] += jnp.sum(e * e, axis=0, keepdims=True)

        @pl.when(i == nsteps - 1)
        def _():
            l_ref[...] = jnp.sum(acc_ref[...], axis=1, keepdims=True) * (0.5 / D)

    row = pl.BlockSpec((tr, D), lambda i: (i, 0))
    return hbm_call(
        body, name=name, grid=(nsteps,), in_specs=[row, row],
        out_specs=[row, pl.BlockSpec((1, 1), lambda i: (0, 0))],
        out_shape=[jax.ShapeDtypeStruct((S, D), F32), jax.ShapeDtypeStruct((1, 1), F32)],
        scratch_shapes=[pltpu.VMEM((1, D), F32)], compiler_params=_params(("arbitrary",)),
    )(y, t)


SWIGLU_ROWS = 256


def swiglu_fwd(gu, name):
    _, S, Fh = gu.shape
    tc = _divisors(Fh, LANES, 1536)[0]
    tr = min(SWIGLU_ROWS, S)

    def body(gu_ref, o_ref):
        g = gu_ref[0]
        o_ref[...] = (g * _sigmoid(g) * gu_ref[1]).astype(o_ref.dtype)

    return hbm_call(
        body, name=name, grid=(S // tr, Fh // tc), in_specs=[pl.BlockSpec((2, tr, tc), lambda i, j: (0, i, j))],
        out_specs=pl.BlockSpec((tr, tc), lambda i, j: (i, j)), out_shape=jax.ShapeDtypeStruct((S, Fh), MXU_DTYPE),
        compiler_params=_params(("parallel", "parallel")),
    )(gu)


def swiglu_bwd(gu, dact, name):
    _, S, Fh = gu.shape
    tc = _divisors(Fh, LANES, 1536)[0]
    tr = min(SWIGLU_ROWS, S)

    def body(gu_ref, d_ref, o_ref):
        g, u, d = gu_ref[0], gu_ref[1], d_ref[...]
        s = _sigmoid(g)
        o_ref[0] = (d * u * (s * (1.0 + g * (1.0 - s)))).astype(o_ref.dtype)
        o_ref[1] = (d * (g * s)).astype(o_ref.dtype)

    both = pl.BlockSpec((2, tr, tc), lambda i, j: (0, i, j))
    return hbm_call(
        body, name=name, grid=(S // tr, Fh // tc), in_specs=[both, pl.BlockSpec((tr, tc), lambda i, j: (i, j))],
        out_specs=both, out_shape=jax.ShapeDtypeStruct((2, S, Fh), MXU_DTYPE), compiler_params=_params(("parallel", "parallel")),
    )(gu, dact)


GATE_COLS = 256


def merge_fwd(proj, pr, pa, name):
    S, D = pr.shape
    tr = min(GATE_ROWS, S)
    c0 = (3 * D + 2 * N_KV_HEADS * HEAD_DIM) // GATE_COLS
    c1 = c0 + D // GATE_COLS

    def body(gr_ref, ga_ref, pr_ref, pa_ref, o_ref):
        o_ref[...] = (_sigmoid(gr_ref[...]) * pr_ref[...] + _sigmoid(ga_ref[...]) * pa_ref[...]).astype(o_ref.dtype)

    blk = pl.BlockSpec((tr, GATE_COLS), lambda i, j: (i, j))
    return hbm_call(
        body, name=name, grid=(S // tr, D // GATE_COLS),
        in_specs=[pl.BlockSpec((tr, GATE_COLS), lambda i, j: (i, c0 + j)), pl.BlockSpec((tr, GATE_COLS), lambda i, j: (i, c1 + j)),
                  blk, blk],
        out_specs=blk, out_shape=jax.ShapeDtypeStruct((S, D), MXU_DTYPE), compiler_params=_params(("parallel", "parallel")),
    )(proj, proj, pr, pa)


def merge_bwd(proj, pr, pa, dm, name):
    S, D = pr.shape
    tr = min(GATE_ROWS, S)
    c0 = (3 * D + 2 * N_KV_HEADS * HEAD_DIM) // GATE_COLS
    c1 = c0 + D // GATE_COLS

    def body(gr_ref, ga_ref, pr_ref, pa_ref, dm_ref, dpr_ref, dpa_ref, dgr_ref, dga_ref):
        sr, sa, d = _sigmoid(gr_ref[...]), _sigmoid(ga_ref[...]), dm_ref[...]
        dpr_ref[...] = (d * sr).astype(dpr_ref.dtype)
        dpa_ref[...] = (d * sa).astype(dpa_ref.dtype)
        dgr_ref[...] = (d * pr_ref[...] * (sr * (1.0 - sr))).astype(dgr_ref.dtype)
        dga_ref[...] = (d * pa_ref[...] * (sa * (1.0 - sa))).astype(dga_ref.dtype)

    blk = pl.BlockSpec((tr, GATE_COLS), lambda i, j: (i, j))
    sds = jax.ShapeDtypeStruct((S, D), MXU_DTYPE)
    return hbm_call(
        body, name=name, grid=(S // tr, D // GATE_COLS),
        in_specs=[pl.BlockSpec((tr, GATE_COLS), lambda i, j: (i, c0 + j)), pl.BlockSpec((tr, GATE_COLS), lambda i, j: (i, c1 + j)),
                  blk, blk, blk],
        out_specs=[blk, blk, blk, blk], out_shape=[sds, sds, sds, sds], compiler_params=_params(("parallel", "parallel")),
    )(proj, proj, pr, pa, dm)


RG_ROWS = 512


def _shift_down(cur, prev, d, row, first):
    halo = jnp.where(first, 0.0, pltpu.roll(prev, d, 0))
    return jnp.where(row >= d, pltpu.roll(cur, d, 0), halo)


def _shift_up(cur, nxt, d, row, last, tr):
    halo = jnp.where(last, 0.0, pltpu.roll(nxt, tr - d, 0))
    return jnp.where(row < tr - d, pltpu.roll(cur, tr - d, 0), halo)


def _lru_coeffs(r, lam):
    sp = _softplus_neg(lam)
    la = -LRU_C * r * sp
    return sp, la, jnp.exp(la), _neg_expm1(2.0 * la)


def rg_gates_fwd(proj, conv_w, conv_b, w_rg, b_rg, w_ig, b_ig, lam, name):
    S = proj.shape[0]
    nblk, bw, _ = w_rg.shape
    D = nblk * bw
    tr = min(RG_ROWS, S)

    def body(xr_ref, xp_ref, cw_ref, cb_ref, wr_ref, br_ref, wi_ref, bi_ref, lam_ref, xc_ref, r_ref, i_ref, a_ref, b_ref):
        first = pl.program_id(1) == 0
        cur, prev = xr_ref[...], xp_ref[...]
        row = lax.broadcasted_iota(jnp.int32, cur.shape, 0)
        xc = cb_ref[...]
        for k in range(CONV_WIDTH - 1):
            xc = xc + _shift_down(cur, prev, CONV_WIDTH - 1 - k, row, first) * cw_ref[k:k + 1, :]
        xc = xc + cur * cw_ref[CONV_WIDTH - 1:CONV_WIDTH, :]
        xm = xc.astype(MXU_DTYPE)
        r = _sigmoid(jnp.dot(xm, wr_ref[...].astype(MXU_DTYPE), preferred_element_type=F32) + br_ref[...])
        ig = _sigmoid(jnp.dot(xm, wi_ref[...].astype(MXU_DTYPE), preferred_element_type=F32) + bi_ref[...])
        _, _, a, em = _lru_coeffs(r, lam_ref[...])
        xc_ref[...] = xc
        r_ref[...] = r
        i_ref[...] = ig
        a_ref[...] = a
        b_ref[...] = jnp.sqrt(em) * (ig * xc)

    tile = pl.BlockSpec((tr, bw), lambda n, i: (i, n))
    vec = pl.BlockSpec((1, bw), lambda n, i: (0, n))
    wblk = pl.BlockSpec((None, bw, bw), lambda n, i: (n, 0, 0))
    sds = jax.ShapeDtypeStruct((S, D), F32)
    return hbm_call(
        body, name=name, grid=(nblk, S // tr),
        in_specs=[tile, pl.BlockSpec((tr, bw), lambda n, i: (jnp.maximum(i - 1, 0), n)),
                  pl.BlockSpec((CONV_WIDTH, bw), lambda n, i: (0, n)), vec, wblk, vec, wblk, vec, vec],
        out_specs=[tile] * 5, out_shape=[sds] * 5, compiler_params=_params(("parallel", "parallel")),
    )(proj, proj, conv_w, conv_b, w_rg, b_rg, w_ig, b_ig, lam)


SCAN_COLS = 256
CHUNK = SUBLANES
SCAN_UNROLL = 4


def rg_scan_fwd(proj, a, b, name):
    S, D = a.shape
    cb = min(SCAN_COLS, D)
    goff = D // cb

    def body(a_ref, b_ref, g_ref, hs_ref, y_ref):
        row = lax.broadcasted_iota(jnp.int32, (CHUNK, cb), 0)

        def step(c, carry):
            r0 = pl.multiple_of(c * CHUNK, CHUNK)
            A = a_ref[pl.ds(r0, CHUNK), :]
            B = b_ref[pl.ds(r0, CHUNK), :]
            for d in (1, 2, 4):
                As = jnp.where(row >= d, pltpu.roll(A, d, 0), 1.0)
                Bs = jnp.where(row >= d, pltpu.roll(B, d, 0), 0.0)
                B = A * Bs + B
                A = A * As
            hs_ref[pl.ds(r0, CHUNK), :] = B + A * carry
            a_end = jnp.sum(jnp.where(row == CHUNK - 1, A, 0.0), axis=0, keepdims=True)
            b_end = jnp.sum(jnp.where(row == CHUNK - 1, B, 0.0), axis=0, keepdims=True)
            return b_end + a_end * carry

        lax.fori_loop(0, S // CHUNK, step, jnp.zeros((1, cb), F32), unroll=SCAN_UNROLL)
        y_ref[...] = (hs_ref[...] * _gelu(g_ref[...])).astype(y_ref.dtype)

    col = pl.BlockSpec((S, cb), lambda j: (0, j))
    return hbm_call(
        body, name=name, grid=(D // cb,), in_specs=[col, col, pl.BlockSpec((S, cb), lambda j: (0, goff + j))],
        out_specs=[col, col], out_shape=[jax.ShapeDtypeStruct((S, D), F32), jax.ShapeDtypeStruct((S, D), MXU_DTYPE)],
        compiler_params=_params(("parallel",), _vmem_limit(5 * S * cb * 4, 4 * S * cb * 4)),
    )(a, b, proj)


def rg_scan_bwd(proj, dy, hs, a, name):
    S, D = a.shape
    cb = min(SCAN_COLS, D)
    goff = D // cb
    nchunks = S // CHUNK

    def body(g_ref, dy_ref, hs_ref, a_ref, dg_ref, gt_ref):
        gate, dy = g_ref[...], dy_ref[...]
        dg_ref[...] = (dy * hs_ref[...] * _gelu_grad(gate)).astype(dg_ref.dtype)
        gt_ref[...] = dy * _gelu(gate)
        row = lax.broadcasted_iota(jnp.int32, (CHUNK, cb), 0)

        def step(k, carry):
            c = nchunks - 1 - k
            r0 = pl.multiple_of(c * CHUNK, CHUNK)
            rn = pl.multiple_of(jnp.minimum(c + 1, nchunks - 1) * CHUNK, CHUNK)
            last = c == nchunks - 1
            nxt = jnp.where(last, 0.0, pltpu.roll(a_ref[pl.ds(rn, CHUNK), :], CHUNK - 1, 0))
            A = jnp.where(row < CHUNK - 1, pltpu.roll(a_ref[pl.ds(r0, CHUNK), :], CHUNK - 1, 0), nxt)
            B = gt_ref[pl.ds(r0, CHUNK), :]
            for d in (1, 2, 4):
                As = jnp.where(row < CHUNK - d, pltpu.roll(A, CHUNK - d, 0), 1.0)
                Bs = jnp.where(row < CHUNK - d, pltpu.roll(B, CHUNK - d, 0), 0.0)
                B = A * Bs + B
                A = A * As
            gt_ref[pl.ds(r0, CHUNK), :] = B + A * carry
            a_end = jnp.sum(jnp.where(row == 0, A, 0.0), axis=0, keepdims=True)
            b_end = jnp.sum(jnp.where(row == 0, B, 0.0), axis=0, keepdims=True)
            return b_end + a_end * carry

        lax.fori_loop(0, nchunks, step, jnp.zeros((1, cb), F32), unroll=SCAN_UNROLL)

    col = pl.BlockSpec((S, cb), lambda j: (0, j))
    return hbm_call(
        body, name=name, grid=(D // cb,), in_specs=[pl.BlockSpec((S, cb), lambda j: (0, goff + j)), col, col, col],
        out_specs=[col, col], out_shape=[jax.ShapeDtypeStruct((S, D), MXU_DTYPE), jax.ShapeDtypeStruct((S, D), F32)],
        compiler_params=_params(("parallel",), _vmem_limit(6 * S * cb * 4, 6 * S * cb * 4)),
    )(proj, dy, hs, a)


def rg_gates_bwd(gt, hs, xc, r, ig, w_rg, w_ig, lam, name):
    S, D = xc.shape
    nblk, bw, _ = w_rg.shape
    tr = min(RG_ROWS, S)

    def body(gt_ref, hs_ref, hp_ref, xc_ref, r_ref, i_ref, wr_ref, wi_ref, lam_ref,
             dxc_ref, dwr_ref, dwi_ref, dbr_ref, dbi_ref, dl_ref):
        step = pl.program_id(1)
        g, hs, xc, r, ig, lam = gt_ref[...], hs_ref[...], xc_ref[...], r_ref[...], i_ref[...], lam_ref[...]
        row = lax.broadcasted_iota(jnp.int32, g.shape, 0)
        hprev = _shift_down(hs, hp_ref[...], 1, row, step == 0)
        sp, _, a, em = _lru_coeffs(r, lam)
        mult = jnp.sqrt(em)
        du = g * mult
        dla = g * hprev * a - (g * (ig * xc)) * (a * a) / mult
        dpr = (dla * (-LRU_C * sp)) * (r * (1.0 - r))
        dpi = (du * xc) * (ig * (1.0 - ig))
        dprm, dpim = dpr.astype(MXU_DTYPE), dpi.astype(MXU_DTYPE)
        nt = (((1,), (1,)), ((), ()))
        dxc_ref[...] = (du * ig + lax.dot_general(dprm, wr_ref[...].astype(MXU_DTYPE), nt, preferred_element_type=F32)
                        + lax.dot_general(dpim, wi_ref[...].astype(MXU_DTYPE), nt, preferred_element_type=F32))

        @pl.when(step == 0)
        def _():
            for ref in (dwr_ref, dwi_ref, dbr_ref, dbi_ref, dl_ref):
                ref[...] = jnp.zeros_like(ref)

        xct = xc.T.astype(MXU_DTYPE)
        dwr_ref[...] += jnp.dot(xct, dprm, preferred_element_type=F32)
        dwi_ref[...] += jnp.dot(xct, dpim, preferred_element_type=F32)
        dbr_ref[...] += jnp.sum(dpr, axis=0, keepdims=True)
        dbi_ref[...] += jnp.sum(dpi, axis=0, keepdims=True)
        dl_ref[...] += jnp.sum(dla * (-LRU_C * r), axis=0, keepdims=True) * (-_sigmoid(-lam))

    tile = pl.BlockSpec((tr, bw), lambda n, i: (i, n))
    vec = pl.BlockSpec((1, bw), lambda n, i: (0, n))
    wblk = pl.BlockSpec((None, bw, bw), lambda n, i: (n, 0, 0))
    return hbm_call(
        body, name=name, grid=(nblk, S // tr),
        in_specs=[tile, tile, pl.BlockSpec((tr, bw), lambda n, i: (jnp.maximum(i - 1, 0), n)), tile, tile, tile, wblk, wblk, vec],
        out_specs=[tile, wblk, wblk, vec, vec, vec],
        out_shape=[jax.ShapeDtypeStruct((S, D), F32), jax.ShapeDtypeStruct((nblk, bw, bw), F32), jax.ShapeDtypeStruct((nblk, bw, bw), F32),
                   jax.ShapeDtypeStruct((1, D), F32), jax.ShapeDtypeStruct((1, D), F32), jax.ShapeDtypeStruct((1, D), F32)],
        compiler_params=_params(("parallel", "arbitrary")),
    )(gt, hs, hs, xc, r, ig, w_rg, w_ig, lam)


def rg_conv_bwd(proj, dxc, conv_w, name):
    S, D = dxc.shape
    bw = min(SCAN_COLS, D)
    tr = min(RG_ROWS, S)
    nsteps = S // tr

    def body(d_ref, dn_ref, xr_ref, xp_ref, cw_ref, dxr_ref, dcw_ref, dcb_ref):
        step = pl.program_id(1)
        d, xr = d_ref[...], xr_ref[...]
        row = lax.broadcasted_iota(jnp.int32, d.shape, 0)
        dxr = d * cw_ref[CONV_WIDTH - 1:CONV_WIDTH, :]
        for k in range(CONV_WIDTH - 1):
            dxr = dxr + _shift_up(d, dn_ref[...], CONV_WIDTH - 1 - k, row, step == nsteps - 1, tr) * cw_ref[k:k + 1, :]
        dxr_ref[...] = dxr.astype(dxr_ref.dtype)

        @pl.when(step == 0)
        def _():
            dcw_ref[...] = jnp.zeros_like(dcw_ref)
            dcb_ref[...] = jnp.zeros_like(dcb_ref)

        for k in range(CONV_WIDTH - 1):
            xs = _shift_down(xr, xp_ref[...], CONV_WIDTH - 1 - k, row, step == 0)
            dcw_ref[k:k + 1, :] += jnp.sum(d * xs, axis=0, keepdims=True)
        dcw_ref[CONV_WIDTH - 1:CONV_WIDTH, :] += jnp.sum(d * xr, axis=0, keepdims=True)
        dcb_ref[...] += jnp.sum(d, axis=0, keepdims=True)

    tile = pl.BlockSpec((tr, bw), lambda n, i: (i, n))
    cwb = pl.BlockSpec((CONV_WIDTH, bw), lambda n, i: (0, n))
    return hbm_call(
        body, name=name, grid=(D // bw, nsteps),
        in_specs=[tile, pl.BlockSpec((tr, bw), lambda n, i: (jnp.minimum(i + 1, nsteps - 1), n)), tile,
                  pl.BlockSpec((tr, bw), lambda n, i: (jnp.maximum(i - 1, 0), n)), cwb],
        out_specs=[tile, cwb, pl.BlockSpec((1, bw), lambda n, i: (0, n))],
        out_shape=[jax.ShapeDtypeStruct((S, D), MXU_DTYPE), jax.ShapeDtypeStruct((CONV_WIDTH, D), F32), jax.ShapeDtypeStruct((1, D), F32)],
        compiler_params=_params(("parallel", "arbitrary")),
    )(dxc, dxc, proj, proj, conv_w)


def rope_table(S):
    half = ROT_DIM // 2
    pos = jnp.arange(S, dtype=F32)
    inv = ROPE_THETA ** (-jnp.arange(0, ROT_DIM, 2, dtype=F32) / ROT_DIM)
    ang = pos[:, None] * inv[None, :]
    cos, sin = jnp.cos(ang), jnp.sin(ang)
    zero = jnp.zeros((S, HEAD_DIM - ROT_DIM), F32)
    c = jnp.concatenate([cos, cos, zero + 1.0], axis=1)
    a = jnp.concatenate([-sin, jnp.zeros((S, half), F32), zero], axis=1)
    b = jnp.concatenate([jnp.zeros((S, half), F32), sin, zero], axis=1)
    return jnp.stack([jnp.tile(t, (1, LANES // HEAD_DIM)) for t in (c, a, b)])


def _rope(t, tab):
    half = ROT_DIM // 2
    return t * tab[0] + pltpu.roll(t, LANES - half, 1) * tab[1] + pltpu.roll(t, half, 1) * tab[2]


def _rope_t(d, tab):
    half = ROT_DIM // 2
    return d * tab[0] + pltpu.roll(d * tab[1], half, 1) + pltpu.roll(d * tab[2], LANES - half, 1)


def _dup_head(t, hk, lo):
    sw = pltpu.roll(t, HEAD_DIM, 1)
    return jnp.where(lo, t, sw) if hk == 0 else jnp.where(lo, sw, t)


def _attn_common(n, sink_ref, q_ref, kp_ref, kc_ref, vp_ref, vc_ref, tc_ref, tp_ref, hk, pairs):
    tq = (tc_ref[0], tc_ref[1], tc_ref[2])
    tp = (tp_ref[0], tp_ref[1], tp_ref[2])
    lo = lax.broadcasted_iota(jnp.int32, (WINDOW, LANES), 1) < HEAD_DIM
    lo2 = lax.broadcasted_iota(jnp.int32, (2 * WINDOW, LANES), 1) < HEAD_DIM
    kband = jnp.concatenate([_rope(kp_ref[...], tp), _rope(kc_ref[...], tq)], axis=0)
    vband = jnp.concatenate([vp_ref[...], vc_ref[...]], axis=0)
    kd = _dup_head(kband, hk, lo2).astype(MXU_DTYPE)
    vd = _dup_head(vband, hk, lo2).astype(MXU_DTYPE)
    rows, sks = [], []
    for j in range(pairs):
        col = hk * pairs + j
        qp = _rope(q_ref[:, col * LANES:(col + 1) * LANES], tq)
        rows += [jnp.where(lo, qp, 0.0), jnp.where(lo, 0.0, qp)]
        sks += [jnp.full((WINDOW, 1), sink_ref[2 * col], F32), jnp.full((WINDOW, 1), sink_ref[2 * col + 1], F32)]
    qg = jnp.concatenate(rows, axis=0)
    sk = jnp.concatenate(sks, axis=0)
    G = 2 * pairs * WINDOW
    own = lax.broadcasted_iota(jnp.int32, (G, WINDOW), 1) <= (lax.broadcasted_iota(jnp.int32, (G, WINDOW), 0) & (WINDOW - 1))
    s = lax.dot_general(qg.astype(MXU_DTYPE), kd, (((1,), (1,)), ((), ())), preferred_element_type=F32) * (HEAD_DIM ** -0.5)
    s = jnp.where(own, s[:, WINDOW:], s[:, :WINDOW] + jnp.where(n > 0, 0.0, NEG_INF))
    m = jnp.maximum(jnp.max(s, axis=1, keepdims=True), sk)
    e = jnp.exp(s - m)
    es = jnp.exp(sk - m)
    inv = 1.0 / (jnp.sum(e, axis=1, keepdims=True) + es)
    return qg, kd, vd, e * inv, es * inv, own, lo, lo2, tq, tp


def _unfold_band(t, own):
    return jnp.concatenate([jnp.where(own, 0.0, t), jnp.where(own, t, 0.0)], axis=1)


def _attn_specs(D, NB):
    kcol = 3 * D // LANES
    q = pl.BlockSpec((WINDOW, D), lambda n: (n, 2))
    kc = pl.BlockSpec((WINDOW, LANES), lambda n: (n, kcol))
    kp = pl.BlockSpec((WINDOW, LANES), lambda n: (jnp.maximum(n - 1, 0), kcol))
    vc = pl.BlockSpec((WINDOW, LANES), lambda n: (n, kcol + 1))
    vp = pl.BlockSpec((WINDOW, LANES), lambda n: (jnp.maximum(n - 1, 0), kcol + 1))
    tc = pl.BlockSpec((3, WINDOW, LANES), lambda n: (0, n, 0))
    tp = pl.BlockSpec((3, WINDOW, LANES), lambda n: (0, jnp.maximum(n - 1, 0), 0))
    sink = pl.BlockSpec(memory_space=pltpu.SMEM)
    return [sink, q, kp, kc, vp, vc, tc, tp]


def attn_fwd(proj, sinks, tab, D, name):
    S = proj.shape[0]
    NB = S // WINDOW
    pairs = D // HEAD_DIM // N_KV_HEADS // 2

    def body(sink_ref, q_ref, kp_ref, kc_ref, vp_ref, vc_ref, tc_ref, tp_ref, o_ref):
        n = pl.program_id(0)
        for hk in range(N_KV_HEADS):
            _, _, vd, p, _, own, lo, _, _, _ = _attn_common(n, sink_ref, q_ref, kp_ref, kc_ref, vp_ref, vc_ref, tc_ref, tp_ref, hk, pairs)
            o = jnp.dot(_unfold_band(p, own).astype(MXU_DTYPE), vd, preferred_element_type=F32)
            for j in range(pairs):
                col = hk * pairs + j
                oa = o[(2 * j) * WINDOW:(2 * j + 1) * WINDOW]
                ob = o[(2 * j + 1) * WINDOW:(2 * j + 2) * WINDOW]
                o_ref[:, col * LANES:(col + 1) * LANES] = jnp.where(lo, oa, ob)

    return hbm_call(
        body, name=name, grid=(NB,), in_specs=_attn_specs(D, NB),
        out_specs=pl.BlockSpec((WINDOW, D), lambda n: (n, 0)), out_shape=jax.ShapeDtypeStruct((S, D), F32),
        compiler_params=_params(("parallel",)),
    )(sinks, proj, proj, proj, proj, proj, tab, tab)


def attn_bwd(proj, sinks, tab, o, do, D, name):
    S = proj.shape[0]
    NB = S // WINDOW
    pairs = D // HEAD_DIM // N_KV_HEADS // 2

    def body(sink_ref, q_ref, kp_ref, kc_ref, vp_ref, vc_ref, tc_ref, tp_ref, o_ref, do_ref, dq_ref, dk_ref, dv_ref, ds_ref):
        n = pl.program_id(0)

        @pl.when(n == 0)
        def _():
            ds_ref[...] = jnp.zeros_like(ds_ref)

        lane1 = lax.broadcasted_iota(jnp.int32, (1, LANES), 1)
        dsink = jnp.zeros((1, LANES), F32)
        dkt = dvt = None
        for hk in range(N_KV_HEADS):
            qg, kd, vd, p, ps, own, lo, lo2, tq, tp = _attn_common(n, sink_ref, q_ref, kp_ref, kc_ref, vp_ref, vc_ref, tc_ref, tp_ref, hk, pairs)
            dos, os_ = [], []
            for j in range(pairs):
                col = hk * pairs + j
                dop = do_ref[:, col * LANES:(col + 1) * LANES]
                op = o_ref[:, col * LANES:(col + 1) * LANES]
                dos += [jnp.where(lo, dop, 0.0), jnp.where(lo, 0.0, dop)]
                os_ += [jnp.where(lo, op, 0.0), jnp.where(lo, 0.0, op)]
            dog = jnp.concatenate(dos, axis=0)
            og = jnp.concatenate(os_, axis=0)
            dogm = dog.astype(MXU_DTYPE)
            dp = lax.dot_general(dogm, vd, (((1,), (1,)), ((), ())), preferred_element_type=F32)
            dp = jnp.where(own, dp[:, WINDOW:], dp[:, :WINDOW])
            dr = jnp.sum(dog * og, axis=1, keepdims=True)
            ds = _unfold_band(p * (dp - dr) * (HEAD_DIM ** -0.5), own)
            dsm = ds.astype(MXU_DTYPE)
            dqg = jnp.dot(dsm, kd, preferred_element_type=F32)
            dkd = jnp.dot(ds.T.astype(MXU_DTYPE), qg.astype(MXU_DTYPE), preferred_element_type=F32)
            dvd = jnp.dot(_unfold_band(p, own).T.astype(MXU_DTYPE), dogm, preferred_element_type=F32)
            dkf = dkd + pltpu.roll(dkd, HEAD_DIM, 1)
            dvf = dvd + pltpu.roll(dvd, HEAD_DIM, 1)
            if hk == 0:
                dkt, dvt = dkf, dvf
            else:
                dkt, dvt = jnp.where(lo2, dkt, dkf), jnp.where(lo2, dvt, dvf)
            sd = ps * dr
            for j in range(pairs):
                col = hk * pairs + j
                dqa = dqg[(2 * j) * WINDOW:(2 * j + 1) * WINDOW]
                dqb = dqg[(2 * j + 1) * WINDOW:(2 * j + 2) * WINDOW]
                dq_ref[:, col * LANES:(col + 1) * LANES] = _rope_t(jnp.where(lo, dqa, dqb), tq).astype(dq_ref.dtype)
                for t in range(2):
                    part = sd[(2 * j + t) * WINDOW:(2 * j + t + 1) * WINDOW]
                    val = jnp.sum(part, axis=0, keepdims=True)
                    dsink = dsink - jnp.where(lane1 == 2 * col + t, val, 0.0)
        dk_ref[...] = jnp.concatenate([_rope_t(dkt[:WINDOW], tp), _rope_t(dkt[WINDOW:], tq)], axis=0)
        dv_ref[...] = dvt
        ds_ref[...] += dsink

    blk = pl.BlockSpec((WINDOW, D), lambda n: (n, 0))
    band = pl.BlockSpec((None, 2 * WINDOW, LANES), lambda n: (n, 0, 0))
    return hbm_call(
        body, name=name, grid=(NB,), in_specs=_attn_specs(D, NB) + [blk, blk],
        out_specs=[blk, band, band, pl.BlockSpec((1, LANES), lambda n: (0, 0))],
        out_shape=[jax.ShapeDtypeStruct((S, D), MXU_DTYPE), jax.ShapeDtypeStruct((NB, 2 * WINDOW, LANES), F32),
                   jax.ShapeDtypeStruct((NB, 2 * WINDOW, LANES), F32), jax.ShapeDtypeStruct((1, LANES), F32)],
        compiler_params=_params(("arbitrary",)),
    )(sinks, proj, proj, proj, proj, proj, tab, tab, o, do)


def band_fold(dkb, dvb, name):
    NB = dkb.shape[0]
    k4 = dkb.reshape(NB, 2, WINDOW, LANES)
    v4 = dvb.reshape(NB, 2, WINDOW, LANES)

    def body(kc_ref, kn_ref, vc_ref, vn_ref, dk_ref, dv_ref):
        more = pl.program_id(0) < NB - 1
        dk_ref[...] = (kc_ref[...] + jnp.where(more, kn_ref[...], 0.0)).astype(dk_ref.dtype)
        dv_ref[...] = (vc_ref[...] + jnp.where(more, vn_ref[...], 0.0)).astype(dv_ref.dtype)

    cur = pl.BlockSpec((None, None, WINDOW, LANES), lambda n: (n, 1, 0, 0))
    nxt = pl.BlockSpec((None, None, WINDOW, LANES), lambda n: (jnp.minimum(n + 1, NB - 1), 0, 0, 0))
    out = pl.BlockSpec((WINDOW, LANES), lambda n: (n, 0))
    sds = jax.ShapeDtypeStruct((NB * WINDOW, LANES), MXU_DTYPE)
    return hbm_call(body, name=name, grid=(NB,), in_specs=[cur, nxt, cur, nxt], out_specs=[out, out], out_shape=[sds, sds],
                          compiler_params=_params(("parallel",)))(k4, k4, v4, v4)


CROSS_ROWS = 512


def _cross_probs(q, k, scale):
    s = lax.dot_general(q.astype(MXU_DTYPE), k.astype(MXU_DTYPE), (((1,), (1,)), ((), ())), preferred_element_type=F32) * scale
    e = jnp.exp(s - jnp.max(s, axis=1, keepdims=True))
    return e / jnp.sum(e, axis=1, keepdims=True)


def cross_fwd(qc, kv, name):
    S, D = qc.shape
    M = kv.shape[0]
    hd = D // CROSS_HEADS
    tq = min(CROSS_ROWS, S)

    def body(q_ref, kv_ref, o_ref):
        for h in range(CROSS_HEADS):
            p = _cross_probs(q_ref[:, h * hd:(h + 1) * hd], kv_ref[:, h * hd:(h + 1) * hd], hd ** -0.5)
            v = kv_ref[:, D + h * hd:D + (h + 1) * hd].astype(MXU_DTYPE)
            o_ref[:, h * hd:(h + 1) * hd] = jnp.dot(p.astype(MXU_DTYPE), v, preferred_element_type=F32).astype(o_ref.dtype)

    return hbm_call(
        body, name=name, grid=(S // tq,), in_specs=[pl.BlockSpec((tq, D), lambda i: (i, 0)), pl.BlockSpec((M, 2 * D), lambda i: (0, 0))],
        out_specs=pl.BlockSpec((tq, D), lambda i: (i, 0)), out_shape=jax.ShapeDtypeStruct((S, D), MXU_DTYPE),
        compiler_params=_params(("parallel",)),
    )(qc, kv)


def cross_bwd(qc, kv, do, name):
    S, D = qc.shape
    M = kv.shape[0]
    hd = D // CROSS_HEADS
    tq = min(CROSS_ROWS, S)

    def body(q_ref, kv_ref, do_ref, dq_ref, dkv_ref):
        @pl.when(pl.program_id(0) == 0)
        def _():
            dkv_ref[...] = jnp.zeros_like(dkv_ref)

        for h in range(CROSS_HEADS):
            q = q_ref[:, h * hd:(h + 1) * hd]
            k = kv_ref[:, h * hd:(h + 1) * hd]
            v = kv_ref[:, D + h * hd:D + (h + 1) * hd].astype(MXU_DTYPE)
            dom = do_ref[:, h * hd:(h + 1) * hd].astype(MXU_DTYPE)
            p = _cross_probs(q, k, hd ** -0.5)
            dp = lax.dot_general(dom, v, (((1,), (1,)), ((), ())), preferred_element_type=F32)
            ds = p * (dp - jnp.sum(p * dp, axis=1, keepdims=True)) * (hd ** -0.5)
            dq_ref[:, h * hd:(h + 1) * hd] = jnp.dot(ds.astype(MXU_DTYPE), k.astype(MXU_DTYPE),
                                                     preferred_element_type=F32).astype(dq_ref.dtype)
            dkv_ref[:, h * hd:(h + 1) * hd] += jnp.dot(ds.T.astype(MXU_DTYPE), q.astype(MXU_DTYPE), preferred_element_type=F32)
            dkv_ref[:, D + h * hd:D + (h + 1) * hd] += jnp.dot(p.T.astype(MXU_DTYPE), dom, preferred_element_type=F32)

    row = pl.BlockSpec((tq, D), lambda i: (i, 0))
    full = pl.BlockSpec((M, 2 * D), lambda i: (0, 0))
    return hbm_call(
        body, name=name, grid=(S // tq,), in_specs=[row, full, row], out_specs=[row, full],
        out_shape=[jax.ShapeDtypeStruct((S, D), MXU_DTYPE), jax.ShapeDtypeStruct((M, 2 * D), F32)],
        compiler_params=_params(("arbitrary",)),
    )(qc, kv, do)


def adamw(w, g, m, v, name, layers=None, into=None, g_row0=0):
    shape = w.shape
    cols = shape[-1]
    lead = shape[0] if len(shape) > 2 else 1
    rows = int(np.prod(shape[:-1])) // lead
    w2, m2, v2 = (t.reshape(lead, rows, cols) for t in (w, m, v))
    g2 = g.reshape(lead, -1, cols)
    tr = _divisors(rows, SUBLANES, max(SUBLANES, (1 << 20) // (cols * 4) // SUBLANES * SUBLANES))[0]
    assert g_row0 % tr == 0
    lo, hi = layers or (0, lead)
    done = [t.reshape(lead, rows, cols) for t in into] if into else []

    def body(w_ref, g_ref, m_ref, v_ref, *refs):
        d_ref, mo_ref, vo_ref, go_ref = refs[len(done):]
        gg = g_ref[...]
        mn = ADAM_B1 * m_ref[...] + (1.0 - ADAM_B1) * gg
        vn = ADAM_B2 * v_ref[...] + (1.0 - ADAM_B2) * (gg * gg)
        m_hat = mn / (1.0 - ADAM_B1 ** ADAM_STEP)
        v_hat = vn / (1.0 - ADAM_B2 ** ADAM_STEP)
        d_ref[...] = -ADAM_LR * (m_hat / (jnp.sqrt(v_hat) + ADAM_EPS) + ADAM_WD * w_ref[...])
        mo_ref[...] = mn
        vo_ref[...] = vn
        go_ref[...] = gg

    blk = pl.BlockSpec((None, tr, cols), lambda l, i: (l + lo, i, 0))
    sds = jax.ShapeDtypeStruct((lead, rows, cols), F32)
    gblk = pl.BlockSpec((None, tr, cols), lambda l, i: (l + lo, i + g_row0 // tr, 0))
    d, mn, vn, go = hbm_call(body, name=name, grid=(hi - lo, rows // tr), in_specs=[blk, gblk, blk, blk] + [pl.BlockSpec(memory_space=pl.ANY)] * len(done),
                             out_specs=[blk] * 4, out_shape=[sds] * 4, input_output_aliases={4 + k: k for k in range(len(done))},
                             compiler_params=_params(("parallel", "parallel")))(w2, g2, m2, v2, *done)
    return d.reshape(shape), mn.reshape(shape), vn.reshape(shape), go.reshape(shape)


def sum_devices(parts, name):
    n, rows, cols = parts.shape

    def body(p_ref, o_ref):
        acc = p_ref[0]
        for k in range(1, n):
            acc = acc + p_ref[k]
        o_ref[...] = acc

    return pl.pallas_call(body, name=name, in_specs=[pl.BlockSpec(memory_space=pltpu.VMEM)],
                          out_specs=pl.BlockSpec(memory_space=pltpu.VMEM), out_shape=jax.ShapeDtypeStruct((rows, cols), F32))(parts)


HBM_SPEC = pl.BlockSpec(memory_space=pltpu.HBM)


def _place():
    return lax.axis_index("x"), lax.axis_index("y"), lax.axis_index("c")


def _remote(src, dst, send_sems, recv_sems, k, to):
    return pltpu.make_async_remote_copy(src_ref=src, dst_ref=dst, send_sem=send_sems.at[k], recv_sem=recv_sems.at[k],
                                        device_id=to, device_id_type=MESH_ID)


SEM_SPEC = pl.BlockSpec(memory_space=pltpu.SEMAPHORE)
ANY_SPEC = pl.BlockSpec(memory_space=pl.ANY)
SPLIT_COPY = pltpu.CompilerParams(has_side_effects=pltpu.SideEffectType.DATAFLOW_SIDE_EFFECTING)


def _in_hbm(arrays):
    return [pltpu.with_memory_space_constraint(a, pltpu.HBM) for a in arrays]


def _split_start(copies, sources, lands, after, n_sems, name):
    n = len(sources)

    def body(*refs):
        for cp in copies(refs[:n], refs[n:2 * n], refs[2 * n + 1], refs[2 * n + 2]):
            cp.start()
        refs[-1][...] = jnp.zeros_like(refs[-1])

    through = [pltpu.HBM(a.shape, a.dtype) for a in list(sources) + list(lands)]
    outs = pl.pallas_call(
        body, name=name, in_specs=[HBM_SPEC] * (2 * n) + [ANY_SPEC],
        out_specs=[SEM_SPEC, SEM_SPEC] + [HBM_SPEC] * (2 * n) + [pl.BlockSpec(memory_space=pltpu.VMEM)],
        out_shape=[pltpu.SemaphoreType.DMA((n_sems,)), pltpu.SemaphoreType.DMA((n_sems,))] + through
        + [jax.ShapeDtypeStruct((SUBLANES, LANES), F32)],
        input_output_aliases={i: 2 + i for i in range(2 * n)}, compiler_params=SPLIT_COPY,
    )(*_in_hbm(sources), *_in_hbm(lands), after)
    return outs[0], outs[1], outs[2:2 + n], outs[2 + n:2 + 2 * n], outs[-1]


def _split_wait(copies, send_sems, recv_sems, sources, lands, after, name):
    n = len(sources)

    def body(*refs):
        for cp in copies(refs[:n], refs[n:2 * n], refs[2 * n], refs[2 * n + 1]):
            cp.wait_send()
            cp.wait_recv()

    through = [pltpu.HBM(a.shape, a.dtype) for a in list(sources) + list(lands)]
    outs = pl.pallas_call(
        body, name=name, in_specs=[HBM_SPEC] * (2 * n) + [SEM_SPEC, SEM_SPEC, ANY_SPEC], out_specs=[HBM_SPEC] * (2 * n),
        out_shape=through, input_output_aliases={i: i for i in range(2 * n)}, compiler_params=SPLIT_COPY,
    )(*sources, *lands, send_sems, recv_sems, after)
    return outs[:n], outs[n:]


def _chip_slab(land, slot, rows):
    return land.at[slot, rows] if len(land.shape) == 3 else land.at[rows, slot]


def _gather_copies(w_refs, land_refs, send_sems, recv_sems):
    n = len(w_refs)
    x, y, c = _place()
    chips = [(1 - x, y), (x, 1 - y), (1 - x, 1 - y)]
    cps = []
    for a in range(n):
        hr = w_refs[a].shape[0] // 2
        mine, every = pl.ds(c * hr, hr), pl.ds(0, 2 * hr)
        cps.append(_remote(w_refs[a], _chip_slab(land_refs[a], 2 * x + y, every), send_sems, recv_sems, 3 * n + a, (x, y, 1 - c)))
        for k, chip in enumerate(chips):
            cps.append(_remote(w_refs[a].at[mine], _chip_slab(land_refs[a], 2 * x + y, mine), send_sems, recv_sems, 3 * a + k, (*chip, c)))
    return cps


def gather_start(shards, after, name):
    lands = [lax.empty(s.shape[:-2] + (N_CHIPS,) + s.shape[-2:], s.dtype) for s in shards]
    return _split_start(_gather_copies, shards, lands, after, 4 * len(shards), name)


def gather_wait(state, after, name):
    send_sems, recv_sems, sources, lands, _ = state
    return _split_wait(_gather_copies, send_sems, recv_sems, sources, lands, after, name)[1]


def gather_pass(lands, name):
    n = len(lands)

    def body(*refs):
        out_refs, send_sems, recv_sems = refs[n:2 * n], refs[2 * n], refs[2 * n + 1]
        x, y, c = _place()
        chips = [(1 - x, y), (x, 1 - y), (1 - x, 1 - y)]
        sent = []
        for a in range(n):
            hr = out_refs[a].shape[0 if len(out_refs[a].shape) == 4 else 1] // 2
            for k, (px, py) in enumerate(chips):
                landed = _chip_slab(out_refs[a], 2 * px + py, pl.ds(c * hr, hr))
                sent.append(_remote(landed, landed, send_sems, recv_sems, 3 * a + k, (x, y, 1 - c)))
        for cp in sent:
            cp.start()
        for a in range(n):
            hr = out_refs[a].shape[0 if len(out_refs[a].shape) == 4 else 1] // 2
            for k, (px, py) in enumerate(chips):
                theirs = _chip_slab(out_refs[a], 2 * px + py, pl.ds((1 - c) * hr, hr))
                _remote(theirs, theirs, send_sems, recv_sems, 3 * a + k, (x, y, 1 - c)).wait_recv()
        for cp in sent:
            cp.wait_send()

    return hbm_call(
        body, name=name, in_specs=[HBM_SPEC] * n, out_specs=[HBM_SPEC] * n,
        out_shape=[jax.ShapeDtypeStruct(a.shape, a.dtype) for a in lands], input_output_aliases={a: a for a in range(n)},
        scratch_shapes=[pltpu.SemaphoreType.DMA((3 * n,))] * 2,
    )(*lands)


def _scatter_copies(t_refs, land_refs, send_sems, recv_sems):
    x, y, c = _place()
    chips = [(1 - x, y), (x, 1 - y), (1 - x, 1 - y)]
    return [_remote(t_refs[a].at[:, 2 * px + py], land_refs[a].at[:, k], send_sems, recv_sems, 3 * a + k, (px, py, c))
            for a in range(len(t_refs)) for k, (px, py) in enumerate(chips)]


def scatter_start(parts, after, name):
    lands = [lax.empty((t.shape[0], N_CHIPS - 1) + t.shape[2:], t.dtype) for t in parts]
    return _split_start(_scatter_copies, parts, lands, after, 3 * len(parts), name)


def scatter_wait(state, after, name):
    send_sems, recv_sems, sources, lands, _ = state
    return _split_wait(_scatter_copies, send_sems, recv_sems, sources, lands, after, name)


def swap_sibling(parts, name):
    n = len(parts)

    def body(*refs):
        v_refs, out_refs, send_sems, recv_sems = refs[:n], refs[n:2 * n], refs[2 * n], refs[2 * n + 1]
        x, y, c = _place()
        cps = []
        for a in range(n):
            hr = v_refs[a].shape[2] // 2
            cps.append(_remote(v_refs[a].at[:, :, pl.ds((1 - c) * hr, hr)], out_refs[a], send_sems, recv_sems, a, (x, y, 1 - c)))
        for cp in cps:
            cp.start()
        for cp in cps:
            cp.wait()

    return hbm_call(
        body, name=name, in_specs=[HBM_SPEC] * n, out_specs=[HBM_SPEC] * n,
        out_shape=[jax.ShapeDtypeStruct(v.shape[:2] + (v.shape[2] // 2, v.shape[3]), v.dtype) for v in parts],
        scratch_shapes=[pltpu.SemaphoreType.DMA((n,))] * 2,
    )(*parts)


def join_halves(halves, layer, name):
    n = len(halves)

    def body(*refs):
        out_refs, send_sems, recv_sems = refs[n:2 * n], refs[2 * n], refs[2 * n + 1]
        x, y, c = _place()
        cps = []
        for a in range(n):
            hr = out_refs[a].shape[1] // 2
            mine = out_refs[a].at[layer, pl.ds(c * hr, hr)]
            cps.append(_remote(mine, mine, send_sems, recv_sems, a, (x, y, 1 - c)))
        for cp in cps:
            cp.start()
        for a in range(n):
            hr = out_refs[a].shape[1] // 2
            theirs = out_refs[a].at[layer, pl.ds((1 - c) * hr, hr)]
            _remote(theirs, theirs, send_sems, recv_sems, a, (x, y, 1 - c)).wait_recv()
        for cp in cps:
            cp.wait_send()

    return hbm_call(
        body, name=name, in_specs=[HBM_SPEC] * n, out_specs=[HBM_SPEC] * n,
        out_shape=[jax.ShapeDtypeStruct(f.shape, f.dtype) for f in halves], input_output_aliases={a: a for a in range(n)},
        scratch_shapes=[pltpu.SemaphoreType.DMA((n,))] * 2,
    )(*halves)


def gather_devices(v, name, after=()):
    def body(v_ref, *refs):
        out_ref, send_sems, recv_sems, local_sem = refs[len(after):]
        x, y, c = _place()
        me = 4 * x + 2 * y + c
        own = pltpu.make_async_copy(v_ref, out_ref.at[me], local_sem)
        own.start()
        peers = [((x + dx) % 2, (y + dy) % 2, (c + dc) % 2) for dx in (0, 1) for dy in (0, 1) for dc in (0, 1)][1:]
        sent = []
        for k, peer in enumerate(peers):
            cp = pltpu.make_async_remote_copy(src_ref=v_ref, dst_ref=out_ref.at[me], send_sem=send_sems.at[k], recv_sem=recv_sems.at[k],
                                              device_id=peer, device_id_type=MESH_ID)
            cp.start()
            sent.append(cp)
        for k, (px, py, pc) in enumerate(peers):
            slot = out_ref.at[4 * px + 2 * py + pc]
            pltpu.make_async_remote_copy(src_ref=slot, dst_ref=slot, send_sem=send_sems.at[k], recv_sem=recv_sems.at[k],
                                         device_id=(px, py, pc), device_id_type=MESH_ID).wait_recv()
        for cp in sent:
            cp.wait_send()
        own.wait()

    vm = pl.BlockSpec(memory_space=pltpu.VMEM)
    return pl.pallas_call(body, name=name, in_specs=[vm] + [ANY_SPEC] * len(after), out_specs=vm,
                          out_shape=jax.ShapeDtypeStruct((N_DEV,) + v.shape, v.dtype),
                          scratch_shapes=[pltpu.SemaphoreType.DMA((N_DEV - 1,)), pltpu.SemaphoreType.DMA((N_DEV - 1,)),
                                          pltpu.SemaphoreType.DMA])(v, *after)


ADD_ROWS = 512


def add_pair(place, a, b, name):
    L, n, hr, cols = b.shape
    tr = _divisors(hr, 2 * SUBLANES, ADD_ROWS)[0]
    nb = hr // tr

    def body(p_ref, a_ref, b_ref, o_ref):
        del p_ref
        o_ref[...] = (a_ref[...].astype(F32) + b_ref[...].astype(F32)).astype(o_ref.dtype)

    blk = pl.BlockSpec((None, None, tr, cols), lambda l, d, i, p: (l, d, i, 0))
    grid_spec = pltpu.PrefetchScalarGridSpec(
        num_scalar_prefetch=1, grid=(L, n, nb),
        in_specs=[pl.BlockSpec((None, None, tr, cols), lambda l, d, i, p: (l, d, p[0] * nb + i, 0)), blk], out_specs=blk)
    return hbm_call(body, name=name, grid_spec=grid_spec, out_shape=jax.ShapeDtypeStruct(b.shape, b.dtype),
                          compiler_params=_params(("parallel", "parallel", "parallel")))(place, a, b)


def add_chips(place, own, others, layer, stacked, name):
    _, n, hr, cols = others.shape
    tr = _divisors(hr, 2 * SUBLANES, ADD_ROWS)[0]
    nb = hr // tr
    create = isinstance(stacked, tuple)

    def body(p_ref, own_ref, *refs):
        del p_ref
        acc = own_ref[...].astype(F32)
        for k in range(n):
            acc = acc + refs[k][...].astype(F32)
        refs[-1][...] = acc

    ins = [pl.BlockSpec((None, None, tr, cols), lambda i, p: (0, p[1], i, 0))]
    ins += [pl.BlockSpec((None, None, tr, cols), functools.partial(lambda k, i, p: (0, k, i, 0), k)) for k in range(n)]
    grid_spec = pltpu.PrefetchScalarGridSpec(num_scalar_prefetch=1, grid=(nb,), in_specs=ins + ([] if create else [ANY_SPEC]),
                                             out_specs=pl.BlockSpec((None, tr, cols), lambda i, p: (layer, p[0] * nb + i, 0)))
    shape = stacked if create else stacked.shape
    return hbm_call(body, name=name, grid_spec=grid_spec, out_shape=jax.ShapeDtypeStruct(shape, F32),
                          input_output_aliases={} if create else {n + 2: 0},
                          compiler_params=_params(("parallel",)))(place, own, *([others] * n), *([] if create else [stacked]))


def _alpha(depth):
    return (2 * depth) ** 0.25


def _wmm(a, weight, mode, name, deps=(), **more):
    arr, how = weight
    return mm(a, arr, mode, name, deps=deps, **how, **more)


def layer_fwd(h, mem, w, tab, alpha, deps=(), late=None):
    D = h.shape[1]
    proj = _wmm(h, w["w_in"], "nt", "mm_proj", deps)
    xc, r, ig, a, b = rg_gates_fwd(proj, w["conv_w"], w["conv_b"], w["w_rg"], w["b_rg"], w["w_ig"], w["b_ig"], w["lru_lambda"], "rg_gates_fwd")
    hs, y_rnn = rg_scan_fwd(proj, a, b, "rg_scan_fwd")
    y_attn = attn_fwd(proj, w["sinks"], tab, D, "attn_fwd")
    deps = ()
    if late is not None:
        rest, deps = late(y_attn)
        w = {**w, **rest}
    pr = _wmm(y_rnn, w["w_br_rnn"], "nn", "mm_br_rnn", deps)
    pa = _wmm(y_attn, w["w_br_attn"], "nn", "mm_br_attn")
    merged = merge_fwd(proj, pr, pa, "merge_fwd")
    h1, xh1, rs1 = _wmm(merged, w["w_out"], "nn", "mm_out_ln1", post_norm=(h, w["ln1_g"], w["ln1_b"], alpha))
    qc = _wmm(h1, w["cq_w"], "nn", "mm_cq", out_dtype=MXU_DTYPE)
    kv = _wmm(mem, w["ckv_w"], "nn", "mm_ckv", out_dtype=MXU_DTYPE)
    o = cross_fwd(qc, kv, "cross_fwd")
    h2, xh2, rs2 = _wmm(o, w["co_w"], "nn", "mm_co_ln2", post_norm=(h1, w["ln2_g"], w["ln2_b"], alpha))
    gu = _wmm(h2, w["ffn_wi"], "nn", "mm_ffn_wi", out_blocks=2)
    act = swiglu_fwd(gu, "swiglu_fwd")
    h3, xh3, rs3 = _wmm(act, w["ffn_wo"], "nn", "mm_ffn_wo_ln3", post_norm=(h2, w["ln3_g"], w["ln3_b"], alpha))
    saved = dict(h=h, proj=proj, xc=xc, r=r, ig=ig, a=a, hs=hs, y_rnn=y_rnn, y_attn=y_attn, pr=pr, pa=pa, xh1=xh1, rs1=rs1, h1=h1,
                 qc=qc, kv=kv, o=o, xh2=xh2, rs2=rs2, h2=h2, gu=gu, xh3=xh3, rs3=rs3, merged=merged, act=act)
    return h3, saved, w


def layer_bwd(dh, mem, w, s, tab, alpha, deps=(), halfway=None):
    D = dh.shape[1]
    g = {}
    wg = dict(out_dtype=MXU_DTYPE)
    dz3, g["ln3_g"], g["ln3_b"] = ln_bwd(dh, None, s["xh3"], s["rs3"], w["ln3_g"], 1.0, "ln3_bwd")
    g["ffn_wo"] = mm(s["act"], dz3, "tn", "mm_d_ffn_wo", deps=deps, **wg)
    dact = _wmm(dz3, w["ffn_wo"], "nt", "mm_dact")
    dgu = swiglu_bwd(s["gu"], dact, "swiglu_bwd")
    g["ffn_wi"] = mm(s["h2"], dgu, "tn", "mm_d_ffn_wi", b_blocks=2, out_blocks=N_CHIPS, **wg)
    dh2 = _wmm(dgu, w["ffn_wi"], "nt", "mm_dh2", a_blocks=2)
    dz2, g["ln2_g"], g["ln2_b"] = ln_bwd(dz3, dh2, s["xh2"], s["rs2"], w["ln2_g"], alpha, "ln2_bwd")
    g["co_w"] = mm(s["o"], dz2, "tn", "mm_d_co", **wg)
    do = _wmm(dz2, w["co_w"], "nt", "mm_do", out_dtype=MXU_DTYPE)
    dqc, dkv = cross_bwd(s["qc"], s["kv"], do, "cross_bwd")
    g["cq_w"] = mm(s["h1"], dqc, "tn", "mm_d_cq", **wg)
    g["ckv_w"] = mm(mem, dkv, "tn", "mm_d_ckv", out_blocks=N_CHIPS, **wg)
    dh1 = _wmm(dqc, w["cq_w"], "nt", "mm_dh1")
    deps = halfway(g, dh1) if halfway is not None else ()
    dz1, g["ln1_g"], g["ln1_b"] = ln_bwd(dz2, dh1, s["xh1"], s["rs1"], w["ln1_g"], alpha, "ln1_bwd")
    g["w_out"] = mm(s["merged"], dz1, "tn", "mm_d_out", deps=deps, **wg)
    dm = _wmm(dz1, w["w_out"], "nt", "mm_dmerged")
    dpr, dpa, dg_rnn, dg_attn = merge_bwd(s["proj"], s["pr"], s["pa"], dm, "merge_bwd")
    g["w_br_rnn"] = mm(s["y_rnn"], dpr, "tn", "mm_d_br_rnn", **wg)
    g["w_br_attn"] = mm(s["y_attn"], dpa, "tn", "mm_d_br_attn", **wg)
    dy_rnn = _wmm(dpr, w["w_br_rnn"], "nt", "mm_dy_rnn")
    dy_attn = _wmm(dpa, w["w_br_attn"], "nt", "mm_dy_attn")
    dq, dkb, dvb, dsink = attn_bwd(s["proj"], w["sinks"], tab, s["y_attn"], dy_attn, D, "attn_bwd")
    dk, dv = band_fold(dkb, dvb, "band_fold")
    g["sinks"] = dsink[:, :w["sinks"].shape[0]]
    dgr, gt = rg_scan_bwd(s["proj"], dy_rnn, s["hs"], s["a"], "rg_scan_bwd")
    dxc, g["w_rg"], g["w_ig"], g["b_rg"], g["b_ig"], g["lru_lambda"] = rg_gates_bwd(
        gt, s["hs"], s["xc"], s["r"], s["ig"], w["w_rg"], w["w_ig"], w["lru_lambda"], "rg_gates_bwd")
    dxr, g["conv_w"], g["conv_b"] = rg_conv_bwd(s["proj"], dxc, w["conv_w"], "rg_conv_bwd")
    dproj = jnp.concatenate([dxr, dgr, dq, dk, dv, dg_rnn, dg_attn], axis=1)
    g["w_in"] = mm(dproj, s["h"], "tn", "mm_d_in", **wg)
    return _wmm(dproj, w["w_in"], "nn", "mm_dh", plus=(dz1, alpha)), g


def local_step(x, mem, target, depth, weights_of, grads_halfway, grads_done):
    alpha = _alpha(depth)
    tab = rope_table(x.shape[0])
    h, saved, layers = x, [], []
    for l in range(depth):
        wl, deps, late = weights_of(l, h)
        h, s, wl = layer_fwd(h, mem, wl, tab, alpha, deps, late)
        layers.append(wl)
        saved.append(s)
    dh, loss = loss_head(h, target, "loss_head")
    deps = ()
    for l in reversed(range(depth)):
        dh, g = layer_bwd(dh, mem, layers[l], saved[l], tab, alpha, deps, grads_halfway(l))
        deps = grads_done(l, g, dh)
    return loss, dh


def _pad_rows(flat):
    n = flat.shape[0]
    rows = -(-n // (LANES * SUBLANES)) * SUBLANES
    return jnp.pad(flat, (0, rows * LANES - n)).reshape(rows, LANES)


def kernel(x, mem, w_in, conv_w, conv_b, w_rg, b_rg, w_ig, b_ig, lru_lambda, w_br_rnn, w_br_attn, sinks, w_out, ln1_g, ln1_b, cq_w, ckv_w, co_w, ln2_g, ln2_b, ffn_wi, ffn_wo, ln3_g, ln3_b, loss_target, m_w_in, m_conv_w, m_conv_b, m_w_rg, m_b_rg, m_w_ig, m_b_ig, m_lru_lambda, m_w_br_rnn, m_w_br_attn, m_sinks, m_w_out, m_ln1_g, m_ln1_b, m_cq_w, m_ckv_w, m_co_w, m_ln2_g, m_ln2_b, m_ffn_wi, m_ffn_wo, m_ln3_g, m_ln3_b, v_w_in, v_conv_w, v_conv_b, v_w_rg, v_b_rg, v_w_ig, v_b_ig, v_lru_lambda, v_w_br_rnn, v_w_br_attn, v_sinks, v_w_out, v_ln1_g, v_ln1_b, v_cq_w, v_ckv_w, v_co_w, v_ln2_g, v_ln2_b, v_ffn_wi, v_ffn_wo, v_ln3_g, v_ln3_b):
    args = dict(locals())
    w = {n: args[n] for n in WEIGHTS}
    m = {n: args["m_" + n] for n in WEIGHTS}
    v = {n: args["v_" + n] for n in WEIGHTS}
    for group in (w, m, v):
        group["w_in"] = jnp.swapaxes(group["w_in"], 1, 2)
    cx, cy, cc = _place()
    chip = 2 * cx + cy
    L = w_in.shape[0]

    place = jnp.stack([cc, chip]).astype(jnp.int32)
    cw_rows = _pad_rows(conv_w.reshape(-1))
    cw_all = gather_devices(cw_rows, "gather_conv_w")[0::2]
    cw_parts = cw_all.reshape(N_CHIPS, -1)[:, :conv_w.size].reshape((N_CHIPS,) + conv_w.shape)
    conv_full = jnp.concatenate([cw_parts[k] for k in range(N_CHIPS)], axis=2)

    shards = [{n: w[n][l].astype(MXU_DTYPE) for n in BIG} for l in range(L)]
    late_names = tuple(n for n in BIG if n not in GATHER_FIRST)
    gathering = {(0, GATHER_FIRST): gather_start([shards[0][n] for n in GATHER_FIRST], cw_rows, "gather_start_0a")}
    gathering[0, late_names] = gather_start([shards[0][n] for n in late_names], gathering[0, GATHER_FIRST][4], "gather_start_0b")

    def gathered(l, names, after, tag):
        lands = gather_pass(gather_wait(gathering.pop((l, names)), after, f"gather_wait_{tag}"), f"gather_pass_{tag}")
        wl = {}
        for n, gw in zip(names, lands):
            rows_joined = gw.reshape(gw.shape[:-3] + (-1, gw.shape[-1]))
            if n in COL_BLOCKED:
                wl[n] = (gw, dict(b_blocks=N_CHIPS))
            elif n in GATE_WEIGHTS:
                wl[n] = rows_joined
            else:
                wl[n] = (rows_joined, {})
        return wl, lands

    def start_layer(l, after):
        if l >= L:
            return ()
        gathering[l, BIG] = gather_start([shards[l][n] for n in BIG], after, f"gather_start_{l}")
        return (gathering[l, BIG][4],)

    def weights_of(l, h):
        deps, late = (), None
        if l == 0:
            wl, _ = gathered(0, GATHER_FIRST, h, "0a")

            def late(after):
                rest, lands = gathered(0, late_names, after, "0b")
                return rest, start_layer(1, lands[0])
        else:
            wl, lands = gathered(l, BIG, h, str(l))
            deps = start_layer(l + 1, lands[0])
        for n in SMALL:
            wl[n] = conv_full[l] if n == "conv_w" else w[n][l] if n == "sinks" else w[n][l][None, :]
        return wl, deps, late

    def for_chips(n, g):
        if n in COL_BLOCKED:
            return g
        if n in GATE_WEIGHTS:
            nb, bw, _ = g.shape
            g = g.reshape(nb, N_CHIPS, bw // N_CHIPS, bw).transpose(1, 0, 2, 3).reshape(N_CHIPS, nb * bw // N_CHIPS, bw)
        else:
            g = g.reshape(N_CHIPS, g.shape[0] // N_CHIPS, g.shape[1])
        return g.astype(MXU_DTYPE)

    reduced, scattering, small_grads = {}, {}, [None] * L
    merged = {"cross": ("co_w", "cq_w"), "gates": GATE_WEIGHTS, "mix": ("w_br_rnn", "w_br_attn", "w_out")}
    first_grads = ("ffn_wo", "ffn_wi", "ckv_w", "cross")
    late_grads = ("w_in", "gates", "mix")

    def chip_parts(n, g):
        if n in merged:
            return jnp.concatenate([for_chips(k, g[k]) for k in merged[n]], axis=1)
        return for_chips(n, g[n])

    def grad_of(n):
        for name, members in merged.items():
            if n in members:
                rows = [int(np.prod(w[k].shape[1:-1])) for k in members]
                return reduced[name], sum(rows[:members.index(n)])
        return reduced[n], 0

    def start_scatter(l, names, g, after, tag):
        partial_sums = [chip_parts(n, g)[None] for n in names]
        from_sibling = swap_sibling(partial_sums, f"grad_to_sibling_{tag}")
        chip_sums = [add_pair(place, a, b, f"grad_add_pair_{n}_{l}") for n, a, b in zip(names, partial_sums, from_sibling)]
        scattering[l, names] = scatter_start(chip_sums, after, f"grad_scatter_start_{tag}")
        return (scattering[l, names][4],)

    def finish_layer(l, after):
        for names in [k[1] for k in list(scattering) if k[0] == l]:
            tag = f"{l}{'a' if names == first_grads else 'b'}"
            chip_sums, from_chips = scatter_wait(scattering.pop((l, names)), after, f"grad_scatter_wait_{tag}")
            for n, own, others in zip(names, chip_sums, from_chips):
                target = reduced.get(n, (L, 2 * own.shape[2], own.shape[3]))
                reduced[n] = add_chips(place, own, others, l, target, f"grad_add_chips_{n}_{l}")
        names = first_grads + late_grads
        reduced.update(zip(names, join_halves([reduced[n] for n in names], l, f"grad_join_{l}")))

    def grads_halfway(l):
        def halfway(g, after):
            return start_scatter(l, first_grads, g, after, f"{l}a")

        return halfway

    def grads_done(l, g, dh):
        small_grads[l] = {n: g[n] for n in SMALL}
        deps = start_scatter(l, late_grads, g, dh, f"{l}b")
        if 1 < l + 1 < L:
            finish_layer(l + 1, dh)
        return deps

    loss11, dx = local_step(x[0], mem[0], loss_target[0], L, weights_of, grads_halfway, grads_done)
    loss = lax.psum(loss11[0, 0], ("x", "y", "c"))

    first = min(2, L)
    updated = {}
    if first < L:
        for n in BIG:
            updated[n] = adamw(w[n], grad_of(n)[0], m[n], v[n], f"adamw_{n}_upper", layers=(first, L), g_row0=grad_of(n)[1])
    behind = (jnp.stack([updated[n][0][(0,) * w[n].ndim] for n in updated]),) if updated else ()
    small_full = {n: jnp.stack([gl[n] for gl in small_grads]).reshape(w[n].shape[:1] + ((CONV_WIDTH, -1) if n == "conv_w" else (-1,)))
                  for n in SMALL}
    small_flat = jnp.concatenate([small_full[n].reshape(-1) for n in SMALL])
    small_sum = sum_devices(gather_devices(_pad_rows(small_flat), "gather_small_grads", behind), "sum_small_grads").reshape(-1)
    delta, new_m, new_v, grad = {}, {}, {}, {}
    off = 0
    for n in SMALL:
        gfull = small_sum[off:off + small_full[n].size].reshape(small_full[n].shape)
        off += small_full[n].size
        if n == "conv_w":
            width = conv_w.shape[2]
            gfull = lax.dynamic_slice_in_dim(gfull, chip * width, width, axis=2)
        delta[n], new_m[n], new_v[n], grad[n] = adamw(w[n], gfull, m[n], v[n], "adamw_" + n)
    after = jnp.stack([delta[n][(0,) * delta[n].ndim] for n in SMALL])
    for l in reversed(range(first)):
        finish_layer(l, after)

    for n in BIG:
        some = dict(layers=(0, first), into=updated[n]) if updated else {}
        delta[n], new_m[n], new_v[n], grad[n] = adamw(w[n], grad_of(n)[0], m[n], v[n], "adamw_" + n, g_row0=grad_of(n)[1], **some)
    for group in (delta, new_m, new_v, grad):
        group["w_in"] = jnp.swapaxes(group["w_in"], 1, 2)
    return (loss, dx[None], *[grad[n] for n in WEIGHTS], *[delta[n] for n in WEIGHTS], *[new_m[n] for n in WEIGHTS],
            *[new_v[n] for n in WEIGHTS])
```
